```python
import math
import jax
import jax.numpy as jnp
from jax import lax
import numpy as np

D_MODEL = 1024
BATCH = 16
SEQ = 256
DEPTH = 2
DEC_BATCH = 2
DEC_SEQ = 4096
PAST_LEN = 256

GRID_W = 64
N_MIXERS = 2
N_LRU_LAYERS = (DEPTH + 1) // 2
N_ATTN_LAYERS = DEPTH // 2
LRU_WIDTH = D_MODEL
LRU_BLOCKS = 8
LRU_BLOCK_W = LRU_WIDTH // LRU_BLOCKS
RG_C = 8.0
CONV_W = 4
CONV_PAD_L = 2
HEAD_DIM = 128
N_HEADS = D_MODEL // HEAD_DIM
N_KV_HEADS = 2
GROUP = N_HEADS // N_KV_HEADS
QKV_DIM = (N_HEADS + 2 * N_KV_HEADS) * HEAD_DIM
AXIS_DIM = HEAD_DIM // 2
ROPE_BASE = 10000.0
Q_BLOCK = 128
N_EXPERTS = 16
EXPERT_FF = 2048
CAPACITY_FACTOR = 2
EPS = 1e-6

kernel_name = "hybrid_rglru_gqa_ec_diffusion_step"


def rmsnorm(x, gain):
    xf = x.astype(jnp.float32)
    y = xf * lax.rsqrt(jnp.mean(xf * xf, axis=-1, keepdims=True) + EPS)
    return (y * gain.astype(jnp.float32)).astype(x.dtype)


def adaln(cond, w_mod, b_mod):
    m = jax.nn.silu(cond) @ w_mod + b_mod
    return [t[:, None, :] for t in jnp.split(m, 6, axis=-1)]


def modulate(x, gain, shift, scale):
    return rmsnorm(x, gain) * (1 + scale) + shift


def centred_dwconv(x, w, b):
    T = x.shape[1]
    xp = jnp.pad(x, ((0, 0), (CONV_PAD_L, CONV_W - 1 - CONV_PAD_L), (0, 0)))
    y = b + xp[:, 0:T] * w[0]
    for k in range(1, CONV_W):
        y = y + xp[:, k:k + T] * w[k]
    return y


def linear_scan(a, b, h0, reverse):
    if reverse:
        b = b.at[:, -1].add(a[:, -1] * h0)
    else:
        b = b.at[:, 0].add(a[:, 0] * h0)

    def combine(l, r):
        a_l, b_l = l
        a_r, b_r = r
        return a_l * a_r, a_r * b_l + b_r

    _, h = lax.associative_scan(combine, (a, b), axis=1, reverse=reverse)
    return h


def recurrent_mixer(h, h0, w_in, conv_w, conv_b, wa, ba, wx, bx, lam, w_out):
    B, T, _ = h.shape
    u = h @ w_in
    gate, xr = jnp.split(u, 2, axis=-1)
    xr = centred_dwconv(xr, conv_w, conv_b).astype(jnp.float32)
    xb = xr.reshape(B, T, LRU_BLOCKS, LRU_BLOCK_W)
    total = None
    finals = []
    for d, rev in ((0, False), (1, True)):
        r = jax.nn.sigmoid(jnp.einsum('btnd,nde->btne', xb, wa[d].astype(jnp.float32)).reshape(B, T, LRU_WIDTH) + ba[d])
        i = jax.nn.sigmoid(jnp.einsum('btnd,nde->btne', xb, wx[d].astype(jnp.float32)).reshape(B, T, LRU_WIDTH) + bx[d])
        log_a = -RG_C * r * jax.nn.softplus(-lam[d].astype(jnp.float32))
        a = jnp.exp(log_a)
        bcoef = jnp.sqrt(-jnp.expm1(2.0 * log_a)) * (i * xr)
        hs = linear_scan(a, bcoef, h0[:, d].astype(jnp.float32), rev)
        finals.append(hs[:, 0] if rev else hs[:, -1])
        total = hs if total is None else total + hs
    y = (total * jax.nn.gelu(gate.astype(jnp.float32))).astype(h.dtype)
    return y @ w_out, jnp.stack(finals, axis=1)


def gqa_project(h, w_qkv, q_gain, k_gain):
    B, T, _ = h.shape
    qkv = h @ w_qkv
    q = qkv[..., :N_HEADS * HEAD_DIM].reshape(B, T, N_HEADS, HEAD_DIM)
    k = qkv[..., N_HEADS * HEAD_DIM:(N_HEADS + N_KV_HEADS) * HEAD_DIM].reshape(B, T, N_KV_HEADS, HEAD_DIM)
    v = qkv[..., (N_HEADS + N_KV_HEADS) * HEAD_DIM:].reshape(B, T, N_KV_HEADS, HEAD_DIM)
    return rmsnorm(q, q_gain), rmsnorm(k, k_gain), v


def axial_rope(x):
    T = x.shape[1]
    n_rows = T // GRID_W
    rows = jnp.repeat(jnp.arange(n_rows), GRID_W)
    cols = jnp.tile(jnp.arange(GRID_W), n_rows)
    inv = ROPE_BASE ** (-jnp.arange(AXIS_DIM // 2, dtype=jnp.float32) * 2.0 / AXIS_DIM)

    def rot(xa, pos):
        ang = pos.astype(jnp.float32)[:, None] * inv
        cos = jnp.cos(ang)[None, :, None, :]
        sin = jnp.sin(ang)[None, :, None, :]
        x1, x2 = jnp.split(xa, 2, axis=-1)
        return jnp.concatenate([x1 * cos - x2 * sin, x2 * cos + x1 * sin], axis=-1)

    xf = x.astype(jnp.float32)
    return jnp.concatenate([rot(xf[..., :AXIS_DIM], rows), rot(xf[..., AXIS_DIM:], cols)], axis=-1).astype(x.dtype)


def block_attention(q, k, v):
    B, T, _, _ = q.shape
    nb = T // Q_BLOCK
    qb = q.reshape(B, nb, Q_BLOCK, N_KV_HEADS, GROUP, HEAD_DIM).transpose(1, 0, 2, 3, 4, 5)
    scale = HEAD_DIM ** -0.5

    def one_block(qblk):
        s = jnp.einsum('bqkgd,bskd->bkgqs', qblk, k, preferred_element_type=jnp.float32) * scale
        p = jax.nn.softmax(s, axis=-1).astype(v.dtype)
        return jnp.einsum('bkgqs,bskd->bqkgd', p, v)

    o = lax.map(one_block, qb)
    return o.transpose(1, 0, 2, 3, 4, 5).reshape(B, T, N_HEADS * HEAD_DIM)


def expert_choice_ffn(h, w_router, w_gate, w_up, w_down):
    B, T, D = h.shape
    cap = CAPACITY_FACTOR * T // N_EXPERTS
    logits = jnp.einsum('btd,de->bte', h, w_router, preferred_element_type=jnp.float32)
    aff = jax.nn.softmax(logits, axis=-1)
    g, idx = lax.top_k(aff.transpose(0, 2, 1), cap)
    xs = jax.vmap(lambda hb, ib: hb[ib])(h, idx)
    hid = jax.nn.silu(jnp.einsum('becd,edf->becf', xs, w_gate)) * jnp.einsum('becd,edf->becf', xs, w_up)
    ye = jnp.einsum('becf,efd->becd', hid, w_down) * g[..., None].astype(h.dtype)
    return jax.vmap(lambda ib, yb: jnp.zeros((T, D), yb.dtype).at[ib.reshape(-1)].add(yb.reshape(-1, D)))(idx, ye)


def setup_inputs(seed: int = 0) -> dict:
    key = jax.random.key(seed)
    ks = jax.random.split(key, 32)

    def nrm(k, shape, scale):
        return jax.random.normal(k, shape, jnp.float32) * scale

    a0 = jax.random.uniform(ks[17], (N_LRU_LAYERS, 2, LRU_WIDTH), jnp.float32, minval=0.9, maxval=0.999)
    return {
        "x_prompt": nrm(ks[0], (BATCH, SEQ, D_MODEL), 1.0),
        "x_sample": nrm(ks[1], (DEC_BATCH, DEC_SEQ, D_MODEL), 1.0),
        "state_lru": nrm(ks[2], (DEC_BATCH, N_LRU_LAYERS, 2, LRU_WIDTH), 0.5),
        "cache_k": nrm(ks[3], (DEC_BATCH, N_ATTN_LAYERS, PAST_LEN, N_KV_HEADS, HEAD_DIM), 1.0),
        "cache_v": nrm(ks[4], (DEC_BATCH, N_ATTN_LAYERS, PAST_LEN, N_KV_HEADS, HEAD_DIM), 1.0),
        "c": nrm(ks[5], (DEC_BATCH, D_MODEL), 1.0),
        "c_ctx": nrm(ks[6], (D_MODEL,), 1.0),
        "w_mod": nrm(ks[7], (DEPTH, D_MODEL, 6 * D_MODEL), 0.5 * D_MODEL ** -0.5),
        "b_mod": nrm(ks[8], (DEPTH, 6 * D_MODEL), 0.02),
        "norm_gain": 1.0 + nrm(ks[9], (DEPTH, 2, D_MODEL), 0.02),
        "final_gain": 1.0 + nrm(ks[10], (D_MODEL,), 0.02),
        "w_lru_in": nrm(ks[11], (N_LRU_LAYERS, D_MODEL, 2 * LRU_WIDTH), D_MODEL ** -0.5),
        "lru_conv_w": nrm(ks[12], (N_LRU_LAYERS, CONV_W, LRU_WIDTH), CONV_W ** -0.5),
        "lru_conv_b": nrm(ks[13], (N_LRU_LAYERS, LRU_WIDTH), 0.02),
        "lru_wa": nrm(ks[14], (N_LRU_LAYERS, 2, LRU_BLOCKS, LRU_BLOCK_W, LRU_BLOCK_W), LRU_BLOCK_W ** -0.5),
        "lru_ba": nrm(ks[15], (N_LRU_LAYERS, 2, LRU_WIDTH), 0.02),
        "lru_wx": nrm(ks[16], (N_LRU_LAYERS, 2, LRU_BLOCKS, LRU_BLOCK_W, LRU_BLOCK_W), LRU_BLOCK_W ** -0.5),
        "lru_bx": nrm(ks[18], (N_LRU_LAYERS, 2, LRU_WIDTH), 0.02),
        "lru_lambda": jnp.log(a0) - jnp.log1p(-a0),
        "w_lru_out": nrm(ks[19], (N_LRU_LAYERS, LRU_WIDTH, D_MODEL), LRU_WIDTH ** -0.5),
        "w_qkv": nrm(ks[20], (N_ATTN_LAYERS, D_MODEL, QKV_DIM), D_MODEL ** -0.5),
        "q_norm": 1.0 + nrm(ks[21], (N_ATTN_LAYERS, HEAD_DIM), 0.02),
        "k_norm": 1.0 + nrm(ks[22], (N_ATTN_LAYERS, HEAD_DIM), 0.02),
        "w_attn_out": nrm(ks[23], (N_ATTN_LAYERS, N_HEADS * HEAD_DIM, D_MODEL), (N_HEADS * HEAD_DIM) ** -0.5),
        "w_router": nrm(ks[24], (DEPTH, D_MODEL, N_EXPERTS), D_MODEL ** -0.5),
        "w_exp_gate": nrm(ks[25], (DEPTH, N_EXPERTS, D_MODEL, EXPERT_FF), D_MODEL ** -0.5),
        "w_exp_up": nrm(ks[26], (DEPTH, N_EXPERTS, D_MODEL, EXPERT_FF), D_MODEL ** -0.5),
        "w_exp_down": nrm(ks[27], (DEPTH, N_EXPERTS, EXPERT_FF, D_MODEL), EXPERT_FF ** -0.5),
    }


def reference(x_prompt, x_sample, state_lru, cache_k, cache_v, c, c_ctx, w_mod, b_mod, norm_gain, final_gain,
              w_lru_in, lru_conv_w, lru_conv_b, lru_wa, lru_ba, lru_wx, lru_bx, lru_lambda, w_lru_out,
              w_qkv, q_norm, k_norm, w_attn_out, w_router, w_exp_gate, w_exp_up, w_exp_down):
    xp, xs = x_prompt, x_sample
    new_lru, new_k, new_v = [], [], []
    for l in range(DEPTH):
        m_ctx = adaln(c_ctx[None, :], w_mod[l], b_mod[l])
        m_lat = adaln(c, w_mod[l], b_mod[l])
        hp = modulate(xp, norm_gain[l, 0], m_ctx[0], m_ctx[1])
        hs = modulate(xs, norm_gain[l, 0], m_lat[0], m_lat[1])
        if l % N_MIXERS == 0:
            li = l // N_MIXERS
            lw = (w_lru_in[li], lru_conv_w[li], lru_conv_b[li], lru_wa[li], lru_ba[li],
                  lru_wx[li], lru_bx[li], lru_lambda[li], w_lru_out[li])
            h0_ctx = jnp.zeros((xp.shape[0], 2, LRU_WIDTH), jnp.float32)
            op, fin_ctx = recurrent_mixer(hp, h0_ctx, *lw)
            new_lru.append(fin_ctx.astype(xp.dtype))
            os_, _ = recurrent_mixer(hs, state_lru[:, li], *lw)
        else:
            ai = l // N_MIXERS
            qp, kp, vp = gqa_project(hp, w_qkv[ai], q_norm[ai], k_norm[ai])
            op = block_attention(qp, kp, vp) @ w_attn_out[ai]
            new_k.append(kp)
            new_v.append(vp)
            qs, kls, vls = gqa_project(hs, w_qkv[ai], q_norm[ai], k_norm[ai])
            qs = axial_rope(qs)
            kls = axial_rope(kls)
            k_all = jnp.concatenate([kls, cache_k[:, ai].astype(kls.dtype)], axis=1)
            v_all = jnp.concatenate([vls, cache_v[:, ai].astype(vls.dtype)], axis=1)
            os_ = block_attention(qs, k_all, v_all) @ w_attn_out[ai]
        xp = xp + m_ctx[2] * op
        xs = xs + m_lat[2] * os_
        hp = modulate(xp, norm_gain[l, 1], m_ctx[3], m_ctx[4])
        hs = modulate(xs, norm_gain[l, 1], m_lat[3], m_lat[4])
        xp = xp + m_ctx[5] * expert_choice_ffn(hp, w_router[l], w_exp_gate[l], w_exp_up[l], w_exp_down[l])
        xs = xs + m_lat[5] * expert_choice_ffn(hs, w_router[l], w_exp_gate[l], w_exp_up[l], w_exp_down[l])
    y_prompt = rmsnorm(xp, final_gain)
    y_sample = rmsnorm(xs, final_gain)
    new_state_lru = jnp.stack(new_lru, axis=1)
    new_cache_k = jnp.stack(new_k, axis=1)
    new_cache_v = jnp.stack(new_v, axis=1)
    return (y_prompt, y_sample, new_state_lru, new_cache_k, new_cache_v)
```

```python
import functools
import math

import jax
import jax.numpy as jnp
from jax import lax
from jax.experimental import pallas as pl
from jax.experimental.pallas import tpu as pltpu

F32 = jnp.float32
BF16 = jnp.bfloat16
I32 = jnp.int32

LANES = 128
SUBLANES = 8
MIB = 1024 * 1024

RG_C = 8.0
CONV_W = 4
CONV_PAD_L = 2
ROPE_BASE = 10000.0
GRID_W = 64
EPS = 1e-6
N_EXPERTS = 16
CAPACITY_FACTOR = 2
HEAD_DIM = 128
LRU_BLOCKS = 8

ROW_BLOCK = 512


def _params(n_axes, vmem_mib):
    return pltpu.CompilerParams(
        dimension_semantics=("arbitrary",) * n_axes, vmem_limit_bytes=vmem_mib * MIB)


def _split_bf16(x):
    hi = x.astype(BF16)
    lo = (x - hi.astype(F32)).astype(BF16)
    return hi, lo


def _modulated_norm(x, gain, shift, scale):
    y = x * lax.rsqrt(jnp.mean(x * x, axis=-1, keepdims=True) + EPS)
    return (y * gain) * (1.0 + scale) + shift


def _mod_kernel(cond_ref, w_ref, b_ref, o_ref):
    c = cond_ref[...]
    a_hi, a_lo = _split_bf16(c * jax.nn.sigmoid(c))
    w_hi, w_lo = _split_bf16(w_ref[0])
    dot = functools.partial(jnp.dot, preferred_element_type=F32)
    o_ref[0] = dot(a_hi, w_hi) + (dot(a_lo, w_hi) + dot(a_hi, w_lo)) + b_ref[0]


def _mod_vectors(cond8, w_mod, b_mod):
    depth, d, d6 = w_mod.shape
    nb = 1536
    return pl.pallas_call(
        _mod_kernel,
        grid=(depth, d6 // nb),
        in_specs=[
            pl.BlockSpec((SUBLANES, d), lambda l, j: (0, 0)),
            pl.BlockSpec((1, d, nb), lambda l, j: (l, 0, j)),
            pl.BlockSpec((1, 1, nb), lambda l, j: (l, 0, j)),
        ],
        out_specs=pl.BlockSpec((1, SUBLANES, nb), lambda l, j: (l, 0, j)),
        out_shape=jax.ShapeDtypeStruct((depth, SUBLANES, d6), F32),
        compiler_params=_params(2, 40),
        name="adaln_mod",
    )(cond8, w_mod, b_mod.reshape(depth, 1, d6))


def _lru_in_kernel(x_ref, mod_ref, gain_ref, w_ref, gate_ref, xr_ref, *, mod_row):
    d = x_ref.shape[1]
    m = mod_row(pl.program_id(0))
    shift = mod_ref[pl.ds(m, 1), pl.ds(0, d)]
    scale = mod_ref[pl.ds(m, 1), pl.ds(d, d)]
    h = _modulated_norm(x_ref[...], gain_ref[...], shift, scale).astype(BF16)
    u = jnp.dot(h, w_ref[...], preferred_element_type=F32)
    w = gate_ref.shape[1]
    gate_ref[...] = u[:, :w]
    xr_ref[...] = u[:, w:]


def _lru_in(x, mod, gain, w_in, mod_row):
    n, d = x.shape
    w2 = w_in.shape[1]
    w = w2 // 2
    row = lambda i: (i, 0)
    fixed = lambda i: (0, 0)
    return pl.pallas_call(
        functools.partial(_lru_in_kernel, mod_row=mod_row),
        grid=(n // ROW_BLOCK,),
        in_specs=[
            pl.BlockSpec((ROW_BLOCK, d), row),
            pl.BlockSpec(mod.shape, fixed),
            pl.BlockSpec((1, d), fixed),
            pl.BlockSpec((d, w2), fixed),
        ],
        out_specs=[pl.BlockSpec((ROW_BLOCK, w), row), pl.BlockSpec((ROW_BLOCK, w), row)],
        out_shape=[jax.ShapeDtypeStruct((n, w), F32), jax.ShapeDtypeStruct((n, w), F32)],
        compiler_params=_params(1, 48),
        name="lru_in",
    )(x, mod, gain, w_in)


def _scan8(a, b, row, reverse):
    for s in (1, 2, 4):
        if reverse:
            a_s = pltpu.roll(a, SUBLANES - s, 0)
            b_s = pltpu.roll(b, SUBLANES - s, 0)
            keep = row < SUBLANES - s
        else:
            a_s = pltpu.roll(a, s, 0)
            b_s = pltpu.roll(b, s, 0)
            keep = row >= s
        a_s = jnp.where(keep, a_s, 1.0)
        b_s = jnp.where(keep, b_s, 0.0)
        b = a * b_s + b
        a = a * a_s
    return a, b


def _lru_core_kernel(xr_ref, gate_ref, cw_ref, cb_ref, wg_ref, bias_ref, lam_ref, h0_ref,
                     y_ref, fin_ref, pad, a_f, b_f, a_b, b_b, *, seq_len):
    t_len = seq_len
    lb = xr_ref.shape[1]
    ch = 256
    sup = 64
    halo = SUBLANES

    zero_rows = jnp.zeros((halo, lb), F32)
    pad[0:halo, :] = zero_rows
    pad[t_len + halo:t_len + 2 * halo, :] = zero_rows

    def copy_in(c, carry):
        r = pl.multiple_of(c * ch, ch)
        pad[pl.ds(r + halo, ch), :] = xr_ref[pl.ds(r, ch), :]
        return carry

    lax.fori_loop(0, t_len // ch, copy_in, 0)

    neg_lam = -lam_ref[...]
    softplus = jnp.maximum(neg_lam, 0.0) + jnp.log1p(jnp.exp(-jnp.abs(neg_lam)))
    cw = cw_ref[...]
    cbias = cb_ref[...]
    bias = bias_ref[...]
    wg = wg_ref[0]

    def gates(c, carry):
        r = pl.multiple_of(c * ch, ch)
        blk = pad[pl.ds(r, ch + 2 * halo), :]
        xc = cbias
        for k in range(CONV_W):
            o = halo - CONV_PAD_L + k
            xc = xc + blk[o:o + ch] * cw[k:k + 1]
        g = jnp.dot(xc.astype(BF16), wg, preferred_element_type=F32)
        for d, (a_ref, b_ref) in enumerate(((a_f, b_f), (a_b, b_b))):
            r_gate = jax.nn.sigmoid(g[:, (2 * d) * lb:(2 * d + 1) * lb] + bias[2 * d:2 * d + 1])
            i_gate = jax.nn.sigmoid(g[:, (2 * d + 1) * lb:(2 * d + 2) * lb] + bias[2 * d + 1:2 * d + 2])
            log_a = (-RG_C * r_gate) * softplus[d:d + 1]
            a = jnp.exp(log_a)
            b = jnp.sqrt(jnp.tanh(-log_a) * (1.0 + a * a)) * (i_gate * xc)
            a_ref[pl.ds(r, ch), :] = a
            b_ref[pl.ds(r, ch), :] = b
        return carry

    lax.fori_loop(0, t_len // ch, gates, 0)

    row = lax.broadcasted_iota(I32, (SUBLANES, lb), 0)
    n_sup = t_len // sup
    per = sup // SUBLANES

    def bwd(ci, h):
        r = pl.multiple_of((n_sup - 1 - ci) * sup, sup)
        for j in reversed(range(per)):
            rows = pl.ds(r + SUBLANES * j, SUBLANES)
            a_c, b_c = _scan8(a_b[rows, :], b_b[rows, :], row, True)
            hs = a_c * h + b_c
            b_b[rows, :] = hs
            h = jnp.broadcast_to(hs[0:1], (SUBLANES, lb))
        return h

    h_b = lax.fori_loop(0, n_sup, bwd, jnp.broadcast_to(h0_ref[0, 1:2, :], (SUBLANES, lb)))

    def fwd(ci, h):
        r = pl.multiple_of(ci * sup, sup)
        for j in range(per):
            rows = pl.ds(r + SUBLANES * j, SUBLANES)
            a_c, b_c = _scan8(a_f[rows, :], b_f[rows, :], row, False)
            hs = a_c * h + b_c
            b_f[rows, :] = hs + b_b[rows, :]
            h = jnp.broadcast_to(hs[SUBLANES - 1:SUBLANES], (SUBLANES, lb))
        return h

    h_f = lax.fori_loop(0, n_sup, fwd, jnp.broadcast_to(h0_ref[0, 0:1, :], (SUBLANES, lb)))

    fin_ref[0, 0:1, :] = h_f[0:1]
    fin_ref[0, 1:2, :] = h_b[0:1]

    def emit(c, carry):
        r = pl.multiple_of(c * ch, ch)
        gt = gate_ref[pl.ds(r, ch), :]
        cdf = 0.5 * (1.0 + jnp.tanh(math.sqrt(2.0 / math.pi) * (gt + 0.044715 * (gt * gt * gt))))
        y_ref[pl.ds(r, ch), :] = (b_f[pl.ds(r, ch), :] * (gt * cdf)).astype(BF16)
        return carry

    lax.fori_loop(0, t_len // ch, emit, 0)


def _lru_core(xr, gate, conv_w, conv_b, wg, bias4, lam, h0, y_prev, *, seq_len, row_off):
    n, w = xr.shape
    n_seq = h0.shape[0]
    lb = w // LRU_BLOCKS
    blk_off = row_off // seq_len
    tok = lambda s, c: (blk_off + s, c)
    chan = lambda s, c: (0, c)
    in_specs = [
        pl.BlockSpec((seq_len, lb), tok),
        pl.BlockSpec((seq_len, lb), tok),
        pl.BlockSpec((CONV_W, lb), chan),
        pl.BlockSpec((1, lb), chan),
        pl.BlockSpec((1, lb, 4 * lb), lambda s, c: (c, 0, 0)),
        pl.BlockSpec((4, lb), chan),
        pl.BlockSpec((2, lb), chan),
        pl.BlockSpec((1, 2, lb), lambda s, c: (s, 0, c)),
    ]
    args = [xr, gate, conv_w, conv_b, wg, bias4, lam, h0]
    aliases = {}
    if y_prev is not None:
        in_specs.append(pl.BlockSpec(memory_space=pl.ANY))
        args.append(y_prev)
        aliases = {len(args) - 1: 0}
    kern = functools.partial(_lru_core_kernel, seq_len=seq_len)
    if y_prev is not None:
        kern = lambda *refs, _k=kern: _k(*refs[:8], *refs[9:])
    return pl.pallas_call(
        kern,
        grid=(n_seq, LRU_BLOCKS),
        in_specs=in_specs,
        out_specs=[pl.BlockSpec((seq_len, lb), tok),
                   pl.BlockSpec((1, 2, lb), lambda s, c: (s, 0, c))],
        out_shape=[jax.ShapeDtypeStruct((n, w), BF16), jax.ShapeDtypeStruct((n_seq, 2, w), F32)],
        scratch_shapes=[pltpu.VMEM((seq_len + 2 * SUBLANES, lb), F32)]
        + [pltpu.VMEM((seq_len, lb), F32)] * 4,
        input_output_aliases=aliases,
        compiler_params=_params(2, 40),
        name=f"lru_core_{seq_len}",
    )(*args)


def _mix_out_kernel(a_ref, x_ref, mod_ref, gain_ref, w_ref, wrh_ref, wrl_ref,
                    x1_ref, h2_ref, lg_ref, *, mod_row):
    d = x_ref.shape[1]
    m = mod_row(pl.program_id(0))
    g_mix = mod_ref[pl.ds(m, 1), pl.ds(2 * d, d)]
    shift = mod_ref[pl.ds(m, 1), pl.ds(3 * d, d)]
    scale = mod_ref[pl.ds(m, 1), pl.ds(4 * d, d)]
    op = jnp.dot(a_ref[...], w_ref[...], preferred_element_type=F32)
    x1 = x_ref[...] + g_mix * op
    x1_ref[...] = x1
    h2 = _modulated_norm(x1, gain_ref[...], shift, scale)
    h2_ref[...] = h2
    h_hi, h_lo = _split_bf16(h2)
    nt = functools.partial(lax.dot_general, dimension_numbers=(((1,), (1,)), ((), ())),
                           preferred_element_type=F32)
    lg_ref[...] = nt(wrh_ref[...], h_hi) + (nt(wrl_ref[...], h_hi) + nt(wrh_ref[...], h_lo))


def _mix_out(a, x, mod, gain, w_out, wr_hi, wr_lo, mod_row):
    n, d = x.shape
    e = wr_hi.shape[0]
    row = lambda i: (i, 0)
    fixed = lambda i: (0, 0)
    return pl.pallas_call(
        functools.partial(_mix_out_kernel, mod_row=mod_row),
        grid=(n // ROW_BLOCK,),
        in_specs=[
            pl.BlockSpec((ROW_BLOCK, a.shape[1]), row),
            pl.BlockSpec((ROW_BLOCK, d), row),
            pl.BlockSpec(mod.shape, fixed),
            pl.BlockSpec((1, d), fixed),
            pl.BlockSpec(w_out.shape, fixed),
            pl.BlockSpec(wr_hi.shape, fixed),
            pl.BlockSpec(wr_lo.shape, fixed),
        ],
        out_specs=[pl.BlockSpec((ROW_BLOCK, d), row), pl.BlockSpec((ROW_BLOCK, d), row),
                   pl.BlockSpec((e, ROW_BLOCK), lambda i: (0, i))],
        out_shape=[jax.ShapeDtypeStruct((n, d), F32), jax.ShapeDtypeStruct((n, d), F32),
                   jax.ShapeDtypeStruct((e, n), F32)],
        compiler_params=_params(1, 48),
        name="mix_out",
    )(a, x, mod, gain, w_out, wr_hi, wr_lo)


def _lane_sum(tiles):
    acc = tiles[0]
    for t in tiles[1:]:
        acc = acc + t
    return jnp.sum(acc, axis=1, keepdims=True)


def _exclusive_cumsum(flags, upper):
    out = []
    off = jnp.zeros((flags[0].shape[0], 1), F32)
    for f in flags:
        out.append(jnp.dot(f.astype(BF16), upper, preferred_element_type=F32) + off)
        off = off + jnp.sum(f, axis=1, keepdims=True)
    return out


def _route_group(pieces, cap, idx_ref, g_ref):
    affs = []
    for lg in pieces:
        ex = jnp.exp(lg - jnp.max(lg, axis=0, keepdims=True))
        affs.append(ex / jnp.sum(ex, axis=0, keepdims=True))
    aff = jnp.concatenate(affs, axis=0)
    n_rows, t_len = aff.shape
    nt = t_len // LANES
    g = [aff[:, j * LANES:(j + 1) * LANES] for j in range(nt)]

    kth_bits = jnp.zeros((n_rows, 1), I32)
    for bit in range(30, -1, -1):
        cand = kth_bits | (1 << bit)
        cand_f = pltpu.bitcast(cand, F32)
        cnt = _lane_sum([jnp.where(t >= cand_f, 1, 0) for t in g])
        kth_bits = jnp.where(cnt >= cap, cand, kth_bits)
    kth = pltpu.bitcast(kth_bits, F32)

    lane = lax.broadcasted_iota(I32, (LANES, LANES), 0)
    upper = jnp.where(lane < lax.broadcasted_iota(I32, (LANES, LANES), 1), 1.0, 0.0).astype(BF16)
    gt = [t > kth for t in g]
    eq = [t == kth for t in g]
    need = (cap - _lane_sum([jnp.where(m, 1, 0) for m in gt])).astype(F32)
    eq_rank = _exclusive_cumsum([jnp.where(m, 1.0, 0.0) for m in eq], upper)
    sel = [jnp.logical_or(gt[j], jnp.logical_and(eq[j], eq_rank[j] < need)) for j in range(nt)]
    pos = _exclusive_cumsum([jnp.where(m, 1.0, 0.0) for m in sel], upper)

    lane_r = lax.broadcasted_iota(I32, (n_rows, LANES), 1)
    d = [jnp.where(sel[j], lane_r + j * LANES - pos[j].astype(I32), -1) for j in range(nt)]
    for k in range(t_len.bit_length() - 1):
        s = 1 << k
        if s < LANES:
            d_rot = [pltpu.roll(x, LANES - s, 1) for x in d]
            g_rot = [pltpu.roll(x, LANES - s, 1) for x in g]
            same = lane_r < LANES - s
            d_in = [jnp.where(same, d_rot[j], d_rot[(j + 1) % nt]) for j in range(nt)]
            g_in = [jnp.where(same, g_rot[j], g_rot[(j + 1) % nt]) for j in range(nt)]
        else:
            q = s // LANES
            d_in = [d[(j + q) % nt] for j in range(nt)]
            g_in = [g[(j + q) % nt] for j in range(nt)]
        new_d, new_g = [], []
        for j in range(nt):
            move = jnp.logical_and(d_in[j] >= 0, ((d_in[j] >> k) & 1) == 1)
            stay = jnp.logical_and(d[j] >= 0, ((d[j] >> k) & 1) == 0)
            new_d.append(jnp.where(move, d_in[j], jnp.where(stay, d[j], -1)))
            new_g.append(jnp.where(move, g_in[j], g[j]))
        d, g = new_d, new_g

    for j in range(idx_ref.shape[1] // LANES):
        idx_ref[:, j * LANES:(j + 1) * LANES] = lane_r + j * LANES + d[j]
        g_ref[:, j * LANES:(j + 1) * LANES] = g[j]


def _route_kernel(lg_ref, idx_p_ref, g_p_ref, idx_s_ref, g_s_ref, *, np_seq, p_len, ns_seq, s_len):
    off = np_seq * p_len
    prompt = [lg_ref[:, s * p_len:(s + 1) * p_len] for s in range(np_seq)]
    sample = [lg_ref[:, off + s * s_len:off + (s + 1) * s_len] for s in range(ns_seq)]
    _route_group(prompt, CAPACITY_FACTOR * p_len // N_EXPERTS, idx_p_ref, g_p_ref)
    _route_group(sample, CAPACITY_FACTOR * s_len // N_EXPERTS, idx_s_ref, g_s_ref)


def _route(lg_t, np_seq, p_len, ns_seq, s_len):
    e = lg_t.shape[0]
    cap_p = CAPACITY_FACTOR * p_len // N_EXPERTS
    cap_s = CAPACITY_FACTOR * s_len // N_EXPERTS
    wp = max(cap_p, LANES)
    ws = max(cap_s, LANES)
    idx_p, g_p, idx_s, g_s = pl.pallas_call(
        functools.partial(_route_kernel, np_seq=np_seq, p_len=p_len, ns_seq=ns_seq, s_len=s_len),
        out_shape=[jax.ShapeDtypeStruct((np_seq * e, wp), I32),
                   jax.ShapeDtypeStruct((np_seq * e, wp), F32),
                   jax.ShapeDtypeStruct((ns_seq * e, ws), I32),
                   jax.ShapeDtypeStruct((ns_seq * e, ws), F32)],
        compiler_params=pltpu.CompilerParams(vmem_limit_bytes=40 * MIB),
        name="route",
    )(lg_t)
    return (idx_p[:, :cap_p].reshape(-1), g_p[:, :cap_p].reshape(-1),
            idx_s[:, :cap_s].reshape(-1), g_s[:, :cap_s].reshape(-1))


def _gather_kernel(idx_ref, h_ref, o_ref, rows, *, ep, cap, n_exp):
    s = pl.program_id(0)
    eb = pl.program_id(1)
    for el in range(ep):
        base = (s * n_exp + eb * ep + el) * cap

        def body(c, carry, base=base):
            t = idx_ref[base + c]
            rows[pl.ds(c, 1), :] = h_ref[pl.ds(t, 1), :]
            return carry

        lax.fori_loop(0, cap, body, 0, unroll=8)
        o_ref[el] = rows[...].astype(BF16)


def _gather(idx, h, *, seq_len, row_off, n_seq, ep):
    n, d = h.shape
    cap = CAPACITY_FACTOR * seq_len // N_EXPERTS
    blk_off = row_off // seq_len
    return pl.pallas_call(
        functools.partial(_gather_kernel, ep=ep, cap=cap, n_exp=N_EXPERTS),
        grid_spec=pltpu.PrefetchScalarGridSpec(
            num_scalar_prefetch=1,
            grid=(n_seq, N_EXPERTS // ep),
            in_specs=[pl.BlockSpec((seq_len, d), lambda s, eb, idx: (blk_off + s, 0))],
            out_specs=pl.BlockSpec((ep, cap, d), lambda s, eb, idx: (eb, s, 0)),
            scratch_shapes=[pltpu.VMEM((cap, d), F32)],
        ),
        out_shape=jax.ShapeDtypeStruct((N_EXPERTS, n_seq * cap, d), BF16),
        compiler_params=_params(2, 48),
        name=f"gather_{seq_len}",
    )(idx, h)


def _ffn_kernel(xp_ref, xs_ref, wg_ref, wu_ref, wd_ref, yp_ref, ys_ref):
    f = pl.program_id(1)
    wg = wg_ref[0, 0].astype(BF16)
    wu = wu_ref[0, 0].astype(BF16)
    wd = wd_ref[0, 0].astype(BF16)
    rc = 512
    for x_ref, y_ref in ((xp_ref, yp_ref), (xs_ref, ys_ref)):
        for r in range(0, x_ref.shape[1], rc):
            x = x_ref[0, r:r + rc, :]
            hg = jnp.dot(x, wg, preferred_element_type=F32)
            hu = jnp.dot(x, wu, preferred_element_type=F32)
            hid = ((hg * jax.nn.sigmoid(hg)) * hu).astype(BF16)
            part = jnp.dot(hid, wd, preferred_element_type=F32)

            @pl.when(f == 0)
            def _():
                y_ref[0, r:r + rc, :] = part

            @pl.when(f != 0)
            def _():
                y_ref[0, r:r + rc, :] += part


def _ffn(xs_p, xs_s, w_gate, w_up, w_down, layer):
    n_exp, rp, d = xs_p.shape
    rs = xs_s.shape[1]
    ff = w_gate.shape[3]
    fc = 512
    return pl.pallas_call(
        _ffn_kernel,
        grid=(n_exp, ff // fc),
        in_specs=[
            pl.BlockSpec((1, rp, d), lambda e, f: (e, 0, 0)),
            pl.BlockSpec((1, rs, d), lambda e, f: (e, 0, 0)),
            pl.BlockSpec((1, 1, d, fc), lambda e, f: (layer, e, 0, f)),
            pl.BlockSpec((1, 1, d, fc), lambda e, f: (layer, e, 0, f)),
            pl.BlockSpec((1, 1, fc, d), lambda e, f: (layer, e, f, 0)),
        ],
        out_specs=[pl.BlockSpec((1, rp, d), lambda e, f: (e, 0, 0)),
                   pl.BlockSpec((1, rs, d), lambda e, f: (e, 0, 0))],
        out_shape=[jax.ShapeDtypeStruct((n_exp, rp, d), F32),
                   jax.ShapeDtypeStruct((n_exp, rs, d), F32)],
        compiler_params=_params(2, 56),
        name="expert_ffn",
    )(xs_p, xs_s, w_gate, w_up, w_down)


def _combine_kernel(idx_ref, g_ref, ye_ref, x_ref, gate_ref, o_ref, *, ep, cap, n_exp, mod_row):
    s = pl.program_id(0)
    eb = pl.program_id(2)
    n_eb = pl.num_programs(2)
    t_len = o_ref.shape[0]
    ch = 256

    @pl.when(eb == 0)
    def _():
        def zero(c, carry):
            o_ref[pl.ds(pl.multiple_of(c * ch, ch), ch), :] = jnp.zeros((ch, o_ref.shape[1]), F32)
            return carry
        lax.fori_loop(0, t_len // ch, zero, 0)

    for el in range(ep):
        base = (s * n_exp + eb * ep + el) * cap

        def body(c, carry, base=base, el=el):
            t = idx_ref[base + c]
            o_ref[pl.ds(t, 1), :] += g_ref[base + c] * ye_ref[el, pl.ds(c, 1), :]
            return carry

        lax.fori_loop(0, cap, body, 0, unroll=4)

    @pl.when(eb == n_eb - 1)
    def _():
        gate = gate_ref[pl.ds(mod_row(s), 1), :]

        def fin(c, carry):
            rows = pl.ds(pl.multiple_of(c * ch, ch), ch)
            o_ref[rows, :] = x_ref[rows, :] + gate * o_ref[rows, :]
            return carry
        lax.fori_loop(0, t_len // ch, fin, 0)


def _combine(idx, g, ye, x, mod, x_prev, *, seq_len, row_off, n_seq, ep, mod_row):
    n, d = x.shape
    cap = CAPACITY_FACTOR * seq_len // N_EXPERTS
    blk_off = row_off // seq_len
    cw = 512
    ncb = d // cw
    gate_blk = 5 * d // cw
    in_specs = [
        pl.BlockSpec((ep, cap, cw), lambda s, cb, eb, i, gg: (eb, s, cb)),
        pl.BlockSpec((seq_len, cw), lambda s, cb, eb, i, gg: (blk_off + s, cb)),
        pl.BlockSpec((mod.shape[0], cw), lambda s, cb, eb, i, gg: (0, gate_blk + cb)),
    ]
    args = [idx, g, ye, x, mod]
    aliases = {}
    kern = functools.partial(_combine_kernel, ep=ep, cap=cap, n_exp=N_EXPERTS, mod_row=mod_row)
    if x_prev is not None:
        in_specs.append(pl.BlockSpec(memory_space=pl.ANY))
        args.append(x_prev)
        aliases = {len(args) - 1: 0}
        kern = lambda *refs, _k=kern: _k(*refs[:5], *refs[6:])
    return pl.pallas_call(
        kern,
        grid_spec=pltpu.PrefetchScalarGridSpec(
            num_scalar_prefetch=2,
            grid=(n_seq, ncb, N_EXPERTS // ep),
            in_specs=in_specs,
            out_specs=pl.BlockSpec((seq_len, cw), lambda s, cb, eb, i, gg: (blk_off + s, cb)),
        ),
        out_shape=jax.ShapeDtypeStruct((n, d), F32),
        input_output_aliases=aliases,
        compiler_params=_params(3, 48),
        name=f"combine_{seq_len}",
    )(*args)


def _head_norm(x, gain):
    return x * lax.rsqrt(jnp.mean(x * x, axis=-1, keepdims=True) + EPS) * gain


def _qkv_kernel(*refs, rope, mod_row, n_heads, n_kv):
    if rope:
        (x_ref, mod_ref, gain_ref, w_ref, qg_ref, kg_ref, cos_ref, sin_ref,
         q_ref, k_ref, v_ref) = refs
    else:
        (x_ref, mod_ref, gain_ref, w_ref, qg_ref, kg_ref,
         q_ref, k_ref, v_ref, kf_ref, vf_ref) = refs
    d = x_ref.shape[1]
    hd = HEAD_DIM
    m = mod_row(pl.program_id(0))
    shift = mod_ref[pl.ds(m, 1), pl.ds(0, d)]
    scale = mod_ref[pl.ds(m, 1), pl.ds(d, d)]
    h = _modulated_norm(x_ref[...], gain_ref[...], shift, scale).astype(BF16)
    qkv = jnp.dot(h, w_ref[...], preferred_element_type=F32)

    if rope:
        cos = cos_ref[...]
        sin = sin_ref[...]
        lane = lax.broadcasted_iota(I32, cos.shape, 1)
        first_half = (lane % (hd // 2)) < hd // 4

        def rot(xh):
            partner = jnp.where(first_half, pltpu.roll(xh, hd - hd // 4, 1), pltpu.roll(xh, hd // 4, 1))
            return xh * cos + partner * sin
    else:
        rot = lambda xh: xh

    qg = qg_ref[...]
    kg = kg_ref[...]
    for i in range(n_heads):
        q_ref[:, i * hd:(i + 1) * hd] = rot(_head_norm(qkv[:, i * hd:(i + 1) * hd], qg)).astype(BF16)
    for i in range(n_kv):
        c0 = (n_heads + i) * hd
        kh = _head_norm(qkv[:, c0:c0 + hd], kg)
        if not rope:
            kf_ref[:, i * hd:(i + 1) * hd] = kh
        k_ref[:, i * hd:(i + 1) * hd] = rot(kh).astype(BF16)
    v = qkv[:, (n_heads + n_kv) * hd:]
    v_ref[...] = v.astype(BF16)
    if not rope:
        vf_ref[...] = v


def _qkv(x, mod, gain, w_qkv, q_gain, k_gain, tables, *, row_off, n_rows, seq_len, mod_row, n_heads, n_kv):
    n, d = x.shape
    hd = HEAD_DIM
    blk_off = row_off // ROW_BLOCK
    row_in = lambda i: (blk_off + i, 0)
    row = lambda i: (i, 0)
    fixed = lambda i: (0, 0)
    rope = tables is not None
    in_specs = [
        pl.BlockSpec((ROW_BLOCK, d), row_in),
        pl.BlockSpec(mod.shape, fixed),
        pl.BlockSpec((1, d), fixed),
        pl.BlockSpec(w_qkv.shape, fixed),
        pl.BlockSpec((1, hd), fixed),
        pl.BlockSpec((1, hd), fixed),
    ]
    args = [x, mod, gain, w_qkv, q_gain, k_gain]
    out_specs = [pl.BlockSpec((ROW_BLOCK, n_heads * hd), row),
                 pl.BlockSpec((ROW_BLOCK, n_kv * hd), row),
                 pl.BlockSpec((ROW_BLOCK, n_kv * hd), row)]
    out_shape = [jax.ShapeDtypeStruct((n_rows, n_heads * hd), BF16),
                 jax.ShapeDtypeStruct((n_rows, n_kv * hd), BF16),
                 jax.ShapeDtypeStruct((n_rows, n_kv * hd), BF16)]
    if rope:
        per_seq = seq_len // ROW_BLOCK
        in_specs += [pl.BlockSpec((ROW_BLOCK, hd), lambda i: (i % per_seq, 0))] * 2
        args += list(tables)
    else:
        out_specs += [pl.BlockSpec((ROW_BLOCK, n_kv * hd), row)] * 2
        out_shape += [jax.ShapeDtypeStruct((n_rows, n_kv * hd), F32)] * 2
    return pl.pallas_call(
        functools.partial(_qkv_kernel, rope=rope, mod_row=mod_row, n_heads=n_heads, n_kv=n_kv),
        grid=(n_rows // ROW_BLOCK,),
        in_specs=in_specs,
        out_specs=out_specs,
        out_shape=out_shape,
        compiler_params=_params(1, 48),
        name="qkv_rope" if rope else "qkv",
    )(*args)


def _attn_kernel(*refs, group, chunk, has_cache):
    if has_cache:
        q_ref, k_ref, v_ref, kc_ref, vc_ref, o_ref = refs
    else:
        q_ref, k_ref, v_ref, o_ref = refs
    hd = HEAD_DIM
    qb = q_ref.shape[0]
    q = jnp.concatenate([q_ref[:, g * hd:(g + 1) * hd] for g in range(group)], axis=0)
    sources = [(k_ref, v_ref, s0, chunk) for s0 in range(0, k_ref.shape[0], chunk)]
    if has_cache:
        sources.append((kc_ref.at[0], vc_ref.at[0], 0, kc_ref.shape[1]))
    scale = hd ** -0.5
    m = jnp.full((group * qb, 1), -jnp.inf, F32)
    l = jnp.zeros((group * qb, 1), F32)
    acc = jnp.zeros((group * qb, hd), F32)
    for kr, vr, s0, size in sources:
        k = kr[s0:s0 + size, :]
        v = vr[s0:s0 + size, :]
        s = lax.dot_general(q, k, (((1,), (1,)), ((), ())), preferred_element_type=F32) * scale
        m_new = jnp.maximum(m, jnp.max(s, axis=-1, keepdims=True))
        alpha = jnp.exp(m - m_new)
        p = jnp.exp(s - m_new)
        l = alpha * l + jnp.sum(p, axis=-1, keepdims=True)
        acc = alpha * acc + jnp.dot(p.astype(BF16), v, preferred_element_type=F32)
        m = m_new
    o = acc / l
    o_ref[...] = jnp.concatenate([o[g * qb:(g + 1) * qb] for g in range(group)], axis=1).astype(BF16)


def _attention(q, k, v, cache, *, n_seq, seq_len, qb, chunk, n_heads, n_kv):
    n = q.shape[0]
    hd = HEAD_DIM
    group = n_heads // n_kv
    nq = seq_len // qb
    in_specs = [
        pl.BlockSpec((qb, group * hd), lambda b, h, i: (b * nq + i, h)),
        pl.BlockSpec((seq_len, hd), lambda b, h, i: (b, h)),
        pl.BlockSpec((seq_len, hd), lambda b, h, i: (b, h)),
    ]
    args = [q, k, v]
    if cache is not None:
        past = cache[0].shape[1]
        in_specs += [pl.BlockSpec((1, past, hd), lambda b, h, i: (b, 0, h))] * 2
        args += list(cache)
    return pl.pallas_call(
        functools.partial(_attn_kernel, group=group, chunk=chunk, has_cache=cache is not None),
        grid=(n_seq, n_kv, nq),
        in_specs=in_specs,
        out_specs=pl.BlockSpec((qb, group * hd), lambda b, h, i: (b * nq + i, h)),
        out_shape=jax.ShapeDtypeStruct((n, n_heads * hd), BF16),
        compiler_params=_params(3, 48),
        name=f"attention_{seq_len}",
    )(*args)


def _final_norm_kernel(x_ref, gain_ref, o_ref):
    x = x_ref[...]
    o_ref[...] = (x * lax.rsqrt(jnp.mean(x * x, axis=-1, keepdims=True) + EPS)) * gain_ref[...]


def _final_norm(x, gain, *, row_off, n_rows):
    d = x.shape[1]
    blk_off = row_off // ROW_BLOCK
    return pl.pallas_call(
        _final_norm_kernel,
        grid=(n_rows // ROW_BLOCK,),
        in_specs=[pl.BlockSpec((ROW_BLOCK, d), lambda i: (blk_off + i, 0)),
                  pl.BlockSpec((1, d), lambda i: (0, 0))],
        out_specs=pl.BlockSpec((ROW_BLOCK, d), lambda i: (i, 0)),
        out_shape=jax.ShapeDtypeStruct((n_rows, d), F32),
        compiler_params=_params(1, 32),
        name="final_norm",
    )(x, gain)


def _rope_tables(seq_len):
    hd = HEAD_DIM
    axis = hd // 2
    t = jnp.arange(seq_len)
    inv = ROPE_BASE ** (-jnp.arange(axis // 2, dtype=F32) * 2.0 / axis)

    def half(pos):
        ang = pos.astype(F32)[:, None] * inv
        c, s = jnp.cos(ang), jnp.sin(ang)
        return jnp.concatenate([c, c], axis=1), jnp.concatenate([-s, s], axis=1)

    c_row, s_row = half(t // GRID_W)
    c_col, s_col = half(t % GRID_W)
    return jnp.concatenate([c_row, c_col], axis=1), jnp.concatenate([s_row, s_col], axis=1)


def kernel(x_prompt, x_sample, state_lru, cache_k, cache_v, c, c_ctx, w_mod, b_mod, norm_gain, final_gain,
           w_lru_in, lru_conv_w, lru_conv_b, lru_wa, lru_ba, lru_wx, lru_bx, lru_lambda, w_lru_out,
           w_qkv, q_norm, k_norm, w_attn_out, w_router, w_exp_gate, w_exp_up, w_exp_down):
    batch, seq, d = x_prompt.shape
    dec_batch, dec_seq, _ = x_sample.shape
    depth = w_mod.shape[0]
    n_p = batch * seq
    n_s = dec_batch * dec_seq
    n_kv = cache_k.shape[3]
    n_heads = w_attn_out.shape[1] // HEAD_DIM
    cap_p = CAPACITY_FACTOR * seq // N_EXPERTS
    cap_s = CAPACITY_FACTOR * dec_seq // N_EXPERTS

    x = jnp.concatenate([x_prompt.reshape(n_p, d), x_sample.reshape(n_s, d)], axis=0)

    cond = jnp.zeros((SUBLANES, d), F32).at[0].set(c_ctx).at[1:1 + dec_batch].set(c)
    mod_all = _mod_vectors(cond, w_mod, b_mod)

    bp = n_p // ROW_BLOCK
    bs = dec_seq // ROW_BLOCK
    row_all = lambda i: jnp.where(i < bp, 0, 1 + (i - bp) // bs)
    row_p = lambda i: 0
    row_s_blk = lambda i: 1 + i // bs
    row_s_seq = lambda s: 1 + s

    new_lru = new_k = new_v = None
    for l in range(depth):
        mod = mod_all[l]
        gain_a = norm_gain[l, 0].reshape(1, d)
        gain_c = norm_gain[l, 1].reshape(1, d)
        if l % 2 == 0:
            li = l // 2
            gate, xr = _lru_in(x, mod, gain_a, w_lru_in[li].astype(BF16), row_all)
            wg = jnp.concatenate([lru_wa[li, 0], lru_wx[li, 0], lru_wa[li, 1], lru_wx[li, 1]],
                                 axis=2).astype(BF16)
            bias4 = jnp.stack([lru_ba[li, 0], lru_bx[li, 0], lru_ba[li, 1], lru_bx[li, 1]])
            core = functools.partial(_lru_core, xr, gate, lru_conv_w[li], lru_conv_b[li].reshape(1, -1),
                                     wg, bias4, lru_lambda[li])
            y, fin = core(jnp.zeros((batch, 2, xr.shape[1]), F32), None, seq_len=seq, row_off=0)
            y, _ = core(state_lru[:, li], y, seq_len=dec_seq, row_off=n_p)
            new_lru = fin.astype(x.dtype)[:, None]
            mixed, w_out = y, w_lru_out[li].astype(BF16)
        else:
            ai = l // 2
            w = w_qkv[ai].astype(BF16)
            qg = q_norm[ai].reshape(1, -1)
            kg = k_norm[ai].reshape(1, -1)
            common = dict(n_heads=n_heads, n_kv=n_kv)
            q_p, k_p, v_p, kf, vf = _qkv(x, mod, gain_a, w, qg, kg, None, row_off=0, n_rows=n_p,
                                         seq_len=seq, mod_row=row_p, **common)
            q_s, k_s, v_s = _qkv(x, mod, gain_a, w, qg, kg, _rope_tables(dec_seq), row_off=n_p,
                                 n_rows=n_s, seq_len=dec_seq, mod_row=row_s_blk, **common)
            new_k = kf.reshape(batch, 1, seq, n_kv, HEAD_DIM)
            new_v = vf.reshape(batch, 1, seq, n_kv, HEAD_DIM)
            past = cache_k.shape[2]
            kc = cache_k[:, ai].reshape(dec_batch, past, n_kv * HEAD_DIM).astype(BF16)
            vc = cache_v[:, ai].reshape(dec_batch, past, n_kv * HEAD_DIM).astype(BF16)
            o_p = _attention(q_p, k_p, v_p, None, n_seq=batch, seq_len=seq, qb=seq, chunk=seq, **common)
            o_s = _attention(q_s, k_s, v_s, (kc, vc), n_seq=dec_batch, seq_len=dec_seq, qb=128,
                             chunk=1024, **common)
            mixed, w_out = jnp.concatenate([o_p, o_s], axis=0), w_attn_out[ai].astype(BF16)

        wr_hi, wr_lo = _split_bf16(w_router[l].T)
        x1, h2, lg_t = _mix_out(mixed, x, mod, gain_c, w_out, wr_hi, wr_lo, row_all)
        idx_p, g_p, idx_s, g_s = _route(lg_t, batch, seq, dec_batch, dec_seq)
        xs_p = _gather(idx_p, h2, seq_len=seq, row_off=0, n_seq=batch, ep=N_EXPERTS)
        xs_s = _gather(idx_s, h2, seq_len=dec_seq, row_off=n_p, n_seq=dec_batch, ep=1)
        ye_p, ye_s = _ffn(xs_p, xs_s, w_exp_gate, w_exp_up, w_exp_down, l)
        x = _combine(idx_p, g_p, ye_p, x1, mod, None, seq_len=seq, row_off=0, n_seq=batch,
                     ep=N_EXPERTS, mod_row=row_p)
        x = _combine(idx_s, g_s, ye_s, x1, mod, x, seq_len=dec_seq, row_off=n_p, n_seq=dec_batch,
                     ep=1, mod_row=row_s_seq)

    fg = final_gain.reshape(1, d)
    y_prompt = _final_norm(x, fg, row_off=0, n_rows=n_p).reshape(batch, seq, d)
    y_sample = _final_norm(x, fg, row_off=n_p, n_rows=n_s).reshape(dec_batch, dec_seq, d)
    return (y_prompt, y_sample, new_lru, new_k, new_v)
```

```python
import functools
import math

import jax
import jax.numpy as jnp
from jax import lax
from jax.experimental import pallas as pl
from jax.experimental.pallas import tpu as pltpu

F32 = jnp.float32
BF16 = jnp.bfloat16
I32 = jnp.int32

LANES = 128
SUBLANES = 8
MIB = 1024 * 1024

RG_C = 8.0
CONV_W = 4
CONV_PAD_L = 2
ROPE_BASE = 10000.0
GRID_W = 64
EPS = 1e-6
N_EXPERTS = 16
CAPACITY_FACTOR = 2
HEAD_DIM = 128
LRU_BLOCKS = 8

ROW_BLOCK = 512


def _params(n_axes, vmem_mib):
    return pltpu.CompilerParams(
        dimension_semantics=("arbitrary",) * n_axes, vmem_limit_bytes=vmem_mib * MIB)


def _split_bf16(x):
    hi = x.astype(BF16)
    lo = (x - hi.astype(F32)).astype(BF16)
    return hi, lo


def _modulated_norm(x, gain, shift, scale):
    y = x * lax.rsqrt(jnp.mean(x * x, axis=-1, keepdims=True) + EPS)
    return (y * gain) * (1.0 + scale) + shift


def _residual_stream(x_ref, prev, m):
    if prev is None:
        return x_ref[...]
    acc_ref, pmod_ref, xo_ref = prev
    d = x_ref.shape[1]
    x = x_ref[...] + pmod_ref[pl.ds(m, 1), pl.ds(5 * d, d)] * acc_ref[...]
    xo_ref[...] = x
    return x


def _mod_kernel(cond_ref, w_ref, b_ref, o_ref):
    c = cond_ref[...]
    a_hi, a_lo = _split_bf16(c * jax.nn.sigmoid(c))
    w_hi, w_lo = _split_bf16(w_ref[0])
    dot = functools.partial(jnp.dot, preferred_element_type=F32)
    o_ref[0] = dot(a_hi, w_hi) + (dot(a_lo, w_hi) + dot(a_hi, w_lo)) + b_ref[0]


def _mod_vectors(cond8, w_mod, b_mod):
    depth, d, d6 = w_mod.shape
    nb = 1536
    return pl.pallas_call(
        _mod_kernel,
        grid=(depth, d6 // nb),
        in_specs=[
            pl.BlockSpec((SUBLANES, d), lambda l, j: (0, 0)),
            pl.BlockSpec((1, d, nb), lambda l, j: (l, 0, j)),
            pl.BlockSpec((1, 1, nb), lambda l, j: (l, 0, j)),
        ],
        out_specs=pl.BlockSpec((1, SUBLANES, nb), lambda l, j: (l, 0, j)),
        out_shape=jax.ShapeDtypeStruct((depth, SUBLANES, d6), F32),
        compiler_params=_params(2, 40),
        name="adaln_mod",
    )(cond8, w_mod, b_mod.reshape(depth, 1, d6))


def _lru_in_kernel(*refs, mod_row, has_prev):
    if has_prev:
        x_ref, acc_ref, pmod_ref, mod_ref, gain_ref, w_ref, xo_ref, gate_ref, xr_ref = refs
        prev = (acc_ref, pmod_ref, xo_ref)
    else:
        x_ref, mod_ref, gain_ref, w_ref, gate_ref, xr_ref = refs
        prev = None
    d = x_ref.shape[1]
    m = mod_row(pl.program_id(0))
    x = _residual_stream(x_ref, prev, m)
    shift = mod_ref[pl.ds(m, 1), pl.ds(0, d)]
    scale = mod_ref[pl.ds(m, 1), pl.ds(d, d)]
    h = _modulated_norm(x, gain_ref[...], shift, scale).astype(BF16)
    u = jnp.dot(h, w_ref[...], preferred_element_type=F32)
    w = gate_ref.shape[1]
    gate_ref[...] = u[:, :w]
    xr_ref[...] = u[:, w:]


def _lru_in(x, prev, mod, gain, w_in, mod_row):
    n, d = x.shape
    w2 = w_in.shape[1]
    w = w2 // 2
    row = lambda i: (i, 0)
    fixed = lambda i: (0, 0)
    in_specs = [pl.BlockSpec((ROW_BLOCK, d), row)]
    args = [x]
    out_specs = [pl.BlockSpec((ROW_BLOCK, w), row), pl.BlockSpec((ROW_BLOCK, w), row)]
    out_shape = [jax.ShapeDtypeStruct((n, w), F32), jax.ShapeDtypeStruct((n, w), F32)]
    if prev is not None:
        in_specs += [pl.BlockSpec((ROW_BLOCK, d), row), pl.BlockSpec(prev[1].shape, fixed)]
        args += list(prev)
        out_specs = [pl.BlockSpec((ROW_BLOCK, d), row)] + out_specs
        out_shape = [jax.ShapeDtypeStruct((n, d), F32)] + out_shape
    in_specs += [pl.BlockSpec(mod.shape, fixed), pl.BlockSpec((1, d), fixed), pl.BlockSpec((d, w2), fixed)]
    args += [mod, gain, w_in]
    outs = pl.pallas_call(
        functools.partial(_lru_in_kernel, mod_row=mod_row, has_prev=prev is not None),
        grid=(n // ROW_BLOCK,),
        in_specs=in_specs,
        out_specs=out_specs,
        out_shape=out_shape,
        compiler_params=_params(1, 48),
        name="lru_in",
    )(*args)
    return outs if prev is not None else [x] + list(outs)


def _scan8(a, b, row, reverse):
    for s in (1, 2, 4):
        if reverse:
            a_s = pltpu.roll(a, SUBLANES - s, 0)
            b_s = pltpu.roll(b, SUBLANES - s, 0)
            keep = row < SUBLANES - s
        else:
            a_s = pltpu.roll(a, s, 0)
            b_s = pltpu.roll(b, s, 0)
            keep = row >= s
        a_s = jnp.where(keep, a_s, 1.0)
        b_s = jnp.where(keep, b_s, 0.0)
        b = a * b_s + b
        a = a * a_s
    return a, b


def _lru_core_kernel(xr_ref, gate_ref, cw_ref, cb_ref, wg_ref, bias_ref, lam_ref, h0_ref,
                     y_ref, fin_ref, pad, a_f, b_f, a_b, b_b):
    t_len, lb = xr_ref.shape
    ch = 256
    sup = 64
    halo = SUBLANES

    zero_rows = jnp.zeros((halo, lb), F32)
    pad[0:halo, :] = zero_rows
    pad[t_len + halo:t_len + 2 * halo, :] = zero_rows

    def copy_in(c, carry):
        r = pl.multiple_of(c * ch, ch)
        pad[pl.ds(r + halo, ch), :] = xr_ref[pl.ds(r, ch), :]
        return carry

    lax.fori_loop(0, t_len // ch, copy_in, 0)

    neg_lam = -lam_ref[...]
    softplus = jnp.maximum(neg_lam, 0.0) + jnp.log1p(jnp.exp(-jnp.abs(neg_lam)))
    cw = cw_ref[...]
    cbias = cb_ref[...]
    bias = bias_ref[...]
    wg = wg_ref[0]

    def gates(c, carry):
        r = pl.multiple_of(c * ch, ch)
        blk = pad[pl.ds(r, ch + 2 * halo), :]
        xc = cbias
        for k in range(CONV_W):
            o = halo - CONV_PAD_L + k
            xc = xc + blk[o:o + ch] * cw[k:k + 1]
        g = jnp.dot(xc.astype(BF16), wg, preferred_element_type=F32)
        for d, (a_ref, b_ref) in enumerate(((a_f, b_f), (a_b, b_b))):
            r_gate = jax.nn.sigmoid(g[:, (2 * d) * lb:(2 * d + 1) * lb] + bias[2 * d:2 * d + 1])
            i_gate = jax.nn.sigmoid(g[:, (2 * d + 1) * lb:(2 * d + 2) * lb] + bias[2 * d + 1:2 * d + 2])
            log_a = (-RG_C * r_gate) * softplus[d:d + 1]
            a = jnp.exp(log_a)
            b = jnp.sqrt(jnp.tanh(-log_a) * (1.0 + a * a)) * (i_gate * xc)
            a_ref[pl.ds(r, ch), :] = a
            b_ref[pl.ds(r, ch), :] = b
        return carry

    lax.fori_loop(0, t_len // ch, gates, 0)

    row = lax.broadcasted_iota(I32, (SUBLANES, lb), 0)
    n_sup = t_len // sup
    per = sup // SUBLANES

    def bwd(ci, h):
        r = pl.multiple_of((n_sup - 1 - ci) * sup, sup)
        for j in reversed(range(per)):
            rows = pl.ds(r + SUBLANES * j, SUBLANES)
            a_c, b_c = _scan8(a_b[rows, :], b_b[rows, :], row, True)
            hs = a_c * h + b_c
            b_b[rows, :] = hs
            h = jnp.broadcast_to(hs[0:1], (SUBLANES, lb))
        return h

    h_b = lax.fori_loop(0, n_sup, bwd, jnp.broadcast_to(h0_ref[0, 1:2, :], (SUBLANES, lb)))

    def fwd(ci, h):
        r = pl.multiple_of(ci * sup, sup)
        for j in range(per):
            rows = pl.ds(r + SUBLANES * j, SUBLANES)
            a_c, b_c = _scan8(a_f[rows, :], b_f[rows, :], row, False)
            hs = a_c * h + b_c
            b_f[rows, :] = hs + b_b[rows, :]
            h = jnp.broadcast_to(hs[SUBLANES - 1:SUBLANES], (SUBLANES, lb))
        return h

    h_f = lax.fori_loop(0, n_sup, fwd, jnp.broadcast_to(h0_ref[0, 0:1, :], (SUBLANES, lb)))

    fin_ref[0, 0:1, :] = h_f[0:1]
    fin_ref[0, 1:2, :] = h_b[0:1]

    def emit(c, carry):
        r = pl.multiple_of(c * ch, ch)
        gt = gate_ref[pl.ds(r, ch), :]
        cdf = 0.5 * (1.0 + jnp.tanh(math.sqrt(2.0 / math.pi) * (gt + 0.044715 * (gt * gt * gt))))
        y_ref[pl.ds(r, ch), :] = (b_f[pl.ds(r, ch), :] * (gt * cdf)).astype(BF16)
        return carry

    lax.fori_loop(0, t_len // ch, emit, 0)


def _lru_core(xr, gate, conv_w, conv_b, wg, bias4, lam, h0, *, seq_len):
    n, w = xr.shape
    n_seq = n // seq_len
    lb = w // LRU_BLOCKS
    tok = lambda s, c: (s, c)
    chan = lambda s, c: (0, c)
    return pl.pallas_call(
        _lru_core_kernel,
        grid=(n_seq, LRU_BLOCKS),
        in_specs=[
            pl.BlockSpec((seq_len, lb), tok),
            pl.BlockSpec((seq_len, lb), tok),
            pl.BlockSpec((CONV_W, lb), chan),
            pl.BlockSpec((1, lb), chan),
            pl.BlockSpec((1, lb, 4 * lb), lambda s, c: (c, 0, 0)),
            pl.BlockSpec((4, lb), chan),
            pl.BlockSpec((2, lb), chan),
            pl.BlockSpec((1, 2, lb), lambda s, c: (s, 0, c)),
        ],
        out_specs=[pl.BlockSpec((seq_len, lb), tok),
                   pl.BlockSpec((1, 2, lb), lambda s, c: (s, 0, c))],
        out_shape=[jax.ShapeDtypeStruct((n, w), BF16), jax.ShapeDtypeStruct((n_seq, 2, w), F32)],
        scratch_shapes=[pltpu.VMEM((seq_len + 2 * SUBLANES, lb), F32)]
        + [pltpu.VMEM((seq_len, lb), F32)] * 4,
        compiler_params=_params(2, 40),
        name=f"lru_core_{seq_len}",
    )(xr, gate, conv_w, conv_b, wg, bias4, lam, h0)


def _mix_out_kernel(a_ref, x_ref, mod_ref, gain_ref, w_ref, wrh_ref, wrl_ref,
                    x1_ref, h2_ref, lg_ref, *, mod_row):
    d = x_ref.shape[1]
    m = mod_row(pl.program_id(0))
    g_mix = mod_ref[pl.ds(m, 1), pl.ds(2 * d, d)]
    shift = mod_ref[pl.ds(m, 1), pl.ds(3 * d, d)]
    scale = mod_ref[pl.ds(m, 1), pl.ds(4 * d, d)]
    op = jnp.dot(a_ref[...], w_ref[...], preferred_element_type=F32)
    x1 = x_ref[...] + g_mix * op
    x1_ref[...] = x1
    h2 = _modulated_norm(x1, gain_ref[...], shift, scale)
    h_hi, h_lo = _split_bf16(h2)
    h2_ref[...] = h2.astype(h2_ref.dtype)
    nt = functools.partial(lax.dot_general, dimension_numbers=(((1,), (1,)), ((), ())),
                           preferred_element_type=F32)
    lg_ref[...] = nt(wrh_ref[...], h_hi) + (nt(wrl_ref[...], h_hi) + nt(wrh_ref[...], h_lo))


def _mix_out(a, x, mod, gain, w_out, wr_hi, wr_lo, mod_row, h2_dtype):
    n, d = x.shape
    e = wr_hi.shape[0]
    row = lambda i: (i, 0)
    fixed = lambda i: (0, 0)
    return pl.pallas_call(
        functools.partial(_mix_out_kernel, mod_row=mod_row),
        grid=(n // ROW_BLOCK,),
        in_specs=[
            pl.BlockSpec((ROW_BLOCK, a.shape[1]), row),
            pl.BlockSpec((ROW_BLOCK, d), row),
            pl.BlockSpec(mod.shape, fixed),
            pl.BlockSpec((1, d), fixed),
            pl.BlockSpec(w_out.shape, fixed),
            pl.BlockSpec(wr_hi.shape, fixed),
            pl.BlockSpec(wr_lo.shape, fixed),
        ],
        out_specs=[pl.BlockSpec((ROW_BLOCK, d), row), pl.BlockSpec((ROW_BLOCK, d), row),
                   pl.BlockSpec((e, ROW_BLOCK), lambda i: (0, i))],
        out_shape=[jax.ShapeDtypeStruct((n, d), F32), jax.ShapeDtypeStruct((n, d), h2_dtype),
                   jax.ShapeDtypeStruct((e, n), F32)],
        compiler_params=_params(1, 48),
        name="mix_out",
    )(a, x, mod, gain, w_out, wr_hi, wr_lo)


def _lane_sum(tiles):
    acc = tiles[0]
    for t in tiles[1:]:
        acc = acc + t
    return jnp.sum(acc, axis=1, keepdims=True)


def _exclusive_cumsum(flags, upper):
    out = []
    off = jnp.zeros((flags[0].shape[0], 1), F32)
    for f in flags:
        out.append(jnp.dot(f.astype(BF16), upper, preferred_element_type=F32) + off)
        off = off + jnp.sum(f, axis=1, keepdims=True)
    return out


def _route_group(lg_ref, n_seq, t_len, idx_ref, g_ref):
    cap = CAPACITY_FACTOR * t_len // N_EXPERTS
    affs = []
    for s in range(n_seq):
        lg = lg_ref[:, s * t_len:(s + 1) * t_len]
        ex = jnp.exp(lg - jnp.max(lg, axis=0, keepdims=True))
        affs.append(ex / jnp.sum(ex, axis=0, keepdims=True))
    aff = jnp.concatenate(affs, axis=0)
    n_rows = aff.shape[0]
    nt = t_len // LANES
    g = [aff[:, j * LANES:(j + 1) * LANES] for j in range(nt)]

    kth_bits = jnp.zeros((n_rows, 1), I32)
    for bit in range(30, -1, -1):
        cand = kth_bits | (1 << bit)
        cand_f = pltpu.bitcast(cand, F32)
        cnt = _lane_sum([jnp.where(t >= cand_f, 1, 0) for t in g])
        kth_bits = jnp.where(cnt >= cap, cand, kth_bits)
    kth = pltpu.bitcast(kth_bits, F32)

    lane = lax.broadcasted_iota(I32, (LANES, LANES), 0)
    upper = jnp.where(lane < lax.broadcasted_iota(I32, (LANES, LANES), 1), 1.0, 0.0).astype(BF16)
    gt = [t > kth for t in g]
    eq = [t == kth for t in g]
    need = (cap - _lane_sum([jnp.where(m, 1, 0) for m in gt])).astype(F32)
    eq_rank = _exclusive_cumsum([jnp.where(m, 1.0, 0.0) for m in eq], upper)
    sel = [jnp.logical_or(gt[j], jnp.logical_and(eq[j], eq_rank[j] < need)) for j in range(nt)]
    pos = _exclusive_cumsum([jnp.where(m, 1.0, 0.0) for m in sel], upper)

    lane_r = lax.broadcasted_iota(I32, (n_rows, LANES), 1)
    d = [jnp.where(sel[j], lane_r + j * LANES - pos[j].astype(I32), -1) for j in range(nt)]
    for k in range(t_len.bit_length() - 1):
        s = 1 << k
        if s < LANES:
            d_rot = [pltpu.roll(x, LANES - s, 1) for x in d]
            g_rot = [pltpu.roll(x, LANES - s, 1) for x in g]
            same = lane_r < LANES - s
            d_in = [jnp.where(same, d_rot[j], d_rot[(j + 1) % nt]) for j in range(nt)]
            g_in = [jnp.where(same, g_rot[j], g_rot[(j + 1) % nt]) for j in range(nt)]
        else:
            q = s // LANES
            d_in = [d[(j + q) % nt] for j in range(nt)]
            g_in = [g[(j + q) % nt] for j in range(nt)]
        new_d, new_g = [], []
        for j in range(nt):
            move = jnp.logical_and(d_in[j] >= 0, ((d_in[j] >> k) & 1) == 1)
            stay = jnp.logical_and(d[j] >= 0, ((d[j] >> k) & 1) == 0)
            new_d.append(jnp.where(move, d_in[j], jnp.where(stay, d[j], -1)))
            new_g.append(jnp.where(move, g_in[j], g[j]))
        d, g = new_d, new_g

    for j in range(idx_ref.shape[1] // LANES):
        idx_ref[:, j * LANES:(j + 1) * LANES] = lane_r + j * LANES + d[j]
        g_ref[:, j * LANES:(j + 1) * LANES] = g[j]


def _route_kernel(lgp_ref, lgs_ref, idx_p_ref, g_p_ref, idx_s_ref, g_s_ref, *, p_len, s_len):
    _route_group(lgp_ref, lgp_ref.shape[1] // p_len, p_len, idx_p_ref, g_p_ref)
    _route_group(lgs_ref, lgs_ref.shape[1] // s_len, s_len, idx_s_ref, g_s_ref)


def _route(lg_p, lg_s, p_len, s_len):
    e = lg_p.shape[0]
    np_seq = lg_p.shape[1] // p_len
    ns_seq = lg_s.shape[1] // s_len
    cap_p = CAPACITY_FACTOR * p_len // N_EXPERTS
    cap_s = CAPACITY_FACTOR * s_len // N_EXPERTS
    wp = max(cap_p, LANES)
    ws = max(cap_s, LANES)
    idx_p, g_p, idx_s, g_s = pl.pallas_call(
        functools.partial(_route_kernel, p_len=p_len, s_len=s_len),
        out_shape=[jax.ShapeDtypeStruct((np_seq * e, wp), I32),
                   jax.ShapeDtypeStruct((np_seq * e, wp), F32),
                   jax.ShapeDtypeStruct((ns_seq * e, ws), I32),
                   jax.ShapeDtypeStruct((ns_seq * e, ws), F32)],
        compiler_params=pltpu.CompilerParams(vmem_limit_bytes=40 * MIB),
        name="route",
    )(lg_p, lg_s)
    return (idx_p[:, :cap_p].reshape(np_seq, e * cap_p), g_p[:, :cap_p].reshape(np_seq, e * cap_p),
            idx_s[:, :cap_s].reshape(-1), g_s[:, :cap_s].reshape(-1))


def _gather_onehot_kernel(idx_ref, h_ref, o_ref):
    t_len = h_ref.shape[0]
    n_exp, cap, _ = o_ref.shape
    idx = idx_ref[0]
    onehot = jnp.where(idx == lax.broadcasted_iota(I32, (idx.shape[0], t_len), 1), 1.0, 0.0).astype(BF16)
    xs = jnp.dot(onehot, h_ref[...], preferred_element_type=F32).astype(BF16)
    for e in range(n_exp):
        o_ref[e] = xs[e * cap:(e + 1) * cap]


def _gather_onehot(idx_col, h, *, seq_len):
    n, d = h.shape
    n_seq, slots, _ = idx_col.shape
    cap = slots // N_EXPERTS
    return pl.pallas_call(
        _gather_onehot_kernel,
        grid=(n_seq,),
        in_specs=[pl.BlockSpec((1, slots, 1), lambda s: (s, 0, 0)),
                  pl.BlockSpec((seq_len, d), lambda s: (s, 0))],
        out_specs=pl.BlockSpec((N_EXPERTS, cap, d), lambda s: (0, s, 0)),
        out_shape=jax.ShapeDtypeStruct((N_EXPERTS, n_seq * cap, d), BF16),
        compiler_params=_params(1, 32),
        name="gather_onehot",
    )(idx_col, h)


def _gather_rows_kernel(idx_ref, h_ref, o_ref, rows, *, cap, n_exp):
    base = (pl.program_id(0) * n_exp + pl.program_id(1)) * cap

    def body(c, carry):
        rows[pl.ds(c, 1), :] = h_ref[pl.ds(idx_ref[base + c], 1), :]
        return carry

    lax.fori_loop(0, cap, body, 0, unroll=8)
    o_ref[0] = rows[...].astype(BF16)


def _gather_rows(idx, h, *, seq_len):
    n, d = h.shape
    n_seq = n // seq_len
    cap = CAPACITY_FACTOR * seq_len // N_EXPERTS
    return pl.pallas_call(
        functools.partial(_gather_rows_kernel, cap=cap, n_exp=N_EXPERTS),
        grid_spec=pltpu.PrefetchScalarGridSpec(
            num_scalar_prefetch=1,
            grid=(n_seq, N_EXPERTS),
            in_specs=[pl.BlockSpec((seq_len, d), lambda s, e, idx: (s, 0))],
            out_specs=pl.BlockSpec((1, cap, d), lambda s, e, idx: (e, s, 0)),
            scratch_shapes=[pltpu.VMEM((cap, d), F32)],
        ),
        out_shape=jax.ShapeDtypeStruct((N_EXPERTS, n_seq * cap, d), BF16),
        compiler_params=_params(2, 48),
        name="gather_rows",
    )(idx, h)


def _ffn_kernel(xp_ref, xs_ref, wg_ref, wu_ref, wd_ref, yp_ref, ys_ref):
    f = pl.program_id(1)
    wg = wg_ref[0, 0].astype(BF16)
    wu = wu_ref[0, 0].astype(BF16)
    wd = wd_ref[0, 0].astype(BF16)
    rc = 512
    for x_ref, y_ref in ((xp_ref, yp_ref), (xs_ref, ys_ref)):
        for r in range(0, x_ref.shape[1], rc):
            x = x_ref[0, r:r + rc, :]
            hg = jnp.dot(x, wg, preferred_element_type=F32)
            hu = jnp.dot(x, wu, preferred_element_type=F32)
            hid = ((hg * jax.nn.sigmoid(hg)) * hu).astype(BF16)
            part = jnp.dot(hid, wd, preferred_element_type=F32)

            @pl.when(f == 0)
            def _():
                y_ref[0, r:r + rc, :] = part

            @pl.when(f != 0)
            def _():
                y_ref[0, r:r + rc, :] += part


def _ffn(xs_p, xs_s, w_gate, w_up, w_down, layer):
    n_exp, rp, d = xs_p.shape
    rs = xs_s.shape[1]
    ff = w_gate.shape[3]
    fc = 512
    return pl.pallas_call(
        _ffn_kernel,
        grid=(n_exp, ff // fc),
        in_specs=[
            pl.BlockSpec((1, rp, d), lambda e, f: (e, 0, 0)),
            pl.BlockSpec((1, rs, d), lambda e, f: (e, 0, 0)),
            pl.BlockSpec((1, 1, d, fc), lambda e, f: (layer, e, 0, f)),
            pl.BlockSpec((1, 1, d, fc), lambda e, f: (layer, e, 0, f)),
            pl.BlockSpec((1, 1, fc, d), lambda e, f: (layer, e, f, 0)),
        ],
        out_specs=[pl.BlockSpec((1, rp, d), lambda e, f: (e, 0, 0)),
                   pl.BlockSpec((1, rs, d), lambda e, f: (e, 0, 0))],
        out_shape=[jax.ShapeDtypeStruct((n_exp, rp, d), F32),
                   jax.ShapeDtypeStruct((n_exp, rs, d), F32)],
        compiler_params=_params(2, 56),
        name="expert_ffn",
    )(xs_p, xs_s, w_gate, w_up, w_down)


def _combine_onehot_kernel(idx_ref, g_ref, ye_ref, o_ref):
    t_len = o_ref.shape[0]
    n_exp = ye_ref.shape[0]
    idx = idx_ref[0]
    hit = idx == lax.broadcasted_iota(I32, (t_len, idx.shape[1]), 0)
    g_hi, g_lo = _split_bf16(g_ref[0])
    m_hi = jnp.where(hit, g_hi.astype(F32), 0.0).astype(BF16)
    m_lo = jnp.where(hit, g_lo.astype(F32), 0.0).astype(BF16)
    y_hi, y_lo = _split_bf16(jnp.concatenate([ye_ref[e] for e in range(n_exp)], axis=0))
    dot = functools.partial(jnp.dot, preferred_element_type=F32)
    o_ref[...] = dot(m_hi, y_hi) + (dot(m_lo, y_hi) + dot(m_hi, y_lo))


def _combine_onehot(idx_row, g_row, ye, *, seq_len):
    n_exp, rows, d = ye.shape
    n_seq, _, slots = idx_row.shape
    cap = slots // n_exp
    return pl.pallas_call(
        _combine_onehot_kernel,
        grid=(n_seq,),
        in_specs=[pl.BlockSpec((1, 1, slots), lambda s: (s, 0, 0)),
                  pl.BlockSpec((1, 1, slots), lambda s: (s, 0, 0)),
                  pl.BlockSpec((n_exp, cap, d), lambda s: (0, s, 0))],
        out_specs=pl.BlockSpec((seq_len, d), lambda s: (s, 0)),
        out_shape=jax.ShapeDtypeStruct((n_seq * seq_len, d), F32),
        compiler_params=_params(1, 32),
        name="combine_onehot",
    )(idx_row, g_row, ye)


COMBINE_GROUP = 4


def _combine_rows_kernel(idx_ref, g_ref, ye_ref, o_ref, *, cap, n_exp):
    e = pl.program_id(1)
    t_len, d = o_ref.shape
    ch = 256

    @pl.when(e == 0)
    def _():
        def zero(c, carry):
            o_ref[pl.ds(pl.multiple_of(c * ch, ch), ch), :] = jnp.zeros((ch, d), F32)
            return carry
        lax.fori_loop(0, t_len // ch, zero, 0)

    base = (pl.program_id(0) * n_exp + e) * cap

    def body(i, carry):
        c0 = i * COMBINE_GROUP
        ts = [idx_ref[base + c0 + k] for k in range(COMBINE_GROUP)]
        vals = [o_ref[pl.ds(ts[k], 1), :] + g_ref[base + c0 + k] * ye_ref[0, pl.ds(c0 + k, 1), :]
                for k in range(COMBINE_GROUP)]
        for k in range(COMBINE_GROUP):
            o_ref[pl.ds(ts[k], 1), :] = vals[k]
        return carry

    lax.fori_loop(0, cap // COMBINE_GROUP, body, 0, unroll=2)


def _combine_rows(idx, g, ye, *, seq_len):
    n_exp, rows, d = ye.shape
    cap = CAPACITY_FACTOR * seq_len // N_EXPERTS
    n_seq = rows // cap
    return pl.pallas_call(
        functools.partial(_combine_rows_kernel, cap=cap, n_exp=n_exp),
        grid_spec=pltpu.PrefetchScalarGridSpec(
            num_scalar_prefetch=2,
            grid=(n_seq, n_exp),
            in_specs=[pl.BlockSpec((1, cap, d), lambda s, e, i, gg: (e, s, 0))],
            out_specs=pl.BlockSpec((seq_len, d), lambda s, e, i, gg: (s, 0)),
        ),
        out_shape=jax.ShapeDtypeStruct((n_seq * seq_len, d), F32),
        compiler_params=_params(2, 48),
        name="combine_rows",
    )(idx, g, ye)


def _head_norm(x, gain):
    return x * lax.rsqrt(jnp.mean(x * x, axis=-1, keepdims=True) + EPS) * gain


def _qkv_kernel(*refs, rope, has_prev, mod_row, n_heads, n_kv):
    refs = list(refs)
    x_ref = refs.pop(0)
    prev = None
    if has_prev:
        acc_ref, pmod_ref = refs.pop(0), refs.pop(0)
    mod_ref, gain_ref, w_ref, qg_ref, kg_ref = refs[:5]
    refs = refs[5:]
    if rope:
        cos_ref, sin_ref = refs.pop(0), refs.pop(0)
    if has_prev:
        prev = (acc_ref, pmod_ref, refs.pop(0))
    q_ref, k_ref, v_ref = refs[:3]
    d = x_ref.shape[1]
    hd = HEAD_DIM
    m = mod_row(pl.program_id(0))
    x = _residual_stream(x_ref, prev, m)
    shift = mod_ref[pl.ds(m, 1), pl.ds(0, d)]
    scale = mod_ref[pl.ds(m, 1), pl.ds(d, d)]
    h = _modulated_norm(x, gain_ref[...], shift, scale).astype(BF16)
    qkv = jnp.dot(h, w_ref[...], preferred_element_type=F32)

    if rope:
        cos = cos_ref[...]
        sin = sin_ref[...]
        lane = lax.broadcasted_iota(I32, cos.shape, 1)
        first_half = (lane % (hd // 2)) < hd // 4

        def rot(xh):
            partner = jnp.where(first_half, pltpu.roll(xh, hd - hd // 4, 1), pltpu.roll(xh, hd // 4, 1))
            return xh * cos + partner * sin
    else:
        rot = lambda xh: xh

    qg = qg_ref[...]
    kg = kg_ref[...]
    q_scale = math.log2(math.e) * hd ** -0.5
    for i in range(n_heads):
        qh = rot(_head_norm(qkv[:, i * hd:(i + 1) * hd], qg)) * q_scale
        q_ref[:, i * hd:(i + 1) * hd] = qh.astype(BF16)
    for i in range(n_kv):
        c0 = (n_heads + i) * hd
        kh = _head_norm(qkv[:, c0:c0 + hd], kg)
        if not rope:
            refs[3][:, i * hd:(i + 1) * hd] = kh
        k_ref[:, i * hd:(i + 1) * hd] = rot(kh).astype(BF16)
    v = qkv[:, (n_heads + n_kv) * hd:]
    v_ref[...] = v.astype(BF16)
    if not rope:
        refs[4][...] = v


def _qkv(x, prev, mod, gain, w_qkv, q_gain, k_gain, tables, *, seq_len, mod_row, n_heads, n_kv):
    n, d = x.shape
    hd = HEAD_DIM
    row = lambda i: (i, 0)
    fixed = lambda i: (0, 0)
    rope = tables is not None
    in_specs = [pl.BlockSpec((ROW_BLOCK, d), row)]
    args = [x]
    if prev is not None:
        in_specs += [pl.BlockSpec((ROW_BLOCK, d), row), pl.BlockSpec(prev[1].shape, fixed)]
        args += list(prev)
    in_specs += [
        pl.BlockSpec(mod.shape, fixed),
        pl.BlockSpec((1, d), fixed),
        pl.BlockSpec(w_qkv.shape, fixed),
        pl.BlockSpec((1, hd), fixed),
        pl.BlockSpec((1, hd), fixed),
    ]
    args += [mod, gain, w_qkv, q_gain, k_gain]
    if rope:
        per_seq = seq_len // ROW_BLOCK
        in_specs += [pl.BlockSpec((ROW_BLOCK, hd), lambda i: (i % per_seq, 0))] * 2
        args += list(tables)
    out_specs, out_shape = [], []
    if prev is not None:
        out_specs.append(pl.BlockSpec((ROW_BLOCK, d), row))
        out_shape.append(jax.ShapeDtypeStruct((n, d), F32))
    out_specs += [pl.BlockSpec((ROW_BLOCK, n_heads * hd), row),
                  pl.BlockSpec((ROW_BLOCK, n_kv * hd), row),
                  pl.BlockSpec((ROW_BLOCK, n_kv * hd), row)]
    out_shape += [jax.ShapeDtypeStruct((n, n_heads * hd), BF16),
                  jax.ShapeDtypeStruct((n, n_kv * hd), BF16),
                  jax.ShapeDtypeStruct((n, n_kv * hd), BF16)]
    if not rope:
        out_specs += [pl.BlockSpec((ROW_BLOCK, n_kv * hd), row)] * 2
        out_shape += [jax.ShapeDtypeStruct((n, n_kv * hd), F32)] * 2
    outs = pl.pallas_call(
        functools.partial(_qkv_kernel, rope=rope, has_prev=prev is not None, mod_row=mod_row,
                          n_heads=n_heads, n_kv=n_kv),
        grid=(n // ROW_BLOCK,),
        in_specs=in_specs,
        out_specs=out_specs,
        out_shape=out_shape,
        compiler_params=_params(1, 48),
        name="qkv_rope" if rope else "qkv",
    )(*args)
    return list(outs) if prev is not None else [x] + list(outs)


def _attn_kernel(*refs, group, chunk, has_cache):
    if has_cache:
        q_ref, k_ref, v_ref, kc_ref, vc_ref, o_ref = refs
    else:
        q_ref, k_ref, v_ref, o_ref = refs
    hd = HEAD_DIM
    qb = q_ref.shape[0]
    rows = group * qb
    q = jnp.concatenate([q_ref[:, g * hd:(g + 1) * hd] for g in range(group)], axis=0)
    sources = [(k_ref, v_ref, s0, chunk) for s0 in range(0, k_ref.shape[0], chunk)]
    if has_cache:
        sources.append((kc_ref.at[0], vc_ref.at[0], 0, kc_ref.shape[1]))
    m = jnp.full((rows, 1), -jnp.inf, F32)
    acc = jnp.zeros((rows, 2 * hd), F32)
    for kr, vr, s0, size in sources:
        s = lax.dot_general(q, kr[s0:s0 + size, :], (((1,), (1,)), ((), ())),
                            preferred_element_type=F32)
        m_new = jnp.maximum(m, jnp.max(s, axis=-1, keepdims=True))
        p = jnp.exp2(s - m_new).astype(BF16)
        v_ones = jnp.concatenate([vr[s0:s0 + size, :], jnp.ones((size, hd), BF16)], axis=1)
        acc = jnp.exp2(m - m_new) * acc + jnp.dot(p, v_ones, preferred_element_type=F32)
        m = m_new
    o = acc[:, :hd] / acc[:, hd:]
    o_ref[...] = jnp.concatenate([o[g * qb:(g + 1) * qb] for g in range(group)], axis=1).astype(BF16)


def _attention(q, k, v, cache, *, seq_len, qb, chunk, n_heads, n_kv):
    n = q.shape[0]
    hd = HEAD_DIM
    group = n_heads // n_kv
    nq = seq_len // qb
    in_specs = [
        pl.BlockSpec((qb, group * hd), lambda b, h, i: (b * nq + i, h)),
        pl.BlockSpec((seq_len, hd), lambda b, h, i: (b, h)),
        pl.BlockSpec((seq_len, hd), lambda b, h, i: (b, h)),
    ]
    args = [q, k, v]
    if cache is not None:
        past = cache[0].shape[1]
        in_specs += [pl.BlockSpec((1, past, hd), lambda b, h, i: (b, 0, h))] * 2
        args += list(cache)
    return pl.pallas_call(
        functools.partial(_attn_kernel, group=group, chunk=chunk, has_cache=cache is not None),
        grid=(n // seq_len, n_kv, nq),
        in_specs=in_specs,
        out_specs=pl.BlockSpec((qb, group * hd), lambda b, h, i: (b * nq + i, h)),
        out_shape=jax.ShapeDtypeStruct((n, n_heads * hd), BF16),
        compiler_params=_params(3, 48),
        name=f"attention_{seq_len}",
    )(*args)


def _final_norm_kernel(x_ref, acc_ref, pmod_ref, gain_ref, o_ref, *, mod_row):
    d = x_ref.shape[1]
    m = mod_row(pl.program_id(0))
    x = x_ref[...] + pmod_ref[pl.ds(m, 1), pl.ds(5 * d, d)] * acc_ref[...]
    o_ref[...] = (x * lax.rsqrt(jnp.mean(x * x, axis=-1, keepdims=True) + EPS)) * gain_ref[...]


def _final_norm(x, acc, pmod, gain, mod_row):
    n, d = x.shape
    row = lambda i: (i, 0)
    fixed = lambda i: (0, 0)
    return pl.pallas_call(
        functools.partial(_final_norm_kernel, mod_row=mod_row),
        grid=(n // ROW_BLOCK,),
        in_specs=[pl.BlockSpec((ROW_BLOCK, d), row), pl.BlockSpec((ROW_BLOCK, d), row),
                  pl.BlockSpec(pmod.shape, fixed), pl.BlockSpec((1, d), fixed)],
        out_specs=pl.BlockSpec((ROW_BLOCK, d), row),
        out_shape=jax.ShapeDtypeStruct((n, d), F32),
        compiler_params=_params(1, 32),
        name="final_norm",
    )(x, acc, pmod, gain)


def _rope_tables(seq_len):
    hd = HEAD_DIM
    axis = hd // 2
    t = jnp.arange(seq_len)
    inv = ROPE_BASE ** (-jnp.arange(axis // 2, dtype=F32) * 2.0 / axis)

    def half(pos):
        ang = pos.astype(F32)[:, None] * inv
        c, s = jnp.cos(ang), jnp.sin(ang)
        return jnp.concatenate([c, c], axis=1), jnp.concatenate([-s, s], axis=1)

    c_row, s_row = half(t // GRID_W)
    c_col, s_col = half(t % GRID_W)
    return jnp.concatenate([c_row, c_col], axis=1), jnp.concatenate([s_row, s_col], axis=1)


def kernel(x_prompt, x_sample, state_lru, cache_k, cache_v, c, c_ctx, w_mod, b_mod, norm_gain, final_gain,
           w_lru_in, lru_conv_w, lru_conv_b, lru_wa, lru_ba, lru_wx, lru_bx, lru_lambda, w_lru_out,
           w_qkv, q_norm, k_norm, w_attn_out, w_router, w_exp_gate, w_exp_up, w_exp_down):
    batch, seq, d = x_prompt.shape
    dec_batch, dec_seq, _ = x_sample.shape
    depth = w_mod.shape[0]
    n_kv = cache_k.shape[3]
    n_heads = w_attn_out.shape[1] // HEAD_DIM
    heads = dict(n_heads=n_heads, n_kv=n_kv)

    cond = jnp.zeros((SUBLANES, d), F32).at[0].set(c_ctx).at[1:1 + dec_batch].set(c)
    mod_all = _mod_vectors(cond, w_mod, b_mod)

    blocks_per_seq = dec_seq // ROW_BLOCK
    xs = [x_prompt.reshape(batch * seq, d), x_sample.reshape(dec_batch * dec_seq, d)]
    lens = [seq, dec_seq]
    mod_rows = [lambda i: 0, lambda i: 1 + i // blocks_per_seq]
    prevs = [None, None]

    new_lru, new_k, new_v = [], [], []
    for l in range(depth):
        mod = mod_all[l]
        gain_a = norm_gain[l, 0].reshape(1, d)
        gain_c = norm_gain[l, 1].reshape(1, d)
        mixed = []
        if l % 2 == 0:
            li = l // 2
            w_in = w_lru_in[li].astype(BF16)
            wg = jnp.concatenate([lru_wa[li, 0], lru_wx[li, 0], lru_wa[li, 1], lru_wx[li, 1]],
                                 axis=2).astype(BF16)
            bias4 = jnp.stack([lru_ba[li, 0], lru_bx[li, 0], lru_ba[li, 1], lru_bx[li, 1]])
            h0s = [jnp.zeros((batch, 2, w_in.shape[1] // 2), F32), state_lru[:, li]]
            for gi in range(2):
                xs[gi], gate, xr = _lru_in(xs[gi], prevs[gi], mod, gain_a, w_in, mod_rows[gi])
                y, fin = _lru_core(xr, gate, lru_conv_w[li], lru_conv_b[li].reshape(1, -1), wg, bias4,
                                   lru_lambda[li], h0s[gi], seq_len=lens[gi])
                mixed.append(y)
                if gi == 0:
                    new_lru.append(fin.astype(x_prompt.dtype))
            w_out = w_lru_out[li].astype(BF16)
        else:
            ai = l // 2
            w = w_qkv[ai].astype(BF16)
            qg = q_norm[ai].reshape(1, -1)
            kg = k_norm[ai].reshape(1, -1)
            xs[0], q_p, k_p, v_p, kf, vf = _qkv(xs[0], prevs[0], mod, gain_a, w, qg, kg, None,
                                                seq_len=seq, mod_row=mod_rows[0], **heads)
            xs[1], q_s, k_s, v_s = _qkv(xs[1], prevs[1], mod, gain_a, w, qg, kg, _rope_tables(dec_seq),
                                        seq_len=dec_seq, mod_row=mod_rows[1], **heads)
            new_k.append(kf.reshape(batch, seq, n_kv, HEAD_DIM))
            new_v.append(vf.reshape(batch, seq, n_kv, HEAD_DIM))
            past = cache_k.shape[2]
            kc = cache_k[:, ai].reshape(dec_batch, past, n_kv * HEAD_DIM).astype(BF16)
            vc = cache_v[:, ai].reshape(dec_batch, past, n_kv * HEAD_DIM).astype(BF16)
            mixed.append(_attention(q_p, k_p, v_p, None, seq_len=seq, qb=seq, chunk=seq, **heads))
            mixed.append(_attention(q_s, k_s, v_s, (kc, vc), seq_len=dec_seq, qb=128, chunk=1024, **heads))
            w_out = w_attn_out[ai].astype(BF16)

        wr_hi, wr_lo = _split_bf16(w_router[l].T)
        x1_p, h2_p, lg_p = _mix_out(mixed[0], xs[0], mod, gain_c, w_out, wr_hi, wr_lo, mod_rows[0], BF16)
        x1_s, h2_s, lg_s = _mix_out(mixed[1], xs[1], mod, gain_c, w_out, wr_hi, wr_lo, mod_rows[1], F32)
        idx_p, g_p, idx_s, g_s = _route(lg_p, lg_s, seq, dec_seq)
        xe_p = _gather_onehot(idx_p[:, :, None], h2_p, seq_len=seq)
        xe_s = _gather_rows(idx_s, h2_s, seq_len=dec_seq)
        ye_p, ye_s = _ffn(xe_p, xe_s, w_exp_gate, w_exp_up, w_exp_down, l)
        acc_p = _combine_onehot(idx_p[:, None, :], g_p[:, None, :], ye_p, seq_len=seq)
        acc_s = _combine_rows(idx_s, g_s, ye_s, seq_len=dec_seq)
        xs = [x1_p, x1_s]
        prevs = [(acc_p, mod), (acc_s, mod)]

    fg = final_gain.reshape(1, d)
    y_prompt = _final_norm(xs[0], *prevs[0], fg, mod_rows[0]).reshape(batch, seq, d)
    y_sample = _final_norm(xs[1], *prevs[1], fg, mod_rows[1]).reshape(dec_batch, dec_seq, d)
    return (y_prompt, y_sample, jnp.stack(new_lru, axis=1), jnp.stack(new_k, axis=1), jnp.stack(new_v, axis=1))
```

```python
import functools
import math

import jax
import jax.numpy as jnp
from jax import lax
from jax.experimental import pallas as pl
from jax.experimental.pallas import tpu as pltpu

F32 = jnp.float32
BF16 = jnp.bfloat16
I32 = jnp.int32

LANES = 128
SUBLANES = 8
MIB = 1024 * 1024

RG_C = 8.0
CONV_W = 4
CONV_PAD_L = 2
ROPE_BASE = 10000.0
GRID_W = 64
EPS = 1e-6
N_EXPERTS = 16
CAPACITY_FACTOR = 2
HEAD_DIM = 128
LRU_BLOCKS = 8

ROW_BLOCK = 512


def _params(n_axes, vmem_mib):
    return pltpu.CompilerParams(
        dimension_semantics=("arbitrary",) * n_axes, vmem_limit_bytes=vmem_mib * MIB)


def _split_bf16(x):
    hi = x.astype(BF16)
    lo = (x - hi.astype(F32)).astype(BF16)
    return hi, lo


def _modulated_norm(x, gain, shift, scale):
    y = x * lax.rsqrt(jnp.mean(x * x, axis=-1, keepdims=True) + EPS)
    return (y * gain) * (1.0 + scale) + shift


def _residual_stream(x_ref, prev, m):
    if prev is None:
        return x_ref[...]
    acc_ref, pmod_ref, xo_ref = prev
    d = x_ref.shape[1]
    x = x_ref[...] + pmod_ref[pl.ds(m, 1), pl.ds(5 * d, d)] * acc_ref[...]
    xo_ref[...] = x
    return x


def _mod_kernel(cond_ref, w_ref, b_ref, o_ref):
    c = cond_ref[...]
    a_hi, a_lo = _split_bf16(c * jax.nn.sigmoid(c))
    w_hi, w_lo = _split_bf16(w_ref[0])
    dot = functools.partial(jnp.dot, preferred_element_type=F32)
    o_ref[0] = dot(a_hi, w_hi) + (dot(a_lo, w_hi) + dot(a_hi, w_lo)) + b_ref[0]


def _mod_vectors(cond8, w_mod, b_mod):
    depth, d, d6 = w_mod.shape
    nb = 1536
    return pl.pallas_call(
        _mod_kernel,
        grid=(depth, d6 // nb),
        in_specs=[
            pl.BlockSpec((SUBLANES, d), lambda l, j: (0, 0)),
            pl.BlockSpec((1, d, nb), lambda l, j: (l, 0, j)),
            pl.BlockSpec((1, 1, nb), lambda l, j: (l, 0, j)),
        ],
        out_specs=pl.BlockSpec((1, SUBLANES, nb), lambda l, j: (l, 0, j)),
        out_shape=jax.ShapeDtypeStruct((depth, SUBLANES, d6), F32),
        compiler_params=_params(2, 40),
        name="adaln_mod",
    )(cond8, w_mod, b_mod.reshape(depth, 1, d6))


def _lru_in_kernel(*refs, mod_row, has_prev):
    if has_prev:
        x_ref, acc_ref, pmod_ref, mod_ref, gain_ref, w_ref, xo_ref, gate_ref, xr_ref = refs
        prev = (acc_ref, pmod_ref, xo_ref)
    else:
        x_ref, mod_ref, gain_ref, w_ref, gate_ref, xr_ref = refs
        prev = None
    d = x_ref.shape[1]
    m = mod_row(pl.program_id(0))
    x = _residual_stream(x_ref, prev, m)
    shift = mod_ref[pl.ds(m, 1), pl.ds(0, d)]
    scale = mod_ref[pl.ds(m, 1), pl.ds(d, d)]
    h = _modulated_norm(x, gain_ref[...], shift, scale).astype(BF16)
    u = jnp.dot(h, w_ref[...], preferred_element_type=F32)
    w = gate_ref.shape[1]
    gate_ref[...] = u[:, :w]
    xr_ref[...] = u[:, w:]


def _lru_in(x, prev, mod, gain, w_in, mod_row):
    n, d = x.shape
    w2 = w_in.shape[1]
    w = w2 // 2
    row = lambda i: (i, 0)
    fixed = lambda i: (0, 0)
    in_specs = [pl.BlockSpec((ROW_BLOCK, d), row)]
    args = [x]
    out_specs = [pl.BlockSpec((ROW_BLOCK, w), row), pl.BlockSpec((ROW_BLOCK, w), row)]
    out_shape = [jax.ShapeDtypeStruct((n, w), F32), jax.ShapeDtypeStruct((n, w), F32)]
    if prev is not None:
        in_specs += [pl.BlockSpec((ROW_BLOCK, d), row), pl.BlockSpec(prev[1].shape, fixed)]
        args += list(prev)
        out_specs = [pl.BlockSpec((ROW_BLOCK, d), row)] + out_specs
        out_shape = [jax.ShapeDtypeStruct((n, d), F32)] + out_shape
    in_specs += [pl.BlockSpec(mod.shape, fixed), pl.BlockSpec((1, d), fixed), pl.BlockSpec((d, w2), fixed)]
    args += [mod, gain, w_in]
    outs = pl.pallas_call(
        functools.partial(_lru_in_kernel, mod_row=mod_row, has_prev=prev is not None),
        grid=(n // ROW_BLOCK,),
        in_specs=in_specs,
        out_specs=out_specs,
        out_shape=out_shape,
        compiler_params=_params(1, 48),
        name="lru_in",
    )(*args)
    return outs if prev is not None else [x] + list(outs)


def _scan8(a, b, row, reverse):
    for s in (1, 2, 4):
        if reverse:
            a_s = pltpu.roll(a, SUBLANES - s, 0)
            b_s = pltpu.roll(b, SUBLANES - s, 0)
            keep = row < SUBLANES - s
        else:
            a_s = pltpu.roll(a, s, 0)
            b_s = pltpu.roll(b, s, 0)
            keep = row >= s
        a_s = jnp.where(keep, a_s, 1.0)
        b_s = jnp.where(keep, b_s, 0.0)
        b = a * b_s + b
        a = a * a_s
    return a, b


def _lru_core_kernel(xr_ref, gate_ref, cw_ref, cb_ref, wg_ref, bias_ref, lam_ref, h0_ref,
                     y_ref, fin_ref, pad, a_f, b_f, a_b, b_b):
    t_len, lb = xr_ref.shape
    ch = 256
    sup = 64
    halo = SUBLANES

    zero_rows = jnp.zeros((halo, lb), F32)
    pad[0:halo, :] = zero_rows
    pad[t_len + halo:t_len + 2 * halo, :] = zero_rows

    def copy_in(c, carry):
        r = pl.multiple_of(c * ch, ch)
        pad[pl.ds(r + halo, ch), :] = xr_ref[pl.ds(r, ch), :]
        return carry

    lax.fori_loop(0, t_len // ch, copy_in, 0)

    neg_lam = -lam_ref[...]
    softplus = jnp.maximum(neg_lam, 0.0) + jnp.log1p(jnp.exp(-jnp.abs(neg_lam)))
    c_nla = (0.5 * RG_C) * softplus
    c_exp2 = (-0.5 * RG_C * math.log2(math.e)) * softplus
    cw = cw_ref[...]
    cbias = cb_ref[...]
    half_bias = bias_ref[...]
    half_wg = wg_ref[0]

    def gates(c, carry):
        r = pl.multiple_of(c * ch, ch)
        blk = pad[pl.ds(r, ch + 2 * halo), :]
        xc = cbias
        for k in range(CONV_W):
            o = halo - CONV_PAD_L + k
            xc = xc + blk[o:o + ch] * cw[k:k + 1]
        g = jnp.dot(xc.astype(BF16), half_wg, preferred_element_type=F32)
        half_xc = 0.5 * xc
        for d, (a_ref, b_ref) in enumerate(((a_f, b_f), (a_b, b_b))):
            u = jnp.tanh(g[:, (2 * d) * lb:(2 * d + 1) * lb] + half_bias[2 * d:2 * d + 1]) + 1.0
            t_i = jnp.tanh(g[:, (2 * d + 1) * lb:(2 * d + 2) * lb] + half_bias[2 * d + 1:2 * d + 2])
            a = jnp.exp2(u * c_exp2[d:d + 1])
            b = jnp.sqrt(jnp.tanh(u * c_nla[d:d + 1]) * (1.0 + a * a)) * (t_i * half_xc + half_xc)
            a_ref[pl.ds(r, ch), :] = a
            b_ref[pl.ds(r, ch), :] = b
        return carry

    lax.fori_loop(0, t_len // ch, gates, 0)

    row = lax.broadcasted_iota(I32, (SUBLANES, lb), 0)
    n_sup = t_len // sup
    per = sup // SUBLANES

    def bwd(ci, h):
        r = pl.multiple_of((n_sup - 1 - ci) * sup, sup)
        for j in reversed(range(per)):
            rows = pl.ds(r + SUBLANES * j, SUBLANES)
            a_c, b_c = _scan8(a_b[rows, :], b_b[rows, :], row, True)
            hs = a_c * h + b_c
            b_b[rows, :] = hs
            h = jnp.broadcast_to(hs[0:1], (SUBLANES, lb))
        return h

    h_b = lax.fori_loop(0, n_sup, bwd, jnp.broadcast_to(h0_ref[0, 1:2, :], (SUBLANES, lb)))

    def fwd(ci, h):
        r = pl.multiple_of(ci * sup, sup)
        for j in range(per):
            rows = pl.ds(r + SUBLANES * j, SUBLANES)
            a_c, b_c = _scan8(a_f[rows, :], b_f[rows, :], row, False)
            hs = a_c * h + b_c
            b_f[rows, :] = hs + b_b[rows, :]
            h = jnp.broadcast_to(hs[SUBLANES - 1:SUBLANES], (SUBLANES, lb))
        return h

    h_f = lax.fori_loop(0, n_sup, fwd, jnp.broadcast_to(h0_ref[0, 0:1, :], (SUBLANES, lb)))

    fin_ref[0, 0:1, :] = h_f[0:1]
    fin_ref[0, 1:2, :] = h_b[0:1]

    def emit(c, carry):
        r = pl.multiple_of(c * ch, ch)
        gt = gate_ref[pl.ds(r, ch), :]
        cdf = 0.5 * (1.0 + jnp.tanh(math.sqrt(2.0 / math.pi) * (gt + 0.044715 * (gt * gt * gt))))
        y_ref[pl.ds(r, ch), :] = (b_f[pl.ds(r, ch), :] * (gt * cdf)).astype(BF16)
        return carry

    lax.fori_loop(0, t_len // ch, emit, 0)


def _lru_core(xr, gate, conv_w, conv_b, wg, bias4, lam, h0, *, seq_len):
    n, w = xr.shape
    n_seq = n // seq_len
    lb = w // LRU_BLOCKS
    tok = lambda s, c: (s, c)
    chan = lambda s, c: (0, c)
    return pl.pallas_call(
        _lru_core_kernel,
        grid=(n_seq, LRU_BLOCKS),
        in_specs=[
            pl.BlockSpec((seq_len, lb), tok),
            pl.BlockSpec((seq_len, lb), tok),
            pl.BlockSpec((CONV_W, lb), chan),
            pl.BlockSpec((1, lb), chan),
            pl.BlockSpec((1, lb, 4 * lb), lambda s, c: (c, 0, 0)),
            pl.BlockSpec((4, lb), chan),
            pl.BlockSpec((2, lb), chan),
            pl.BlockSpec((1, 2, lb), lambda s, c: (s, 0, c)),
        ],
        out_specs=[pl.BlockSpec((seq_len, lb), tok),
                   pl.BlockSpec((1, 2, lb), lambda s, c: (s, 0, c))],
        out_shape=[jax.ShapeDtypeStruct((n, w), BF16), jax.ShapeDtypeStruct((n_seq, 2, w), F32)],
        scratch_shapes=[pltpu.VMEM((seq_len + 2 * SUBLANES, lb), F32)]
        + [pltpu.VMEM((seq_len, lb), F32)] * 4,
        compiler_params=_params(2, 40),
        name=f"lru_core_{seq_len}",
    )(xr, gate, conv_w, conv_b, wg, bias4, lam, h0)


def _mix_out_kernel(a_ref, x_ref, mod_ref, gain_ref, w_ref, wrh_ref, wrl_ref,
                    x1_ref, h2_ref, lg_ref, *, mod_row):
    d = x_ref.shape[1]
    m = mod_row(pl.program_id(0))
    g_mix = mod_ref[pl.ds(m, 1), pl.ds(2 * d, d)]
    shift = mod_ref[pl.ds(m, 1), pl.ds(3 * d, d)]
    scale = mod_ref[pl.ds(m, 1), pl.ds(4 * d, d)]
    op = jnp.dot(a_ref[...], w_ref[...], preferred_element_type=F32)
    x1 = x_ref[...] + g_mix * op
    x1_ref[...] = x1
    h2 = _modulated_norm(x1, gain_ref[...], shift, scale)
    h_hi, h_lo = _split_bf16(h2)
    h2_ref[...] = h2.astype(h2_ref.dtype)
    nt = functools.partial(lax.dot_general, dimension_numbers=(((1,), (1,)), ((), ())),
                           preferred_element_type=F32)
    lg_ref[...] = nt(wrh_ref[...], h_hi) + (nt(wrl_ref[...], h_hi) + nt(wrh_ref[...], h_lo))


def _mix_out(a, x, mod, gain, w_out, wr_hi, wr_lo, mod_row, h2_dtype):
    n, d = x.shape
    e = wr_hi.shape[0]
    row = lambda i: (i, 0)
    fixed = lambda i: (0, 0)
    return pl.pallas_call(
        functools.partial(_mix_out_kernel, mod_row=mod_row),
        grid=(n // ROW_BLOCK,),
        in_specs=[
            pl.BlockSpec((ROW_BLOCK, a.shape[1]), row),
            pl.BlockSpec((ROW_BLOCK, d), row),
            pl.BlockSpec(mod.shape, fixed),
            pl.BlockSpec((1, d), fixed),
            pl.BlockSpec(w_out.shape, fixed),
            pl.BlockSpec(wr_hi.shape, fixed),
            pl.BlockSpec(wr_lo.shape, fixed),
        ],
        out_specs=[pl.BlockSpec((ROW_BLOCK, d), row), pl.BlockSpec((ROW_BLOCK, d), row),
                   pl.BlockSpec((e, ROW_BLOCK), lambda i: (0, i))],
        out_shape=[jax.ShapeDtypeStruct((n, d), F32), jax.ShapeDtypeStruct((n, d), h2_dtype),
                   jax.ShapeDtypeStruct((e, n), F32)],
        compiler_params=_params(1, 48),
        name="mix_out",
    )(a, x, mod, gain, w_out, wr_hi, wr_lo)


def _lane_sum(tiles):
    acc = tiles[0]
    for t in tiles[1:]:
        acc = acc + t
    return jnp.sum(acc, axis=1, keepdims=True)


def _exclusive_cumsum(flags, upper):
    out = []
    off = jnp.zeros((flags[0].shape[0], 1), F32)
    for f in flags:
        out.append(jnp.dot(f.astype(BF16), upper, preferred_element_type=F32) + off)
        off = off + jnp.sum(f, axis=1, keepdims=True)
    return out


def _route_group(lg_ref, n_seq, t_len, idx_ref, g_ref):
    cap = CAPACITY_FACTOR * t_len // N_EXPERTS
    affs = []
    for s in range(n_seq):
        lg = lg_ref[:, s * t_len:(s + 1) * t_len]
        ex = jnp.exp(lg - jnp.max(lg, axis=0, keepdims=True))
        affs.append(ex / jnp.sum(ex, axis=0, keepdims=True))
    aff = jnp.concatenate(affs, axis=0)
    n_rows = aff.shape[0]
    nt = t_len // LANES
    g = [aff[:, j * LANES:(j + 1) * LANES] for j in range(nt)]

    kth_bits = jnp.zeros((n_rows, 1), I32)
    for bit in range(30, -1, -1):
        cand = kth_bits | (1 << bit)
        cand_f = pltpu.bitcast(cand, F32)
        cnt = _lane_sum([jnp.where(t >= cand_f, 1, 0) for t in g])
        kth_bits = jnp.where(cnt >= cap, cand, kth_bits)
    kth = pltpu.bitcast(kth_bits, F32)

    lane = lax.broadcasted_iota(I32, (LANES, LANES), 0)
    upper = jnp.where(lane < lax.broadcasted_iota(I32, (LANES, LANES), 1), 1.0, 0.0).astype(BF16)
    gt = [t > kth for t in g]
    eq = [t == kth for t in g]
    need = (cap - _lane_sum([jnp.where(m, 1, 0) for m in gt])).astype(F32)
    eq_rank = _exclusive_cumsum([jnp.where(m, 1.0, 0.0) for m in eq], upper)
    sel = [jnp.logical_or(gt[j], jnp.logical_and(eq[j], eq_rank[j] < need)) for j in range(nt)]
    pos = _exclusive_cumsum([jnp.where(m, 1.0, 0.0) for m in sel], upper)

    lane_r = lax.broadcasted_iota(I32, (n_rows, LANES), 1)
    d = [jnp.where(sel[j], lane_r + j * LANES - pos[j].astype(I32), -1) for j in range(nt)]
    for k in range(t_len.bit_length() - 1):
        s = 1 << k
        if s < LANES:
            d_rot = [pltpu.roll(x, LANES - s, 1) for x in d]
            g_rot = [pltpu.roll(x, LANES - s, 1) for x in g]
            same = lane_r < LANES - s
            d_in = [jnp.where(same, d_rot[j], d_rot[(j + 1) % nt]) for j in range(nt)]
            g_in = [jnp.where(same, g_rot[j], g_rot[(j + 1) % nt]) for j in range(nt)]
        else:
            q = s // LANES
            d_in = [d[(j + q) % nt] for j in range(nt)]
            g_in = [g[(j + q) % nt] for j in range(nt)]
        new_d, new_g = [], []
        for j in range(nt):
            move = jnp.logical_and(d_in[j] >= 0, ((d_in[j] >> k) & 1) == 1)
            stay = jnp.logical_and(d[j] >= 0, ((d[j] >> k) & 1) == 0)
            new_d.append(jnp.where(move, d_in[j], jnp.where(stay, d[j], -1)))
            new_g.append(jnp.where(move, g_in[j], g[j]))
        d, g = new_d, new_g

    for j in range(idx_ref.shape[1] // LANES):
        idx_ref[:, j * LANES:(j + 1) * LANES] = lane_r + j * LANES + d[j]
        g_ref[:, j * LANES:(j + 1) * LANES] = g[j]


def _route_kernel(lgp_ref, lgs_ref, idx_p_ref, g_p_ref, idx_s_ref, g_s_ref, *, p_len, s_len):
    _route_group(lgp_ref, lgp_ref.shape[1] // p_len, p_len, idx_p_ref, g_p_ref)
    _route_group(lgs_ref, lgs_ref.shape[1] // s_len, s_len, idx_s_ref, g_s_ref)


def _route(lg_p, lg_s, p_len, s_len):
    e = lg_p.shape[0]
    np_seq = lg_p.shape[1] // p_len
    ns_seq = lg_s.shape[1] // s_len
    cap_p = CAPACITY_FACTOR * p_len // N_EXPERTS
    cap_s = CAPACITY_FACTOR * s_len // N_EXPERTS
    wp = max(cap_p, LANES)
    ws = max(cap_s, LANES)
    idx_p, g_p, idx_s, g_s = pl.pallas_call(
        functools.partial(_route_kernel, p_len=p_len, s_len=s_len),
        out_shape=[jax.ShapeDtypeStruct((np_seq * e, wp), I32),
                   jax.ShapeDtypeStruct((np_seq * e, wp), F32),
                   jax.ShapeDtypeStruct((ns_seq * e, ws), I32),
                   jax.ShapeDtypeStruct((ns_seq * e, ws), F32)],
        compiler_params=pltpu.CompilerParams(vmem_limit_bytes=40 * MIB),
        name="route",
    )(lg_p, lg_s)
    return (idx_p[:, :cap_p].reshape(np_seq, e * cap_p), g_p[:, :cap_p].reshape(np_seq, e * cap_p),
            idx_s[:, :cap_s].reshape(-1), g_s[:, :cap_s].reshape(-1))


def _gather_onehot_kernel(idx_ref, h_ref, o_ref):
    t_len = h_ref.shape[0]
    n_exp, cap, _ = o_ref.shape
    idx = idx_ref[0]
    onehot = jnp.where(idx == lax.broadcasted_iota(I32, (idx.shape[0], t_len), 1), 1.0, 0.0).astype(BF16)
    xs = jnp.dot(onehot, h_ref[...], preferred_element_type=F32).astype(BF16)
    for e in range(n_exp):
        o_ref[e] = xs[e * cap:(e + 1) * cap]


def _gather_onehot(idx_col, h, *, seq_len):
    n, d = h.shape
    n_seq, slots, _ = idx_col.shape
    cap = slots // N_EXPERTS
    return pl.pallas_call(
        _gather_onehot_kernel,
        grid=(n_seq,),
        in_specs=[pl.BlockSpec((1, slots, 1), lambda s: (s, 0, 0)),
                  pl.BlockSpec((seq_len, d), lambda s: (s, 0))],
        out_specs=pl.BlockSpec((N_EXPERTS, cap, d), lambda s: (0, s, 0)),
        out_shape=jax.ShapeDtypeStruct((N_EXPERTS, n_seq * cap, d), BF16),
        compiler_params=_params(1, 32),
        name="gather_onehot",
    )(idx_col, h)


def _gather_rows_kernel(idx_ref, h_ref, o_ref, rows, *, cap, n_exp):
    base = (pl.program_id(0) * n_exp + pl.program_id(1)) * cap

    def body(c, carry):
        rows[pl.ds(c, 1), :] = h_ref[pl.ds(idx_ref[base + c], 1), :]
        return carry

    lax.fori_loop(0, cap, body, 0, unroll=8)
    o_ref[0] = rows[...].astype(BF16)


def _gather_rows(idx, h, *, seq_len):
    n, d = h.shape
    n_seq = n // seq_len
    cap = CAPACITY_FACTOR * seq_len // N_EXPERTS
    return pl.pallas_call(
        functools.partial(_gather_rows_kernel, cap=cap, n_exp=N_EXPERTS),
        grid_spec=pltpu.PrefetchScalarGridSpec(
            num_scalar_prefetch=1,
            grid=(n_seq, N_EXPERTS),
            in_specs=[pl.BlockSpec((seq_len, d), lambda s, e, idx: (s, 0))],
            out_specs=pl.BlockSpec((1, cap, d), lambda s, e, idx: (e, s, 0)),
            scratch_shapes=[pltpu.VMEM((cap, d), F32)],
        ),
        out_shape=jax.ShapeDtypeStruct((N_EXPERTS, n_seq * cap, d), BF16),
        compiler_params=_params(2, 48),
        name="gather_rows",
    )(idx, h)


def _ffn_kernel(xp_ref, xs_ref, wg_ref, wu_ref, wd_ref, yp_ref, ys_ref):
    rc = 512

    @pl.when(pl.program_id(1) == 0)
    def _():
        for y_ref in (yp_ref, ys_ref):
            for r in range(0, y_ref.shape[1], rc):
                y_ref[0, r:r + rc, :] = jnp.zeros((rc, y_ref.shape[2]), F32)

    wg = wg_ref[0, 0].astype(BF16)
    wu = wu_ref[0, 0].astype(BF16)
    wd = wd_ref[0, 0].astype(BF16)
    for x_ref, y_ref in ((xp_ref, yp_ref), (xs_ref, ys_ref)):
        for r in range(0, x_ref.shape[1], rc):
            x = x_ref[0, r:r + rc, :]
            hg = jnp.dot(x, wg, preferred_element_type=F32)
            hu = jnp.dot(x, wu, preferred_element_type=F32)
            hid = ((hg * jax.nn.sigmoid(hg)) * hu).astype(BF16)
            y_ref[0, r:r + rc, :] += jnp.dot(hid, wd, preferred_element_type=F32)


def _ffn(xs_p, xs_s, w_gate, w_up, w_down, layer):
    n_exp, rp, d = xs_p.shape
    rs = xs_s.shape[1]
    ff = w_gate.shape[3]
    fc = 512
    return pl.pallas_call(
        _ffn_kernel,
        grid=(n_exp, ff // fc),
        in_specs=[
            pl.BlockSpec((1, rp, d), lambda e, f: (e, 0, 0)),
            pl.BlockSpec((1, rs, d), lambda e, f: (e, 0, 0)),
            pl.BlockSpec((1, 1, d, fc), lambda e, f: (layer, e, 0, f)),
            pl.BlockSpec((1, 1, d, fc), lambda e, f: (layer, e, 0, f)),
            pl.BlockSpec((1, 1, fc, d), lambda e, f: (layer, e, f, 0)),
        ],
        out_specs=[pl.BlockSpec((1, rp, d), lambda e, f: (e, 0, 0)),
                   pl.BlockSpec((1, rs, d), lambda e, f: (e, 0, 0))],
        out_shape=[jax.ShapeDtypeStruct((n_exp, rp, d), F32),
                   jax.ShapeDtypeStruct((n_exp, rs, d), F32)],
        compiler_params=_params(2, 56),
        name="expert_ffn",
    )(xs_p, xs_s, w_gate, w_up, w_down)


def _combine_onehot_kernel(idx_ref, g_ref, ye_ref, o_ref):
    t_len = o_ref.shape[0]
    n_exp = ye_ref.shape[0]
    idx = idx_ref[0]
    hit = idx == lax.broadcasted_iota(I32, (t_len, idx.shape[1]), 0)
    g_hi, g_lo = _split_bf16(g_ref[0])
    m_hi = jnp.where(hit, g_hi.astype(F32), 0.0).astype(BF16)
    m_lo = jnp.where(hit, g_lo.astype(F32), 0.0).astype(BF16)
    y_hi, y_lo = _split_bf16(jnp.concatenate([ye_ref[e] for e in range(n_exp)], axis=0))
    dot = functools.partial(jnp.dot, preferred_element_type=F32)
    o_ref[...] = dot(m_hi, y_hi) + (dot(m_lo, y_hi) + dot(m_hi, y_lo))


def _combine_onehot(idx_row, g_row, ye, *, seq_len):
    n_exp, rows, d = ye.shape
    n_seq, _, slots = idx_row.shape
    cap = slots // n_exp
    return pl.pallas_call(
        _combine_onehot_kernel,
        grid=(n_seq,),
        in_specs=[pl.BlockSpec((1, 1, slots), lambda s: (s, 0, 0)),
                  pl.BlockSpec((1, 1, slots), lambda s: (s, 0, 0)),
                  pl.BlockSpec((n_exp, cap, d), lambda s: (0, s, 0))],
        out_specs=pl.BlockSpec((seq_len, d), lambda s: (s, 0)),
        out_shape=jax.ShapeDtypeStruct((n_seq * seq_len, d), F32),
        compiler_params=_params(1, 32),
        name="combine_onehot",
    )(idx_row, g_row, ye)


COMBINE_GROUP = 4


def _combine_rows_kernel(idx_ref, g_ref, ye_ref, o_ref, *, cap, n_exp):
    e = pl.program_id(1)
    t_len, d = o_ref.shape
    ch = 256

    @pl.when(e == 0)
    def _():
        def zero(c, carry):
            o_ref[pl.ds(pl.multiple_of(c * ch, ch), ch), :] = jnp.zeros((ch, d), F32)
            return carry
        lax.fori_loop(0, t_len // ch, zero, 0)

    base = (pl.program_id(0) * n_exp + e) * cap

    def body(i, carry):
        c0 = i * COMBINE_GROUP
        ts = [idx_ref[base + c0 + k] for k in range(COMBINE_GROUP)]
        vals = [o_ref[pl.ds(ts[k], 1), :] + g_ref[base + c0 + k] * ye_ref[0, pl.ds(c0 + k, 1), :]
                for k in range(COMBINE_GROUP)]
        for k in range(COMBINE_GROUP):
            o_ref[pl.ds(ts[k], 1), :] = vals[k]
        return carry

    lax.fori_loop(0, cap // COMBINE_GROUP, body, 0, unroll=2)


def _combine_rows(idx, g, ye, *, seq_len):
    n_exp, rows, d = ye.shape
    cap = CAPACITY_FACTOR * seq_len // N_EXPERTS
    n_seq = rows // cap
    return pl.pallas_call(
        functools.partial(_combine_rows_kernel, cap=cap, n_exp=n_exp),
        grid_spec=pltpu.PrefetchScalarGridSpec(
            num_scalar_prefetch=2,
            grid=(n_seq, n_exp),
            in_specs=[pl.BlockSpec((1, cap, d), lambda s, e, i, gg: (e, s, 0))],
            out_specs=pl.BlockSpec((seq_len, d), lambda s, e, i, gg: (s, 0)),
        ),
        out_shape=jax.ShapeDtypeStruct((n_seq * seq_len, d), F32),
        compiler_params=_params(2, 48),
        name="combine_rows",
    )(idx, g, ye)


def _head_norm(x, gain):
    return x * lax.rsqrt(jnp.mean(x * x, axis=-1, keepdims=True) + EPS) * gain


def _qkv_kernel(*refs, rope, has_prev, mod_row, n_heads, n_kv):
    refs = list(refs)
    x_ref = refs.pop(0)
    prev = None
    if has_prev:
        acc_ref, pmod_ref = refs.pop(0), refs.pop(0)
    mod_ref, gain_ref, w_ref, qg_ref, kg_ref = refs[:5]
    refs = refs[5:]
    if rope:
        cos_ref, sin_ref = refs.pop(0), refs.pop(0)
    if has_prev:
        prev = (acc_ref, pmod_ref, refs.pop(0))
    q_ref, k_ref, v_ref = refs[:3]
    d = x_ref.shape[1]
    hd = HEAD_DIM
    m = mod_row(pl.program_id(0))
    x = _residual_stream(x_ref, prev, m)
    shift = mod_ref[pl.ds(m, 1), pl.ds(0, d)]
    scale = mod_ref[pl.ds(m, 1), pl.ds(d, d)]
    h = _modulated_norm(x, gain_ref[...], shift, scale).astype(BF16)
    qkv = jnp.dot(h, w_ref[...], preferred_element_type=F32)

    if rope:
        cos = cos_ref[...]
        sin = sin_ref[...]
        lane = lax.broadcasted_iota(I32, cos.shape, 1)
        first_half = (lane % (hd // 2)) < hd // 4

        def rot(xh):
            partner = jnp.where(first_half, pltpu.roll(xh, hd - hd // 4, 1), pltpu.roll(xh, hd // 4, 1))
            return xh * cos + partner * sin
    else:
        rot = lambda xh: xh

    qg = qg_ref[...]
    kg = kg_ref[...]
    q_scale = math.log2(math.e) * hd ** -0.5
    for i in range(n_heads):
        qh = rot(_head_norm(qkv[:, i * hd:(i + 1) * hd], qg)) * q_scale
        q_ref[:, i * hd:(i + 1) * hd] = qh.astype(BF16)
    for i in range(n_kv):
        c0 = (n_heads + i) * hd
        kh = _head_norm(qkv[:, c0:c0 + hd], kg)
        if not rope:
            refs[3][:, i * hd:(i + 1) * hd] = kh
        k_ref[:, i * hd:(i + 1) * hd] = rot(kh).astype(BF16)
    v = qkv[:, (n_heads + n_kv) * hd:]
    v_ref[...] = v.astype(BF16)
    if not rope:
        refs[4][...] = v


def _qkv(x, prev, mod, gain, w_qkv, q_gain, k_gain, tables, *, seq_len, mod_row, n_heads, n_kv):
    n, d = x.shape
    hd = HEAD_DIM
    row = lambda i: (i, 0)
    fixed = lambda i: (0, 0)
    rope = tables is not None
    in_specs = [pl.BlockSpec((ROW_BLOCK, d), row)]
    args = [x]
    if prev is not None:
        in_specs += [pl.BlockSpec((ROW_BLOCK, d), row), pl.BlockSpec(prev[1].shape, fixed)]
        args += list(prev)
    in_specs += [
        pl.BlockSpec(mod.shape, fixed),
        pl.BlockSpec((1, d), fixed),
        pl.BlockSpec(w_qkv.shape, fixed),
        pl.BlockSpec((1, hd), fixed),
        pl.BlockSpec((1, hd), fixed),
    ]
    args += [mod, gain, w_qkv, q_gain, k_gain]
    if rope:
        per_seq = seq_len // ROW_BLOCK
        in_specs += [pl.BlockSpec((ROW_BLOCK, hd), lambda i: (i % per_seq, 0))] * 2
        args += list(tables)
    out_specs, out_shape = [], []
    if prev is not None:
        out_specs.append(pl.BlockSpec((ROW_BLOCK, d), row))
        out_shape.append(jax.ShapeDtypeStruct((n, d), F32))
    out_specs += [pl.BlockSpec((ROW_BLOCK, n_heads * hd), row),
                  pl.BlockSpec((ROW_BLOCK, n_kv * hd), row),
                  pl.BlockSpec((ROW_BLOCK, n_kv * hd), row)]
    out_shape += [jax.ShapeDtypeStruct((n, n_heads * hd), BF16),
                  jax.ShapeDtypeStruct((n, n_kv * hd), BF16),
                  jax.ShapeDtypeStruct((n, n_kv * hd), BF16)]
    if not rope:
        out_specs += [pl.BlockSpec((ROW_BLOCK, n_kv * hd), row)] * 2
        out_shape += [jax.ShapeDtypeStruct((n, n_kv * hd), F32)] * 2
    outs = pl.pallas_call(
        functools.partial(_qkv_kernel, rope=rope, has_prev=prev is not None, mod_row=mod_row,
                          n_heads=n_heads, n_kv=n_kv),
        grid=(n // ROW_BLOCK,),
        in_specs=in_specs,
        out_specs=out_specs,
        out_shape=out_shape,
        compiler_params=_params(1, 48),
        name="qkv_rope" if rope else "qkv",
    )(*args)
    return list(outs) if prev is not None else [x] + list(outs)


def _attn_kernel(*refs, group, chunk, has_cache):
    if has_cache:
        q_ref, k_ref, v_ref, kc_ref, vc_ref, o_ref = refs
    else:
        q_ref, k_ref, v_ref, o_ref = refs
    hd = HEAD_DIM
    qb = q_ref.shape[0]
    rows = group * qb
    q = jnp.concatenate([q_ref[:, g * hd:(g + 1) * hd] for g in range(group)], axis=0)
    sources = [(k_ref, v_ref, s0, chunk) for s0 in range(0, k_ref.shape[0], chunk)]
    if has_cache:
        sources.append((kc_ref.at[0], vc_ref.at[0], 0, kc_ref.shape[1]))
    m = jnp.full((rows, 1), -jnp.inf, F32)
    acc = jnp.zeros((rows, 2 * hd), F32)
    for kr, vr, s0, size in sources:
        s = lax.dot_general(q, kr[s0:s0 + size, :], (((1,), (1,)), ((), ())),
                            preferred_element_type=F32)
        m_new = jnp.maximum(m, jnp.max(s, axis=-1, keepdims=True))
        p = jnp.exp2(s - m_new).astype(BF16)
        v_ones = jnp.concatenate([vr[s0:s0 + size, :], jnp.ones((size, hd), BF16)], axis=1)
        acc = jnp.exp2(m - m_new) * acc + jnp.dot(p, v_ones, preferred_element_type=F32)
        m = m_new
    o = acc[:, :hd] / acc[:, hd:]
    o_ref[...] = jnp.concatenate([o[g * qb:(g + 1) * qb] for g in range(group)], axis=1).astype(BF16)


def _attention(q, k, v, cache, *, seq_len, qb, chunk, n_heads, n_kv):
    n = q.shape[0]
    hd = HEAD_DIM
    group = n_heads // n_kv
    nq = seq_len // qb
    in_specs = [
        pl.BlockSpec((qb, group * hd), lambda b, h, i: (b * nq + i, h)),
        pl.BlockSpec((seq_len, hd), lambda b, h, i: (b, h)),
        pl.BlockSpec((seq_len, hd), lambda b, h, i: (b, h)),
    ]
    args = [q, k, v]
    if cache is not None:
        past = cache[0].shape[1]
        in_specs += [pl.BlockSpec((1, past, hd), lambda b, h, i: (b, 0, h))] * 2
        args += list(cache)
    return pl.pallas_call(
        functools.partial(_attn_kernel, group=group, chunk=chunk, has_cache=cache is not None),
        grid=(n // seq_len, n_kv, nq),
        in_specs=in_specs,
        out_specs=pl.BlockSpec((qb, group * hd), lambda b, h, i: (b * nq + i, h)),
        out_shape=jax.ShapeDtypeStruct((n, n_heads * hd), BF16),
        compiler_params=_params(3, 48),
        name=f"attention_{seq_len}",
    )(*args)


def _final_norm_kernel(x_ref, acc_ref, pmod_ref, gain_ref, o_ref, *, mod_row):
    d = x_ref.shape[1]
    m = mod_row(pl.program_id(0))
    x = x_ref[...] + pmod_ref[pl.ds(m, 1), pl.ds(5 * d, d)] * acc_ref[...]
    o_ref[...] = (x * lax.rsqrt(jnp.mean(x * x, axis=-1, keepdims=True) + EPS)) * gain_ref[...]


def _final_norm(x, acc, pmod, gain, mod_row):
    n, d = x.shape
    row = lambda i: (i, 0)
    fixed = lambda i: (0, 0)
    return pl.pallas_call(
        functools.partial(_final_norm_kernel, mod_row=mod_row),
        grid=(n // ROW_BLOCK,),
        in_specs=[pl.BlockSpec((ROW_BLOCK, d), row), pl.BlockSpec((ROW_BLOCK, d), row),
                  pl.BlockSpec(pmod.shape, fixed), pl.BlockSpec((1, d), fixed)],
        out_specs=pl.BlockSpec((ROW_BLOCK, d), row),
        out_shape=jax.ShapeDtypeStruct((n, d), F32),
        compiler_params=_params(1, 32),
        name="final_norm",
    )(x, acc, pmod, gain)


def _rope_tables(seq_len):
    hd = HEAD_DIM
    axis = hd // 2
    t = jnp.arange(seq_len)
    inv = ROPE_BASE ** (-jnp.arange(axis // 2, dtype=F32) * 2.0 / axis)

    def half(pos):
        ang = pos.astype(F32)[:, None] * inv
        c, s = jnp.cos(ang), jnp.sin(ang)
        return jnp.concatenate([c, c], axis=1), jnp.concatenate([-s, s], axis=1)

    c_row, s_row = half(t // GRID_W)
    c_col, s_col = half(t % GRID_W)
    return jnp.concatenate([c_row, c_col], axis=1), jnp.concatenate([s_row, s_col], axis=1)


def kernel(x_prompt, x_sample, state_lru, cache_k, cache_v, c, c_ctx, w_mod, b_mod, norm_gain, final_gain,
           w_lru_in, lru_conv_w, lru_conv_b, lru_wa, lru_ba, lru_wx, lru_bx, lru_lambda, w_lru_out,
           w_qkv, q_norm, k_norm, w_attn_out, w_router, w_exp_gate, w_exp_up, w_exp_down):
    batch, seq, d = x_prompt.shape
    dec_batch, dec_seq, _ = x_sample.shape
    depth = w_mod.shape[0]
    n_kv = cache_k.shape[3]
    n_heads = w_attn_out.shape[1] // HEAD_DIM
    heads = dict(n_heads=n_heads, n_kv=n_kv)

    cond = jnp.zeros((SUBLANES, d), F32).at[0].set(c_ctx).at[1:1 + dec_batch].set(c)
    mod_all = _mod_vectors(cond, w_mod, b_mod)

    blocks_per_seq = dec_seq // ROW_BLOCK
    xs = [x_prompt.reshape(batch * seq, d), x_sample.reshape(dec_batch * dec_seq, d)]
    lens = [seq, dec_seq]
    mod_rows = [lambda i: 0, lambda i: 1 + i // blocks_per_seq]
    prevs = [None, None]

    new_lru, new_k, new_v = [], [], []
    for l in range(depth):
        mod = mod_all[l]
        gain_a = norm_gain[l, 0].reshape(1, d)
        gain_c = norm_gain[l, 1].reshape(1, d)
        mixed = []
        if l % 2 == 0:
            li = l // 2
            w_in = w_lru_in[li].astype(BF16)
            wg = (0.5 * jnp.concatenate([lru_wa[li, 0], lru_wx[li, 0], lru_wa[li, 1], lru_wx[li, 1]],
                                        axis=2)).astype(BF16)
            bias4 = 0.5 * jnp.stack([lru_ba[li, 0], lru_bx[li, 0], lru_ba[li, 1], lru_bx[li, 1]])
            h0s = [jnp.zeros((batch, 2, w_in.shape[1] // 2), F32), state_lru[:, li]]
            for gi in range(2):
                xs[gi], gate, xr = _lru_in(xs[gi], prevs[gi], mod, gain_a, w_in, mod_rows[gi])
                y, fin = _lru_core(xr, gate, lru_conv_w[li], lru_conv_b[li].reshape(1, -1), wg, bias4,
                                   lru_lambda[li], h0s[gi], seq_len=lens[gi])
                mixed.append(y)
                if gi == 0:
                    new_lru.append(fin.astype(x_prompt.dtype))
            w_out = w_lru_out[li].astype(BF16)
        else:
            ai = l // 2
            w = w_qkv[ai].astype(BF16)
            qg = q_norm[ai].reshape(1, -1)
            kg = k_norm[ai].reshape(1, -1)
            xs[0], q_p, k_p, v_p, kf, vf = _qkv(xs[0], prevs[0], mod, gain_a, w, qg, kg, None,
                                                seq_len=seq, mod_row=mod_rows[0], **heads)
            xs[1], q_s, k_s, v_s = _qkv(xs[1], prevs[1], mod, gain_a, w, qg, kg, _rope_tables(dec_seq),
                                        seq_len=dec_seq, mod_row=mod_rows[1], **heads)
            new_k.append(kf.reshape(batch, seq, n_kv, HEAD_DIM))
            new_v.append(vf.reshape(batch, seq, n_kv, HEAD_DIM))
            past = cache_k.shape[2]
            kc = cache_k[:, ai].reshape(dec_batch, past, n_kv * HEAD_DIM).astype(BF16)
            vc = cache_v[:, ai].reshape(dec_batch, past, n_kv * HEAD_DIM).astype(BF16)
            mixed.append(_attention(q_p, k_p, v_p, None, seq_len=seq, qb=seq, chunk=seq, **heads))
            mixed.append(_attention(q_s, k_s, v_s, (kc, vc), seq_len=dec_seq, qb=128, chunk=1024, **heads))
            w_out = w_attn_out[ai].astype(BF16)

        wr_hi, wr_lo = _split_bf16(w_router[l].T)
        x1_p, h2_p, lg_p = _mix_out(mixed[0], xs[0], mod, gain_c, w_out, wr_hi, wr_lo, mod_rows[0], BF16)
        x1_s, h2_s, lg_s = _mix_out(mixed[1], xs[1], mod, gain_c, w_out, wr_hi, wr_lo, mod_rows[1], F32)
        idx_p, g_p, idx_s, g_s = _route(lg_p, lg_s, seq, dec_seq)
        xe_p = _gather_onehot(idx_p[:, :, None], h2_p, seq_len=seq)
        xe_s = _gather_rows(idx_s, h2_s, seq_len=dec_seq)
        ye_p, ye_s = _ffn(xe_p, xe_s, w_exp_gate, w_exp_up, w_exp_down, l)
        acc_p = _combine_onehot(idx_p[:, None, :], g_p[:, None, :], ye_p, seq_len=seq)
        acc_s = _combine_rows(idx_s, g_s, ye_s, seq_len=dec_seq)
        xs = [x1_p, x1_s]
        prevs = [(acc_p, mod), (acc_s, mod)]

    fg = final_gain.reshape(1, d)
    y_prompt = _final_norm(xs[0], *prevs[0], fg, mod_rows[0]).reshape(batch, seq, d)
    y_sample = _final_norm(xs[1], *prevs[1], fg, mod_rows[1]).reshape(dec_batch, dec_seq, d)
    return (y_prompt, y_sample, jnp.stack(new_lru, axis=1), jnp.stack(new_k, axis=1), jnp.stack(new_v, axis=1))
```

```python
import functools
import math

import jax
import jax.numpy as jnp
from jax import lax
from jax.experimental import pallas as pl
from jax.experimental.pallas import tpu as pltpu

F32 = jnp.float32
BF16 = jnp.bfloat16
I32 = jnp.int32

LANES = 128
SUBLANES = 8
MIB = 1024 * 1024

RG_C = 8.0
CONV_W = 4
CONV_PAD_L = 2
ROPE_BASE = 10000.0
GRID_W = 64
EPS = 1e-6
N_EXPERTS = 16
CAPACITY_FACTOR = 2
HEAD_DIM = 128
LRU_BLOCKS = 8

ROW_BLOCK = 512


def _params(n_axes, vmem_mib):
    return pltpu.CompilerParams(
        dimension_semantics=("arbitrary",) * n_axes, vmem_limit_bytes=vmem_mib * MIB)


def _split_bf16(x):
    hi = x.astype(BF16)
    lo = (x - hi.astype(F32)).astype(BF16)
    return hi, lo


def _modulated_norm(x, gain, shift, scale):
    y = x * lax.rsqrt(jnp.mean(x * x, axis=-1, keepdims=True) + EPS)
    return (y * gain) * (1.0 + scale) + shift


def _tiles_per_row(d):
    return d // LANES


def _load_rows(ref, n_rows, d):
    if ref.shape == (n_rows, d):
        return ref[...]
    tpr = _tiles_per_row(d)
    return jnp.concatenate([ref[pl.ds(s, n_rows, stride=tpr), :] for s in range(tpr)], axis=1)


def _store_rows(ref, x):
    n_rows, d = x.shape
    if ref.shape == (n_rows, d):
        ref[...] = x.astype(ref.dtype)
        return
    tpr = _tiles_per_row(d)
    for s in range(tpr):
        ref[pl.ds(s, n_rows, stride=tpr), :] = x[:, s * LANES:(s + 1) * LANES]


def _rows_spec(arr, d):
    if arr.shape[1] == d:
        return pl.BlockSpec((ROW_BLOCK, d), lambda i: (i, 0))
    return pl.BlockSpec((ROW_BLOCK * _tiles_per_row(d), LANES), lambda i: (i, 0))


def _residual_stream(x_ref, prev, m):
    if prev is None:
        return x_ref[...]
    acc_ref, pmod_ref, xo_ref = prev
    n_rows, d = x_ref.shape
    x = x_ref[...] + pmod_ref[pl.ds(m, 1), pl.ds(5 * d, d)] * _load_rows(acc_ref, n_rows, d)
    xo_ref[...] = x
    return x


def _mod_kernel(cond_ref, w_ref, b_ref, o_ref):
    c = cond_ref[...]
    a_hi, a_lo = _split_bf16(c * jax.nn.sigmoid(c))
    w_hi, w_lo = _split_bf16(w_ref[0])
    dot = functools.partial(jnp.dot, preferred_element_type=F32)
    o_ref[0] = dot(a_hi, w_hi) + (dot(a_lo, w_hi) + dot(a_hi, w_lo)) + b_ref[0]


def _mod_vectors(cond8, w_mod, b_mod):
    depth, d, d6 = w_mod.shape
    nb = 1536
    return pl.pallas_call(
        _mod_kernel,
        grid=(depth, d6 // nb),
        in_specs=[
            pl.BlockSpec((SUBLANES, d), lambda l, j: (0, 0)),
            pl.BlockSpec((1, d, nb), lambda l, j: (l, 0, j)),
            pl.BlockSpec((1, 1, nb), lambda l, j: (l, 0, j)),
        ],
        out_specs=pl.BlockSpec((1, SUBLANES, nb), lambda l, j: (l, 0, j)),
        out_shape=jax.ShapeDtypeStruct((depth, SUBLANES, d6), F32),
        compiler_params=_params(2, 40),
        name="adaln_mod",
    )(cond8, w_mod, b_mod.reshape(depth, 1, d6))


def _lru_in_kernel(*refs, mod_row, has_prev):
    if has_prev:
        x_ref, acc_ref, pmod_ref, mod_ref, gain_ref, w_ref, xo_ref, gate_ref, xr_ref = refs
        prev = (acc_ref, pmod_ref, xo_ref)
    else:
        x_ref, mod_ref, gain_ref, w_ref, gate_ref, xr_ref = refs
        prev = None
    d = x_ref.shape[1]
    m = mod_row(pl.program_id(0))
    x = _residual_stream(x_ref, prev, m)
    shift = mod_ref[pl.ds(m, 1), pl.ds(0, d)]
    scale = mod_ref[pl.ds(m, 1), pl.ds(d, d)]
    h = _modulated_norm(x, gain_ref[...], shift, scale).astype(BF16)
    u = jnp.dot(h, w_ref[...], preferred_element_type=F32)
    w = gate_ref.shape[1]
    gate_ref[...] = u[:, :w]
    xr_ref[...] = u[:, w:]


def _lru_in(x, prev, mod, gain, w_in, mod_row):
    n, d = x.shape
    w2 = w_in.shape[1]
    w = w2 // 2
    row = lambda i: (i, 0)
    fixed = lambda i: (0, 0)
    in_specs = [pl.BlockSpec((ROW_BLOCK, d), row)]
    args = [x]
    out_specs = [pl.BlockSpec((ROW_BLOCK, w), row), pl.BlockSpec((ROW_BLOCK, w), row)]
    out_shape = [jax.ShapeDtypeStruct((n, w), F32), jax.ShapeDtypeStruct((n, w), F32)]
    if prev is not None:
        in_specs += [_rows_spec(prev[0], d), pl.BlockSpec(prev[1].shape, fixed)]
        args += list(prev)
        out_specs = [pl.BlockSpec((ROW_BLOCK, d), row)] + out_specs
        out_shape = [jax.ShapeDtypeStruct((n, d), F32)] + out_shape
    in_specs += [pl.BlockSpec(mod.shape, fixed), pl.BlockSpec((1, d), fixed), pl.BlockSpec((d, w2), fixed)]
    args += [mod, gain, w_in]
    outs = pl.pallas_call(
        functools.partial(_lru_in_kernel, mod_row=mod_row, has_prev=prev is not None),
        grid=(n // ROW_BLOCK,),
        in_specs=in_specs,
        out_specs=out_specs,
        out_shape=out_shape,
        compiler_params=_params(1, 48),
        name="lru_in",
    )(*args)
    return outs if prev is not None else [x] + list(outs)


def _scan8(a, b, row, reverse):
    for s in (1, 2, 4):
        if reverse:
            a_s = pltpu.roll(a, SUBLANES - s, 0)
            b_s = pltpu.roll(b, SUBLANES - s, 0)
            keep = row < SUBLANES - s
        else:
            a_s = pltpu.roll(a, s, 0)
            b_s = pltpu.roll(b, s, 0)
            keep = row >= s
        a_s = jnp.where(keep, a_s, 1.0)
        b_s = jnp.where(keep, b_s, 0.0)
        b = a * b_s + b
        a = a * a_s
    return a, b


def _lru_core_kernel(xr_ref, gate_ref, cw_ref, cb_ref, wg_ref, bias_ref, lam_ref, h0_ref,
                     y_ref, fin_ref, pad, a_f, b_f, a_b, b_b):
    t_len, lb = xr_ref.shape
    ch = 256
    sup = 64
    halo = SUBLANES

    zero_rows = jnp.zeros((halo, lb), F32)
    pad[0:halo, :] = zero_rows
    pad[t_len + halo:t_len + 2 * halo, :] = zero_rows

    def copy_in(c, carry):
        r = pl.multiple_of(c * ch, ch)
        pad[pl.ds(r + halo, ch), :] = xr_ref[pl.ds(r, ch), :]
        return carry

    lax.fori_loop(0, t_len // ch, copy_in, 0)

    neg_lam = -lam_ref[...]
    softplus = jnp.maximum(neg_lam, 0.0) + jnp.log1p(jnp.exp(-jnp.abs(neg_lam)))
    c_nla = (0.5 * RG_C) * softplus
    c_exp2 = (-0.5 * RG_C * math.log2(math.e)) * softplus
    cw = cw_ref[...]
    cbias = cb_ref[...]
    half_bias = bias_ref[...]
    half_wg = wg_ref[0]

    def gates(c, carry):
        r = pl.multiple_of(c * ch, ch)
        blk = pad[pl.ds(r, ch + 2 * halo), :]
        xc = cbias
        for k in range(CONV_W):
            o = halo - CONV_PAD_L + k
            xc = xc + blk[o:o + ch] * cw[k:k + 1]
        g = jnp.dot(xc.astype(BF16), half_wg, preferred_element_type=F32)
        half_xc = 0.5 * xc
        for d, (a_ref, b_ref) in enumerate(((a_f, b_f), (a_b, b_b))):
            u = jnp.tanh(g[:, (2 * d) * lb:(2 * d + 1) * lb] + half_bias[2 * d:2 * d + 1]) + 1.0
            t_i = jnp.tanh(g[:, (2 * d + 1) * lb:(2 * d + 2) * lb] + half_bias[2 * d + 1:2 * d + 2])
            a = jnp.exp2(u * c_exp2[d:d + 1])
            b = jnp.sqrt(jnp.tanh(u * c_nla[d:d + 1]) * (1.0 + a * a)) * (t_i * half_xc + half_xc)
            a_ref[pl.ds(r, ch), :] = a
            b_ref[pl.ds(r, ch), :] = b
        return carry

    lax.fori_loop(0, t_len // ch, gates, 0)

    row = lax.broadcasted_iota(I32, (SUBLANES, lb), 0)
    n_sup = t_len // sup
    per = sup // SUBLANES

    def bwd(ci, h):
        r = pl.multiple_of((n_sup - 1 - ci) * sup, sup)
        for j in reversed(range(per)):
            rows = pl.ds(r + SUBLANES * j, SUBLANES)
            a_c, b_c = _scan8(a_b[rows, :], b_b[rows, :], row, True)
            hs = a_c * h + b_c
            b_b[rows, :] = hs
            h = jnp.broadcast_to(hs[0:1], (SUBLANES, lb))
        return h

    h_b = lax.fori_loop(0, n_sup, bwd, jnp.broadcast_to(h0_ref[0, 1:2, :], (SUBLANES, lb)))

    def fwd(ci, h):
        r = pl.multiple_of(ci * sup, sup)
        for j in range(per):
            rows = pl.ds(r + SUBLANES * j, SUBLANES)
            a_c, b_c = _scan8(a_f[rows, :], b_f[rows, :], row, False)
            hs = a_c * h + b_c
            b_f[rows, :] = hs + b_b[rows, :]
            h = jnp.broadcast_to(hs[SUBLANES - 1:SUBLANES], (SUBLANES, lb))
        return h

    h_f = lax.fori_loop(0, n_sup, fwd, jnp.broadcast_to(h0_ref[0, 0:1, :], (SUBLANES, lb)))

    fin_ref[0, 0:1, :] = h_f[0:1]
    fin_ref[0, 1:2, :] = h_b[0:1]

    def emit(c, carry):
        r = pl.multiple_of(c * ch, ch)
        gt = gate_ref[pl.ds(r, ch), :]
        cdf = 0.5 * (1.0 + jnp.tanh(math.sqrt(2.0 / math.pi) * (gt + 0.044715 * (gt * gt * gt))))
        y_ref[pl.ds(r, ch), :] = (b_f[pl.ds(r, ch), :] * (gt * cdf)).astype(BF16)
        return carry

    lax.fori_loop(0, t_len // ch, emit, 0)


def _lru_core(xr, gate, conv_w, conv_b, wg, bias4, lam, h0, *, seq_len):
    n, w = xr.shape
    n_seq = n // seq_len
    lb = w // LRU_BLOCKS
    tok = lambda s, c: (s, c)
    chan = lambda s, c: (0, c)
    return pl.pallas_call(
        _lru_core_kernel,
        grid=(n_seq, LRU_BLOCKS),
        in_specs=[
            pl.BlockSpec((seq_len, lb), tok),
            pl.BlockSpec((seq_len, lb), tok),
            pl.BlockSpec((CONV_W, lb), chan),
            pl.BlockSpec((1, lb), chan),
            pl.BlockSpec((1, lb, 4 * lb), lambda s, c: (c, 0, 0)),
            pl.BlockSpec((4, lb), chan),
            pl.BlockSpec((2, lb), chan),
            pl.BlockSpec((1, 2, lb), lambda s, c: (s, 0, c)),
        ],
        out_specs=[pl.BlockSpec((seq_len, lb), tok),
                   pl.BlockSpec((1, 2, lb), lambda s, c: (s, 0, c))],
        out_shape=[jax.ShapeDtypeStruct((n, w), BF16), jax.ShapeDtypeStruct((n_seq, 2, w), F32)],
        scratch_shapes=[pltpu.VMEM((seq_len + 2 * SUBLANES, lb), F32)]
        + [pltpu.VMEM((seq_len, lb), F32)] * 4,
        compiler_params=_params(2, 40),
        name=f"lru_core_{seq_len}",
    )(xr, gate, conv_w, conv_b, wg, bias4, lam, h0)


def _mix_out_kernel(a_ref, x_ref, mod_ref, gain_ref, w_ref, wrh_ref, wrl_ref,
                    x1_ref, h2_ref, lg_ref, *, mod_row):
    d = x_ref.shape[1]
    m = mod_row(pl.program_id(0))
    g_mix = mod_ref[pl.ds(m, 1), pl.ds(2 * d, d)]
    shift = mod_ref[pl.ds(m, 1), pl.ds(3 * d, d)]
    scale = mod_ref[pl.ds(m, 1), pl.ds(4 * d, d)]
    op = jnp.dot(a_ref[...], w_ref[...], preferred_element_type=F32)
    x1 = x_ref[...] + g_mix * op
    x1_ref[...] = x1
    h2 = _modulated_norm(x1, gain_ref[...], shift, scale)
    h_hi, h_lo = _split_bf16(h2)
    _store_rows(h2_ref, h2)
    nt = functools.partial(lax.dot_general, dimension_numbers=(((1,), (1,)), ((), ())),
                           preferred_element_type=F32)
    lg_ref[...] = nt(wrh_ref[...], h_hi) + (nt(wrl_ref[...], h_hi) + nt(wrh_ref[...], h_lo))


def _mix_out(a, x, mod, gain, w_out, wr_hi, wr_lo, mod_row, token_tiled):
    n, d = x.shape
    e = wr_hi.shape[0]
    row = lambda i: (i, 0)
    fixed = lambda i: (0, 0)
    tpr = _tiles_per_row(d)
    if token_tiled:
        h2_spec = pl.BlockSpec((ROW_BLOCK * tpr, LANES), row)
        h2_shape = jax.ShapeDtypeStruct((n * tpr, LANES), F32)
    else:
        h2_spec = pl.BlockSpec((ROW_BLOCK, d), row)
        h2_shape = jax.ShapeDtypeStruct((n, d), BF16)
    return pl.pallas_call(
        functools.partial(_mix_out_kernel, mod_row=mod_row),
        grid=(n // ROW_BLOCK,),
        in_specs=[
            pl.BlockSpec((ROW_BLOCK, a.shape[1]), row),
            pl.BlockSpec((ROW_BLOCK, d), row),
            pl.BlockSpec(mod.shape, fixed),
            pl.BlockSpec((1, d), fixed),
            pl.BlockSpec(w_out.shape, fixed),
            pl.BlockSpec(wr_hi.shape, fixed),
            pl.BlockSpec(wr_lo.shape, fixed),
        ],
        out_specs=[pl.BlockSpec((ROW_BLOCK, d), row), h2_spec,
                   pl.BlockSpec((e, ROW_BLOCK), lambda i: (0, i))],
        out_shape=[jax.ShapeDtypeStruct((n, d), F32), h2_shape, jax.ShapeDtypeStruct((e, n), F32)],
        compiler_params=_params(1, 48),
        name="mix_out",
    )(a, x, mod, gain, w_out, wr_hi, wr_lo)


def _lane_sum(tiles):
    acc = tiles[0]
    for t in tiles[1:]:
        acc = acc + t
    return jnp.sum(acc, axis=1, keepdims=True)


def _exclusive_cumsum(flags, upper):
    out = []
    off = jnp.zeros((flags[0].shape[0], 1), F32)
    for f in flags:
        out.append(jnp.dot(f.astype(BF16), upper, preferred_element_type=F32) + off)
        off = off + jnp.sum(f, axis=1, keepdims=True)
    return out


def _route_group(lg_ref, n_seq, t_len, idx_ref, g_ref):
    cap = CAPACITY_FACTOR * t_len // N_EXPERTS
    affs = []
    for s in range(n_seq):
        lg = lg_ref[:, s * t_len:(s + 1) * t_len]
        ex = jnp.exp(lg - jnp.max(lg, axis=0, keepdims=True))
        affs.append(ex / jnp.sum(ex, axis=0, keepdims=True))
    aff = jnp.concatenate(affs, axis=0)
    n_rows = aff.shape[0]
    nt = t_len // LANES
    g = [aff[:, j * LANES:(j + 1) * LANES] for j in range(nt)]

    kth_bits = jnp.zeros((n_rows, 1), I32)
    for bit in range(30, -1, -1):
        cand = kth_bits | (1 << bit)
        cand_f = pltpu.bitcast(cand, F32)
        cnt = _lane_sum([jnp.where(t >= cand_f, 1, 0) for t in g])
        kth_bits = jnp.where(cnt >= cap, cand, kth_bits)
    kth = pltpu.bitcast(kth_bits, F32)

    lane = lax.broadcasted_iota(I32, (LANES, LANES), 0)
    upper = jnp.where(lane < lax.broadcasted_iota(I32, (LANES, LANES), 1), 1.0, 0.0).astype(BF16)
    gt = [t > kth for t in g]
    eq = [t == kth for t in g]
    need = (cap - _lane_sum([jnp.where(m, 1, 0) for m in gt])).astype(F32)
    eq_rank = _exclusive_cumsum([jnp.where(m, 1.0, 0.0) for m in eq], upper)
    sel = [jnp.logical_or(gt[j], jnp.logical_and(eq[j], eq_rank[j] < need)) for j in range(nt)]
    pos = _exclusive_cumsum([jnp.where(m, 1.0, 0.0) for m in sel], upper)

    lane_r = lax.broadcasted_iota(I32, (n_rows, LANES), 1)
    d = [jnp.where(sel[j], lane_r + j * LANES - pos[j].astype(I32), -1) for j in range(nt)]
    for k in range(t_len.bit_length() - 1):
        s = 1 << k
        if s < LANES:
            d_rot = [pltpu.roll(x, LANES - s, 1) for x in d]
            g_rot = [pltpu.roll(x, LANES - s, 1) for x in g]
            same = lane_r < LANES - s
            d_in = [jnp.where(same, d_rot[j], d_rot[(j + 1) % nt]) for j in range(nt)]
            g_in = [jnp.where(same, g_rot[j], g_rot[(j + 1) % nt]) for j in range(nt)]
        else:
            q = s // LANES
            d_in = [d[(j + q) % nt] for j in range(nt)]
            g_in = [g[(j + q) % nt] for j in range(nt)]
        new_d, new_g = [], []
        for j in range(nt):
            move = jnp.logical_and(d_in[j] >= 0, ((d_in[j] >> k) & 1) == 1)
            stay = jnp.logical_and(d[j] >= 0, ((d[j] >> k) & 1) == 0)
            new_d.append(jnp.where(move, d_in[j], jnp.where(stay, d[j], -1)))
            new_g.append(jnp.where(move, g_in[j], g[j]))
        d, g = new_d, new_g

    for j in range(idx_ref.shape[1] // LANES):
        idx_ref[:, j * LANES:(j + 1) * LANES] = lane_r + j * LANES + d[j]
        g_ref[:, j * LANES:(j + 1) * LANES] = g[j]


def _route_kernel(lgp_ref, lgs_ref, idx_p_ref, g_p_ref, idx_s_ref, g_s_ref, *, p_len, s_len):
    _route_group(lgp_ref, lgp_ref.shape[1] // p_len, p_len, idx_p_ref, g_p_ref)
    _route_group(lgs_ref, lgs_ref.shape[1] // s_len, s_len, idx_s_ref, g_s_ref)


def _route(lg_p, lg_s, p_len, s_len):
    e = lg_p.shape[0]
    np_seq = lg_p.shape[1] // p_len
    ns_seq = lg_s.shape[1] // s_len
    cap_p = CAPACITY_FACTOR * p_len // N_EXPERTS
    cap_s = CAPACITY_FACTOR * s_len // N_EXPERTS
    wp = max(cap_p, LANES)
    ws = max(cap_s, LANES)
    idx_p, g_p, idx_s, g_s = pl.pallas_call(
        functools.partial(_route_kernel, p_len=p_len, s_len=s_len),
        out_shape=[jax.ShapeDtypeStruct((np_seq * e, wp), I32),
                   jax.ShapeDtypeStruct((np_seq * e, wp), F32),
                   jax.ShapeDtypeStruct((ns_seq * e, ws), I32),
                   jax.ShapeDtypeStruct((ns_seq * e, ws), F32)],
        compiler_params=pltpu.CompilerParams(vmem_limit_bytes=40 * MIB),
        name="route",
    )(lg_p, lg_s)
    return (idx_p[:, :cap_p].reshape(np_seq, e * cap_p), g_p[:, :cap_p].reshape(np_seq, e * cap_p),
            idx_s[:, :cap_s].reshape(-1), g_s[:, :cap_s].reshape(-1))


def _gather_onehot_kernel(idx_ref, h_ref, o_ref):
    t_len = h_ref.shape[0]
    n_exp, cap, _ = o_ref.shape
    idx = idx_ref[0]
    onehot = jnp.where(idx == lax.broadcasted_iota(I32, (idx.shape[0], t_len), 1), 1.0, 0.0).astype(BF16)
    xs = jnp.dot(onehot, h_ref[...], preferred_element_type=F32).astype(BF16)
    for e in range(n_exp):
        o_ref[e] = xs[e * cap:(e + 1) * cap]


def _gather_onehot(idx_col, h, *, seq_len):
    n, d = h.shape
    n_seq, slots, _ = idx_col.shape
    cap = slots // N_EXPERTS
    return pl.pallas_call(
        _gather_onehot_kernel,
        grid=(n_seq,),
        in_specs=[pl.BlockSpec((1, slots, 1), lambda s: (s, 0, 0)),
                  pl.BlockSpec((seq_len, d), lambda s: (s, 0))],
        out_specs=pl.BlockSpec((N_EXPERTS, cap, d), lambda s: (0, s, 0)),
        out_shape=jax.ShapeDtypeStruct((N_EXPERTS, n_seq * cap, d), BF16),
        compiler_params=_params(1, 32),
        name="gather_onehot",
    )(idx_col, h)


def _gather_rows_kernel(idx_ref, h_ref, o_ref, tiles, *, cap, n_exp):
    tpr = _tiles_per_row(o_ref.shape[2])
    base = (pl.program_id(0) * n_exp + pl.program_id(1)) * cap

    def body(c, carry):
        src = pl.multiple_of(idx_ref[base + c] * tpr, tpr)
        tiles[pl.ds(pl.multiple_of(c * tpr, tpr), tpr), :] = h_ref[pl.ds(src, tpr), :]
        return carry

    lax.fori_loop(0, cap, body, 0, unroll=8)
    for s in range(tpr):
        o_ref[0, :, s * LANES:(s + 1) * LANES] = tiles[pl.ds(s, cap, stride=tpr), :].astype(BF16)


def _gather_rows(idx, h, d, *, seq_len):
    tpr = _tiles_per_row(d)
    n_seq = h.shape[0] // (seq_len * tpr)
    cap = CAPACITY_FACTOR * seq_len // N_EXPERTS
    return pl.pallas_call(
        functools.partial(_gather_rows_kernel, cap=cap, n_exp=N_EXPERTS),
        grid_spec=pltpu.PrefetchScalarGridSpec(
            num_scalar_prefetch=1,
            grid=(n_seq, N_EXPERTS),
            in_specs=[pl.BlockSpec((seq_len * tpr, LANES), lambda s, e, idx: (s, 0))],
            out_specs=pl.BlockSpec((1, cap, d), lambda s, e, idx: (e, s, 0)),
            scratch_shapes=[pltpu.VMEM((cap * tpr, LANES), F32)],
        ),
        out_shape=jax.ShapeDtypeStruct((N_EXPERTS, n_seq * cap, d), BF16),
        compiler_params=_params(2, 48),
        name="gather_rows",
    )(idx, h)


def _ffn_kernel(xp_ref, xs_ref, wg_ref, wu_ref, wd_ref, yp_ref, ys_ref):
    rc = 512

    @pl.when(pl.program_id(1) == 0)
    def _():
        for y_ref in (yp_ref, ys_ref):
            for r in range(0, y_ref.shape[1], rc):
                y_ref[0, r:r + rc, :] = jnp.zeros((rc, y_ref.shape[2]), F32)

    wg = wg_ref[0, 0].astype(BF16)
    wu = wu_ref[0, 0].astype(BF16)
    wd = wd_ref[0, 0].astype(BF16)
    for x_ref, y_ref in ((xp_ref, yp_ref), (xs_ref, ys_ref)):
        for r in range(0, x_ref.shape[1], rc):
            x = x_ref[0, r:r + rc, :]
            hg = jnp.dot(x, wg, preferred_element_type=F32)
            hu = jnp.dot(x, wu, preferred_element_type=F32)
            hid = ((hg * jax.nn.sigmoid(hg)) * hu).astype(BF16)
            y_ref[0, r:r + rc, :] += jnp.dot(hid, wd, preferred_element_type=F32)


def _ffn(xs_p, xs_s, w_gate, w_up, w_down, layer):
    n_exp, rp, d = xs_p.shape
    rs = xs_s.shape[1]
    ff = w_gate.shape[3]
    fc = 512
    return pl.pallas_call(
        _ffn_kernel,
        grid=(n_exp, ff // fc),
        in_specs=[
            pl.BlockSpec((1, rp, d), lambda e, f: (e, 0, 0)),
            pl.BlockSpec((1, rs, d), lambda e, f: (e, 0, 0)),
            pl.BlockSpec((1, 1, d, fc), lambda e, f: (layer, e, 0, f)),
            pl.BlockSpec((1, 1, d, fc), lambda e, f: (layer, e, 0, f)),
            pl.BlockSpec((1, 1, fc, d), lambda e, f: (layer, e, f, 0)),
        ],
        out_specs=[pl.BlockSpec((1, rp, d), lambda e, f: (e, 0, 0)),
                   pl.BlockSpec((1, rs, d), lambda e, f: (e, 0, 0))],
        out_shape=[jax.ShapeDtypeStruct((n_exp, rp, d), F32),
                   jax.ShapeDtypeStruct((n_exp, rs, d), F32)],
        compiler_params=_params(2, 56),
        name="expert_ffn",
    )(xs_p, xs_s, w_gate, w_up, w_down)


def _combine_onehot_kernel(idx_ref, g_ref, ye_ref, o_ref):
    t_len = o_ref.shape[0]
    n_exp = ye_ref.shape[0]
    idx = idx_ref[0]
    hit = idx == lax.broadcasted_iota(I32, (t_len, idx.shape[1]), 0)
    g_hi, g_lo = _split_bf16(g_ref[0])
    m_hi = jnp.where(hit, g_hi.astype(F32), 0.0).astype(BF16)
    m_lo = jnp.where(hit, g_lo.astype(F32), 0.0).astype(BF16)
    y_hi, y_lo = _split_bf16(jnp.concatenate([ye_ref[e] for e in range(n_exp)], axis=0))
    dot = functools.partial(jnp.dot, preferred_element_type=F32)
    o_ref[...] = dot(m_hi, y_hi) + (dot(m_lo, y_hi) + dot(m_hi, y_lo))


def _combine_onehot(idx_row, g_row, ye, *, seq_len):
    n_exp, rows, d = ye.shape
    n_seq, _, slots = idx_row.shape
    cap = slots // n_exp
    return pl.pallas_call(
        _combine_onehot_kernel,
        grid=(n_seq,),
        in_specs=[pl.BlockSpec((1, 1, slots), lambda s: (s, 0, 0)),
                  pl.BlockSpec((1, 1, slots), lambda s: (s, 0, 0)),
                  pl.BlockSpec((n_exp, cap, d), lambda s: (0, s, 0))],
        out_specs=pl.BlockSpec((seq_len, d), lambda s: (s, 0)),
        out_shape=jax.ShapeDtypeStruct((n_seq * seq_len, d), F32),
        compiler_params=_params(1, 32),
        name="combine_onehot",
    )(idx_row, g_row, ye)


COMBINE_GROUP = 8


def _combine_rows_kernel(idx_ref, g_ref, ye_ref, o_ref, tiles, *, cap, n_exp):
    e = pl.program_id(1)
    d = ye_ref.shape[2]
    tpr = _tiles_per_row(d)
    ch = 2048

    @pl.when(e == 0)
    def _():
        def zero(c, carry):
            o_ref[pl.ds(pl.multiple_of(c * ch, ch), ch), :] = jnp.zeros((ch, LANES), F32)
            return carry
        lax.fori_loop(0, o_ref.shape[0] // ch, zero, 0)

    for s in range(tpr):
        tiles[pl.ds(s, cap, stride=tpr), :] = ye_ref[0, :, s * LANES:(s + 1) * LANES]

    base = (pl.program_id(0) * n_exp + e) * cap

    def body(i, carry):
        c0 = i * COMBINE_GROUP
        dst = [pl.ds(pl.multiple_of(idx_ref[base + c0 + k] * tpr, tpr), tpr) for k in range(COMBINE_GROUP)]
        vals = [o_ref[dst[k], :]
                + g_ref[base + c0 + k] * tiles[pl.ds(pl.multiple_of((c0 + k) * tpr, tpr), tpr), :]
                for k in range(COMBINE_GROUP)]
        for k in range(COMBINE_GROUP):
            o_ref[dst[k], :] = vals[k]
        return carry

    lax.fori_loop(0, cap // COMBINE_GROUP, body, 0)


def _combine_rows(idx, g, ye, *, seq_len):
    n_exp, rows, d = ye.shape
    tpr = _tiles_per_row(d)
    cap = CAPACITY_FACTOR * seq_len // N_EXPERTS
    n_seq = rows // cap
    return pl.pallas_call(
        functools.partial(_combine_rows_kernel, cap=cap, n_exp=n_exp),
        grid_spec=pltpu.PrefetchScalarGridSpec(
            num_scalar_prefetch=2,
            grid=(n_seq, n_exp),
            in_specs=[pl.BlockSpec((1, cap, d), lambda s, e, i, gg: (e, s, 0))],
            out_specs=pl.BlockSpec((seq_len * tpr, LANES), lambda s, e, i, gg: (s, 0)),
            scratch_shapes=[pltpu.VMEM((cap * tpr, LANES), F32)],
        ),
        out_shape=jax.ShapeDtypeStruct((n_seq * seq_len * tpr, LANES), F32),
        compiler_params=_params(2, 48),
        name="combine_rows",
    )(idx, g, ye)


def _head_norm(x, gain):
    return x * lax.rsqrt(jnp.mean(x * x, axis=-1, keepdims=True) + EPS) * gain


def _qkv_kernel(*refs, rope, has_prev, mod_row, n_heads, n_kv):
    refs = list(refs)
    x_ref = refs.pop(0)
    prev = None
    if has_prev:
        acc_ref, pmod_ref = refs.pop(0), refs.pop(0)
    mod_ref, gain_ref, w_ref, qg_ref, kg_ref = refs[:5]
    refs = refs[5:]
    if rope:
        cos_ref, sin_ref = refs.pop(0), refs.pop(0)
    if has_prev:
        prev = (acc_ref, pmod_ref, refs.pop(0))
    q_ref, k_ref, v_ref = refs[:3]
    d = x_ref.shape[1]
    hd = HEAD_DIM
    m = mod_row(pl.program_id(0))
    x = _residual_stream(x_ref, prev, m)
    shift = mod_ref[pl.ds(m, 1), pl.ds(0, d)]
    scale = mod_ref[pl.ds(m, 1), pl.ds(d, d)]
    h = _modulated_norm(x, gain_ref[...], shift, scale).astype(BF16)
    qkv = jnp.dot(h, w_ref[...], preferred_element_type=F32)

    if rope:
        cos = cos_ref[...]
        sin = sin_ref[...]
        lane = lax.broadcasted_iota(I32, cos.shape, 1)
        first_half = (lane % (hd // 2)) < hd // 4

        def rot(xh):
            partner = jnp.where(first_half, pltpu.roll(xh, hd - hd // 4, 1), pltpu.roll(xh, hd // 4, 1))
            return xh * cos + partner * sin
    else:
        rot = lambda xh: xh

    qg = qg_ref[...]
    kg = kg_ref[...]
    q_scale = math.log2(math.e) * hd ** -0.5
    for i in range(n_heads):
        qh = rot(_head_norm(qkv[:, i * hd:(i + 1) * hd], qg)) * q_scale
        q_ref[:, i * hd:(i + 1) * hd] = qh.astype(BF16)
    for i in range(n_kv):
        c0 = (n_heads + i) * hd
        kh = _head_norm(qkv[:, c0:c0 + hd], kg)
        if not rope:
            refs[3][:, i * hd:(i + 1) * hd] = kh
        k_ref[:, i * hd:(i + 1) * hd] = rot(kh).astype(BF16)
    v = qkv[:, (n_heads + n_kv) * hd:]
    v_ref[...] = v.astype(BF16)
    if not rope:
        refs[4][...] = v


def _qkv(x, prev, mod, gain, w_qkv, q_gain, k_gain, tables, *, seq_len, mod_row, n_heads, n_kv):
    n, d = x.shape
    hd = HEAD_DIM
    row = lambda i: (i, 0)
    fixed = lambda i: (0, 0)
    rope = tables is not None
    in_specs = [pl.BlockSpec((ROW_BLOCK, d), row)]
    args = [x]
    if prev is not None:
        in_specs += [_rows_spec(prev[0], d), pl.BlockSpec(prev[1].shape, fixed)]
        args += list(prev)
    in_specs += [
        pl.BlockSpec(mod.shape, fixed),
        pl.BlockSpec((1, d), fixed),
        pl.BlockSpec(w_qkv.shape, fixed),
        pl.BlockSpec((1, hd), fixed),
        pl.BlockSpec((1, hd), fixed),
    ]
    args += [mod, gain, w_qkv, q_gain, k_gain]
    if rope:
        per_seq = seq_len // ROW_BLOCK
        in_specs += [pl.BlockSpec((ROW_BLOCK, hd), lambda i: (i % per_seq, 0))] * 2
        args += list(tables)
    out_specs, out_shape = [], []
    if prev is not None:
        out_specs.append(pl.BlockSpec((ROW_BLOCK, d), row))
        out_shape.append(jax.ShapeDtypeStruct((n, d), F32))
    out_specs += [pl.BlockSpec((ROW_BLOCK, n_heads * hd), row),
                  pl.BlockSpec((ROW_BLOCK, n_kv * hd), row),
                  pl.BlockSpec((ROW_BLOCK, n_kv * hd), row)]
    out_shape += [jax.ShapeDtypeStruct((n, n_heads * hd), BF16),
                  jax.ShapeDtypeStruct((n, n_kv * hd), BF16),
                  jax.ShapeDtypeStruct((n, n_kv * hd), BF16)]
    if not rope:
        out_specs += [pl.BlockSpec((ROW_BLOCK, n_kv * hd), row)] * 2
        out_shape += [jax.ShapeDtypeStruct((n, n_kv * hd), F32)] * 2
    outs = pl.pallas_call(
        functools.partial(_qkv_kernel, rope=rope, has_prev=prev is not None, mod_row=mod_row,
                          n_heads=n_heads, n_kv=n_kv),
        grid=(n // ROW_BLOCK,),
        in_specs=in_specs,
        out_specs=out_specs,
        out_shape=out_shape,
        compiler_params=_params(1, 48),
        name="qkv_rope" if rope else "qkv",
    )(*args)
    return list(outs) if prev is not None else [x] + list(outs)


def _attn_kernel(*refs, group, chunk, has_cache):
    if has_cache:
        q_ref, k_ref, v_ref, kc_ref, vc_ref, o_ref = refs
    else:
        q_ref, k_ref, v_ref, o_ref = refs
    hd = HEAD_DIM
    qb = q_ref.shape[0]
    rows = group * qb
    q = jnp.concatenate([q_ref[:, g * hd:(g + 1) * hd] for g in range(group)], axis=0)
    sources = [(k_ref, v_ref, s0, chunk) for s0 in range(0, k_ref.shape[0], chunk)]
    if has_cache:
        sources.append((kc_ref.at[0], vc_ref.at[0], 0, kc_ref.shape[1]))
    m = jnp.full((rows, 1), -jnp.inf, F32)
    acc = jnp.zeros((rows, 2 * hd), F32)
    for kr, vr, s0, size in sources:
        s = lax.dot_general(q, kr[s0:s0 + size, :], (((1,), (1,)), ((), ())),
                            preferred_element_type=F32)
        m_new = jnp.maximum(m, jnp.max(s, axis=-1, keepdims=True))
        p = jnp.exp2(s - m_new).astype(BF16)
        v_ones = jnp.concatenate([vr[s0:s0 + size, :], jnp.ones((size, hd), BF16)], axis=1)
        acc = jnp.exp2(m - m_new) * acc + jnp.dot(p, v_ones, preferred_element_type=F32)
        m = m_new
    o = acc[:, :hd] / acc[:, hd:]
    o_ref[...] = jnp.concatenate([o[g * qb:(g + 1) * qb] for g in range(group)], axis=1).astype(BF16)


def _attention(q, k, v, cache, *, seq_len, qb, chunk, n_heads, n_kv):
    n = q.shape[0]
    hd = HEAD_DIM
    group = n_heads // n_kv
    nq = seq_len // qb
    in_specs = [
        pl.BlockSpec((qb, group * hd), lambda b, h, i: (b * nq + i, h)),
        pl.BlockSpec((seq_len, hd), lambda b, h, i: (b, h)),
        pl.BlockSpec((seq_len, hd), lambda b, h, i: (b, h)),
    ]
    args = [q, k, v]
    if cache is not None:
        past = cache[0].shape[1]
        in_specs += [pl.BlockSpec((1, past, hd), lambda b, h, i: (b, 0, h))] * 2
        args += list(cache)
    return pl.pallas_call(
        functools.partial(_attn_kernel, group=group, chunk=chunk, has_cache=cache is not None),
        grid=(n // seq_len, n_kv, nq),
        in_specs=in_specs,
        out_specs=pl.BlockSpec((qb, group * hd), lambda b, h, i: (b * nq + i, h)),
        out_shape=jax.ShapeDtypeStruct((n, n_heads * hd), BF16),
        compiler_params=_params(3, 48),
        name=f"attention_{seq_len}",
    )(*args)


def _final_norm_kernel(x_ref, acc_ref, pmod_ref, gain_ref, o_ref, *, mod_row):
    n_rows, d = x_ref.shape
    m = mod_row(pl.program_id(0))
    x = x_ref[...] + pmod_ref[pl.ds(m, 1), pl.ds(5 * d, d)] * _load_rows(acc_ref, n_rows, d)
    o_ref[...] = (x * lax.rsqrt(jnp.mean(x * x, axis=-1, keepdims=True) + EPS)) * gain_ref[...]


def _final_norm(x, acc, pmod, gain, mod_row):
    n, d = x.shape
    row = lambda i: (i, 0)
    fixed = lambda i: (0, 0)
    return pl.pallas_call(
        functools.partial(_final_norm_kernel, mod_row=mod_row),
        grid=(n // ROW_BLOCK,),
        in_specs=[pl.BlockSpec((ROW_BLOCK, d), row), _rows_spec(acc, d),
                  pl.BlockSpec(pmod.shape, fixed), pl.BlockSpec((1, d), fixed)],
        out_specs=pl.BlockSpec((ROW_BLOCK, d), row),
        out_shape=jax.ShapeDtypeStruct((n, d), F32),
        compiler_params=_params(1, 32),
        name="final_norm",
    )(x, acc, pmod, gain)


def _rope_tables(seq_len):
    hd = HEAD_DIM
    axis = hd // 2
    t = jnp.arange(seq_len)
    inv = ROPE_BASE ** (-jnp.arange(axis // 2, dtype=F32) * 2.0 / axis)

    def half(pos):
        ang = pos.astype(F32)[:, None] * inv
        c, s = jnp.cos(ang), jnp.sin(ang)
        return jnp.concatenate([c, c], axis=1), jnp.concatenate([-s, s], axis=1)

    c_row, s_row = half(t // GRID_W)
    c_col, s_col = half(t % GRID_W)
    return jnp.concatenate([c_row, c_col], axis=1), jnp.concatenate([s_row, s_col], axis=1)


def kernel(x_prompt, x_sample, state_lru, cache_k, cache_v, c, c_ctx, w_mod, b_mod, norm_gain, final_gain,
           w_lru_in, lru_conv_w, lru_conv_b, lru_wa, lru_ba, lru_wx, lru_bx, lru_lambda, w_lru_out,
           w_qkv, q_norm, k_norm, w_attn_out, w_router, w_exp_gate, w_exp_up, w_exp_down):
    batch, seq, d = x_prompt.shape
    dec_batch, dec_seq, _ = x_sample.shape
    depth = w_mod.shape[0]
    n_kv = cache_k.shape[3]
    n_heads = w_attn_out.shape[1] // HEAD_DIM
    heads = dict(n_heads=n_heads, n_kv=n_kv)

    cond = jnp.zeros((SUBLANES, d), F32).at[0].set(c_ctx).at[1:1 + dec_batch].set(c)
    mod_all = _mod_vectors(cond, w_mod, b_mod)

    blocks_per_seq = dec_seq // ROW_BLOCK
    xs = [x_prompt.reshape(batch * seq, d), x_sample.reshape(dec_batch * dec_seq, d)]
    lens = [seq, dec_seq]
    mod_rows = [lambda i: 0, lambda i: 1 + i // blocks_per_seq]
    prevs = [None, None]

    new_lru, new_k, new_v = [], [], []
    for l in range(depth):
        mod = mod_all[l]
        gain_a = norm_gain[l, 0].reshape(1, d)
        gain_c = norm_gain[l, 1].reshape(1, d)
        mixed = []
        if l % 2 == 0:
            li = l // 2
            w_in = w_lru_in[li].astype(BF16)
            wg = (0.5 * jnp.concatenate([lru_wa[li, 0], lru_wx[li, 0], lru_wa[li, 1], lru_wx[li, 1]],
                                        axis=2)).astype(BF16)
            bias4 = 0.5 * jnp.stack([lru_ba[li, 0], lru_bx[li, 0], lru_ba[li, 1], lru_bx[li, 1]])
            h0s = [jnp.zeros((batch, 2, w_in.shape[1] // 2), F32), state_lru[:, li]]
            for gi in range(2):
                xs[gi], gate, xr = _lru_in(xs[gi], prevs[gi], mod, gain_a, w_in, mod_rows[gi])
                y, fin = _lru_core(xr, gate, lru_conv_w[li], lru_conv_b[li].reshape(1, -1), wg, bias4,
                                   lru_lambda[li], h0s[gi], seq_len=lens[gi])
                mixed.append(y)
                if gi == 0:
                    new_lru.append(fin.astype(x_prompt.dtype))
            w_out = w_lru_out[li].astype(BF16)
        else:
            ai = l // 2
            w = w_qkv[ai].astype(BF16)
            qg = q_norm[ai].reshape(1, -1)
            kg = k_norm[ai].reshape(1, -1)
            xs[0], q_p, k_p, v_p, kf, vf = _qkv(xs[0], prevs[0], mod, gain_a, w, qg, kg, None,
                                                seq_len=seq, mod_row=mod_rows[0], **heads)
            xs[1], q_s, k_s, v_s = _qkv(xs[1], prevs[1], mod, gain_a, w, qg, kg, _rope_tables(dec_seq),
                                        seq_len=dec_seq, mod_row=mod_rows[1], **heads)
            new_k.append(kf.reshape(batch, seq, n_kv, HEAD_DIM))
            new_v.append(vf.reshape(batch, seq, n_kv, HEAD_DIM))
            past = cache_k.shape[2]
            kc = cache_k[:, ai].reshape(dec_batch, past, n_kv * HEAD_DIM).astype(BF16)
            vc = cache_v[:, ai].reshape(dec_batch, past, n_kv * HEAD_DIM).astype(BF16)
            mixed.append(_attention(q_p, k_p, v_p, None, seq_len=seq, qb=seq, chunk=seq, **heads))
            mixed.append(_attention(q_s, k_s, v_s, (kc, vc), seq_len=dec_seq, qb=128, chunk=1024, **heads))
            w_out = w_attn_out[ai].astype(BF16)

        wr_hi, wr_lo = _split_bf16(w_router[l].T)
        x1_p, h2_p, lg_p = _mix_out(mixed[0], xs[0], mod, gain_c, w_out, wr_hi, wr_lo, mod_rows[0], False)
        x1_s, h2_s, lg_s = _mix_out(mixed[1], xs[1], mod, gain_c, w_out, wr_hi, wr_lo, mod_rows[1], True)
        idx_p, g_p, idx_s, g_s = _route(lg_p, lg_s, seq, dec_seq)
        xe_p = _gather_onehot(idx_p[:, :, None], h2_p, seq_len=seq)
        xe_s = _gather_rows(idx_s, h2_s, d, seq_len=dec_seq)
        ye_p, ye_s = _ffn(xe_p, xe_s, w_exp_gate, w_exp_up, w_exp_down, l)
        acc_p = _combine_onehot(idx_p[:, None, :], g_p[:, None, :], ye_p, seq_len=seq)
        acc_s = _combine_rows(idx_s, g_s, ye_s, seq_len=dec_seq)
        xs = [x1_p, x1_s]
        prevs = [(acc_p, mod), (acc_s, mod)]

    fg = final_gain.reshape(1, d)
    y_prompt = _final_norm(xs[0], *prevs[0], fg, mod_rows[0]).reshape(batch, seq, d)
    y_sample = _final_norm(xs[1], *prevs[1], fg, mod_rows[1]).reshape(dec_batch, dec_seq, d)
    return (y_prompt, y_sample, jnp.stack(new_lru, axis=1), jnp.stack(new_k, axis=1), jnp.stack(new_v, axis=1))
```

```python
import functools
import math

import jax
import jax.numpy as jnp
from jax import lax
from jax.experimental import pallas as pl
from jax.experimental.pallas import tpu as pltpu

F32 = jnp.float32
BF16 = jnp.bfloat16
I32 = jnp.int32

LANES = 128
SUBLANES = 8
MIB = 1024 * 1024

RG_C = 8.0
CONV_W = 4
CONV_PAD_L = 2
ROPE_BASE = 10000.0
GRID_W = 64
EPS = 1e-6
N_EXPERTS = 16
CAPACITY_FACTOR = 2
HEAD_DIM = 128
LRU_BLOCKS = 8
LRU_STEP_BLOCKS = 2

ROW_BLOCK = 512


def _params(n_axes, vmem_mib):
    return pltpu.CompilerParams(
        dimension_semantics=("arbitrary",) * n_axes, vmem_limit_bytes=vmem_mib * MIB)


def _split_bf16(x):
    hi = x.astype(BF16)
    lo = (x - hi.astype(F32)).astype(BF16)
    return hi, lo


def _modulated_norm(x, gain, shift, scale):
    y = x * lax.rsqrt(jnp.mean(x * x, axis=-1, keepdims=True) + EPS)
    return (y * gain) * (1.0 + scale) + shift


def _tiles_per_row(d):
    return d // LANES


def _load_rows(ref, n_rows, d):
    if ref.shape == (n_rows, d):
        return ref[...]
    tpr = _tiles_per_row(d)
    return jnp.concatenate([ref[pl.ds(s, n_rows, stride=tpr), :] for s in range(tpr)], axis=1)


def _store_rows(ref, x, r0):
    n_rows, d = x.shape
    if ref.shape[1] == d:
        ref[r0:r0 + n_rows, :] = x.astype(ref.dtype)
        return
    tpr = _tiles_per_row(d)
    for s in range(tpr):
        ref[pl.ds(r0 * tpr + s, n_rows, stride=tpr), :] = x[:, s * LANES:(s + 1) * LANES]


def _rows_spec(arr, d):
    if arr.shape[1] == d:
        return pl.BlockSpec((ROW_BLOCK, d), lambda i: (i, 0))
    return pl.BlockSpec((ROW_BLOCK * _tiles_per_row(d), LANES), lambda i: (i, 0))


def _residual_stream(x_ref, prev, m):
    if prev is None:
        return x_ref[...]
    acc_ref, pmod_ref, xo_ref = prev
    n_rows, d = x_ref.shape
    x = x_ref[...] + pmod_ref[pl.ds(m, 1), pl.ds(5 * d, d)] * _load_rows(acc_ref, n_rows, d)
    xo_ref[...] = x
    return x


def _mod_kernel(cond_ref, w_ref, b_ref, o_ref):
    c = cond_ref[...]
    a_hi, a_lo = _split_bf16(c * jax.nn.sigmoid(c))
    w_hi, w_lo = _split_bf16(w_ref[0])
    dot = functools.partial(jnp.dot, preferred_element_type=F32)
    o_ref[0] = dot(a_hi, w_hi) + (dot(a_lo, w_hi) + dot(a_hi, w_lo)) + b_ref[0]


def _mod_vectors(cond8, w_mod, b_mod):
    depth, d, d6 = w_mod.shape
    nb = 1536
    return pl.pallas_call(
        _mod_kernel,
        grid=(depth, d6 // nb),
        in_specs=[
            pl.BlockSpec((SUBLANES, d), lambda l, j: (0, 0)),
            pl.BlockSpec((1, d, nb), lambda l, j: (l, 0, j)),
            pl.BlockSpec((1, 1, nb), lambda l, j: (l, 0, j)),
        ],
        out_specs=pl.BlockSpec((1, SUBLANES, nb), lambda l, j: (l, 0, j)),
        out_shape=jax.ShapeDtypeStruct((depth, SUBLANES, d6), F32),
        compiler_params=_params(2, 40),
        name="adaln_mod",
    )(cond8, w_mod, b_mod.reshape(depth, 1, d6))


def _lru_in_kernel(*refs, mod_row, has_prev):
    if has_prev:
        x_ref, acc_ref, pmod_ref, mod_ref, gain_ref, w_ref, xo_ref, gate_ref, xr_ref = refs
        prev = (acc_ref, pmod_ref, xo_ref)
    else:
        x_ref, mod_ref, gain_ref, w_ref, gate_ref, xr_ref = refs
        prev = None
    d = x_ref.shape[1]
    m = mod_row(pl.program_id(0))
    x = _residual_stream(x_ref, prev, m)
    shift = mod_ref[pl.ds(m, 1), pl.ds(0, d)]
    scale = mod_ref[pl.ds(m, 1), pl.ds(d, d)]
    h = _modulated_norm(x, gain_ref[...], shift, scale).astype(BF16)
    u = jnp.dot(h, w_ref[...], preferred_element_type=F32)
    w = gate_ref.shape[1]
    gate_ref[...] = u[:, :w]
    xr_ref[...] = u[:, w:]


def _lru_in(x, prev, mod, gain, w_in, mod_row):
    n, d = x.shape
    w2 = w_in.shape[1]
    w = w2 // 2
    row = lambda i: (i, 0)
    fixed = lambda i: (0, 0)
    in_specs = [pl.BlockSpec((ROW_BLOCK, d), row)]
    args = [x]
    out_specs = [pl.BlockSpec((ROW_BLOCK, w), row), pl.BlockSpec((ROW_BLOCK, w), row)]
    out_shape = [jax.ShapeDtypeStruct((n, w), F32), jax.ShapeDtypeStruct((n, w), F32)]
    if prev is not None:
        in_specs += [_rows_spec(prev[0], d), pl.BlockSpec(prev[1].shape, fixed)]
        args += list(prev)
        out_specs = [pl.BlockSpec((ROW_BLOCK, d), row)] + out_specs
        out_shape = [jax.ShapeDtypeStruct((n, d), F32)] + out_shape
    in_specs += [pl.BlockSpec(mod.shape, fixed), pl.BlockSpec((1, d), fixed), pl.BlockSpec((d, w2), fixed)]
    args += [mod, gain, w_in]
    outs = pl.pallas_call(
        functools.partial(_lru_in_kernel, mod_row=mod_row, has_prev=prev is not None),
        grid=(n // ROW_BLOCK,),
        in_specs=in_specs,
        out_specs=out_specs,
        out_shape=out_shape,
        compiler_params=_params(1, 48),
        name="lru_in",
    )(*args)
    return outs if prev is not None else [x] + list(outs)


def _lru_core_kernel(xr_ref, gate_ref, cw_ref, cb_ref, wg_ref, bias_ref, lam_ref, h0_ref,
                     y_ref, fin_ref, pad, a_f, b_f, a_b, b_b):
    t_len, lb = xr_ref.shape
    ch = 256
    halo = SUBLANES
    seg = t_len // SUBLANES
    piece = min(ch, seg)
    seg_shift = seg.bit_length() - 1

    def seg_rows(t0):
        r = lax.shift_right_logical(t0, seg_shift)
        return pl.ds((t0 - r * seg) * SUBLANES + r, piece, stride=SUBLANES)

    zero_rows = jnp.zeros((halo, lb), F32)
    pad[0:halo, :] = zero_rows
    pad[t_len + halo:t_len + 2 * halo, :] = zero_rows

    def copy_in(c, carry):
        r = pl.multiple_of(c * ch, ch)
        pad[pl.ds(r + halo, ch), :] = xr_ref[pl.ds(r, ch), :]
        return carry

    lax.fori_loop(0, t_len // ch, copy_in, 0)

    neg_lam = -lam_ref[...]
    softplus = jnp.maximum(neg_lam, 0.0) + jnp.log1p(jnp.exp(-jnp.abs(neg_lam)))
    c_nla = (0.5 * RG_C) * softplus
    c_exp2 = (-0.5 * RG_C * math.log2(math.e)) * softplus
    cw = cw_ref[...]
    cbias = cb_ref[...]
    half_bias = bias_ref[...]
    half_wg = wg_ref[...]
    bw = half_wg.shape[1]
    n_slab = lb // LANES

    def gates(c, carry):
        r = pl.multiple_of(c * ch, ch)
        blk = pad[pl.ds(r, ch + 2 * halo), :]
        xc = cbias
        for k in range(CONV_W):
            o = halo - CONV_PAD_L + k
            xc = xc + blk[o:o + ch] * cw[k:k + 1]
        xc_bf = xc.astype(BF16)
        g = [jnp.dot(xc_bf[:, j * bw:(j + 1) * bw], half_wg[j], preferred_element_type=F32)
             for j in range(lb // bw)]
        pre = lambda k: jnp.concatenate([gj[:, k * bw:(k + 1) * bw] for gj in g], axis=1) + half_bias[k:k + 1]
        half_xc = 0.5 * xc
        for d, (a_ref, b_ref) in enumerate(((a_f, b_f), (a_b, b_b))):
            u = jnp.tanh(pre(2 * d)) + 1.0
            t_i = jnp.tanh(pre(2 * d + 1))
            a = jnp.exp2(u * c_exp2[d:d + 1])
            b = jnp.sqrt(jnp.tanh(u * c_nla[d:d + 1]) * (1.0 + a * a)) * (t_i * half_xc + half_xc)
            for p in range(ch // piece):
                rows = seg_rows(r + p * piece)
                for j in range(n_slab):
                    a_ref[j, rows, :] = a[p * piece:(p + 1) * piece, j * LANES:(j + 1) * LANES]
                    b_ref[j, rows, :] = b[p * piece:(p + 1) * piece, j * LANES:(j + 1) * LANES]
        return carry

    lax.fori_loop(0, t_len // ch, gates, 0)

    def scan(i, carry):
        rows = (pl.ds(pl.multiple_of(i * SUBLANES, SUBLANES), SUBLANES),
                pl.ds(pl.multiple_of((seg - 1 - i) * SUBLANES, SUBLANES), SUBLANES))
        out = []
        for k, (h, prod) in enumerate(carry):
            d, j = divmod(k, n_slab)
            a_ref, b_ref = ((a_f, b_f), (a_b, b_b))[d]
            a = a_ref[j, rows[d], :]
            h = a * h + b_ref[j, rows[d], :]
            prod = a * prod
            b_ref[j, rows[d], :] = h
            a_ref[j, rows[d], :] = prod
            out.append((h, prod))
        return tuple(out)

    start = (jnp.zeros((SUBLANES, LANES), F32), jnp.ones((SUBLANES, LANES), F32))
    ends = lax.fori_loop(0, seg, scan, (start,) * (2 * n_slab), unroll=8)

    enter = []
    for k, (h, prod) in enumerate(ends):
        d, j = divmod(k, n_slab)
        states = [h0_ref[0, d:d + 1, j * LANES:(j + 1) * LANES]]
        for r in (range(SUBLANES) if d == 0 else reversed(range(SUBLANES))):
            states.append(h[r:r + 1] + prod[r:r + 1] * states[-1])
        fin_ref[0, d:d + 1, j * LANES:(j + 1) * LANES] = states[SUBLANES]
        order = states[:SUBLANES] if d == 0 else states[:SUBLANES][::-1]
        enter.append(jnp.concatenate(order, axis=0))

    def fix(i, carry):
        rows = pl.ds(pl.multiple_of(i * SUBLANES, SUBLANES), SUBLANES)
        for j in range(n_slab):
            b_f[j, rows, :] = ((b_f[j, rows, :] + a_f[j, rows, :] * enter[j])
                               + (b_b[j, rows, :] + a_b[j, rows, :] * enter[n_slab + j]))
        return carry

    lax.fori_loop(0, seg, fix, 0, unroll=8)

    def emit(c, carry):
        r = pl.multiple_of(c * ch, ch)
        gt = gate_ref[pl.ds(r, ch), :]
        cdf = 0.5 * (1.0 + jnp.tanh(math.sqrt(2.0 / math.pi) * (gt + 0.044715 * (gt * gt * gt))))
        y = gt * cdf
        for p in range(ch // piece):
            rows = seg_rows(r + p * piece)
            total = jnp.concatenate([b_f[j, rows, :] for j in range(n_slab)], axis=1)
            y_ref[pl.ds(r + p * piece, piece), :] = (total * y[p * piece:(p + 1) * piece]).astype(BF16)
        return carry

    lax.fori_loop(0, t_len // ch, emit, 0)


def _lru_core(xr, gate, conv_w, conv_b, wg, bias4, lam, h0, *, seq_len):
    n, w = xr.shape
    n_seq = n // seq_len
    bw = w // LRU_BLOCKS
    lb = LRU_STEP_BLOCKS * bw
    tok = lambda s, c: (s, c)
    chan = lambda s, c: (0, c)
    return pl.pallas_call(
        _lru_core_kernel,
        grid=(n_seq, LRU_BLOCKS // LRU_STEP_BLOCKS),
        in_specs=[
            pl.BlockSpec((seq_len, lb), tok),
            pl.BlockSpec((seq_len, lb), tok),
            pl.BlockSpec((CONV_W, lb), chan),
            pl.BlockSpec((1, lb), chan),
            pl.BlockSpec((LRU_STEP_BLOCKS, bw, 4 * bw), lambda s, c: (c, 0, 0)),
            pl.BlockSpec((4, lb), chan),
            pl.BlockSpec((2, lb), chan),
            pl.BlockSpec((1, 2, lb), lambda s, c: (s, 0, c)),
        ],
        out_specs=[pl.BlockSpec((seq_len, lb), tok),
                   pl.BlockSpec((1, 2, lb), lambda s, c: (s, 0, c))],
        out_shape=[jax.ShapeDtypeStruct((n, w), BF16), jax.ShapeDtypeStruct((n_seq, 2, w), F32)],
        scratch_shapes=[pltpu.VMEM((seq_len + 2 * SUBLANES, lb), F32)]
        + [pltpu.VMEM((lb // LANES, seq_len, LANES), F32)] * 4,
        compiler_params=_params(2, 48),
        name=f"lru_core_{seq_len}",
    )(xr, gate, conv_w, conv_b, wg, bias4, lam, h0)


def _mix_out_kernel(a_ref, x_ref, mod_ref, gain_ref, w_ref, wrh_ref, wrl_ref,
                    x1_ref, h2_ref, lg_ref, *, mod_row):
    d = x_ref.shape[1]
    m = mod_row(pl.program_id(0))
    g_mix = mod_ref[pl.ds(m, 1), pl.ds(2 * d, d)]
    shift = mod_ref[pl.ds(m, 1), pl.ds(3 * d, d)]
    scale = mod_ref[pl.ds(m, 1), pl.ds(4 * d, d)]
    op = jnp.dot(a_ref[...], w_ref[...], preferred_element_type=F32)
    x1 = x_ref[...] + g_mix * op
    x1_ref[...] = x1
    h2 = _modulated_norm(x1, gain_ref[...], shift, scale)
    h_hi, h_lo = _split_bf16(h2)
    _store_rows(h2_ref, h2, 0)
    nt = functools.partial(lax.dot_general, dimension_numbers=(((1,), (1,)), ((), ())),
                           preferred_element_type=F32)
    lg_ref[...] = nt(wrh_ref[...], h_hi) + (nt(wrl_ref[...], h_hi) + nt(wrh_ref[...], h_lo))


def _mix_out(a, x, mod, gain, w_out, wr_hi, wr_lo, mod_row, token_tiled):
    n, d = x.shape
    e = wr_hi.shape[0]
    row = lambda i: (i, 0)
    fixed = lambda i: (0, 0)
    tpr = _tiles_per_row(d)
    if token_tiled:
        h2_spec = pl.BlockSpec((ROW_BLOCK * tpr, LANES), row)
        h2_shape = jax.ShapeDtypeStruct((n * tpr, LANES), F32)
    else:
        h2_spec = pl.BlockSpec((ROW_BLOCK, d), row)
        h2_shape = jax.ShapeDtypeStruct((n, d), BF16)
    return pl.pallas_call(
        functools.partial(_mix_out_kernel, mod_row=mod_row),
        grid=(n // ROW_BLOCK,),
        in_specs=[
            pl.BlockSpec((ROW_BLOCK, a.shape[1]), row),
            pl.BlockSpec((ROW_BLOCK, d), row),
            pl.BlockSpec(mod.shape, fixed),
            pl.BlockSpec((1, d), fixed),
            pl.BlockSpec(w_out.shape, fixed),
            pl.BlockSpec(wr_hi.shape, fixed),
            pl.BlockSpec(wr_lo.shape, fixed),
        ],
        out_specs=[pl.BlockSpec((ROW_BLOCK, d), row), h2_spec,
                   pl.BlockSpec((e, ROW_BLOCK), lambda i: (0, i))],
        out_shape=[jax.ShapeDtypeStruct((n, d), F32), h2_shape, jax.ShapeDtypeStruct((e, n), F32)],
        compiler_params=_params(1, 48),
        name="mix_out",
    )(a, x, mod, gain, w_out, wr_hi, wr_lo)


def _lane_sum(tiles):
    acc = tiles[0]
    for t in tiles[1:]:
        acc = acc + t
    return jnp.sum(acc, axis=1, keepdims=True)


def _exclusive_cumsum(flags, upper):
    out = []
    off = jnp.zeros((flags[0].shape[0], 1), F32)
    for f in flags:
        out.append(jnp.dot(f.astype(BF16), upper, preferred_element_type=F32) + off)
        off = off + jnp.sum(f, axis=1, keepdims=True)
    return out


def _route_group(lg_ref, n_seq, t_len, idx_ref, g_ref):
    cap = CAPACITY_FACTOR * t_len // N_EXPERTS
    affs = []
    for s in range(n_seq):
        lg = lg_ref[:, s * t_len:(s + 1) * t_len]
        ex = jnp.exp(lg - jnp.max(lg, axis=0, keepdims=True))
        affs.append(ex / jnp.sum(ex, axis=0, keepdims=True))
    aff = jnp.concatenate(affs, axis=0)
    n_rows = aff.shape[0]
    nt = t_len // LANES
    g = [aff[:, j * LANES:(j + 1) * LANES] for j in range(nt)]

    kth_bits = jnp.zeros((n_rows, 1), I32)
    for bit in range(30, -1, -1):
        cand = kth_bits | (1 << bit)
        cand_f = pltpu.bitcast(cand, F32)
        cnt = _lane_sum([jnp.where(t >= cand_f, 1, 0) for t in g])
        kth_bits = jnp.where(cnt >= cap, cand, kth_bits)
    kth = pltpu.bitcast(kth_bits, F32)

    lane = lax.broadcasted_iota(I32, (LANES, LANES), 0)
    upper = jnp.where(lane < lax.broadcasted_iota(I32, (LANES, LANES), 1), 1.0, 0.0).astype(BF16)
    gt = [t > kth for t in g]
    eq = [t == kth for t in g]
    need = (cap - _lane_sum([jnp.where(m, 1, 0) for m in gt])).astype(F32)
    eq_rank = _exclusive_cumsum([jnp.where(m, 1.0, 0.0) for m in eq], upper)
    sel = [jnp.logical_or(gt[j], jnp.logical_and(eq[j], eq_rank[j] < need)) for j in range(nt)]
    pos = _exclusive_cumsum([jnp.where(m, 1.0, 0.0) for m in sel], upper)

    lane_r = lax.broadcasted_iota(I32, (n_rows, LANES), 1)
    d = [jnp.where(sel[j], lane_r + j * LANES - pos[j].astype(I32), -1) for j in range(nt)]
    for k in range(t_len.bit_length() - 1):
        s = 1 << k
        if s < LANES:
            d_rot = [pltpu.roll(x, LANES - s, 1) for x in d]
            g_rot = [pltpu.roll(x, LANES - s, 1) for x in g]
            same = lane_r < LANES - s
            d_in = [jnp.where(same, d_rot[j], d_rot[(j + 1) % nt]) for j in range(nt)]
            g_in = [jnp.where(same, g_rot[j], g_rot[(j + 1) % nt]) for j in range(nt)]
        else:
            q = s // LANES
            d_in = [d[(j + q) % nt] for j in range(nt)]
            g_in = [g[(j + q) % nt] for j in range(nt)]
        new_d, new_g = [], []
        for j in range(nt):
            move = jnp.logical_and(d_in[j] >= 0, ((d_in[j] >> k) & 1) == 1)
            stay = jnp.logical_and(d[j] >= 0, ((d[j] >> k) & 1) == 0)
            new_d.append(jnp.where(move, d_in[j], jnp.where(stay, d[j], -1)))
            new_g.append(jnp.where(move, g_in[j], g[j]))
        d, g = new_d, new_g

    for j in range(idx_ref.shape[1] // LANES):
        idx_ref[:, j * LANES:(j + 1) * LANES] = lane_r + j * LANES + d[j]
        g_ref[:, j * LANES:(j + 1) * LANES] = g[j]


def _route_kernel(lgp_ref, lgs_ref, idx_p_ref, g_p_ref, idx_s_ref, g_s_ref, *, p_len, s_len):
    _route_group(lgp_ref, lgp_ref.shape[1] // p_len, p_len, idx_p_ref, g_p_ref)
    _route_group(lgs_ref, lgs_ref.shape[1] // s_len, s_len, idx_s_ref, g_s_ref)


def _route(lg_p, lg_s, p_len, s_len):
    e = lg_p.shape[0]
    np_seq = lg_p.shape[1] // p_len
    ns_seq = lg_s.shape[1] // s_len
    cap_p = CAPACITY_FACTOR * p_len // N_EXPERTS
    cap_s = CAPACITY_FACTOR * s_len // N_EXPERTS
    wp = max(cap_p, LANES)
    ws = max(cap_s, LANES)
    idx_p, g_p, idx_s, g_s = pl.pallas_call(
        functools.partial(_route_kernel, p_len=p_len, s_len=s_len),
        out_shape=[jax.ShapeDtypeStruct((np_seq * e, wp), I32),
                   jax.ShapeDtypeStruct((np_seq * e, wp), F32),
                   jax.ShapeDtypeStruct((ns_seq * e, ws), I32),
                   jax.ShapeDtypeStruct((ns_seq * e, ws), F32)],
        compiler_params=pltpu.CompilerParams(vmem_limit_bytes=40 * MIB),
        name="route",
    )(lg_p, lg_s)
    return (idx_p[:, :cap_p].reshape(np_seq, e * cap_p), g_p[:, :cap_p].reshape(np_seq, e * cap_p),
            idx_s[:, :cap_s].reshape(-1), g_s[:, :cap_s].reshape(-1))


def _gather_onehot_kernel(idx_ref, h_ref, o_ref):
    t_len = h_ref.shape[0]
    n_exp, cap, _ = o_ref.shape
    idx = idx_ref[0]
    onehot = jnp.where(idx == lax.broadcasted_iota(I32, (idx.shape[0], t_len), 1), 1.0, 0.0).astype(BF16)
    xs = jnp.dot(onehot, h_ref[...], preferred_element_type=F32).astype(BF16)
    for e in range(n_exp):
        o_ref[e] = xs[e * cap:(e + 1) * cap]


def _gather_onehot(idx_col, h, *, seq_len):
    n, d = h.shape
    n_seq, slots, _ = idx_col.shape
    cap = slots // N_EXPERTS
    return pl.pallas_call(
        _gather_onehot_kernel,
        grid=(n_seq,),
        in_specs=[pl.BlockSpec((1, slots, 1), lambda s: (s, 0, 0)),
                  pl.BlockSpec((seq_len, d), lambda s: (s, 0))],
        out_specs=pl.BlockSpec((N_EXPERTS, cap, d), lambda s: (0, s, 0)),
        out_shape=jax.ShapeDtypeStruct((N_EXPERTS, n_seq * cap, d), BF16),
        compiler_params=_params(1, 32),
        name="gather_onehot",
    )(idx_col, h)


def _gather_rows_kernel(idx_ref, h_ref, o_ref, tiles, *, cap, n_exp):
    tpr = _tiles_per_row(o_ref.shape[2])
    base = (pl.program_id(0) * n_exp + pl.program_id(1)) * cap

    def body(c, carry):
        src = pl.multiple_of(idx_ref[base + c] * tpr, tpr)
        tiles[pl.ds(pl.multiple_of(c * tpr, tpr), tpr), :] = h_ref[pl.ds(src, tpr), :]
        return carry

    lax.fori_loop(0, cap, body, 0, unroll=8)
    for s in range(tpr):
        o_ref[0, :, s * LANES:(s + 1) * LANES] = tiles[pl.ds(s, cap, stride=tpr), :].astype(BF16)


def _gather_rows(idx, h, d, *, seq_len):
    tpr = _tiles_per_row(d)
    n_seq = h.shape[0] // (seq_len * tpr)
    cap = CAPACITY_FACTOR * seq_len // N_EXPERTS
    return pl.pallas_call(
        functools.partial(_gather_rows_kernel, cap=cap, n_exp=N_EXPERTS),
        grid_spec=pltpu.PrefetchScalarGridSpec(
            num_scalar_prefetch=1,
            grid=(n_seq, N_EXPERTS),
            in_specs=[pl.BlockSpec((seq_len * tpr, LANES), lambda s, e, idx: (s, 0))],
            out_specs=pl.BlockSpec((1, cap, d), lambda s, e, idx: (e, s, 0)),
            scratch_shapes=[pltpu.VMEM((cap * tpr, LANES), F32)],
        ),
        out_shape=jax.ShapeDtypeStruct((N_EXPERTS, n_seq * cap, d), BF16),
        compiler_params=_params(2, 48),
        name="gather_rows",
    )(idx, h)


def _ffn_kernel(xp_ref, xs_ref, wg_ref, wu_ref, wd_ref, yp_ref, ys_ref):
    rc = 512

    @pl.when(pl.program_id(1) == 0)
    def _():
        for y_ref in (yp_ref, ys_ref):
            for r in range(0, y_ref.shape[1], rc):
                y_ref[0, r:r + rc, :] = jnp.zeros((rc, y_ref.shape[2]), F32)

    wg = wg_ref[0, 0].astype(BF16)
    wu = wu_ref[0, 0].astype(BF16)
    wd = wd_ref[0, 0].astype(BF16)
    for x_ref, y_ref in ((xp_ref, yp_ref), (xs_ref, ys_ref)):
        for r in range(0, x_ref.shape[1], rc):
            x = x_ref[0, r:r + rc, :]
            hg = jnp.dot(x, wg, preferred_element_type=F32)
            hu = jnp.dot(x, wu, preferred_element_type=F32)
            hid = ((hg * jax.nn.sigmoid(hg)) * hu).astype(BF16)
            y_ref[0, r:r + rc, :] += jnp.dot(hid, wd, preferred_element_type=F32)


def _ffn(xs_p, xs_s, w_gate, w_up, w_down, layer):
    n_exp, rp, d = xs_p.shape
    rs = xs_s.shape[1]
    ff = w_gate.shape[3]
    fc = 512
    return pl.pallas_call(
        _ffn_kernel,
        grid=(n_exp, ff // fc),
        in_specs=[
            pl.BlockSpec((1, rp, d), lambda e, f: (e, 0, 0)),
            pl.BlockSpec((1, rs, d), lambda e, f: (e, 0, 0)),
            pl.BlockSpec((1, 1, d, fc), lambda e, f: (layer, e, 0, f)),
            pl.BlockSpec((1, 1, d, fc), lambda e, f: (layer, e, 0, f)),
            pl.BlockSpec((1, 1, fc, d), lambda e, f: (layer, e, f, 0)),
        ],
        out_specs=[pl.BlockSpec((1, rp, d), lambda e, f: (e, 0, 0)),
                   pl.BlockSpec((1, rs, d), lambda e, f: (e, 0, 0))],
        out_shape=[jax.ShapeDtypeStruct((n_exp, rp, d), F32),
                   jax.ShapeDtypeStruct((n_exp, rs, d), F32)],
        compiler_params=_params(2, 56),
        name="expert_ffn",
    )(xs_p, xs_s, w_gate, w_up, w_down)


def _combine_onehot_kernel(idx_ref, g_ref, ye_ref, o_ref):
    t_len = o_ref.shape[0]
    n_exp = ye_ref.shape[0]
    idx = idx_ref[0]
    hit = idx == lax.broadcasted_iota(I32, (t_len, idx.shape[1]), 0)
    g_hi, g_lo = _split_bf16(g_ref[0])
    m_hi = jnp.where(hit, g_hi.astype(F32), 0.0).astype(BF16)
    m_lo = jnp.where(hit, g_lo.astype(F32), 0.0).astype(BF16)
    y_hi, y_lo = _split_bf16(jnp.concatenate([ye_ref[e] for e in range(n_exp)], axis=0))
    dot = functools.partial(jnp.dot, preferred_element_type=F32)
    o_ref[...] = dot(m_hi, y_hi) + (dot(m_lo, y_hi) + dot(m_hi, y_lo))


def _combine_onehot(idx_row, g_row, ye, *, seq_len):
    n_exp, rows, d = ye.shape
    n_seq, _, slots = idx_row.shape
    cap = slots // n_exp
    return pl.pallas_call(
        _combine_onehot_kernel,
        grid=(n_seq,),
        in_specs=[pl.BlockSpec((1, 1, slots), lambda s: (s, 0, 0)),
                  pl.BlockSpec((1, 1, slots), lambda s: (s, 0, 0)),
                  pl.BlockSpec((n_exp, cap, d), lambda s: (0, s, 0))],
        out_specs=pl.BlockSpec((seq_len, d), lambda s: (s, 0)),
        out_shape=jax.ShapeDtypeStruct((n_seq * seq_len, d), F32),
        compiler_params=_params(1, 32),
        name="combine_onehot",
    )(idx_row, g_row, ye)


COMBINE_GROUP = 8


def _combine_rows_kernel(idx_ref, g_ref, ye_ref, o_ref, tiles, *, cap, n_exp):
    e = pl.program_id(1)
    d = ye_ref.shape[2]
    tpr = _tiles_per_row(d)
    ch = 2048

    @pl.when(e == 0)
    def _():
        def zero(c, carry):
            o_ref[pl.ds(pl.multiple_of(c * ch, ch), ch), :] = jnp.zeros((ch, LANES), F32)
            return carry
        lax.fori_loop(0, o_ref.shape[0] // ch, zero, 0)

    for s in range(tpr):
        tiles[pl.ds(s, cap, stride=tpr), :] = ye_ref[0, :, s * LANES:(s + 1) * LANES]

    base = (pl.program_id(0) * n_exp + e) * cap

    def body(i, carry):
        c0 = i * COMBINE_GROUP
        dst = [pl.ds(pl.multiple_of(idx_ref[base + c0 + k] * tpr, tpr), tpr) for k in range(COMBINE_GROUP)]
        vals = [o_ref[dst[k], :]
                + g_ref[base + c0 + k] * tiles[pl.ds(pl.multiple_of((c0 + k) * tpr, tpr), tpr), :]
                for k in range(COMBINE_GROUP)]
        for k in range(COMBINE_GROUP):
            o_ref[dst[k], :] = vals[k]
        return carry

    lax.fori_loop(0, cap // COMBINE_GROUP, body, 0)


def _combine_rows(idx, g, ye, *, seq_len):
    n_exp, rows, d = ye.shape
    tpr = _tiles_per_row(d)
    cap = CAPACITY_FACTOR * seq_len // N_EXPERTS
    n_seq = rows // cap
    return pl.pallas_call(
        functools.partial(_combine_rows_kernel, cap=cap, n_exp=n_exp),
        grid_spec=pltpu.PrefetchScalarGridSpec(
            num_scalar_prefetch=2,
            grid=(n_seq, n_exp),
            in_specs=[pl.BlockSpec((1, cap, d), lambda s, e, i, gg: (e, s, 0))],
            out_specs=pl.BlockSpec((seq_len * tpr, LANES), lambda s, e, i, gg: (s, 0)),
            scratch_shapes=[pltpu.VMEM((cap * tpr, LANES), F32)],
        ),
        out_shape=jax.ShapeDtypeStruct((n_seq * seq_len * tpr, LANES), F32),
        compiler_params=_params(2, 48),
        name="combine_rows",
    )(idx, g, ye)


def _head_norm(x, gain):
    return x * lax.rsqrt(jnp.mean(x * x, axis=-1, keepdims=True) + EPS) * gain


def _qkv_kernel(*refs, rope, has_prev, mod_row, n_heads, n_kv):
    refs = list(refs)
    x_ref = refs.pop(0)
    prev = None
    if has_prev:
        acc_ref, pmod_ref = refs.pop(0), refs.pop(0)
    mod_ref, gain_ref, w_ref, qg_ref, kg_ref = refs[:5]
    refs = refs[5:]
    if rope:
        cos_ref, sin_ref = refs.pop(0), refs.pop(0)
    if has_prev:
        prev = (acc_ref, pmod_ref, refs.pop(0))
    q_ref, k_ref, v_ref = refs[:3]
    d = x_ref.shape[1]
    hd = HEAD_DIM
    m = mod_row(pl.program_id(0))
    x = _residual_stream(x_ref, prev, m)
    shift = mod_ref[pl.ds(m, 1), pl.ds(0, d)]
    scale = mod_ref[pl.ds(m, 1), pl.ds(d, d)]
    h = _modulated_norm(x, gain_ref[...], shift, scale).astype(BF16)
    qg = qg_ref[...]
    kg = kg_ref[...]
    q_scale = math.log2(math.e) * hd ** -0.5
    half = x_ref.shape[0] // 2
    for r0 in (0, half):
        rows = slice(r0, r0 + half)
        qkv = jnp.dot(h[rows], w_ref[...], preferred_element_type=F32)
        if rope:
            cos = cos_ref[rows, :]
            sin = sin_ref[rows, :]

            def rot(xh, cos=cos, sin=sin):
                return xh * cos + pltpu.roll(xh, hd // 2, 1) * sin
        else:
            rot = lambda xh: xh

        for i in range(n_heads):
            qh = rot(_head_norm(qkv[:, i * hd:(i + 1) * hd], qg)) * q_scale
            q_ref[rows, i * hd:(i + 1) * hd] = qh.astype(BF16)
        for i in range(n_kv):
            c0 = (n_heads + i) * hd
            kh = _head_norm(qkv[:, c0:c0 + hd], kg)
            if not rope:
                refs[3][rows, i * hd:(i + 1) * hd] = kh
            k_ref[rows, i * hd:(i + 1) * hd] = rot(kh).astype(BF16)
        v = qkv[:, (n_heads + n_kv) * hd:]
        v_ref[rows, :] = v.astype(BF16)
        if not rope:
            refs[4][rows, :] = v


def _qkv(x, prev, mod, gain, w_qkv, q_gain, k_gain, tables, *, seq_len, mod_row, n_heads, n_kv):
    n, d = x.shape
    hd = HEAD_DIM
    row = lambda i: (i, 0)
    fixed = lambda i: (0, 0)
    rope = tables is not None
    in_specs = [pl.BlockSpec((ROW_BLOCK, d), row)]
    args = [x]
    if prev is not None:
        in_specs += [_rows_spec(prev[0], d), pl.BlockSpec(prev[1].shape, fixed)]
        args += list(prev)
    in_specs += [
        pl.BlockSpec(mod.shape, fixed),
        pl.BlockSpec((1, d), fixed),
        pl.BlockSpec(w_qkv.shape, fixed),
        pl.BlockSpec((1, hd), fixed),
        pl.BlockSpec((1, hd), fixed),
    ]
    args += [mod, gain, w_qkv, q_gain, k_gain]
    if rope:
        per_seq = seq_len // ROW_BLOCK
        in_specs += [pl.BlockSpec((ROW_BLOCK, hd), lambda i: (i % per_seq, 0))] * 2
        args += list(tables)
    out_specs, out_shape = [], []
    if prev is not None:
        out_specs.append(pl.BlockSpec((ROW_BLOCK, d), row))
        out_shape.append(jax.ShapeDtypeStruct((n, d), F32))
    out_specs += [pl.BlockSpec((ROW_BLOCK, n_heads * hd), row),
                  pl.BlockSpec((ROW_BLOCK, n_kv * hd), row),
                  pl.BlockSpec((ROW_BLOCK, n_kv * hd), row)]
    out_shape += [jax.ShapeDtypeStruct((n, n_heads * hd), BF16),
                  jax.ShapeDtypeStruct((n, n_kv * hd), BF16),
                  jax.ShapeDtypeStruct((n, n_kv * hd), BF16)]
    if not rope:
        out_specs += [pl.BlockSpec((ROW_BLOCK, n_kv * hd), row)] * 2
        out_shape += [jax.ShapeDtypeStruct((n, n_kv * hd), F32)] * 2
    outs = pl.pallas_call(
        functools.partial(_qkv_kernel, rope=rope, has_prev=prev is not None, mod_row=mod_row,
                          n_heads=n_heads, n_kv=n_kv),
        grid=(n // ROW_BLOCK,),
        in_specs=in_specs,
        out_specs=out_specs,
        out_shape=out_shape,
        compiler_params=_params(1, 48),
        name="qkv_rope" if rope else "qkv",
    )(*args)
    return list(outs) if prev is not None else [x] + list(outs)


def _attn_kernel(*refs, group, chunk, has_cache):
    if has_cache:
        q_ref, k_ref, v_ref, kc_ref, vc_ref, o_ref = refs
    else:
        q_ref, k_ref, v_ref, o_ref = refs
    hd = HEAD_DIM
    qb = q_ref.shape[0]
    rows = group * qb
    q = jnp.concatenate([q_ref[:, g * hd:(g + 1) * hd] for g in range(group)], axis=0)
    sources = [(k_ref, v_ref, s0, chunk) for s0 in range(0, k_ref.shape[0], chunk)]
    if has_cache:
        sources.append((kc_ref.at[0], vc_ref.at[0], 0, kc_ref.shape[1]))
    m = jnp.full((rows, 1), -jnp.inf, F32)
    acc = jnp.zeros((rows, 2 * hd), F32)
    for kr, vr, s0, size in sources:
        s = lax.dot_general(q, kr[s0:s0 + size, :], (((1,), (1,)), ((), ())),
                            preferred_element_type=F32)
        m_new = jnp.maximum(m, jnp.max(s, axis=-1, keepdims=True))
        p = jnp.exp2(s - m_new).astype(BF16)
        v_ones = jnp.concatenate([vr[s0:s0 + size, :], jnp.ones((size, hd), BF16)], axis=1)
        acc = jnp.exp2(m - m_new) * acc + jnp.dot(p, v_ones, preferred_element_type=F32)
        m = m_new
    o = acc[:, :hd] / acc[:, hd:]
    o_ref[...] = jnp.concatenate([o[g * qb:(g + 1) * qb] for g in range(group)], axis=1).astype(BF16)


def _attention(q, k, v, cache, *, seq_len, qb, chunk, n_heads, n_kv):
    n = q.shape[0]
    hd = HEAD_DIM
    group = n_heads // n_kv
    nq = seq_len // qb
    in_specs = [
        pl.BlockSpec((qb, group * hd), lambda b, h, i: (b * nq + i, h)),
        pl.BlockSpec((seq_len, hd), lambda b, h, i: (b, h)),
        pl.BlockSpec((seq_len, hd), lambda b, h, i: (b, h)),
    ]
    args = [q, k, v]
    if cache is not None:
        past = cache[0].shape[1]
        in_specs += [pl.BlockSpec((1, past, hd), lambda b, h, i: (b, 0, h))] * 2
        args += list(cache)
    return pl.pallas_call(
        functools.partial(_attn_kernel, group=group, chunk=chunk, has_cache=cache is not None),
        grid=(n // seq_len, n_kv, nq),
        in_specs=in_specs,
        out_specs=pl.BlockSpec((qb, group * hd), lambda b, h, i: (b * nq + i, h)),
        out_shape=jax.ShapeDtypeStruct((n, n_heads * hd), BF16),
        compiler_params=_params(3, 48),
        name=f"attention_{seq_len}",
    )(*args)


def _final_norm_kernel(x_ref, acc_ref, pmod_ref, gain_ref, o_ref, *, mod_row):
    n_rows, d = x_ref.shape
    m = mod_row(pl.program_id(0))
    x = x_ref[...] + pmod_ref[pl.ds(m, 1), pl.ds(5 * d, d)] * _load_rows(acc_ref, n_rows, d)
    o_ref[...] = (x * lax.rsqrt(jnp.mean(x * x, axis=-1, keepdims=True) + EPS)) * gain_ref[...]


def _final_norm(x, acc, pmod, gain, mod_row):
    n, d = x.shape
    row = lambda i: (i, 0)
    fixed = lambda i: (0, 0)
    return pl.pallas_call(
        functools.partial(_final_norm_kernel, mod_row=mod_row),
        grid=(n // ROW_BLOCK,),
        in_specs=[pl.BlockSpec((ROW_BLOCK, d), row), _rows_spec(acc, d),
                  pl.BlockSpec(pmod.shape, fixed), pl.BlockSpec((1, d), fixed)],
        out_specs=pl.BlockSpec((ROW_BLOCK, d), row),
        out_shape=jax.ShapeDtypeStruct((n, d), F32),
        compiler_params=_params(1, 32),
        name="final_norm",
    )(x, acc, pmod, gain)


def _rope_head_order():
    q = HEAD_DIM // 4
    return jnp.concatenate([jnp.arange(0, q), jnp.arange(2 * q, 3 * q), jnp.arange(q, 2 * q),
                            jnp.arange(3 * q, 4 * q)])


def _rope_tables(seq_len):
    axis = HEAD_DIM // 2
    t = jnp.arange(seq_len)
    inv = ROPE_BASE ** (-jnp.arange(axis // 2, dtype=F32) * 2.0 / axis)
    ang_row = (t // GRID_W).astype(F32)[:, None] * inv
    ang_col = (t % GRID_W).astype(F32)[:, None] * inv
    ang = jnp.concatenate([ang_row, ang_col], axis=1)
    c, s = jnp.cos(ang), jnp.sin(ang)
    return jnp.concatenate([c, c], axis=1), jnp.concatenate([-s, s], axis=1)


def kernel(x_prompt, x_sample, state_lru, cache_k, cache_v, c, c_ctx, w_mod, b_mod, norm_gain, final_gain,
           w_lru_in, lru_conv_w, lru_conv_b, lru_wa, lru_ba, lru_wx, lru_bx, lru_lambda, w_lru_out,
           w_qkv, q_norm, k_norm, w_attn_out, w_router, w_exp_gate, w_exp_up, w_exp_down):
    batch, seq, d = x_prompt.shape
    dec_batch, dec_seq, _ = x_sample.shape
    depth = w_mod.shape[0]
    n_kv = cache_k.shape[3]
    n_heads = w_attn_out.shape[1] // HEAD_DIM
    heads = dict(n_heads=n_heads, n_kv=n_kv)

    cond = jnp.zeros((SUBLANES, d), F32).at[0].set(c_ctx).at[1:1 + dec_batch].set(c)
    mod_all = _mod_vectors(cond, w_mod, b_mod)

    blocks_per_seq = dec_seq // ROW_BLOCK
    xs = [x_prompt.reshape(batch * seq, d), x_sample.reshape(dec_batch * dec_seq, d)]
    lens = [seq, dec_seq]
    mod_rows = [lambda i: 0, lambda i: 1 + i // blocks_per_seq]
    prevs = [None, None]

    new_lru, new_k, new_v = [], [], []
    for l in range(depth):
        mod = mod_all[l]
        gain_a = norm_gain[l, 0].reshape(1, d)
        gain_c = norm_gain[l, 1].reshape(1, d)
        mixed = []
        if l % 2 == 0:
            li = l // 2
            w_in = w_lru_in[li].astype(BF16)
            wg = (0.5 * jnp.concatenate([lru_wa[li, 0], lru_wx[li, 0], lru_wa[li, 1], lru_wx[li, 1]],
                                        axis=2)).astype(BF16)
            bias4 = 0.5 * jnp.stack([lru_ba[li, 0], lru_bx[li, 0], lru_ba[li, 1], lru_bx[li, 1]])
            h0s = [jnp.zeros((batch, 2, w_in.shape[1] // 2), F32), state_lru[:, li]]
            for gi in range(2):
                xs[gi], gate, xr = _lru_in(xs[gi], prevs[gi], mod, gain_a, w_in, mod_rows[gi])
                y, fin = _lru_core(xr, gate, lru_conv_w[li], lru_conv_b[li].reshape(1, -1), wg, bias4,
                                   lru_lambda[li], h0s[gi], seq_len=lens[gi])
                mixed.append(y)
                if gi == 0:
                    new_lru.append(fin.astype(x_prompt.dtype))
            w_out = w_lru_out[li].astype(BF16)
        else:
            ai = l // 2
            w = w_qkv[ai].astype(BF16)
            qg = q_norm[ai].reshape(1, -1)
            kg = k_norm[ai].reshape(1, -1)
            xs[0], q_p, k_p, v_p, kf, vf = _qkv(xs[0], prevs[0], mod, gain_a, w, qg, kg, None,
                                                seq_len=seq, mod_row=mod_rows[0], **heads)
            order = _rope_head_order()
            n_qk = (n_heads + n_kv) * HEAD_DIM
            w_qk = w_qkv[ai][:, :n_qk].reshape(d, n_heads + n_kv, HEAD_DIM)[:, :, order].reshape(d, n_qk)
            w_r = jnp.concatenate([w_qk, w_qkv[ai][:, n_qk:]], axis=1).astype(BF16)
            xs[1], q_s, k_s, v_s = _qkv(xs[1], prevs[1], mod, gain_a, w_r, qg[:, order], kg[:, order],
                                        _rope_tables(dec_seq),
                                        seq_len=dec_seq, mod_row=mod_rows[1], **heads)
            new_k.append(kf.reshape(batch, seq, n_kv, HEAD_DIM))
            new_v.append(vf.reshape(batch, seq, n_kv, HEAD_DIM))
            past = cache_k.shape[2]
            kc = cache_k[:, ai][..., order].reshape(dec_batch, past, n_kv * HEAD_DIM).astype(BF16)
            vc = cache_v[:, ai].reshape(dec_batch, past, n_kv * HEAD_DIM).astype(BF16)
            mixed.append(_attention(q_p, k_p, v_p, None, seq_len=seq, qb=seq, chunk=seq, **heads))
            mixed.append(_attention(q_s, k_s, v_s, (kc, vc), seq_len=dec_seq, qb=256, chunk=256, **heads))
            w_out = w_attn_out[ai].astype(BF16)

        wr_hi, wr_lo = _split_bf16(w_router[l].T)
        x1_p, h2_p, lg_p = _mix_out(mixed[0], xs[0], mod, gain_c, w_out, wr_hi, wr_lo, mod_rows[0], False)
        x1_s, h2_s, lg_s = _mix_out(mixed[1], xs[1], mod, gain_c, w_out, wr_hi, wr_lo, mod_rows[1], True)
        idx_p, g_p, idx_s, g_s = _route(lg_p, lg_s, seq, dec_seq)
        xe_p = _gather_onehot(idx_p[:, :, None], h2_p, seq_len=seq)
        xe_s = _gather_rows(idx_s, h2_s, d, seq_len=dec_seq)
        ye_p, ye_s = _ffn(xe_p, xe_s, w_exp_gate, w_exp_up, w_exp_down, l)
        acc_p = _combine_onehot(idx_p[:, None, :], g_p[:, None, :], ye_p, seq_len=seq)
        acc_s = _combine_rows(idx_s, g_s, ye_s, seq_len=dec_seq)
        xs = [x1_p, x1_s]
        prevs = [(acc_p, mod), (acc_s, mod)]

    fg = final_gain.reshape(1, d)
    y_prompt = _final_norm(xs[0], *prevs[0], fg, mod_rows[0]).reshape(batch, seq, d)
    y_sample = _final_norm(xs[1], *prevs[1], fg, mod_rows[1]).reshape(dec_batch, dec_seq, d)
    return (y_prompt, y_sample, jnp.stack(new_lru, axis=1), jnp.stack(new_k, axis=1), jnp.stack(new_v, axis=1))
```

```python
import functools
import math

import jax
import jax.numpy as jnp
from jax import lax
from jax.experimental import pallas as pl
from jax.experimental.pallas import tpu as pltpu

F32 = jnp.float32
BF16 = jnp.bfloat16
I32 = jnp.int32

LANES = 128
SUBLANES = 8
MIB = 1024 * 1024

RG_C = 8.0
CONV_W = 4
CONV_PAD_L = 2
ROPE_BASE = 10000.0
GRID_W = 64
EPS = 1e-6
N_EXPERTS = 16
CAPACITY_FACTOR = 2
HEAD_DIM = 128
LRU_BLOCKS = 8
LRU_STEP_BLOCKS = 2

ROW_BLOCK = 512


def _params(n_axes, vmem_mib):
    return pltpu.CompilerParams(
        dimension_semantics=("arbitrary",) * n_axes, vmem_limit_bytes=vmem_mib * MIB)


def _split_bf16(x):
    hi = x.astype(BF16)
    lo = (x - hi.astype(F32)).astype(BF16)
    return hi, lo


def _modulated_norm(x, gain, shift, scale):
    y = x * lax.rsqrt(jnp.mean(x * x, axis=-1, keepdims=True) + EPS)
    return (y * gain) * (1.0 + scale) + shift


def _tiles_per_row(d):
    return d // LANES


def _load_rows(ref, n_rows, d):
    if ref.shape == (n_rows, d):
        return ref[...]
    tpr = _tiles_per_row(d)
    return jnp.concatenate([ref[pl.ds(s, n_rows, stride=tpr), :] for s in range(tpr)], axis=1)


def _store_rows(ref, x, r0):
    n_rows, d = x.shape
    if ref.shape[1] == d:
        ref[r0:r0 + n_rows, :] = x.astype(ref.dtype)
        return
    tpr = _tiles_per_row(d)
    for s in range(tpr):
        ref[pl.ds(r0 * tpr + s, n_rows, stride=tpr), :] = x[:, s * LANES:(s + 1) * LANES]


def _rows_spec(arr, d):
    if arr.shape[1] == d:
        return pl.BlockSpec((ROW_BLOCK, d), lambda i: (i, 0))
    return pl.BlockSpec((ROW_BLOCK * _tiles_per_row(d), LANES), lambda i: (i, 0))


def _residual_stream(x_ref, prev, m):
    if prev is None:
        return x_ref[...]
    acc_ref, pmod_ref, xo_ref = prev
    n_rows, d = x_ref.shape
    x = x_ref[...] + pmod_ref[pl.ds(m, 1), pl.ds(5 * d, d)] * _load_rows(acc_ref, n_rows, d)
    xo_ref[...] = x
    return x


def _mod_kernel(cond_ref, w_ref, b_ref, o_ref):
    c = cond_ref[...]
    a_hi, a_lo = _split_bf16(c * jax.nn.sigmoid(c))
    w_hi, w_lo = _split_bf16(w_ref[0])
    dot = functools.partial(jnp.dot, preferred_element_type=F32)
    o_ref[0] = dot(a_hi, w_hi) + (dot(a_lo, w_hi) + dot(a_hi, w_lo)) + b_ref[0]


def _mod_vectors(cond8, w_mod, b_mod):
    depth, d, d6 = w_mod.shape
    nb = 1536
    return pl.pallas_call(
        _mod_kernel,
        grid=(depth, d6 // nb),
        in_specs=[
            pl.BlockSpec((SUBLANES, d), lambda l, j: (0, 0)),
            pl.BlockSpec((1, d, nb), lambda l, j: (l, 0, j)),
            pl.BlockSpec((1, 1, nb), lambda l, j: (l, 0, j)),
        ],
        out_specs=pl.BlockSpec((1, SUBLANES, nb), lambda l, j: (l, 0, j)),
        out_shape=jax.ShapeDtypeStruct((depth, SUBLANES, d6), F32),
        compiler_params=_params(2, 40),
        name="adaln_mod",
    )(cond8, w_mod, b_mod.reshape(depth, 1, d6))


def _lru_in_kernel(*refs, mod_row, has_prev):
    if has_prev:
        x_ref, acc_ref, pmod_ref, mod_ref, gain_ref, w_ref, xo_ref, gate_ref, xr_ref = refs
        prev = (acc_ref, pmod_ref, xo_ref)
    else:
        x_ref, mod_ref, gain_ref, w_ref, gate_ref, xr_ref = refs
        prev = None
    d = x_ref.shape[1]
    m = mod_row(pl.program_id(0))
    x = _residual_stream(x_ref, prev, m)
    shift = mod_ref[pl.ds(m, 1), pl.ds(0, d)]
    scale = mod_ref[pl.ds(m, 1), pl.ds(d, d)]
    h = _modulated_norm(x, gain_ref[...], shift, scale).astype(BF16)
    u = jnp.dot(h, w_ref[...], preferred_element_type=F32)
    w = gate_ref.shape[1]
    gate_ref[...] = u[:, :w]
    xr_ref[...] = u[:, w:]


def _lru_in(x, prev, mod, gain, w_in, mod_row):
    n, d = x.shape
    w2 = w_in.shape[1]
    w = w2 // 2
    row = lambda i: (i, 0)
    fixed = lambda i: (0, 0)
    in_specs = [pl.BlockSpec((ROW_BLOCK, d), row)]
    args = [x]
    out_specs = [pl.BlockSpec((ROW_BLOCK, w), row), pl.BlockSpec((ROW_BLOCK, w), row)]
    out_shape = [jax.ShapeDtypeStruct((n, w), F32), jax.ShapeDtypeStruct((n, w), F32)]
    if prev is not None:
        in_specs += [_rows_spec(prev[0], d), pl.BlockSpec(prev[1].shape, fixed)]
        args += list(prev)
        out_specs = [pl.BlockSpec((ROW_BLOCK, d), row)] + out_specs
        out_shape = [jax.ShapeDtypeStruct((n, d), F32)] + out_shape
    in_specs += [pl.BlockSpec(mod.shape, fixed), pl.BlockSpec((1, d), fixed), pl.BlockSpec((d, w2), fixed)]
    args += [mod, gain, w_in]
    outs = pl.pallas_call(
        functools.partial(_lru_in_kernel, mod_row=mod_row, has_prev=prev is not None),
        grid=(n // ROW_BLOCK,),
        in_specs=in_specs,
        out_specs=out_specs,
        out_shape=out_shape,
        compiler_params=_params(1, 48),
        name="lru_in",
    )(*args)
    return outs if prev is not None else [x] + list(outs)


def _lru_core_kernel(xr_ref, gate_ref, cw_ref, cb_ref, wg_ref, bias_ref, lam_ref, h0_ref,
                     y_ref, fin_ref, pad, a_f, b_f, a_b, b_b):
    t_len, lb = xr_ref.shape
    ch = 256
    halo = SUBLANES
    seg = t_len // SUBLANES
    piece = min(ch, seg)
    seg_shift = seg.bit_length() - 1

    def seg_rows(t0):
        r = lax.shift_right_logical(t0, seg_shift)
        return pl.ds((t0 - r * seg) * SUBLANES + r, piece, stride=SUBLANES)

    zero_rows = jnp.zeros((halo, lb), F32)
    pad[0:halo, :] = zero_rows
    pad[t_len + halo:t_len + 2 * halo, :] = zero_rows

    def copy_in(c, carry):
        r = pl.multiple_of(c * ch, ch)
        pad[pl.ds(r + halo, ch), :] = xr_ref[pl.ds(r, ch), :]
        return carry

    lax.fori_loop(0, t_len // ch, copy_in, 0)

    neg_lam = -lam_ref[...]
    softplus = jnp.maximum(neg_lam, 0.0) + jnp.log1p(jnp.exp(-jnp.abs(neg_lam)))
    c_nla = (0.5 * RG_C) * softplus
    c_exp2 = (-0.5 * RG_C * math.log2(math.e)) * softplus
    cw = cw_ref[...]
    cbias = cb_ref[...]
    half_bias = bias_ref[...]
    half_wg = wg_ref[...]
    bw = half_wg.shape[1]
    n_slab = lb // LANES

    def gates(c, carry):
        r = pl.multiple_of(c * ch, ch)
        blk = pad[pl.ds(r, ch + 2 * halo), :]
        xc = cbias
        for k in range(CONV_W):
            o = halo - CONV_PAD_L + k
            xc = xc + blk[o:o + ch] * cw[k:k + 1]
        xc_bf = xc.astype(BF16)
        g = [jnp.dot(xc_bf[:, j * bw:(j + 1) * bw], half_wg[j], preferred_element_type=F32)
             for j in range(lb // bw)]
        pre = lambda k: jnp.concatenate([gj[:, k * bw:(k + 1) * bw] for gj in g], axis=1) + half_bias[k:k + 1]
        half_xc = 0.5 * xc
        for d, (a_ref, b_ref) in enumerate(((a_f, b_f), (a_b, b_b))):
            u = jnp.tanh(pre(2 * d)) + 1.0
            t_i = jnp.tanh(pre(2 * d + 1))
            a = jnp.exp2(u * c_exp2[d:d + 1])
            b = jnp.sqrt(jnp.tanh(u * c_nla[d:d + 1]) * (1.0 + a * a)) * (t_i * half_xc + half_xc)
            for p in range(ch // piece):
                rows = seg_rows(r + p * piece)
                for j in range(n_slab):
                    a_ref[j, rows, :] = a[p * piece:(p + 1) * piece, j * LANES:(j + 1) * LANES]
                    b_ref[j, rows, :] = b[p * piece:(p + 1) * piece, j * LANES:(j + 1) * LANES]
        return carry

    lax.fori_loop(0, t_len // ch, gates, 0)

    def scan(i, carry):
        rows = (pl.ds(pl.multiple_of(i * SUBLANES, SUBLANES), SUBLANES),
                pl.ds(pl.multiple_of((seg - 1 - i) * SUBLANES, SUBLANES), SUBLANES))
        out = []
        for k, (h, prod) in enumerate(carry):
            d, j = divmod(k, n_slab)
            a_ref, b_ref = ((a_f, b_f), (a_b, b_b))[d]
            a = a_ref[j, rows[d], :]
            h = a * h + b_ref[j, rows[d], :]
            prod = a * prod
            b_ref[j, rows[d], :] = h
            a_ref[j, rows[d], :] = prod
            out.append((h, prod))
        return tuple(out)

    start = (jnp.zeros((SUBLANES, LANES), F32), jnp.ones((SUBLANES, LANES), F32))
    ends = lax.fori_loop(0, seg, scan, (start,) * (2 * n_slab), unroll=8)

    enter = []
    for k, (h, prod) in enumerate(ends):
        d, j = divmod(k, n_slab)
        states = [h0_ref[0, d:d + 1, j * LANES:(j + 1) * LANES]]
        for r in (range(SUBLANES) if d == 0 else reversed(range(SUBLANES))):
            states.append(h[r:r + 1] + prod[r:r + 1] * states[-1])
        fin_ref[0, d:d + 1, j * LANES:(j + 1) * LANES] = states[SUBLANES]
        order = states[:SUBLANES] if d == 0 else states[:SUBLANES][::-1]
        enter.append(jnp.concatenate(order, axis=0))

    def fix(i, carry):
        rows = pl.ds(pl.multiple_of(i * SUBLANES, SUBLANES), SUBLANES)
        for j in range(n_slab):
            b_f[j, rows, :] = ((b_f[j, rows, :] + a_f[j, rows, :] * enter[j])
                               + (b_b[j, rows, :] + a_b[j, rows, :] * enter[n_slab + j]))
        return carry

    lax.fori_loop(0, seg, fix, 0, unroll=8)

    def emit(c, carry):
        r = pl.multiple_of(c * ch, ch)
        gt = gate_ref[pl.ds(r, ch), :]
        cdf = 0.5 * (1.0 + jnp.tanh(math.sqrt(2.0 / math.pi) * (gt + 0.044715 * (gt * gt * gt))))
        y = gt * cdf
        for p in range(ch // piece):
            rows = seg_rows(r + p * piece)
            total = jnp.concatenate([b_f[j, rows, :] for j in range(n_slab)], axis=1)
            y_ref[pl.ds(r + p * piece, piece), :] = (total * y[p * piece:(p + 1) * piece]).astype(BF16)
        return carry

    lax.fori_loop(0, t_len // ch, emit, 0)


def _lru_core(xr, gate, conv_w, conv_b, wg, bias4, lam, h0, *, seq_len):
    n, w = xr.shape
    n_seq = n // seq_len
    bw = w // LRU_BLOCKS
    lb = LRU_STEP_BLOCKS * bw
    tok = lambda s, c: (s, c)
    chan = lambda s, c: (0, c)
    return pl.pallas_call(
        _lru_core_kernel,
        grid=(n_seq, LRU_BLOCKS // LRU_STEP_BLOCKS),
        in_specs=[
            pl.BlockSpec((seq_len, lb), tok),
            pl.BlockSpec((seq_len, lb), tok),
            pl.BlockSpec((CONV_W, lb), chan),
            pl.BlockSpec((1, lb), chan),
            pl.BlockSpec((LRU_STEP_BLOCKS, bw, 4 * bw), lambda s, c: (c, 0, 0)),
            pl.BlockSpec((4, lb), chan),
            pl.BlockSpec((2, lb), chan),
            pl.BlockSpec((1, 2, lb), lambda s, c: (s, 0, c)),
        ],
        out_specs=[pl.BlockSpec((seq_len, lb), tok),
                   pl.BlockSpec((1, 2, lb), lambda s, c: (s, 0, c))],
        out_shape=[jax.ShapeDtypeStruct((n, w), BF16), jax.ShapeDtypeStruct((n_seq, 2, w), F32)],
        scratch_shapes=[pltpu.VMEM((seq_len + 2 * SUBLANES, lb), F32)]
        + [pltpu.VMEM((lb // LANES, seq_len, LANES), F32)] * 4,
        compiler_params=_params(2, 48),
        name=f"lru_core_{seq_len}",
    )(xr, gate, conv_w, conv_b, wg, bias4, lam, h0)


def _mix_out_kernel(a_ref, x_ref, mod_ref, gain_ref, w_ref, wrh_ref, wrl_ref,
                    x1_ref, h2_ref, lg_ref, *, mod_row):
    d = x_ref.shape[1]
    m = mod_row(pl.program_id(0))
    g_mix = mod_ref[pl.ds(m, 1), pl.ds(2 * d, d)]
    shift = mod_ref[pl.ds(m, 1), pl.ds(3 * d, d)]
    scale = mod_ref[pl.ds(m, 1), pl.ds(4 * d, d)]
    op = jnp.dot(a_ref[...], w_ref[...], preferred_element_type=F32)
    x1 = x_ref[...] + g_mix * op
    x1_ref[...] = x1
    h2 = _modulated_norm(x1, gain_ref[...], shift, scale)
    h_hi, h_lo = _split_bf16(h2)
    _store_rows(h2_ref, h2, 0)
    nt = functools.partial(lax.dot_general, dimension_numbers=(((1,), (1,)), ((), ())),
                           preferred_element_type=F32)
    lg_ref[...] = nt(wrh_ref[...], h_hi) + (nt(wrl_ref[...], h_hi) + nt(wrh_ref[...], h_lo))


def _mix_out(a, x, mod, gain, w_out, wr_hi, wr_lo, mod_row, token_tiled):
    n, d = x.shape
    e = wr_hi.shape[0]
    row = lambda i: (i, 0)
    fixed = lambda i: (0, 0)
    tpr = _tiles_per_row(d)
    if token_tiled:
        h2_spec = pl.BlockSpec((ROW_BLOCK * tpr, LANES), row)
        h2_shape = jax.ShapeDtypeStruct((n * tpr, LANES), F32)
    else:
        h2_spec = pl.BlockSpec((ROW_BLOCK, d), row)
        h2_shape = jax.ShapeDtypeStruct((n, d), BF16)
    return pl.pallas_call(
        functools.partial(_mix_out_kernel, mod_row=mod_row),
        grid=(n // ROW_BLOCK,),
        in_specs=[
            pl.BlockSpec((ROW_BLOCK, a.shape[1]), row),
            pl.BlockSpec((ROW_BLOCK, d), row),
            pl.BlockSpec(mod.shape, fixed),
            pl.BlockSpec((1, d), fixed),
            pl.BlockSpec(w_out.shape, fixed),
            pl.BlockSpec(wr_hi.shape, fixed),
            pl.BlockSpec(wr_lo.shape, fixed),
        ],
        out_specs=[pl.BlockSpec((ROW_BLOCK, d), row), h2_spec,
                   pl.BlockSpec((e, ROW_BLOCK), lambda i: (0, i))],
        out_shape=[jax.ShapeDtypeStruct((n, d), F32), h2_shape, jax.ShapeDtypeStruct((e, n), F32)],
        compiler_params=_params(1, 48),
        name="mix_out",
    )(a, x, mod, gain, w_out, wr_hi, wr_lo)


def _lane_sum(tiles):
    acc = tiles[0]
    for t in tiles[1:]:
        acc = acc + t
    return jnp.sum(acc, axis=1, keepdims=True)


def _exclusive_cumsum(flags, upper):
    out = []
    off = jnp.zeros((flags[0].shape[0], 1), F32)
    for f in flags:
        out.append(jnp.dot(f.astype(BF16), upper, preferred_element_type=F32) + off)
        off = off + jnp.sum(f, axis=1, keepdims=True)
    return out


def _route_group(lg_ref, n_seq, t_len, idx_ref, g_ref):
    cap = CAPACITY_FACTOR * t_len // N_EXPERTS
    affs = []
    for s in range(n_seq):
        lg = lg_ref[:, s * t_len:(s + 1) * t_len]
        ex = jnp.exp(lg - jnp.max(lg, axis=0, keepdims=True))
        affs.append(ex / jnp.sum(ex, axis=0, keepdims=True))
    aff = jnp.concatenate(affs, axis=0)
    n_rows = aff.shape[0]
    nt = t_len // LANES
    g = [aff[:, j * LANES:(j + 1) * LANES] for j in range(nt)]

    kth_bits = jnp.zeros((n_rows, 1), I32)
    for bit in range(30, -1, -1):
        cand = kth_bits | (1 << bit)
        cand_f = pltpu.bitcast(cand, F32)
        cnt = _lane_sum([jnp.where(t >= cand_f, 1, 0) for t in g])
        kth_bits = jnp.where(cnt >= cap, cand, kth_bits)
    kth = pltpu.bitcast(kth_bits, F32)

    lane = lax.broadcasted_iota(I32, (LANES, LANES), 0)
    upper = jnp.where(lane < lax.broadcasted_iota(I32, (LANES, LANES), 1), 1.0, 0.0).astype(BF16)
    gt = [t > kth for t in g]
    eq = [t == kth for t in g]
    need = (cap - _lane_sum([jnp.where(m, 1, 0) for m in gt])).astype(F32)
    eq_rank = _exclusive_cumsum([jnp.where(m, 1.0, 0.0) for m in eq], upper)
    sel = [jnp.logical_or(gt[j], jnp.logical_and(eq[j], eq_rank[j] < need)) for j in range(nt)]
    pos = _exclusive_cumsum([jnp.where(m, 1.0, 0.0) for m in sel], upper)

    lane_r = lax.broadcasted_iota(I32, (n_rows, LANES), 1)
    d = [jnp.where(sel[j], lane_r + j * LANES - pos[j].astype(I32), -1) for j in range(nt)]
    for k in range(t_len.bit_length() - 1):
        s = 1 << k
        if s < LANES:
            d_rot = [pltpu.roll(x, LANES - s, 1) for x in d]
            g_rot = [pltpu.roll(x, LANES - s, 1) for x in g]
            same = lane_r < LANES - s
            d_in = [jnp.where(same, d_rot[j], d_rot[(j + 1) % nt]) for j in range(nt)]
            g_in = [jnp.where(same, g_rot[j], g_rot[(j + 1) % nt]) for j in range(nt)]
        else:
            q = s // LANES
            d_in = [d[(j + q) % nt] for j in range(nt)]
            g_in = [g[(j + q) % nt] for j in range(nt)]
        new_d, new_g = [], []
        for j in range(nt):
            move = jnp.logical_and(d_in[j] >= 0, ((d_in[j] >> k) & 1) == 1)
            stay = jnp.logical_and(d[j] >= 0, ((d[j] >> k) & 1) == 0)
            new_d.append(jnp.where(move, d_in[j], jnp.where(stay, d[j], -1)))
            new_g.append(jnp.where(move, g_in[j], g[j]))
        d, g = new_d, new_g

    for j in range(idx_ref.shape[1] // LANES):
        idx_ref[:, j * LANES:(j + 1) * LANES] = lane_r + j * LANES + d[j]
        g_ref[:, j * LANES:(j + 1) * LANES] = g[j]


def _route_kernel(lgp_ref, lgs_ref, idx_p_ref, g_p_ref, idx_s_ref, g_s_ref, *, p_len, s_len):
    _route_group(lgp_ref, lgp_ref.shape[1] // p_len, p_len, idx_p_ref, g_p_ref)
    _route_group(lgs_ref, lgs_ref.shape[1] // s_len, s_len, idx_s_ref, g_s_ref)


def _route(lg_p, lg_s, p_len, s_len):
    e = lg_p.shape[0]
    np_seq = lg_p.shape[1] // p_len
    ns_seq = lg_s.shape[1] // s_len
    cap_p = CAPACITY_FACTOR * p_len // N_EXPERTS
    cap_s = CAPACITY_FACTOR * s_len // N_EXPERTS
    wp = max(cap_p, LANES)
    ws = max(cap_s, LANES)
    idx_p, g_p, idx_s, g_s = pl.pallas_call(
        functools.partial(_route_kernel, p_len=p_len, s_len=s_len),
        out_shape=[jax.ShapeDtypeStruct((np_seq * e, wp), I32),
                   jax.ShapeDtypeStruct((np_seq * e, wp), F32),
                   jax.ShapeDtypeStruct((ns_seq * e, ws), I32),
                   jax.ShapeDtypeStruct((ns_seq * e, ws), F32)],
        compiler_params=pltpu.CompilerParams(vmem_limit_bytes=40 * MIB),
        name="route",
    )(lg_p, lg_s)
    return (idx_p[:, :cap_p].reshape(np_seq, e * cap_p), g_p[:, :cap_p].reshape(np_seq, e * cap_p),
            idx_s[:, :cap_s].reshape(-1), g_s[:, :cap_s].reshape(-1))


def _gather_onehot_kernel(idx_ref, h_ref, o_ref):
    t_len = h_ref.shape[0]
    n_exp, cap, _ = o_ref.shape
    idx = idx_ref[0]
    onehot = jnp.where(idx == lax.broadcasted_iota(I32, (idx.shape[0], t_len), 1), 1.0, 0.0).astype(BF16)
    xs = jnp.dot(onehot, h_ref[...], preferred_element_type=F32).astype(BF16)
    for e in range(n_exp):
        o_ref[e] = xs[e * cap:(e + 1) * cap]


def _gather_onehot(idx_col, h, *, seq_len):
    n, d = h.shape
    n_seq, slots, _ = idx_col.shape
    cap = slots // N_EXPERTS
    return pl.pallas_call(
        _gather_onehot_kernel,
        grid=(n_seq,),
        in_specs=[pl.BlockSpec((1, slots, 1), lambda s: (s, 0, 0)),
                  pl.BlockSpec((seq_len, d), lambda s: (s, 0))],
        out_specs=pl.BlockSpec((N_EXPERTS, cap, d), lambda s: (0, s, 0)),
        out_shape=jax.ShapeDtypeStruct((N_EXPERTS, n_seq * cap, d), BF16),
        compiler_params=_params(1, 32),
        name="gather_onehot",
    )(idx_col, h)


def _gather_rows_kernel(idx_ref, h_ref, o_ref, tiles, *, cap, n_exp):
    tpr = _tiles_per_row(o_ref.shape[2])
    base = (pl.program_id(0) * n_exp + pl.program_id(1)) * cap

    def body(c, carry):
        src = pl.multiple_of(idx_ref[base + c], tpr)
        tiles[pl.ds(pl.multiple_of(c * tpr, tpr), tpr), :] = h_ref[pl.ds(src, tpr), :]
        return carry

    lax.fori_loop(0, cap, body, 0, unroll=8)
    for s in range(tpr):
        o_ref[0, :, s * LANES:(s + 1) * LANES] = tiles[pl.ds(s, cap, stride=tpr), :].astype(BF16)


def _gather_rows(idx, h, d, *, seq_len):
    tpr = _tiles_per_row(d)
    n_seq = h.shape[0] // (seq_len * tpr)
    cap = CAPACITY_FACTOR * seq_len // N_EXPERTS
    return pl.pallas_call(
        functools.partial(_gather_rows_kernel, cap=cap, n_exp=N_EXPERTS),
        grid_spec=pltpu.PrefetchScalarGridSpec(
            num_scalar_prefetch=1,
            grid=(n_seq, N_EXPERTS),
            in_specs=[pl.BlockSpec((seq_len * tpr, LANES), lambda s, e, idx: (s, 0))],
            out_specs=pl.BlockSpec((1, cap, d), lambda s, e, idx: (e, s, 0)),
            scratch_shapes=[pltpu.VMEM((cap * tpr, LANES), F32)],
        ),
        out_shape=jax.ShapeDtypeStruct((N_EXPERTS, n_seq * cap, d), BF16),
        compiler_params=_params(2, 48),
        name="gather_rows",
    )(idx, h)


def _ffn_kernel(xp_ref, xs_ref, gp_ref, gs_ref, wg_ref, wu_ref, wd_ref, yp_ref, ys_ref):
    rc = 512

    @pl.when(pl.program_id(1) == 0)
    def _():
        for y_ref in (yp_ref, ys_ref):
            for r in range(0, y_ref.shape[1], rc):
                y_ref[0, r:r + rc, :] = jnp.zeros((rc, y_ref.shape[2]), F32)

    wg = wg_ref[0, 0].astype(BF16)
    wu = wu_ref[0, 0].astype(BF16)
    wd = wd_ref[0, 0].astype(BF16)
    for x_ref, y_ref in ((xp_ref, yp_ref), (xs_ref, ys_ref)):
        for r in range(0, x_ref.shape[1], rc):
            x = x_ref[0, r:r + rc, :]
            hg = jnp.dot(x, wg, preferred_element_type=F32)
            hu = jnp.dot(x, wu, preferred_element_type=F32)
            hid = ((hg * jax.nn.sigmoid(hg)) * hu).astype(BF16)
            y_ref[0, r:r + rc, :] += jnp.dot(hid, wd, preferred_element_type=F32)

    @pl.when(pl.program_id(1) == pl.num_programs(1) - 1)
    def _():
        for g_ref, y_ref in ((gp_ref, yp_ref), (gs_ref, ys_ref)):
            for r in range(0, y_ref.shape[1], rc):
                y_ref[0, r:r + rc, :] = y_ref[0, r:r + rc, :] * g_ref[0, r:r + rc, :]


def _ffn(xs_p, xs_s, g_p, g_s, w_gate, w_up, w_down, layer):
    n_exp, rp, d = xs_p.shape
    rs = xs_s.shape[1]
    ff = w_gate.shape[3]
    fc = 512
    return pl.pallas_call(
        _ffn_kernel,
        grid=(n_exp, ff // fc),
        in_specs=[
            pl.BlockSpec((1, rp, d), lambda e, f: (e, 0, 0)),
            pl.BlockSpec((1, rs, d), lambda e, f: (e, 0, 0)),
            pl.BlockSpec((1, rp, 1), lambda e, f: (e, 0, 0)),
            pl.BlockSpec((1, rs, 1), lambda e, f: (e, 0, 0)),
            pl.BlockSpec((1, 1, d, fc), lambda e, f: (layer, e, 0, f)),
            pl.BlockSpec((1, 1, d, fc), lambda e, f: (layer, e, 0, f)),
            pl.BlockSpec((1, 1, fc, d), lambda e, f: (layer, e, f, 0)),
        ],
        out_specs=[pl.BlockSpec((1, rp, d), lambda e, f: (e, 0, 0)),
                   pl.BlockSpec((1, rs, d), lambda e, f: (e, 0, 0))],
        out_shape=[jax.ShapeDtypeStruct((n_exp, rp, d), F32),
                   jax.ShapeDtypeStruct((n_exp, rs, d), F32)],
        compiler_params=_params(2, 56),
        name="expert_ffn",
    )(xs_p, xs_s, g_p, g_s, w_gate, w_up, w_down)


def _combine_onehot_kernel(idx_ref, ye_ref, o_ref):
    t_len = o_ref.shape[0]
    n_exp = ye_ref.shape[0]
    idx = idx_ref[0]
    hit = jnp.where(idx == lax.broadcasted_iota(I32, (t_len, idx.shape[1]), 0), 1.0, 0.0).astype(BF16)
    y_hi, y_lo = _split_bf16(jnp.concatenate([ye_ref[e] for e in range(n_exp)], axis=0))
    o_ref[...] = (jnp.dot(hit, y_hi, preferred_element_type=F32)
                  + jnp.dot(hit, y_lo, preferred_element_type=F32))


def _combine_onehot(idx_row, ye, *, seq_len):
    n_exp, rows, d = ye.shape
    n_seq, _, slots = idx_row.shape
    cap = slots // n_exp
    return pl.pallas_call(
        _combine_onehot_kernel,
        grid=(n_seq,),
        in_specs=[pl.BlockSpec((1, 1, slots), lambda s: (s, 0, 0)),
                  pl.BlockSpec((n_exp, cap, d), lambda s: (0, s, 0))],
        out_specs=pl.BlockSpec((seq_len, d), lambda s: (s, 0)),
        out_shape=jax.ShapeDtypeStruct((n_seq * seq_len, d), F32),
        compiler_params=_params(1, 32),
        name="combine_onehot",
    )(idx_row, ye)


COMBINE_GROUP = 16


def _combine_rows_kernel(idx_ref, ye_ref, o_ref, tiles, *, cap, n_exp):
    e = pl.program_id(1)
    d = ye_ref.shape[2]
    tpr = _tiles_per_row(d)
    ch = 2048

    @pl.when(e == 0)
    def _():
        def zero(c, carry):
            o_ref[pl.ds(pl.multiple_of(c * ch, ch), ch), :] = jnp.zeros((ch, LANES), F32)
            return carry
        lax.fori_loop(0, o_ref.shape[0] // ch, zero, 0)

    for s in range(tpr):
        tiles[pl.ds(s, cap, stride=tpr), :] = ye_ref[0, :, s * LANES:(s + 1) * LANES]

    base = (pl.program_id(0) * n_exp + e) * cap

    def body(i, carry):
        c0 = i * COMBINE_GROUP
        dst = [pl.ds(pl.multiple_of(idx_ref[base + c0 + k], tpr), tpr) for k in range(COMBINE_GROUP)]
        vals = [o_ref[dst[k], :] + tiles[pl.ds(pl.multiple_of((c0 + k) * tpr, tpr), tpr), :]
                for k in range(COMBINE_GROUP)]
        for k in range(COMBINE_GROUP):
            o_ref[dst[k], :] = vals[k]
        return carry

    lax.fori_loop(0, cap // COMBINE_GROUP, body, 0)


def _combine_rows(idx, ye, *, seq_len):
    n_exp, rows, d = ye.shape
    tpr = _tiles_per_row(d)
    cap = CAPACITY_FACTOR * seq_len // N_EXPERTS
    n_seq = rows // cap
    return pl.pallas_call(
        functools.partial(_combine_rows_kernel, cap=cap, n_exp=n_exp),
        grid_spec=pltpu.PrefetchScalarGridSpec(
            num_scalar_prefetch=1,
            grid=(n_seq, n_exp),
            in_specs=[pl.BlockSpec((1, cap, d), lambda s, e, i: (e, s, 0))],
            out_specs=pl.BlockSpec((seq_len * tpr, LANES), lambda s, e, i: (s, 0)),
            scratch_shapes=[pltpu.VMEM((cap * tpr, LANES), F32)],
        ),
        out_shape=jax.ShapeDtypeStruct((n_seq * seq_len * tpr, LANES), F32),
        compiler_params=_params(2, 48),
        name="combine_rows",
    )(idx, ye)


def _head_norm(x, gain):
    return x * lax.rsqrt(jnp.mean(x * x, axis=-1, keepdims=True) + EPS) * gain


def _qkv_kernel(*refs, rope, has_prev, mod_row, n_heads, n_kv):
    refs = list(refs)
    x_ref = refs.pop(0)
    prev = None
    if has_prev:
        acc_ref, pmod_ref = refs.pop(0), refs.pop(0)
    mod_ref, gain_ref, w_ref, qg_ref, kg_ref = refs[:5]
    refs = refs[5:]
    if rope:
        cos_ref, sin_ref = refs.pop(0), refs.pop(0)
    if has_prev:
        prev = (acc_ref, pmod_ref, refs.pop(0))
    q_ref, k_ref, v_ref = refs[:3]
    d = x_ref.shape[1]
    hd = HEAD_DIM
    m = mod_row(pl.program_id(0))
    x = _residual_stream(x_ref, prev, m)
    shift = mod_ref[pl.ds(m, 1), pl.ds(0, d)]
    scale = mod_ref[pl.ds(m, 1), pl.ds(d, d)]
    h = _modulated_norm(x, gain_ref[...], shift, scale).astype(BF16)
    qg = qg_ref[...]
    kg = kg_ref[...]
    q_scale = math.log2(math.e) * hd ** -0.5
    half = x_ref.shape[0] // 2
    for r0 in (0, half):
        rows = slice(r0, r0 + half)
        qkv = jnp.dot(h[rows], w_ref[...], preferred_element_type=F32)
        if rope:
            cos = cos_ref[rows, :]
            sin = sin_ref[rows, :]

            def rot(xh, cos=cos, sin=sin):
                return xh * cos + pltpu.roll(xh, hd // 2, 1) * sin
        else:
            rot = lambda xh: xh

        for i in range(n_heads):
            qh = rot(_head_norm(qkv[:, i * hd:(i + 1) * hd], qg)) * q_scale
            q_ref[rows, i * hd:(i + 1) * hd] = qh.astype(BF16)
        for i in range(n_kv):
            c0 = (n_heads + i) * hd
            kh = _head_norm(qkv[:, c0:c0 + hd], kg)
            if not rope:
                refs[3][rows, i * hd:(i + 1) * hd] = kh
            k_ref[rows, i * hd:(i + 1) * hd] = rot(kh).astype(BF16)
        v = qkv[:, (n_heads + n_kv) * hd:]
        v_ref[rows, :] = v.astype(BF16)
        if not rope:
            refs[4][rows, :] = v


def _qkv(x, prev, mod, gain, w_qkv, q_gain, k_gain, tables, *, seq_len, mod_row, n_heads, n_kv):
    n, d = x.shape
    hd = HEAD_DIM
    row = lambda i: (i, 0)
    fixed = lambda i: (0, 0)
    rope = tables is not None
    in_specs = [pl.BlockSpec((ROW_BLOCK, d), row)]
    args = [x]
    if prev is not None:
        in_specs += [_rows_spec(prev[0], d), pl.BlockSpec(prev[1].shape, fixed)]
        args += list(prev)
    in_specs += [
        pl.BlockSpec(mod.shape, fixed),
        pl.BlockSpec((1, d), fixed),
        pl.BlockSpec(w_qkv.shape, fixed),
        pl.BlockSpec((1, hd), fixed),
        pl.BlockSpec((1, hd), fixed),
    ]
    args += [mod, gain, w_qkv, q_gain, k_gain]
    if rope:
        per_seq = seq_len // ROW_BLOCK
        in_specs += [pl.BlockSpec((ROW_BLOCK, hd), lambda i: (i % per_seq, 0))] * 2
        args += list(tables)
    out_specs, out_shape = [], []
    if prev is not None:
        out_specs.append(pl.BlockSpec((ROW_BLOCK, d), row))
        out_shape.append(jax.ShapeDtypeStruct((n, d), F32))
    out_specs += [pl.BlockSpec((ROW_BLOCK, n_heads * hd), row),
                  pl.BlockSpec((ROW_BLOCK, n_kv * hd), row),
                  pl.BlockSpec((ROW_BLOCK, n_kv * hd), row)]
    out_shape += [jax.ShapeDtypeStruct((n, n_heads * hd), BF16),
                  jax.ShapeDtypeStruct((n, n_kv * hd), BF16),
                  jax.ShapeDtypeStruct((n, n_kv * hd), BF16)]
    if not rope:
        out_specs += [pl.BlockSpec((ROW_BLOCK, n_kv * hd), row)] * 2
        out_shape += [jax.ShapeDtypeStruct((n, n_kv * hd), F32)] * 2
    outs = pl.pallas_call(
        functools.partial(_qkv_kernel, rope=rope, has_prev=prev is not None, mod_row=mod_row,
                          n_heads=n_heads, n_kv=n_kv),
        grid=(n // ROW_BLOCK,),
        in_specs=in_specs,
        out_specs=out_specs,
        out_shape=out_shape,
        compiler_params=_params(1, 48),
        name="qkv_rope" if rope else "qkv",
    )(*args)
    return list(outs) if prev is not None else [x] + list(outs)


def _attn_kernel(*refs, group, chunk, has_cache):
    if has_cache:
        q_ref, k_ref, v_ref, kc_ref, vc_ref, o_ref = refs
    else:
        q_ref, k_ref, v_ref, o_ref = refs
    hd = HEAD_DIM
    qb = q_ref.shape[0]
    rows = group * qb
    q = jnp.concatenate([q_ref[:, g * hd:(g + 1) * hd] for g in range(group)], axis=0)
    sources = [(k_ref, v_ref, s0, chunk) for s0 in range(0, k_ref.shape[0], chunk)]
    if has_cache:
        sources.append((kc_ref.at[0], vc_ref.at[0], 0, kc_ref.shape[1]))
    m = jnp.full((rows, 1), -jnp.inf, F32)
    acc = jnp.zeros((rows, 2 * hd), F32)
    for kr, vr, s0, size in sources:
        s = lax.dot_general(q, kr[s0:s0 + size, :], (((1,), (1,)), ((), ())),
                            preferred_element_type=F32)
        m_new = jnp.maximum(m, jnp.max(s, axis=-1, keepdims=True))
        p = jnp.exp2(s - m_new).astype(BF16)
        v_ones = jnp.concatenate([vr[s0:s0 + size, :], jnp.ones((size, hd), BF16)], axis=1)
        acc = jnp.exp2(m - m_new) * acc + jnp.dot(p, v_ones, preferred_element_type=F32)
        m = m_new
    o = acc[:, :hd] / acc[:, hd:]
    o_ref[...] = jnp.concatenate([o[g * qb:(g + 1) * qb] for g in range(group)], axis=1).astype(BF16)


def _attention(q, k, v, cache, *, seq_len, qb, chunk, n_heads, n_kv):
    n = q.shape[0]
    hd = HEAD_DIM
    group = n_heads // n_kv
    nq = seq_len // qb
    in_specs = [
        pl.BlockSpec((qb, group * hd), lambda b, h, i: (b * nq + i, h)),
        pl.BlockSpec((seq_len, hd), lambda b, h, i: (b, h)),
        pl.BlockSpec((seq_len, hd), lambda b, h, i: (b, h)),
    ]
    args = [q, k, v]
    if cache is not None:
        past = cache[0].shape[1]
        in_specs += [pl.BlockSpec((1, past, hd), lambda b, h, i: (b, 0, h))] * 2
        args += list(cache)
    return pl.pallas_call(
        functools.partial(_attn_kernel, group=group, chunk=chunk, has_cache=cache is not None),
        grid=(n // seq_len, n_kv, nq),
        in_specs=in_specs,
        out_specs=pl.BlockSpec((qb, group * hd), lambda b, h, i: (b * nq + i, h)),
        out_shape=jax.ShapeDtypeStruct((n, n_heads * hd), BF16),
        compiler_params=_params(3, 48),
        name=f"attention_{seq_len}",
    )(*args)


def _final_norm_kernel(x_ref, acc_ref, pmod_ref, gain_ref, o_ref, *, mod_row):
    n_rows, d = x_ref.shape
    m = mod_row(pl.program_id(0))
    x = x_ref[...] + pmod_ref[pl.ds(m, 1), pl.ds(5 * d, d)] * _load_rows(acc_ref, n_rows, d)
    o_ref[...] = (x * lax.rsqrt(jnp.mean(x * x, axis=-1, keepdims=True) + EPS)) * gain_ref[...]


def _final_norm(x, acc, pmod, gain, mod_row):
    n, d = x.shape
    row = lambda i: (i, 0)
    fixed = lambda i: (0, 0)
    return pl.pallas_call(
        functools.partial(_final_norm_kernel, mod_row=mod_row),
        grid=(n // ROW_BLOCK,),
        in_specs=[pl.BlockSpec((ROW_BLOCK, d), row), _rows_spec(acc, d),
                  pl.BlockSpec(pmod.shape, fixed), pl.BlockSpec((1, d), fixed)],
        out_specs=pl.BlockSpec((ROW_BLOCK, d), row),
        out_shape=jax.ShapeDtypeStruct((n, d), F32),
        compiler_params=_params(1, 32),
        name="final_norm",
    )(x, acc, pmod, gain)


def _rope_head_order():
    q = HEAD_DIM // 4
    return jnp.concatenate([jnp.arange(0, q), jnp.arange(2 * q, 3 * q), jnp.arange(q, 2 * q),
                            jnp.arange(3 * q, 4 * q)])


def _rope_tables(seq_len):
    axis = HEAD_DIM // 2
    t = jnp.arange(seq_len)
    inv = ROPE_BASE ** (-jnp.arange(axis // 2, dtype=F32) * 2.0 / axis)
    ang_row = (t // GRID_W).astype(F32)[:, None] * inv
    ang_col = (t % GRID_W).astype(F32)[:, None] * inv
    ang = jnp.concatenate([ang_row, ang_col], axis=1)
    c, s = jnp.cos(ang), jnp.sin(ang)
    return jnp.concatenate([c, c], axis=1), jnp.concatenate([-s, s], axis=1)


def kernel(x_prompt, x_sample, state_lru, cache_k, cache_v, c, c_ctx, w_mod, b_mod, norm_gain, final_gain,
           w_lru_in, lru_conv_w, lru_conv_b, lru_wa, lru_ba, lru_wx, lru_bx, lru_lambda, w_lru_out,
           w_qkv, q_norm, k_norm, w_attn_out, w_router, w_exp_gate, w_exp_up, w_exp_down):
    batch, seq, d = x_prompt.shape
    dec_batch, dec_seq, _ = x_sample.shape
    depth = w_mod.shape[0]
    n_kv = cache_k.shape[3]
    n_heads = w_attn_out.shape[1] // HEAD_DIM
    heads = dict(n_heads=n_heads, n_kv=n_kv)

    cond = jnp.zeros((SUBLANES, d), F32).at[0].set(c_ctx).at[1:1 + dec_batch].set(c)
    mod_all = _mod_vectors(cond, w_mod, b_mod)

    blocks_per_seq = dec_seq // ROW_BLOCK
    xs = [x_prompt.reshape(batch * seq, d), x_sample.reshape(dec_batch * dec_seq, d)]
    lens = [seq, dec_seq]
    mod_rows = [lambda i: 0, lambda i: 1 + i // blocks_per_seq]
    prevs = [None, None]

    new_lru, new_k, new_v = [], [], []
    for l in range(depth):
        mod = mod_all[l]
        gain_a = norm_gain[l, 0].reshape(1, d)
        gain_c = norm_gain[l, 1].reshape(1, d)
        mixed = []
        if l % 2 == 0:
            li = l // 2
            w_in = w_lru_in[li].astype(BF16)
            wg = (0.5 * jnp.concatenate([lru_wa[li, 0], lru_wx[li, 0], lru_wa[li, 1], lru_wx[li, 1]],
                                        axis=2)).astype(BF16)
            bias4 = 0.5 * jnp.stack([lru_ba[li, 0], lru_bx[li, 0], lru_ba[li, 1], lru_bx[li, 1]])
            h0s = [jnp.zeros((batch, 2, w_in.shape[1] // 2), F32), state_lru[:, li]]
            for gi in range(2):
                xs[gi], gate, xr = _lru_in(xs[gi], prevs[gi], mod, gain_a, w_in, mod_rows[gi])
                y, fin = _lru_core(xr, gate, lru_conv_w[li], lru_conv_b[li].reshape(1, -1), wg, bias4,
                                   lru_lambda[li], h0s[gi], seq_len=lens[gi])
                mixed.append(y)
                if gi == 0:
                    new_lru.append(fin.astype(x_prompt.dtype))
            w_out = w_lru_out[li].astype(BF16)
        else:
            ai = l // 2
            w = w_qkv[ai].astype(BF16)
            qg = q_norm[ai].reshape(1, -1)
            kg = k_norm[ai].reshape(1, -1)
            xs[0], q_p, k_p, v_p, kf, vf = _qkv(xs[0], prevs[0], mod, gain_a, w, qg, kg, None,
                                                seq_len=seq, mod_row=mod_rows[0], **heads)
            order = _rope_head_order()
            n_qk = (n_heads + n_kv) * HEAD_DIM
            w_qk = w_qkv[ai][:, :n_qk].reshape(d, n_heads + n_kv, HEAD_DIM)[:, :, order].reshape(d, n_qk)
            w_r = jnp.concatenate([w_qk, w_qkv[ai][:, n_qk:]], axis=1).astype(BF16)
            xs[1], q_s, k_s, v_s = _qkv(xs[1], prevs[1], mod, gain_a, w_r, qg[:, order], kg[:, order],
                                        _rope_tables(dec_seq),
                                        seq_len=dec_seq, mod_row=mod_rows[1], **heads)
            new_k.append(kf.reshape(batch, seq, n_kv, HEAD_DIM))
            new_v.append(vf.reshape(batch, seq, n_kv, HEAD_DIM))
            past = cache_k.shape[2]
            kc = cache_k[:, ai][..., order].reshape(dec_batch, past, n_kv * HEAD_DIM).astype(BF16)
            vc = cache_v[:, ai].reshape(dec_batch, past, n_kv * HEAD_DIM).astype(BF16)
            mixed.append(_attention(q_p, k_p, v_p, None, seq_len=seq, qb=seq, chunk=seq, **heads))
            mixed.append(_attention(q_s, k_s, v_s, (kc, vc), seq_len=dec_seq, qb=256, chunk=256, **heads))
            w_out = w_attn_out[ai].astype(BF16)

        wr_hi, wr_lo = _split_bf16(w_router[l].T)
        x1_p, h2_p, lg_p = _mix_out(mixed[0], xs[0], mod, gain_c, w_out, wr_hi, wr_lo, mod_rows[0], False)
        x1_s, h2_s, lg_s = _mix_out(mixed[1], xs[1], mod, gain_c, w_out, wr_hi, wr_lo, mod_rows[1], True)
        idx_p, g_p, idx_s, g_s = _route(lg_p, lg_s, seq, dec_seq)
        xe_p = _gather_onehot(idx_p[:, :, None], h2_p, seq_len=seq)
        off_s = idx_s * _tiles_per_row(d)
        xe_s = _gather_rows(off_s, h2_s, d, seq_len=dec_seq)
        e_n = N_EXPERTS
        gc_p = g_p.reshape(batch, e_n, -1).transpose(1, 0, 2).reshape(e_n, -1, 1)
        gc_s = g_s.reshape(dec_batch, e_n, -1).transpose(1, 0, 2).reshape(e_n, -1, 1)
        ye_p, ye_s = _ffn(xe_p, xe_s, gc_p, gc_s, w_exp_gate, w_exp_up, w_exp_down, l)
        acc_p = _combine_onehot(idx_p[:, None, :], ye_p, seq_len=seq)
        acc_s = _combine_rows(off_s, ye_s, seq_len=dec_seq)
        xs = [x1_p, x1_s]
        prevs = [(acc_p, mod), (acc_s, mod)]

    fg = final_gain.reshape(1, d)
    y_prompt = _final_norm(xs[0], *prevs[0], fg, mod_rows[0]).reshape(batch, seq, d)
    y_sample = _final_norm(xs[1], *prevs[1], fg, mod_rows[1]).reshape(dec_batch, dec_seq, d)
    return (y_prompt, y_sample, jnp.stack(new_lru, axis=1), jnp.stack(new_k, axis=1), jnp.stack(new_v, axis=1))
```

```python
import functools
import math

import jax
import jax.numpy as jnp
from jax import lax
from jax.experimental import pallas as pl
from jax.experimental.pallas import tpu as pltpu

F32 = jnp.float32
BF16 = jnp.bfloat16
I32 = jnp.int32

LANES = 128
SUBLANES = 8
MIB = 1024 * 1024

RG_C = 8.0
CONV_W = 4
CONV_PAD_L = 2
ROPE_BASE = 10000.0
GRID_W = 64
EPS = 1e-6
N_EXPERTS = 16
CAPACITY_FACTOR = 2
HEAD_DIM = 128
LRU_BLOCKS = 8
LRU_STEP_BLOCKS = 2

ROW_BLOCK = 512
MIX_PARTS = 2


def _params(n_axes, vmem_mib):
    return pltpu.CompilerParams(
        dimension_semantics=("arbitrary",) * n_axes, vmem_limit_bytes=vmem_mib * MIB)


def _split_bf16(x):
    hi = x.astype(BF16)
    lo = (x - hi.astype(F32)).astype(BF16)
    return hi, lo


def _modulated_norm(x, gain, shift, scale):
    y = x * lax.rsqrt(jnp.mean(x * x, axis=-1, keepdims=True) + EPS)
    return (y * gain) * (1.0 + scale) + shift


def _tiles_per_row(d):
    return d // LANES


def _load_rows(ref, n_rows, d):
    if ref.shape == (n_rows, d):
        return ref[...]
    tpr = _tiles_per_row(d)
    return jnp.concatenate([ref[pl.ds(s, n_rows, stride=tpr), :] for s in range(tpr)], axis=1)


def _store_rows(ref, x, r0):
    n_rows, d = x.shape
    if ref.shape[1] == d:
        ref[r0:r0 + n_rows, :] = x.astype(ref.dtype)
        return
    tpr = _tiles_per_row(d)
    for s in range(tpr):
        ref[pl.ds(r0 * tpr + s, n_rows, stride=tpr), :] = x[:, s * LANES:(s + 1) * LANES]


def _rows_spec(arr, d):
    if arr.shape[1] == d:
        return pl.BlockSpec((ROW_BLOCK, d), lambda i: (i, 0))
    return pl.BlockSpec((ROW_BLOCK * _tiles_per_row(d), LANES), lambda i: (i, 0))


def _residual_stream(x_ref, prev, m):
    if prev is None:
        return x_ref[...]
    acc_ref, pmod_ref, xo_ref = prev
    n_rows, d = x_ref.shape
    x = x_ref[...] + pmod_ref[pl.ds(m, 1), pl.ds(5 * d, d)] * _load_rows(acc_ref, n_rows, d)
    xo_ref[...] = x
    return x


def _mod_kernel(cond_ref, w_ref, b_ref, o_ref):
    c = cond_ref[...]
    a_hi, a_lo = _split_bf16(c * jax.nn.sigmoid(c))
    w_hi, w_lo = _split_bf16(w_ref[0])
    dot = functools.partial(jnp.dot, preferred_element_type=F32)
    o_ref[0] = dot(a_hi, w_hi) + (dot(a_lo, w_hi) + dot(a_hi, w_lo)) + b_ref[0]


def _mod_vectors(cond8, w_mod, b_mod):
    depth, d, d6 = w_mod.shape
    nb = 1536
    return pl.pallas_call(
        _mod_kernel,
        grid=(depth, d6 // nb),
        in_specs=[
            pl.BlockSpec((SUBLANES, d), lambda l, j: (0, 0)),
            pl.BlockSpec((1, d, nb), lambda l, j: (l, 0, j)),
            pl.BlockSpec((1, 1, nb), lambda l, j: (l, 0, j)),
        ],
        out_specs=pl.BlockSpec((1, SUBLANES, nb), lambda l, j: (l, 0, j)),
        out_shape=jax.ShapeDtypeStruct((depth, SUBLANES, d6), F32),
        compiler_params=_params(2, 40),
        name="adaln_mod",
    )(cond8, w_mod, b_mod.reshape(depth, 1, d6))


def _lru_in_kernel(*refs, mod_row, has_prev):
    if has_prev:
        x_ref, acc_ref, pmod_ref, mod_ref, gain_ref, w_ref, xo_ref, gate_ref, xr_ref = refs
        prev = (acc_ref, pmod_ref, xo_ref)
    else:
        x_ref, mod_ref, gain_ref, w_ref, gate_ref, xr_ref = refs
        prev = None
    d = x_ref.shape[1]
    m = mod_row(pl.program_id(0))
    x = _residual_stream(x_ref, prev, m)
    shift = mod_ref[pl.ds(m, 1), pl.ds(0, d)]
    scale = mod_ref[pl.ds(m, 1), pl.ds(d, d)]
    h = _modulated_norm(x, gain_ref[...], shift, scale).astype(BF16)
    u = jnp.dot(h, w_ref[...], preferred_element_type=F32)
    w = gate_ref.shape[1]
    gate_ref[...] = u[:, :w]
    xr_ref[...] = u[:, w:]


def _lru_in(x, prev, mod, gain, w_in, mod_row):
    n, d = x.shape
    w2 = w_in.shape[1]
    w = w2 // 2
    row = lambda i: (i, 0)
    fixed = lambda i: (0, 0)
    in_specs = [pl.BlockSpec((ROW_BLOCK, d), row)]
    args = [x]
    out_specs = [pl.BlockSpec((ROW_BLOCK, w), row), pl.BlockSpec((ROW_BLOCK, w), row)]
    out_shape = [jax.ShapeDtypeStruct((n, w), F32), jax.ShapeDtypeStruct((n, w), F32)]
    if prev is not None:
        in_specs += [_rows_spec(prev[0], d), pl.BlockSpec(prev[1].shape, fixed)]
        args += list(prev)
        out_specs = [pl.BlockSpec((ROW_BLOCK, d), row)] + out_specs
        out_shape = [jax.ShapeDtypeStruct((n, d), F32)] + out_shape
    in_specs += [pl.BlockSpec(mod.shape, fixed), pl.BlockSpec((1, d), fixed), pl.BlockSpec((d, w2), fixed)]
    args += [mod, gain, w_in]
    outs = pl.pallas_call(
        functools.partial(_lru_in_kernel, mod_row=mod_row, has_prev=prev is not None),
        grid=(n // ROW_BLOCK,),
        in_specs=in_specs,
        out_specs=out_specs,
        out_shape=out_shape,
        compiler_params=_params(1, 48),
        name="lru_in",
    )(*args)
    return outs if prev is not None else [x] + list(outs)


def _lru_core_kernel(xr_ref, gate_ref, cw_ref, cb_ref, wg_ref, bias_ref, lam_ref, h0_ref,
                     y_ref, fin_ref, pad, a_f, b_f, a_b, b_b):
    t_len, lb = xr_ref.shape
    ch = 256
    halo = SUBLANES
    seg = t_len // SUBLANES
    piece = min(ch, seg)
    seg_shift = seg.bit_length() - 1

    def seg_rows(t0):
        r = lax.shift_right_logical(t0, seg_shift)
        return pl.ds((t0 - r * seg) * SUBLANES + r, piece, stride=SUBLANES)

    zero_rows = jnp.zeros((halo, lb), F32)
    pad[0:halo, :] = zero_rows
    pad[t_len + halo:t_len + 2 * halo, :] = zero_rows

    def copy_in(c, carry):
        r = pl.multiple_of(c * ch, ch)
        pad[pl.ds(r + halo, ch), :] = xr_ref[pl.ds(r, ch), :]
        return carry

    lax.fori_loop(0, t_len // ch, copy_in, 0)

    neg_lam = -lam_ref[...]
    softplus = jnp.maximum(neg_lam, 0.0) + jnp.log1p(jnp.exp(-jnp.abs(neg_lam)))
    c_nla = (0.5 * RG_C) * softplus
    c_exp2 = (-0.5 * RG_C * math.log2(math.e)) * softplus
    cw = cw_ref[...]
    cbias = cb_ref[...]
    half_bias = bias_ref[...]
    half_wg = wg_ref[...]
    bw = half_wg.shape[1]
    n_slab = lb // LANES

    def gates(c, carry):
        r = pl.multiple_of(c * ch, ch)
        blk = pad[pl.ds(r, ch + 2 * halo), :]
        xc = cbias
        for k in range(CONV_W):
            o = halo - CONV_PAD_L + k
            xc = xc + blk[o:o + ch] * cw[k:k + 1]
        xc_bf = xc.astype(BF16)
        g = [jnp.dot(xc_bf[:, j * bw:(j + 1) * bw], half_wg[j], preferred_element_type=F32)
             for j in range(lb // bw)]
        pre = lambda k: jnp.concatenate([gj[:, k * bw:(k + 1) * bw] for gj in g], axis=1) + half_bias[k:k + 1]
        half_xc = 0.5 * xc
        for d, (a_ref, b_ref) in enumerate(((a_f, b_f), (a_b, b_b))):
            u = jnp.tanh(pre(2 * d)) + 1.0
            t_i = jnp.tanh(pre(2 * d + 1))
            a = jnp.exp2(u * c_exp2[d:d + 1])
            b = jnp.sqrt(jnp.tanh(u * c_nla[d:d + 1]) * (1.0 + a * a)) * (t_i * half_xc + half_xc)
            for p in range(ch // piece):
                rows = seg_rows(r + p * piece)
                for j in range(n_slab):
                    a_ref[j, rows, :] = a[p * piece:(p + 1) * piece, j * LANES:(j + 1) * LANES]
                    b_ref[j, rows, :] = b[p * piece:(p + 1) * piece, j * LANES:(j + 1) * LANES]
        return carry

    lax.fori_loop(0, t_len // ch, gates, 0)

    def scan(i, carry):
        rows = (pl.ds(pl.multiple_of(i * SUBLANES, SUBLANES), SUBLANES),
                pl.ds(pl.multiple_of((seg - 1 - i) * SUBLANES, SUBLANES), SUBLANES))
        out = []
        for k, (h, prod) in enumerate(carry):
            d, j = divmod(k, n_slab)
            a_ref, b_ref = ((a_f, b_f), (a_b, b_b))[d]
            a = a_ref[j, rows[d], :]
            h = a * h + b_ref[j, rows[d], :]
            prod = a * prod
            b_ref[j, rows[d], :] = h
            a_ref[j, rows[d], :] = prod
            out.append((h, prod))
        return tuple(out)

    start = (jnp.zeros((SUBLANES, LANES), F32), jnp.ones((SUBLANES, LANES), F32))
    ends = lax.fori_loop(0, seg, scan, (start,) * (2 * n_slab), unroll=8)

    enter = []
    for k, (h, prod) in enumerate(ends):
        d, j = divmod(k, n_slab)
        states = [h0_ref[0, d:d + 1, j * LANES:(j + 1) * LANES]]
        for r in (range(SUBLANES) if d == 0 else reversed(range(SUBLANES))):
            states.append(h[r:r + 1] + prod[r:r + 1] * states[-1])
        fin_ref[0, d:d + 1, j * LANES:(j + 1) * LANES] = states[SUBLANES]
        order = states[:SUBLANES] if d == 0 else states[:SUBLANES][::-1]
        enter.append(jnp.concatenate(order, axis=0))

    def fix(i, carry):
        rows = pl.ds(pl.multiple_of(i * SUBLANES, SUBLANES), SUBLANES)
        for j in range(n_slab):
            b_f[j, rows, :] = ((b_f[j, rows, :] + a_f[j, rows, :] * enter[j])
                               + (b_b[j, rows, :] + a_b[j, rows, :] * enter[n_slab + j]))
        return carry

    lax.fori_loop(0, seg, fix, 0, unroll=8)

    def emit(c, carry):
        r = pl.multiple_of(c * ch, ch)
        gt = gate_ref[pl.ds(r, ch), :]
        cdf = 0.5 * (1.0 + jnp.tanh(math.sqrt(2.0 / math.pi) * (gt + 0.044715 * (gt * gt * gt))))
        y = gt * cdf
        for p in range(ch // piece):
            rows = seg_rows(r + p * piece)
            total = jnp.concatenate([b_f[j, rows, :] for j in range(n_slab)], axis=1)
            y_ref[pl.ds(r + p * piece, piece), :] = (total * y[p * piece:(p + 1) * piece]).astype(BF16)
        return carry

    lax.fori_loop(0, t_len // ch, emit, 0)


def _lru_core(xr, gate, conv_w, conv_b, wg, bias4, lam, h0, *, seq_len):
    n, w = xr.shape
    n_seq = n // seq_len
    bw = w // LRU_BLOCKS
    lb = LRU_STEP_BLOCKS * bw
    tok = lambda s, c: (s, c)
    chan = lambda s, c: (0, c)
    return pl.pallas_call(
        _lru_core_kernel,
        grid=(n_seq, LRU_BLOCKS // LRU_STEP_BLOCKS),
        in_specs=[
            pl.BlockSpec((seq_len, lb), tok),
            pl.BlockSpec((seq_len, lb), tok),
            pl.BlockSpec((CONV_W, lb), chan),
            pl.BlockSpec((1, lb), chan),
            pl.BlockSpec((LRU_STEP_BLOCKS, bw, 4 * bw), lambda s, c: (c, 0, 0)),
            pl.BlockSpec((4, lb), chan),
            pl.BlockSpec((2, lb), chan),
            pl.BlockSpec((1, 2, lb), lambda s, c: (s, 0, c)),
        ],
        out_specs=[pl.BlockSpec((seq_len, lb), tok),
                   pl.BlockSpec((1, 2, lb), lambda s, c: (s, 0, c))],
        out_shape=[jax.ShapeDtypeStruct((n, w), BF16), jax.ShapeDtypeStruct((n_seq, 2, w), F32)],
        scratch_shapes=[pltpu.VMEM((seq_len + 2 * SUBLANES, lb), F32)]
        + [pltpu.VMEM((lb // LANES, seq_len, LANES), F32)] * 4,
        compiler_params=_params(2, 48),
        name=f"lru_core_{seq_len}",
    )(xr, gate, conv_w, conv_b, wg, bias4, lam, h0)


def _mix_out_kernel(a_ref, x_ref, mod_ref, gain_ref, w_ref, wrh_ref, wrl_ref,
                    x1_ref, h2_ref, lg_ref, *, mod_row):
    d = x_ref.shape[1]
    m = mod_row(pl.program_id(0))
    g_mix = mod_ref[pl.ds(m, 1), pl.ds(2 * d, d)]
    shift = mod_ref[pl.ds(m, 1), pl.ds(3 * d, d)]
    scale = mod_ref[pl.ds(m, 1), pl.ds(4 * d, d)]
    dot = functools.partial(jnp.dot, preferred_element_type=F32)
    part = x_ref.shape[0] // MIX_PARTS
    for r0 in range(0, x_ref.shape[0], part):
        rows = slice(r0, r0 + part)
        x1 = x_ref[rows, :] + g_mix * dot(a_ref[rows, :], w_ref[...])
        x1_ref[rows, :] = x1
        h2 = _modulated_norm(x1, gain_ref[...], shift, scale)
        h_hi, h_lo = _split_bf16(h2)
        _store_rows(h2_ref, h2, r0)
        both = dot(h_hi, wrh_ref[...])
        e_pad = lg_ref.shape[1]
        lg_ref[rows, :] = both[:, :e_pad] + (dot(h_lo, wrl_ref[...]) + both[:, e_pad:])


def _mix_out(a, x, mod, gain, w_out, wr_hi, wr_lo, mod_row, token_tiled):
    n, d = x.shape
    e_pad = wr_lo.shape[1]
    row = lambda i: (i, 0)
    fixed = lambda i: (0, 0)
    tpr = _tiles_per_row(d)
    if token_tiled:
        h2_spec = pl.BlockSpec((ROW_BLOCK * tpr, LANES), row)
        h2_shape = jax.ShapeDtypeStruct((n * tpr, LANES), F32)
    else:
        h2_spec = pl.BlockSpec((ROW_BLOCK, d), row)
        h2_shape = jax.ShapeDtypeStruct((n, d), BF16)
    return pl.pallas_call(
        functools.partial(_mix_out_kernel, mod_row=mod_row),
        grid=(n // ROW_BLOCK,),
        in_specs=[
            pl.BlockSpec((ROW_BLOCK, a.shape[1]), row),
            pl.BlockSpec((ROW_BLOCK, d), row),
            pl.BlockSpec(mod.shape, fixed),
            pl.BlockSpec((1, d), fixed),
            pl.BlockSpec(w_out.shape, fixed),
            pl.BlockSpec(wr_hi.shape, fixed),
            pl.BlockSpec(wr_lo.shape, fixed),
        ],
        out_specs=[pl.BlockSpec((ROW_BLOCK, d), row), h2_spec, pl.BlockSpec((ROW_BLOCK, e_pad), row)],
        out_shape=[jax.ShapeDtypeStruct((n, d), F32), h2_shape, jax.ShapeDtypeStruct((n, e_pad), F32)],
        compiler_params=_params(1, 48),
        name="mix_out",
    )(a, x, mod, gain, w_out, wr_hi, wr_lo)


def _lane_sum(tiles):
    acc = tiles[0]
    for t in tiles[1:]:
        acc = acc + t
    return jnp.sum(acc, axis=1, keepdims=True)


def _exclusive_cumsum(flags, upper):
    out = []
    off = jnp.zeros((flags[0].shape[0], 1), F32)
    for f in flags:
        out.append(jnp.dot(f.astype(BF16), upper, preferred_element_type=F32) + off)
        off = off + jnp.sum(f, axis=1, keepdims=True)
    return out


def _route_group(lg_ref, n_seq, t_len, idx_ref, g_ref):
    cap = CAPACITY_FACTOR * t_len // N_EXPERTS
    affs = []
    for s in range(n_seq):
        lg = lg_ref[s * t_len:(s + 1) * t_len, :].T[:N_EXPERTS]
        ex = jnp.exp(lg - jnp.max(lg, axis=0, keepdims=True))
        affs.append(ex / jnp.sum(ex, axis=0, keepdims=True))
    aff = jnp.concatenate(affs, axis=0)
    n_rows = aff.shape[0]
    nt = t_len // LANES
    g = [aff[:, j * LANES:(j + 1) * LANES] for j in range(nt)]

    kth_bits = jnp.zeros((n_rows, 1), I32)
    for bit in range(30, -1, -1):
        cand = kth_bits | (1 << bit)
        cand_f = pltpu.bitcast(cand, F32)
        cnt = _lane_sum([jnp.where(t >= cand_f, 1, 0) for t in g])
        kth_bits = jnp.where(cnt >= cap, cand, kth_bits)
    kth = pltpu.bitcast(kth_bits, F32)

    lane = lax.broadcasted_iota(I32, (LANES, LANES), 0)
    upper = jnp.where(lane < lax.broadcasted_iota(I32, (LANES, LANES), 1), 1.0, 0.0).astype(BF16)
    gt = [t > kth for t in g]
    eq = [t == kth for t in g]
    need = (cap - _lane_sum([jnp.where(m, 1, 0) for m in gt])).astype(F32)
    eq_rank = _exclusive_cumsum([jnp.where(m, 1.0, 0.0) for m in eq], upper)
    sel = [jnp.logical_or(gt[j], jnp.logical_and(eq[j], eq_rank[j] < need)) for j in range(nt)]
    pos = _exclusive_cumsum([jnp.where(m, 1.0, 0.0) for m in sel], upper)

    lane_r = lax.broadcasted_iota(I32, (n_rows, LANES), 1)
    d = [jnp.where(sel[j], lane_r + j * LANES - pos[j].astype(I32), -1) for j in range(nt)]
    for k in range(t_len.bit_length() - 1):
        s = 1 << k
        if s < LANES:
            d_rot = [pltpu.roll(x, LANES - s, 1) for x in d]
            g_rot = [pltpu.roll(x, LANES - s, 1) for x in g]
            same = lane_r < LANES - s
            d_in = [jnp.where(same, d_rot[j], d_rot[(j + 1) % nt]) for j in range(nt)]
            g_in = [jnp.where(same, g_rot[j], g_rot[(j + 1) % nt]) for j in range(nt)]
        else:
            q = s // LANES
            d_in = [d[(j + q) % nt] for j in range(nt)]
            g_in = [g[(j + q) % nt] for j in range(nt)]
        new_d, new_g = [], []
        for j in range(nt):
            move = jnp.logical_and(d_in[j] >= 0, ((d_in[j] >> k) & 1) == 1)
            stay = jnp.logical_and(d[j] >= 0, ((d[j] >> k) & 1) == 0)
            new_d.append(jnp.where(move, d_in[j], jnp.where(stay, d[j], -1)))
            new_g.append(jnp.where(move, g_in[j], g[j]))
        d, g = new_d, new_g

    for j in range(idx_ref.shape[1] // LANES):
        idx_ref[:, j * LANES:(j + 1) * LANES] = lane_r + j * LANES + d[j]
        g_ref[:, j * LANES:(j + 1) * LANES] = g[j]


def _route_kernel(lgp_ref, lgs_ref, idx_p_ref, g_p_ref, idx_s_ref, g_s_ref, *, p_len, s_len):
    _route_group(lgp_ref, lgp_ref.shape[0] // p_len, p_len, idx_p_ref, g_p_ref)
    _route_group(lgs_ref, lgs_ref.shape[0] // s_len, s_len, idx_s_ref, g_s_ref)


def _route(lg_p, lg_s, p_len, s_len):
    e = N_EXPERTS
    np_seq = lg_p.shape[0] // p_len
    ns_seq = lg_s.shape[0] // s_len
    cap_p = CAPACITY_FACTOR * p_len // N_EXPERTS
    cap_s = CAPACITY_FACTOR * s_len // N_EXPERTS
    wp = max(cap_p, LANES)
    ws = max(cap_s, LANES)
    idx_p, g_p, idx_s, g_s = pl.pallas_call(
        functools.partial(_route_kernel, p_len=p_len, s_len=s_len),
        out_shape=[jax.ShapeDtypeStruct((np_seq * e, wp), I32),
                   jax.ShapeDtypeStruct((np_seq * e, wp), F32),
                   jax.ShapeDtypeStruct((ns_seq * e, ws), I32),
                   jax.ShapeDtypeStruct((ns_seq * e, ws), F32)],
        compiler_params=pltpu.CompilerParams(vmem_limit_bytes=40 * MIB),
        name="route",
    )(lg_p, lg_s)
    return (idx_p[:, :cap_p].reshape(np_seq, e * cap_p), g_p[:, :cap_p].reshape(np_seq, e * cap_p),
            idx_s[:, :cap_s].reshape(-1), g_s[:, :cap_s].reshape(-1))


def _gather_onehot_kernel(idx_ref, h_ref, o_ref):
    t_len = h_ref.shape[0]
    n_exp, cap, _ = o_ref.shape
    idx = idx_ref[0]
    onehot = jnp.where(idx == lax.broadcasted_iota(I32, (idx.shape[0], t_len), 1), 1.0, 0.0).astype(BF16)
    xs = jnp.dot(onehot, h_ref[...], preferred_element_type=F32).astype(BF16)
    for e in range(n_exp):
        o_ref[e] = xs[e * cap:(e + 1) * cap]


def _gather_onehot(idx_col, h, *, seq_len):
    n, d = h.shape
    n_seq, slots, _ = idx_col.shape
    cap = slots // N_EXPERTS
    return pl.pallas_call(
        _gather_onehot_kernel,
        grid=(n_seq,),
        in_specs=[pl.BlockSpec((1, slots, 1), lambda s: (s, 0, 0)),
                  pl.BlockSpec((seq_len, d), lambda s: (s, 0))],
        out_specs=pl.BlockSpec((N_EXPERTS, cap, d), lambda s: (0, s, 0)),
        out_shape=jax.ShapeDtypeStruct((N_EXPERTS, n_seq * cap, d), BF16),
        compiler_params=_params(1, 32),
        name="gather_onehot",
    )(idx_col, h)


def _gather_rows_kernel(idx_ref, h_ref, o_ref, tiles, *, cap, n_exp):
    tpr = _tiles_per_row(o_ref.shape[2])
    base = (pl.program_id(0) * n_exp + pl.program_id(1)) * cap

    def body(c, carry):
        src = pl.multiple_of(idx_ref[base + c], tpr)
        tiles[pl.ds(pl.multiple_of(c * tpr, tpr), tpr), :] = h_ref[pl.ds(src, tpr), :]
        return carry

    lax.fori_loop(0, cap, body, 0, unroll=8)
    for s in range(tpr):
        o_ref[0, :, s * LANES:(s + 1) * LANES] = tiles[pl.ds(s, cap, stride=tpr), :].astype(BF16)


def _gather_rows(idx, h, d, *, seq_len):
    tpr = _tiles_per_row(d)
    n_seq = h.shape[0] // (seq_len * tpr)
    cap = CAPACITY_FACTOR * seq_len // N_EXPERTS
    return pl.pallas_call(
        functools.partial(_gather_rows_kernel, cap=cap, n_exp=N_EXPERTS),
        grid_spec=pltpu.PrefetchScalarGridSpec(
            num_scalar_prefetch=1,
            grid=(n_seq, N_EXPERTS),
            in_specs=[pl.BlockSpec((seq_len * tpr, LANES), lambda s, e, idx: (s, 0))],
            out_specs=pl.BlockSpec((1, cap, d), lambda s, e, idx: (e, s, 0)),
            scratch_shapes=[pltpu.VMEM((cap * tpr, LANES), F32)],
        ),
        out_shape=jax.ShapeDtypeStruct((N_EXPERTS, n_seq * cap, d), BF16),
        compiler_params=_params(2, 48),
        name="gather_rows",
    )(idx, h)


def _ffn_kernel(xp_ref, xs_ref, gp_ref, gs_ref, wg_ref, wu_ref, wd_ref, yp_ref, ys_ref):
    rc = 512

    @pl.when(pl.program_id(1) == 0)
    def _():
        for y_ref in (yp_ref, ys_ref):
            for r in range(0, y_ref.shape[1], rc):
                y_ref[0, r:r + rc, :] = jnp.zeros((rc, y_ref.shape[2]), F32)

    wg = wg_ref[0, 0].astype(BF16)
    wu = wu_ref[0, 0].astype(BF16)
    wd = wd_ref[0, 0].astype(BF16)
    last = pl.program_id(1) == pl.num_programs(1) - 1
    for x_ref, g_ref, y_ref in ((xp_ref, gp_ref, yp_ref), (xs_ref, gs_ref, ys_ref)):
        for r in range(0, x_ref.shape[1], rc):
            x = x_ref[0, r:r + rc, :]
            hg = jnp.dot(x, wg, preferred_element_type=F32)
            hu = jnp.dot(x, wu, preferred_element_type=F32)
            hid = ((hg * jax.nn.sigmoid(hg)) * hu).astype(BF16)
            weight = jnp.where(last, g_ref[0, r:r + rc, :], 1.0)
            y_ref[0, r:r + rc, :] = (y_ref[0, r:r + rc, :] + jnp.dot(hid, wd, preferred_element_type=F32)) * weight


def _ffn(xs_p, xs_s, g_p, g_s, w_gate, w_up, w_down, layer):
    n_exp, rp, d = xs_p.shape
    rs = xs_s.shape[1]
    ff = w_gate.shape[3]
    fc = 512
    return pl.pallas_call(
        _ffn_kernel,
        grid=(n_exp, ff // fc),
        in_specs=[
            pl.BlockSpec((1, rp, d), lambda e, f: (e, 0, 0)),
            pl.BlockSpec((1, rs, d), lambda e, f: (e, 0, 0)),
            pl.BlockSpec((1, rp, 1), lambda e, f: (e, 0, 0)),
            pl.BlockSpec((1, rs, 1), lambda e, f: (e, 0, 0)),
            pl.BlockSpec((1, 1, d, fc), lambda e, f: (layer, e, 0, f)),
            pl.BlockSpec((1, 1, d, fc), lambda e, f: (layer, e, 0, f)),
            pl.BlockSpec((1, 1, fc, d), lambda e, f: (layer, e, f, 0)),
        ],
        out_specs=[pl.BlockSpec((1, rp, d), lambda e, f: (e, 0, 0)),
                   pl.BlockSpec((1, rs, d), lambda e, f: (e, 0, 0))],
        out_shape=[jax.ShapeDtypeStruct((n_exp, rp, d), F32),
                   jax.ShapeDtypeStruct((n_exp, rs, d), F32)],
        compiler_params=_params(2, 56),
        name="expert_ffn",
    )(xs_p, xs_s, g_p, g_s, w_gate, w_up, w_down)


def _combine_onehot_kernel(*refs, mod_row, final):
    if final:
        idx_ref, ye_ref, x_ref, pmod_ref, gain_ref, o_ref = refs
    else:
        idx_ref, ye_ref, x_ref, pmod_ref, o_ref = refs
    t_len, d = o_ref.shape
    n_exp = ye_ref.shape[0]
    idx = idx_ref[0]
    hit = jnp.where(idx == lax.broadcasted_iota(I32, (t_len, idx.shape[1]), 0), 1.0, 0.0).astype(BF16)
    y_hi, y_lo = _split_bf16(jnp.concatenate([ye_ref[e] for e in range(n_exp)], axis=0))
    acc = jnp.dot(hit, y_hi, preferred_element_type=F32) + jnp.dot(hit, y_lo, preferred_element_type=F32)
    x = x_ref[...] + pmod_ref[pl.ds(mod_row(pl.program_id(0)), 1), pl.ds(5 * d, d)] * acc
    if final:
        x = (x * lax.rsqrt(jnp.mean(x * x, axis=-1, keepdims=True) + EPS)) * gain_ref[...]
    o_ref[...] = x


def _combine_onehot(idx_row, ye, x, pmod, final_gain, *, seq_len, mod_row):
    n_exp, rows, d = ye.shape
    n_seq, _, slots = idx_row.shape
    cap = slots // n_exp
    in_specs = [pl.BlockSpec((1, 1, slots), lambda s: (s, 0, 0)),
                pl.BlockSpec((n_exp, cap, d), lambda s: (0, s, 0)),
                pl.BlockSpec((seq_len, d), lambda s: (s, 0)),
                pl.BlockSpec(pmod.shape, lambda s: (0, 0))]
    args = [idx_row, ye, x, pmod]
    if final_gain is not None:
        in_specs.append(pl.BlockSpec((1, d), lambda s: (0, 0)))
        args.append(final_gain)
    return pl.pallas_call(
        functools.partial(_combine_onehot_kernel, mod_row=mod_row, final=final_gain is not None),
        grid=(n_seq,),
        in_specs=in_specs,
        out_specs=pl.BlockSpec((seq_len, d), lambda s: (s, 0)),
        out_shape=jax.ShapeDtypeStruct((n_seq * seq_len, d), F32),
        compiler_params=_params(1, 32),
        name="combine_onehot",
    )(*args)


COMBINE_GROUP = 16


def _combine_rows_kernel(idx_ref, ye_ref, o_ref, tiles, *, cap, n_exp):
    e = pl.program_id(1)
    d = ye_ref.shape[2]
    tpr = _tiles_per_row(d)
    ch = 2048

    @pl.when(e == 0)
    def _():
        def zero(c, carry):
            o_ref[pl.ds(pl.multiple_of(c * ch, ch), ch), :] = jnp.zeros((ch, LANES), F32)
            return carry
        lax.fori_loop(0, o_ref.shape[0] // ch, zero, 0)

    for s in range(tpr):
        tiles[pl.ds(s, cap, stride=tpr), :] = ye_ref[0, :, s * LANES:(s + 1) * LANES]

    base = (pl.program_id(0) * n_exp + e) * cap

    def body(i, carry):
        c0 = i * COMBINE_GROUP
        dst = [pl.ds(pl.multiple_of(idx_ref[base + c0 + k], tpr), tpr) for k in range(COMBINE_GROUP)]
        vals = [o_ref[dst[k], :] + tiles[pl.ds(pl.multiple_of((c0 + k) * tpr, tpr), tpr), :]
                for k in range(COMBINE_GROUP)]
        for k in range(COMBINE_GROUP):
            o_ref[dst[k], :] = vals[k]
        return carry

    lax.fori_loop(0, cap // COMBINE_GROUP, body, 0)


def _combine_rows(idx, ye, *, seq_len):
    n_exp, rows, d = ye.shape
    tpr = _tiles_per_row(d)
    cap = CAPACITY_FACTOR * seq_len // N_EXPERTS
    n_seq = rows // cap
    return pl.pallas_call(
        functools.partial(_combine_rows_kernel, cap=cap, n_exp=n_exp),
        grid_spec=pltpu.PrefetchScalarGridSpec(
            num_scalar_prefetch=1,
            grid=(n_seq, n_exp),
            in_specs=[pl.BlockSpec((1, cap, d), lambda s, e, i: (e, s, 0))],
            out_specs=pl.BlockSpec((seq_len * tpr, LANES), lambda s, e, i: (s, 0)),
            scratch_shapes=[pltpu.VMEM((cap * tpr, LANES), F32)],
        ),
        out_shape=jax.ShapeDtypeStruct((n_seq * seq_len * tpr, LANES), F32),
        compiler_params=_params(2, 48),
        name="combine_rows",
    )(idx, ye)


def _head_norm(x, gain):
    return x * lax.rsqrt(jnp.mean(x * x, axis=-1, keepdims=True) + EPS) * gain


def _qkv_kernel(*refs, rope, has_prev, mod_row, n_heads, n_kv):
    refs = list(refs)
    x_ref = refs.pop(0)
    prev = None
    if has_prev:
        acc_ref, pmod_ref = refs.pop(0), refs.pop(0)
    mod_ref, gain_ref, w_ref, qg_ref, kg_ref = refs[:5]
    refs = refs[5:]
    if rope:
        cos_ref, sin_ref = refs.pop(0), refs.pop(0)
    if has_prev:
        prev = (acc_ref, pmod_ref, refs.pop(0))
    q_ref, k_ref, v_ref = refs[:3]
    d = x_ref.shape[1]
    hd = HEAD_DIM
    m = mod_row(pl.program_id(0))
    x = _residual_stream(x_ref, prev, m)
    shift = mod_ref[pl.ds(m, 1), pl.ds(0, d)]
    scale = mod_ref[pl.ds(m, 1), pl.ds(d, d)]
    h = _modulated_norm(x, gain_ref[...], shift, scale).astype(BF16)
    qg = qg_ref[...]
    kg = kg_ref[...]
    q_scale = math.log2(math.e) * hd ** -0.5
    half = x_ref.shape[0] // 2
    for r0 in (0, half):
        rows = slice(r0, r0 + half)
        qkv = jnp.dot(h[rows], w_ref[...], preferred_element_type=F32)
        if rope:
            cos = cos_ref[rows, :]
            sin = sin_ref[rows, :]

            def rot(xh, cos=cos, sin=sin):
                return xh * cos + pltpu.roll(xh, hd // 2, 1) * sin
        else:
            rot = lambda xh: xh

        for i in range(n_heads):
            qh = rot(_head_norm(qkv[:, i * hd:(i + 1) * hd], qg)) * q_scale
            q_ref[rows, i * hd:(i + 1) * hd] = qh.astype(BF16)
        for i in range(n_kv):
            c0 = (n_heads + i) * hd
            kh = _head_norm(qkv[:, c0:c0 + hd], kg)
            if not rope:
                refs[3][rows, i * hd:(i + 1) * hd] = kh
            k_ref[rows, i * hd:(i + 1) * hd] = rot(kh).astype(BF16)
        v = qkv[:, (n_heads + n_kv) * hd:]
        v_ref[rows, :] = v.astype(BF16)
        if not rope:
            refs[4][rows, :] = v


def _qkv(x, prev, mod, gain, w_qkv, q_gain, k_gain, tables, *, seq_len, mod_row, n_heads, n_kv):
    n, d = x.shape
    hd = HEAD_DIM
    row = lambda i: (i, 0)
    fixed = lambda i: (0, 0)
    rope = tables is not None
    in_specs = [pl.BlockSpec((ROW_BLOCK, d), row)]
    args = [x]
    if prev is not None:
        in_specs += [_rows_spec(prev[0], d), pl.BlockSpec(prev[1].shape, fixed)]
        args += list(prev)
    in_specs += [
        pl.BlockSpec(mod.shape, fixed),
        pl.BlockSpec((1, d), fixed),
        pl.BlockSpec(w_qkv.shape, fixed),
        pl.BlockSpec((1, hd), fixed),
        pl.BlockSpec((1, hd), fixed),
    ]
    args += [mod, gain, w_qkv, q_gain, k_gain]
    if rope:
        per_seq = seq_len // ROW_BLOCK
        in_specs += [pl.BlockSpec((ROW_BLOCK, hd), lambda i: (i % per_seq, 0))] * 2
        args += list(tables)
    out_specs, out_shape = [], []
    if prev is not None:
        out_specs.append(pl.BlockSpec((ROW_BLOCK, d), row))
        out_shape.append(jax.ShapeDtypeStruct((n, d), F32))
    out_specs += [pl.BlockSpec((ROW_BLOCK, n_heads * hd), row),
                  pl.BlockSpec((ROW_BLOCK, n_kv * hd), row),
                  pl.BlockSpec((ROW_BLOCK, n_kv * hd), row)]
    out_shape += [jax.ShapeDtypeStruct((n, n_heads * hd), BF16),
                  jax.ShapeDtypeStruct((n, n_kv * hd), BF16),
                  jax.ShapeDtypeStruct((n, n_kv * hd), BF16)]
    if not rope:
        out_specs += [pl.BlockSpec((ROW_BLOCK, n_kv * hd), row)] * 2
        out_shape += [jax.ShapeDtypeStruct((n, n_kv * hd), F32)] * 2
    outs = pl.pallas_call(
        functools.partial(_qkv_kernel, rope=rope, has_prev=prev is not None, mod_row=mod_row,
                          n_heads=n_heads, n_kv=n_kv),
        grid=(n // ROW_BLOCK,),
        in_specs=in_specs,
        out_specs=out_specs,
        out_shape=out_shape,
        compiler_params=_params(1, 48),
        name="qkv_rope" if rope else "qkv",
    )(*args)
    return list(outs) if prev is not None else [x] + list(outs)


def _attn_kernel(*refs, group, chunk, has_cache):
    if has_cache:
        q_ref, k_ref, v_ref, kc_ref, vc_ref, o_ref = refs
    else:
        q_ref, k_ref, v_ref, o_ref = refs
    hd = HEAD_DIM
    qb = q_ref.shape[0]
    rows = group * qb
    q = jnp.concatenate([q_ref[:, g * hd:(g + 1) * hd] for g in range(group)], axis=0)
    sources = [(k_ref, v_ref, s0, chunk) for s0 in range(0, k_ref.shape[0], chunk)]
    if has_cache:
        sources.append((kc_ref.at[0], vc_ref.at[0], 0, kc_ref.shape[1]))
    m = jnp.full((rows, 1), -jnp.inf, F32)
    acc = jnp.zeros((rows, 2 * hd), F32)
    for kr, vr, s0, size in sources:
        s = lax.dot_general(q, kr[s0:s0 + size, :], (((1,), (1,)), ((), ())),
                            preferred_element_type=F32)
        m_new = jnp.maximum(m, jnp.max(s, axis=-1, keepdims=True))
        p = jnp.exp2(s - m_new).astype(BF16)
        v_ones = jnp.concatenate([vr[s0:s0 + size, :], jnp.ones((size, hd), BF16)], axis=1)
        acc = jnp.exp2(m - m_new) * acc + jnp.dot(p, v_ones, preferred_element_type=F32)
        m = m_new
    o = acc[:, :hd] / acc[:, hd:]
    o_ref[...] = jnp.concatenate([o[g * qb:(g + 1) * qb] for g in range(group)], axis=1).astype(BF16)


def _attention(q, k, v, cache, *, seq_len, qb, chunk, n_heads, n_kv):
    n = q.shape[0]
    hd = HEAD_DIM
    group = n_heads // n_kv
    nq = seq_len // qb
    in_specs = [
        pl.BlockSpec((qb, group * hd), lambda b, h, i: (b * nq + i, h)),
        pl.BlockSpec((seq_len, hd), lambda b, h, i: (b, h)),
        pl.BlockSpec((seq_len, hd), lambda b, h, i: (b, h)),
    ]
    args = [q, k, v]
    if cache is not None:
        past = cache[0].shape[1]
        in_specs += [pl.BlockSpec((1, past, hd), lambda b, h, i: (b, 0, h))] * 2
        args += list(cache)
    return pl.pallas_call(
        functools.partial(_attn_kernel, group=group, chunk=chunk, has_cache=cache is not None),
        grid=(n // seq_len, n_kv, nq),
        in_specs=in_specs,
        out_specs=pl.BlockSpec((qb, group * hd), lambda b, h, i: (b * nq + i, h)),
        out_shape=jax.ShapeDtypeStruct((n, n_heads * hd), BF16),
        compiler_params=_params(3, 48),
        name=f"attention_{seq_len}",
    )(*args)


def _final_norm_kernel(x_ref, acc_ref, pmod_ref, gain_ref, o_ref, *, mod_row):
    n_rows, d = x_ref.shape
    m = mod_row(pl.program_id(0))
    x = x_ref[...] + pmod_ref[pl.ds(m, 1), pl.ds(5 * d, d)] * _load_rows(acc_ref, n_rows, d)
    o_ref[...] = (x * lax.rsqrt(jnp.mean(x * x, axis=-1, keepdims=True) + EPS)) * gain_ref[...]


def _final_norm(x, acc, pmod, gain, mod_row):
    n, d = x.shape
    row = lambda i: (i, 0)
    fixed = lambda i: (0, 0)
    return pl.pallas_call(
        functools.partial(_final_norm_kernel, mod_row=mod_row),
        grid=(n // ROW_BLOCK,),
        in_specs=[pl.BlockSpec((ROW_BLOCK, d), row), _rows_spec(acc, d),
                  pl.BlockSpec(pmod.shape, fixed), pl.BlockSpec((1, d), fixed)],
        out_specs=pl.BlockSpec((ROW_BLOCK, d), row),
        out_shape=jax.ShapeDtypeStruct((n, d), F32),
        compiler_params=_params(1, 32),
        name="final_norm",
    )(x, acc, pmod, gain)


def _rope_head_order():
    q = HEAD_DIM // 4
    return jnp.concatenate([jnp.arange(0, q), jnp.arange(2 * q, 3 * q), jnp.arange(q, 2 * q),
                            jnp.arange(3 * q, 4 * q)])


def _rope_tables(seq_len):
    axis = HEAD_DIM // 2
    t = jnp.arange(seq_len)
    inv = ROPE_BASE ** (-jnp.arange(axis // 2, dtype=F32) * 2.0 / axis)
    ang_row = (t // GRID_W).astype(F32)[:, None] * inv
    ang_col = (t % GRID_W).astype(F32)[:, None] * inv
    ang = jnp.concatenate([ang_row, ang_col], axis=1)
    c, s = jnp.cos(ang), jnp.sin(ang)
    return jnp.concatenate([c, c], axis=1), jnp.concatenate([-s, s], axis=1)


def kernel(x_prompt, x_sample, state_lru, cache_k, cache_v, c, c_ctx, w_mod, b_mod, norm_gain, final_gain,
           w_lru_in, lru_conv_w, lru_conv_b, lru_wa, lru_ba, lru_wx, lru_bx, lru_lambda, w_lru_out,
           w_qkv, q_norm, k_norm, w_attn_out, w_router, w_exp_gate, w_exp_up, w_exp_down):
    batch, seq, d = x_prompt.shape
    dec_batch, dec_seq, _ = x_sample.shape
    depth = w_mod.shape[0]
    n_kv = cache_k.shape[3]
    n_heads = w_attn_out.shape[1] // HEAD_DIM
    heads = dict(n_heads=n_heads, n_kv=n_kv)

    cond = jnp.zeros((SUBLANES, d), F32).at[0].set(c_ctx).at[1:1 + dec_batch].set(c)
    mod_all = _mod_vectors(cond, w_mod, b_mod)

    blocks_per_seq = dec_seq // ROW_BLOCK
    xs = [x_prompt.reshape(batch * seq, d), x_sample.reshape(dec_batch * dec_seq, d)]
    lens = [seq, dec_seq]
    mod_rows = [lambda i: 0, lambda i: 1 + i // blocks_per_seq]
    prevs = [None, None]

    fg = final_gain.reshape(1, d)
    new_lru, new_k, new_v = [], [], []
    for l in range(depth):
        mod = mod_all[l]
        gain_a = norm_gain[l, 0].reshape(1, d)
        gain_c = norm_gain[l, 1].reshape(1, d)
        mixed = []
        if l % 2 == 0:
            li = l // 2
            w_in = w_lru_in[li].astype(BF16)
            wg = (0.5 * jnp.concatenate([lru_wa[li, 0], lru_wx[li, 0], lru_wa[li, 1], lru_wx[li, 1]],
                                        axis=2)).astype(BF16)
            bias4 = 0.5 * jnp.stack([lru_ba[li, 0], lru_bx[li, 0], lru_ba[li, 1], lru_bx[li, 1]])
            h0s = [jnp.zeros((batch, 2, w_in.shape[1] // 2), F32), state_lru[:, li]]
            for gi in range(2):
                xs[gi], gate, xr = _lru_in(xs[gi], prevs[gi], mod, gain_a, w_in, mod_rows[gi])
                y, fin = _lru_core(xr, gate, lru_conv_w[li], lru_conv_b[li].reshape(1, -1), wg, bias4,
                                   lru_lambda[li], h0s[gi], seq_len=lens[gi])
                mixed.append(y)
                if gi == 0:
                    new_lru.append(fin.astype(x_prompt.dtype))
            w_out = w_lru_out[li].astype(BF16)
        else:
            ai = l // 2
            w = w_qkv[ai].astype(BF16)
            qg = q_norm[ai].reshape(1, -1)
            kg = k_norm[ai].reshape(1, -1)
            xs[0], q_p, k_p, v_p, kf, vf = _qkv(xs[0], prevs[0], mod, gain_a, w, qg, kg, None,
                                                seq_len=seq, mod_row=mod_rows[0], **heads)
            order = _rope_head_order()
            n_qk = (n_heads + n_kv) * HEAD_DIM
            w_qk = w_qkv[ai][:, :n_qk].reshape(d, n_heads + n_kv, HEAD_DIM)[:, :, order].reshape(d, n_qk)
            w_r = jnp.concatenate([w_qk, w_qkv[ai][:, n_qk:]], axis=1).astype(BF16)
            xs[1], q_s, k_s, v_s = _qkv(xs[1], prevs[1], mod, gain_a, w_r, qg[:, order], kg[:, order],
                                        _rope_tables(dec_seq),
                                        seq_len=dec_seq, mod_row=mod_rows[1], **heads)
            new_k.append(kf.reshape(batch, seq, n_kv, HEAD_DIM))
            new_v.append(vf.reshape(batch, seq, n_kv, HEAD_DIM))
            past = cache_k.shape[2]
            kc = cache_k[:, ai][..., order].reshape(dec_batch, past, n_kv * HEAD_DIM).astype(BF16)
            vc = cache_v[:, ai].reshape(dec_batch, past, n_kv * HEAD_DIM).astype(BF16)
            mixed.append(_attention(q_p, k_p, v_p, None, seq_len=seq, qb=seq, chunk=seq, **heads))
            mixed.append(_attention(q_s, k_s, v_s, (kc, vc), seq_len=dec_seq, qb=256, chunk=256, **heads))
            w_out = w_attn_out[ai].astype(BF16)

        wr_hi, wr_lo = _split_bf16(jnp.pad(w_router[l], ((0, 0), (0, LANES - N_EXPERTS))))
        wr_both = jnp.concatenate([wr_hi, wr_lo], axis=1)
        x1_p, h2_p, lg_p = _mix_out(mixed[0], xs[0], mod, gain_c, w_out, wr_both, wr_hi, mod_rows[0], False)
        x1_s, h2_s, lg_s = _mix_out(mixed[1], xs[1], mod, gain_c, w_out, wr_both, wr_hi, mod_rows[1], True)
        idx_p, g_p, idx_s, g_s = _route(lg_p, lg_s, seq, dec_seq)
        xe_p = _gather_onehot(idx_p[:, :, None], h2_p, seq_len=seq)
        off_s = idx_s * _tiles_per_row(d)
        xe_s = _gather_rows(off_s, h2_s, d, seq_len=dec_seq)
        e_n = N_EXPERTS
        gc_p = g_p.reshape(batch, e_n, -1).transpose(1, 0, 2).reshape(e_n, -1, 1)
        gc_s = g_s.reshape(dec_batch, e_n, -1).transpose(1, 0, 2).reshape(e_n, -1, 1)
        ye_p, ye_s = _ffn(xe_p, xe_s, gc_p, gc_s, w_exp_gate, w_exp_up, w_exp_down, l)
        last = l == depth - 1
        x_p = _combine_onehot(idx_p[:, None, :], ye_p, x1_p, mod, fg if last else None, seq_len=seq,
                              mod_row=lambda s: 0)
        acc_s = _combine_rows(off_s, ye_s, seq_len=dec_seq)
        xs = [x_p, x1_s]
        prevs = [None, (acc_s, mod)]

    y_prompt = xs[0].reshape(batch, seq, d)
    y_sample = _final_norm(xs[1], *prevs[1], fg, mod_rows[1]).reshape(dec_batch, dec_seq, d)
    return (y_prompt, y_sample, jnp.stack(new_lru, axis=1), jnp.stack(new_k, axis=1), jnp.stack(new_v, axis=1))
```

```python
import functools
import math

import jax
import jax.numpy as jnp
from jax import lax
from jax.experimental import pallas as pl
from jax.experimental.pallas import tpu as pltpu

F32 = jnp.float32
BF16 = jnp.bfloat16
I32 = jnp.int32

LANES = 128
SUBLANES = 8
MIB = 1024 * 1024

RG_C = 8.0
CONV_W = 4
CONV_PAD_L = 2
ROPE_BASE = 10000.0
GRID_W = 64
EPS = 1e-6
N_EXPERTS = 16
CAPACITY_FACTOR = 2
HEAD_DIM = 128
LRU_BLOCKS = 8
LRU_STEP_BLOCKS = 2

ROW_BLOCK = 512


def _params(n_axes, vmem_mib):
    return pltpu.CompilerParams(
        dimension_semantics=("arbitrary",) * n_axes, vmem_limit_bytes=vmem_mib * MIB)


def _split_bf16(x):
    hi = x.astype(BF16)
    lo = (x - hi.astype(F32)).astype(BF16)
    return hi, lo


def _modulated_norm(x, gain, shift, scale):
    y = x * lax.rsqrt(jnp.mean(x * x, axis=-1, keepdims=True) + EPS)
    return (y * gain) * (1.0 + scale) + shift


def _tiles_per_row(d):
    return d // LANES


def _load_rows(ref, n_rows, d):
    if ref.shape == (n_rows, d):
        return ref[...]
    tpr = _tiles_per_row(d)
    return jnp.concatenate([ref[pl.ds(s, n_rows, stride=tpr), :] for s in range(tpr)], axis=1)


def _store_rows(ref, x, r0):
    n_rows, d = x.shape
    if ref.shape[1] == d:
        ref[r0:r0 + n_rows, :] = x.astype(ref.dtype)
        return
    tpr = _tiles_per_row(d)
    for s in range(tpr):
        ref[pl.ds(r0 * tpr + s, n_rows, stride=tpr), :] = x[:, s * LANES:(s + 1) * LANES]


def _rows_spec(arr, d):
    if arr.shape[1] == d:
        return pl.BlockSpec((ROW_BLOCK, d), lambda i: (i, 0))
    return pl.BlockSpec((ROW_BLOCK * _tiles_per_row(d), LANES), lambda i: (i, 0))


def _residual_stream(x_ref, prev, m):
    if prev is None:
        return x_ref[...]
    acc_ref, pmod_ref, xo_ref = prev
    n_rows, d = x_ref.shape
    x = x_ref[...] + pmod_ref[pl.ds(m, 1), pl.ds(5 * d, d)] * _load_rows(acc_ref, n_rows, d)
    xo_ref[...] = x
    return x


def _mod_kernel(cond_ref, w_ref, b_ref, o_ref):
    c = cond_ref[...]
    a_hi, a_lo = _split_bf16(c * jax.nn.sigmoid(c))
    w_hi, w_lo = _split_bf16(w_ref[0])
    dot = functools.partial(jnp.dot, preferred_element_type=F32)
    o_ref[0] = dot(a_hi, w_hi) + (dot(a_lo, w_hi) + dot(a_hi, w_lo)) + b_ref[0]


def _mod_vectors(cond8, w_mod, b_mod):
    depth, d, d6 = w_mod.shape
    nb = 1536
    return pl.pallas_call(
        _mod_kernel,
        grid=(depth, d6 // nb),
        in_specs=[
            pl.BlockSpec((SUBLANES, d), lambda l, j: (0, 0)),
            pl.BlockSpec((1, d, nb), lambda l, j: (l, 0, j)),
            pl.BlockSpec((1, 1, nb), lambda l, j: (l, 0, j)),
        ],
        out_specs=pl.BlockSpec((1, SUBLANES, nb), lambda l, j: (l, 0, j)),
        out_shape=jax.ShapeDtypeStruct((depth, SUBLANES, d6), F32),
        compiler_params=_params(2, 40),
        name="adaln_mod",
    )(cond8, w_mod, b_mod.reshape(depth, 1, d6))


def _lru_in_kernel(*refs, mod_row, has_prev):
    if has_prev:
        x_ref, acc_ref, pmod_ref, mod_ref, gain_ref, w_ref, xo_ref, gate_ref, xr_ref = refs
        prev = (acc_ref, pmod_ref, xo_ref)
    else:
        x_ref, mod_ref, gain_ref, w_ref, gate_ref, xr_ref = refs
        prev = None
    d = x_ref.shape[1]
    m = mod_row(pl.program_id(0))
    x = _residual_stream(x_ref, prev, m)
    shift = mod_ref[pl.ds(m, 1), pl.ds(0, d)]
    scale = mod_ref[pl.ds(m, 1), pl.ds(d, d)]
    h = _modulated_norm(x, gain_ref[...], shift, scale).astype(BF16)
    u = jnp.dot(h, w_ref[...], preferred_element_type=F32)
    w = gate_ref.shape[1]
    gate_ref[...] = u[:, :w]
    xr_ref[...] = u[:, w:]


def _lru_in(x, prev, mod, gain, w_in, mod_row):
    n, d = x.shape
    w2 = w_in.shape[1]
    w = w2 // 2
    row = lambda i: (i, 0)
    fixed = lambda i: (0, 0)
    in_specs = [pl.BlockSpec((ROW_BLOCK, d), row)]
    args = [x]
    out_specs = [pl.BlockSpec((ROW_BLOCK, w), row), pl.BlockSpec((ROW_BLOCK, w), row)]
    out_shape = [jax.ShapeDtypeStruct((n, w), F32), jax.ShapeDtypeStruct((n, w), F32)]
    if prev is not None:
        in_specs += [_rows_spec(prev[0], d), pl.BlockSpec(prev[1].shape, fixed)]
        args += list(prev)
        out_specs = [pl.BlockSpec((ROW_BLOCK, d), row)] + out_specs
        out_shape = [jax.ShapeDtypeStruct((n, d), F32)] + out_shape
    in_specs += [pl.BlockSpec(mod.shape, fixed), pl.BlockSpec((1, d), fixed), pl.BlockSpec((d, w2), fixed)]
    args += [mod, gain, w_in]
    outs = pl.pallas_call(
        functools.partial(_lru_in_kernel, mod_row=mod_row, has_prev=prev is not None),
        grid=(n // ROW_BLOCK,),
        in_specs=in_specs,
        out_specs=out_specs,
        out_shape=out_shape,
        compiler_params=_params(1, 48),
        name="lru_in",
    )(*args)
    return outs if prev is not None else [x] + list(outs)


def _lru_core_kernel(xr_ref, gate_ref, cw_ref, cb_ref, wg_ref, bias_ref, lam_ref, h0_ref,
                     y_ref, fin_ref, pad, a_f, b_f, a_b, b_b):
    t_len, lb = xr_ref.shape
    ch = 256
    halo = SUBLANES
    seg = t_len // SUBLANES
    piece = min(ch, seg)
    seg_shift = seg.bit_length() - 1

    def seg_rows(t0):
        r = lax.shift_right_logical(t0, seg_shift)
        return pl.ds((t0 - r * seg) * SUBLANES + r, piece, stride=SUBLANES)

    zero_rows = jnp.zeros((halo, lb), F32)
    pad[0:halo, :] = zero_rows
    pad[t_len + halo:t_len + 2 * halo, :] = zero_rows

    def copy_in(c, carry):
        r = pl.multiple_of(c * ch, ch)
        pad[pl.ds(r + halo, ch), :] = xr_ref[pl.ds(r, ch), :]
        return carry

    lax.fori_loop(0, t_len // ch, copy_in, 0)

    neg_lam = -lam_ref[...]
    softplus = jnp.maximum(neg_lam, 0.0) + jnp.log1p(jnp.exp(-jnp.abs(neg_lam)))
    c_nla = (0.5 * RG_C) * softplus
    c_exp2 = (-0.5 * RG_C * math.log2(math.e)) * softplus
    cw = cw_ref[...]
    cbias = cb_ref[...]
    half_bias = bias_ref[...]
    half_wg = wg_ref[...]
    bw = half_wg.shape[1]
    n_slab = lb // LANES

    def gates(c, carry):
        r = pl.multiple_of(c * ch, ch)
        blk = pad[pl.ds(r, ch + 2 * halo), :]
        xc = cbias
        for k in range(CONV_W):
            o = halo - CONV_PAD_L + k
            xc = xc + blk[o:o + ch] * cw[k:k + 1]
        xc_bf = xc.astype(BF16)
        g = [jnp.dot(xc_bf[:, j * bw:(j + 1) * bw], half_wg[j], preferred_element_type=F32)
             for j in range(lb // bw)]
        pre = lambda k: jnp.concatenate([gj[:, k * bw:(k + 1) * bw] for gj in g], axis=1) + half_bias[k:k + 1]
        half_xc = 0.5 * xc
        for d, (a_ref, b_ref) in enumerate(((a_f, b_f), (a_b, b_b))):
            u = jnp.tanh(pre(2 * d)) + 1.0
            t_i = jnp.tanh(pre(2 * d + 1))
            a = jnp.exp2(u * c_exp2[d:d + 1])
            b = jnp.sqrt(jnp.tanh(u * c_nla[d:d + 1]) * (1.0 + a * a)) * (t_i * half_xc + half_xc)
            for p in range(ch // piece):
                rows = seg_rows(r + p * piece)
                for j in range(n_slab):
                    a_ref[j, rows, :] = a[p * piece:(p + 1) * piece, j * LANES:(j + 1) * LANES]
                    b_ref[j, rows, :] = b[p * piece:(p + 1) * piece, j * LANES:(j + 1) * LANES]
        return carry

    lax.fori_loop(0, t_len // ch, gates, 0)

    def scan(i, carry):
        rows = (pl.ds(pl.multiple_of(i * SUBLANES, SUBLANES), SUBLANES),
                pl.ds(pl.multiple_of((seg - 1 - i) * SUBLANES, SUBLANES), SUBLANES))
        out = []
        for k, (h, prod) in enumerate(carry):
            d, j = divmod(k, n_slab)
            a_ref, b_ref = ((a_f, b_f), (a_b, b_b))[d]
            a = a_ref[j, rows[d], :]
            h = a * h + b_ref[j, rows[d], :]
            prod = a * prod
            b_ref[j, rows[d], :] = h
            a_ref[j, rows[d], :] = prod
            out.append((h, prod))
        return tuple(out)

    start = (jnp.zeros((SUBLANES, LANES), F32), jnp.ones((SUBLANES, LANES), F32))
    ends = lax.fori_loop(0, seg, scan, (start,) * (2 * n_slab), unroll=8)

    enter = []
    for k, (h, prod) in enumerate(ends):
        d, j = divmod(k, n_slab)
        states = [h0_ref[0, d:d + 1, j * LANES:(j + 1) * LANES]]
        for r in (range(SUBLANES) if d == 0 else reversed(range(SUBLANES))):
            states.append(h[r:r + 1] + prod[r:r + 1] * states[-1])
        fin_ref[0, d:d + 1, j * LANES:(j + 1) * LANES] = states[SUBLANES]
        order = states[:SUBLANES] if d == 0 else states[:SUBLANES][::-1]
        enter.append(jnp.concatenate(order, axis=0))

    def fix(i, carry):
        rows = pl.ds(pl.multiple_of(i * SUBLANES, SUBLANES), SUBLANES)
        for j in range(n_slab):
            b_f[j, rows, :] = ((b_f[j, rows, :] + a_f[j, rows, :] * enter[j])
                               + (b_b[j, rows, :] + a_b[j, rows, :] * enter[n_slab + j]))
        return carry

    lax.fori_loop(0, seg, fix, 0, unroll=8)

    def emit(c, carry):
        r = pl.multiple_of(c * ch, ch)
        gt = gate_ref[pl.ds(r, ch), :]
        cdf = 0.5 * (1.0 + jnp.tanh(math.sqrt(2.0 / math.pi) * (gt + 0.044715 * (gt * gt * gt))))
        y = gt * cdf
        for p in range(ch // piece):
            rows = seg_rows(r + p * piece)
            total = jnp.concatenate([b_f[j, rows, :] for j in range(n_slab)], axis=1)
            y_ref[pl.ds(r + p * piece, piece), :] = (total * y[p * piece:(p + 1) * piece]).astype(BF16)
        return carry

    lax.fori_loop(0, t_len // ch, emit, 0)


def _lru_core(xr, gate, conv_w, conv_b, wg, bias4, lam, h0, *, seq_len):
    n, w = xr.shape
    n_seq = n // seq_len
    bw = w // LRU_BLOCKS
    lb = LRU_STEP_BLOCKS * bw
    tok = lambda s, c: (s, c)
    chan = lambda s, c: (0, c)
    return pl.pallas_call(
        _lru_core_kernel,
        grid=(n_seq, LRU_BLOCKS // LRU_STEP_BLOCKS),
        in_specs=[
            pl.BlockSpec((seq_len, lb), tok),
            pl.BlockSpec((seq_len, lb), tok),
            pl.BlockSpec((CONV_W, lb), chan),
            pl.BlockSpec((1, lb), chan),
            pl.BlockSpec((LRU_STEP_BLOCKS, bw, 4 * bw), lambda s, c: (c, 0, 0)),
            pl.BlockSpec((4, lb), chan),
            pl.BlockSpec((2, lb), chan),
            pl.BlockSpec((1, 2, lb), lambda s, c: (s, 0, c)),
        ],
        out_specs=[pl.BlockSpec((seq_len, lb), tok),
                   pl.BlockSpec((1, 2, lb), lambda s, c: (s, 0, c))],
        out_shape=[jax.ShapeDtypeStruct((n, w), BF16), jax.ShapeDtypeStruct((n_seq, 2, w), F32)],
        scratch_shapes=[pltpu.VMEM((seq_len + 2 * SUBLANES, lb), F32)]
        + [pltpu.VMEM((lb // LANES, seq_len, LANES), F32)] * 4,
        compiler_params=_params(2, 48),
        name=f"lru_core_{seq_len}",
    )(xr, gate, conv_w, conv_b, wg, bias4, lam, h0)


def _mix_out_kernel(a_ref, x_ref, mod_ref, gain_ref, w_ref, wrh_ref, wrl_ref,
                    x1_ref, h2_ref, lg_ref, *, mod_row):
    d = x_ref.shape[1]
    m = mod_row(pl.program_id(0))
    g_mix = mod_ref[pl.ds(m, 1), pl.ds(2 * d, d)]
    shift = mod_ref[pl.ds(m, 1), pl.ds(3 * d, d)]
    scale = mod_ref[pl.ds(m, 1), pl.ds(4 * d, d)]
    op = jnp.dot(a_ref[...], w_ref[...], preferred_element_type=F32)
    x1 = x_ref[...] + g_mix * op
    x1_ref[...] = x1
    h2 = _modulated_norm(x1, gain_ref[...], shift, scale)
    h_hi, h_lo = _split_bf16(h2)
    _store_rows(h2_ref, h2, 0)
    nt = functools.partial(lax.dot_general, dimension_numbers=(((1,), (1,)), ((), ())),
                           preferred_element_type=F32)
    wr_hi = wrh_ref[0]
    lg_ref[...] = nt(wr_hi, h_hi) + (nt(wrl_ref[0], h_hi) + nt(wr_hi, h_lo))


def _mix_out(a, x, mod, gain, w_out, wr_hi, wr_lo, layer, mod_row, token_tiled):
    n, d = x.shape
    e = wr_hi.shape[1]
    row = lambda i: (i, 0)
    fixed = lambda i: (0, 0)
    router = pl.BlockSpec((1,) + wr_hi.shape[1:], lambda i: (layer, 0, 0))
    tpr = _tiles_per_row(d)
    if token_tiled:
        h2_spec = pl.BlockSpec((ROW_BLOCK * tpr, LANES), row)
        h2_shape = jax.ShapeDtypeStruct((n * tpr, LANES), F32)
    else:
        h2_spec = pl.BlockSpec((ROW_BLOCK, d), row)
        h2_shape = jax.ShapeDtypeStruct((n, d), BF16)
    return pl.pallas_call(
        functools.partial(_mix_out_kernel, mod_row=mod_row),
        grid=(n // ROW_BLOCK,),
        in_specs=[
            pl.BlockSpec((ROW_BLOCK, a.shape[1]), row),
            pl.BlockSpec((ROW_BLOCK, d), row),
            pl.BlockSpec(mod.shape, fixed),
            pl.BlockSpec((1, d), fixed),
            pl.BlockSpec(w_out.shape, fixed),
            router,
            router,
        ],
        out_specs=[pl.BlockSpec((ROW_BLOCK, d), row), h2_spec,
                   pl.BlockSpec((e, ROW_BLOCK), lambda i: (0, i))],
        out_shape=[jax.ShapeDtypeStruct((n, d), F32), h2_shape, jax.ShapeDtypeStruct((e, n), F32)],
        compiler_params=_params(1, 48),
        name="mix_out",
    )(a, x, mod, gain, w_out, wr_hi, wr_lo)


def _lane_sum(tiles):
    acc = tiles[0]
    for t in tiles[1:]:
        acc = acc + t
    return jnp.sum(acc, axis=1, keepdims=True)


def _exclusive_cumsum(flags, upper):
    out = []
    off = jnp.zeros((flags[0].shape[0], 1), F32)
    for f in flags:
        out.append(jnp.dot(f.astype(BF16), upper, preferred_element_type=F32) + off)
        off = off + jnp.sum(f, axis=1, keepdims=True)
    return out


def _route_group(lg_ref, n_seq, t_len, idx_ref, g_ref):
    cap = CAPACITY_FACTOR * t_len // N_EXPERTS
    affs = []
    for s in range(n_seq):
        lg = lg_ref[:, s * t_len:(s + 1) * t_len]
        ex = jnp.exp(lg - jnp.max(lg, axis=0, keepdims=True))
        affs.append(ex / jnp.sum(ex, axis=0, keepdims=True))
    aff = jnp.concatenate(affs, axis=0)
    n_rows = aff.shape[0]
    nt = t_len // LANES
    g = [aff[:, j * LANES:(j + 1) * LANES] for j in range(nt)]

    kth_bits = jnp.zeros((n_rows, 1), I32)
    for bit in range(30, -1, -1):
        cand = kth_bits | (1 << bit)
        cand_f = pltpu.bitcast(cand, F32)
        cnt = _lane_sum([jnp.where(t >= cand_f, 1, 0) for t in g])
        kth_bits = jnp.where(cnt >= cap, cand, kth_bits)
    kth = pltpu.bitcast(kth_bits, F32)

    lane = lax.broadcasted_iota(I32, (LANES, LANES), 0)
    upper = jnp.where(lane < lax.broadcasted_iota(I32, (LANES, LANES), 1), 1.0, 0.0).astype(BF16)
    gt = [t > kth for t in g]
    eq = [t == kth for t in g]
    need = (cap - _lane_sum([jnp.where(m, 1, 0) for m in gt])).astype(F32)
    eq_rank = _exclusive_cumsum([jnp.where(m, 1.0, 0.0) for m in eq], upper)
    sel = [jnp.logical_or(gt[j], jnp.logical_and(eq[j], eq_rank[j] < need)) for j in range(nt)]
    pos = _exclusive_cumsum([jnp.where(m, 1.0, 0.0) for m in sel], upper)

    lane_r = lax.broadcasted_iota(I32, (n_rows, LANES), 1)
    d = [jnp.where(sel[j], lane_r + j * LANES - pos[j].astype(I32), -1) for j in range(nt)]
    for k in range(t_len.bit_length() - 1):
        s = 1 << k
        if s < LANES:
            d_rot = [pltpu.roll(x, LANES - s, 1) for x in d]
            g_rot = [pltpu.roll(x, LANES - s, 1) for x in g]
            same = lane_r < LANES - s
            d_in = [jnp.where(same, d_rot[j], d_rot[(j + 1) % nt]) for j in range(nt)]
            g_in = [jnp.where(same, g_rot[j], g_rot[(j + 1) % nt]) for j in range(nt)]
        else:
            q = s // LANES
            d_in = [d[(j + q) % nt] for j in range(nt)]
            g_in = [g[(j + q) % nt] for j in range(nt)]
        new_d, new_g = [], []
        for j in range(nt):
            move = jnp.logical_and(d_in[j] >= 0, ((d_in[j] >> k) & 1) == 1)
            stay = jnp.logical_and(d[j] >= 0, ((d[j] >> k) & 1) == 0)
            new_d.append(jnp.where(move, d_in[j], jnp.where(stay, d[j], -1)))
            new_g.append(jnp.where(move, g_in[j], g[j]))
        d, g = new_d, new_g

    for j in range(idx_ref.shape[1] // LANES):
        idx_ref[:, j * LANES:(j + 1) * LANES] = lane_r + j * LANES + d[j]
        g_ref[:, j * LANES:(j + 1) * LANES] = g[j]


def _route_kernel(lgp_ref, lgs_ref, idx_p_ref, g_p_ref, idx_s_ref, g_s_ref, *, p_len, s_len):
    _route_group(lgp_ref, lgp_ref.shape[1] // p_len, p_len, idx_p_ref, g_p_ref)
    _route_group(lgs_ref, lgs_ref.shape[1] // s_len, s_len, idx_s_ref, g_s_ref)


def _route(lg_p, lg_s, p_len, s_len):
    e = lg_p.shape[0]
    np_seq = lg_p.shape[1] // p_len
    ns_seq = lg_s.shape[1] // s_len
    cap_p = CAPACITY_FACTOR * p_len // N_EXPERTS
    cap_s = CAPACITY_FACTOR * s_len // N_EXPERTS
    wp = max(cap_p, LANES)
    ws = max(cap_s, LANES)
    idx_p, g_p, idx_s, g_s = pl.pallas_call(
        functools.partial(_route_kernel, p_len=p_len, s_len=s_len),
        out_shape=[jax.ShapeDtypeStruct((np_seq * e, wp), I32),
                   jax.ShapeDtypeStruct((np_seq * e, wp), F32),
                   jax.ShapeDtypeStruct((ns_seq * e, ws), I32),
                   jax.ShapeDtypeStruct((ns_seq * e, ws), F32)],
        compiler_params=pltpu.CompilerParams(vmem_limit_bytes=40 * MIB),
        name="route",
    )(lg_p, lg_s)
    return (idx_p[:, :cap_p].reshape(np_seq, e * cap_p), g_p[:, :cap_p].reshape(np_seq, e * cap_p),
            idx_s[:, :cap_s].reshape(-1), g_s[:, :cap_s].reshape(-1))


def _gather_onehot_kernel(idx_ref, h_ref, o_ref):
    t_len = h_ref.shape[0]
    n_exp, cap, _ = o_ref.shape
    idx = idx_ref[0]
    onehot = jnp.where(idx == lax.broadcasted_iota(I32, (idx.shape[0], t_len), 1), 1.0, 0.0).astype(BF16)
    xs = jnp.dot(onehot, h_ref[...], preferred_element_type=F32).astype(BF16)
    for e in range(n_exp):
        o_ref[e] = xs[e * cap:(e + 1) * cap]


def _gather_onehot(idx_col, h, *, seq_len):
    n, d = h.shape
    n_seq, slots, _ = idx_col.shape
    cap = slots // N_EXPERTS
    return pl.pallas_call(
        _gather_onehot_kernel,
        grid=(n_seq,),
        in_specs=[pl.BlockSpec((1, slots, 1), lambda s: (s, 0, 0)),
                  pl.BlockSpec((seq_len, d), lambda s: (s, 0))],
        out_specs=pl.BlockSpec((N_EXPERTS, cap, d), lambda s: (0, s, 0)),
        out_shape=jax.ShapeDtypeStruct((N_EXPERTS, n_seq * cap, d), BF16),
        compiler_params=_params(1, 32),
        name="gather_onehot",
    )(idx_col, h)


def _gather_rows_kernel(idx_ref, h_ref, o_ref, tiles, *, cap, n_exp):
    tpr = _tiles_per_row(o_ref.shape[2])
    base = (pl.program_id(0) * n_exp + pl.program_id(1)) * cap

    def body(c, carry):
        src = pl.multiple_of(idx_ref[base + c], tpr)
        tiles[pl.ds(pl.multiple_of(c * tpr, tpr), tpr), :] = h_ref[pl.ds(src, tpr), :]
        return carry

    lax.fori_loop(0, cap, body, 0, unroll=8)
    for s in range(tpr):
        o_ref[0, :, s * LANES:(s + 1) * LANES] = tiles[pl.ds(s, cap, stride=tpr), :].astype(BF16)


def _gather_rows(idx, h, d, *, seq_len):
    tpr = _tiles_per_row(d)
    n_seq = h.shape[0] // (seq_len * tpr)
    cap = CAPACITY_FACTOR * seq_len // N_EXPERTS
    return pl.pallas_call(
        functools.partial(_gather_rows_kernel, cap=cap, n_exp=N_EXPERTS),
        grid_spec=pltpu.PrefetchScalarGridSpec(
            num_scalar_prefetch=1,
            grid=(n_seq, N_EXPERTS),
            in_specs=[pl.BlockSpec((seq_len * tpr, LANES), lambda s, e, idx: (s, 0))],
            out_specs=pl.BlockSpec((1, cap, d), lambda s, e, idx: (e, s, 0)),
            scratch_shapes=[pltpu.VMEM((cap * tpr, LANES), F32)],
        ),
        out_shape=jax.ShapeDtypeStruct((N_EXPERTS, n_seq * cap, d), BF16),
        compiler_params=_params(2, 48),
        name="gather_rows",
    )(idx, h)


def _ffn_kernel(xp_ref, xs_ref, gp_ref, gs_ref, wg_ref, wu_ref, wd_ref, yp_ref, ys_ref):
    rc = 512

    @pl.when(pl.program_id(1) == 0)
    def _():
        for y_ref in (yp_ref, ys_ref):
            for r in range(0, y_ref.shape[1], rc):
                y_ref[0, r:r + rc, :] = jnp.zeros((rc, y_ref.shape[2]), F32)

    wg = wg_ref[0, 0].astype(BF16)
    wu = wu_ref[0, 0].astype(BF16)
    wd = wd_ref[0, 0].astype(BF16)
    last = pl.program_id(1) == pl.num_programs(1) - 1
    for x_ref, g_ref, y_ref in ((xp_ref, gp_ref, yp_ref), (xs_ref, gs_ref, ys_ref)):
        for r in range(0, x_ref.shape[1], rc):
            x = x_ref[0, r:r + rc, :]
            hg = jnp.dot(x, wg, preferred_element_type=F32)
            hu = jnp.dot(x, wu, preferred_element_type=F32)
            hid = ((hg * jax.nn.sigmoid(hg)) * hu).astype(BF16)
            weight = jnp.where(last, g_ref[0, r:r + rc, :], 1.0)
            y_ref[0, r:r + rc, :] = (y_ref[0, r:r + rc, :] + jnp.dot(hid, wd, preferred_element_type=F32)) * weight


def _ffn(xs_p, xs_s, g_p, g_s, w_gate, w_up, w_down, layer):
    n_exp, rp, d = xs_p.shape
    rs = xs_s.shape[1]
    ff = w_gate.shape[3]
    fc = 512
    return pl.pallas_call(
        _ffn_kernel,
        grid=(n_exp, ff // fc),
        in_specs=[
            pl.BlockSpec((1, rp, d), lambda e, f: (e, 0, 0)),
            pl.BlockSpec((1, rs, d), lambda e, f: (e, 0, 0)),
            pl.BlockSpec((1, rp, 1), lambda e, f: (e, 0, 0)),
            pl.BlockSpec((1, rs, 1), lambda e, f: (e, 0, 0)),
            pl.BlockSpec((1, 1, d, fc), lambda e, f: (layer, e, 0, f)),
            pl.BlockSpec((1, 1, d, fc), lambda e, f: (layer, e, 0, f)),
            pl.BlockSpec((1, 1, fc, d), lambda e, f: (layer, e, f, 0)),
        ],
        out_specs=[pl.BlockSpec((1, rp, d), lambda e, f: (e, 0, 0)),
                   pl.BlockSpec((1, rs, d), lambda e, f: (e, 0, 0))],
        out_shape=[jax.ShapeDtypeStruct((n_exp, rp, d), F32),
                   jax.ShapeDtypeStruct((n_exp, rs, d), F32)],
        compiler_params=_params(2, 56),
        name="expert_ffn",
    )(xs_p, xs_s, g_p, g_s, w_gate, w_up, w_down)


def _combine_onehot_kernel(*refs, mod_row, final):
    if final:
        idx_ref, ye_ref, x_ref, pmod_ref, gain_ref, o_ref = refs
    else:
        idx_ref, ye_ref, x_ref, pmod_ref, o_ref = refs
    t_len, d = o_ref.shape
    n_exp = ye_ref.shape[0]
    idx = idx_ref[0]
    hit = jnp.where(idx == lax.broadcasted_iota(I32, (t_len, idx.shape[1]), 0), 1.0, 0.0).astype(BF16)
    y_hi, y_lo = _split_bf16(jnp.concatenate([ye_ref[e] for e in range(n_exp)], axis=0))
    acc = jnp.dot(hit, y_hi, preferred_element_type=F32) + jnp.dot(hit, y_lo, preferred_element_type=F32)
    x = x_ref[...] + pmod_ref[pl.ds(mod_row(pl.program_id(0)), 1), pl.ds(5 * d, d)] * acc
    if final:
        x = (x * lax.rsqrt(jnp.mean(x * x, axis=-1, keepdims=True) + EPS)) * gain_ref[...]
    o_ref[...] = x


def _combine_onehot(idx_row, ye, x, pmod, final_gain, *, seq_len, mod_row):
    n_exp, rows, d = ye.shape
    n_seq, _, slots = idx_row.shape
    cap = slots // n_exp
    in_specs = [pl.BlockSpec((1, 1, slots), lambda s: (s, 0, 0)),
                pl.BlockSpec((n_exp, cap, d), lambda s: (0, s, 0)),
                pl.BlockSpec((seq_len, d), lambda s: (s, 0)),
                pl.BlockSpec(pmod.shape, lambda s: (0, 0))]
    args = [idx_row, ye, x, pmod]
    if final_gain is not None:
        in_specs.append(pl.BlockSpec((1, d), lambda s: (0, 0)))
        args.append(final_gain)
    return pl.pallas_call(
        functools.partial(_combine_onehot_kernel, mod_row=mod_row, final=final_gain is not None),
        grid=(n_seq,),
        in_specs=in_specs,
        out_specs=pl.BlockSpec((seq_len, d), lambda s: (s, 0)),
        out_shape=jax.ShapeDtypeStruct((n_seq * seq_len, d), F32),
        compiler_params=_params(1, 32),
        name="combine_onehot",
    )(*args)


COMBINE_GROUP = 16


def _combine_rows_kernel(idx_ref, ye_ref, o_ref, tiles, *, cap, n_exp):
    e = pl.program_id(1)
    d = ye_ref.shape[2]
    tpr = _tiles_per_row(d)
    ch = 2048

    @pl.when(e == 0)
    def _():
        def zero(c, carry):
            o_ref[pl.ds(pl.multiple_of(c * ch, ch), ch), :] = jnp.zeros((ch, LANES), F32)
            return carry
        lax.fori_loop(0, o_ref.shape[0] // ch, zero, 0)

    for s in range(tpr):
        tiles[pl.ds(s, cap, stride=tpr), :] = ye_ref[0, :, s * LANES:(s + 1) * LANES]

    base = (pl.program_id(0) * n_exp + e) * cap

    def body(i, carry):
        c0 = i * COMBINE_GROUP
        dst = [pl.ds(pl.multiple_of(idx_ref[base + c0 + k], tpr), tpr) for k in range(COMBINE_GROUP)]
        vals = [o_ref[dst[k], :] + tiles[pl.ds(pl.multiple_of((c0 + k) * tpr, tpr), tpr), :]
                for k in range(COMBINE_GROUP)]
        for k in range(COMBINE_GROUP):
            o_ref[dst[k], :] = vals[k]
        return carry

    lax.fori_loop(0, cap // COMBINE_GROUP, body, 0)


def _combine_rows(idx, ye, *, seq_len):
    n_exp, rows, d = ye.shape
    tpr = _tiles_per_row(d)
    cap = CAPACITY_FACTOR * seq_len // N_EXPERTS
    n_seq = rows // cap
    return pl.pallas_call(
        functools.partial(_combine_rows_kernel, cap=cap, n_exp=n_exp),
        grid_spec=pltpu.PrefetchScalarGridSpec(
            num_scalar_prefetch=1,
            grid=(n_seq, n_exp),
            in_specs=[pl.BlockSpec((1, cap, d), lambda s, e, i: (e, s, 0))],
            out_specs=pl.BlockSpec((seq_len * tpr, LANES), lambda s, e, i: (s, 0)),
            scratch_shapes=[pltpu.VMEM((cap * tpr, LANES), F32)],
        ),
        out_shape=jax.ShapeDtypeStruct((n_seq * seq_len * tpr, LANES), F32),
        compiler_params=_params(2, 48),
        name="combine_rows",
    )(idx, ye)


def _head_norm(x, gain):
    return x * lax.rsqrt(jnp.mean(x * x, axis=-1, keepdims=True) + EPS) * gain


def _qkv_kernel(*refs, rope, has_prev, mod_row, n_heads, n_kv):
    refs = list(refs)
    x_ref = refs.pop(0)
    prev = None
    if has_prev:
        acc_ref, pmod_ref = refs.pop(0), refs.pop(0)
    mod_ref, gain_ref, w_ref, qg_ref, kg_ref = refs[:5]
    refs = refs[5:]
    if rope:
        cos_ref, sin_ref = refs.pop(0), refs.pop(0)
    if has_prev:
        prev = (acc_ref, pmod_ref, refs.pop(0))
    q_ref, k_ref, v_ref = refs[:3]
    d = x_ref.shape[1]
    hd = HEAD_DIM
    m = mod_row(pl.program_id(0))
    x = _residual_stream(x_ref, prev, m)
    shift = mod_ref[pl.ds(m, 1), pl.ds(0, d)]
    scale = mod_ref[pl.ds(m, 1), pl.ds(d, d)]
    h = _modulated_norm(x, gain_ref[...], shift, scale).astype(BF16)
    qg = qg_ref[...]
    kg = kg_ref[...]
    q_scale = math.log2(math.e) * hd ** -0.5
    half = x_ref.shape[0] // 2
    for r0 in (0, half):
        rows = slice(r0, r0 + half)
        qkv = jnp.dot(h[rows], w_ref[...], preferred_element_type=F32)
        if rope:
            cos = cos_ref[rows, :]
            sin = sin_ref[rows, :]

            def rot(xh, cos=cos, sin=sin):
                return xh * cos + pltpu.roll(xh, hd // 2, 1) * sin
        else:
            rot = lambda xh: xh

        for i in range(n_heads):
            qh = rot(_head_norm(qkv[:, i * hd:(i + 1) * hd], qg)) * q_scale
            q_ref[rows, i * hd:(i + 1) * hd] = qh.astype(BF16)
        for i in range(n_kv):
            c0 = (n_heads + i) * hd
            kh = _head_norm(qkv[:, c0:c0 + hd], kg)
            if not rope:
                refs[3][rows, i * hd:(i + 1) * hd] = kh
            k_ref[rows, i * hd:(i + 1) * hd] = rot(kh).astype(BF16)
        v = qkv[:, (n_heads + n_kv) * hd:]
        v_ref[rows, :] = v.astype(BF16)
        if not rope:
            refs[4][rows, :] = v


def _qkv(x, prev, mod, gain, w_qkv, q_gain, k_gain, tables, *, seq_len, mod_row, n_heads, n_kv):
    n, d = x.shape
    hd = HEAD_DIM
    row = lambda i: (i, 0)
    fixed = lambda i: (0, 0)
    rope = tables is not None
    in_specs = [pl.BlockSpec((ROW_BLOCK, d), row)]
    args = [x]
    if prev is not None:
        in_specs += [_rows_spec(prev[0], d), pl.BlockSpec(prev[1].shape, fixed)]
        args += list(prev)
    in_specs += [
        pl.BlockSpec(mod.shape, fixed),
        pl.BlockSpec((1, d), fixed),
        pl.BlockSpec(w_qkv.shape, fixed),
        pl.BlockSpec((1, hd), fixed),
        pl.BlockSpec((1, hd), fixed),
    ]
    args += [mod, gain, w_qkv, q_gain, k_gain]
    if rope:
        per_seq = seq_len // ROW_BLOCK
        in_specs += [pl.BlockSpec((ROW_BLOCK, hd), lambda i: (i % per_seq, 0))] * 2
        args += list(tables)
    out_specs, out_shape = [], []
    if prev is not None:
        out_specs.append(pl.BlockSpec((ROW_BLOCK, d), row))
        out_shape.append(jax.ShapeDtypeStruct((n, d), F32))
    out_specs += [pl.BlockSpec((ROW_BLOCK, n_heads * hd), row),
                  pl.BlockSpec((ROW_BLOCK, n_kv * hd), row),
                  pl.BlockSpec((ROW_BLOCK, n_kv * hd), row)]
    out_shape += [jax.ShapeDtypeStruct((n, n_heads * hd), BF16),
                  jax.ShapeDtypeStruct((n, n_kv * hd), BF16),
                  jax.ShapeDtypeStruct((n, n_kv * hd), BF16)]
    if not rope:
        out_specs += [pl.BlockSpec((ROW_BLOCK, n_kv * hd), row)] * 2
        out_shape += [jax.ShapeDtypeStruct((n, n_kv * hd), F32)] * 2
    outs = pl.pallas_call(
        functools.partial(_qkv_kernel, rope=rope, has_prev=prev is not None, mod_row=mod_row,
                          n_heads=n_heads, n_kv=n_kv),
        grid=(n // ROW_BLOCK,),
        in_specs=in_specs,
        out_specs=out_specs,
        out_shape=out_shape,
        compiler_params=_params(1, 48),
        name="qkv_rope" if rope else "qkv",
    )(*args)
    return list(outs) if prev is not None else [x] + list(outs)


def _attn_kernel(*refs, group, chunk, has_cache):
    if has_cache:
        q_ref, k_ref, v_ref, kc_ref, vc_ref, o_ref = refs
    else:
        q_ref, k_ref, v_ref, o_ref = refs
    hd = HEAD_DIM
    qb = q_ref.shape[0]
    rows = group * qb
    q = jnp.concatenate([q_ref[:, g * hd:(g + 1) * hd] for g in range(group)], axis=0)
    sources = [(k_ref, v_ref, s0, chunk) for s0 in range(0, k_ref.shape[0], chunk)]
    if has_cache:
        sources.append((kc_ref.at[0], vc_ref.at[0], 0, kc_ref.shape[1]))
    m = jnp.full((rows, 1), -jnp.inf, F32)
    acc = jnp.zeros((rows, 2 * hd), F32)
    for kr, vr, s0, size in sources:
        s = lax.dot_general(q, kr[s0:s0 + size, :], (((1,), (1,)), ((), ())),
                            preferred_element_type=F32)
        m_new = jnp.maximum(m, jnp.max(s, axis=-1, keepdims=True))
        p = jnp.exp2(s - m_new).astype(BF16)
        v_ones = jnp.concatenate([vr[s0:s0 + size, :], jnp.ones((size, hd), BF16)], axis=1)
        acc = jnp.exp2(m - m_new) * acc + jnp.dot(p, v_ones, preferred_element_type=F32)
        m = m_new
    o = acc[:, :hd] / acc[:, hd:]
    o_ref[...] = jnp.concatenate([o[g * qb:(g + 1) * qb] for g in range(group)], axis=1).astype(BF16)


def _attention(q, k, v, cache, *, seq_len, qb, chunk, n_heads, n_kv):
    n = q.shape[0]
    hd = HEAD_DIM
    group = n_heads // n_kv
    nq = seq_len // qb
    in_specs = [
        pl.BlockSpec((qb, group * hd), lambda b, h, i: (b * nq + i, h)),
        pl.BlockSpec((seq_len, hd), lambda b, h, i: (b, h)),
        pl.BlockSpec((seq_len, hd), lambda b, h, i: (b, h)),
    ]
    args = [q, k, v]
    if cache is not None:
        past = cache[0].shape[1]
        in_specs += [pl.BlockSpec((1, past, hd), lambda b, h, i: (b, 0, h))] * 2
        args += list(cache)
    return pl.pallas_call(
        functools.partial(_attn_kernel, group=group, chunk=chunk, has_cache=cache is not None),
        grid=(n // seq_len, n_kv, nq),
        in_specs=in_specs,
        out_specs=pl.BlockSpec((qb, group * hd), lambda b, h, i: (b * nq + i, h)),
        out_shape=jax.ShapeDtypeStruct((n, n_heads * hd), BF16),
        compiler_params=_params(3, 48),
        name=f"attention_{seq_len}",
    )(*args)


def _final_norm_kernel(x_ref, acc_ref, pmod_ref, gain_ref, o_ref, *, mod_row):
    n_rows, d = x_ref.shape
    m = mod_row(pl.program_id(0))
    x = x_ref[...] + pmod_ref[pl.ds(m, 1), pl.ds(5 * d, d)] * _load_rows(acc_ref, n_rows, d)
    o_ref[...] = (x * lax.rsqrt(jnp.mean(x * x, axis=-1, keepdims=True) + EPS)) * gain_ref[...]


def _final_norm(x, acc, pmod, gain, mod_row):
    n, d = x.shape
    row = lambda i: (i, 0)
    fixed = lambda i: (0, 0)
    return pl.pallas_call(
        functools.partial(_final_norm_kernel, mod_row=mod_row),
        grid=(n // ROW_BLOCK,),
        in_specs=[pl.BlockSpec((ROW_BLOCK, d), row), _rows_spec(acc, d),
                  pl.BlockSpec(pmod.shape, fixed), pl.BlockSpec((1, d), fixed)],
        out_specs=pl.BlockSpec((ROW_BLOCK, d), row),
        out_shape=jax.ShapeDtypeStruct((n, d), F32),
        compiler_params=_params(1, 32),
        name="final_norm",
    )(x, acc, pmod, gain)


def _rope_head_order():
    q = HEAD_DIM // 4
    return jnp.concatenate([jnp.arange(0, q), jnp.arange(2 * q, 3 * q), jnp.arange(q, 2 * q),
                            jnp.arange(3 * q, 4 * q)])


def _rope_tables(seq_len):
    axis = HEAD_DIM // 2
    t = jnp.arange(seq_len)
    inv = ROPE_BASE ** (-jnp.arange(axis // 2, dtype=F32) * 2.0 / axis)
    ang_row = (t // GRID_W).astype(F32)[:, None] * inv
    ang_col = (t % GRID_W).astype(F32)[:, None] * inv
    ang = jnp.concatenate([ang_row, ang_col], axis=1)
    c, s = jnp.cos(ang), jnp.sin(ang)
    return jnp.concatenate([c, c], axis=1), jnp.concatenate([-s, s], axis=1)


def kernel(x_prompt, x_sample, state_lru, cache_k, cache_v, c, c_ctx, w_mod, b_mod, norm_gain, final_gain,
           w_lru_in, lru_conv_w, lru_conv_b, lru_wa, lru_ba, lru_wx, lru_bx, lru_lambda, w_lru_out,
           w_qkv, q_norm, k_norm, w_attn_out, w_router, w_exp_gate, w_exp_up, w_exp_down):
    batch, seq, d = x_prompt.shape
    dec_batch, dec_seq, _ = x_sample.shape
    depth = w_mod.shape[0]
    n_kv = cache_k.shape[3]
    n_heads = w_attn_out.shape[1] // HEAD_DIM
    heads = dict(n_heads=n_heads, n_kv=n_kv)

    cond = jnp.zeros((SUBLANES, d), F32).at[0].set(c_ctx).at[1:1 + dec_batch].set(c)
    mod_all = _mod_vectors(cond, w_mod, b_mod)

    blocks_per_seq = dec_seq // ROW_BLOCK
    xs = [x_prompt.reshape(batch * seq, d), x_sample.reshape(dec_batch * dec_seq, d)]
    lens = [seq, dec_seq]
    mod_rows = [lambda i: 0, lambda i: 1 + i // blocks_per_seq]
    prevs = [None, None]

    fg = final_gain.reshape(1, d)
    wr_hi, wr_lo = _split_bf16(jnp.swapaxes(w_router, 1, 2))
    new_lru, new_k, new_v = [], [], []
    for l in range(depth):
        mod = mod_all[l]
        gain_a = norm_gain[l, 0].reshape(1, d)
        gain_c = norm_gain[l, 1].reshape(1, d)
        mixed = []
        if l % 2 == 0:
            li = l // 2
            w_in = w_lru_in[li].astype(BF16)
            wg = (0.5 * jnp.concatenate([lru_wa[li, 0], lru_wx[li, 0], lru_wa[li, 1], lru_wx[li, 1]],
                                        axis=2)).astype(BF16)
            bias4 = 0.5 * jnp.stack([lru_ba[li, 0], lru_bx[li, 0], lru_ba[li, 1], lru_bx[li, 1]])
            h0s = [jnp.zeros((batch, 2, w_in.shape[1] // 2), F32), state_lru[:, li]]
            for gi in range(2):
                xs[gi], gate, xr = _lru_in(xs[gi], prevs[gi], mod, gain_a, w_in, mod_rows[gi])
                y, fin = _lru_core(xr, gate, lru_conv_w[li], lru_conv_b[li].reshape(1, -1), wg, bias4,
                                   lru_lambda[li], h0s[gi], seq_len=lens[gi])
                mixed.append(y)
                if gi == 0:
                    new_lru.append(fin.astype(x_prompt.dtype))
            w_out = w_lru_out[li].astype(BF16)
        else:
            ai = l // 2
            w = w_qkv[ai].astype(BF16)
            qg = q_norm[ai].reshape(1, -1)
            kg = k_norm[ai].reshape(1, -1)
            xs[0], q_p, k_p, v_p, kf, vf = _qkv(xs[0], prevs[0], mod, gain_a, w, qg, kg, None,
                                                seq_len=seq, mod_row=mod_rows[0], **heads)
            order = _rope_head_order()
            n_qk = (n_heads + n_kv) * HEAD_DIM
            w_qk = w_qkv[ai][:, :n_qk].reshape(d, n_heads + n_kv, HEAD_DIM)[:, :, order].reshape(d, n_qk)
            w_r = jnp.concatenate([w_qk, w_qkv[ai][:, n_qk:]], axis=1).astype(BF16)
            xs[1], q_s, k_s, v_s = _qkv(xs[1], prevs[1], mod, gain_a, w_r, qg[:, order], kg[:, order],
                                        _rope_tables(dec_seq),
                                        seq_len=dec_seq, mod_row=mod_rows[1], **heads)
            new_k.append(kf.reshape(batch, seq, n_kv, HEAD_DIM))
            new_v.append(vf.reshape(batch, seq, n_kv, HEAD_DIM))
            past = cache_k.shape[2]
            kc = cache_k[:, ai][..., order].reshape(dec_batch, past, n_kv * HEAD_DIM).astype(BF16)
            vc = cache_v[:, ai].reshape(dec_batch, past, n_kv * HEAD_DIM).astype(BF16)
            mixed.append(_attention(q_p, k_p, v_p, None, seq_len=seq, qb=seq, chunk=seq, **heads))
            mixed.append(_attention(q_s, k_s, v_s, (kc, vc), seq_len=dec_seq, qb=256, chunk=256, **heads))
            w_out = w_attn_out[ai].astype(BF16)

        x1_p, h2_p, lg_p = _mix_out(mixed[0], xs[0], mod, gain_c, w_out, wr_hi, wr_lo, l, mod_rows[0], False)
        x1_s, h2_s, lg_s = _mix_out(mixed[1], xs[1], mod, gain_c, w_out, wr_hi, wr_lo, l, mod_rows[1], True)
        idx_p, g_p, idx_s, g_s = _route(lg_p, lg_s, seq, dec_seq)
        xe_p = _gather_onehot(idx_p[:, :, None], h2_p, seq_len=seq)
        off_s = idx_s * _tiles_per_row(d)
        xe_s = _gather_rows(off_s, h2_s, d, seq_len=dec_seq)
        e_n = N_EXPERTS
        gc_p = g_p.reshape(batch, e_n, -1).transpose(1, 0, 2).reshape(e_n, -1, 1)
        gc_s = g_s.reshape(dec_batch, e_n, -1).transpose(1, 0, 2).reshape(e_n, -1, 1)
        ye_p, ye_s = _ffn(xe_p, xe_s, gc_p, gc_s, w_exp_gate, w_exp_up, w_exp_down, l)
        last = l == depth - 1
        x_p = _combine_onehot(idx_p[:, None, :], ye_p, x1_p, mod, fg if last else None, seq_len=seq,
                              mod_row=lambda s: 0)
        acc_s = _combine_rows(off_s, ye_s, seq_len=dec_seq)
        xs = [x_p, x1_s]
        prevs = [None, (acc_s, mod)]

    y_prompt = xs[0].reshape(batch, seq, d)
    y_sample = _final_norm(xs[1], *prevs[1], fg, mod_rows[1]).reshape(dec_batch, dec_seq, d)
    return (y_prompt, y_sample, jnp.stack(new_lru, axis=1), jnp.stack(new_k, axis=1), jnp.stack(new_v, axis=1))
```

```python
import functools
import math

import jax
import jax.numpy as jnp
import numpy as np
from jax import lax
from jax.experimental import pallas as pl
from jax.experimental.pallas import tpu as pltpu

F32 = jnp.float32
BF16 = jnp.bfloat16
I32 = jnp.int32

LANES = 128
SUBLANES = 8
MIB = 1024 * 1024

RG_C = 8.0
CONV_W = 4
CONV_PAD_L = 2
ROPE_BASE = 10000.0
GRID_W = 64
EPS = 1e-6
N_EXPERTS = 16
CAPACITY_FACTOR = 2
HEAD_DIM = 128
LRU_BLOCKS = 8
LRU_STEP_BLOCKS = 2

ROW_BLOCK = 512


def _params(n_axes, vmem_mib):
    return pltpu.CompilerParams(
        dimension_semantics=("arbitrary",) * n_axes, vmem_limit_bytes=vmem_mib * MIB)


def _split_bf16(x):
    hi = x.astype(BF16)
    lo = (x - hi.astype(F32)).astype(BF16)
    return hi, lo


def _modulated_norm(x, gain, shift, scale):
    y = x * lax.rsqrt(jnp.mean(x * x, axis=-1, keepdims=True) + EPS)
    return (y * gain) * (1.0 + scale) + shift


def _tiles_per_row(d):
    return d // LANES


def _load_rows(ref, n_rows, d):
    if ref.shape == (n_rows, d):
        return ref[...]
    tpr = _tiles_per_row(d)
    return jnp.concatenate([ref[pl.ds(s, n_rows, stride=tpr), :] for s in range(tpr)], axis=1)


def _store_rows(ref, x, r0):
    n_rows, d = x.shape
    if ref.shape[1] == d:
        ref[r0:r0 + n_rows, :] = x.astype(ref.dtype)
        return
    tpr = _tiles_per_row(d)
    for s in range(tpr):
        ref[pl.ds(r0 * tpr + s, n_rows, stride=tpr), :] = x[:, s * LANES:(s + 1) * LANES]


def _rows_spec(arr, d):
    if arr.shape[1] == d:
        return pl.BlockSpec((ROW_BLOCK, d), lambda i: (i, 0))
    return pl.BlockSpec((ROW_BLOCK * _tiles_per_row(d), LANES), lambda i: (i, 0))


def _residual_stream(x_ref, prev, m):
    if prev is None:
        return x_ref[...]
    acc_ref, pmod_ref, xo_ref = prev
    n_rows, d = x_ref.shape
    x = x_ref[...] + pmod_ref[pl.ds(m, 1), pl.ds(5 * d, d)] * _load_rows(acc_ref, n_rows, d)
    xo_ref[...] = x
    return x


def _mod_kernel(cond_ref, w_ref, b_ref, o_ref):
    c = cond_ref[...]
    a_hi, a_lo = _split_bf16(c * jax.nn.sigmoid(c))
    w_hi, w_lo = _split_bf16(w_ref[0])
    dot = functools.partial(jnp.dot, preferred_element_type=F32)
    o_ref[0] = dot(a_hi, w_hi) + (dot(a_lo, w_hi) + dot(a_hi, w_lo)) + b_ref[0]


def _mod_vectors(cond8, w_mod, b_mod):
    depth, d, d6 = w_mod.shape
    nb = 1536
    return pl.pallas_call(
        _mod_kernel,
        grid=(depth, d6 // nb),
        in_specs=[
            pl.BlockSpec((SUBLANES, d), lambda l, j: (0, 0)),
            pl.BlockSpec((1, d, nb), lambda l, j: (l, 0, j)),
            pl.BlockSpec((1, 1, nb), lambda l, j: (l, 0, j)),
        ],
        out_specs=pl.BlockSpec((1, SUBLANES, nb), lambda l, j: (l, 0, j)),
        out_shape=jax.ShapeDtypeStruct((depth, SUBLANES, d6), F32),
        compiler_params=_params(2, 40),
        name="adaln_mod",
    )(cond8, w_mod, b_mod.reshape(depth, 1, d6))


def _lru_in_kernel(*refs, mod_row, has_prev):
    if has_prev:
        x_ref, acc_ref, pmod_ref, mod_ref, gain_ref, w_ref, xo_ref, gate_ref, xr_ref = refs
        prev = (acc_ref, pmod_ref, xo_ref)
    else:
        x_ref, mod_ref, gain_ref, w_ref, gate_ref, xr_ref = refs
        prev = None
    d = x_ref.shape[1]
    m = mod_row(pl.program_id(0))
    x = _residual_stream(x_ref, prev, m)
    shift = mod_ref[pl.ds(m, 1), pl.ds(0, d)]
    scale = mod_ref[pl.ds(m, 1), pl.ds(d, d)]
    h = _modulated_norm(x, gain_ref[...], shift, scale).astype(BF16)
    u = jnp.dot(h, w_ref[...], preferred_element_type=F32)
    w = gate_ref.shape[1]
    gate_ref[...] = u[:, :w]
    xr_ref[...] = u[:, w:]


def _lru_in(x, prev, mod, gain, w_in, mod_row):
    n, d = x.shape
    w2 = w_in.shape[1]
    w = w2 // 2
    row = lambda i: (i, 0)
    fixed = lambda i: (0, 0)
    in_specs = [pl.BlockSpec((ROW_BLOCK, d), row)]
    args = [x]
    out_specs = [pl.BlockSpec((ROW_BLOCK, w), row), pl.BlockSpec((ROW_BLOCK, w), row)]
    out_shape = [jax.ShapeDtypeStruct((n, w), F32), jax.ShapeDtypeStruct((n, w), F32)]
    if prev is not None:
        in_specs += [_rows_spec(prev[0], d), pl.BlockSpec(prev[1].shape, fixed)]
        args += list(prev)
        out_specs = [pl.BlockSpec((ROW_BLOCK, d), row)] + out_specs
        out_shape = [jax.ShapeDtypeStruct((n, d), F32)] + out_shape
    in_specs += [pl.BlockSpec(mod.shape, fixed), pl.BlockSpec((1, d), fixed), pl.BlockSpec((d, w2), fixed)]
    args += [mod, gain, w_in]
    outs = pl.pallas_call(
        functools.partial(_lru_in_kernel, mod_row=mod_row, has_prev=prev is not None),
        grid=(n // ROW_BLOCK,),
        in_specs=in_specs,
        out_specs=out_specs,
        out_shape=out_shape,
        compiler_params=_params(1, 48),
        name="lru_in",
    )(*args)
    return outs if prev is not None else [x] + list(outs)


def _lru_core_kernel(xr_ref, gate_ref, cw_ref, cb_ref, wg_ref, bias_ref, lam_ref, h0_ref,
                     y_ref, fin_ref, pad, a_f, b_f, a_b, b_b):
    t_len, lb = xr_ref.shape
    ch = 256
    halo = SUBLANES
    seg = t_len // SUBLANES
    piece = min(ch, seg)
    seg_shift = seg.bit_length() - 1

    def seg_rows(t0):
        r = lax.shift_right_logical(t0, seg_shift)
        return pl.ds((t0 - r * seg) * SUBLANES + r, piece, stride=SUBLANES)

    zero_rows = jnp.zeros((halo, lb), F32)
    pad[0:halo, :] = zero_rows
    pad[t_len + halo:t_len + 2 * halo, :] = zero_rows

    def copy_in(c, carry):
        r = pl.multiple_of(c * ch, ch)
        pad[pl.ds(r + halo, ch), :] = xr_ref[pl.ds(r, ch), :]
        return carry

    lax.fori_loop(0, t_len // ch, copy_in, 0)

    neg_lam = -lam_ref[...]
    softplus = jnp.maximum(neg_lam, 0.0) + jnp.log1p(jnp.exp(-jnp.abs(neg_lam)))
    c_nla = (0.5 * RG_C) * softplus
    c_exp2 = (-0.5 * RG_C * math.log2(math.e)) * softplus
    cw = cw_ref[...]
    cbias = cb_ref[...]
    half_bias = bias_ref[...]
    half_wg = wg_ref[...]
    bw = half_wg.shape[1]
    n_slab = lb // LANES

    def gates(c, carry):
        r = pl.multiple_of(c * ch, ch)
        blk = pad[pl.ds(r, ch + 2 * halo), :]
        xc = cbias
        for k in range(CONV_W):
            o = halo - CONV_PAD_L + k
            xc = xc + blk[o:o + ch] * cw[k:k + 1]
        xc_bf = xc.astype(BF16)
        g = [jnp.dot(xc_bf[:, j * bw:(j + 1) * bw], half_wg[j], preferred_element_type=F32)
             for j in range(lb // bw)]
        pre = lambda k: jnp.concatenate([gj[:, k * bw:(k + 1) * bw] for gj in g], axis=1) + half_bias[k:k + 1]
        half_xc = 0.5 * xc
        for d, (a_ref, b_ref) in enumerate(((a_f, b_f), (a_b, b_b))):
            u = jnp.tanh(pre(2 * d)) + 1.0
            t_i = jnp.tanh(pre(2 * d + 1))
            a = jnp.exp2(u * c_exp2[d:d + 1])
            b = jnp.sqrt(jnp.tanh(u * c_nla[d:d + 1]) * (1.0 + a * a)) * (t_i * half_xc + half_xc)
            for p in range(ch // piece):
                rows = seg_rows(r + p * piece)
                for j in range(n_slab):
                    a_ref[j, rows, :] = a[p * piece:(p + 1) * piece, j * LANES:(j + 1) * LANES]
                    b_ref[j, rows, :] = b[p * piece:(p + 1) * piece, j * LANES:(j + 1) * LANES]
        return carry

    lax.fori_loop(0, t_len // ch, gates, 0)

    def scan(i, carry):
        rows = (pl.ds(pl.multiple_of(i * SUBLANES, SUBLANES), SUBLANES),
                pl.ds(pl.multiple_of((seg - 1 - i) * SUBLANES, SUBLANES), SUBLANES))
        out = []
        for k, (h, prod) in enumerate(carry):
            d, j = divmod(k, n_slab)
            a_ref, b_ref = ((a_f, b_f), (a_b, b_b))[d]
            a = a_ref[j, rows[d], :]
            h = a * h + b_ref[j, rows[d], :]
            prod = a * prod
            b_ref[j, rows[d], :] = h
            a_ref[j, rows[d], :] = prod
            out.append((h, prod))
        return tuple(out)

    start = (jnp.zeros((SUBLANES, LANES), F32), jnp.ones((SUBLANES, LANES), F32))
    ends = lax.fori_loop(0, seg, scan, (start,) * (2 * n_slab), unroll=8)

    enter = []
    for k, (h, prod) in enumerate(ends):
        d, j = divmod(k, n_slab)
        states = [h0_ref[0, d:d + 1, j * LANES:(j + 1) * LANES]]
        for r in (range(SUBLANES) if d == 0 else reversed(range(SUBLANES))):
            states.append(h[r:r + 1] + prod[r:r + 1] * states[-1])
        fin_ref[0, d:d + 1, j * LANES:(j + 1) * LANES] = states[SUBLANES]
        order = states[:SUBLANES] if d == 0 else states[:SUBLANES][::-1]
        enter.append(jnp.concatenate(order, axis=0))

    def fix(i, carry):
        rows = pl.ds(pl.multiple_of(i * SUBLANES, SUBLANES), SUBLANES)
        for j in range(n_slab):
            b_f[j, rows, :] = ((b_f[j, rows, :] + a_f[j, rows, :] * enter[j])
                               + (b_b[j, rows, :] + a_b[j, rows, :] * enter[n_slab + j]))
        return carry

    lax.fori_loop(0, seg, fix, 0, unroll=8)

    def emit(c, carry):
        r = pl.multiple_of(c * ch, ch)
        gt = gate_ref[pl.ds(r, ch), :]
        cdf = 0.5 * (1.0 + jnp.tanh(math.sqrt(2.0 / math.pi) * (gt + 0.044715 * (gt * gt * gt))))
        y = gt * cdf
        for p in range(ch // piece):
            rows = seg_rows(r + p * piece)
            total = jnp.concatenate([b_f[j, rows, :] for j in range(n_slab)], axis=1)
            y_ref[pl.ds(r + p * piece, piece), :] = (total * y[p * piece:(p + 1) * piece]).astype(BF16)
        return carry

    lax.fori_loop(0, t_len // ch, emit, 0)


def _lru_core(xr, gate, conv_w, conv_b, wg, bias4, lam, h0, *, seq_len):
    n, w = xr.shape
    n_seq = n // seq_len
    bw = w // LRU_BLOCKS
    lb = LRU_STEP_BLOCKS * bw
    tok = lambda s, c: (s, c)
    chan = lambda s, c: (0, c)
    return pl.pallas_call(
        _lru_core_kernel,
        grid=(n_seq, LRU_BLOCKS // LRU_STEP_BLOCKS),
        in_specs=[
            pl.BlockSpec((seq_len, lb), tok),
            pl.BlockSpec((seq_len, lb), tok),
            pl.BlockSpec((CONV_W, lb), chan),
            pl.BlockSpec((1, lb), chan),
            pl.BlockSpec((LRU_STEP_BLOCKS, bw, 4 * bw), lambda s, c: (c, 0, 0)),
            pl.BlockSpec((4, lb), chan),
            pl.BlockSpec((2, lb), chan),
            pl.BlockSpec((1, 2, lb), lambda s, c: (s, 0, c)),
        ],
        out_specs=[pl.BlockSpec((seq_len, lb), tok),
                   pl.BlockSpec((1, 2, lb), lambda s, c: (s, 0, c))],
        out_shape=[jax.ShapeDtypeStruct((n, w), BF16), jax.ShapeDtypeStruct((n_seq, 2, w), F32)],
        scratch_shapes=[pltpu.VMEM((seq_len + 2 * SUBLANES, lb), F32)]
        + [pltpu.VMEM((lb // LANES, seq_len, LANES), F32)] * 4,
        compiler_params=_params(2, 48),
        name=f"lru_core_{seq_len}",
    )(xr, gate, conv_w, conv_b, wg, bias4, lam, h0)


def _mix_out_kernel(a_ref, x_ref, mod_ref, gain_ref, w_ref, wrh_ref, wrl_ref,
                    x1_ref, h2_ref, lg_ref, *, mod_row):
    d = x_ref.shape[1]
    m = mod_row(pl.program_id(0))
    g_mix = mod_ref[pl.ds(m, 1), pl.ds(2 * d, d)]
    shift = mod_ref[pl.ds(m, 1), pl.ds(3 * d, d)]
    scale = mod_ref[pl.ds(m, 1), pl.ds(4 * d, d)]
    op = jnp.dot(a_ref[...], w_ref[...], preferred_element_type=F32)
    x1 = x_ref[...] + g_mix * op
    x1_ref[...] = x1
    h2 = _modulated_norm(x1, gain_ref[...], shift, scale)
    h_hi, h_lo = _split_bf16(h2)
    _store_rows(h2_ref, h2, 0)
    nt = functools.partial(lax.dot_general, dimension_numbers=(((1,), (1,)), ((), ())),
                           preferred_element_type=F32)
    wr_hi = wrh_ref[0]
    lg_ref[...] = nt(wr_hi, h_hi) + (nt(wrl_ref[0], h_hi) + nt(wr_hi, h_lo))


def _mix_out(a, x, mod, gain, w_out, wr_hi, wr_lo, layer, mod_row, token_tiled):
    n, d = x.shape
    e = wr_hi.shape[1]
    row = lambda i: (i, 0)
    fixed = lambda i: (0, 0)
    router = pl.BlockSpec((1,) + wr_hi.shape[1:], lambda i: (layer, 0, 0))
    tpr = _tiles_per_row(d)
    if token_tiled:
        h2_spec = pl.BlockSpec((ROW_BLOCK * tpr, LANES), row)
        h2_shape = jax.ShapeDtypeStruct((n * tpr, LANES), F32)
    else:
        h2_spec = pl.BlockSpec((ROW_BLOCK, d), row)
        h2_shape = jax.ShapeDtypeStruct((n, d), BF16)
    return pl.pallas_call(
        functools.partial(_mix_out_kernel, mod_row=mod_row),
        grid=(n // ROW_BLOCK,),
        in_specs=[
            pl.BlockSpec((ROW_BLOCK, a.shape[1]), row),
            pl.BlockSpec((ROW_BLOCK, d), row),
            pl.BlockSpec(mod.shape, fixed),
            pl.BlockSpec((1, d), fixed),
            pl.BlockSpec(w_out.shape, fixed),
            router,
            router,
        ],
        out_specs=[pl.BlockSpec((ROW_BLOCK, d), row), h2_spec,
                   pl.BlockSpec((e, ROW_BLOCK), lambda i: (0, i))],
        out_shape=[jax.ShapeDtypeStruct((n, d), F32), h2_shape, jax.ShapeDtypeStruct((e, n), F32)],
        compiler_params=_params(1, 48),
        name="mix_out",
    )(a, x, mod, gain, w_out, wr_hi, wr_lo)


def _lane_sum(tiles):
    acc = tiles[0]
    for t in tiles[1:]:
        acc = acc + t
    return jnp.sum(acc, axis=1, keepdims=True)


def _exclusive_cumsum(flags, upper):
    out = []
    off = jnp.zeros((flags[0].shape[0], 1), F32)
    for f in flags:
        out.append(jnp.dot(f.astype(BF16), upper, preferred_element_type=F32) + off)
        off = off + jnp.sum(f, axis=1, keepdims=True)
    return out


def _route_group(lg_ref, n_seq, t_len, idx_ref, g_ref):
    cap = CAPACITY_FACTOR * t_len // N_EXPERTS
    affs = []
    for s in range(n_seq):
        lg = lg_ref[:, s * t_len:(s + 1) * t_len]
        ex = jnp.exp(lg - jnp.max(lg, axis=0, keepdims=True))
        affs.append(ex / jnp.sum(ex, axis=0, keepdims=True))
    aff = jnp.concatenate(affs, axis=0)
    n_rows = aff.shape[0]
    nt = t_len // LANES
    g = [aff[:, j * LANES:(j + 1) * LANES] for j in range(nt)]

    kth_bits = jnp.zeros((n_rows, 1), I32)
    for bit in range(30, -1, -1):
        cand = kth_bits | (1 << bit)
        cand_f = pltpu.bitcast(cand, F32)
        cnt = _lane_sum([jnp.where(t >= cand_f, 1, 0) for t in g])
        kth_bits = jnp.where(cnt >= cap, cand, kth_bits)
    kth = pltpu.bitcast(kth_bits, F32)

    lane = lax.broadcasted_iota(I32, (LANES, LANES), 0)
    upper = jnp.where(lane < lax.broadcasted_iota(I32, (LANES, LANES), 1), 1.0, 0.0).astype(BF16)
    gt = [t > kth for t in g]
    eq = [t == kth for t in g]
    need = (cap - _lane_sum([jnp.where(m, 1, 0) for m in gt])).astype(F32)
    eq_rank = _exclusive_cumsum([jnp.where(m, 1.0, 0.0) for m in eq], upper)
    sel = [jnp.logical_or(gt[j], jnp.logical_and(eq[j], eq_rank[j] < need)) for j in range(nt)]
    pos = _exclusive_cumsum([jnp.where(m, 1.0, 0.0) for m in sel], upper)

    lane_r = lax.broadcasted_iota(I32, (n_rows, LANES), 1)
    d = [jnp.where(sel[j], lane_r + j * LANES - pos[j].astype(I32), -1) for j in range(nt)]
    for k in range(t_len.bit_length() - 1):
        s = 1 << k
        if s < LANES:
            d_rot = [pltpu.roll(x, LANES - s, 1) for x in d]
            g_rot = [pltpu.roll(x, LANES - s, 1) for x in g]
            same = lane_r < LANES - s
            d_in = [jnp.where(same, d_rot[j], d_rot[(j + 1) % nt]) for j in range(nt)]
            g_in = [jnp.where(same, g_rot[j], g_rot[(j + 1) % nt]) for j in range(nt)]
        else:
            q = s // LANES
            d_in = [d[(j + q) % nt] for j in range(nt)]
            g_in = [g[(j + q) % nt] for j in range(nt)]
        new_d, new_g = [], []
        for j in range(nt):
            move = jnp.logical_and(d_in[j] >= 0, ((d_in[j] >> k) & 1) == 1)
            stay = jnp.logical_and(d[j] >= 0, ((d[j] >> k) & 1) == 0)
            new_d.append(jnp.where(move, d_in[j], jnp.where(stay, d[j], -1)))
            new_g.append(jnp.where(move, g_in[j], g[j]))
        d, g = new_d, new_g

    for j in range(idx_ref.shape[1] // LANES):
        idx_ref[:, j * LANES:(j + 1) * LANES] = lane_r + j * LANES + d[j]
        g_ref[:, j * LANES:(j + 1) * LANES] = g[j]


def _route_kernel(lgp_ref, lgs_ref, idx_p_ref, g_p_ref, idx_s_ref, g_s_ref, *, p_len, s_len):
    _route_group(lgp_ref, lgp_ref.shape[1] // p_len, p_len, idx_p_ref, g_p_ref)
    _route_group(lgs_ref, lgs_ref.shape[1] // s_len, s_len, idx_s_ref, g_s_ref)


def _route(lg_p, lg_s, p_len, s_len):
    e = lg_p.shape[0]
    np_seq = lg_p.shape[1] // p_len
    ns_seq = lg_s.shape[1] // s_len
    cap_p = CAPACITY_FACTOR * p_len // N_EXPERTS
    cap_s = CAPACITY_FACTOR * s_len // N_EXPERTS
    wp = max(cap_p, LANES)
    ws = max(cap_s, LANES)
    idx_p, g_p, idx_s, g_s = pl.pallas_call(
        functools.partial(_route_kernel, p_len=p_len, s_len=s_len),
        out_shape=[jax.ShapeDtypeStruct((np_seq * e, wp), I32),
                   jax.ShapeDtypeStruct((np_seq * e, wp), F32),
                   jax.ShapeDtypeStruct((ns_seq * e, ws), I32),
                   jax.ShapeDtypeStruct((ns_seq * e, ws), F32)],
        compiler_params=pltpu.CompilerParams(vmem_limit_bytes=40 * MIB),
        name="route",
    )(lg_p, lg_s)
    return (idx_p[:, :cap_p].reshape(np_seq, e * cap_p), g_p[:, :cap_p].reshape(np_seq, e * cap_p),
            idx_s[:, :cap_s].reshape(-1), g_s[:, :cap_s].reshape(-1))


def _gather_onehot_kernel(idx_ref, h_ref, o_ref):
    t_len = h_ref.shape[0]
    n_exp, cap, _ = o_ref.shape
    idx = idx_ref[0]
    hit = jnp.where(idx == lax.broadcasted_iota(I32, (t_len, idx.shape[1]), 0), 1.0, 0.0).astype(BF16)
    xs = lax.dot_general(hit, h_ref[...], (((0,), (0,)), ((), ())), preferred_element_type=F32).astype(BF16)
    for e in range(n_exp):
        o_ref[e] = xs[e * cap:(e + 1) * cap]


def _gather_onehot(idx_row, h, *, seq_len):
    n, d = h.shape
    n_seq, _, slots = idx_row.shape
    cap = slots // N_EXPERTS
    return pl.pallas_call(
        _gather_onehot_kernel,
        grid=(n_seq,),
        in_specs=[pl.BlockSpec((1, 1, slots), lambda s: (s, 0, 0)),
                  pl.BlockSpec((seq_len, d), lambda s: (s, 0))],
        out_specs=pl.BlockSpec((N_EXPERTS, cap, d), lambda s: (0, s, 0)),
        out_shape=jax.ShapeDtypeStruct((N_EXPERTS, n_seq * cap, d), BF16),
        compiler_params=_params(1, 32),
        name="gather_onehot",
    )(idx_row, h)


def _gather_rows_kernel(idx_ref, h_ref, o_ref, tiles, *, cap, n_exp):
    tpr = _tiles_per_row(o_ref.shape[2])
    base = (pl.program_id(0) * n_exp + pl.program_id(1)) * cap

    def body(c, carry):
        src = pl.multiple_of(idx_ref[base + c], tpr)
        tiles[pl.ds(pl.multiple_of(c * tpr, tpr), tpr), :] = h_ref[pl.ds(src, tpr), :]
        return carry

    lax.fori_loop(0, cap, body, 0, unroll=8)
    for s in range(tpr):
        o_ref[0, :, s * LANES:(s + 1) * LANES] = tiles[pl.ds(s, cap, stride=tpr), :].astype(BF16)


def _gather_rows(idx, h, d, *, seq_len):
    tpr = _tiles_per_row(d)
    n_seq = h.shape[0] // (seq_len * tpr)
    cap = CAPACITY_FACTOR * seq_len // N_EXPERTS
    return pl.pallas_call(
        functools.partial(_gather_rows_kernel, cap=cap, n_exp=N_EXPERTS),
        grid_spec=pltpu.PrefetchScalarGridSpec(
            num_scalar_prefetch=1,
            grid=(n_seq, N_EXPERTS),
            in_specs=[pl.BlockSpec((seq_len * tpr, LANES), lambda s, e, idx: (s, 0))],
            out_specs=pl.BlockSpec((1, cap, d), lambda s, e, idx: (e, s, 0)),
            scratch_shapes=[pltpu.VMEM((cap * tpr, LANES), F32)],
        ),
        out_shape=jax.ShapeDtypeStruct((N_EXPERTS, n_seq * cap, d), BF16),
        compiler_params=_params(2, 48),
        name="gather_rows",
    )(idx, h)


def _ffn_kernel(xp_ref, xs_ref, gp_ref, gs_ref, wg_ref, wu_ref, wd_ref, yp_ref, ys_ref):
    rc = 512

    @pl.when(pl.program_id(1) == 0)
    def _():
        for y_ref in (yp_ref, ys_ref):
            for r in range(0, y_ref.shape[1], rc):
                y_ref[0, r:r + rc, :] = jnp.zeros((rc, y_ref.shape[2]), F32)

    wg = wg_ref[0, 0].astype(BF16)
    wu = wu_ref[0, 0].astype(BF16)
    wd = wd_ref[0, 0].astype(BF16)
    last = pl.program_id(1) == pl.num_programs(1) - 1
    for x_ref, g_ref, y_ref in ((xp_ref, gp_ref, yp_ref), (xs_ref, gs_ref, ys_ref)):
        for r in range(0, x_ref.shape[1], rc):
            x = x_ref[0, r:r + rc, :]
            hg = jnp.dot(x, wg, preferred_element_type=F32)
            hu = jnp.dot(x, wu, preferred_element_type=F32)
            hid = ((hg * jax.nn.sigmoid(hg)) * hu).astype(BF16)
            g_hi, g_lo = _split_bf16(g_ref[0, :, r:r + rc])
            ones = jnp.ones((g_hi.shape[0], LANES), BF16)
            tn = functools.partial(lax.dot_general, dimension_numbers=(((0,), (0,)), ((), ())),
                                   preferred_element_type=F32)
            weight = jnp.where(last, tn(g_hi, ones) + tn(g_lo, ones), 1.0)
            weight = jnp.concatenate([weight] * (y_ref.shape[2] // LANES), axis=1)
            y_ref[0, r:r + rc, :] = (y_ref[0, r:r + rc, :] + jnp.dot(hid, wd, preferred_element_type=F32)) * weight


def _ffn(xs_p, xs_s, g_p, g_s, w_gate, w_up, w_down, layer):
    n_exp, rp, d = xs_p.shape
    rs = xs_s.shape[1]
    ff = w_gate.shape[3]
    fc = 512
    return pl.pallas_call(
        _ffn_kernel,
        grid=(n_exp, ff // fc),
        in_specs=[
            pl.BlockSpec((1, rp, d), lambda e, f: (e, 0, 0)),
            pl.BlockSpec((1, rs, d), lambda e, f: (e, 0, 0)),
            pl.BlockSpec((1, SUBLANES, rp), lambda e, f: (e, 0, 0)),
            pl.BlockSpec((1, SUBLANES, rs), lambda e, f: (e, 0, 0)),
            pl.BlockSpec((1, 1, d, fc), lambda e, f: (layer, e, 0, f)),
            pl.BlockSpec((1, 1, d, fc), lambda e, f: (layer, e, 0, f)),
            pl.BlockSpec((1, 1, fc, d), lambda e, f: (layer, e, f, 0)),
        ],
        out_specs=[pl.BlockSpec((1, rp, d), lambda e, f: (e, 0, 0)),
                   pl.BlockSpec((1, rs, d), lambda e, f: (e, 0, 0))],
        out_shape=[jax.ShapeDtypeStruct((n_exp, rp, d), F32),
                   jax.ShapeDtypeStruct((n_exp, rs, d), F32)],
        compiler_params=_params(2, 56),
        name="expert_ffn",
    )(xs_p, xs_s, g_p, g_s, w_gate, w_up, w_down)


def _combine_onehot_kernel(*refs, mod_row, final):
    if final:
        idx_ref, ye_ref, x_ref, pmod_ref, gain_ref, o_ref = refs
    else:
        idx_ref, ye_ref, x_ref, pmod_ref, o_ref = refs
    t_len, d = o_ref.shape
    n_exp = ye_ref.shape[0]
    idx = idx_ref[0]
    hit = jnp.where(idx == lax.broadcasted_iota(I32, (t_len, idx.shape[1]), 0), 1.0, 0.0).astype(BF16)
    y_hi, y_lo = _split_bf16(jnp.concatenate([ye_ref[e] for e in range(n_exp)], axis=0))
    acc = jnp.dot(hit, y_hi, preferred_element_type=F32) + jnp.dot(hit, y_lo, preferred_element_type=F32)
    x = x_ref[...] + pmod_ref[pl.ds(mod_row(pl.program_id(0)), 1), pl.ds(5 * d, d)] * acc
    if final:
        x = (x * lax.rsqrt(jnp.mean(x * x, axis=-1, keepdims=True) + EPS)) * gain_ref[...]
    o_ref[...] = x


def _combine_onehot(idx_row, ye, x, pmod, final_gain, *, seq_len, mod_row):
    n_exp, rows, d = ye.shape
    n_seq, _, slots = idx_row.shape
    cap = slots // n_exp
    in_specs = [pl.BlockSpec((1, 1, slots), lambda s: (s, 0, 0)),
                pl.BlockSpec((n_exp, cap, d), lambda s: (0, s, 0)),
                pl.BlockSpec((seq_len, d), lambda s: (s, 0)),
                pl.BlockSpec(pmod.shape, lambda s: (0, 0))]
    args = [idx_row, ye, x, pmod]
    if final_gain is not None:
        in_specs.append(pl.BlockSpec((1, d), lambda s: (0, 0)))
        args.append(final_gain)
    return pl.pallas_call(
        functools.partial(_combine_onehot_kernel, mod_row=mod_row, final=final_gain is not None),
        grid=(n_seq,),
        in_specs=in_specs,
        out_specs=pl.BlockSpec((seq_len, d), lambda s: (s, 0)),
        out_shape=jax.ShapeDtypeStruct((n_seq * seq_len, d), F32),
        compiler_params=_params(1, 32),
        name="combine_onehot",
    )(*args)


COMBINE_GROUP = 16


def _combine_rows_kernel(idx_ref, ye_ref, o_ref, tiles, *, cap, n_exp):
    e = pl.program_id(1)
    d = ye_ref.shape[2]
    tpr = _tiles_per_row(d)
    ch = 2048

    @pl.when(e == 0)
    def _():
        def zero(c, carry):
            o_ref[pl.ds(pl.multiple_of(c * ch, ch), ch), :] = jnp.zeros((ch, LANES), F32)
            return carry
        lax.fori_loop(0, o_ref.shape[0] // ch, zero, 0)

    for s in range(tpr):
        tiles[pl.ds(s, cap, stride=tpr), :] = ye_ref[0, :, s * LANES:(s + 1) * LANES]

    base = (pl.program_id(0) * n_exp + e) * cap

    def body(i, carry):
        c0 = i * COMBINE_GROUP
        dst = [pl.ds(pl.multiple_of(idx_ref[base + c0 + k], tpr), tpr) for k in range(COMBINE_GROUP)]
        vals = [o_ref[dst[k], :] + tiles[pl.ds(pl.multiple_of((c0 + k) * tpr, tpr), tpr), :]
                for k in range(COMBINE_GROUP)]
        for k in range(COMBINE_GROUP):
            o_ref[dst[k], :] = vals[k]
        return carry

    lax.fori_loop(0, cap // COMBINE_GROUP, body, 0)


def _combine_rows(idx, ye, *, seq_len):
    n_exp, rows, d = ye.shape
    tpr = _tiles_per_row(d)
    cap = CAPACITY_FACTOR * seq_len // N_EXPERTS
    n_seq = rows // cap
    return pl.pallas_call(
        functools.partial(_combine_rows_kernel, cap=cap, n_exp=n_exp),
        grid_spec=pltpu.PrefetchScalarGridSpec(
            num_scalar_prefetch=1,
            grid=(n_seq, n_exp),
            in_specs=[pl.BlockSpec((1, cap, d), lambda s, e, i: (e, s, 0))],
            out_specs=pl.BlockSpec((seq_len * tpr, LANES), lambda s, e, i: (s, 0)),
            scratch_shapes=[pltpu.VMEM((cap * tpr, LANES), F32)],
        ),
        out_shape=jax.ShapeDtypeStruct((n_seq * seq_len * tpr, LANES), F32),
        compiler_params=_params(2, 48),
        name="combine_rows",
    )(idx, ye)


def _head_norm(x, gain):
    return x * lax.rsqrt(jnp.mean(x * x, axis=-1, keepdims=True) + EPS) * gain


def _qkv_kernel(*refs, rope, has_prev, mod_row, n_heads, n_kv):
    refs = list(refs)
    x_ref = refs.pop(0)
    prev = None
    if has_prev:
        acc_ref, pmod_ref = refs.pop(0), refs.pop(0)
    mod_ref, gain_ref, w_ref, qg_ref, kg_ref = refs[:5]
    refs = refs[5:]
    if rope:
        cos_ref, sin_ref = refs.pop(0), refs.pop(0)
    if has_prev:
        prev = (acc_ref, pmod_ref, refs.pop(0))
    q_ref, k_ref, v_ref = refs[:3]
    d = x_ref.shape[1]
    hd = HEAD_DIM
    m = mod_row(pl.program_id(0))
    x = _residual_stream(x_ref, prev, m)
    shift = mod_ref[pl.ds(m, 1), pl.ds(0, d)]
    scale = mod_ref[pl.ds(m, 1), pl.ds(d, d)]
    h = _modulated_norm(x, gain_ref[...], shift, scale).astype(BF16)
    qg = qg_ref[...]
    kg = kg_ref[...]
    q_scale = math.log2(math.e) * hd ** -0.5
    half = x_ref.shape[0] // 2
    for r0 in (0, half):
        rows = slice(r0, r0 + half)
        qkv = jnp.dot(h[rows], w_ref[...], preferred_element_type=F32)
        if rope:
            cos = cos_ref[rows, :]
            sin = sin_ref[rows, :]

            def rot(xh, cos=cos, sin=sin):
                return xh * cos + pltpu.roll(xh, hd // 2, 1) * sin
        else:
            rot = lambda xh: xh

        for i in range(n_heads):
            qh = rot(_head_norm(qkv[:, i * hd:(i + 1) * hd], qg)) * q_scale
            q_ref[rows, i * hd:(i + 1) * hd] = qh.astype(BF16)
        for i in range(n_kv):
            c0 = (n_heads + i) * hd
            kh = _head_norm(qkv[:, c0:c0 + hd], kg)
            if not rope:
                refs[3][rows, i * hd:(i + 1) * hd] = kh
            k_ref[rows, i * hd:(i + 1) * hd] = rot(kh).astype(BF16)
        v = qkv[:, (n_heads + n_kv) * hd:]
        v_ref[rows, :] = v.astype(BF16)
        if not rope:
            refs[4][rows, :] = v


def _qkv(x, prev, mod, gain, w_qkv, q_gain, k_gain, tables, *, seq_len, mod_row, n_heads, n_kv):
    n, d = x.shape
    hd = HEAD_DIM
    row = lambda i: (i, 0)
    fixed = lambda i: (0, 0)
    rope = tables is not None
    in_specs = [pl.BlockSpec((ROW_BLOCK, d), row)]
    args = [x]
    if prev is not None:
        in_specs += [_rows_spec(prev[0], d), pl.BlockSpec(prev[1].shape, fixed)]
        args += list(prev)
    in_specs += [
        pl.BlockSpec(mod.shape, fixed),
        pl.BlockSpec((1, d), fixed),
        pl.BlockSpec(w_qkv.shape, fixed),
        pl.BlockSpec((1, hd), fixed),
        pl.BlockSpec((1, hd), fixed),
    ]
    args += [mod, gain, w_qkv, q_gain, k_gain]
    if rope:
        per_seq = seq_len // ROW_BLOCK
        in_specs += [pl.BlockSpec((ROW_BLOCK, hd), lambda i: (i % per_seq, 0))] * 2
        args += list(tables)
    out_specs, out_shape = [], []
    if prev is not None:
        out_specs.append(pl.BlockSpec((ROW_BLOCK, d), row))
        out_shape.append(jax.ShapeDtypeStruct((n, d), F32))
    out_specs += [pl.BlockSpec((ROW_BLOCK, n_heads * hd), row),
                  pl.BlockSpec((ROW_BLOCK, n_kv * hd), row),
                  pl.BlockSpec((ROW_BLOCK, n_kv * hd), row)]
    out_shape += [jax.ShapeDtypeStruct((n, n_heads * hd), BF16),
                  jax.ShapeDtypeStruct((n, n_kv * hd), BF16),
                  jax.ShapeDtypeStruct((n, n_kv * hd), BF16)]
    if not rope:
        out_specs += [pl.BlockSpec((ROW_BLOCK, n_kv * hd), row)] * 2
        out_shape += [jax.ShapeDtypeStruct((n, n_kv * hd), F32)] * 2
    outs = pl.pallas_call(
        functools.partial(_qkv_kernel, rope=rope, has_prev=prev is not None, mod_row=mod_row,
                          n_heads=n_heads, n_kv=n_kv),
        grid=(n // ROW_BLOCK,),
        in_specs=in_specs,
        out_specs=out_specs,
        out_shape=out_shape,
        compiler_params=_params(1, 48),
        name="qkv_rope" if rope else "qkv",
    )(*args)
    return list(outs) if prev is not None else [x] + list(outs)


def _attn_kernel(*refs, group, chunk, has_cache):
    if has_cache:
        q_ref, k_ref, v_ref, kc_ref, vc_ref, o_ref = refs
    else:
        q_ref, k_ref, v_ref, o_ref = refs
    hd = HEAD_DIM
    qb = q_ref.shape[0]
    rows = group * qb
    q = jnp.concatenate([q_ref[:, g * hd:(g + 1) * hd] for g in range(group)], axis=0)
    sources = [(k_ref, v_ref, s0, chunk) for s0 in range(0, k_ref.shape[0], chunk)]
    if has_cache:
        sources.append((kc_ref.at[0], vc_ref.at[0], 0, kc_ref.shape[1]))
    m = jnp.full((rows, 1), -jnp.inf, F32)
    acc = jnp.zeros((rows, 2 * hd), F32)
    for kr, vr, s0, size in sources:
        s = lax.dot_general(q, kr[s0:s0 + size, :], (((1,), (1,)), ((), ())),
                            preferred_element_type=F32)
        m_new = jnp.maximum(m, jnp.max(s, axis=-1, keepdims=True))
        p = jnp.exp2(s - m_new).astype(BF16)
        v_ones = jnp.concatenate([vr[s0:s0 + size, :], jnp.ones((size, hd), BF16)], axis=1)
        acc = jnp.exp2(m - m_new) * acc + jnp.dot(p, v_ones, preferred_element_type=F32)
        m = m_new
    o = acc[:, :hd] / acc[:, hd:]
    o_ref[...] = jnp.concatenate([o[g * qb:(g + 1) * qb] for g in range(group)], axis=1).astype(BF16)


def _attention(q, k, v, cache, *, seq_len, qb, chunk, n_heads, n_kv):
    n = q.shape[0]
    hd = HEAD_DIM
    group = n_heads // n_kv
    nq = seq_len // qb
    in_specs = [
        pl.BlockSpec((qb, group * hd), lambda b, h, i: (b * nq + i, h)),
        pl.BlockSpec((seq_len, hd), lambda b, h, i: (b, h)),
        pl.BlockSpec((seq_len, hd), lambda b, h, i: (b, h)),
    ]
    args = [q, k, v]
    if cache is not None:
        past = cache[0].shape[1]
        in_specs += [pl.BlockSpec((1, past, hd), lambda b, h, i: (b, 0, h))] * 2
        args += list(cache)
    return pl.pallas_call(
        functools.partial(_attn_kernel, group=group, chunk=chunk, has_cache=cache is not None),
        grid=(n // seq_len, n_kv, nq),
        in_specs=in_specs,
        out_specs=pl.BlockSpec((qb, group * hd), lambda b, h, i: (b * nq + i, h)),
        out_shape=jax.ShapeDtypeStruct((n, n_heads * hd), BF16),
        compiler_params=_params(3, 48),
        name=f"attention_{seq_len}",
    )(*args)


def _final_norm_kernel(x_ref, acc_ref, pmod_ref, gain_ref, o_ref, *, mod_row):
    n_rows, d = x_ref.shape
    m = mod_row(pl.program_id(0))
    x = x_ref[...] + pmod_ref[pl.ds(m, 1), pl.ds(5 * d, d)] * _load_rows(acc_ref, n_rows, d)
    o_ref[...] = (x * lax.rsqrt(jnp.mean(x * x, axis=-1, keepdims=True) + EPS)) * gain_ref[...]


def _final_norm(x, acc, pmod, gain, mod_row):
    n, d = x.shape
    row = lambda i: (i, 0)
    fixed = lambda i: (0, 0)
    return pl.pallas_call(
        functools.partial(_final_norm_kernel, mod_row=mod_row),
        grid=(n // ROW_BLOCK,),
        in_specs=[pl.BlockSpec((ROW_BLOCK, d), row), _rows_spec(acc, d),
                  pl.BlockSpec(pmod.shape, fixed), pl.BlockSpec((1, d), fixed)],
        out_specs=pl.BlockSpec((ROW_BLOCK, d), row),
        out_shape=jax.ShapeDtypeStruct((n, d), F32),
        compiler_params=_params(1, 32),
        name="final_norm",
    )(x, acc, pmod, gain)


def _rope_head_order(x):
    q = HEAD_DIM // 4
    row1, row2, col1, col2 = (x[..., i * q:(i + 1) * q] for i in range(4))
    return jnp.concatenate([row1, col1, row2, col2], axis=-1)


def _rope_tables(seq_len):
    axis = HEAD_DIM // 2
    t = np.arange(seq_len)
    inv = ROPE_BASE ** (-np.arange(axis // 2, dtype=np.float64) * 2.0 / axis)
    ang = np.concatenate([(t // GRID_W)[:, None] * inv, (t % GRID_W)[:, None] * inv], axis=1)
    c, s = np.cos(ang), np.sin(ang)
    return (jnp.asarray(np.concatenate([c, c], axis=1), F32), jnp.asarray(np.concatenate([-s, s], axis=1), F32))


def kernel(x_prompt, x_sample, state_lru, cache_k, cache_v, c, c_ctx, w_mod, b_mod, norm_gain, final_gain,
           w_lru_in, lru_conv_w, lru_conv_b, lru_wa, lru_ba, lru_wx, lru_bx, lru_lambda, w_lru_out,
           w_qkv, q_norm, k_norm, w_attn_out, w_router, w_exp_gate, w_exp_up, w_exp_down):
    batch, seq, d = x_prompt.shape
    dec_batch, dec_seq, _ = x_sample.shape
    depth = w_mod.shape[0]
    n_kv = cache_k.shape[3]
    n_heads = w_attn_out.shape[1] // HEAD_DIM
    heads = dict(n_heads=n_heads, n_kv=n_kv)

    cond = jnp.zeros((SUBLANES, d), F32).at[0].set(c_ctx).at[1:1 + dec_batch].set(c)
    mod_all = _mod_vectors(cond, w_mod, b_mod)

    blocks_per_seq = dec_seq // ROW_BLOCK
    xs = [x_prompt.reshape(batch * seq, d), x_sample.reshape(dec_batch * dec_seq, d)]
    lens = [seq, dec_seq]
    mod_rows = [lambda i: 0, lambda i: 1 + i // blocks_per_seq]
    prevs = [None, None]

    fg = final_gain.reshape(1, d)
    wr_hi, wr_lo = _split_bf16(jnp.swapaxes(w_router, 1, 2))
    new_lru, new_k, new_v = [], [], []
    for l in range(depth):
        mod = mod_all[l]
        gain_a = norm_gain[l, 0].reshape(1, d)
        gain_c = norm_gain[l, 1].reshape(1, d)
        mixed = []
        if l % 2 == 0:
            li = l // 2
            w_in = w_lru_in[li].astype(BF16)
            wg = (0.5 * jnp.concatenate([lru_wa[li, 0], lru_wx[li, 0], lru_wa[li, 1], lru_wx[li, 1]],
                                        axis=2)).astype(BF16)
            bias4 = 0.5 * jnp.stack([lru_ba[li, 0], lru_bx[li, 0], lru_ba[li, 1], lru_bx[li, 1]])
            h0s = [jnp.zeros((batch, 2, w_in.shape[1] // 2), F32), state_lru[:, li]]
            for gi in range(2):
                xs[gi], gate, xr = _lru_in(xs[gi], prevs[gi], mod, gain_a, w_in, mod_rows[gi])
                y, fin = _lru_core(xr, gate, lru_conv_w[li], lru_conv_b[li].reshape(1, -1), wg, bias4,
                                   lru_lambda[li], h0s[gi], seq_len=lens[gi])
                mixed.append(y)
                if gi == 0:
                    new_lru.append(fin.astype(x_prompt.dtype))
            w_out = w_lru_out[li].astype(BF16)
        else:
            ai = l // 2
            w = w_qkv[ai].astype(BF16)
            qg = q_norm[ai].reshape(1, -1)
            kg = k_norm[ai].reshape(1, -1)
            xs[0], q_p, k_p, v_p, kf, vf = _qkv(xs[0], prevs[0], mod, gain_a, w, qg, kg, None,
                                                seq_len=seq, mod_row=mod_rows[0], **heads)
            n_qk = (n_heads + n_kv) * HEAD_DIM
            w_qk = _rope_head_order(w_qkv[ai][:, :n_qk].reshape(d, n_heads + n_kv, HEAD_DIM)).reshape(d, n_qk)
            w_r = jnp.concatenate([w_qk, w_qkv[ai][:, n_qk:]], axis=1).astype(BF16)
            xs[1], q_s, k_s, v_s = _qkv(xs[1], prevs[1], mod, gain_a, w_r, _rope_head_order(qg),
                                        _rope_head_order(kg), _rope_tables(dec_seq),
                                        seq_len=dec_seq, mod_row=mod_rows[1], **heads)
            new_k.append(kf.reshape(batch, seq, n_kv, HEAD_DIM))
            new_v.append(vf.reshape(batch, seq, n_kv, HEAD_DIM))
            past = cache_k.shape[2]
            kc = _rope_head_order(cache_k[:, ai]).reshape(dec_batch, past, n_kv * HEAD_DIM).astype(BF16)
            vc = cache_v[:, ai].reshape(dec_batch, past, n_kv * HEAD_DIM).astype(BF16)
            mixed.append(_attention(q_p, k_p, v_p, None, seq_len=seq, qb=seq, chunk=seq, **heads))
            mixed.append(_attention(q_s, k_s, v_s, (kc, vc), seq_len=dec_seq, qb=256, chunk=256, **heads))
            w_out = w_attn_out[ai].astype(BF16)

        x1_p, h2_p, lg_p = _mix_out(mixed[0], xs[0], mod, gain_c, w_out, wr_hi, wr_lo, l, mod_rows[0], False)
        x1_s, h2_s, lg_s = _mix_out(mixed[1], xs[1], mod, gain_c, w_out, wr_hi, wr_lo, l, mod_rows[1], True)
        idx_p, g_p, idx_s, g_s = _route(lg_p, lg_s, seq, dec_seq)
        xe_p = _gather_onehot(idx_p[:, None, :], h2_p, seq_len=seq)
        off_s = idx_s * _tiles_per_row(d)
        xe_s = _gather_rows(off_s, h2_s, d, seq_len=dec_seq)
        e_n = N_EXPERTS
        zero_rows = ((0, 0), (0, SUBLANES - 1), (0, 0))
        gc_p = jnp.pad(g_p.reshape(batch, e_n, -1).transpose(1, 0, 2).reshape(e_n, 1, -1), zero_rows)
        gc_s = jnp.pad(g_s.reshape(dec_batch, e_n, -1).transpose(1, 0, 2).reshape(e_n, 1, -1), zero_rows)
        ye_p, ye_s = _ffn(xe_p, xe_s, gc_p, gc_s, w_exp_gate, w_exp_up, w_exp_down, l)
        last = l == depth - 1
        x_p = _combine_onehot(idx_p[:, None, :], ye_p, x1_p, mod, fg if last else None, seq_len=seq,
                              mod_row=lambda s: 0)
        acc_s = _combine_rows(off_s, ye_s, seq_len=dec_seq)
        xs = [x_p, x1_s]
        prevs = [None, (acc_s, mod)]

    y_prompt = xs[0].reshape(batch, seq, d)
    y_sample = _final_norm(xs[1], *prevs[1], fg, mod_rows[1]).reshape(dec_batch, dec_seq, d)
    return (y_prompt, y_sample, jnp.stack(new_lru, axis=1), jnp.stack(new_k, axis=1), jnp.stack(new_v, axis=1))
```

```python
import functools
import math

import jax
import jax.numpy as jnp
import numpy as np
from jax import lax
from jax.experimental import pallas as pl
from jax.experimental.pallas import tpu as pltpu

F32 = jnp.float32
BF16 = jnp.bfloat16
I32 = jnp.int32

LANES = 128
SUBLANES = 8
MIB = 1024 * 1024
F32_TINY = float(np.finfo(np.float32).tiny)

RG_C = 8.0
CONV_W = 4
CONV_PAD_L = 2
ROPE_BASE = 10000.0
GRID_W = 64
EPS = 1e-6
N_EXPERTS = 16
CAPACITY_FACTOR = 2
HEAD_DIM = 128
LRU_BLOCKS = 8
LRU_STEP_BLOCKS = 2

ROW_BLOCK = 512


def _params(n_axes, vmem_mib):
    return pltpu.CompilerParams(
        dimension_semantics=("arbitrary",) * n_axes, vmem_limit_bytes=vmem_mib * MIB)


def _split_bf16(x):
    hi = x.astype(BF16)
    lo = (x - hi.astype(F32)).astype(BF16)
    return hi, lo


def _modulated_norm(x, gain, shift, scale):
    y = x * lax.rsqrt(jnp.mean(x * x, axis=-1, keepdims=True) + EPS)
    return (y * gain) * (1.0 + scale) + shift


def _tiles_per_row(d):
    return d // LANES


def _load_rows(ref, n_rows, d):
    if ref.shape == (n_rows, d):
        return ref[...]
    tpr = _tiles_per_row(d)
    return jnp.concatenate([ref[pl.ds(s, n_rows, stride=tpr), :] for s in range(tpr)], axis=1)


def _store_rows(ref, x, r0):
    n_rows, d = x.shape
    if ref.shape[1] == d:
        ref[r0:r0 + n_rows, :] = x.astype(ref.dtype)
        return
    tpr = _tiles_per_row(d)
    for s in range(tpr):
        ref[pl.ds(r0 * tpr + s, n_rows, stride=tpr), :] = x[:, s * LANES:(s + 1) * LANES]


def _rows_spec(arr, d):
    if arr.shape[1] == d:
        return pl.BlockSpec((ROW_BLOCK, d), lambda i: (i, 0))
    return pl.BlockSpec((ROW_BLOCK * _tiles_per_row(d), LANES), lambda i: (i, 0))


def _residual_stream(x_ref, prev, m):
    if prev is None:
        return x_ref[...]
    acc_ref, pmod_ref, xo_ref = prev
    n_rows, d = x_ref.shape
    x = x_ref[...] + pmod_ref[pl.ds(m, 1), pl.ds(5 * d, d)] * _load_rows(acc_ref, n_rows, d)
    xo_ref[...] = x
    return x


def _mod_kernel(cond_ref, w_ref, b_ref, o_ref):
    c = cond_ref[...]
    a_hi, a_lo = _split_bf16(c * jax.nn.sigmoid(c))
    w_hi, w_lo = _split_bf16(w_ref[0])
    dot = functools.partial(jnp.dot, preferred_element_type=F32)
    o_ref[0] = dot(a_hi, w_hi) + (dot(a_lo, w_hi) + dot(a_hi, w_lo)) + b_ref[0]


def _mod_vectors(cond8, w_mod, b_mod):
    depth, d, d6 = w_mod.shape
    nb = 1536
    return pl.pallas_call(
        _mod_kernel,
        grid=(depth, d6 // nb),
        in_specs=[
            pl.BlockSpec((SUBLANES, d), lambda l, j: (0, 0)),
            pl.BlockSpec((1, d, nb), lambda l, j: (l, 0, j)),
            pl.BlockSpec((1, 1, nb), lambda l, j: (l, 0, j)),
        ],
        out_specs=pl.BlockSpec((1, SUBLANES, nb), lambda l, j: (l, 0, j)),
        out_shape=jax.ShapeDtypeStruct((depth, SUBLANES, d6), F32),
        compiler_params=_params(2, 40),
        name="adaln_mod",
    )(cond8, w_mod, b_mod.reshape(depth, 1, d6))


def _lru_in_kernel(*refs, mod_row, has_prev):
    if has_prev:
        x_ref, acc_ref, pmod_ref, mod_ref, gain_ref, w_ref, xo_ref, gate_ref, xr_ref = refs
        prev = (acc_ref, pmod_ref, xo_ref)
    else:
        x_ref, mod_ref, gain_ref, w_ref, gate_ref, xr_ref = refs
        prev = None
    d = x_ref.shape[1]
    m = mod_row(pl.program_id(0))
    x = _residual_stream(x_ref, prev, m)
    shift = mod_ref[pl.ds(m, 1), pl.ds(0, d)]
    scale = mod_ref[pl.ds(m, 1), pl.ds(d, d)]
    h = _modulated_norm(x, gain_ref[...], shift, scale).astype(BF16)
    u = jnp.dot(h, w_ref[...], preferred_element_type=F32)
    w = gate_ref.shape[1]
    gate_ref[...] = u[:, :w]
    xr_ref[...] = u[:, w:]


def _lru_in(x, prev, mod, gain, w_in, mod_row):
    n, d = x.shape
    w2 = w_in.shape[1]
    w = w2 // 2
    row = lambda i: (i, 0)
    fixed = lambda i: (0, 0)
    in_specs = [pl.BlockSpec((ROW_BLOCK, d), row)]
    args = [x]
    out_specs = [pl.BlockSpec((ROW_BLOCK, w), row), pl.BlockSpec((ROW_BLOCK, w), row)]
    out_shape = [jax.ShapeDtypeStruct((n, w), F32), jax.ShapeDtypeStruct((n, w), F32)]
    if prev is not None:
        in_specs += [_rows_spec(prev[0], d), pl.BlockSpec(prev[1].shape, fixed)]
        args += list(prev)
        out_specs = [pl.BlockSpec((ROW_BLOCK, d), row)] + out_specs
        out_shape = [jax.ShapeDtypeStruct((n, d), F32)] + out_shape
    in_specs += [pl.BlockSpec(mod.shape, fixed), pl.BlockSpec((1, d), fixed), pl.BlockSpec((d, w2), fixed)]
    args += [mod, gain, w_in]
    outs = pl.pallas_call(
        functools.partial(_lru_in_kernel, mod_row=mod_row, has_prev=prev is not None),
        grid=(n // ROW_BLOCK,),
        in_specs=in_specs,
        out_specs=out_specs,
        out_shape=out_shape,
        compiler_params=_params(1, 48),
        name="lru_in",
    )(*args)
    return outs if prev is not None else [x] + list(outs)


def _lru_core_kernel(xr_ref, gate_ref, cw_ref, cb_ref, wg_ref, bias_ref, lam_ref, h0_ref,
                     y_ref, fin_ref, pad, a_f, b_f, a_b, b_b):
    t_len, lb = xr_ref.shape
    ch = 256
    halo = SUBLANES
    seg = t_len // SUBLANES
    piece = min(ch, seg)
    seg_shift = seg.bit_length() - 1

    def seg_rows(t0):
        r = lax.shift_right_logical(t0, seg_shift)
        return pl.ds((t0 - r * seg) * SUBLANES + r, piece, stride=SUBLANES)

    zero_rows = jnp.zeros((halo, lb), F32)
    pad[0:halo, :] = zero_rows
    pad[t_len + halo:t_len + 2 * halo, :] = zero_rows

    def copy_in(c, carry):
        r = pl.multiple_of(c * ch, ch)
        pad[pl.ds(r + halo, ch), :] = xr_ref[pl.ds(r, ch), :]
        return carry

    lax.fori_loop(0, t_len // ch, copy_in, 0)

    neg_lam = -lam_ref[...]
    softplus = jnp.maximum(neg_lam, 0.0) + jnp.log1p(jnp.exp(-jnp.abs(neg_lam)))
    c_nla = (0.5 * RG_C) * softplus
    c_exp2 = (-0.5 * RG_C * math.log2(math.e)) * softplus
    cw = cw_ref[...]
    cbias = cb_ref[...]
    half_bias = bias_ref[...]
    half_wg = wg_ref[...]
    bw = half_wg.shape[1]
    n_slab = lb // LANES

    def gates(c, carry):
        r = pl.multiple_of(c * ch, ch)
        blk = pad[pl.ds(r, ch + 2 * halo), :]
        xc = cbias
        for k in range(CONV_W):
            o = halo - CONV_PAD_L + k
            xc = xc + blk[o:o + ch] * cw[k:k + 1]
        xc_bf = xc.astype(BF16)
        g = [jnp.dot(xc_bf[:, j * bw:(j + 1) * bw], half_wg[j], preferred_element_type=F32)
             for j in range(lb // bw)]
        pre = lambda k: jnp.concatenate([gj[:, k * bw:(k + 1) * bw] for gj in g], axis=1) + half_bias[k:k + 1]
        half_xc = 0.5 * xc
        for d, (a_ref, b_ref) in enumerate(((a_f, b_f), (a_b, b_b))):
            u = jnp.tanh(pre(2 * d)) + 1.0
            t_i = jnp.tanh(pre(2 * d + 1))
            a = jnp.exp2(u * c_exp2[d:d + 1])
            q = jnp.tanh(u * c_nla[d:d + 1]) * (1.0 + a * a)
            b = (q * lax.rsqrt(jnp.maximum(q, F32_TINY))) * (t_i * half_xc + half_xc)
            for p in range(ch // piece):
                rows = seg_rows(r + p * piece)
                for j in range(n_slab):
                    a_ref[j, rows, :] = a[p * piece:(p + 1) * piece, j * LANES:(j + 1) * LANES]
                    b_ref[j, rows, :] = b[p * piece:(p + 1) * piece, j * LANES:(j + 1) * LANES]
        return carry

    lax.fori_loop(0, t_len // ch, gates, 0)

    def scan(i, carry):
        rows = (pl.ds(pl.multiple_of(i * SUBLANES, SUBLANES), SUBLANES),
                pl.ds(pl.multiple_of((seg - 1 - i) * SUBLANES, SUBLANES), SUBLANES))
        out = []
        for k, (h, prod) in enumerate(carry):
            d, j = divmod(k, n_slab)
            a_ref, b_ref = ((a_f, b_f), (a_b, b_b))[d]
            a = a_ref[j, rows[d], :]
            h = a * h + b_ref[j, rows[d], :]
            prod = a * prod
            b_ref[j, rows[d], :] = h
            a_ref[j, rows[d], :] = prod
            out.append((h, prod))
        return tuple(out)

    start = (jnp.zeros((SUBLANES, LANES), F32), jnp.ones((SUBLANES, LANES), F32))
    ends = lax.fori_loop(0, seg, scan, (start,) * (2 * n_slab), unroll=8)

    enter = []
    for k, (h, prod) in enumerate(ends):
        d, j = divmod(k, n_slab)
        states = [h0_ref[0, d:d + 1, j * LANES:(j + 1) * LANES]]
        for r in (range(SUBLANES) if d == 0 else reversed(range(SUBLANES))):
            states.append(h[r:r + 1] + prod[r:r + 1] * states[-1])
        fin_ref[0, d:d + 1, j * LANES:(j + 1) * LANES] = states[SUBLANES]
        order = states[:SUBLANES] if d == 0 else states[:SUBLANES][::-1]
        enter.append(jnp.concatenate(order, axis=0))

    def fix(i, carry):
        rows = pl.ds(pl.multiple_of(i * SUBLANES, SUBLANES), SUBLANES)
        for j in range(n_slab):
            b_f[j, rows, :] = ((b_f[j, rows, :] + a_f[j, rows, :] * enter[j])
                               + (b_b[j, rows, :] + a_b[j, rows, :] * enter[n_slab + j]))
        return carry

    lax.fori_loop(0, seg, fix, 0, unroll=8)

    def emit(c, carry):
        r = pl.multiple_of(c * ch, ch)
        gt = gate_ref[pl.ds(r, ch), :]
        cdf = 0.5 * (1.0 + jnp.tanh(math.sqrt(2.0 / math.pi) * (gt + 0.044715 * (gt * gt * gt))))
        y = gt * cdf
        for p in range(ch // piece):
            rows = seg_rows(r + p * piece)
            total = jnp.concatenate([b_f[j, rows, :] for j in range(n_slab)], axis=1)
            y_ref[pl.ds(r + p * piece, piece), :] = (total * y[p * piece:(p + 1) * piece]).astype(BF16)
        return carry

    lax.fori_loop(0, t_len // ch, emit, 0)


def _lru_core(xr, gate, conv_w, conv_b, wg, bias4, lam, h0, *, seq_len):
    n, w = xr.shape
    n_seq = n // seq_len
    bw = w // LRU_BLOCKS
    lb = LRU_STEP_BLOCKS * bw
    tok = lambda s, c: (s, c)
    chan = lambda s, c: (0, c)
    return pl.pallas_call(
        _lru_core_kernel,
        grid=(n_seq, LRU_BLOCKS // LRU_STEP_BLOCKS),
        in_specs=[
            pl.BlockSpec((seq_len, lb), tok),
            pl.BlockSpec((seq_len, lb), tok),
            pl.BlockSpec((CONV_W, lb), chan),
            pl.BlockSpec((1, lb), chan),
            pl.BlockSpec((LRU_STEP_BLOCKS, bw, 4 * bw), lambda s, c: (c, 0, 0)),
            pl.BlockSpec((4, lb), chan),
            pl.BlockSpec((2, lb), chan),
            pl.BlockSpec((1, 2, lb), lambda s, c: (s, 0, c)),
        ],
        out_specs=[pl.BlockSpec((seq_len, lb), tok),
                   pl.BlockSpec((1, 2, lb), lambda s, c: (s, 0, c))],
        out_shape=[jax.ShapeDtypeStruct((n, w), BF16), jax.ShapeDtypeStruct((n_seq, 2, w), F32)],
        scratch_shapes=[pltpu.VMEM((seq_len + 2 * SUBLANES, lb), F32)]
        + [pltpu.VMEM((lb // LANES, seq_len, LANES), F32)] * 4,
        compiler_params=_params(2, 48),
        name=f"lru_core_{seq_len}",
    )(xr, gate, conv_w, conv_b, wg, bias4, lam, h0)


def _mix_out_kernel(a_ref, x_ref, mod_ref, gain_ref, w_ref, wrh_ref, wrl_ref,
                    x1_ref, h2_ref, lg_ref, *, mod_row):
    d = x_ref.shape[1]
    m = mod_row(pl.program_id(0))
    g_mix = mod_ref[pl.ds(m, 1), pl.ds(2 * d, d)]
    shift = mod_ref[pl.ds(m, 1), pl.ds(3 * d, d)]
    scale = mod_ref[pl.ds(m, 1), pl.ds(4 * d, d)]
    op = jnp.dot(a_ref[...], w_ref[...], preferred_element_type=F32)
    x1 = x_ref[...] + g_mix * op
    x1_ref[...] = x1
    h2 = _modulated_norm(x1, gain_ref[...], shift, scale)
    h_hi, h_lo = _split_bf16(h2)
    _store_rows(h2_ref, h2, 0)
    nt = functools.partial(lax.dot_general, dimension_numbers=(((1,), (1,)), ((), ())),
                           preferred_element_type=F32)
    wr_hi = wrh_ref[0]
    lg_ref[...] = nt(wr_hi, h_hi) + (nt(wrl_ref[0], h_hi) + nt(wr_hi, h_lo))


def _mix_out(a, x, mod, gain, w_out, wr_hi, wr_lo, layer, mod_row, token_tiled):
    n, d = x.shape
    e = wr_hi.shape[1]
    row = lambda i: (i, 0)
    fixed = lambda i: (0, 0)
    router = pl.BlockSpec((1,) + wr_hi.shape[1:], lambda i: (layer, 0, 0))
    tpr = _tiles_per_row(d)
    if token_tiled:
        h2_spec = pl.BlockSpec((ROW_BLOCK * tpr, LANES), row)
        h2_shape = jax.ShapeDtypeStruct((n * tpr, LANES), F32)
    else:
        h2_spec = pl.BlockSpec((ROW_BLOCK, d), row)
        h2_shape = jax.ShapeDtypeStruct((n, d), BF16)
    return pl.pallas_call(
        functools.partial(_mix_out_kernel, mod_row=mod_row),
        grid=(n // ROW_BLOCK,),
        in_specs=[
            pl.BlockSpec((ROW_BLOCK, a.shape[1]), row),
            pl.BlockSpec((ROW_BLOCK, d), row),
            pl.BlockSpec(mod.shape, fixed),
            pl.BlockSpec((1, d), fixed),
            pl.BlockSpec(w_out.shape, fixed),
            router,
            router,
        ],
        out_specs=[pl.BlockSpec((ROW_BLOCK, d), row), h2_spec,
                   pl.BlockSpec((e, ROW_BLOCK), lambda i: (0, i))],
        out_shape=[jax.ShapeDtypeStruct((n, d), F32), h2_shape, jax.ShapeDtypeStruct((e, n), F32)],
        compiler_params=_params(1, 48),
        name="mix_out",
    )(a, x, mod, gain, w_out, wr_hi, wr_lo)


def _lane_sum(tiles):
    acc = tiles[0]
    for t in tiles[1:]:
        acc = acc + t
    return jnp.sum(acc, axis=1, keepdims=True)


def _exclusive_cumsum(flags, upper):
    out = []
    off = jnp.zeros((flags[0].shape[0], 1), F32)
    for f in flags:
        out.append(jnp.dot(f.astype(BF16), upper, preferred_element_type=F32) + off)
        off = off + jnp.sum(f, axis=1, keepdims=True)
    return out


def _route_group(lg_ref, n_seq, t_len, idx_ref, g_ref):
    cap = CAPACITY_FACTOR * t_len // N_EXPERTS
    affs = []
    for s in range(n_seq):
        lg = lg_ref[:, s * t_len:(s + 1) * t_len]
        ex = jnp.exp(lg - jnp.max(lg, axis=0, keepdims=True))
        affs.append(ex / jnp.sum(ex, axis=0, keepdims=True))
    aff = jnp.concatenate(affs, axis=0)
    n_rows = aff.shape[0]
    nt = t_len // LANES
    g = [aff[:, j * LANES:(j + 1) * LANES] for j in range(nt)]

    kth_bits = jnp.zeros((n_rows, 1), I32)
    for bit in range(30, -1, -1):
        cand = kth_bits | (1 << bit)
        cand_f = pltpu.bitcast(cand, F32)
        cnt = _lane_sum([jnp.where(t >= cand_f, 1, 0) for t in g])
        kth_bits = jnp.where(cnt >= cap, cand, kth_bits)
    kth = pltpu.bitcast(kth_bits, F32)

    lane = lax.broadcasted_iota(I32, (LANES, LANES), 0)
    upper = jnp.where(lane < lax.broadcasted_iota(I32, (LANES, LANES), 1), 1.0, 0.0).astype(BF16)
    gt = [t > kth for t in g]
    eq = [t == kth for t in g]
    need = (cap - _lane_sum([jnp.where(m, 1, 0) for m in gt])).astype(F32)
    eq_rank = _exclusive_cumsum([jnp.where(m, 1.0, 0.0) for m in eq], upper)
    sel = [jnp.logical_or(gt[j], jnp.logical_and(eq[j], eq_rank[j] < need)) for j in range(nt)]
    pos = _exclusive_cumsum([jnp.where(m, 1.0, 0.0) for m in sel], upper)

    lane_r = lax.broadcasted_iota(I32, (n_rows, LANES), 1)
    d = [jnp.where(sel[j], lane_r + j * LANES - pos[j].astype(I32), -1) for j in range(nt)]
    for k in range(t_len.bit_length() - 1):
        s = 1 << k
        if s < LANES:
            d_rot = [pltpu.roll(x, LANES - s, 1) for x in d]
            g_rot = [pltpu.roll(x, LANES - s, 1) for x in g]
            same = lane_r < LANES - s
            d_in = [jnp.where(same, d_rot[j], d_rot[(j + 1) % nt]) for j in range(nt)]
            g_in = [jnp.where(same, g_rot[j], g_rot[(j + 1) % nt]) for j in range(nt)]
        else:
            q = s // LANES
            d_in = [d[(j + q) % nt] for j in range(nt)]
            g_in = [g[(j + q) % nt] for j in range(nt)]
        new_d, new_g = [], []
        for j in range(nt):
            move = jnp.logical_and(d_in[j] >= 0, ((d_in[j] >> k) & 1) == 1)
            stay = jnp.logical_and(d[j] >= 0, ((d[j] >> k) & 1) == 0)
            new_d.append(jnp.where(move, d_in[j], jnp.where(stay, d[j], -1)))
            new_g.append(jnp.where(move, g_in[j], g[j]))
        d, g = new_d, new_g

    for j in range(idx_ref.shape[1] // LANES):
        idx_ref[:, j * LANES:(j + 1) * LANES] = lane_r + j * LANES + d[j]
        g_ref[:, j * LANES:(j + 1) * LANES] = g[j]


def _route_kernel(lgp_ref, lgs_ref, idx_p_ref, g_p_ref, idx_s_ref, g_s_ref, *, p_len, s_len):
    _route_group(lgp_ref, lgp_ref.shape[1] // p_len, p_len, idx_p_ref, g_p_ref)
    _route_group(lgs_ref, lgs_ref.shape[1] // s_len, s_len, idx_s_ref, g_s_ref)


def _route(lg_p, lg_s, p_len, s_len):
    e = lg_p.shape[0]
    np_seq = lg_p.shape[1] // p_len
    ns_seq = lg_s.shape[1] // s_len
    cap_p = CAPACITY_FACTOR * p_len // N_EXPERTS
    cap_s = CAPACITY_FACTOR * s_len // N_EXPERTS
    wp = max(cap_p, LANES)
    ws = max(cap_s, LANES)
    idx_p, g_p, idx_s, g_s = pl.pallas_call(
        functools.partial(_route_kernel, p_len=p_len, s_len=s_len),
        out_shape=[jax.ShapeDtypeStruct((np_seq * e, wp), I32),
                   jax.ShapeDtypeStruct((np_seq * e, wp), F32),
                   jax.ShapeDtypeStruct((ns_seq * e, ws), I32),
                   jax.ShapeDtypeStruct((ns_seq * e, ws), F32)],
        compiler_params=pltpu.CompilerParams(vmem_limit_bytes=40 * MIB),
        name="route",
    )(lg_p, lg_s)
    return (idx_p[:, :cap_p].reshape(np_seq, e * cap_p), g_p[:, :cap_p].reshape(np_seq, e * cap_p),
            idx_s[:, :cap_s].reshape(-1), g_s[:, :cap_s].reshape(-1))


def _gather_onehot_kernel(idx_ref, h_ref, o_ref):
    t_len = h_ref.shape[0]
    n_exp, cap, _ = o_ref.shape
    idx = idx_ref[0]
    hit = jnp.where(idx == lax.broadcasted_iota(I32, (t_len, idx.shape[1]), 0), 1.0, 0.0).astype(BF16)
    xs = lax.dot_general(hit, h_ref[...], (((0,), (0,)), ((), ())), preferred_element_type=F32).astype(BF16)
    for e in range(n_exp):
        o_ref[e] = xs[e * cap:(e + 1) * cap]


def _gather_onehot(idx_row, h, *, seq_len):
    n, d = h.shape
    n_seq, _, slots = idx_row.shape
    cap = slots // N_EXPERTS
    return pl.pallas_call(
        _gather_onehot_kernel,
        grid=(n_seq,),
        in_specs=[pl.BlockSpec((1, 1, slots), lambda s: (s, 0, 0)),
                  pl.BlockSpec((seq_len, d), lambda s: (s, 0))],
        out_specs=pl.BlockSpec((N_EXPERTS, cap, d), lambda s: (0, s, 0)),
        out_shape=jax.ShapeDtypeStruct((N_EXPERTS, n_seq * cap, d), BF16),
        compiler_params=_params(1, 32),
        name="gather_onehot",
    )(idx_row, h)


def _gather_rows_kernel(idx_ref, h_ref, o_ref, tiles, *, cap, n_exp):
    tpr = _tiles_per_row(o_ref.shape[2])
    base = (pl.program_id(0) * n_exp + pl.program_id(1)) * cap

    def body(c, carry):
        src = pl.multiple_of(idx_ref[base + c], tpr)
        tiles[pl.ds(pl.multiple_of(c * tpr, tpr), tpr), :] = h_ref[pl.ds(src, tpr), :]
        return carry

    lax.fori_loop(0, cap, body, 0, unroll=8)
    for s in range(tpr):
        o_ref[0, :, s * LANES:(s + 1) * LANES] = tiles[pl.ds(s, cap, stride=tpr), :].astype(BF16)


def _gather_rows(idx, h, d, *, seq_len):
    tpr = _tiles_per_row(d)
    n_seq = h.shape[0] // (seq_len * tpr)
    cap = CAPACITY_FACTOR * seq_len // N_EXPERTS
    return pl.pallas_call(
        functools.partial(_gather_rows_kernel, cap=cap, n_exp=N_EXPERTS),
        grid_spec=pltpu.PrefetchScalarGridSpec(
            num_scalar_prefetch=1,
            grid=(n_seq, N_EXPERTS),
            in_specs=[pl.BlockSpec((seq_len * tpr, LANES), lambda s, e, idx: (s, 0))],
            out_specs=pl.BlockSpec((1, cap, d), lambda s, e, idx: (e, s, 0)),
            scratch_shapes=[pltpu.VMEM((cap * tpr, LANES), F32)],
        ),
        out_shape=jax.ShapeDtypeStruct((N_EXPERTS, n_seq * cap, d), BF16),
        compiler_params=_params(2, 48),
        name="gather_rows",
    )(idx, h)


def _ffn_kernel(xp_ref, xs_ref, gp_ref, gs_ref, wg_ref, wu_ref, wd_ref, yp_ref, ys_ref):
    rc = 512

    @pl.when(pl.program_id(1) == 0)
    def _():
        for y_ref in (yp_ref, ys_ref):
            for r in range(0, y_ref.shape[1], rc):
                y_ref[0, r:r + rc, :] = jnp.zeros((rc, y_ref.shape[2]), F32)

    wg = wg_ref[0, 0].astype(BF16)
    wu = wu_ref[0, 0].astype(BF16)
    wd = wd_ref[0, 0].astype(BF16)
    last = pl.program_id(1) == pl.num_programs(1) - 1
    for x_ref, g_ref, y_ref in ((xp_ref, gp_ref, yp_ref), (xs_ref, gs_ref, ys_ref)):
        for r in range(0, x_ref.shape[1], rc):
            x = x_ref[0, r:r + rc, :]
            hg = jnp.dot(x, wg, preferred_element_type=F32)
            hu = jnp.dot(x, wu, preferred_element_type=F32)
            hid = ((hg * jax.nn.sigmoid(hg)) * hu).astype(BF16)
            g_hi, g_lo = _split_bf16(g_ref[0, :, r:r + rc])
            ones = jnp.ones((g_hi.shape[0], LANES), BF16)
            tn = functools.partial(lax.dot_general, dimension_numbers=(((0,), (0,)), ((), ())),
                                   preferred_element_type=F32)
            weight = jnp.where(last, tn(g_hi, ones) + tn(g_lo, ones), 1.0)
            weight = jnp.concatenate([weight] * (y_ref.shape[2] // LANES), axis=1)
            y_ref[0, r:r + rc, :] = (y_ref[0, r:r + rc, :] + jnp.dot(hid, wd, preferred_element_type=F32)) * weight


def _ffn(xs_p, xs_s, g_p, g_s, w_gate, w_up, w_down, layer):
    n_exp, rp, d = xs_p.shape
    rs = xs_s.shape[1]
    ff = w_gate.shape[3]
    fc = 1024
    return pl.pallas_call(
        _ffn_kernel,
        grid=(n_exp, ff // fc),
        in_specs=[
            pl.BlockSpec((1, rp, d), lambda e, f: (e, 0, 0)),
            pl.BlockSpec((1, rs, d), lambda e, f: (e, 0, 0)),
            pl.BlockSpec((1, SUBLANES, rp), lambda e, f: (e, 0, 0)),
            pl.BlockSpec((1, SUBLANES, rs), lambda e, f: (e, 0, 0)),
            pl.BlockSpec((1, 1, d, fc), lambda e, f: (layer, e, 0, f)),
            pl.BlockSpec((1, 1, d, fc), lambda e, f: (layer, e, 0, f)),
            pl.BlockSpec((1, 1, fc, d), lambda e, f: (layer, e, f, 0)),
        ],
        out_specs=[pl.BlockSpec((1, rp, d), lambda e, f: (e, 0, 0)),
                   pl.BlockSpec((1, rs, d), lambda e, f: (e, 0, 0))],
        out_shape=[jax.ShapeDtypeStruct((n_exp, rp, d), F32),
                   jax.ShapeDtypeStruct((n_exp, rs, d), F32)],
        compiler_params=_params(2, 58),
        name="expert_ffn",
    )(xs_p, xs_s, g_p, g_s, w_gate, w_up, w_down)


def _combine_onehot_kernel(*refs, mod_row, final):
    if final:
        idx_ref, ye_ref, x_ref, pmod_ref, gain_ref, o_ref = refs
    else:
        idx_ref, ye_ref, x_ref, pmod_ref, o_ref = refs
    t_len, d = o_ref.shape
    n_exp = ye_ref.shape[0]
    idx = idx_ref[0]
    hit = jnp.where(idx == lax.broadcasted_iota(I32, (t_len, idx.shape[1]), 0), 1.0, 0.0).astype(BF16)
    y_hi, y_lo = _split_bf16(jnp.concatenate([ye_ref[e] for e in range(n_exp)], axis=0))
    acc = jnp.dot(hit, y_hi, preferred_element_type=F32) + jnp.dot(hit, y_lo, preferred_element_type=F32)
    x = x_ref[...] + pmod_ref[pl.ds(mod_row(pl.program_id(0)), 1), pl.ds(5 * d, d)] * acc
    if final:
        x = (x * lax.rsqrt(jnp.mean(x * x, axis=-1, keepdims=True) + EPS)) * gain_ref[...]
    o_ref[...] = x


def _combine_onehot(idx_row, ye, x, pmod, final_gain, *, seq_len, mod_row):
    n_exp, rows, d = ye.shape
    n_seq, _, slots = idx_row.shape
    cap = slots // n_exp
    in_specs = [pl.BlockSpec((1, 1, slots), lambda s: (s, 0, 0)),
                pl.BlockSpec((n_exp, cap, d), lambda s: (0, s, 0)),
                pl.BlockSpec((seq_len, d), lambda s: (s, 0)),
                pl.BlockSpec(pmod.shape, lambda s: (0, 0))]
    args = [idx_row, ye, x, pmod]
    if final_gain is not None:
        in_specs.append(pl.BlockSpec((1, d), lambda s: (0, 0)))
        args.append(final_gain)
    return pl.pallas_call(
        functools.partial(_combine_onehot_kernel, mod_row=mod_row, final=final_gain is not None),
        grid=(n_seq,),
        in_specs=in_specs,
        out_specs=pl.BlockSpec((seq_len, d), lambda s: (s, 0)),
        out_shape=jax.ShapeDtypeStruct((n_seq * seq_len, d), F32),
        compiler_params=_params(1, 32),
        name="combine_onehot",
    )(*args)


COMBINE_GROUP = 16


def _combine_rows_kernel(idx_ref, ye_ref, o_ref, tiles, *, cap, n_exp):
    e = pl.program_id(1)
    d = ye_ref.shape[2]
    tpr = _tiles_per_row(d)
    ch = 2048

    @pl.when(e == 0)
    def _():
        def zero(c, carry):
            o_ref[pl.ds(pl.multiple_of(c * ch, ch), ch), :] = jnp.zeros((ch, LANES), F32)
            return carry
        lax.fori_loop(0, o_ref.shape[0] // ch, zero, 0)

    for s in range(tpr):
        tiles[pl.ds(s, cap, stride=tpr), :] = ye_ref[0, :, s * LANES:(s + 1) * LANES]

    base = (pl.program_id(0) * n_exp + e) * cap

    def body(i, carry):
        c0 = i * COMBINE_GROUP
        dst = [pl.ds(pl.multiple_of(idx_ref[base + c0 + k], tpr), tpr) for k in range(COMBINE_GROUP)]
        vals = [o_ref[dst[k], :] + tiles[pl.ds(pl.multiple_of((c0 + k) * tpr, tpr), tpr), :]
                for k in range(COMBINE_GROUP)]
        for k in range(COMBINE_GROUP):
            o_ref[dst[k], :] = vals[k]
        return carry

    lax.fori_loop(0, cap // COMBINE_GROUP, body, 0)


def _combine_rows(idx, ye, *, seq_len):
    n_exp, rows, d = ye.shape
    tpr = _tiles_per_row(d)
    cap = CAPACITY_FACTOR * seq_len // N_EXPERTS
    n_seq = rows // cap
    return pl.pallas_call(
        functools.partial(_combine_rows_kernel, cap=cap, n_exp=n_exp),
        grid_spec=pltpu.PrefetchScalarGridSpec(
            num_scalar_prefetch=1,
            grid=(n_seq, n_exp),
            in_specs=[pl.BlockSpec((1, cap, d), lambda s, e, i: (e, s, 0))],
            out_specs=pl.BlockSpec((seq_len * tpr, LANES), lambda s, e, i: (s, 0)),
            scratch_shapes=[pltpu.VMEM((cap * tpr, LANES), F32)],
        ),
        out_shape=jax.ShapeDtypeStruct((n_seq * seq_len * tpr, LANES), F32),
        compiler_params=_params(2, 48),
        name="combine_rows",
    )(idx, ye)


def _head_norm(x, gain):
    return x * lax.rsqrt(jnp.mean(x * x, axis=-1, keepdims=True) + EPS) * gain


def _qkv_kernel(*refs, rope, has_prev, mod_row, n_heads, n_kv):
    refs = list(refs)
    x_ref = refs.pop(0)
    prev = None
    if has_prev:
        acc_ref, pmod_ref = refs.pop(0), refs.pop(0)
    mod_ref, gain_ref, w_ref, qg_ref, kg_ref = refs[:5]
    refs = refs[5:]
    if rope:
        cos_ref, sin_ref = refs.pop(0), refs.pop(0)
    if has_prev:
        prev = (acc_ref, pmod_ref, refs.pop(0))
    q_ref, k_ref, v_ref = refs[:3]
    d = x_ref.shape[1]
    hd = HEAD_DIM
    m = mod_row(pl.program_id(0))
    x = _residual_stream(x_ref, prev, m)
    shift = mod_ref[pl.ds(m, 1), pl.ds(0, d)]
    scale = mod_ref[pl.ds(m, 1), pl.ds(d, d)]
    h = _modulated_norm(x, gain_ref[...], shift, scale).astype(BF16)
    qg = qg_ref[...]
    kg = kg_ref[...]
    q_scale = math.log2(math.e) * hd ** -0.5
    half = x_ref.shape[0] // 2
    for r0 in (0, half):
        rows = slice(r0, r0 + half)
        qkv = jnp.dot(h[rows], w_ref[...], preferred_element_type=F32)
        if rope:
            cos = cos_ref[rows, :]
            sin = sin_ref[rows, :]

            def rot(xh, cos=cos, sin=sin):
                return xh * cos + pltpu.roll(xh, hd // 2, 1) * sin
        else:
            rot = lambda xh: xh

        for i in range(n_heads):
            qh = rot(_head_norm(qkv[:, i * hd:(i + 1) * hd], qg)) * q_scale
            q_ref[rows, i * hd:(i + 1) * hd] = qh.astype(BF16)
        for i in range(n_kv):
            c0 = (n_heads + i) * hd
            kh = _head_norm(qkv[:, c0:c0 + hd], kg)
            if not rope:
                refs[3][rows, i * hd:(i + 1) * hd] = kh
            k_ref[rows, i * hd:(i + 1) * hd] = rot(kh).astype(BF16)
        v = qkv[:, (n_heads + n_kv) * hd:]
        v_ref[rows, :] = v.astype(BF16)
        if not rope:
            refs[4][rows, :] = v


def _qkv(x, prev, mod, gain, w_qkv, q_gain, k_gain, tables, *, seq_len, mod_row, n_heads, n_kv):
    n, d = x.shape
    hd = HEAD_DIM
    row = lambda i: (i, 0)
    fixed = lambda i: (0, 0)
    rope = tables is not None
    in_specs = [pl.BlockSpec((ROW_BLOCK, d), row)]
    args = [x]
    if prev is not None:
        in_specs += [_rows_spec(prev[0], d), pl.BlockSpec(prev[1].shape, fixed)]
        args += list(prev)
    in_specs += [
        pl.BlockSpec(mod.shape, fixed),
        pl.BlockSpec((1, d), fixed),
        pl.BlockSpec(w_qkv.shape, fixed),
        pl.BlockSpec((1, hd), fixed),
        pl.BlockSpec((1, hd), fixed),
    ]
    args += [mod, gain, w_qkv, q_gain, k_gain]
    if rope:
        per_seq = seq_len // ROW_BLOCK
        in_specs += [pl.BlockSpec((ROW_BLOCK, hd), lambda i: (i % per_seq, 0))] * 2
        args += list(tables)
    out_specs, out_shape = [], []
    if prev is not None:
        out_specs.append(pl.BlockSpec((ROW_BLOCK, d), row))
        out_shape.append(jax.ShapeDtypeStruct((n, d), F32))
    out_specs += [pl.BlockSpec((ROW_BLOCK, n_heads * hd), row),
                  pl.BlockSpec((ROW_BLOCK, n_kv * hd), row),
                  pl.BlockSpec((ROW_BLOCK, n_kv * hd), row)]
    out_shape += [jax.ShapeDtypeStruct((n, n_heads * hd), BF16),
                  jax.ShapeDtypeStruct((n, n_kv * hd), BF16),
                  jax.ShapeDtypeStruct((n, n_kv * hd), BF16)]
    if not rope:
        out_specs += [pl.BlockSpec((ROW_BLOCK, n_kv * hd), row)] * 2
        out_shape += [jax.ShapeDtypeStruct((n, n_kv * hd), F32)] * 2
    outs = pl.pallas_call(
        functools.partial(_qkv_kernel, rope=rope, has_prev=prev is not None, mod_row=mod_row,
                          n_heads=n_heads, n_kv=n_kv),
        grid=(n // ROW_BLOCK,),
        in_specs=in_specs,
        out_specs=out_specs,
        out_shape=out_shape,
        compiler_params=_params(1, 48),
        name="qkv_rope" if rope else "qkv",
    )(*args)
    return list(outs) if prev is not None else [x] + list(outs)


def _attn_kernel(*refs, group, chunk, has_cache):
    if has_cache:
        q_ref, k_ref, v_ref, kc_ref, vc_ref, o_ref = refs
    else:
        q_ref, k_ref, v_ref, o_ref = refs
    hd = HEAD_DIM
    qb = q_ref.shape[0]
    rows = group * qb
    q = jnp.concatenate([q_ref[:, g * hd:(g + 1) * hd] for g in range(group)], axis=0)
    sources = [(k_ref, v_ref, s0, chunk) for s0 in range(0, k_ref.shape[0], chunk)]
    if has_cache:
        sources.append((kc_ref.at[0], vc_ref.at[0], 0, kc_ref.shape[1]))
    m = jnp.full((rows, 1), -jnp.inf, F32)
    acc = jnp.zeros((rows, 2 * hd), F32)
    for kr, vr, s0, size in sources:
        s = lax.dot_general(q, kr[s0:s0 + size, :], (((1,), (1,)), ((), ())),
                            preferred_element_type=F32)
        m_new = jnp.maximum(m, jnp.max(s, axis=-1, keepdims=True))
        p = jnp.exp2(s - m_new).astype(BF16)
        v_ones = jnp.concatenate([vr[s0:s0 + size, :], jnp.ones((size, hd), BF16)], axis=1)
        acc = jnp.exp2(m - m_new) * acc + jnp.dot(p, v_ones, preferred_element_type=F32)
        m = m_new
    o = acc[:, :hd] / acc[:, hd:]
    o_ref[...] = jnp.concatenate([o[g * qb:(g + 1) * qb] for g in range(group)], axis=1).astype(BF16)


def _attention(q, k, v, cache, *, seq_len, qb, chunk, n_heads, n_kv):
    n = q.shape[0]
    hd = HEAD_DIM
    group = n_heads // n_kv
    nq = seq_len // qb
    in_specs = [
        pl.BlockSpec((qb, group * hd), lambda b, h, i: (b * nq + i, h)),
        pl.BlockSpec((seq_len, hd), lambda b, h, i: (b, h)),
        pl.BlockSpec((seq_len, hd), lambda b, h, i: (b, h)),
    ]
    args = [q, k, v]
    if cache is not None:
        past = cache[0].shape[1]
        in_specs += [pl.BlockSpec((1, past, hd), lambda b, h, i: (b, 0, h))] * 2
        args += list(cache)
    return pl.pallas_call(
        functools.partial(_attn_kernel, group=group, chunk=chunk, has_cache=cache is not None),
        grid=(n // seq_len, n_kv, nq),
        in_specs=in_specs,
        out_specs=pl.BlockSpec((qb, group * hd), lambda b, h, i: (b * nq + i, h)),
        out_shape=jax.ShapeDtypeStruct((n, n_heads * hd), BF16),
        compiler_params=_params(3, 48),
        name=f"attention_{seq_len}",
    )(*args)


def _final_norm_kernel(x_ref, acc_ref, pmod_ref, gain_ref, o_ref, *, mod_row):
    n_rows, d = x_ref.shape
    m = mod_row(pl.program_id(0))
    x = x_ref[...] + pmod_ref[pl.ds(m, 1), pl.ds(5 * d, d)] * _load_rows(acc_ref, n_rows, d)
    o_ref[...] = (x * lax.rsqrt(jnp.mean(x * x, axis=-1, keepdims=True) + EPS)) * gain_ref[...]


def _final_norm(x, acc, pmod, gain, mod_row):
    n, d = x.shape
    row = lambda i: (i, 0)
    fixed = lambda i: (0, 0)
    return pl.pallas_call(
        functools.partial(_final_norm_kernel, mod_row=mod_row),
        grid=(n // ROW_BLOCK,),
        in_specs=[pl.BlockSpec((ROW_BLOCK, d), row), _rows_spec(acc, d),
                  pl.BlockSpec(pmod.shape, fixed), pl.BlockSpec((1, d), fixed)],
        out_specs=pl.BlockSpec((ROW_BLOCK, d), row),
        out_shape=jax.ShapeDtypeStruct((n, d), F32),
        compiler_params=_params(1, 32),
        name="final_norm",
    )(x, acc, pmod, gain)


def _rope_head_order(x):
    q = HEAD_DIM // 4
    row1, row2, col1, col2 = (x[..., i * q:(i + 1) * q] for i in range(4))
    return jnp.concatenate([row1, col1, row2, col2], axis=-1)


def _rope_tables(seq_len):
    axis = HEAD_DIM // 2
    t = np.arange(seq_len)
    inv = ROPE_BASE ** (-np.arange(axis // 2, dtype=np.float64) * 2.0 / axis)
    ang = np.concatenate([(t // GRID_W)[:, None] * inv, (t % GRID_W)[:, None] * inv], axis=1)
    c, s = np.cos(ang), np.sin(ang)
    return (jnp.asarray(np.concatenate([c, c], axis=1), F32), jnp.asarray(np.concatenate([-s, s], axis=1), F32))


def kernel(x_prompt, x_sample, state_lru, cache_k, cache_v, c, c_ctx, w_mod, b_mod, norm_gain, final_gain,
           w_lru_in, lru_conv_w, lru_conv_b, lru_wa, lru_ba, lru_wx, lru_bx, lru_lambda, w_lru_out,
           w_qkv, q_norm, k_norm, w_attn_out, w_router, w_exp_gate, w_exp_up, w_exp_down):
    batch, seq, d = x_prompt.shape
    dec_batch, dec_seq, _ = x_sample.shape
    depth = w_mod.shape[0]
    n_kv = cache_k.shape[3]
    n_heads = w_attn_out.shape[1] // HEAD_DIM
    heads = dict(n_heads=n_heads, n_kv=n_kv)

    cond = jnp.zeros((SUBLANES, d), F32).at[0].set(c_ctx).at[1:1 + dec_batch].set(c)
    mod_all = _mod_vectors(cond, w_mod, b_mod)

    blocks_per_seq = dec_seq // ROW_BLOCK
    xs = [x_prompt.reshape(batch * seq, d), x_sample.reshape(dec_batch * dec_seq, d)]
    lens = [seq, dec_seq]
    mod_rows = [lambda i: 0, lambda i: 1 + i // blocks_per_seq]
    prevs = [None, None]

    fg = final_gain.reshape(1, d)
    wr_hi, wr_lo = _split_bf16(jnp.swapaxes(w_router, 1, 2))
    new_lru, new_k, new_v = [], [], []
    for l in range(depth):
        mod = mod_all[l]
        gain_a = norm_gain[l, 0].reshape(1, d)
        gain_c = norm_gain[l, 1].reshape(1, d)
        mixed = []
        if l % 2 == 0:
            li = l // 2
            w_in = w_lru_in[li].astype(BF16)
            wg = (0.5 * jnp.concatenate([lru_wa[li, 0], lru_wx[li, 0], lru_wa[li, 1], lru_wx[li, 1]],
                                        axis=2)).astype(BF16)
            bias4 = 0.5 * jnp.stack([lru_ba[li, 0], lru_bx[li, 0], lru_ba[li, 1], lru_bx[li, 1]])
            h0s = [jnp.zeros((batch, 2, w_in.shape[1] // 2), F32), state_lru[:, li]]
            for gi in range(2):
                xs[gi], gate, xr = _lru_in(xs[gi], prevs[gi], mod, gain_a, w_in, mod_rows[gi])
                y, fin = _lru_core(xr, gate, lru_conv_w[li], lru_conv_b[li].reshape(1, -1), wg, bias4,
                                   lru_lambda[li], h0s[gi], seq_len=lens[gi])
                mixed.append(y)
                if gi == 0:
                    new_lru.append(fin.astype(x_prompt.dtype))
            w_out = w_lru_out[li].astype(BF16)
        else:
            ai = l // 2
            w = w_qkv[ai].astype(BF16)
            qg = q_norm[ai].reshape(1, -1)
            kg = k_norm[ai].reshape(1, -1)
            xs[0], q_p, k_p, v_p, kf, vf = _qkv(xs[0], prevs[0], mod, gain_a, w, qg, kg, None,
                                                seq_len=seq, mod_row=mod_rows[0], **heads)
            n_qk = (n_heads + n_kv) * HEAD_DIM
            w_qk = _rope_head_order(w_qkv[ai][:, :n_qk].reshape(d, n_heads + n_kv, HEAD_DIM)).reshape(d, n_qk)
            w_r = jnp.concatenate([w_qk, w_qkv[ai][:, n_qk:]], axis=1).astype(BF16)
            xs[1], q_s, k_s, v_s = _qkv(xs[1], prevs[1], mod, gain_a, w_r, _rope_head_order(qg),
                                        _rope_head_order(kg), _rope_tables(dec_seq),
                                        seq_len=dec_seq, mod_row=mod_rows[1], **heads)
            new_k.append(kf.reshape(batch, seq, n_kv, HEAD_DIM))
            new_v.append(vf.reshape(batch, seq, n_kv, HEAD_DIM))
            past = cache_k.shape[2]
            kc = _rope_head_order(cache_k[:, ai]).reshape(dec_batch, past, n_kv * HEAD_DIM).astype(BF16)
            vc = cache_v[:, ai].reshape(dec_batch, past, n_kv * HEAD_DIM).astype(BF16)
            mixed.append(_attention(q_p, k_p, v_p, None, seq_len=seq, qb=seq, chunk=seq, **heads))
            mixed.append(_attention(q_s, k_s, v_s, (kc, vc), seq_len=dec_seq, qb=256, chunk=256, **heads))
            w_out = w_attn_out[ai].astype(BF16)

        x1_p, h2_p, lg_p = _mix_out(mixed[0], xs[0], mod, gain_c, w_out, wr_hi, wr_lo, l, mod_rows[0], False)
        x1_s, h2_s, lg_s = _mix_out(mixed[1], xs[1], mod, gain_c, w_out, wr_hi, wr_lo, l, mod_rows[1], True)
        idx_p, g_p, idx_s, g_s = _route(lg_p, lg_s, seq, dec_seq)
        xe_p = _gather_onehot(idx_p[:, None, :], h2_p, seq_len=seq)
        off_s = idx_s * _tiles_per_row(d)
        xe_s = _gather_rows(off_s, h2_s, d, seq_len=dec_seq)
        e_n = N_EXPERTS
        zero_rows = ((0, 0), (0, SUBLANES - 1), (0, 0))
        gc_p = jnp.pad(g_p.reshape(batch, e_n, -1).transpose(1, 0, 2).reshape(e_n, 1, -1), zero_rows)
        gc_s = jnp.pad(g_s.reshape(dec_batch, e_n, -1).transpose(1, 0, 2).reshape(e_n, 1, -1), zero_rows)
        ye_p, ye_s = _ffn(xe_p, xe_s, gc_p, gc_s, w_exp_gate, w_exp_up, w_exp_down, l)
        last = l == depth - 1
        x_p = _combine_onehot(idx_p[:, None, :], ye_p, x1_p, mod, fg if last else None, seq_len=seq,
                              mod_row=lambda s: 0)
        acc_s = _combine_rows(off_s, ye_s, seq_len=dec_seq)
        xs = [x_p, x1_s]
        prevs = [None, (acc_s, mod)]

    y_prompt = xs[0].reshape(batch, seq, d)
    y_sample = _final_norm(xs[1], *prevs[1], fg, mod_rows[1]).reshape(dec_batch, dec_seq, d)
    return (y_prompt, y_sample, jnp.stack(new_lru, axis=1), jnp.stack(new_k, axis=1), jnp.stack(new_v, axis=1))
```

```python
import functools
import math

import jax
import jax.numpy as jnp
import numpy as np
from jax import lax
from jax.experimental import pallas as pl
from jax.experimental.pallas import tpu as pltpu

F32 = jnp.float32
BF16 = jnp.bfloat16
I32 = jnp.int32

LANES = 128
SUBLANES = 8
MIB = 1024 * 1024
F32_TINY = float(np.finfo(np.float32).tiny)

RG_C = 8.0
CONV_W = 4
CONV_PAD_L = 2
ROPE_BASE = 10000.0
GRID_W = 64
EPS = 1e-6
N_EXPERTS = 16
CAPACITY_FACTOR = 2
HEAD_DIM = 128
LRU_BLOCKS = 8
LRU_MAX_STEP_BLOCKS = 4
LRU_CORE_VMEM_MIB = 44

ROW_BLOCK = 512


def _params(n_axes, vmem_mib):
    return pltpu.CompilerParams(
        dimension_semantics=("arbitrary",) * n_axes, vmem_limit_bytes=vmem_mib * MIB)


def _split_bf16(x):
    hi = x.astype(BF16)
    lo = (x - hi.astype(F32)).astype(BF16)
    return hi, lo


def _modulated_norm(x, gain, shift, scale):
    y = x * lax.rsqrt(jnp.mean(x * x, axis=-1, keepdims=True) + EPS)
    return (y * gain) * (1.0 + scale) + shift


def _tiles_per_row(d):
    return d // LANES


def _load_rows(ref, n_rows, d):
    if ref.shape == (n_rows, d):
        return ref[...]
    tpr = _tiles_per_row(d)
    return jnp.concatenate([ref[pl.ds(s, n_rows, stride=tpr), :] for s in range(tpr)], axis=1)


def _store_rows(ref, x, r0):
    n_rows, d = x.shape
    if ref.shape[1] == d:
        ref[r0:r0 + n_rows, :] = x.astype(ref.dtype)
        return
    tpr = _tiles_per_row(d)
    for s in range(tpr):
        ref[pl.ds(r0 * tpr + s, n_rows, stride=tpr), :] = x[:, s * LANES:(s + 1) * LANES]


def _rows_spec(arr, d):
    if arr.shape[1] == d:
        return pl.BlockSpec((ROW_BLOCK, d), lambda i: (i, 0))
    return pl.BlockSpec((ROW_BLOCK * _tiles_per_row(d), LANES), lambda i: (i, 0))


def _residual_stream(x_ref, prev, m):
    if prev is None:
        return x_ref[...]
    acc_ref, pmod_ref, xo_ref = prev
    n_rows, d = x_ref.shape
    x = x_ref[...] + pmod_ref[pl.ds(m, 1), pl.ds(5 * d, d)] * _load_rows(acc_ref, n_rows, d)
    xo_ref[...] = x
    return x


def _mod_kernel(cond_ref, w_ref, b_ref, o_ref):
    c = cond_ref[...]
    a_hi, a_lo = _split_bf16(c * jax.nn.sigmoid(c))
    w_hi, w_lo = _split_bf16(w_ref[0])
    dot = functools.partial(jnp.dot, preferred_element_type=F32)
    o_ref[0] = dot(a_hi, w_hi) + (dot(a_lo, w_hi) + dot(a_hi, w_lo)) + b_ref[0]


def _mod_vectors(cond8, w_mod, b_mod):
    depth, d, d6 = w_mod.shape
    nb = 1536
    return pl.pallas_call(
        _mod_kernel,
        grid=(depth, d6 // nb),
        in_specs=[
            pl.BlockSpec((SUBLANES, d), lambda l, j: (0, 0)),
            pl.BlockSpec((1, d, nb), lambda l, j: (l, 0, j)),
            pl.BlockSpec((1, 1, nb), lambda l, j: (l, 0, j)),
        ],
        out_specs=pl.BlockSpec((1, SUBLANES, nb), lambda l, j: (l, 0, j)),
        out_shape=jax.ShapeDtypeStruct((depth, SUBLANES, d6), F32),
        compiler_params=_params(2, 40),
        name="adaln_mod",
    )(cond8, w_mod, b_mod.reshape(depth, 1, d6))


def _lru_in_kernel(*refs, mod_row, has_prev):
    if has_prev:
        x_ref, acc_ref, pmod_ref, mod_ref, gain_ref, w_ref, xo_ref, gate_ref, xr_ref = refs
        prev = (acc_ref, pmod_ref, xo_ref)
    else:
        x_ref, mod_ref, gain_ref, w_ref, gate_ref, xr_ref = refs
        prev = None
    d = x_ref.shape[1]
    m = mod_row(pl.program_id(0))
    x = _residual_stream(x_ref, prev, m)
    shift = mod_ref[pl.ds(m, 1), pl.ds(0, d)]
    scale = mod_ref[pl.ds(m, 1), pl.ds(d, d)]
    h = _modulated_norm(x, gain_ref[...], shift, scale).astype(BF16)
    u = jnp.dot(h, w_ref[...], preferred_element_type=F32)
    w = gate_ref.shape[1]
    gate_ref[...] = u[:, :w]
    xr_ref[...] = u[:, w:]


def _lru_in(x, prev, mod, gain, w_in, mod_row):
    n, d = x.shape
    w2 = w_in.shape[1]
    w = w2 // 2
    row = lambda i: (i, 0)
    fixed = lambda i: (0, 0)
    in_specs = [pl.BlockSpec((ROW_BLOCK, d), row)]
    args = [x]
    out_specs = [pl.BlockSpec((ROW_BLOCK, w), row), pl.BlockSpec((ROW_BLOCK, w), row)]
    out_shape = [jax.ShapeDtypeStruct((n, w), F32), jax.ShapeDtypeStruct((n, w), F32)]
    if prev is not None:
        in_specs += [_rows_spec(prev[0], d), pl.BlockSpec(prev[1].shape, fixed)]
        args += list(prev)
        out_specs = [pl.BlockSpec((ROW_BLOCK, d), row)] + out_specs
        out_shape = [jax.ShapeDtypeStruct((n, d), F32)] + out_shape
    in_specs += [pl.BlockSpec(mod.shape, fixed), pl.BlockSpec((1, d), fixed), pl.BlockSpec((d, w2), fixed)]
    args += [mod, gain, w_in]
    outs = pl.pallas_call(
        functools.partial(_lru_in_kernel, mod_row=mod_row, has_prev=prev is not None),
        grid=(n // ROW_BLOCK,),
        in_specs=in_specs,
        out_specs=out_specs,
        out_shape=out_shape,
        compiler_params=_params(1, 48),
        name="lru_in",
    )(*args)
    return outs if prev is not None else [x] + list(outs)


def _lru_core_kernel(xr_ref, gate_ref, cw_ref, cb_ref, wg_ref, bias_ref, lam_ref, h0_ref,
                     y_ref, fin_ref, pad, a_f, b_f, a_b, b_b):
    t_len, lb = xr_ref.shape
    ch = 256
    halo = SUBLANES
    seg = t_len // SUBLANES
    piece = min(ch, seg)
    seg_shift = seg.bit_length() - 1

    def seg_rows(t0):
        r = lax.shift_right_logical(t0, seg_shift)
        return pl.ds((t0 - r * seg) * SUBLANES + r, piece, stride=SUBLANES)

    zero_rows = jnp.zeros((halo, lb), F32)
    pad[0:halo, :] = zero_rows
    pad[t_len + halo:t_len + 2 * halo, :] = zero_rows

    def copy_in(c, carry):
        r = pl.multiple_of(c * ch, ch)
        pad[pl.ds(r + halo, ch), :] = xr_ref[pl.ds(r, ch), :]
        return carry

    lax.fori_loop(0, t_len // ch, copy_in, 0)

    neg_lam = -lam_ref[...]
    softplus = jnp.maximum(neg_lam, 0.0) + jnp.log1p(jnp.exp(-jnp.abs(neg_lam)))
    c_nla = (0.5 * RG_C) * softplus
    c_exp2 = (-0.5 * RG_C * math.log2(math.e)) * softplus
    cw = cw_ref[...]
    cbias = cb_ref[...]
    half_bias = bias_ref[...]
    half_wg = wg_ref[...]
    bw = half_wg.shape[1]
    n_slab = lb // LANES

    def gates(c, carry):
        r = pl.multiple_of(c * ch, ch)
        blk = pad[pl.ds(r, ch + 2 * halo), :]
        xc = cbias
        for k in range(CONV_W):
            o = halo - CONV_PAD_L + k
            xc = xc + blk[o:o + ch] * cw[k:k + 1]
        xc_bf = xc.astype(BF16)
        g = [jnp.dot(xc_bf[:, j * bw:(j + 1) * bw], half_wg[j], preferred_element_type=F32)
             for j in range(lb // bw)]
        pre = lambda k: jnp.concatenate([gj[:, k * bw:(k + 1) * bw] for gj in g], axis=1) + half_bias[k:k + 1]
        half_xc = 0.5 * xc
        for d, (a_ref, b_ref) in enumerate(((a_f, b_f), (a_b, b_b))):
            u = jnp.tanh(pre(2 * d)) + 1.0
            t_i = jnp.tanh(pre(2 * d + 1))
            a = jnp.exp2(u * c_exp2[d:d + 1])
            q = jnp.tanh(u * c_nla[d:d + 1]) * (1.0 + a * a)
            b = (q * lax.rsqrt(jnp.maximum(q, F32_TINY))) * (t_i * half_xc + half_xc)
            for p in range(ch // piece):
                rows = seg_rows(r + p * piece)
                for j in range(n_slab):
                    a_ref[j, rows, :] = a[p * piece:(p + 1) * piece, j * LANES:(j + 1) * LANES]
                    b_ref[j, rows, :] = b[p * piece:(p + 1) * piece, j * LANES:(j + 1) * LANES]
        return carry

    lax.fori_loop(0, t_len // ch, gates, 0)

    def scan(i, carry):
        rows = (pl.ds(pl.multiple_of(i * SUBLANES, SUBLANES), SUBLANES),
                pl.ds(pl.multiple_of((seg - 1 - i) * SUBLANES, SUBLANES), SUBLANES))
        out = []
        for k, (h, prod) in enumerate(carry):
            d, j = divmod(k, n_slab)
            a_ref, b_ref = ((a_f, b_f), (a_b, b_b))[d]
            a = a_ref[j, rows[d], :]
            h = a * h + b_ref[j, rows[d], :]
            prod = a * prod
            b_ref[j, rows[d], :] = h
            a_ref[j, rows[d], :] = prod
            out.append((h, prod))
        return tuple(out)

    start = (jnp.zeros((SUBLANES, LANES), F32), jnp.ones((SUBLANES, LANES), F32))
    ends = lax.fori_loop(0, seg, scan, (start,) * (2 * n_slab), unroll=8)

    enter = []
    for k, (h, prod) in enumerate(ends):
        d, j = divmod(k, n_slab)
        states = [h0_ref[0, d:d + 1, j * LANES:(j + 1) * LANES]]
        for r in (range(SUBLANES) if d == 0 else reversed(range(SUBLANES))):
            states.append(h[r:r + 1] + prod[r:r + 1] * states[-1])
        fin_ref[0, d:d + 1, j * LANES:(j + 1) * LANES] = states[SUBLANES]
        order = states[:SUBLANES] if d == 0 else states[:SUBLANES][::-1]
        enter.append(jnp.concatenate(order, axis=0))

    def fix(i, carry):
        rows = pl.ds(pl.multiple_of(i * SUBLANES, SUBLANES), SUBLANES)
        for j in range(n_slab):
            b_f[j, rows, :] = ((b_f[j, rows, :] + a_f[j, rows, :] * enter[j])
                               + (b_b[j, rows, :] + a_b[j, rows, :] * enter[n_slab + j]))
        return carry

    lax.fori_loop(0, seg, fix, 0, unroll=8)

    def emit(c, carry):
        r = pl.multiple_of(c * ch, ch)
        gt = gate_ref[pl.ds(r, ch), :]
        cdf = 0.5 * (1.0 + jnp.tanh(math.sqrt(2.0 / math.pi) * (gt + 0.044715 * (gt * gt * gt))))
        y = gt * cdf
        for p in range(ch // piece):
            rows = seg_rows(r + p * piece)
            total = jnp.concatenate([b_f[j, rows, :] for j in range(n_slab)], axis=1)
            y_ref[pl.ds(r + p * piece, piece), :] = (total * y[p * piece:(p + 1) * piece]).astype(BF16)
        return carry

    lax.fori_loop(0, t_len // ch, emit, 0)


def _lru_core(xr, gate, conv_w, conv_b, wg, bias4, lam, h0, *, seq_len):
    n, w = xr.shape
    n_seq = n // seq_len
    bw = w // LRU_BLOCKS
    block_bytes = seq_len * bw * (5 * 4 + 2 * 2 * 4 + 2 * 2)
    step_blocks = LRU_MAX_STEP_BLOCKS
    while step_blocks * block_bytes > LRU_CORE_VMEM_MIB * MIB:
        step_blocks //= 2
    lb = step_blocks * bw
    tok = lambda s, c: (s, c)
    chan = lambda s, c: (0, c)
    return pl.pallas_call(
        _lru_core_kernel,
        grid=(n_seq, LRU_BLOCKS // step_blocks),
        in_specs=[
            pl.BlockSpec((seq_len, lb), tok),
            pl.BlockSpec((seq_len, lb), tok),
            pl.BlockSpec((CONV_W, lb), chan),
            pl.BlockSpec((1, lb), chan),
            pl.BlockSpec((step_blocks, bw, 4 * bw), lambda s, c: (c, 0, 0)),
            pl.BlockSpec((4, lb), chan),
            pl.BlockSpec((2, lb), chan),
            pl.BlockSpec((1, 2, lb), lambda s, c: (s, 0, c)),
        ],
        out_specs=[pl.BlockSpec((seq_len, lb), tok),
                   pl.BlockSpec((1, 2, lb), lambda s, c: (s, 0, c))],
        out_shape=[jax.ShapeDtypeStruct((n, w), BF16), jax.ShapeDtypeStruct((n_seq, 2, w), F32)],
        scratch_shapes=[pltpu.VMEM((seq_len + 2 * SUBLANES, lb), F32)]
        + [pltpu.VMEM((lb // LANES, seq_len, LANES), F32)] * 4,
        compiler_params=_params(2, LRU_CORE_VMEM_MIB + 4),
        name=f"lru_core_{seq_len}",
    )(xr, gate, conv_w, conv_b, wg, bias4, lam, h0)


def _mix_out_kernel(a_ref, x_ref, mod_ref, gain_ref, w_ref, wrh_ref, wrl_ref,
                    x1_ref, h2_ref, lg_ref, *, mod_row):
    d = x_ref.shape[1]
    m = mod_row(pl.program_id(0))
    g_mix = mod_ref[pl.ds(m, 1), pl.ds(2 * d, d)]
    shift = mod_ref[pl.ds(m, 1), pl.ds(3 * d, d)]
    scale = mod_ref[pl.ds(m, 1), pl.ds(4 * d, d)]
    op = jnp.dot(a_ref[...], w_ref[...], preferred_element_type=F32)
    x1 = x_ref[...] + g_mix * op
    x1_ref[...] = x1
    h2 = _modulated_norm(x1, gain_ref[...], shift, scale)
    h_hi, h_lo = _split_bf16(h2)
    _store_rows(h2_ref, h2, 0)
    nt = functools.partial(lax.dot_general, dimension_numbers=(((1,), (1,)), ((), ())),
                           preferred_element_type=F32)
    wr_hi = wrh_ref[0]
    lg_ref[...] = nt(wr_hi, h_hi) + (nt(wrl_ref[0], h_hi) + nt(wr_hi, h_lo))


def _mix_out(a, x, mod, gain, w_out, wr_hi, wr_lo, layer, mod_row, token_tiled):
    n, d = x.shape
    e = wr_hi.shape[1]
    row = lambda i: (i, 0)
    fixed = lambda i: (0, 0)
    router = pl.BlockSpec((1,) + wr_hi.shape[1:], lambda i: (layer, 0, 0))
    tpr = _tiles_per_row(d)
    if token_tiled:
        h2_spec = pl.BlockSpec((ROW_BLOCK * tpr, LANES), row)
        h2_shape = jax.ShapeDtypeStruct((n * tpr, LANES), F32)
    else:
        h2_spec = pl.BlockSpec((ROW_BLOCK, d), row)
        h2_shape = jax.ShapeDtypeStruct((n, d), BF16)
    return pl.pallas_call(
        functools.partial(_mix_out_kernel, mod_row=mod_row),
        grid=(n // ROW_BLOCK,),
        in_specs=[
            pl.BlockSpec((ROW_BLOCK, a.shape[1]), row),
            pl.BlockSpec((ROW_BLOCK, d), row),
            pl.BlockSpec(mod.shape, fixed),
            pl.BlockSpec((1, d), fixed),
            pl.BlockSpec(w_out.shape, fixed),
            router,
            router,
        ],
        out_specs=[pl.BlockSpec((ROW_BLOCK, d), row), h2_spec,
                   pl.BlockSpec((e, ROW_BLOCK), lambda i: (0, i))],
        out_shape=[jax.ShapeDtypeStruct((n, d), F32), h2_shape, jax.ShapeDtypeStruct((e, n), F32)],
        compiler_params=_params(1, 48),
        name="mix_out",
    )(a, x, mod, gain, w_out, wr_hi, wr_lo)


def _lane_sum(tiles):
    acc = tiles[0]
    for t in tiles[1:]:
        acc = acc + t
    return jnp.sum(acc, axis=1, keepdims=True)


def _exclusive_cumsum(flags, upper):
    out = []
    off = jnp.zeros((flags[0].shape[0], 1), F32)
    for f in flags:
        out.append(jnp.dot(f.astype(BF16), upper, preferred_element_type=F32) + off)
        off = off + jnp.sum(f, axis=1, keepdims=True)
    return out


def _route_group(lg_ref, n_seq, t_len, idx_ref, g_ref):
    cap = CAPACITY_FACTOR * t_len // N_EXPERTS
    affs = []
    for s in range(n_seq):
        lg = lg_ref[:, s * t_len:(s + 1) * t_len]
        ex = jnp.exp(lg - jnp.max(lg, axis=0, keepdims=True))
        affs.append(ex / jnp.sum(ex, axis=0, keepdims=True))
    aff = jnp.concatenate(affs, axis=0)
    n_rows = aff.shape[0]
    nt = t_len // LANES
    g = [aff[:, j * LANES:(j + 1) * LANES] for j in range(nt)]

    kth_bits = jnp.zeros((n_rows, 1), I32)
    for bit in range(30, -1, -1):
        cand = kth_bits | (1 << bit)
        cand_f = pltpu.bitcast(cand, F32)
        cnt = _lane_sum([jnp.where(t >= cand_f, 1, 0) for t in g])
        kth_bits = jnp.where(cnt >= cap, cand, kth_bits)
    kth = pltpu.bitcast(kth_bits, F32)

    lane = lax.broadcasted_iota(I32, (LANES, LANES), 0)
    upper = jnp.where(lane < lax.broadcasted_iota(I32, (LANES, LANES), 1), 1.0, 0.0).astype(BF16)
    gt = [t > kth for t in g]
    eq = [t == kth for t in g]
    need = (cap - _lane_sum([jnp.where(m, 1, 0) for m in gt])).astype(F32)
    eq_rank = _exclusive_cumsum([jnp.where(m, 1.0, 0.0) for m in eq], upper)
    sel = [jnp.logical_or(gt[j], jnp.logical_and(eq[j], eq_rank[j] < need)) for j in range(nt)]
    pos = _exclusive_cumsum([jnp.where(m, 1.0, 0.0) for m in sel], upper)

    lane_r = lax.broadcasted_iota(I32, (n_rows, LANES), 1)
    d = [jnp.where(sel[j], lane_r + j * LANES - pos[j].astype(I32), -1) for j in range(nt)]
    for k in range(t_len.bit_length() - 1):
        s = 1 << k
        if s < LANES:
            d_rot = [pltpu.roll(x, LANES - s, 1) for x in d]
            g_rot = [pltpu.roll(x, LANES - s, 1) for x in g]
            same = lane_r < LANES - s
            d_in = [jnp.where(same, d_rot[j], d_rot[(j + 1) % nt]) for j in range(nt)]
            g_in = [jnp.where(same, g_rot[j], g_rot[(j + 1) % nt]) for j in range(nt)]
        else:
            q = s // LANES
            d_in = [d[(j + q) % nt] for j in range(nt)]
            g_in = [g[(j + q) % nt] for j in range(nt)]
        new_d, new_g = [], []
        for j in range(nt):
            move = jnp.logical_and(d_in[j] >= 0, ((d_in[j] >> k) & 1) == 1)
            stay = jnp.logical_and(d[j] >= 0, ((d[j] >> k) & 1) == 0)
            new_d.append(jnp.where(move, d_in[j], jnp.where(stay, d[j], -1)))
            new_g.append(jnp.where(move, g_in[j], g[j]))
        d, g = new_d, new_g

    for j in range(idx_ref.shape[1] // LANES):
        idx_ref[:, j * LANES:(j + 1) * LANES] = lane_r + j * LANES + d[j]
        g_ref[:, j * LANES:(j + 1) * LANES] = g[j]


def _route_kernel(lgp_ref, lgs_ref, idx_p_ref, g_p_ref, idx_s_ref, g_s_ref, *, p_len, s_len):
    _route_group(lgp_ref, lgp_ref.shape[1] // p_len, p_len, idx_p_ref, g_p_ref)
    _route_group(lgs_ref, lgs_ref.shape[1] // s_len, s_len, idx_s_ref, g_s_ref)


def _route(lg_p, lg_s, p_len, s_len):
    e = lg_p.shape[0]
    np_seq = lg_p.shape[1] // p_len
    ns_seq = lg_s.shape[1] // s_len
    cap_p = CAPACITY_FACTOR * p_len // N_EXPERTS
    cap_s = CAPACITY_FACTOR * s_len // N_EXPERTS
    wp = max(cap_p, LANES)
    ws = max(cap_s, LANES)
    idx_p, g_p, idx_s, g_s = pl.pallas_call(
        functools.partial(_route_kernel, p_len=p_len, s_len=s_len),
        out_shape=[jax.ShapeDtypeStruct((np_seq * e, wp), I32),
                   jax.ShapeDtypeStruct((np_seq * e, wp), F32),
                   jax.ShapeDtypeStruct((ns_seq * e, ws), I32),
                   jax.ShapeDtypeStruct((ns_seq * e, ws), F32)],
        compiler_params=pltpu.CompilerParams(vmem_limit_bytes=40 * MIB),
        name="route",
    )(lg_p, lg_s)
    return (idx_p[:, :cap_p].reshape(np_seq, e * cap_p), g_p[:, :cap_p].reshape(np_seq, e * cap_p),
            idx_s[:, :cap_s].reshape(-1), g_s[:, :cap_s].reshape(-1))


def _gather_onehot_kernel(idx_ref, h_ref, o_ref):
    t_len = h_ref.shape[0]
    n_exp, cap, _ = o_ref.shape
    idx = idx_ref[0]
    hit = jnp.where(idx == lax.broadcasted_iota(I32, (t_len, idx.shape[1]), 0), 1.0, 0.0).astype(BF16)
    xs = lax.dot_general(hit, h_ref[...], (((0,), (0,)), ((), ())), preferred_element_type=F32).astype(BF16)
    for e in range(n_exp):
        o_ref[e] = xs[e * cap:(e + 1) * cap]


def _gather_onehot(idx_row, h, *, seq_len):
    n, d = h.shape
    n_seq, _, slots = idx_row.shape
    cap = slots // N_EXPERTS
    return pl.pallas_call(
        _gather_onehot_kernel,
        grid=(n_seq,),
        in_specs=[pl.BlockSpec((1, 1, slots), lambda s: (s, 0, 0)),
                  pl.BlockSpec((seq_len, d), lambda s: (s, 0))],
        out_specs=pl.BlockSpec((N_EXPERTS, cap, d), lambda s: (0, s, 0)),
        out_shape=jax.ShapeDtypeStruct((N_EXPERTS, n_seq * cap, d), BF16),
        compiler_params=_params(1, 32),
        name="gather_onehot",
    )(idx_row, h)


def _gather_rows_kernel(idx_ref, h_ref, o_ref, tiles, *, cap, n_exp):
    tpr = _tiles_per_row(o_ref.shape[2])
    base = (pl.program_id(0) * n_exp + pl.program_id(1)) * cap

    def body(c, carry):
        src = pl.multiple_of(idx_ref[base + c], tpr)
        tiles[pl.ds(pl.multiple_of(c * tpr, tpr), tpr), :] = h_ref[pl.ds(src, tpr), :]
        return carry

    lax.fori_loop(0, cap, body, 0, unroll=8)
    for s in range(tpr):
        o_ref[0, :, s * LANES:(s + 1) * LANES] = tiles[pl.ds(s, cap, stride=tpr), :].astype(BF16)


def _gather_rows(idx, h, d, *, seq_len):
    tpr = _tiles_per_row(d)
    n_seq = h.shape[0] // (seq_len * tpr)
    cap = CAPACITY_FACTOR * seq_len // N_EXPERTS
    return pl.pallas_call(
        functools.partial(_gather_rows_kernel, cap=cap, n_exp=N_EXPERTS),
        grid_spec=pltpu.PrefetchScalarGridSpec(
            num_scalar_prefetch=1,
            grid=(n_seq, N_EXPERTS),
            in_specs=[pl.BlockSpec((seq_len * tpr, LANES), lambda s, e, idx: (s, 0))],
            out_specs=pl.BlockSpec((1, cap, d), lambda s, e, idx: (e, s, 0)),
            scratch_shapes=[pltpu.VMEM((cap * tpr, LANES), F32)],
        ),
        out_shape=jax.ShapeDtypeStruct((N_EXPERTS, n_seq * cap, d), BF16),
        compiler_params=_params(2, 48),
        name="gather_rows",
    )(idx, h)


def _ffn_kernel(xp_ref, xs_ref, gp_ref, gs_ref, wg_ref, wu_ref, wd_ref, yp_ref, ys_ref):
    rc = 512
    f = pl.program_id(1)
    last = f == pl.num_programs(1) - 1

    def body(first):
        wg = wg_ref[0, 0].astype(BF16)
        wu = wu_ref[0, 0].astype(BF16)
        wd = wd_ref[0, 0].astype(BF16)
        for x_ref, g_ref, y_ref in ((xp_ref, gp_ref, yp_ref), (xs_ref, gs_ref, ys_ref)):
            for r in range(0, x_ref.shape[1], rc):
                x = x_ref[0, r:r + rc, :]
                hg = jnp.dot(x, wg, preferred_element_type=F32)
                hu = jnp.dot(x, wu, preferred_element_type=F32)
                hid = ((hg * jax.nn.sigmoid(hg)) * hu).astype(BF16)
                g_hi, g_lo = _split_bf16(g_ref[0, :, r:r + rc])
                ones = jnp.ones((g_hi.shape[0], LANES), BF16)
                tn = functools.partial(lax.dot_general, dimension_numbers=(((0,), (0,)), ((), ())),
                                       preferred_element_type=F32)
                weight = jnp.where(last, tn(g_hi, ones) + tn(g_lo, ones), 1.0)
                weight = jnp.concatenate([weight] * (y_ref.shape[2] // LANES), axis=1)
                y = jnp.dot(hid, wd, preferred_element_type=F32)
                if not first:
                    y = y_ref[0, r:r + rc, :] + y
                y_ref[0, r:r + rc, :] = y * weight

    pl.when(f == 0)(functools.partial(body, True))
    pl.when(f != 0)(functools.partial(body, False))


def _ffn(xs_p, xs_s, g_p, g_s, w_gate, w_up, w_down, layer):
    n_exp, rp, d = xs_p.shape
    rs = xs_s.shape[1]
    ff = w_gate.shape[3]
    fc = 1024
    return pl.pallas_call(
        _ffn_kernel,
        grid=(n_exp, ff // fc),
        in_specs=[
            pl.BlockSpec((1, rp, d), lambda e, f: (e, 0, 0)),
            pl.BlockSpec((1, rs, d), lambda e, f: (e, 0, 0)),
            pl.BlockSpec((1, SUBLANES, rp), lambda e, f: (e, 0, 0)),
            pl.BlockSpec((1, SUBLANES, rs), lambda e, f: (e, 0, 0)),
            pl.BlockSpec((1, 1, d, fc), lambda e, f: (layer, e, 0, f)),
            pl.BlockSpec((1, 1, d, fc), lambda e, f: (layer, e, 0, f)),
            pl.BlockSpec((1, 1, fc, d), lambda e, f: (layer, e, f, 0)),
        ],
        out_specs=[pl.BlockSpec((1, rp, d), lambda e, f: (e, 0, 0)),
                   pl.BlockSpec((1, rs, d), lambda e, f: (e, 0, 0))],
        out_shape=[jax.ShapeDtypeStruct((n_exp, rp, d), F32),
                   jax.ShapeDtypeStruct((n_exp, rs, d), F32)],
        compiler_params=_params(2, 58),
        name="expert_ffn",
    )(xs_p, xs_s, g_p, g_s, w_gate, w_up, w_down)


def _combine_onehot_kernel(*refs, mod_row, final):
    if final:
        idx_ref, ye_ref, x_ref, pmod_ref, gain_ref, o_ref = refs
    else:
        idx_ref, ye_ref, x_ref, pmod_ref, o_ref = refs
    t_len, d = o_ref.shape
    n_exp = ye_ref.shape[0]
    idx = idx_ref[0]
    hit = jnp.where(idx == lax.broadcasted_iota(I32, (t_len, idx.shape[1]), 0), 1.0, 0.0).astype(BF16)
    y_hi, y_lo = _split_bf16(jnp.concatenate([ye_ref[e] for e in range(n_exp)], axis=0))
    acc = jnp.dot(hit, y_hi, preferred_element_type=F32) + jnp.dot(hit, y_lo, preferred_element_type=F32)
    x = x_ref[...] + pmod_ref[pl.ds(mod_row(pl.program_id(0)), 1), pl.ds(5 * d, d)] * acc
    if final:
        x = (x * lax.rsqrt(jnp.mean(x * x, axis=-1, keepdims=True) + EPS)) * gain_ref[...]
    o_ref[...] = x


def _combine_onehot(idx_row, ye, x, pmod, final_gain, *, seq_len, mod_row):
    n_exp, rows, d = ye.shape
    n_seq, _, slots = idx_row.shape
    cap = slots // n_exp
    in_specs = [pl.BlockSpec((1, 1, slots), lambda s: (s, 0, 0)),
                pl.BlockSpec((n_exp, cap, d), lambda s: (0, s, 0)),
                pl.BlockSpec((seq_len, d), lambda s: (s, 0)),
                pl.BlockSpec(pmod.shape, lambda s: (0, 0))]
    args = [idx_row, ye, x, pmod]
    if final_gain is not None:
        in_specs.append(pl.BlockSpec((1, d), lambda s: (0, 0)))
        args.append(final_gain)
    return pl.pallas_call(
        functools.partial(_combine_onehot_kernel, mod_row=mod_row, final=final_gain is not None),
        grid=(n_seq,),
        in_specs=in_specs,
        out_specs=pl.BlockSpec((seq_len, d), lambda s: (s, 0)),
        out_shape=jax.ShapeDtypeStruct((n_seq * seq_len, d), F32),
        compiler_params=_params(1, 32),
        name="combine_onehot",
    )(*args)


COMBINE_GROUP = 16


def _combine_rows_kernel(idx_ref, ye_ref, o_ref, tiles, *, cap, n_exp):
    e = pl.program_id(1)
    d = ye_ref.shape[2]
    tpr = _tiles_per_row(d)
    ch = 2048

    @pl.when(e == 0)
    def _():
        def zero(c, carry):
            o_ref[pl.ds(pl.multiple_of(c * ch, ch), ch), :] = jnp.zeros((ch, LANES), F32)
            return carry
        lax.fori_loop(0, o_ref.shape[0] // ch, zero, 0)

    for s in range(tpr):
        tiles[pl.ds(s, cap, stride=tpr), :] = ye_ref[0, :, s * LANES:(s + 1) * LANES]

    base = (pl.program_id(0) * n_exp + e) * cap

    def body(i, carry):
        c0 = i * COMBINE_GROUP
        dst = [pl.ds(pl.multiple_of(idx_ref[base + c0 + k], tpr), tpr) for k in range(COMBINE_GROUP)]
        vals = [o_ref[dst[k], :] + tiles[pl.ds(pl.multiple_of((c0 + k) * tpr, tpr), tpr), :]
                for k in range(COMBINE_GROUP)]
        for k in range(COMBINE_GROUP):
            o_ref[dst[k], :] = vals[k]
        return carry

    lax.fori_loop(0, cap // COMBINE_GROUP, body, 0)


def _combine_rows(idx, ye, *, seq_len):
    n_exp, rows, d = ye.shape
    tpr = _tiles_per_row(d)
    cap = CAPACITY_FACTOR * seq_len // N_EXPERTS
    n_seq = rows // cap
    return pl.pallas_call(
        functools.partial(_combine_rows_kernel, cap=cap, n_exp=n_exp),
        grid_spec=pltpu.PrefetchScalarGridSpec(
            num_scalar_prefetch=1,
            grid=(n_seq, n_exp),
            in_specs=[pl.BlockSpec((1, cap, d), lambda s, e, i: (e, s, 0))],
            out_specs=pl.BlockSpec((seq_len * tpr, LANES), lambda s, e, i: (s, 0)),
            scratch_shapes=[pltpu.VMEM((cap * tpr, LANES), F32)],
        ),
        out_shape=jax.ShapeDtypeStruct((n_seq * seq_len * tpr, LANES), F32),
        compiler_params=_params(2, 48),
        name="combine_rows",
    )(idx, ye)


def _head_norm(x, gain):
    return x * lax.rsqrt(jnp.mean(x * x, axis=-1, keepdims=True) + EPS) * gain


def _qkv_kernel(*refs, rope, has_prev, mod_row, n_heads, n_kv):
    refs = list(refs)
    x_ref = refs.pop(0)
    prev = None
    if has_prev:
        acc_ref, pmod_ref = refs.pop(0), refs.pop(0)
    mod_ref, gain_ref, w_ref, qg_ref, kg_ref = refs[:5]
    refs = refs[5:]
    if rope:
        cos_ref, sin_ref = refs.pop(0), refs.pop(0)
    if has_prev:
        prev = (acc_ref, pmod_ref, refs.pop(0))
    q_ref, k_ref, v_ref = refs[:3]
    d = x_ref.shape[1]
    hd = HEAD_DIM
    m = mod_row(pl.program_id(0))
    x = _residual_stream(x_ref, prev, m)
    shift = mod_ref[pl.ds(m, 1), pl.ds(0, d)]
    scale = mod_ref[pl.ds(m, 1), pl.ds(d, d)]
    h = _modulated_norm(x, gain_ref[...], shift, scale).astype(BF16)
    qg = qg_ref[...]
    kg = kg_ref[...]
    q_scale = math.log2(math.e) * hd ** -0.5
    half = x_ref.shape[0] // 2
    for r0 in (0, half):
        rows = slice(r0, r0 + half)
        qkv = jnp.dot(h[rows], w_ref[...], preferred_element_type=F32)
        if rope:
            cos = cos_ref[rows, :]
            sin = sin_ref[rows, :]

            def rot(xh, cos=cos, sin=sin):
                return xh * cos + pltpu.roll(xh, hd // 2, 1) * sin
        else:
            rot = lambda xh: xh

        for i in range(n_heads):
            qh = rot(_head_norm(qkv[:, i * hd:(i + 1) * hd], qg)) * q_scale
            q_ref[rows, i * hd:(i + 1) * hd] = qh.astype(BF16)
        for i in range(n_kv):
            c0 = (n_heads + i) * hd
            kh = _head_norm(qkv[:, c0:c0 + hd], kg)
            if not rope:
                refs[3][rows, i * hd:(i + 1) * hd] = kh
            k_ref[rows, i * hd:(i + 1) * hd] = rot(kh).astype(BF16)
        v = qkv[:, (n_heads + n_kv) * hd:]
        v_ref[rows, :] = v.astype(BF16)
        if not rope:
            refs[4][rows, :] = v


def _qkv(x, prev, mod, gain, w_qkv, q_gain, k_gain, tables, *, seq_len, mod_row, n_heads, n_kv):
    n, d = x.shape
    hd = HEAD_DIM
    row = lambda i: (i, 0)
    fixed = lambda i: (0, 0)
    rope = tables is not None
    in_specs = [pl.BlockSpec((ROW_BLOCK, d), row)]
    args = [x]
    if prev is not None:
        in_specs += [_rows_spec(prev[0], d), pl.BlockSpec(prev[1].shape, fixed)]
        args += list(prev)
    in_specs += [
        pl.BlockSpec(mod.shape, fixed),
        pl.BlockSpec((1, d), fixed),
        pl.BlockSpec(w_qkv.shape, fixed),
        pl.BlockSpec((1, hd), fixed),
        pl.BlockSpec((1, hd), fixed),
    ]
    args += [mod, gain, w_qkv, q_gain, k_gain]
    if rope:
        per_seq = seq_len // ROW_BLOCK
        in_specs += [pl.BlockSpec((ROW_BLOCK, hd), lambda i: (i % per_seq, 0))] * 2
        args += list(tables)
    out_specs, out_shape = [], []
    if prev is not None:
        out_specs.append(pl.BlockSpec((ROW_BLOCK, d), row))
        out_shape.append(jax.ShapeDtypeStruct((n, d), F32))
    out_specs += [pl.BlockSpec((ROW_BLOCK, n_heads * hd), row),
                  pl.BlockSpec((ROW_BLOCK, n_kv * hd), row),
                  pl.BlockSpec((ROW_BLOCK, n_kv * hd), row)]
    out_shape += [jax.ShapeDtypeStruct((n, n_heads * hd), BF16),
                  jax.ShapeDtypeStruct((n, n_kv * hd), BF16),
                  jax.ShapeDtypeStruct((n, n_kv * hd), BF16)]
    if not rope:
        out_specs += [pl.BlockSpec((ROW_BLOCK, n_kv * hd), row)] * 2
        out_shape += [jax.ShapeDtypeStruct((n, n_kv * hd), F32)] * 2
    outs = pl.pallas_call(
        functools.partial(_qkv_kernel, rope=rope, has_prev=prev is not None, mod_row=mod_row,
                          n_heads=n_heads, n_kv=n_kv),
        grid=(n // ROW_BLOCK,),
        in_specs=in_specs,
        out_specs=out_specs,
        out_shape=out_shape,
        compiler_params=_params(1, 48),
        name="qkv_rope" if rope else "qkv",
    )(*args)
    return list(outs) if prev is not None else [x] + list(outs)


def _attn_kernel(*refs, group, chunk, has_cache):
    if has_cache:
        q_ref, k_ref, v_ref, kc_ref, vc_ref, o_ref = refs
    else:
        q_ref, k_ref, v_ref, o_ref = refs
    hd = HEAD_DIM
    qb = q_ref.shape[0]
    rows = group * qb
    q = jnp.concatenate([q_ref[:, g * hd:(g + 1) * hd] for g in range(group)], axis=0)
    sources = [(k_ref, v_ref, s0, chunk) for s0 in range(0, k_ref.shape[0], chunk)]
    if has_cache:
        sources.append((kc_ref.at[0], vc_ref.at[0], 0, kc_ref.shape[1]))
    m = jnp.full((rows, 1), -jnp.inf, F32)
    acc = jnp.zeros((rows, 2 * hd), F32)
    for kr, vr, s0, size in sources:
        s = lax.dot_general(q, kr[s0:s0 + size, :], (((1,), (1,)), ((), ())),
                            preferred_element_type=F32)
        m_new = jnp.maximum(m, jnp.max(s, axis=-1, keepdims=True))
        p = jnp.exp2(s - m_new).astype(BF16)
        v_ones = jnp.concatenate([vr[s0:s0 + size, :], jnp.ones((size, hd), BF16)], axis=1)
        acc = jnp.exp2(m - m_new) * acc + jnp.dot(p, v_ones, preferred_element_type=F32)
        m = m_new
    o = acc[:, :hd] / acc[:, hd:]
    o_ref[...] = jnp.concatenate([o[g * qb:(g + 1) * qb] for g in range(group)], axis=1).astype(BF16)


def _attention(q, k, v, cache, *, seq_len, qb, chunk, n_heads, n_kv):
    n = q.shape[0]
    hd = HEAD_DIM
    group = n_heads // n_kv
    nq = seq_len // qb
    in_specs = [
        pl.BlockSpec((qb, group * hd), lambda b, h, i: (b * nq + i, h)),
        pl.BlockSpec((seq_len, hd), lambda b, h, i: (b, h)),
        pl.BlockSpec((seq_len, hd), lambda b, h, i: (b, h)),
    ]
    args = [q, k, v]
    if cache is not None:
        past = cache[0].shape[1]
        in_specs += [pl.BlockSpec((1, past, hd), lambda b, h, i: (b, 0, h))] * 2
        args += list(cache)
    return pl.pallas_call(
        functools.partial(_attn_kernel, group=group, chunk=chunk, has_cache=cache is not None),
        grid=(n // seq_len, n_kv, nq),
        in_specs=in_specs,
        out_specs=pl.BlockSpec((qb, group * hd), lambda b, h, i: (b * nq + i, h)),
        out_shape=jax.ShapeDtypeStruct((n, n_heads * hd), BF16),
        compiler_params=_params(3, 48),
        name=f"attention_{seq_len}",
    )(*args)


def _final_norm_kernel(x_ref, acc_ref, pmod_ref, gain_ref, o_ref, *, mod_row):
    n_rows, d = x_ref.shape
    m = mod_row(pl.program_id(0))
    x = x_ref[...] + pmod_ref[pl.ds(m, 1), pl.ds(5 * d, d)] * _load_rows(acc_ref, n_rows, d)
    o_ref[...] = (x * lax.rsqrt(jnp.mean(x * x, axis=-1, keepdims=True) + EPS)) * gain_ref[...]


def _final_norm(x, acc, pmod, gain, mod_row):
    n, d = x.shape
    row = lambda i: (i, 0)
    fixed = lambda i: (0, 0)
    return pl.pallas_call(
        functools.partial(_final_norm_kernel, mod_row=mod_row),
        grid=(n // ROW_BLOCK,),
        in_specs=[pl.BlockSpec((ROW_BLOCK, d), row), _rows_spec(acc, d),
                  pl.BlockSpec(pmod.shape, fixed), pl.BlockSpec((1, d), fixed)],
        out_specs=pl.BlockSpec((ROW_BLOCK, d), row),
        out_shape=jax.ShapeDtypeStruct((n, d), F32),
        compiler_params=_params(1, 32),
        name="final_norm",
    )(x, acc, pmod, gain)


def _rope_head_order(x):
    q = HEAD_DIM // 4
    row1, row2, col1, col2 = (x[..., i * q:(i + 1) * q] for i in range(4))
    return jnp.concatenate([row1, col1, row2, col2], axis=-1)


def _rope_tables(seq_len):
    axis = HEAD_DIM // 2
    t = np.arange(seq_len)
    inv = ROPE_BASE ** (-np.arange(axis // 2, dtype=np.float64) * 2.0 / axis)
    ang = np.concatenate([(t // GRID_W)[:, None] * inv, (t % GRID_W)[:, None] * inv], axis=1)
    c, s = np.cos(ang), np.sin(ang)
    return (jnp.asarray(np.concatenate([c, c], axis=1), F32), jnp.asarray(np.concatenate([-s, s], axis=1), F32))


def kernel(x_prompt, x_sample, state_lru, cache_k, cache_v, c, c_ctx, w_mod, b_mod, norm_gain, final_gain,
           w_lru_in, lru_conv_w, lru_conv_b, lru_wa, lru_ba, lru_wx, lru_bx, lru_lambda, w_lru_out,
           w_qkv, q_norm, k_norm, w_attn_out, w_router, w_exp_gate, w_exp_up, w_exp_down):
    batch, seq, d = x_prompt.shape
    dec_batch, dec_seq, _ = x_sample.shape
    depth = w_mod.shape[0]
    n_kv = cache_k.shape[3]
    n_heads = w_attn_out.shape[1] // HEAD_DIM
    heads = dict(n_heads=n_heads, n_kv=n_kv)

    cond = jnp.zeros((SUBLANES, d), F32).at[0].set(c_ctx).at[1:1 + dec_batch].set(c)
    mod_all = _mod_vectors(cond, w_mod, b_mod)

    blocks_per_seq = dec_seq // ROW_BLOCK
    xs = [x_prompt.reshape(batch * seq, d), x_sample.reshape(dec_batch * dec_seq, d)]
    lens = [seq, dec_seq]
    mod_rows = [lambda i: 0, lambda i: 1 + i // blocks_per_seq]
    prevs = [None, None]

    fg = final_gain.reshape(1, d)
    wr_hi, wr_lo = _split_bf16(jnp.swapaxes(w_router, 1, 2))
    new_lru, new_k, new_v = [], [], []
    for l in range(depth):
        mod = mod_all[l]
        gain_a = norm_gain[l, 0].reshape(1, d)
        gain_c = norm_gain[l, 1].reshape(1, d)
        mixed = []
        if l % 2 == 0:
            li = l // 2
            w_in = w_lru_in[li].astype(BF16)
            wg = (0.5 * jnp.concatenate([lru_wa[li, 0], lru_wx[li, 0], lru_wa[li, 1], lru_wx[li, 1]],
                                        axis=2)).astype(BF16)
            bias4 = 0.5 * jnp.stack([lru_ba[li, 0], lru_bx[li, 0], lru_ba[li, 1], lru_bx[li, 1]])
            h0s = [jnp.zeros((batch, 2, w_in.shape[1] // 2), F32), state_lru[:, li]]
            for gi in range(2):
                xs[gi], gate, xr = _lru_in(xs[gi], prevs[gi], mod, gain_a, w_in, mod_rows[gi])
                y, fin = _lru_core(xr, gate, lru_conv_w[li], lru_conv_b[li].reshape(1, -1), wg, bias4,
                                   lru_lambda[li], h0s[gi], seq_len=lens[gi])
                mixed.append(y)
                if gi == 0:
                    new_lru.append(fin.astype(x_prompt.dtype))
            w_out = w_lru_out[li].astype(BF16)
        else:
            ai = l // 2
            w = w_qkv[ai].astype(BF16)
            qg = q_norm[ai].reshape(1, -1)
            kg = k_norm[ai].reshape(1, -1)
            xs[0], q_p, k_p, v_p, kf, vf = _qkv(xs[0], prevs[0], mod, gain_a, w, qg, kg, None,
                                                seq_len=seq, mod_row=mod_rows[0], **heads)
            n_qk = (n_heads + n_kv) * HEAD_DIM
            w_qk = _rope_head_order(w_qkv[ai][:, :n_qk].reshape(d, n_heads + n_kv, HEAD_DIM)).reshape(d, n_qk)
            w_r = jnp.concatenate([w_qk, w_qkv[ai][:, n_qk:]], axis=1).astype(BF16)
            xs[1], q_s, k_s, v_s = _qkv(xs[1], prevs[1], mod, gain_a, w_r, _rope_head_order(qg),
                                        _rope_head_order(kg), _rope_tables(dec_seq),
                                        seq_len=dec_seq, mod_row=mod_rows[1], **heads)
            new_k.append(kf.reshape(batch, seq, n_kv, HEAD_DIM))
            new_v.append(vf.reshape(batch, seq, n_kv, HEAD_DIM))
            past = cache_k.shape[2]
            kc = _rope_head_order(cache_k[:, ai]).reshape(dec_batch, past, n_kv * HEAD_DIM).astype(BF16)
            vc = cache_v[:, ai].reshape(dec_batch, past, n_kv * HEAD_DIM).astype(BF16)
            mixed.append(_attention(q_p, k_p, v_p, None, seq_len=seq, qb=seq, chunk=seq, **heads))
            mixed.append(_attention(q_s, k_s, v_s, (kc, vc), seq_len=dec_seq, qb=256, chunk=256, **heads))
            w_out = w_attn_out[ai].astype(BF16)

        x1_p, h2_p, lg_p = _mix_out(mixed[0], xs[0], mod, gain_c, w_out, wr_hi, wr_lo, l, mod_rows[0], False)
        x1_s, h2_s, lg_s = _mix_out(mixed[1], xs[1], mod, gain_c, w_out, wr_hi, wr_lo, l, mod_rows[1], True)
        idx_p, g_p, idx_s, g_s = _route(lg_p, lg_s, seq, dec_seq)
        xe_p = _gather_onehot(idx_p[:, None, :], h2_p, seq_len=seq)
        off_s = idx_s * _tiles_per_row(d)
        xe_s = _gather_rows(off_s, h2_s, d, seq_len=dec_seq)
        e_n = N_EXPERTS
        zero_rows = ((0, 0), (0, SUBLANES - 1), (0, 0))
        gc_p = jnp.pad(g_p.reshape(batch, e_n, -1).transpose(1, 0, 2).reshape(e_n, 1, -1), zero_rows)
        gc_s = jnp.pad(g_s.reshape(dec_batch, e_n, -1).transpose(1, 0, 2).reshape(e_n, 1, -1), zero_rows)
        ye_p, ye_s = _ffn(xe_p, xe_s, gc_p, gc_s, w_exp_gate, w_exp_up, w_exp_down, l)
        last = l == depth - 1
        x_p = _combine_onehot(idx_p[:, None, :], ye_p, x1_p, mod, fg if last else None, seq_len=seq,
                              mod_row=lambda s: 0)
        acc_s = _combine_rows(off_s, ye_s, seq_len=dec_seq)
        xs = [x_p, x1_s]
        prevs = [None, (acc_s, mod)]

    y_prompt = xs[0].reshape(batch, seq, d)
    y_sample = _final_norm(xs[1], *prevs[1], fg, mod_rows[1]).reshape(dec_batch, dec_seq, d)
    return (y_prompt, y_sample, jnp.stack(new_lru, axis=1), jnp.stack(new_k, axis=1), jnp.stack(new_v, axis=1))
```

```python
import functools
import math

import jax
import jax.numpy as jnp
import numpy as np
from jax import lax
from jax.experimental import pallas as pl
from jax.experimental.pallas import tpu as pltpu

F32 = jnp.float32
BF16 = jnp.bfloat16
I32 = jnp.int32

LANES = 128
SUBLANES = 8
MIB = 1024 * 1024
F32_TINY = float(np.finfo(np.float32).tiny)

RG_C = 8.0
CONV_W = 4
CONV_PAD_L = 2
ROPE_BASE = 10000.0
GRID_W = 64
EPS = 1e-6
N_EXPERTS = 16
CAPACITY_FACTOR = 2
HEAD_DIM = 128
LRU_BLOCKS = 8
LRU_MAX_STEP_BLOCKS = 4
LRU_CORE_VMEM_MIB = 44

ROW_BLOCK = 1024
QKV_ROW_BLOCK = 512


def _params(n_axes, vmem_mib):
    return pltpu.CompilerParams(
        dimension_semantics=("arbitrary",) * n_axes, vmem_limit_bytes=vmem_mib * MIB)


def _split_bf16(x):
    hi = x.astype(BF16)
    lo = (x - hi.astype(F32)).astype(BF16)
    return hi, lo


def _modulated_norm(x, gain, shift, scale):
    y = x * lax.rsqrt(jnp.mean(x * x, axis=-1, keepdims=True) + EPS)
    return (y * gain) * (1.0 + scale) + shift


def _tiles_per_row(d):
    return d // LANES


def _load_rows(ref, n_rows, d):
    if ref.shape == (n_rows, d):
        return ref[...]
    tpr = _tiles_per_row(d)
    return jnp.concatenate([ref[pl.ds(s, n_rows, stride=tpr), :] for s in range(tpr)], axis=1)


def _store_rows(ref, x, r0):
    n_rows, d = x.shape
    if ref.shape[1] == d:
        ref[r0:r0 + n_rows, :] = x.astype(ref.dtype)
        return
    tpr = _tiles_per_row(d)
    for s in range(tpr):
        ref[pl.ds(r0 * tpr + s, n_rows, stride=tpr), :] = x[:, s * LANES:(s + 1) * LANES]


def _rows_spec(arr, d, rows):
    if arr.shape[1] == d:
        return pl.BlockSpec((rows, d), lambda i: (i, 0))
    return pl.BlockSpec((rows * _tiles_per_row(d), LANES), lambda i: (i, 0))


def _residual_stream(x_ref, prev, m):
    if prev is None:
        return x_ref[...]
    acc_ref, pmod_ref, xo_ref = prev
    n_rows, d = x_ref.shape
    x = x_ref[...] + pmod_ref[pl.ds(m, 1), pl.ds(5 * d, d)] * _load_rows(acc_ref, n_rows, d)
    xo_ref[...] = x
    return x


def _mod_kernel(cond_ref, w_ref, b_ref, o_ref):
    c = cond_ref[...]
    a_hi, a_lo = _split_bf16(c * jax.nn.sigmoid(c))
    w_hi, w_lo = _split_bf16(w_ref[0])
    dot = functools.partial(jnp.dot, preferred_element_type=F32)
    o_ref[0] = dot(a_hi, w_hi) + (dot(a_lo, w_hi) + dot(a_hi, w_lo)) + b_ref[0]


def _mod_vectors(cond8, w_mod, b_mod):
    depth, d, d6 = w_mod.shape
    nb = 1536
    return pl.pallas_call(
        _mod_kernel,
        grid=(depth, d6 // nb),
        in_specs=[
            pl.BlockSpec((SUBLANES, d), lambda l, j: (0, 0)),
            pl.BlockSpec((1, d, nb), lambda l, j: (l, 0, j)),
            pl.BlockSpec((1, 1, nb), lambda l, j: (l, 0, j)),
        ],
        out_specs=pl.BlockSpec((1, SUBLANES, nb), lambda l, j: (l, 0, j)),
        out_shape=jax.ShapeDtypeStruct((depth, SUBLANES, d6), F32),
        compiler_params=_params(2, 40),
        name="adaln_mod",
    )(cond8, w_mod, b_mod.reshape(depth, 1, d6))


def _lru_in_kernel(*refs, mod_row, has_prev):
    if has_prev:
        x_ref, acc_ref, pmod_ref, mod_ref, gain_ref, w_ref, xo_ref, gate_ref, xr_ref = refs
        prev = (acc_ref, pmod_ref, xo_ref)
    else:
        x_ref, mod_ref, gain_ref, w_ref, gate_ref, xr_ref = refs
        prev = None
    d = x_ref.shape[1]
    m = mod_row(pl.program_id(0), x_ref.shape[0])
    x = _residual_stream(x_ref, prev, m)
    shift = mod_ref[pl.ds(m, 1), pl.ds(0, d)]
    scale = mod_ref[pl.ds(m, 1), pl.ds(d, d)]
    h = _modulated_norm(x, gain_ref[...], shift, scale).astype(BF16)
    u = jnp.dot(h, w_ref[...], preferred_element_type=F32)
    w = gate_ref.shape[1]
    gate_ref[...] = u[:, :w]
    xr_ref[...] = u[:, w:]


def _lru_in(x, prev, mod, gain, w_in, mod_row):
    n, d = x.shape
    w2 = w_in.shape[1]
    w = w2 // 2
    row = lambda i: (i, 0)
    fixed = lambda i: (0, 0)
    in_specs = [pl.BlockSpec((ROW_BLOCK, d), row)]
    args = [x]
    out_specs = [pl.BlockSpec((ROW_BLOCK, w), row), pl.BlockSpec((ROW_BLOCK, w), row)]
    out_shape = [jax.ShapeDtypeStruct((n, w), F32), jax.ShapeDtypeStruct((n, w), F32)]
    if prev is not None:
        in_specs += [_rows_spec(prev[0], d, ROW_BLOCK), pl.BlockSpec(prev[1].shape, fixed)]
        args += list(prev)
        out_specs = [pl.BlockSpec((ROW_BLOCK, d), row)] + out_specs
        out_shape = [jax.ShapeDtypeStruct((n, d), F32)] + out_shape
    in_specs += [pl.BlockSpec(mod.shape, fixed), pl.BlockSpec((1, d), fixed), pl.BlockSpec((d, w2), fixed)]
    args += [mod, gain, w_in]
    outs = pl.pallas_call(
        functools.partial(_lru_in_kernel, mod_row=mod_row, has_prev=prev is not None),
        grid=(n // ROW_BLOCK,),
        in_specs=in_specs,
        out_specs=out_specs,
        out_shape=out_shape,
        compiler_params=_params(1, 48),
        name="lru_in",
    )(*args)
    return outs if prev is not None else [x] + list(outs)


def _lru_core_kernel(xr_ref, gate_ref, cw_ref, cb_ref, wg_ref, bias_ref, lam_ref, h0_ref,
                     y_ref, fin_ref, pad, a_f, b_f, a_b, b_b):
    t_len, lb = xr_ref.shape
    ch = 256
    halo = SUBLANES
    seg = t_len // SUBLANES
    piece = min(ch, seg)
    seg_shift = seg.bit_length() - 1

    def seg_rows(t0):
        r = lax.shift_right_logical(t0, seg_shift)
        return pl.ds((t0 - r * seg) * SUBLANES + r, piece, stride=SUBLANES)

    zero_rows = jnp.zeros((halo, lb), F32)
    pad[0:halo, :] = zero_rows
    pad[t_len + halo:t_len + 2 * halo, :] = zero_rows

    def copy_in(c, carry):
        r = pl.multiple_of(c * ch, ch)
        pad[pl.ds(r + halo, ch), :] = xr_ref[pl.ds(r, ch), :]
        return carry

    lax.fori_loop(0, t_len // ch, copy_in, 0)

    neg_lam = -lam_ref[...]
    softplus = jnp.maximum(neg_lam, 0.0) + jnp.log1p(jnp.exp(-jnp.abs(neg_lam)))
    c_nla = (0.5 * RG_C) * softplus
    c_exp2 = (-0.5 * RG_C * math.log2(math.e)) * softplus
    cw = cw_ref[...]
    cbias = cb_ref[...]
    half_bias = bias_ref[...]
    half_wg = wg_ref[...]
    bw = half_wg.shape[1]
    n_slab = lb // LANES

    def gates(c, carry):
        r = pl.multiple_of(c * ch, ch)
        blk = pad[pl.ds(r, ch + 2 * halo), :]
        xc = cbias
        for k in range(CONV_W):
            o = halo - CONV_PAD_L + k
            xc = xc + blk[o:o + ch] * cw[k:k + 1]
        xc_bf = xc.astype(BF16)
        g = [jnp.dot(xc_bf[:, j * bw:(j + 1) * bw], half_wg[j], preferred_element_type=F32)
             for j in range(lb // bw)]
        pre = lambda k: jnp.concatenate([gj[:, k * bw:(k + 1) * bw] for gj in g], axis=1) + half_bias[k:k + 1]
        half_xc = 0.5 * xc
        for d, (a_ref, b_ref) in enumerate(((a_f, b_f), (a_b, b_b))):
            u = jnp.tanh(pre(2 * d)) + 1.0
            t_i = jnp.tanh(pre(2 * d + 1))
            a = jnp.exp2(u * c_exp2[d:d + 1])
            q = jnp.tanh(u * c_nla[d:d + 1]) * (1.0 + a * a)
            b = (q * lax.rsqrt(jnp.maximum(q, F32_TINY))) * (t_i * half_xc + half_xc)
            for p in range(ch // piece):
                rows = seg_rows(r + p * piece)
                for j in range(n_slab):
                    a_ref[j, rows, :] = a[p * piece:(p + 1) * piece, j * LANES:(j + 1) * LANES]
                    b_ref[j, rows, :] = b[p * piece:(p + 1) * piece, j * LANES:(j + 1) * LANES]
        return carry

    lax.fori_loop(0, t_len // ch, gates, 0)

    def scan(i, carry):
        rows = (pl.ds(pl.multiple_of(i * SUBLANES, SUBLANES), SUBLANES),
                pl.ds(pl.multiple_of((seg - 1 - i) * SUBLANES, SUBLANES), SUBLANES))
        out = []
        for k, (h, prod) in enumerate(carry):
            d, j = divmod(k, n_slab)
            a_ref, b_ref = ((a_f, b_f), (a_b, b_b))[d]
            a = a_ref[j, rows[d], :]
            h = a * h + b_ref[j, rows[d], :]
            prod = a * prod
            b_ref[j, rows[d], :] = h
            a_ref[j, rows[d], :] = prod
            out.append((h, prod))
        return tuple(out)

    start = (jnp.zeros((SUBLANES, LANES), F32), jnp.ones((SUBLANES, LANES), F32))
    ends = lax.fori_loop(0, seg, scan, (start,) * (2 * n_slab), unroll=8)

    enter = []
    for k, (h, prod) in enumerate(ends):
        d, j = divmod(k, n_slab)
        states = [h0_ref[0, d:d + 1, j * LANES:(j + 1) * LANES]]
        for r in (range(SUBLANES) if d == 0 else reversed(range(SUBLANES))):
            states.append(h[r:r + 1] + prod[r:r + 1] * states[-1])
        fin_ref[0, d:d + 1, j * LANES:(j + 1) * LANES] = states[SUBLANES]
        order = states[:SUBLANES] if d == 0 else states[:SUBLANES][::-1]
        enter.append(jnp.concatenate(order, axis=0))

    def fix(i, carry):
        rows = pl.ds(pl.multiple_of(i * SUBLANES, SUBLANES), SUBLANES)
        for j in range(n_slab):
            b_f[j, rows, :] = ((b_f[j, rows, :] + a_f[j, rows, :] * enter[j])
                               + (b_b[j, rows, :] + a_b[j, rows, :] * enter[n_slab + j]))
        return carry

    lax.fori_loop(0, seg, fix, 0, unroll=8)

    def emit(c, carry):
        r = pl.multiple_of(c * ch, ch)
        gt = gate_ref[pl.ds(r, ch), :]
        cdf = 0.5 * (1.0 + jnp.tanh(math.sqrt(2.0 / math.pi) * (gt + 0.044715 * (gt * gt * gt))))
        y = gt * cdf
        for p in range(ch // piece):
            rows = seg_rows(r + p * piece)
            total = jnp.concatenate([b_f[j, rows, :] for j in range(n_slab)], axis=1)
            y_ref[pl.ds(r + p * piece, piece), :] = (total * y[p * piece:(p + 1) * piece]).astype(BF16)
        return carry

    lax.fori_loop(0, t_len // ch, emit, 0)


def _lru_core(xr, gate, conv_w, conv_b, wg, bias4, lam, h0, *, seq_len):
    n, w = xr.shape
    n_seq = n // seq_len
    bw = w // LRU_BLOCKS
    block_bytes = seq_len * bw * (5 * 4 + 2 * 2 * 4 + 2 * 2)
    step_blocks = LRU_MAX_STEP_BLOCKS
    while step_blocks * block_bytes > LRU_CORE_VMEM_MIB * MIB:
        step_blocks //= 2
    lb = step_blocks * bw
    tok = lambda s, c: (s, c)
    chan = lambda s, c: (0, c)
    return pl.pallas_call(
        _lru_core_kernel,
        grid=(n_seq, LRU_BLOCKS // step_blocks),
        in_specs=[
            pl.BlockSpec((seq_len, lb), tok),
            pl.BlockSpec((seq_len, lb), tok),
            pl.BlockSpec((CONV_W, lb), chan),
            pl.BlockSpec((1, lb), chan),
            pl.BlockSpec((step_blocks, bw, 4 * bw), lambda s, c: (c, 0, 0)),
            pl.BlockSpec((4, lb), chan),
            pl.BlockSpec((2, lb), chan),
            pl.BlockSpec((1, 2, lb), lambda s, c: (s, 0, c)),
        ],
        out_specs=[pl.BlockSpec((seq_len, lb), tok),
                   pl.BlockSpec((1, 2, lb), lambda s, c: (s, 0, c))],
        out_shape=[jax.ShapeDtypeStruct((n, w), BF16), jax.ShapeDtypeStruct((n_seq, 2, w), F32)],
        scratch_shapes=[pltpu.VMEM((seq_len + 2 * SUBLANES, lb), F32)]
        + [pltpu.VMEM((lb // LANES, seq_len, LANES), F32)] * 4,
        compiler_params=_params(2, LRU_CORE_VMEM_MIB + 4),
        name=f"lru_core_{seq_len}",
    )(xr, gate, conv_w, conv_b, wg, bias4, lam, h0)


def _mix_out_kernel(a_ref, x_ref, mod_ref, gain_ref, w_ref, wrh_ref, wrl_ref,
                    x1_ref, h2_ref, lg_ref, *, mod_row):
    d = x_ref.shape[1]
    m = mod_row(pl.program_id(0), x_ref.shape[0])
    g_mix = mod_ref[pl.ds(m, 1), pl.ds(2 * d, d)]
    shift = mod_ref[pl.ds(m, 1), pl.ds(3 * d, d)]
    scale = mod_ref[pl.ds(m, 1), pl.ds(4 * d, d)]
    op = jnp.dot(a_ref[...], w_ref[...], preferred_element_type=F32)
    x1 = x_ref[...] + g_mix * op
    x1_ref[...] = x1
    h2 = _modulated_norm(x1, gain_ref[...], shift, scale)
    h_hi, h_lo = _split_bf16(h2)
    _store_rows(h2_ref, h2, 0)
    nt = functools.partial(lax.dot_general, dimension_numbers=(((1,), (1,)), ((), ())),
                           preferred_element_type=F32)
    wr_hi = wrh_ref[0]
    lg_ref[...] = nt(wr_hi, h_hi) + (nt(wrl_ref[0], h_hi) + nt(wr_hi, h_lo))


def _mix_out(a, x, mod, gain, w_out, wr_hi, wr_lo, layer, mod_row, token_tiled):
    n, d = x.shape
    e = wr_hi.shape[1]
    row = lambda i: (i, 0)
    fixed = lambda i: (0, 0)
    router = pl.BlockSpec((1,) + wr_hi.shape[1:], lambda i: (layer, 0, 0))
    tpr = _tiles_per_row(d)
    if token_tiled:
        h2_spec = pl.BlockSpec((ROW_BLOCK * tpr, LANES), row)
        h2_shape = jax.ShapeDtypeStruct((n * tpr, LANES), F32)
    else:
        h2_spec = pl.BlockSpec((ROW_BLOCK, d), row)
        h2_shape = jax.ShapeDtypeStruct((n, d), BF16)
    return pl.pallas_call(
        functools.partial(_mix_out_kernel, mod_row=mod_row),
        grid=(n // ROW_BLOCK,),
        in_specs=[
            pl.BlockSpec((ROW_BLOCK, a.shape[1]), row),
            pl.BlockSpec((ROW_BLOCK, d), row),
            pl.BlockSpec(mod.shape, fixed),
            pl.BlockSpec((1, d), fixed),
            pl.BlockSpec(w_out.shape, fixed),
            router,
            router,
        ],
        out_specs=[pl.BlockSpec((ROW_BLOCK, d), row), h2_spec,
                   pl.BlockSpec((e, ROW_BLOCK), lambda i: (0, i))],
        out_shape=[jax.ShapeDtypeStruct((n, d), F32), h2_shape, jax.ShapeDtypeStruct((e, n), F32)],
        compiler_params=_params(1, 48),
        name="mix_out",
    )(a, x, mod, gain, w_out, wr_hi, wr_lo)


def _lane_sum(tiles):
    acc = tiles[0]
    for t in tiles[1:]:
        acc = acc + t
    return jnp.sum(acc, axis=1, keepdims=True)


def _exclusive_cumsum(flags, upper):
    out = []
    off = jnp.zeros((flags[0].shape[0], 1), F32)
    for f in flags:
        out.append(jnp.dot(f.astype(BF16), upper, preferred_element_type=F32) + off)
        off = off + jnp.sum(f, axis=1, keepdims=True)
    return out


def _route_group(lg_ref, n_seq, t_len, idx_ref, g_ref):
    cap = CAPACITY_FACTOR * t_len // N_EXPERTS
    affs = []
    for s in range(n_seq):
        lg = lg_ref[:, s * t_len:(s + 1) * t_len]
        ex = jnp.exp(lg - jnp.max(lg, axis=0, keepdims=True))
        affs.append(ex / jnp.sum(ex, axis=0, keepdims=True))
    aff = jnp.concatenate(affs, axis=0)
    n_rows = aff.shape[0]
    nt = t_len // LANES
    g = [aff[:, j * LANES:(j + 1) * LANES] for j in range(nt)]

    kth_bits = jnp.zeros((n_rows, 1), I32)
    for bit in range(30, -1, -1):
        cand = kth_bits | (1 << bit)
        cand_f = pltpu.bitcast(cand, F32)
        cnt = _lane_sum([jnp.where(t >= cand_f, 1, 0) for t in g])
        kth_bits = jnp.where(cnt >= cap, cand, kth_bits)
    kth = pltpu.bitcast(kth_bits, F32)

    lane = lax.broadcasted_iota(I32, (LANES, LANES), 0)
    upper = jnp.where(lane < lax.broadcasted_iota(I32, (LANES, LANES), 1), 1.0, 0.0).astype(BF16)
    gt = [t > kth for t in g]
    eq = [t == kth for t in g]
    need = (cap - _lane_sum([jnp.where(m, 1, 0) for m in gt])).astype(F32)
    eq_rank = _exclusive_cumsum([jnp.where(m, 1.0, 0.0) for m in eq], upper)
    sel = [jnp.logical_or(gt[j], jnp.logical_and(eq[j], eq_rank[j] < need)) for j in range(nt)]
    pos = _exclusive_cumsum([jnp.where(m, 1.0, 0.0) for m in sel], upper)

    lane_r = lax.broadcasted_iota(I32, (n_rows, LANES), 1)
    d = [jnp.where(sel[j], lane_r + j * LANES - pos[j].astype(I32), -1) for j in range(nt)]
    for k in range(t_len.bit_length() - 1):
        s = 1 << k
        if s < LANES:
            d_rot = [pltpu.roll(x, LANES - s, 1) for x in d]
            g_rot = [pltpu.roll(x, LANES - s, 1) for x in g]
            same = lane_r < LANES - s
            d_in = [jnp.where(same, d_rot[j], d_rot[(j + 1) % nt]) for j in range(nt)]
            g_in = [jnp.where(same, g_rot[j], g_rot[(j + 1) % nt]) for j in range(nt)]
        else:
            q = s // LANES
            d_in = [d[(j + q) % nt] for j in range(nt)]
            g_in = [g[(j + q) % nt] for j in range(nt)]
        new_d, new_g = [], []
        for j in range(nt):
            move = jnp.logical_and(d_in[j] >= 0, ((d_in[j] >> k) & 1) == 1)
            stay = jnp.logical_and(d[j] >= 0, ((d[j] >> k) & 1) == 0)
            new_d.append(jnp.where(move, d_in[j], jnp.where(stay, d[j], -1)))
            new_g.append(jnp.where(move, g_in[j], g[j]))
        d, g = new_d, new_g

    for j in range(idx_ref.shape[1] // LANES):
        idx_ref[:, j * LANES:(j + 1) * LANES] = lane_r + j * LANES + d[j]
        g_ref[:, j * LANES:(j + 1) * LANES] = g[j]


def _route_kernel(lgp_ref, lgs_ref, idx_p_ref, g_p_ref, idx_s_ref, g_s_ref, *, p_len, s_len):
    _route_group(lgp_ref, lgp_ref.shape[1] // p_len, p_len, idx_p_ref, g_p_ref)
    _route_group(lgs_ref, lgs_ref.shape[1] // s_len, s_len, idx_s_ref, g_s_ref)


def _route(lg_p, lg_s, p_len, s_len):
    e = lg_p.shape[0]
    np_seq = lg_p.shape[1] // p_len
    ns_seq = lg_s.shape[1] // s_len
    cap_p = CAPACITY_FACTOR * p_len // N_EXPERTS
    cap_s = CAPACITY_FACTOR * s_len // N_EXPERTS
    wp = max(cap_p, LANES)
    ws = max(cap_s, LANES)
    idx_p, g_p, idx_s, g_s = pl.pallas_call(
        functools.partial(_route_kernel, p_len=p_len, s_len=s_len),
        out_shape=[jax.ShapeDtypeStruct((np_seq * e, wp), I32),
                   jax.ShapeDtypeStruct((np_seq * e, wp), F32),
                   jax.ShapeDtypeStruct((ns_seq * e, ws), I32),
                   jax.ShapeDtypeStruct((ns_seq * e, ws), F32)],
        compiler_params=pltpu.CompilerParams(vmem_limit_bytes=40 * MIB),
        name="route",
    )(lg_p, lg_s)
    return (idx_p[:, :cap_p].reshape(np_seq, e * cap_p), g_p[:, :cap_p].reshape(np_seq, e * cap_p),
            idx_s[:, :cap_s].reshape(-1), g_s[:, :cap_s].reshape(-1))


def _gather_onehot_kernel(idx_ref, h_ref, o_ref):
    t_len = h_ref.shape[0]
    n_exp, cap, _ = o_ref.shape
    idx = idx_ref[0]
    hit = jnp.where(idx == lax.broadcasted_iota(I32, (t_len, idx.shape[1]), 0), 1.0, 0.0).astype(BF16)
    xs = lax.dot_general(hit, h_ref[...], (((0,), (0,)), ((), ())), preferred_element_type=F32).astype(BF16)
    for e in range(n_exp):
        o_ref[e] = xs[e * cap:(e + 1) * cap]


def _gather_onehot(idx_row, h, *, seq_len):
    n, d = h.shape
    n_seq, _, slots = idx_row.shape
    cap = slots // N_EXPERTS
    return pl.pallas_call(
        _gather_onehot_kernel,
        grid=(n_seq,),
        in_specs=[pl.BlockSpec((1, 1, slots), lambda s: (s, 0, 0)),
                  pl.BlockSpec((seq_len, d), lambda s: (s, 0))],
        out_specs=pl.BlockSpec((N_EXPERTS, cap, d), lambda s: (0, s, 0)),
        out_shape=jax.ShapeDtypeStruct((N_EXPERTS, n_seq * cap, d), BF16),
        compiler_params=_params(1, 32),
        name="gather_onehot",
    )(idx_row, h)


def _gather_rows_kernel(idx_ref, h_ref, o_ref, tiles, *, cap, n_exp):
    tpr = _tiles_per_row(o_ref.shape[2])
    base = (pl.program_id(0) * n_exp + pl.program_id(1)) * cap

    def body(c, carry):
        src = pl.multiple_of(idx_ref[base + c], tpr)
        tiles[pl.ds(pl.multiple_of(c * tpr, tpr), tpr), :] = h_ref[pl.ds(src, tpr), :]
        return carry

    lax.fori_loop(0, cap, body, 0, unroll=8)
    for s in range(tpr):
        o_ref[0, :, s * LANES:(s + 1) * LANES] = tiles[pl.ds(s, cap, stride=tpr), :].astype(BF16)


def _gather_rows(idx, h, d, *, seq_len):
    tpr = _tiles_per_row(d)
    n_seq = h.shape[0] // (seq_len * tpr)
    cap = CAPACITY_FACTOR * seq_len // N_EXPERTS
    return pl.pallas_call(
        functools.partial(_gather_rows_kernel, cap=cap, n_exp=N_EXPERTS),
        grid_spec=pltpu.PrefetchScalarGridSpec(
            num_scalar_prefetch=1,
            grid=(n_seq, N_EXPERTS),
            in_specs=[pl.BlockSpec((seq_len * tpr, LANES), lambda s, e, idx: (s, 0))],
            out_specs=pl.BlockSpec((1, cap, d), lambda s, e, idx: (e, s, 0)),
            scratch_shapes=[pltpu.VMEM((cap * tpr, LANES), F32)],
        ),
        out_shape=jax.ShapeDtypeStruct((N_EXPERTS, n_seq * cap, d), BF16),
        compiler_params=_params(2, 48),
        name="gather_rows",
    )(idx, h)


def _ffn_kernel(xp_ref, xs_ref, gp_ref, gs_ref, wg_ref, wu_ref, wd_ref, yp_ref, ys_ref):
    rc = 512
    f = pl.program_id(1)
    last = f == pl.num_programs(1) - 1

    def body(first):
        wg = wg_ref[0, 0].astype(BF16)
        wu = wu_ref[0, 0].astype(BF16)
        wd = wd_ref[0, 0].astype(BF16)
        for x_ref, g_ref, y_ref in ((xp_ref, gp_ref, yp_ref), (xs_ref, gs_ref, ys_ref)):
            for r in range(0, x_ref.shape[1], rc):
                x = x_ref[0, r:r + rc, :]
                hg = jnp.dot(x, wg, preferred_element_type=F32)
                hu = jnp.dot(x, wu, preferred_element_type=F32)
                hid = ((hg * jax.nn.sigmoid(hg)) * hu).astype(BF16)
                g_hi, g_lo = _split_bf16(g_ref[0, :, r:r + rc])
                ones = jnp.ones((g_hi.shape[0], LANES), BF16)
                tn = functools.partial(lax.dot_general, dimension_numbers=(((0,), (0,)), ((), ())),
                                       preferred_element_type=F32)
                weight = jnp.where(last, tn(g_hi, ones) + tn(g_lo, ones), 1.0)
                weight = jnp.concatenate([weight] * (y_ref.shape[2] // LANES), axis=1)
                y = jnp.dot(hid, wd, preferred_element_type=F32)
                if not first:
                    y = y_ref[0, r:r + rc, :] + y
                y_ref[0, r:r + rc, :] = y * weight

    pl.when(f == 0)(functools.partial(body, True))
    pl.when(f != 0)(functools.partial(body, False))


def _ffn(xs_p, xs_s, g_p, g_s, w_gate, w_up, w_down, layer):
    n_exp, rp, d = xs_p.shape
    rs = xs_s.shape[1]
    ff = w_gate.shape[3]
    fc = 1024
    return pl.pallas_call(
        _ffn_kernel,
        grid=(n_exp, ff // fc),
        in_specs=[
            pl.BlockSpec((1, rp, d), lambda e, f: (e, 0, 0)),
            pl.BlockSpec((1, rs, d), lambda e, f: (e, 0, 0)),
            pl.BlockSpec((1, SUBLANES, rp), lambda e, f: (e, 0, 0)),
            pl.BlockSpec((1, SUBLANES, rs), lambda e, f: (e, 0, 0)),
            pl.BlockSpec((1, 1, d, fc), lambda e, f: (layer, e, 0, f)),
            pl.BlockSpec((1, 1, d, fc), lambda e, f: (layer, e, 0, f)),
            pl.BlockSpec((1, 1, fc, d), lambda e, f: (layer, e, f, 0)),
        ],
        out_specs=[pl.BlockSpec((1, rp, d), lambda e, f: (e, 0, 0)),
                   pl.BlockSpec((1, rs, d), lambda e, f: (e, 0, 0))],
        out_shape=[jax.ShapeDtypeStruct((n_exp, rp, d), F32),
                   jax.ShapeDtypeStruct((n_exp, rs, d), F32)],
        compiler_params=_params(2, 58),
        name="expert_ffn",
    )(xs_p, xs_s, g_p, g_s, w_gate, w_up, w_down)


def _combine_onehot_kernel(*refs, mod_row, final):
    if final:
        idx_ref, ye_ref, x_ref, pmod_ref, gain_ref, o_ref = refs
    else:
        idx_ref, ye_ref, x_ref, pmod_ref, o_ref = refs
    t_len, d = o_ref.shape
    n_exp = ye_ref.shape[0]
    idx = idx_ref[0]
    hit = jnp.where(idx == lax.broadcasted_iota(I32, (t_len, idx.shape[1]), 0), 1.0, 0.0).astype(BF16)
    y_hi, y_lo = _split_bf16(jnp.concatenate([ye_ref[e] for e in range(n_exp)], axis=0))
    acc = jnp.dot(hit, y_hi, preferred_element_type=F32) + jnp.dot(hit, y_lo, preferred_element_type=F32)
    x = x_ref[...] + pmod_ref[pl.ds(mod_row(pl.program_id(0), t_len), 1), pl.ds(5 * d, d)] * acc
    if final:
        x = (x * lax.rsqrt(jnp.mean(x * x, axis=-1, keepdims=True) + EPS)) * gain_ref[...]
    o_ref[...] = x


def _combine_onehot(idx_row, ye, x, pmod, final_gain, *, seq_len, mod_row):
    n_exp, rows, d = ye.shape
    n_seq, _, slots = idx_row.shape
    cap = slots // n_exp
    in_specs = [pl.BlockSpec((1, 1, slots), lambda s: (s, 0, 0)),
                pl.BlockSpec((n_exp, cap, d), lambda s: (0, s, 0)),
                pl.BlockSpec((seq_len, d), lambda s: (s, 0)),
                pl.BlockSpec(pmod.shape, lambda s: (0, 0))]
    args = [idx_row, ye, x, pmod]
    if final_gain is not None:
        in_specs.append(pl.BlockSpec((1, d), lambda s: (0, 0)))
        args.append(final_gain)
    return pl.pallas_call(
        functools.partial(_combine_onehot_kernel, mod_row=mod_row, final=final_gain is not None),
        grid=(n_seq,),
        in_specs=in_specs,
        out_specs=pl.BlockSpec((seq_len, d), lambda s: (s, 0)),
        out_shape=jax.ShapeDtypeStruct((n_seq * seq_len, d), F32),
        compiler_params=_params(1, 32),
        name="combine_onehot",
    )(*args)


COMBINE_GROUP = 16


def _combine_rows_kernel(idx_ref, ye_ref, o_ref, tiles, *, cap, n_exp):
    e = pl.program_id(1)
    d = ye_ref.shape[2]
    tpr = _tiles_per_row(d)
    ch = 2048

    @pl.when(e == 0)
    def _():
        def zero(c, carry):
            o_ref[pl.ds(pl.multiple_of(c * ch, ch), ch), :] = jnp.zeros((ch, LANES), F32)
            return carry
        lax.fori_loop(0, o_ref.shape[0] // ch, zero, 0)

    for s in range(tpr):
        tiles[pl.ds(s, cap, stride=tpr), :] = ye_ref[0, :, s * LANES:(s + 1) * LANES]

    base = (pl.program_id(0) * n_exp + e) * cap

    def body(i, carry):
        c0 = i * COMBINE_GROUP
        dst = [pl.ds(pl.multiple_of(idx_ref[base + c0 + k], tpr), tpr) for k in range(COMBINE_GROUP)]
        vals = [o_ref[dst[k], :] + tiles[pl.ds(pl.multiple_of((c0 + k) * tpr, tpr), tpr), :]
                for k in range(COMBINE_GROUP)]
        for k in range(COMBINE_GROUP):
            o_ref[dst[k], :] = vals[k]
        return carry

    lax.fori_loop(0, cap // COMBINE_GROUP, body, 0)


def _combine_rows(idx, ye, *, seq_len):
    n_exp, rows, d = ye.shape
    tpr = _tiles_per_row(d)
    cap = CAPACITY_FACTOR * seq_len // N_EXPERTS
    n_seq = rows // cap
    return pl.pallas_call(
        functools.partial(_combine_rows_kernel, cap=cap, n_exp=n_exp),
        grid_spec=pltpu.PrefetchScalarGridSpec(
            num_scalar_prefetch=1,
            grid=(n_seq, n_exp),
            in_specs=[pl.BlockSpec((1, cap, d), lambda s, e, i: (e, s, 0))],
            out_specs=pl.BlockSpec((seq_len * tpr, LANES), lambda s, e, i: (s, 0)),
            scratch_shapes=[pltpu.VMEM((cap * tpr, LANES), F32)],
        ),
        out_shape=jax.ShapeDtypeStruct((n_seq * seq_len * tpr, LANES), F32),
        compiler_params=_params(2, 48),
        name="combine_rows",
    )(idx, ye)


def _head_norm(x, gain):
    return x * lax.rsqrt(jnp.mean(x * x, axis=-1, keepdims=True) + EPS) * gain


def _qkv_kernel(*refs, rope, has_prev, mod_row, n_heads, n_kv):
    refs = list(refs)
    x_ref = refs.pop(0)
    prev = None
    if has_prev:
        acc_ref, pmod_ref = refs.pop(0), refs.pop(0)
    mod_ref, gain_ref, w_ref, qg_ref, kg_ref = refs[:5]
    refs = refs[5:]
    if rope:
        cos_ref, sin_ref = refs.pop(0), refs.pop(0)
    if has_prev:
        prev = (acc_ref, pmod_ref, refs.pop(0))
    q_ref, k_ref, v_ref = refs[:3]
    d = x_ref.shape[1]
    hd = HEAD_DIM
    m = mod_row(pl.program_id(0), x_ref.shape[0])
    x = _residual_stream(x_ref, prev, m)
    shift = mod_ref[pl.ds(m, 1), pl.ds(0, d)]
    scale = mod_ref[pl.ds(m, 1), pl.ds(d, d)]
    h = _modulated_norm(x, gain_ref[...], shift, scale).astype(BF16)
    qg = qg_ref[...]
    kg = kg_ref[...]
    q_scale = math.log2(math.e) * hd ** -0.5
    half = x_ref.shape[0] // 2
    for r0 in (0, half):
        rows = slice(r0, r0 + half)
        qkv = jnp.dot(h[rows], w_ref[...], preferred_element_type=F32)
        if rope:
            cos = cos_ref[rows, :]
            sin = sin_ref[rows, :]

            def rot(xh, cos=cos, sin=sin):
                return xh * cos + pltpu.roll(xh, hd // 2, 1) * sin
        else:
            rot = lambda xh: xh

        for i in range(n_heads):
            qh = rot(_head_norm(qkv[:, i * hd:(i + 1) * hd], qg)) * q_scale
            q_ref[rows, i * hd:(i + 1) * hd] = qh.astype(BF16)
        for i in range(n_kv):
            c0 = (n_heads + i) * hd
            kh = _head_norm(qkv[:, c0:c0 + hd], kg)
            if not rope:
                refs[3][rows, i * hd:(i + 1) * hd] = kh
            k_ref[rows, i * hd:(i + 1) * hd] = rot(kh).astype(BF16)
        v = qkv[:, (n_heads + n_kv) * hd:]
        v_ref[rows, :] = v.astype(BF16)
        if not rope:
            refs[4][rows, :] = v


def _qkv(x, prev, mod, gain, w_qkv, q_gain, k_gain, tables, *, seq_len, mod_row, n_heads, n_kv):
    n, d = x.shape
    hd = HEAD_DIM
    row = lambda i: (i, 0)
    fixed = lambda i: (0, 0)
    rope = tables is not None
    rb = QKV_ROW_BLOCK
    in_specs = [pl.BlockSpec((rb, d), row)]
    args = [x]
    if prev is not None:
        in_specs += [_rows_spec(prev[0], d, rb), pl.BlockSpec(prev[1].shape, fixed)]
        args += list(prev)
    in_specs += [
        pl.BlockSpec(mod.shape, fixed),
        pl.BlockSpec((1, d), fixed),
        pl.BlockSpec(w_qkv.shape, fixed),
        pl.BlockSpec((1, hd), fixed),
        pl.BlockSpec((1, hd), fixed),
    ]
    args += [mod, gain, w_qkv, q_gain, k_gain]
    if rope:
        per_seq = seq_len // rb
        in_specs += [pl.BlockSpec((rb, hd), lambda i: (i % per_seq, 0))] * 2
        args += list(tables)
    out_specs, out_shape = [], []
    if prev is not None:
        out_specs.append(pl.BlockSpec((rb, d), row))
        out_shape.append(jax.ShapeDtypeStruct((n, d), F32))
    out_specs += [pl.BlockSpec((rb, n_heads * hd), row),
                  pl.BlockSpec((rb, n_kv * hd), row),
                  pl.BlockSpec((rb, n_kv * hd), row)]
    out_shape += [jax.ShapeDtypeStruct((n, n_heads * hd), BF16),
                  jax.ShapeDtypeStruct((n, n_kv * hd), BF16),
                  jax.ShapeDtypeStruct((n, n_kv * hd), BF16)]
    if not rope:
        out_specs += [pl.BlockSpec((rb, n_kv * hd), row)] * 2
        out_shape += [jax.ShapeDtypeStruct((n, n_kv * hd), F32)] * 2
    outs = pl.pallas_call(
        functools.partial(_qkv_kernel, rope=rope, has_prev=prev is not None, mod_row=mod_row,
                          n_heads=n_heads, n_kv=n_kv),
        grid=(n // rb,),
        in_specs=in_specs,
        out_specs=out_specs,
        out_shape=out_shape,
        compiler_params=_params(1, 48),
        name="qkv_rope" if rope else "qkv",
    )(*args)
    return list(outs) if prev is not None else [x] + list(outs)


def _attn_kernel(*refs, group, chunk, has_cache):
    if has_cache:
        q_ref, k_ref, v_ref, kc_ref, vc_ref, o_ref = refs
    else:
        q_ref, k_ref, v_ref, o_ref = refs
    hd = HEAD_DIM
    qb = q_ref.shape[0]
    rows = group * qb
    q = jnp.concatenate([q_ref[:, g * hd:(g + 1) * hd] for g in range(group)], axis=0)
    sources = [(k_ref, v_ref, s0, chunk) for s0 in range(0, k_ref.shape[0], chunk)]
    if has_cache:
        sources.append((kc_ref.at[0], vc_ref.at[0], 0, kc_ref.shape[1]))
    m = jnp.full((rows, 1), -jnp.inf, F32)
    acc = jnp.zeros((rows, 2 * hd), F32)
    for kr, vr, s0, size in sources:
        s = lax.dot_general(q, kr[s0:s0 + size, :], (((1,), (1,)), ((), ())),
                            preferred_element_type=F32)
        m_new = jnp.maximum(m, jnp.max(s, axis=-1, keepdims=True))
        p = jnp.exp2(s - m_new).astype(BF16)
        v_ones = jnp.concatenate([vr[s0:s0 + size, :], jnp.ones((size, hd), BF16)], axis=1)
        acc = jnp.exp2(m - m_new) * acc + jnp.dot(p, v_ones, preferred_element_type=F32)
        m = m_new
    o = acc[:, :hd] / acc[:, hd:]
    o_ref[...] = jnp.concatenate([o[g * qb:(g + 1) * qb] for g in range(group)], axis=1).astype(BF16)


def _attention(q, k, v, cache, *, seq_len, qb, chunk, n_heads, n_kv):
    n = q.shape[0]
    hd = HEAD_DIM
    group = n_heads // n_kv
    nq = seq_len // qb
    in_specs = [
        pl.BlockSpec((qb, group * hd), lambda b, h, i: (b * nq + i, h)),
        pl.BlockSpec((seq_len, hd), lambda b, h, i: (b, h)),
        pl.BlockSpec((seq_len, hd), lambda b, h, i: (b, h)),
    ]
    args = [q, k, v]
    if cache is not None:
        past = cache[0].shape[1]
        in_specs += [pl.BlockSpec((1, past, hd), lambda b, h, i: (b, 0, h))] * 2
        args += list(cache)
    return pl.pallas_call(
        functools.partial(_attn_kernel, group=group, chunk=chunk, has_cache=cache is not None),
        grid=(n // seq_len, n_kv, nq),
        in_specs=in_specs,
        out_specs=pl.BlockSpec((qb, group * hd), lambda b, h, i: (b * nq + i, h)),
        out_shape=jax.ShapeDtypeStruct((n, n_heads * hd), BF16),
        compiler_params=_params(3, 48),
        name=f"attention_{seq_len}",
    )(*args)


def _final_norm_kernel(x_ref, acc_ref, pmod_ref, gain_ref, o_ref, *, mod_row):
    n_rows, d = x_ref.shape
    m = mod_row(pl.program_id(0), x_ref.shape[0])
    x = x_ref[...] + pmod_ref[pl.ds(m, 1), pl.ds(5 * d, d)] * _load_rows(acc_ref, n_rows, d)
    o_ref[...] = (x * lax.rsqrt(jnp.mean(x * x, axis=-1, keepdims=True) + EPS)) * gain_ref[...]


def _final_norm(x, acc, pmod, gain, mod_row):
    n, d = x.shape
    row = lambda i: (i, 0)
    fixed = lambda i: (0, 0)
    return pl.pallas_call(
        functools.partial(_final_norm_kernel, mod_row=mod_row),
        grid=(n // ROW_BLOCK,),
        in_specs=[pl.BlockSpec((ROW_BLOCK, d), row), _rows_spec(acc, d, ROW_BLOCK),
                  pl.BlockSpec(pmod.shape, fixed), pl.BlockSpec((1, d), fixed)],
        out_specs=pl.BlockSpec((ROW_BLOCK, d), row),
        out_shape=jax.ShapeDtypeStruct((n, d), F32),
        compiler_params=_params(1, 32),
        name="final_norm",
    )(x, acc, pmod, gain)


def _rope_head_order(x):
    q = HEAD_DIM // 4
    row1, row2, col1, col2 = (x[..., i * q:(i + 1) * q] for i in range(4))
    return jnp.concatenate([row1, col1, row2, col2], axis=-1)


def _rope_tables(seq_len):
    axis = HEAD_DIM // 2
    t = np.arange(seq_len)
    inv = ROPE_BASE ** (-np.arange(axis // 2, dtype=np.float64) * 2.0 / axis)
    ang = np.concatenate([(t // GRID_W)[:, None] * inv, (t % GRID_W)[:, None] * inv], axis=1)
    c, s = np.cos(ang), np.sin(ang)
    return (jnp.asarray(np.concatenate([c, c], axis=1), F32), jnp.asarray(np.concatenate([-s, s], axis=1), F32))


def kernel(x_prompt, x_sample, state_lru, cache_k, cache_v, c, c_ctx, w_mod, b_mod, norm_gain, final_gain,
           w_lru_in, lru_conv_w, lru_conv_b, lru_wa, lru_ba, lru_wx, lru_bx, lru_lambda, w_lru_out,
           w_qkv, q_norm, k_norm, w_attn_out, w_router, w_exp_gate, w_exp_up, w_exp_down):
    batch, seq, d = x_prompt.shape
    dec_batch, dec_seq, _ = x_sample.shape
    depth = w_mod.shape[0]
    n_kv = cache_k.shape[3]
    n_heads = w_attn_out.shape[1] // HEAD_DIM
    heads = dict(n_heads=n_heads, n_kv=n_kv)

    cond = jnp.zeros((SUBLANES, d), F32).at[0].set(c_ctx).at[1:1 + dec_batch].set(c)
    mod_all = _mod_vectors(cond, w_mod, b_mod)

    xs = [x_prompt.reshape(batch * seq, d), x_sample.reshape(dec_batch * dec_seq, d)]
    lens = [seq, dec_seq]
    mod_rows = [lambda i, rows: 0, lambda i, rows: 1 + (i * rows) // dec_seq]
    prevs = [None, None]

    fg = final_gain.reshape(1, d)
    wr_hi, wr_lo = _split_bf16(jnp.swapaxes(w_router, 1, 2))
    new_lru, new_k, new_v = [], [], []
    for l in range(depth):
        mod = mod_all[l]
        gain_a = norm_gain[l, 0].reshape(1, d)
        gain_c = norm_gain[l, 1].reshape(1, d)
        mixed = []
        if l % 2 == 0:
            li = l // 2
            w_in = w_lru_in[li].astype(BF16)
            wg = (0.5 * jnp.concatenate([lru_wa[li, 0], lru_wx[li, 0], lru_wa[li, 1], lru_wx[li, 1]],
                                        axis=2)).astype(BF16)
            bias4 = 0.5 * jnp.stack([lru_ba[li, 0], lru_bx[li, 0], lru_ba[li, 1], lru_bx[li, 1]])
            h0s = [jnp.zeros((batch, 2, w_in.shape[1] // 2), F32), state_lru[:, li]]
            for gi in range(2):
                xs[gi], gate, xr = _lru_in(xs[gi], prevs[gi], mod, gain_a, w_in, mod_rows[gi])
                y, fin = _lru_core(xr, gate, lru_conv_w[li], lru_conv_b[li].reshape(1, -1), wg, bias4,
                                   lru_lambda[li], h0s[gi], seq_len=lens[gi])
                mixed.append(y)
                if gi == 0:
                    new_lru.append(fin.astype(x_prompt.dtype))
            w_out = w_lru_out[li].astype(BF16)
        else:
            ai = l // 2
            w = w_qkv[ai].astype(BF16)
            qg = q_norm[ai].reshape(1, -1)
            kg = k_norm[ai].reshape(1, -1)
            xs[0], q_p, k_p, v_p, kf, vf = _qkv(xs[0], prevs[0], mod, gain_a, w, qg, kg, None,
                                                seq_len=seq, mod_row=mod_rows[0], **heads)
            n_qk = (n_heads + n_kv) * HEAD_DIM
            w_qk = _rope_head_order(w_qkv[ai][:, :n_qk].reshape(d, n_heads + n_kv, HEAD_DIM)).reshape(d, n_qk)
            w_r = jnp.concatenate([w_qk, w_qkv[ai][:, n_qk:]], axis=1).astype(BF16)
            xs[1], q_s, k_s, v_s = _qkv(xs[1], prevs[1], mod, gain_a, w_r, _rope_head_order(qg),
                                        _rope_head_order(kg), _rope_tables(dec_seq),
                                        seq_len=dec_seq, mod_row=mod_rows[1], **heads)
            new_k.append(kf.reshape(batch, seq, n_kv, HEAD_DIM))
            new_v.append(vf.reshape(batch, seq, n_kv, HEAD_DIM))
            past = cache_k.shape[2]
            kc = _rope_head_order(cache_k[:, ai]).reshape(dec_batch, past, n_kv * HEAD_DIM).astype(BF16)
            vc = cache_v[:, ai].reshape(dec_batch, past, n_kv * HEAD_DIM).astype(BF16)
            mixed.append(_attention(q_p, k_p, v_p, None, seq_len=seq, qb=seq, chunk=seq, **heads))
            mixed.append(_attention(q_s, k_s, v_s, (kc, vc), seq_len=dec_seq, qb=256, chunk=256, **heads))
            w_out = w_attn_out[ai].astype(BF16)

        x1_p, h2_p, lg_p = _mix_out(mixed[0], xs[0], mod, gain_c, w_out, wr_hi, wr_lo, l, mod_rows[0], False)
        x1_s, h2_s, lg_s = _mix_out(mixed[1], xs[1], mod, gain_c, w_out, wr_hi, wr_lo, l, mod_rows[1], True)
        idx_p, g_p, idx_s, g_s = _route(lg_p, lg_s, seq, dec_seq)
        xe_p = _gather_onehot(idx_p[:, None, :], h2_p, seq_len=seq)
        off_s = idx_s * _tiles_per_row(d)
        xe_s = _gather_rows(off_s, h2_s, d, seq_len=dec_seq)
        e_n = N_EXPERTS
        zero_rows = ((0, 0), (0, SUBLANES - 1), (0, 0))
        gc_p = jnp.pad(g_p.reshape(batch, e_n, -1).transpose(1, 0, 2).reshape(e_n, 1, -1), zero_rows)
        gc_s = jnp.pad(g_s.reshape(dec_batch, e_n, -1).transpose(1, 0, 2).reshape(e_n, 1, -1), zero_rows)
        ye_p, ye_s = _ffn(xe_p, xe_s, gc_p, gc_s, w_exp_gate, w_exp_up, w_exp_down, l)
        last = l == depth - 1
        x_p = _combine_onehot(idx_p[:, None, :], ye_p, x1_p, mod, fg if last else None, seq_len=seq,
                              mod_row=mod_rows[0])
        acc_s = _combine_rows(off_s, ye_s, seq_len=dec_seq)
        xs = [x_p, x1_s]
        prevs = [None, (acc_s, mod)]

    y_prompt = xs[0].reshape(batch, seq, d)
    y_sample = _final_norm(xs[1], *prevs[1], fg, mod_rows[1]).reshape(dec_batch, dec_seq, d)
    return (y_prompt, y_sample, jnp.stack(new_lru, axis=1), jnp.stack(new_k, axis=1), jnp.stack(new_v, axis=1))
```

```python
import functools
import math

import jax
import jax.numpy as jnp
import numpy as np
from jax import lax
from jax.experimental import pallas as pl
from jax.experimental.pallas import tpu as pltpu

F32 = jnp.float32
BF16 = jnp.bfloat16
I32 = jnp.int32

LANES = 128
SUBLANES = 8
MIB = 1024 * 1024
F32_TINY = float(np.finfo(np.float32).tiny)

RG_C = 8.0
CONV_W = 4
CONV_PAD_L = 2
ROPE_BASE = 10000.0
GRID_W = 64
EPS = 1e-6
N_EXPERTS = 16
CAPACITY_FACTOR = 2
HEAD_DIM = 128
LRU_BLOCKS = 8
LRU_MAX_STEP_BLOCKS = 4
LRU_CORE_VMEM_MIB = 44

ROW_BLOCK = 1024
QKV_ROW_BLOCK = 512


def _params(n_axes, vmem_mib):
    return pltpu.CompilerParams(
        dimension_semantics=("arbitrary",) * n_axes, vmem_limit_bytes=vmem_mib * MIB)


def _split_bf16(x):
    hi = x.astype(BF16)
    lo = (x - hi.astype(F32)).astype(BF16)
    return hi, lo


def _modulated_norm(x, gain, shift, scale):
    y = x * lax.rsqrt(jnp.mean(x * x, axis=-1, keepdims=True) + EPS)
    return (y * gain) * (1.0 + scale) + shift


def _tiles_per_row(d):
    return d // LANES


def _load_rows(ref, n_rows, d):
    if ref.shape == (n_rows, d):
        return ref[...]
    tpr = _tiles_per_row(d)
    return jnp.concatenate([ref[pl.ds(s, n_rows, stride=tpr), :] for s in range(tpr)], axis=1)


def _store_rows(ref, x, r0):
    n_rows, d = x.shape
    if ref.shape[1] == d:
        ref[r0:r0 + n_rows, :] = x.astype(ref.dtype)
        return
    tpr = _tiles_per_row(d)
    for s in range(tpr):
        ref[pl.ds(r0 * tpr + s, n_rows, stride=tpr), :] = x[:, s * LANES:(s + 1) * LANES]


def _rows_spec(arr, d, rows):
    if arr.shape[1] == d:
        return pl.BlockSpec((rows, d), lambda i: (i, 0))
    return pl.BlockSpec((rows * _tiles_per_row(d), LANES), lambda i: (i, 0))


def _residual_stream(x_ref, prev, m):
    if prev is None:
        return x_ref[...]
    acc_ref, pmod_ref, xo_ref = prev
    n_rows, d = x_ref.shape
    x = x_ref[...] + pmod_ref[pl.ds(m, 1), pl.ds(5 * d, d)] * _load_rows(acc_ref, n_rows, d)
    xo_ref[...] = x
    return x


def _mod_kernel(cond_ref, w_ref, b_ref, o_ref):
    c = cond_ref[...]
    a_hi, a_lo = _split_bf16(c * jax.nn.sigmoid(c))
    w_hi, w_lo = _split_bf16(w_ref[0])
    dot = functools.partial(jnp.dot, preferred_element_type=F32)
    o_ref[0] = dot(a_hi, w_hi) + (dot(a_lo, w_hi) + dot(a_hi, w_lo)) + b_ref[0]


def _mod_vectors(cond8, w_mod, b_mod):
    depth, d, d6 = w_mod.shape
    nb = 1536
    return pl.pallas_call(
        _mod_kernel,
        grid=(depth, d6 // nb),
        in_specs=[
            pl.BlockSpec((SUBLANES, d), lambda l, j: (0, 0)),
            pl.BlockSpec((1, d, nb), lambda l, j: (l, 0, j)),
            pl.BlockSpec((1, 1, nb), lambda l, j: (l, 0, j)),
        ],
        out_specs=pl.BlockSpec((1, SUBLANES, nb), lambda l, j: (l, 0, j)),
        out_shape=jax.ShapeDtypeStruct((depth, SUBLANES, d6), F32),
        compiler_params=_params(2, 40),
        name="adaln_mod",
    )(cond8, w_mod, b_mod.reshape(depth, 1, d6))


def _lru_in_kernel(*refs, mod_row, has_prev):
    if has_prev:
        x_ref, acc_ref, pmod_ref, mod_ref, gain_ref, w_ref, xo_ref, gate_ref, xr_ref = refs
        prev = (acc_ref, pmod_ref, xo_ref)
    else:
        x_ref, mod_ref, gain_ref, w_ref, gate_ref, xr_ref = refs
        prev = None
    d = x_ref.shape[1]
    m = mod_row(pl.program_id(0), x_ref.shape[0])
    x = _residual_stream(x_ref, prev, m)
    shift = mod_ref[pl.ds(m, 1), pl.ds(0, d)]
    scale = mod_ref[pl.ds(m, 1), pl.ds(d, d)]
    h = _modulated_norm(x, gain_ref[...], shift, scale).astype(BF16)
    u = jnp.dot(h, w_ref[...], preferred_element_type=F32)
    w = gate_ref.shape[1]
    gate_ref[...] = u[:, :w]
    xr_ref[...] = u[:, w:]


def _lru_in(x, prev, mod, gain, w_in, mod_row):
    n, d = x.shape
    w2 = w_in.shape[1]
    w = w2 // 2
    row = lambda i: (i, 0)
    fixed = lambda i: (0, 0)
    in_specs = [pl.BlockSpec((ROW_BLOCK, d), row)]
    args = [x]
    out_specs = [pl.BlockSpec((ROW_BLOCK, w), row), pl.BlockSpec((ROW_BLOCK, w), row)]
    out_shape = [jax.ShapeDtypeStruct((n, w), F32), jax.ShapeDtypeStruct((n, w), F32)]
    if prev is not None:
        in_specs += [_rows_spec(prev[0], d, ROW_BLOCK), pl.BlockSpec(prev[1].shape, fixed)]
        args += list(prev)
        out_specs = [pl.BlockSpec((ROW_BLOCK, d), row)] + out_specs
        out_shape = [jax.ShapeDtypeStruct((n, d), F32)] + out_shape
    in_specs += [pl.BlockSpec(mod.shape, fixed), pl.BlockSpec((1, d), fixed), pl.BlockSpec((d, w2), fixed)]
    args += [mod, gain, w_in]
    outs = pl.pallas_call(
        functools.partial(_lru_in_kernel, mod_row=mod_row, has_prev=prev is not None),
        grid=(n // ROW_BLOCK,),
        in_specs=in_specs,
        out_specs=out_specs,
        out_shape=out_shape,
        compiler_params=_params(1, 48),
        name="lru_in",
    )(*args)
    return outs if prev is not None else [x] + list(outs)


def _lru_core_kernel(xr_ref, gate_ref, cw_ref, cb_ref, wg_ref, bias_ref, lam_ref, h0_ref,
                     y_ref, fin_ref, pad, a_f, b_f, a_b, b_b):
    t_len, lb = xr_ref.shape
    ch = 256
    halo = SUBLANES
    seg = t_len // SUBLANES
    piece = min(ch, seg)
    seg_shift = seg.bit_length() - 1

    def seg_rows(t0):
        r = lax.shift_right_logical(t0, seg_shift)
        return pl.ds((t0 - r * seg) * SUBLANES + r, piece, stride=SUBLANES)

    zero_rows = jnp.zeros((halo, lb), F32)
    pad[0:halo, :] = zero_rows
    pad[t_len + halo:t_len + 2 * halo, :] = zero_rows

    def copy_in(c, carry):
        r = pl.multiple_of(c * ch, ch)
        pad[pl.ds(r + halo, ch), :] = xr_ref[pl.ds(r, ch), :]
        return carry

    lax.fori_loop(0, t_len // ch, copy_in, 0)

    neg_lam = -lam_ref[...]
    softplus = jnp.maximum(neg_lam, 0.0) + jnp.log1p(jnp.exp(-jnp.abs(neg_lam)))
    c_nla = (0.5 * RG_C) * softplus
    c_exp2 = (-0.5 * RG_C * math.log2(math.e)) * softplus
    cw = cw_ref[...]
    cbias = cb_ref[...]
    half_bias = bias_ref[...]
    half_wg = wg_ref[...]
    bw = half_wg.shape[1]
    n_slab = lb // LANES

    def gates(c, carry):
        r = pl.multiple_of(c * ch, ch)
        blk = pad[pl.ds(r, ch + 2 * halo), :]
        xc = cbias
        for k in range(CONV_W):
            o = halo - CONV_PAD_L + k
            xc = xc + blk[o:o + ch] * cw[k:k + 1]
        xc_bf = xc.astype(BF16)
        g = [jnp.dot(xc_bf[:, j * bw:(j + 1) * bw], half_wg[j], preferred_element_type=F32)
             for j in range(lb // bw)]
        pre = lambda k: jnp.concatenate([gj[:, k * bw:(k + 1) * bw] for gj in g], axis=1) + half_bias[k:k + 1]
        half_xc = 0.5 * xc
        for d, (a_ref, b_ref) in enumerate(((a_f, b_f), (a_b, b_b))):
            u = jnp.tanh(pre(2 * d)) + 1.0
            t_i = jnp.tanh(pre(2 * d + 1))
            a = jnp.exp2(u * c_exp2[d:d + 1])
            q = jnp.tanh(u * c_nla[d:d + 1]) * (1.0 + a * a)
            b = (q * lax.rsqrt(jnp.maximum(q, F32_TINY))) * (t_i * half_xc + half_xc)
            for p in range(ch // piece):
                rows = seg_rows(r + p * piece)
                for j in range(n_slab):
                    a_ref[j, rows, :] = a[p * piece:(p + 1) * piece, j * LANES:(j + 1) * LANES]
                    b_ref[j, rows, :] = b[p * piece:(p + 1) * piece, j * LANES:(j + 1) * LANES]
        return carry

    lax.fori_loop(0, t_len // ch, gates, 0)

    def scan(i, carry):
        rows = (pl.ds(pl.multiple_of(i * SUBLANES, SUBLANES), SUBLANES),
                pl.ds(pl.multiple_of((seg - 1 - i) * SUBLANES, SUBLANES), SUBLANES))
        out = []
        for k, (h, prod) in enumerate(carry):
            d, j = divmod(k, n_slab)
            a_ref, b_ref = ((a_f, b_f), (a_b, b_b))[d]
            a = a_ref[j, rows[d], :]
            h = a * h + b_ref[j, rows[d], :]
            prod = a * prod
            b_ref[j, rows[d], :] = h
            a_ref[j, rows[d], :] = prod
            out.append((h, prod))
        return tuple(out)

    start = (jnp.zeros((SUBLANES, LANES), F32), jnp.ones((SUBLANES, LANES), F32))
    ends = lax.fori_loop(0, seg, scan, (start,) * (2 * n_slab), unroll=8)

    enter = []
    for k, (h, prod) in enumerate(ends):
        d, j = divmod(k, n_slab)
        states = [h0_ref[0, d:d + 1, j * LANES:(j + 1) * LANES]]
        for r in (range(SUBLANES) if d == 0 else reversed(range(SUBLANES))):
            states.append(h[r:r + 1] + prod[r:r + 1] * states[-1])
        fin_ref[0, d:d + 1, j * LANES:(j + 1) * LANES] = states[SUBLANES]
        order = states[:SUBLANES] if d == 0 else states[:SUBLANES][::-1]
        enter.append(jnp.concatenate(order, axis=0))

    def fix(i, carry):
        rows = pl.ds(pl.multiple_of(i * SUBLANES, SUBLANES), SUBLANES)
        for j in range(n_slab):
            b_f[j, rows, :] = ((b_f[j, rows, :] + a_f[j, rows, :] * enter[j])
                               + (b_b[j, rows, :] + a_b[j, rows, :] * enter[n_slab + j]))
        return carry

    lax.fori_loop(0, seg, fix, 0, unroll=8)

    def emit(c, carry):
        r = pl.multiple_of(c * ch, ch)
        gt = gate_ref[pl.ds(r, ch), :]
        cdf = 0.5 * (1.0 + jnp.tanh(math.sqrt(2.0 / math.pi) * (gt + 0.044715 * (gt * gt * gt))))
        y = gt * cdf
        for p in range(ch // piece):
            rows = seg_rows(r + p * piece)
            total = jnp.concatenate([b_f[j, rows, :] for j in range(n_slab)], axis=1)
            y_ref[pl.ds(r + p * piece, piece), :] = (total * y[p * piece:(p + 1) * piece]).astype(BF16)
        return carry

    lax.fori_loop(0, t_len // ch, emit, 0)


def _lru_core(xr, gate, conv_w, conv_b, wg, bias4, lam, h0, *, seq_len):
    n, w = xr.shape
    n_seq = n // seq_len
    bw = w // LRU_BLOCKS
    block_bytes = seq_len * bw * (5 * 4 + 2 * 2 * 4 + 2 * 2)
    step_blocks = LRU_MAX_STEP_BLOCKS
    while step_blocks * block_bytes > LRU_CORE_VMEM_MIB * MIB:
        step_blocks //= 2
    lb = step_blocks * bw
    tok = lambda s, c: (s, c)
    chan = lambda s, c: (0, c)
    return pl.pallas_call(
        _lru_core_kernel,
        grid=(n_seq, LRU_BLOCKS // step_blocks),
        in_specs=[
            pl.BlockSpec((seq_len, lb), tok),
            pl.BlockSpec((seq_len, lb), tok),
            pl.BlockSpec((CONV_W, lb), chan),
            pl.BlockSpec((1, lb), chan),
            pl.BlockSpec((step_blocks, bw, 4 * bw), lambda s, c: (c, 0, 0)),
            pl.BlockSpec((4, lb), chan),
            pl.BlockSpec((2, lb), chan),
            pl.BlockSpec((1, 2, lb), lambda s, c: (s, 0, c)),
        ],
        out_specs=[pl.BlockSpec((seq_len, lb), tok),
                   pl.BlockSpec((1, 2, lb), lambda s, c: (s, 0, c))],
        out_shape=[jax.ShapeDtypeStruct((n, w), BF16), jax.ShapeDtypeStruct((n_seq, 2, w), F32)],
        scratch_shapes=[pltpu.VMEM((seq_len + 2 * SUBLANES, lb), F32)]
        + [pltpu.VMEM((lb // LANES, seq_len, LANES), F32)] * 4,
        compiler_params=_params(2, LRU_CORE_VMEM_MIB + 4),
        name=f"lru_core_{seq_len}",
    )(xr, gate, conv_w, conv_b, wg, bias4, lam, h0)


def _mix_out_kernel(a_ref, x_ref, mod_ref, gain_ref, w_ref, wrh_ref, wrl_ref,
                    x1_ref, h2_ref, lg_ref, *, mod_row):
    d = x_ref.shape[1]
    m = mod_row(pl.program_id(0), x_ref.shape[0])
    g_mix = mod_ref[pl.ds(m, 1), pl.ds(2 * d, d)]
    shift = mod_ref[pl.ds(m, 1), pl.ds(3 * d, d)]
    scale = mod_ref[pl.ds(m, 1), pl.ds(4 * d, d)]
    op = jnp.dot(a_ref[...], w_ref[...], preferred_element_type=F32)
    x1 = x_ref[...] + g_mix * op
    x1_ref[...] = x1
    h2 = _modulated_norm(x1, gain_ref[...], shift, scale)
    h_hi, h_lo = _split_bf16(h2)
    _store_rows(h2_ref, h2, 0)
    nt = functools.partial(lax.dot_general, dimension_numbers=(((1,), (1,)), ((), ())),
                           preferred_element_type=F32)
    wr_hi = wrh_ref[0]
    lg_ref[...] = nt(wr_hi, h_hi) + (nt(wrl_ref[0], h_hi) + nt(wr_hi, h_lo))


def _mix_out(a, x, mod, gain, w_out, wr_hi, wr_lo, layer, mod_row, token_tiled):
    n, d = x.shape
    e = wr_hi.shape[1]
    row = lambda i: (i, 0)
    fixed = lambda i: (0, 0)
    router = pl.BlockSpec((1,) + wr_hi.shape[1:], lambda i: (layer, 0, 0))
    tpr = _tiles_per_row(d)
    if token_tiled:
        h2_spec = pl.BlockSpec((ROW_BLOCK * tpr, LANES), row)
        h2_shape = jax.ShapeDtypeStruct((n * tpr, LANES), F32)
    else:
        h2_spec = pl.BlockSpec((ROW_BLOCK, d), row)
        h2_shape = jax.ShapeDtypeStruct((n, d), BF16)
    return pl.pallas_call(
        functools.partial(_mix_out_kernel, mod_row=mod_row),
        grid=(n // ROW_BLOCK,),
        in_specs=[
            pl.BlockSpec((ROW_BLOCK, a.shape[1]), row),
            pl.BlockSpec((ROW_BLOCK, d), row),
            pl.BlockSpec(mod.shape, fixed),
            pl.BlockSpec((1, d), fixed),
            pl.BlockSpec(w_out.shape, fixed),
            router,
            router,
        ],
        out_specs=[pl.BlockSpec((ROW_BLOCK, d), row), h2_spec,
                   pl.BlockSpec((e, ROW_BLOCK), lambda i: (0, i))],
        out_shape=[jax.ShapeDtypeStruct((n, d), F32), h2_shape, jax.ShapeDtypeStruct((e, n), F32)],
        compiler_params=_params(1, 48),
        name="mix_out",
    )(a, x, mod, gain, w_out, wr_hi, wr_lo)


def _lane_sum(tiles):
    acc = tiles[0]
    for t in tiles[1:]:
        acc = acc + t
    return jnp.sum(acc, axis=1, keepdims=True)


def _exclusive_cumsum(flags, upper):
    out = []
    off = jnp.zeros((flags[0].shape[0], 1), F32)
    for f in flags:
        out.append(jnp.dot(f.astype(BF16), upper, preferred_element_type=F32) + off)
        off = off + jnp.sum(f, axis=1, keepdims=True)
    return out


def _route_group(lg_ref, n_seq, t_len, idx_ref, g_ref):
    cap = CAPACITY_FACTOR * t_len // N_EXPERTS
    affs = []
    for s in range(n_seq):
        lg = lg_ref[:, s * t_len:(s + 1) * t_len]
        ex = jnp.exp(lg - jnp.max(lg, axis=0, keepdims=True))
        affs.append(ex / jnp.sum(ex, axis=0, keepdims=True))
    aff = jnp.concatenate(affs, axis=0)
    n_rows = aff.shape[0]
    nt = t_len // LANES
    g = [aff[:, j * LANES:(j + 1) * LANES] for j in range(nt)]

    kth_bits = jnp.zeros((n_rows, 1), I32)
    for bit in range(30, -1, -1):
        cand = kth_bits | (1 << bit)
        cand_f = pltpu.bitcast(cand, F32)
        cnt = _lane_sum([jnp.where(t >= cand_f, 1, 0) for t in g])
        kth_bits = jnp.where(cnt >= cap, cand, kth_bits)
    kth = pltpu.bitcast(kth_bits, F32)

    lane = lax.broadcasted_iota(I32, (LANES, LANES), 0)
    upper = jnp.where(lane < lax.broadcasted_iota(I32, (LANES, LANES), 1), 1.0, 0.0).astype(BF16)
    gt = [t > kth for t in g]
    eq = [t == kth for t in g]
    need = (cap - _lane_sum([jnp.where(m, 1, 0) for m in gt])).astype(F32)
    eq_rank = _exclusive_cumsum([jnp.where(m, 1.0, 0.0) for m in eq], upper)
    sel = [jnp.logical_or(gt[j], jnp.logical_and(eq[j], eq_rank[j] < need)) for j in range(nt)]
    pos = _exclusive_cumsum([jnp.where(m, 1.0, 0.0) for m in sel], upper)

    lane_r = lax.broadcasted_iota(I32, (n_rows, LANES), 1)
    d = [jnp.where(sel[j], lane_r + j * LANES - pos[j].astype(I32), -1) for j in range(nt)]
    for k in range(t_len.bit_length() - 1):
        s = 1 << k
        if s < LANES:
            d_rot = [pltpu.roll(x, LANES - s, 1) for x in d]
            g_rot = [pltpu.roll(x, LANES - s, 1) for x in g]
            same = lane_r < LANES - s
            d_in = [jnp.where(same, d_rot[j], d_rot[(j + 1) % nt]) for j in range(nt)]
            g_in = [jnp.where(same, g_rot[j], g_rot[(j + 1) % nt]) for j in range(nt)]
        else:
            q = s // LANES
            d_in = [d[(j + q) % nt] for j in range(nt)]
            g_in = [g[(j + q) % nt] for j in range(nt)]
        new_d, new_g = [], []
        for j in range(nt):
            move = jnp.logical_and(d_in[j] >= 0, ((d_in[j] >> k) & 1) == 1)
            stay = jnp.logical_and(d[j] >= 0, ((d[j] >> k) & 1) == 0)
            new_d.append(jnp.where(move, d_in[j], jnp.where(stay, d[j], -1)))
            new_g.append(jnp.where(move, g_in[j], g[j]))
        d, g = new_d, new_g

    for j in range(idx_ref.shape[1] // LANES):
        idx_ref[:, j * LANES:(j + 1) * LANES] = lane_r + j * LANES + d[j]
        g_ref[:, j * LANES:(j + 1) * LANES] = g[j]


def _route_kernel(lgp_ref, lgs_ref, idx_p_ref, g_p_ref, idx_s_ref, g_s_ref, *, p_len, s_len):
    _route_group(lgp_ref, lgp_ref.shape[1] // p_len, p_len, idx_p_ref, g_p_ref)
    _route_group(lgs_ref, lgs_ref.shape[1] // s_len, s_len, idx_s_ref, g_s_ref)


def _route(lg_p, lg_s, p_len, s_len):
    e = lg_p.shape[0]
    np_seq = lg_p.shape[1] // p_len
    ns_seq = lg_s.shape[1] // s_len
    cap_p = CAPACITY_FACTOR * p_len // N_EXPERTS
    cap_s = CAPACITY_FACTOR * s_len // N_EXPERTS
    wp = max(cap_p, LANES)
    ws = max(cap_s, LANES)
    idx_p, g_p, idx_s, g_s = pl.pallas_call(
        functools.partial(_route_kernel, p_len=p_len, s_len=s_len),
        out_shape=[jax.ShapeDtypeStruct((np_seq * e, wp), I32),
                   jax.ShapeDtypeStruct((np_seq * e, wp), F32),
                   jax.ShapeDtypeStruct((ns_seq * e, ws), I32),
                   jax.ShapeDtypeStruct((ns_seq * e, ws), F32)],
        compiler_params=pltpu.CompilerParams(vmem_limit_bytes=40 * MIB),
        name="route",
    )(lg_p, lg_s)
    return (idx_p[:, :cap_p].reshape(np_seq, e * cap_p), g_p[:, :cap_p].reshape(np_seq, e * cap_p),
            idx_s[:, :cap_s].reshape(-1), g_s[:, :cap_s].reshape(-1))


def _gather_onehot_kernel(idx_ref, h_ref, o_ref):
    t_len = h_ref.shape[0]
    n_exp, cap, _ = o_ref.shape
    idx = idx_ref[0]
    hit = jnp.where(idx == lax.broadcasted_iota(I32, (t_len, idx.shape[1]), 0), 1.0, 0.0).astype(BF16)
    xs = lax.dot_general(hit, h_ref[...], (((0,), (0,)), ((), ())), preferred_element_type=F32).astype(BF16)
    for e in range(n_exp):
        o_ref[e] = xs[e * cap:(e + 1) * cap]


def _gather_onehot(idx_row, h, *, seq_len):
    n, d = h.shape
    n_seq, _, slots = idx_row.shape
    cap = slots // N_EXPERTS
    return pl.pallas_call(
        _gather_onehot_kernel,
        grid=(n_seq,),
        in_specs=[pl.BlockSpec((1, 1, slots), lambda s: (s, 0, 0)),
                  pl.BlockSpec((seq_len, d), lambda s: (s, 0))],
        out_specs=pl.BlockSpec((N_EXPERTS, cap, d), lambda s: (0, s, 0)),
        out_shape=jax.ShapeDtypeStruct((N_EXPERTS, n_seq * cap, d), BF16),
        compiler_params=_params(1, 32),
        name="gather_onehot",
    )(idx_row, h)


def _gather_rows_kernel(idx_ref, h_ref, o_ref, tiles, *, cap, n_exp):
    tpr = _tiles_per_row(o_ref.shape[2])
    for el in range(ROW_LOOP_EXPERTS):
        base = (pl.program_id(0) * n_exp + pl.program_id(1) * ROW_LOOP_EXPERTS + el) * cap

        def body(c, carry, base=base):
            src = pl.multiple_of(idx_ref[base + c], tpr)
            tiles[pl.ds(pl.multiple_of(c * tpr, tpr), tpr), :] = h_ref[pl.ds(src, tpr), :]
            return carry

        lax.fori_loop(0, cap, body, 0, unroll=8)
        for s in range(tpr):
            o_ref[el, :, s * LANES:(s + 1) * LANES] = tiles[pl.ds(s, cap, stride=tpr), :].astype(BF16)


def _gather_rows(idx, h, d, *, seq_len):
    tpr = _tiles_per_row(d)
    n_seq = h.shape[0] // (seq_len * tpr)
    cap = CAPACITY_FACTOR * seq_len // N_EXPERTS
    return pl.pallas_call(
        functools.partial(_gather_rows_kernel, cap=cap, n_exp=N_EXPERTS),
        grid_spec=pltpu.PrefetchScalarGridSpec(
            num_scalar_prefetch=1,
            grid=(n_seq, N_EXPERTS // ROW_LOOP_EXPERTS),
            in_specs=[pl.BlockSpec((seq_len * tpr, LANES), lambda s, e, idx: (s, 0))],
            out_specs=pl.BlockSpec((ROW_LOOP_EXPERTS, cap, d), lambda s, e, idx: (e, s, 0)),
            scratch_shapes=[pltpu.VMEM((cap * tpr, LANES), F32)],
        ),
        out_shape=jax.ShapeDtypeStruct((N_EXPERTS, n_seq * cap, d), BF16),
        compiler_params=_params(2, 48),
        name="gather_rows",
    )(idx, h)


def _ffn_kernel(xp_ref, xs_ref, gp_ref, gs_ref, wg_ref, wu_ref, wd_ref, yp_ref, ys_ref):
    rc = 512
    f = pl.program_id(1)
    last = f == pl.num_programs(1) - 1

    def body(first):
        wg = wg_ref[0, 0].astype(BF16)
        wu = wu_ref[0, 0].astype(BF16)
        wd = wd_ref[0, 0].astype(BF16)
        for x_ref, g_ref, y_ref in ((xp_ref, gp_ref, yp_ref), (xs_ref, gs_ref, ys_ref)):
            for r in range(0, x_ref.shape[1], rc):
                x = x_ref[0, r:r + rc, :]
                hg = jnp.dot(x, wg, preferred_element_type=F32)
                hu = jnp.dot(x, wu, preferred_element_type=F32)
                hid = ((hg * jax.nn.sigmoid(hg)) * hu).astype(BF16)
                g_hi, g_lo = _split_bf16(g_ref[0, :, r:r + rc])
                ones = jnp.ones((g_hi.shape[0], LANES), BF16)
                tn = functools.partial(lax.dot_general, dimension_numbers=(((0,), (0,)), ((), ())),
                                       preferred_element_type=F32)
                weight = jnp.where(last, tn(g_hi, ones) + tn(g_lo, ones), 1.0)
                weight = jnp.concatenate([weight] * (y_ref.shape[2] // LANES), axis=1)
                y = jnp.dot(hid, wd, preferred_element_type=F32)
                if not first:
                    y = y_ref[0, r:r + rc, :] + y
                y_ref[0, r:r + rc, :] = y * weight

    pl.when(f == 0)(functools.partial(body, True))
    pl.when(f != 0)(functools.partial(body, False))


def _ffn(xs_p, xs_s, g_p, g_s, w_gate, w_up, w_down, layer):
    n_exp, rp, d = xs_p.shape
    rs = xs_s.shape[1]
    ff = w_gate.shape[3]
    fc = 1024
    return pl.pallas_call(
        _ffn_kernel,
        grid=(n_exp, ff // fc),
        in_specs=[
            pl.BlockSpec((1, rp, d), lambda e, f: (e, 0, 0)),
            pl.BlockSpec((1, rs, d), lambda e, f: (e, 0, 0)),
            pl.BlockSpec((1, SUBLANES, rp), lambda e, f: (e, 0, 0)),
            pl.BlockSpec((1, SUBLANES, rs), lambda e, f: (e, 0, 0)),
            pl.BlockSpec((1, 1, d, fc), lambda e, f: (layer, e, 0, f)),
            pl.BlockSpec((1, 1, d, fc), lambda e, f: (layer, e, 0, f)),
            pl.BlockSpec((1, 1, fc, d), lambda e, f: (layer, e, f, 0)),
        ],
        out_specs=[pl.BlockSpec((1, rp, d), lambda e, f: (e, 0, 0)),
                   pl.BlockSpec((1, rs, d), lambda e, f: (e, 0, 0))],
        out_shape=[jax.ShapeDtypeStruct((n_exp, rp, d), F32),
                   jax.ShapeDtypeStruct((n_exp, rs, d), F32)],
        compiler_params=_params(2, 58),
        name="expert_ffn",
    )(xs_p, xs_s, g_p, g_s, w_gate, w_up, w_down)


def _combine_onehot_kernel(*refs, mod_row, final):
    if final:
        idx_ref, ye_ref, x_ref, pmod_ref, gain_ref, o_ref = refs
    else:
        idx_ref, ye_ref, x_ref, pmod_ref, o_ref = refs
    t_len, d = o_ref.shape
    n_exp = ye_ref.shape[0]
    idx = idx_ref[0]
    hit = jnp.where(idx == lax.broadcasted_iota(I32, (t_len, idx.shape[1]), 0), 1.0, 0.0).astype(BF16)
    y_hi, y_lo = _split_bf16(jnp.concatenate([ye_ref[e] for e in range(n_exp)], axis=0))
    acc = jnp.dot(hit, y_hi, preferred_element_type=F32) + jnp.dot(hit, y_lo, preferred_element_type=F32)
    x = x_ref[...] + pmod_ref[pl.ds(mod_row(pl.program_id(0), t_len), 1), pl.ds(5 * d, d)] * acc
    if final:
        x = (x * lax.rsqrt(jnp.mean(x * x, axis=-1, keepdims=True) + EPS)) * gain_ref[...]
    o_ref[...] = x


def _combine_onehot(idx_row, ye, x, pmod, final_gain, *, seq_len, mod_row):
    n_exp, rows, d = ye.shape
    n_seq, _, slots = idx_row.shape
    cap = slots // n_exp
    in_specs = [pl.BlockSpec((1, 1, slots), lambda s: (s, 0, 0)),
                pl.BlockSpec((n_exp, cap, d), lambda s: (0, s, 0)),
                pl.BlockSpec((seq_len, d), lambda s: (s, 0)),
                pl.BlockSpec(pmod.shape, lambda s: (0, 0))]
    args = [idx_row, ye, x, pmod]
    if final_gain is not None:
        in_specs.append(pl.BlockSpec((1, d), lambda s: (0, 0)))
        args.append(final_gain)
    return pl.pallas_call(
        functools.partial(_combine_onehot_kernel, mod_row=mod_row, final=final_gain is not None),
        grid=(n_seq,),
        in_specs=in_specs,
        out_specs=pl.BlockSpec((seq_len, d), lambda s: (s, 0)),
        out_shape=jax.ShapeDtypeStruct((n_seq * seq_len, d), F32),
        compiler_params=_params(1, 32),
        name="combine_onehot",
    )(*args)


COMBINE_GROUP = 16
ROW_LOOP_EXPERTS = 2


def _combine_rows_kernel(idx_ref, ye_ref, o_ref, tiles, *, cap, n_exp):
    e = pl.program_id(1)
    d = ye_ref.shape[2]
    tpr = _tiles_per_row(d)
    ch = 2048

    @pl.when(e == 0)
    def _():
        def zero(c, carry):
            o_ref[pl.ds(pl.multiple_of(c * ch, ch), ch), :] = jnp.zeros((ch, LANES), F32)
            return carry
        lax.fori_loop(0, o_ref.shape[0] // ch, zero, 0)

    for el in range(ROW_LOOP_EXPERTS):
        for s in range(tpr):
            tiles[pl.ds(s, cap, stride=tpr), :] = ye_ref[el, :, s * LANES:(s + 1) * LANES]

        base = (pl.program_id(0) * n_exp + e * ROW_LOOP_EXPERTS + el) * cap

        def body(i, carry, base=base):
            c0 = i * COMBINE_GROUP
            dst = [pl.ds(pl.multiple_of(idx_ref[base + c0 + k], tpr), tpr) for k in range(COMBINE_GROUP)]
            vals = [o_ref[dst[k], :] + tiles[pl.ds(pl.multiple_of((c0 + k) * tpr, tpr), tpr), :]
                    for k in range(COMBINE_GROUP)]
            for k in range(COMBINE_GROUP):
                o_ref[dst[k], :] = vals[k]
            return carry

        lax.fori_loop(0, cap // COMBINE_GROUP, body, 0)


def _combine_rows(idx, ye, *, seq_len):
    n_exp, rows, d = ye.shape
    tpr = _tiles_per_row(d)
    cap = CAPACITY_FACTOR * seq_len // N_EXPERTS
    n_seq = rows // cap
    return pl.pallas_call(
        functools.partial(_combine_rows_kernel, cap=cap, n_exp=n_exp),
        grid_spec=pltpu.PrefetchScalarGridSpec(
            num_scalar_prefetch=1,
            grid=(n_seq, n_exp // ROW_LOOP_EXPERTS),
            in_specs=[pl.BlockSpec((ROW_LOOP_EXPERTS, cap, d), lambda s, e, i: (e, s, 0))],
            out_specs=pl.BlockSpec((seq_len * tpr, LANES), lambda s, e, i: (s, 0)),
            scratch_shapes=[pltpu.VMEM((cap * tpr, LANES), F32)],
        ),
        out_shape=jax.ShapeDtypeStruct((n_seq * seq_len * tpr, LANES), F32),
        compiler_params=_params(2, 48),
        name="combine_rows",
    )(idx, ye)


def _head_norm(x, gain):
    return x * lax.rsqrt(jnp.mean(x * x, axis=-1, keepdims=True) + EPS) * gain


def _qkv_kernel(*refs, rope, has_prev, mod_row, n_heads, n_kv):
    refs = list(refs)
    x_ref = refs.pop(0)
    prev = None
    if has_prev:
        acc_ref, pmod_ref = refs.pop(0), refs.pop(0)
    mod_ref, gain_ref, w_ref, qg_ref, kg_ref = refs[:5]
    refs = refs[5:]
    if rope:
        cos_ref, sin_ref = refs.pop(0), refs.pop(0)
    if has_prev:
        prev = (acc_ref, pmod_ref, refs.pop(0))
    q_ref, k_ref, v_ref = refs[:3]
    d = x_ref.shape[1]
    hd = HEAD_DIM
    m = mod_row(pl.program_id(0), x_ref.shape[0])
    x = _residual_stream(x_ref, prev, m)
    shift = mod_ref[pl.ds(m, 1), pl.ds(0, d)]
    scale = mod_ref[pl.ds(m, 1), pl.ds(d, d)]
    h = _modulated_norm(x, gain_ref[...], shift, scale).astype(BF16)
    qg = qg_ref[...]
    kg = kg_ref[...]
    q_scale = math.log2(math.e) * hd ** -0.5
    half = x_ref.shape[0] // 2
    for r0 in (0, half):
        rows = slice(r0, r0 + half)
        qkv = jnp.dot(h[rows], w_ref[...], preferred_element_type=F32)
        if rope:
            cos = cos_ref[rows, :]
            sin = sin_ref[rows, :]

            def rot(xh, cos=cos, sin=sin):
                return xh * cos + pltpu.roll(xh, hd // 2, 1) * sin
        else:
            rot = lambda xh: xh

        for i in range(n_heads):
            qh = rot(_head_norm(qkv[:, i * hd:(i + 1) * hd], qg)) * q_scale
            q_ref[rows, i * hd:(i + 1) * hd] = qh.astype(BF16)
        for i in range(n_kv):
            c0 = (n_heads + i) * hd
            kh = _head_norm(qkv[:, c0:c0 + hd], kg)
            if not rope:
                refs[3][rows, i * hd:(i + 1) * hd] = kh
            k_ref[rows, i * hd:(i + 1) * hd] = rot(kh).astype(BF16)
        v = qkv[:, (n_heads + n_kv) * hd:]
        v_ref[rows, :] = v.astype(BF16)
        if not rope:
            refs[4][rows, :] = v


def _qkv(x, prev, mod, gain, w_qkv, q_gain, k_gain, tables, *, seq_len, mod_row, n_heads, n_kv):
    n, d = x.shape
    hd = HEAD_DIM
    row = lambda i: (i, 0)
    fixed = lambda i: (0, 0)
    rope = tables is not None
    rb = QKV_ROW_BLOCK
    in_specs = [pl.BlockSpec((rb, d), row)]
    args = [x]
    if prev is not None:
        in_specs += [_rows_spec(prev[0], d, rb), pl.BlockSpec(prev[1].shape, fixed)]
        args += list(prev)
    in_specs += [
        pl.BlockSpec(mod.shape, fixed),
        pl.BlockSpec((1, d), fixed),
        pl.BlockSpec(w_qkv.shape, fixed),
        pl.BlockSpec((1, hd), fixed),
        pl.BlockSpec((1, hd), fixed),
    ]
    args += [mod, gain, w_qkv, q_gain, k_gain]
    if rope:
        per_seq = seq_len // rb
        in_specs += [pl.BlockSpec((rb, hd), lambda i: (i % per_seq, 0))] * 2
        args += list(tables)
    out_specs, out_shape = [], []
    if prev is not None:
        out_specs.append(pl.BlockSpec((rb, d), row))
        out_shape.append(jax.ShapeDtypeStruct((n, d), F32))
    out_specs += [pl.BlockSpec((rb, n_heads * hd), row),
                  pl.BlockSpec((rb, n_kv * hd), row),
                  pl.BlockSpec((rb, n_kv * hd), row)]
    out_shape += [jax.ShapeDtypeStruct((n, n_heads * hd), BF16),
                  jax.ShapeDtypeStruct((n, n_kv * hd), BF16),
                  jax.ShapeDtypeStruct((n, n_kv * hd), BF16)]
    if not rope:
        out_specs += [pl.BlockSpec((rb, n_kv * hd), row)] * 2
        out_shape += [jax.ShapeDtypeStruct((n, n_kv * hd), F32)] * 2
    outs = pl.pallas_call(
        functools.partial(_qkv_kernel, rope=rope, has_prev=prev is not None, mod_row=mod_row,
                          n_heads=n_heads, n_kv=n_kv),
        grid=(n // rb,),
        in_specs=in_specs,
        out_specs=out_specs,
        out_shape=out_shape,
        compiler_params=_params(1, 48),
        name="qkv_rope" if rope else "qkv",
    )(*args)
    return list(outs) if prev is not None else [x] + list(outs)


def _attn_kernel(*refs, group, chunk, has_cache):
    if has_cache:
        q_ref, k_ref, v_ref, kc_ref, vc_ref, o_ref = refs
    else:
        q_ref, k_ref, v_ref, o_ref = refs
    hd = HEAD_DIM
    qb = q_ref.shape[0]
    rows = group * qb
    q = jnp.concatenate([q_ref[:, g * hd:(g + 1) * hd] for g in range(group)], axis=0)
    sources = [(k_ref, v_ref, s0, chunk) for s0 in range(0, k_ref.shape[0], chunk)]
    if has_cache:
        sources.append((kc_ref.at[0], vc_ref.at[0], 0, kc_ref.shape[1]))
    m = jnp.full((rows, 1), -jnp.inf, F32)
    acc = jnp.zeros((rows, 2 * hd), F32)
    for kr, vr, s0, size in sources:
        s = lax.dot_general(q, kr[s0:s0 + size, :], (((1,), (1,)), ((), ())),
                            preferred_element_type=F32)
        m_new = jnp.maximum(m, jnp.max(s, axis=-1, keepdims=True))
        p = jnp.exp2(s - m_new).astype(BF16)
        v_ones = jnp.concatenate([vr[s0:s0 + size, :], jnp.ones((size, hd), BF16)], axis=1)
        acc = jnp.exp2(m - m_new) * acc + jnp.dot(p, v_ones, preferred_element_type=F32)
        m = m_new
    o = acc[:, :hd] / acc[:, hd:]
    o_ref[...] = jnp.concatenate([o[g * qb:(g + 1) * qb] for g in range(group)], axis=1).astype(BF16)


def _attention(q, k, v, cache, *, seq_len, qb, chunk, n_heads, n_kv):
    n = q.shape[0]
    hd = HEAD_DIM
    group = n_heads // n_kv
    nq = seq_len // qb
    in_specs = [
        pl.BlockSpec((qb, group * hd), lambda b, h, i: (b * nq + i, h)),
        pl.BlockSpec((seq_len, hd), lambda b, h, i: (b, h)),
        pl.BlockSpec((seq_len, hd), lambda b, h, i: (b, h)),
    ]
    args = [q, k, v]
    if cache is not None:
        past = cache[0].shape[1]
        in_specs += [pl.BlockSpec((1, past, hd), lambda b, h, i: (b, 0, h))] * 2
        args += list(cache)
    return pl.pallas_call(
        functools.partial(_attn_kernel, group=group, chunk=chunk, has_cache=cache is not None),
        grid=(n // seq_len, n_kv, nq),
        in_specs=in_specs,
        out_specs=pl.BlockSpec((qb, group * hd), lambda b, h, i: (b * nq + i, h)),
        out_shape=jax.ShapeDtypeStruct((n, n_heads * hd), BF16),
        compiler_params=_params(3, 48),
        name=f"attention_{seq_len}",
    )(*args)


def _final_norm_kernel(x_ref, acc_ref, pmod_ref, gain_ref, o_ref, *, mod_row):
    n_rows, d = x_ref.shape
    m = mod_row(pl.program_id(0), x_ref.shape[0])
    x = x_ref[...] + pmod_ref[pl.ds(m, 1), pl.ds(5 * d, d)] * _load_rows(acc_ref, n_rows, d)
    o_ref[...] = (x * lax.rsqrt(jnp.mean(x * x, axis=-1, keepdims=True) + EPS)) * gain_ref[...]


def _final_norm(x, acc, pmod, gain, mod_row):
    n, d = x.shape
    row = lambda i: (i, 0)
    fixed = lambda i: (0, 0)
    return pl.pallas_call(
        functools.partial(_final_norm_kernel, mod_row=mod_row),
        grid=(n // ROW_BLOCK,),
        in_specs=[pl.BlockSpec((ROW_BLOCK, d), row), _rows_spec(acc, d, ROW_BLOCK),
                  pl.BlockSpec(pmod.shape, fixed), pl.BlockSpec((1, d), fixed)],
        out_specs=pl.BlockSpec((ROW_BLOCK, d), row),
        out_shape=jax.ShapeDtypeStruct((n, d), F32),
        compiler_params=_params(1, 32),
        name="final_norm",
    )(x, acc, pmod, gain)


def _rope_head_order(x):
    q = HEAD_DIM // 4
    row1, row2, col1, col2 = (x[..., i * q:(i + 1) * q] for i in range(4))
    return jnp.concatenate([row1, col1, row2, col2], axis=-1)


def _rope_tables(seq_len):
    axis = HEAD_DIM // 2
    t = np.arange(seq_len)
    inv = ROPE_BASE ** (-np.arange(axis // 2, dtype=np.float64) * 2.0 / axis)
    ang = np.concatenate([(t // GRID_W)[:, None] * inv, (t % GRID_W)[:, None] * inv], axis=1)
    c, s = np.cos(ang), np.sin(ang)
    return (jnp.asarray(np.concatenate([c, c], axis=1), F32), jnp.asarray(np.concatenate([-s, s], axis=1), F32))


def kernel(x_prompt, x_sample, state_lru, cache_k, cache_v, c, c_ctx, w_mod, b_mod, norm_gain, final_gain,
           w_lru_in, lru_conv_w, lru_conv_b, lru_wa, lru_ba, lru_wx, lru_bx, lru_lambda, w_lru_out,
           w_qkv, q_norm, k_norm, w_attn_out, w_router, w_exp_gate, w_exp_up, w_exp_down):
    batch, seq, d = x_prompt.shape
    dec_batch, dec_seq, _ = x_sample.shape
    depth = w_mod.shape[0]
    n_kv = cache_k.shape[3]
    n_heads = w_attn_out.shape[1] // HEAD_DIM
    heads = dict(n_heads=n_heads, n_kv=n_kv)

    cond = jnp.zeros((SUBLANES, d), F32).at[0].set(c_ctx).at[1:1 + dec_batch].set(c)
    mod_all = _mod_vectors(cond, w_mod, b_mod)

    xs = [x_prompt.reshape(batch * seq, d), x_sample.reshape(dec_batch * dec_seq, d)]
    lens = [seq, dec_seq]
    mod_rows = [lambda i, rows: 0, lambda i, rows: 1 + (i * rows) // dec_seq]
    prevs = [None, None]

    fg = final_gain.reshape(1, d)
    wr_hi, wr_lo = _split_bf16(jnp.swapaxes(w_router, 1, 2))
    new_lru, new_k, new_v = [], [], []
    for l in range(depth):
        mod = mod_all[l]
        gain_a = norm_gain[l, 0].reshape(1, d)
        gain_c = norm_gain[l, 1].reshape(1, d)
        mixed = []
        if l % 2 == 0:
            li = l // 2
            w_in = w_lru_in[li].astype(BF16)
            wg = (0.5 * jnp.concatenate([lru_wa[li, 0], lru_wx[li, 0], lru_wa[li, 1], lru_wx[li, 1]],
                                        axis=2)).astype(BF16)
            bias4 = 0.5 * jnp.stack([lru_ba[li, 0], lru_bx[li, 0], lru_ba[li, 1], lru_bx[li, 1]])
            h0s = [jnp.zeros((batch, 2, w_in.shape[1] // 2), F32), state_lru[:, li]]
            for gi in range(2):
                xs[gi], gate, xr = _lru_in(xs[gi], prevs[gi], mod, gain_a, w_in, mod_rows[gi])
                y, fin = _lru_core(xr, gate, lru_conv_w[li], lru_conv_b[li].reshape(1, -1), wg, bias4,
                                   lru_lambda[li], h0s[gi], seq_len=lens[gi])
                mixed.append(y)
                if gi == 0:
                    new_lru.append(fin.astype(x_prompt.dtype))
            w_out = w_lru_out[li].astype(BF16)
        else:
            ai = l // 2
            w = w_qkv[ai].astype(BF16)
            qg = q_norm[ai].reshape(1, -1)
            kg = k_norm[ai].reshape(1, -1)
            xs[0], q_p, k_p, v_p, kf, vf = _qkv(xs[0], prevs[0], mod, gain_a, w, qg, kg, None,
                                                seq_len=seq, mod_row=mod_rows[0], **heads)
            n_qk = (n_heads + n_kv) * HEAD_DIM
            w_qk = _rope_head_order(w_qkv[ai][:, :n_qk].reshape(d, n_heads + n_kv, HEAD_DIM)).reshape(d, n_qk)
            w_r = jnp.concatenate([w_qk, w_qkv[ai][:, n_qk:]], axis=1).astype(BF16)
            xs[1], q_s, k_s, v_s = _qkv(xs[1], prevs[1], mod, gain_a, w_r, _rope_head_order(qg),
                                        _rope_head_order(kg), _rope_tables(dec_seq),
                                        seq_len=dec_seq, mod_row=mod_rows[1], **heads)
            new_k.append(kf.reshape(batch, seq, n_kv, HEAD_DIM))
            new_v.append(vf.reshape(batch, seq, n_kv, HEAD_DIM))
            past = cache_k.shape[2]
            kc = _rope_head_order(cache_k[:, ai]).reshape(dec_batch, past, n_kv * HEAD_DIM).astype(BF16)
            vc = cache_v[:, ai].reshape(dec_batch, past, n_kv * HEAD_DIM).astype(BF16)
            mixed.append(_attention(q_p, k_p, v_p, None, seq_len=seq, qb=seq, chunk=seq, **heads))
            mixed.append(_attention(q_s, k_s, v_s, (kc, vc), seq_len=dec_seq, qb=512, chunk=256, **heads))
            w_out = w_attn_out[ai].astype(BF16)

        x1_p, h2_p, lg_p = _mix_out(mixed[0], xs[0], mod, gain_c, w_out, wr_hi, wr_lo, l, mod_rows[0], False)
        x1_s, h2_s, lg_s = _mix_out(mixed[1], xs[1], mod, gain_c, w_out, wr_hi, wr_lo, l, mod_rows[1], True)
        idx_p, g_p, idx_s, g_s = _route(lg_p, lg_s, seq, dec_seq)
        xe_p = _gather_onehot(idx_p[:, None, :], h2_p, seq_len=seq)
        off_s = idx_s * _tiles_per_row(d)
        xe_s = _gather_rows(off_s, h2_s, d, seq_len=dec_seq)
        e_n = N_EXPERTS
        zero_rows = ((0, 0), (0, SUBLANES - 1), (0, 0))
        gc_p = jnp.pad(g_p.reshape(batch, e_n, -1).transpose(1, 0, 2).reshape(e_n, 1, -1), zero_rows)
        gc_s = jnp.pad(g_s.reshape(dec_batch, e_n, -1).transpose(1, 0, 2).reshape(e_n, 1, -1), zero_rows)
        ye_p, ye_s = _ffn(xe_p, xe_s, gc_p, gc_s, w_exp_gate, w_exp_up, w_exp_down, l)
        last = l == depth - 1
        x_p = _combine_onehot(idx_p[:, None, :], ye_p, x1_p, mod, fg if last else None, seq_len=seq,
                              mod_row=mod_rows[0])
        acc_s = _combine_rows(off_s, ye_s, seq_len=dec_seq)
        xs = [x_p, x1_s]
        prevs = [None, (acc_s, mod)]

    y_prompt = xs[0].reshape(batch, seq, d)
    y_sample = _final_norm(xs[1], *prevs[1], fg, mod_rows[1]).reshape(dec_batch, dec_seq, d)
    return (y_prompt, y_sample, jnp.stack(new_lru, axis=1), jnp.stack(new_k, axis=1), jnp.stack(new_v, axis=1))
```

```python
import functools
import math

import jax
import jax.numpy as jnp
import numpy as np
from jax import lax
from jax.experimental import pallas as pl
from jax.experimental.pallas import tpu as pltpu

F32 = jnp.float32
BF16 = jnp.bfloat16
I32 = jnp.int32

LANES = 128
SUBLANES = 8
MIB = 1024 * 1024
F32_TINY = float(np.finfo(np.float32).tiny)

RG_C = 8.0
CONV_W = 4
CONV_PAD_L = 2
ROPE_BASE = 10000.0
GRID_W = 64
EPS = 1e-6
N_EXPERTS = 16
CAPACITY_FACTOR = 2
HEAD_DIM = 128
LRU_BLOCKS = 8
LRU_MAX_STEP_BLOCKS = 4
LRU_CORE_VMEM_MIB = 44

ROW_BLOCK = 1024
QKV_ROW_BLOCK = 512


def _params(n_axes, vmem_mib):
    return pltpu.CompilerParams(
        dimension_semantics=("arbitrary",) * n_axes, vmem_limit_bytes=vmem_mib * MIB)


def _split_bf16(x):
    hi = x.astype(BF16)
    lo = (x - hi.astype(F32)).astype(BF16)
    return hi, lo


def _modulated_norm(x, gain, shift, scale):
    y = x * lax.rsqrt(jnp.mean(x * x, axis=-1, keepdims=True) + EPS)
    return (y * gain) * (1.0 + scale) + shift


def _tiles_per_row(d):
    return d // LANES


def _load_rows(ref, n_rows, d):
    if ref.shape == (n_rows, d):
        return ref[...]
    tpr = _tiles_per_row(d)
    return jnp.concatenate([ref[pl.ds(s, n_rows, stride=tpr), :] for s in range(tpr)], axis=1)


def _store_rows(ref, x, r0):
    n_rows, d = x.shape
    if ref.shape[1] == d:
        ref[r0:r0 + n_rows, :] = x.astype(ref.dtype)
        return
    tpr = _tiles_per_row(d)
    for s in range(tpr):
        ref[pl.ds(r0 * tpr + s, n_rows, stride=tpr), :] = x[:, s * LANES:(s + 1) * LANES]


def _rows_spec(arr, d, rows):
    if arr.shape[1] == d:
        return pl.BlockSpec((rows, d), lambda i: (i, 0))
    return pl.BlockSpec((rows * _tiles_per_row(d), LANES), lambda i: (i, 0))


def _residual_stream(x_ref, prev, m):
    if prev is None:
        return x_ref[...]
    acc_ref, pmod_ref, xo_ref = prev
    n_rows, d = x_ref.shape
    x = x_ref[...] + pmod_ref[pl.ds(m, 1), pl.ds(5 * d, d)] * _load_rows(acc_ref, n_rows, d)
    xo_ref[...] = x
    return x


def _mod_kernel(cond_ref, w_ref, b_ref, o_ref):
    c = cond_ref[...]
    a_hi, a_lo = _split_bf16(c * jax.nn.sigmoid(c))
    w_hi, w_lo = _split_bf16(w_ref[0])
    dot = functools.partial(jnp.dot, preferred_element_type=F32)
    o_ref[0] = dot(a_hi, w_hi) + (dot(a_lo, w_hi) + dot(a_hi, w_lo)) + b_ref[0]


def _mod_vectors(cond8, w_mod, b_mod):
    depth, d, d6 = w_mod.shape
    nb = 1536
    return pl.pallas_call(
        _mod_kernel,
        grid=(depth, d6 // nb),
        in_specs=[
            pl.BlockSpec((SUBLANES, d), lambda l, j: (0, 0)),
            pl.BlockSpec((1, d, nb), lambda l, j: (l, 0, j)),
            pl.BlockSpec((1, 1, nb), lambda l, j: (l, 0, j)),
        ],
        out_specs=pl.BlockSpec((1, SUBLANES, nb), lambda l, j: (l, 0, j)),
        out_shape=jax.ShapeDtypeStruct((depth, SUBLANES, d6), F32),
        compiler_params=_params(2, 40),
        name="adaln_mod",
    )(cond8, w_mod, b_mod.reshape(depth, 1, d6))


def _lru_in_kernel(*refs, mod_row, has_prev):
    if has_prev:
        x_ref, acc_ref, pmod_ref, mod_ref, gain_ref, w_ref, xo_ref, gate_ref, xr_ref = refs
        prev = (acc_ref, pmod_ref, xo_ref)
    else:
        x_ref, mod_ref, gain_ref, w_ref, gate_ref, xr_ref = refs
        prev = None
    d = x_ref.shape[1]
    m = mod_row(pl.program_id(0), x_ref.shape[0])
    x = _residual_stream(x_ref, prev, m)
    shift = mod_ref[pl.ds(m, 1), pl.ds(0, d)]
    scale = mod_ref[pl.ds(m, 1), pl.ds(d, d)]
    h = _modulated_norm(x, gain_ref[...], shift, scale).astype(BF16)
    u = jnp.dot(h, w_ref[...], preferred_element_type=F32)
    w = gate_ref.shape[1]
    gate_ref[...] = u[:, :w]
    xr_ref[...] = u[:, w:]


def _lru_in(x, prev, mod, gain, w_in, mod_row):
    n, d = x.shape
    w2 = w_in.shape[1]
    w = w2 // 2
    row = lambda i: (i, 0)
    fixed = lambda i: (0, 0)
    in_specs = [pl.BlockSpec((ROW_BLOCK, d), row)]
    args = [x]
    out_specs = [pl.BlockSpec((ROW_BLOCK, w), row), pl.BlockSpec((ROW_BLOCK, w), row)]
    out_shape = [jax.ShapeDtypeStruct((n, w), F32), jax.ShapeDtypeStruct((n, w), F32)]
    if prev is not None:
        in_specs += [_rows_spec(prev[0], d, ROW_BLOCK), pl.BlockSpec(prev[1].shape, fixed)]
        args += list(prev)
        out_specs = [pl.BlockSpec((ROW_BLOCK, d), row)] + out_specs
        out_shape = [jax.ShapeDtypeStruct((n, d), F32)] + out_shape
    in_specs += [pl.BlockSpec(mod.shape, fixed), pl.BlockSpec((1, d), fixed), pl.BlockSpec((d, w2), fixed)]
    args += [mod, gain, w_in]
    outs = pl.pallas_call(
        functools.partial(_lru_in_kernel, mod_row=mod_row, has_prev=prev is not None),
        grid=(n // ROW_BLOCK,),
        in_specs=in_specs,
        out_specs=out_specs,
        out_shape=out_shape,
        compiler_params=_params(1, 48),
        name="lru_in",
    )(*args)
    return outs if prev is not None else [x] + list(outs)


def _lru_core_kernel(xr_ref, gate_ref, cw_ref, cb_ref, wg_ref, bias_ref, lam_ref, h0_ref,
                     y_ref, fin_ref, pad, a_f, b_f, a_b, b_b):
    t_len, lb = xr_ref.shape
    ch = 256
    halo = SUBLANES
    seg = t_len // SUBLANES
    piece = min(ch, seg)
    seg_shift = seg.bit_length() - 1

    def seg_rows(t0):
        r = lax.shift_right_logical(t0, seg_shift)
        return pl.ds((t0 - r * seg) * SUBLANES + r, piece, stride=SUBLANES)

    zero_rows = jnp.zeros((halo, lb), F32)
    pad[0:halo, :] = zero_rows
    pad[t_len + halo:t_len + 2 * halo, :] = zero_rows

    def copy_in(c, carry):
        r = pl.multiple_of(c * ch, ch)
        pad[pl.ds(r + halo, ch), :] = xr_ref[pl.ds(r, ch), :]
        return carry

    lax.fori_loop(0, t_len // ch, copy_in, 0)

    neg_lam = -lam_ref[...]
    softplus = jnp.maximum(neg_lam, 0.0) + jnp.log1p(jnp.exp(-jnp.abs(neg_lam)))
    c_nla = (0.5 * RG_C) * softplus
    c_exp2 = (-0.5 * RG_C * math.log2(math.e)) * softplus
    cw = cw_ref[...]
    cbias = cb_ref[...]
    half_bias = bias_ref[...]
    half_wg = wg_ref[...]
    bw = half_wg.shape[1]
    n_slab = lb // LANES

    def gates(c, carry):
        r = pl.multiple_of(c * ch, ch)
        blk = pad[pl.ds(r, ch + 2 * halo), :]
        xc = cbias
        for k in range(CONV_W):
            o = halo - CONV_PAD_L + k
            xc = xc + blk[o:o + ch] * cw[k:k + 1]
        xc_bf = xc.astype(BF16)
        g = [jnp.dot(xc_bf[:, j * bw:(j + 1) * bw], half_wg[j], preferred_element_type=F32)
             for j in range(lb // bw)]
        pre = lambda k: jnp.concatenate([gj[:, k * bw:(k + 1) * bw] for gj in g], axis=1) + half_bias[k:k + 1]
        half_xc = 0.5 * xc
        for d, (a_ref, b_ref) in enumerate(((a_f, b_f), (a_b, b_b))):
            u = jnp.tanh(pre(2 * d)) + 1.0
            t_i = jnp.tanh(pre(2 * d + 1))
            a = jnp.exp2(u * c_exp2[d:d + 1])
            q = jnp.tanh(u * c_nla[d:d + 1]) * (1.0 + a * a)
            b = (q * lax.rsqrt(jnp.maximum(q, F32_TINY))) * (t_i * half_xc + half_xc)
            for p in range(ch // piece):
                rows = seg_rows(r + p * piece)
                for j in range(n_slab):
                    a_ref[j, rows, :] = a[p * piece:(p + 1) * piece, j * LANES:(j + 1) * LANES]
                    b_ref[j, rows, :] = b[p * piece:(p + 1) * piece, j * LANES:(j + 1) * LANES]
        return carry

    lax.fori_loop(0, t_len // ch, gates, 0)

    def scan(i, carry):
        rows = (pl.ds(pl.multiple_of(i * SUBLANES, SUBLANES), SUBLANES),
                pl.ds(pl.multiple_of((seg - 1 - i) * SUBLANES, SUBLANES), SUBLANES))
        out = []
        for k, (h, prod) in enumerate(carry):
            d, j = divmod(k, n_slab)
            a_ref, b_ref = ((a_f, b_f), (a_b, b_b))[d]
            a = a_ref[j, rows[d], :]
            h = a * h + b_ref[j, rows[d], :]
            prod = a * prod
            b_ref[j, rows[d], :] = h
            a_ref[j, rows[d], :] = prod
            out.append((h, prod))
        return tuple(out)

    start = (jnp.zeros((SUBLANES, LANES), F32), jnp.ones((SUBLANES, LANES), F32))
    ends = lax.fori_loop(0, seg, scan, (start,) * (2 * n_slab), unroll=8)

    enter = []
    for k, (h, prod) in enumerate(ends):
        d, j = divmod(k, n_slab)
        states = [h0_ref[0, d:d + 1, j * LANES:(j + 1) * LANES]]
        for r in (range(SUBLANES) if d == 0 else reversed(range(SUBLANES))):
            states.append(h[r:r + 1] + prod[r:r + 1] * states[-1])
        fin_ref[0, d:d + 1, j * LANES:(j + 1) * LANES] = states[SUBLANES]
        order = states[:SUBLANES] if d == 0 else states[:SUBLANES][::-1]
        enter.append(jnp.concatenate(order, axis=0))

    def fix(i, carry):
        rows = pl.ds(pl.multiple_of(i * SUBLANES, SUBLANES), SUBLANES)
        for j in range(n_slab):
            b_f[j, rows, :] = ((b_f[j, rows, :] + a_f[j, rows, :] * enter[j])
                               + (b_b[j, rows, :] + a_b[j, rows, :] * enter[n_slab + j]))
        return carry

    lax.fori_loop(0, seg, fix, 0, unroll=8)

    def emit(c, carry):
        r = pl.multiple_of(c * ch, ch)
        gt = gate_ref[pl.ds(r, ch), :]
        cdf = 0.5 * (1.0 + jnp.tanh(math.sqrt(2.0 / math.pi) * (gt + 0.044715 * (gt * gt * gt))))
        y = gt * cdf
        for p in range(ch // piece):
            rows = seg_rows(r + p * piece)
            total = jnp.concatenate([b_f[j, rows, :] for j in range(n_slab)], axis=1)
            y_ref[pl.ds(r + p * piece, piece), :] = (total * y[p * piece:(p + 1) * piece]).astype(BF16)
        return carry

    lax.fori_loop(0, t_len // ch, emit, 0)


def _lru_core(xr, gate, conv_w, conv_b, wg, bias4, lam, h0, *, seq_len):
    n, w = xr.shape
    n_seq = n // seq_len
    bw = w // LRU_BLOCKS
    block_bytes = seq_len * bw * (5 * 4 + 2 * 2 * 4 + 2 * 2)
    step_blocks = LRU_MAX_STEP_BLOCKS
    while step_blocks * block_bytes > LRU_CORE_VMEM_MIB * MIB:
        step_blocks //= 2
    lb = step_blocks * bw
    tok = lambda s, c: (s, c)
    chan = lambda s, c: (0, c)
    return pl.pallas_call(
        _lru_core_kernel,
        grid=(n_seq, LRU_BLOCKS // step_blocks),
        in_specs=[
            pl.BlockSpec((seq_len, lb), tok),
            pl.BlockSpec((seq_len, lb), tok),
            pl.BlockSpec((CONV_W, lb), chan),
            pl.BlockSpec((1, lb), chan),
            pl.BlockSpec((step_blocks, bw, 4 * bw), lambda s, c: (c, 0, 0)),
            pl.BlockSpec((4, lb), chan),
            pl.BlockSpec((2, lb), chan),
            pl.BlockSpec((1, 2, lb), lambda s, c: (s, 0, c)),
        ],
        out_specs=[pl.BlockSpec((seq_len, lb), tok),
                   pl.BlockSpec((1, 2, lb), lambda s, c: (s, 0, c))],
        out_shape=[jax.ShapeDtypeStruct((n, w), BF16), jax.ShapeDtypeStruct((n_seq, 2, w), F32)],
        scratch_shapes=[pltpu.VMEM((seq_len + 2 * SUBLANES, lb), F32)]
        + [pltpu.VMEM((lb // LANES, seq_len, LANES), F32)] * 4,
        compiler_params=_params(2, LRU_CORE_VMEM_MIB + 4),
        name=f"lru_core_{seq_len}",
    )(xr, gate, conv_w, conv_b, wg, bias4, lam, h0)


def _mix_out_kernel(a_ref, x_ref, mod_ref, gain_ref, w_ref, wrh_ref, wrl_ref,
                    x1_ref, h2_ref, lg_ref, *, mod_row):
    d = x_ref.shape[1]
    m = mod_row(pl.program_id(0), x_ref.shape[0])
    g_mix = mod_ref[pl.ds(m, 1), pl.ds(2 * d, d)]
    shift = mod_ref[pl.ds(m, 1), pl.ds(3 * d, d)]
    scale = mod_ref[pl.ds(m, 1), pl.ds(4 * d, d)]
    op = jnp.dot(a_ref[...], w_ref[...], preferred_element_type=F32)
    x1 = x_ref[...] + g_mix * op
    x1_ref[...] = x1
    h2 = _modulated_norm(x1, gain_ref[...], shift, scale)
    h_hi, h_lo = _split_bf16(h2)
    _store_rows(h2_ref, h2, 0)
    nt = functools.partial(lax.dot_general, dimension_numbers=(((1,), (1,)), ((), ())),
                           preferred_element_type=F32)
    wr_hi = wrh_ref[0]
    lg_ref[...] = nt(wr_hi, h_hi) + (nt(wrl_ref[0], h_hi) + nt(wr_hi, h_lo))


def _mix_out(a, x, mod, gain, w_out, wr_hi, wr_lo, layer, mod_row, token_tiled):
    n, d = x.shape
    e = wr_hi.shape[1]
    row = lambda i: (i, 0)
    fixed = lambda i: (0, 0)
    router = pl.BlockSpec((1,) + wr_hi.shape[1:], lambda i: (layer, 0, 0))
    tpr = _tiles_per_row(d)
    if token_tiled:
        h2_spec = pl.BlockSpec((ROW_BLOCK * tpr, LANES), row)
        h2_shape = jax.ShapeDtypeStruct((n * tpr, LANES), F32)
    else:
        h2_spec = pl.BlockSpec((ROW_BLOCK, d), row)
        h2_shape = jax.ShapeDtypeStruct((n, d), BF16)
    return pl.pallas_call(
        functools.partial(_mix_out_kernel, mod_row=mod_row),
        grid=(n // ROW_BLOCK,),
        in_specs=[
            pl.BlockSpec((ROW_BLOCK, a.shape[1]), row),
            pl.BlockSpec((ROW_BLOCK, d), row),
            pl.BlockSpec(mod.shape, fixed),
            pl.BlockSpec((1, d), fixed),
            pl.BlockSpec(w_out.shape, fixed),
            router,
            router,
        ],
        out_specs=[pl.BlockSpec((ROW_BLOCK, d), row), h2_spec,
                   pl.BlockSpec((e, ROW_BLOCK), lambda i: (0, i))],
        out_shape=[jax.ShapeDtypeStruct((n, d), F32), h2_shape, jax.ShapeDtypeStruct((e, n), F32)],
        compiler_params=_params(1, 48),
        name="mix_out",
    )(a, x, mod, gain, w_out, wr_hi, wr_lo)


def _lane_sum(tiles):
    acc = tiles[0]
    for t in tiles[1:]:
        acc = acc + t
    return jnp.sum(acc, axis=1, keepdims=True)


def _exclusive_cumsum(flags, upper):
    out = []
    off = jnp.zeros((flags[0].shape[0], 1), F32)
    for f in flags:
        out.append(jnp.dot(f.astype(BF16), upper, preferred_element_type=F32) + off)
        off = off + jnp.sum(f, axis=1, keepdims=True)
    return out


def _route_group(lg_ref, n_seq, t_len, idx_ref, g_ref):
    cap = CAPACITY_FACTOR * t_len // N_EXPERTS
    affs = []
    for s in range(n_seq):
        lg = lg_ref[:, s * t_len:(s + 1) * t_len]
        ex = jnp.exp(lg - jnp.max(lg, axis=0, keepdims=True))
        affs.append(ex / jnp.sum(ex, axis=0, keepdims=True))
    aff = jnp.concatenate(affs, axis=0)
    n_rows = aff.shape[0]
    nt = t_len // LANES
    g = [aff[:, j * LANES:(j + 1) * LANES] for j in range(nt)]

    kth_bits = jnp.zeros((n_rows, 1), I32)
    for bit in range(30, -1, -1):
        cand = kth_bits | (1 << bit)
        cand_f = pltpu.bitcast(cand, F32)
        cnt = _lane_sum([jnp.where(t >= cand_f, 1, 0) for t in g])
        kth_bits = jnp.where(cnt >= cap, cand, kth_bits)
    kth = pltpu.bitcast(kth_bits, F32)

    lane = lax.broadcasted_iota(I32, (LANES, LANES), 0)
    upper = jnp.where(lane < lax.broadcasted_iota(I32, (LANES, LANES), 1), 1.0, 0.0).astype(BF16)
    gt = [t > kth for t in g]
    eq = [t == kth for t in g]
    need = (cap - _lane_sum([jnp.where(m, 1, 0) for m in gt])).astype(F32)
    eq_rank = _exclusive_cumsum([jnp.where(m, 1.0, 0.0) for m in eq], upper)
    sel = [jnp.logical_or(gt[j], jnp.logical_and(eq[j], eq_rank[j] < need)) for j in range(nt)]
    pos = _exclusive_cumsum([jnp.where(m, 1.0, 0.0) for m in sel], upper)

    lane_r = lax.broadcasted_iota(I32, (n_rows, LANES), 1)
    d = [jnp.where(sel[j], lane_r + j * LANES - pos[j].astype(I32), -1) for j in range(nt)]
    for k in range(t_len.bit_length() - 1):
        s = 1 << k
        if s < LANES:
            d_rot = [pltpu.roll(x, LANES - s, 1) for x in d]
            g_rot = [pltpu.roll(x, LANES - s, 1) for x in g]
            same = lane_r < LANES - s
            d_in = [jnp.where(same, d_rot[j], d_rot[(j + 1) % nt]) for j in range(nt)]
            g_in = [jnp.where(same, g_rot[j], g_rot[(j + 1) % nt]) for j in range(nt)]
        else:
            q = s // LANES
            d_in = [d[(j + q) % nt] for j in range(nt)]
            g_in = [g[(j + q) % nt] for j in range(nt)]
        new_d, new_g = [], []
        for j in range(nt):
            move = jnp.logical_and(d_in[j] >= 0, ((d_in[j] >> k) & 1) == 1)
            stay = jnp.logical_and(d[j] >= 0, ((d[j] >> k) & 1) == 0)
            new_d.append(jnp.where(move, d_in[j], jnp.where(stay, d[j], -1)))
            new_g.append(jnp.where(move, g_in[j], g[j]))
        d, g = new_d, new_g

    for j in range(idx_ref.shape[1] // LANES):
        idx_ref[:, j * LANES:(j + 1) * LANES] = lane_r + j * LANES + d[j]
        g_ref[:, j * LANES:(j + 1) * LANES] = g[j]


def _route_kernel(lgp_ref, lgs_ref, idx_p_ref, g_p_ref, idx_s_ref, g_s_ref, *, p_len, s_len):
    _route_group(lgp_ref, lgp_ref.shape[1] // p_len, p_len, idx_p_ref, g_p_ref)
    _route_group(lgs_ref, lgs_ref.shape[1] // s_len, s_len, idx_s_ref, g_s_ref)


def _route(lg_p, lg_s, p_len, s_len):
    e = lg_p.shape[0]
    np_seq = lg_p.shape[1] // p_len
    ns_seq = lg_s.shape[1] // s_len
    cap_p = CAPACITY_FACTOR * p_len // N_EXPERTS
    cap_s = CAPACITY_FACTOR * s_len // N_EXPERTS
    wp = max(cap_p, LANES)
    ws = max(cap_s, LANES)
    idx_p, g_p, idx_s, g_s = pl.pallas_call(
        functools.partial(_route_kernel, p_len=p_len, s_len=s_len),
        out_shape=[jax.ShapeDtypeStruct((np_seq * e, wp), I32),
                   jax.ShapeDtypeStruct((np_seq * e, wp), F32),
                   jax.ShapeDtypeStruct((ns_seq * e, ws), I32),
                   jax.ShapeDtypeStruct((ns_seq * e, ws), F32)],
        compiler_params=pltpu.CompilerParams(vmem_limit_bytes=40 * MIB),
        name="route",
    )(lg_p, lg_s)
    return (idx_p[:, :cap_p].reshape(np_seq, e * cap_p), g_p[:, :cap_p].reshape(np_seq, e * cap_p),
            idx_s[:, :cap_s].reshape(-1), g_s[:, :cap_s].reshape(-1))


def _gather_onehot_kernel(idx_ref, h_ref, o_ref):
    t_len = h_ref.shape[0]
    n_exp, cap, _ = o_ref.shape
    idx = idx_ref[0]
    hit = jnp.where(idx == lax.broadcasted_iota(I32, (t_len, idx.shape[1]), 0), 1.0, 0.0).astype(BF16)
    xs = lax.dot_general(hit, h_ref[...], (((0,), (0,)), ((), ())), preferred_element_type=F32).astype(BF16)
    for e in range(n_exp):
        o_ref[e] = xs[e * cap:(e + 1) * cap]


def _gather_onehot(idx_row, h, *, seq_len):
    n, d = h.shape
    n_seq, _, slots = idx_row.shape
    cap = slots // N_EXPERTS
    return pl.pallas_call(
        _gather_onehot_kernel,
        grid=(n_seq,),
        in_specs=[pl.BlockSpec((1, 1, slots), lambda s: (s, 0, 0)),
                  pl.BlockSpec((seq_len, d), lambda s: (s, 0))],
        out_specs=pl.BlockSpec((N_EXPERTS, cap, d), lambda s: (0, s, 0)),
        out_shape=jax.ShapeDtypeStruct((N_EXPERTS, n_seq * cap, d), BF16),
        compiler_params=_params(1, 32),
        name="gather_onehot",
    )(idx_row, h)


def _gather_rows_kernel(idx_ref, h_ref, o_ref, tiles, *, cap, n_exp):
    tpr = _tiles_per_row(o_ref.shape[2])
    for el in range(ROW_LOOP_EXPERTS):
        base = (pl.program_id(0) * n_exp + pl.program_id(1) * ROW_LOOP_EXPERTS + el) * cap

        def body(c, carry, base=base):
            src = pl.multiple_of(idx_ref[base + c], tpr)
            tiles[pl.ds(pl.multiple_of(c * tpr, tpr), tpr), :] = h_ref[pl.ds(src, tpr), :]
            return carry

        lax.fori_loop(0, cap, body, 0, unroll=8)
        o_ref[el] = tiles[...].reshape(cap, tpr, LANES).reshape(cap, tpr * LANES).astype(BF16)


def _gather_rows(idx, h, d, *, seq_len):
    tpr = _tiles_per_row(d)
    n_seq = h.shape[0] // (seq_len * tpr)
    cap = CAPACITY_FACTOR * seq_len // N_EXPERTS
    return pl.pallas_call(
        functools.partial(_gather_rows_kernel, cap=cap, n_exp=N_EXPERTS),
        grid_spec=pltpu.PrefetchScalarGridSpec(
            num_scalar_prefetch=1,
            grid=(n_seq, N_EXPERTS // ROW_LOOP_EXPERTS),
            in_specs=[pl.BlockSpec((seq_len * tpr, LANES), lambda s, e, idx: (s, 0))],
            out_specs=pl.BlockSpec((ROW_LOOP_EXPERTS, cap, d), lambda s, e, idx: (e, s, 0)),
            scratch_shapes=[pltpu.VMEM((cap * tpr, LANES), F32)],
        ),
        out_shape=jax.ShapeDtypeStruct((N_EXPERTS, n_seq * cap, d), BF16),
        compiler_params=_params(2, 48),
        name="gather_rows",
    )(idx, h)


def _ffn_kernel(xp_ref, xs_ref, gp_ref, gs_ref, wg_ref, wu_ref, wd_ref, yp_ref, ys_ref):
    rc = 512
    f = pl.program_id(1)
    last = f == pl.num_programs(1) - 1

    def body(first):
        wg = wg_ref[0, 0].astype(BF16)
        wu = wu_ref[0, 0].astype(BF16)
        wd = wd_ref[0, 0].astype(BF16)
        for x_ref, g_ref, y_ref in ((xp_ref, gp_ref, yp_ref), (xs_ref, gs_ref, ys_ref)):
            for r in range(0, x_ref.shape[1], rc):
                x = x_ref[0, r:r + rc, :]
                hg = jnp.dot(x, wg, preferred_element_type=F32)
                hu = jnp.dot(x, wu, preferred_element_type=F32)
                hid = ((hg * jax.nn.sigmoid(hg)) * hu).astype(BF16)
                g_hi, g_lo = _split_bf16(g_ref[0, :, r:r + rc])
                ones = jnp.ones((g_hi.shape[0], LANES), BF16)
                tn = functools.partial(lax.dot_general, dimension_numbers=(((0,), (0,)), ((), ())),
                                       preferred_element_type=F32)
                weight = jnp.where(last, tn(g_hi, ones) + tn(g_lo, ones), 1.0)
                weight = jnp.concatenate([weight] * (y_ref.shape[2] // LANES), axis=1)
                y = jnp.dot(hid, wd, preferred_element_type=F32)
                if not first:
                    y = y_ref[0, r:r + rc, :] + y
                y_ref[0, r:r + rc, :] = y * weight

    pl.when(f == 0)(functools.partial(body, True))
    pl.when(f != 0)(functools.partial(body, False))


def _ffn(xs_p, xs_s, g_p, g_s, w_gate, w_up, w_down, layer):
    n_exp, rp, d = xs_p.shape
    rs = xs_s.shape[1]
    ff = w_gate.shape[3]
    fc = 1024
    return pl.pallas_call(
        _ffn_kernel,
        grid=(n_exp, ff // fc),
        in_specs=[
            pl.BlockSpec((1, rp, d), lambda e, f: (e, 0, 0)),
            pl.BlockSpec((1, rs, d), lambda e, f: (e, 0, 0)),
            pl.BlockSpec((1, SUBLANES, rp), lambda e, f: (e, 0, 0)),
            pl.BlockSpec((1, SUBLANES, rs), lambda e, f: (e, 0, 0)),
            pl.BlockSpec((1, 1, d, fc), lambda e, f: (layer, e, 0, f)),
            pl.BlockSpec((1, 1, d, fc), lambda e, f: (layer, e, 0, f)),
            pl.BlockSpec((1, 1, fc, d), lambda e, f: (layer, e, f, 0)),
        ],
        out_specs=[pl.BlockSpec((1, rp, d), lambda e, f: (e, 0, 0)),
                   pl.BlockSpec((1, rs, d), lambda e, f: (e, 0, 0))],
        out_shape=[jax.ShapeDtypeStruct((n_exp, rp, d), F32),
                   jax.ShapeDtypeStruct((n_exp, rs, d), F32)],
        compiler_params=_params(2, 58),
        name="expert_ffn",
    )(xs_p, xs_s, g_p, g_s, w_gate, w_up, w_down)


def _combine_onehot_kernel(*refs, mod_row, final):
    if final:
        idx_ref, ye_ref, x_ref, pmod_ref, gain_ref, o_ref = refs
    else:
        idx_ref, ye_ref, x_ref, pmod_ref, o_ref = refs
    t_len, d = o_ref.shape
    n_exp = ye_ref.shape[0]
    idx = idx_ref[0]
    hit = jnp.where(idx == lax.broadcasted_iota(I32, (t_len, idx.shape[1]), 0), 1.0, 0.0).astype(BF16)
    y_hi, y_lo = _split_bf16(jnp.concatenate([ye_ref[e] for e in range(n_exp)], axis=0))
    acc = jnp.dot(hit, y_hi, preferred_element_type=F32) + jnp.dot(hit, y_lo, preferred_element_type=F32)
    x = x_ref[...] + pmod_ref[pl.ds(mod_row(pl.program_id(0), t_len), 1), pl.ds(5 * d, d)] * acc
    if final:
        x = (x * lax.rsqrt(jnp.mean(x * x, axis=-1, keepdims=True) + EPS)) * gain_ref[...]
    o_ref[...] = x


def _combine_onehot(idx_row, ye, x, pmod, final_gain, *, seq_len, mod_row):
    n_exp, rows, d = ye.shape
    n_seq, _, slots = idx_row.shape
    cap = slots // n_exp
    in_specs = [pl.BlockSpec((1, 1, slots), lambda s: (s, 0, 0)),
                pl.BlockSpec((n_exp, cap, d), lambda s: (0, s, 0)),
                pl.BlockSpec((seq_len, d), lambda s: (s, 0)),
                pl.BlockSpec(pmod.shape, lambda s: (0, 0))]
    args = [idx_row, ye, x, pmod]
    if final_gain is not None:
        in_specs.append(pl.BlockSpec((1, d), lambda s: (0, 0)))
        args.append(final_gain)
    return pl.pallas_call(
        functools.partial(_combine_onehot_kernel, mod_row=mod_row, final=final_gain is not None),
        grid=(n_seq,),
        in_specs=in_specs,
        out_specs=pl.BlockSpec((seq_len, d), lambda s: (s, 0)),
        out_shape=jax.ShapeDtypeStruct((n_seq * seq_len, d), F32),
        compiler_params=_params(1, 32),
        name="combine_onehot",
    )(*args)


COMBINE_GROUP = 16
ROW_LOOP_EXPERTS = 2


def _combine_rows_kernel(idx_ref, ye_ref, o_ref, tiles, *, cap, n_exp):
    e = pl.program_id(1)
    d = ye_ref.shape[2]
    tpr = _tiles_per_row(d)
    ch = 2048

    @pl.when(e == 0)
    def _():
        def zero(c, carry):
            o_ref[pl.ds(pl.multiple_of(c * ch, ch), ch), :] = jnp.zeros((ch, LANES), F32)
            return carry
        lax.fori_loop(0, o_ref.shape[0] // ch, zero, 0)

    for el in range(ROW_LOOP_EXPERTS):
        tiles[...] = ye_ref[el].reshape(cap, tpr, LANES).reshape(cap * tpr, LANES)

        base = (pl.program_id(0) * n_exp + e * ROW_LOOP_EXPERTS + el) * cap

        def body(i, carry, base=base):
            c0 = i * COMBINE_GROUP
            dst = [pl.ds(pl.multiple_of(idx_ref[base + c0 + k], tpr), tpr) for k in range(COMBINE_GROUP)]
            vals = [o_ref[dst[k], :] + tiles[pl.ds(pl.multiple_of((c0 + k) * tpr, tpr), tpr), :]
                    for k in range(COMBINE_GROUP)]
            for k in range(COMBINE_GROUP):
                o_ref[dst[k], :] = vals[k]
            return carry

        lax.fori_loop(0, cap // COMBINE_GROUP, body, 0)


def _combine_rows(idx, ye, *, seq_len):
    n_exp, rows, d = ye.shape
    tpr = _tiles_per_row(d)
    cap = CAPACITY_FACTOR * seq_len // N_EXPERTS
    n_seq = rows // cap
    return pl.pallas_call(
        functools.partial(_combine_rows_kernel, cap=cap, n_exp=n_exp),
        grid_spec=pltpu.PrefetchScalarGridSpec(
            num_scalar_prefetch=1,
            grid=(n_seq, n_exp // ROW_LOOP_EXPERTS),
            in_specs=[pl.BlockSpec((ROW_LOOP_EXPERTS, cap, d), lambda s, e, i: (e, s, 0))],
            out_specs=pl.BlockSpec((seq_len * tpr, LANES), lambda s, e, i: (s, 0)),
            scratch_shapes=[pltpu.VMEM((cap * tpr, LANES), F32)],
        ),
        out_shape=jax.ShapeDtypeStruct((n_seq * seq_len * tpr, LANES), F32),
        compiler_params=_params(2, 48),
        name="combine_rows",
    )(idx, ye)


def _head_norm(x, gain):
    return x * lax.rsqrt(jnp.mean(x * x, axis=-1, keepdims=True) + EPS) * gain


def _qkv_kernel(*refs, rope, has_prev, mod_row, n_heads, n_kv):
    refs = list(refs)
    x_ref = refs.pop(0)
    prev = None
    if has_prev:
        acc_ref, pmod_ref = refs.pop(0), refs.pop(0)
    mod_ref, gain_ref, w_ref, qg_ref, kg_ref = refs[:5]
    refs = refs[5:]
    if rope:
        cos_ref, sin_ref = refs.pop(0), refs.pop(0)
    if has_prev:
        prev = (acc_ref, pmod_ref, refs.pop(0))
    q_ref, k_ref, v_ref = refs[:3]
    d = x_ref.shape[1]
    hd = HEAD_DIM
    m = mod_row(pl.program_id(0), x_ref.shape[0])
    x = _residual_stream(x_ref, prev, m)
    shift = mod_ref[pl.ds(m, 1), pl.ds(0, d)]
    scale = mod_ref[pl.ds(m, 1), pl.ds(d, d)]
    h = _modulated_norm(x, gain_ref[...], shift, scale).astype(BF16)
    qg = qg_ref[...]
    kg = kg_ref[...]
    q_scale = math.log2(math.e) * hd ** -0.5
    half = x_ref.shape[0] // 2
    for r0 in (0, half):
        rows = slice(r0, r0 + half)
        qkv = jnp.dot(h[rows], w_ref[...], preferred_element_type=F32)
        if rope:
            cos = cos_ref[rows, :]
            sin = sin_ref[rows, :]

            def rot(xh, cos=cos, sin=sin):
                return xh * cos + pltpu.roll(xh, hd // 2, 1) * sin
        else:
            rot = lambda xh: xh

        for i in range(n_heads):
            qh = rot(_head_norm(qkv[:, i * hd:(i + 1) * hd], qg)) * q_scale
            q_ref[rows, i * hd:(i + 1) * hd] = qh.astype(BF16)
        for i in range(n_kv):
            c0 = (n_heads + i) * hd
            kh = _head_norm(qkv[:, c0:c0 + hd], kg)
            if not rope:
                refs[3][rows, i * hd:(i + 1) * hd] = kh
            k_ref[rows, i * hd:(i + 1) * hd] = rot(kh).astype(BF16)
        v = qkv[:, (n_heads + n_kv) * hd:]
        v_ref[rows, :] = v.astype(BF16)
        if not rope:
            refs[4][rows, :] = v


def _qkv(x, prev, mod, gain, w_qkv, q_gain, k_gain, tables, *, seq_len, mod_row, n_heads, n_kv):
    n, d = x.shape
    hd = HEAD_DIM
    row = lambda i: (i, 0)
    fixed = lambda i: (0, 0)
    rope = tables is not None
    rb = QKV_ROW_BLOCK
    in_specs = [pl.BlockSpec((rb, d), row)]
    args = [x]
    if prev is not None:
        in_specs += [_rows_spec(prev[0], d, rb), pl.BlockSpec(prev[1].shape, fixed)]
        args += list(prev)
    in_specs += [
        pl.BlockSpec(mod.shape, fixed),
        pl.BlockSpec((1, d), fixed),
        pl.BlockSpec(w_qkv.shape, fixed),
        pl.BlockSpec((1, hd), fixed),
        pl.BlockSpec((1, hd), fixed),
    ]
    args += [mod, gain, w_qkv, q_gain, k_gain]
    if rope:
        per_seq = seq_len // rb
        in_specs += [pl.BlockSpec((rb, hd), lambda i: (i % per_seq, 0))] * 2
        args += list(tables)
    out_specs, out_shape = [], []
    if prev is not None:
        out_specs.append(pl.BlockSpec((rb, d), row))
        out_shape.append(jax.ShapeDtypeStruct((n, d), F32))
    out_specs += [pl.BlockSpec((rb, n_heads * hd), row),
                  pl.BlockSpec((rb, n_kv * hd), row),
                  pl.BlockSpec((rb, n_kv * hd), row)]
    out_shape += [jax.ShapeDtypeStruct((n, n_heads * hd), BF16),
                  jax.ShapeDtypeStruct((n, n_kv * hd), BF16),
                  jax.ShapeDtypeStruct((n, n_kv * hd), BF16)]
    if not rope:
        out_specs += [pl.BlockSpec((rb, n_kv * hd), row)] * 2
        out_shape += [jax.ShapeDtypeStruct((n, n_kv * hd), F32)] * 2
    outs = pl.pallas_call(
        functools.partial(_qkv_kernel, rope=rope, has_prev=prev is not None, mod_row=mod_row,
                          n_heads=n_heads, n_kv=n_kv),
        grid=(n // rb,),
        in_specs=in_specs,
        out_specs=out_specs,
        out_shape=out_shape,
        compiler_params=_params(1, 48),
        name="qkv_rope" if rope else "qkv",
    )(*args)
    return list(outs) if prev is not None else [x] + list(outs)


def _attn_kernel(*refs, group, chunk, has_cache):
    if has_cache:
        q_ref, k_ref, v_ref, kc_ref, vc_ref, o_ref = refs
    else:
        q_ref, k_ref, v_ref, o_ref = refs
    hd = HEAD_DIM
    qb = q_ref.shape[0]
    rows = group * qb
    q = jnp.concatenate([q_ref[:, g * hd:(g + 1) * hd] for g in range(group)], axis=0)
    sources = [(k_ref, v_ref, s0, chunk) for s0 in range(0, k_ref.shape[0], chunk)]
    if has_cache:
        sources.append((kc_ref.at[0], vc_ref.at[0], 0, kc_ref.shape[1]))
    m = jnp.full((rows, 1), -jnp.inf, F32)
    acc = jnp.zeros((rows, 2 * hd), F32)
    for kr, vr, s0, size in sources:
        s = lax.dot_general(q, kr[s0:s0 + size, :], (((1,), (1,)), ((), ())),
                            preferred_element_type=F32)
        m_new = jnp.maximum(m, jnp.max(s, axis=-1, keepdims=True))
        p = jnp.exp2(s - m_new).astype(BF16)
        v_ones = jnp.concatenate([vr[s0:s0 + size, :], jnp.ones((size, hd), BF16)], axis=1)
        acc = jnp.exp2(m - m_new) * acc + jnp.dot(p, v_ones, preferred_element_type=F32)
        m = m_new
    o = acc[:, :hd] / acc[:, hd:]
    o_ref[...] = jnp.concatenate([o[g * qb:(g + 1) * qb] for g in range(group)], axis=1).astype(BF16)


def _attention(q, k, v, cache, *, seq_len, qb, chunk, n_heads, n_kv):
    n = q.shape[0]
    hd = HEAD_DIM
    group = n_heads // n_kv
    nq = seq_len // qb
    in_specs = [
        pl.BlockSpec((qb, group * hd), lambda b, h, i: (b * nq + i, h)),
        pl.BlockSpec((seq_len, hd), lambda b, h, i: (b, h)),
        pl.BlockSpec((seq_len, hd), lambda b, h, i: (b, h)),
    ]
    args = [q, k, v]
    if cache is not None:
        past = cache[0].shape[1]
        in_specs += [pl.BlockSpec((1, past, hd), lambda b, h, i: (b, 0, h))] * 2
        args += list(cache)
    return pl.pallas_call(
        functools.partial(_attn_kernel, group=group, chunk=chunk, has_cache=cache is not None),
        grid=(n // seq_len, n_kv, nq),
        in_specs=in_specs,
        out_specs=pl.BlockSpec((qb, group * hd), lambda b, h, i: (b * nq + i, h)),
        out_shape=jax.ShapeDtypeStruct((n, n_heads * hd), BF16),
        compiler_params=_params(3, 48),
        name=f"attention_{seq_len}",
    )(*args)


def _final_norm_kernel(x_ref, acc_ref, pmod_ref, gain_ref, o_ref, *, mod_row):
    n_rows, d = x_ref.shape
    m = mod_row(pl.program_id(0), x_ref.shape[0])
    x = x_ref[...] + pmod_ref[pl.ds(m, 1), pl.ds(5 * d, d)] * _load_rows(acc_ref, n_rows, d)
    o_ref[...] = (x * lax.rsqrt(jnp.mean(x * x, axis=-1, keepdims=True) + EPS)) * gain_ref[...]


def _final_norm(x, acc, pmod, gain, mod_row):
    n, d = x.shape
    row = lambda i: (i, 0)
    fixed = lambda i: (0, 0)
    return pl.pallas_call(
        functools.partial(_final_norm_kernel, mod_row=mod_row),
        grid=(n // ROW_BLOCK,),
        in_specs=[pl.BlockSpec((ROW_BLOCK, d), row), _rows_spec(acc, d, ROW_BLOCK),
                  pl.BlockSpec(pmod.shape, fixed), pl.BlockSpec((1, d), fixed)],
        out_specs=pl.BlockSpec((ROW_BLOCK, d), row),
        out_shape=jax.ShapeDtypeStruct((n, d), F32),
        compiler_params=_params(1, 32),
        name="final_norm",
    )(x, acc, pmod, gain)


def _rope_head_order(x):
    q = HEAD_DIM // 4
    row1, row2, col1, col2 = (x[..., i * q:(i + 1) * q] for i in range(4))
    return jnp.concatenate([row1, col1, row2, col2], axis=-1)


def _rope_tables(seq_len):
    axis = HEAD_DIM // 2
    t = np.arange(seq_len)
    inv = ROPE_BASE ** (-np.arange(axis // 2, dtype=np.float64) * 2.0 / axis)
    ang = np.concatenate([(t // GRID_W)[:, None] * inv, (t % GRID_W)[:, None] * inv], axis=1)
    c, s = np.cos(ang), np.sin(ang)
    return (jnp.asarray(np.concatenate([c, c], axis=1), F32), jnp.asarray(np.concatenate([-s, s], axis=1), F32))


def kernel(x_prompt, x_sample, state_lru, cache_k, cache_v, c, c_ctx, w_mod, b_mod, norm_gain, final_gain,
           w_lru_in, lru_conv_w, lru_conv_b, lru_wa, lru_ba, lru_wx, lru_bx, lru_lambda, w_lru_out,
           w_qkv, q_norm, k_norm, w_attn_out, w_router, w_exp_gate, w_exp_up, w_exp_down):
    batch, seq, d = x_prompt.shape
    dec_batch, dec_seq, _ = x_sample.shape
    depth = w_mod.shape[0]
    n_kv = cache_k.shape[3]
    n_heads = w_attn_out.shape[1] // HEAD_DIM
    heads = dict(n_heads=n_heads, n_kv=n_kv)

    cond = jnp.zeros((SUBLANES, d), F32).at[0].set(c_ctx).at[1:1 + dec_batch].set(c)
    mod_all = _mod_vectors(cond, w_mod, b_mod)

    xs = [x_prompt.reshape(batch * seq, d), x_sample.reshape(dec_batch * dec_seq, d)]
    lens = [seq, dec_seq]
    mod_rows = [lambda i, rows: 0, lambda i, rows: 1 + (i * rows) // dec_seq]
    prevs = [None, None]

    fg = final_gain.reshape(1, d)
    wr_hi, wr_lo = _split_bf16(jnp.swapaxes(w_router, 1, 2))
    new_lru, new_k, new_v = [], [], []
    for l in range(depth):
        mod = mod_all[l]
        gain_a = norm_gain[l, 0].reshape(1, d)
        gain_c = norm_gain[l, 1].reshape(1, d)
        mixed = []
        if l % 2 == 0:
            li = l // 2
            w_in = w_lru_in[li].astype(BF16)
            wg = (0.5 * jnp.concatenate([lru_wa[li, 0], lru_wx[li, 0], lru_wa[li, 1], lru_wx[li, 1]],
                                        axis=2)).astype(BF16)
            bias4 = 0.5 * jnp.stack([lru_ba[li, 0], lru_bx[li, 0], lru_ba[li, 1], lru_bx[li, 1]])
            h0s = [jnp.zeros((batch, 2, w_in.shape[1] // 2), F32), state_lru[:, li]]
            for gi in range(2):
                xs[gi], gate, xr = _lru_in(xs[gi], prevs[gi], mod, gain_a, w_in, mod_rows[gi])
                y, fin = _lru_core(xr, gate, lru_conv_w[li], lru_conv_b[li].reshape(1, -1), wg, bias4,
                                   lru_lambda[li], h0s[gi], seq_len=lens[gi])
                mixed.append(y)
                if gi == 0:
                    new_lru.append(fin.astype(x_prompt.dtype))
            w_out = w_lru_out[li].astype(BF16)
        else:
            ai = l // 2
            w = w_qkv[ai].astype(BF16)
            qg = q_norm[ai].reshape(1, -1)
            kg = k_norm[ai].reshape(1, -1)
            xs[0], q_p, k_p, v_p, kf, vf = _qkv(xs[0], prevs[0], mod, gain_a, w, qg, kg, None,
                                                seq_len=seq, mod_row=mod_rows[0], **heads)
            n_qk = (n_heads + n_kv) * HEAD_DIM
            w_qk = _rope_head_order(w_qkv[ai][:, :n_qk].reshape(d, n_heads + n_kv, HEAD_DIM)).reshape(d, n_qk)
            w_r = jnp.concatenate([w_qk, w_qkv[ai][:, n_qk:]], axis=1).astype(BF16)
            xs[1], q_s, k_s, v_s = _qkv(xs[1], prevs[1], mod, gain_a, w_r, _rope_head_order(qg),
                                        _rope_head_order(kg), _rope_tables(dec_seq),
                                        seq_len=dec_seq, mod_row=mod_rows[1], **heads)
            new_k.append(kf.reshape(batch, seq, n_kv, HEAD_DIM))
            new_v.append(vf.reshape(batch, seq, n_kv, HEAD_DIM))
            past = cache_k.shape[2]
            kc = _rope_head_order(cache_k[:, ai]).reshape(dec_batch, past, n_kv * HEAD_DIM).astype(BF16)
            vc = cache_v[:, ai].reshape(dec_batch, past, n_kv * HEAD_DIM).astype(BF16)
            mixed.append(_attention(q_p, k_p, v_p, None, seq_len=seq, qb=seq, chunk=seq, **heads))
            mixed.append(_attention(q_s, k_s, v_s, (kc, vc), seq_len=dec_seq, qb=512, chunk=256, **heads))
            w_out = w_attn_out[ai].astype(BF16)

        x1_p, h2_p, lg_p = _mix_out(mixed[0], xs[0], mod, gain_c, w_out, wr_hi, wr_lo, l, mod_rows[0], False)
        x1_s, h2_s, lg_s = _mix_out(mixed[1], xs[1], mod, gain_c, w_out, wr_hi, wr_lo, l, mod_rows[1], True)
        idx_p, g_p, idx_s, g_s = _route(lg_p, lg_s, seq, dec_seq)
        xe_p = _gather_onehot(idx_p[:, None, :], h2_p, seq_len=seq)
        off_s = idx_s * _tiles_per_row(d)
        xe_s = _gather_rows(off_s, h2_s, d, seq_len=dec_seq)
        e_n = N_EXPERTS
        zero_rows = ((0, 0), (0, SUBLANES - 1), (0, 0))
        gc_p = jnp.pad(g_p.reshape(batch, e_n, -1).transpose(1, 0, 2).reshape(e_n, 1, -1), zero_rows)
        gc_s = jnp.pad(g_s.reshape(dec_batch, e_n, -1).transpose(1, 0, 2).reshape(e_n, 1, -1), zero_rows)
        ye_p, ye_s = _ffn(xe_p, xe_s, gc_p, gc_s, w_exp_gate, w_exp_up, w_exp_down, l)
        last = l == depth - 1
        x_p = _combine_onehot(idx_p[:, None, :], ye_p, x1_p, mod, fg if last else None, seq_len=seq,
                              mod_row=mod_rows[0])
        acc_s = _combine_rows(off_s, ye_s, seq_len=dec_seq)
        xs = [x_p, x1_s]
        prevs = [None, (acc_s, mod)]

    y_prompt = xs[0].reshape(batch, seq, d)
    y_sample = _final_norm(xs[1], *prevs[1], fg, mod_rows[1]).reshape(dec_batch, dec_seq, d)
    return (y_prompt, y_sample, jnp.stack(new_lru, axis=1), jnp.stack(new_k, axis=1), jnp.stack(new_v, axis=1))
```

```python
import functools
import math

import jax
import jax.numpy as jnp
import numpy as np
from jax import lax
from jax.experimental import pallas as pl
from jax.experimental.pallas import tpu as pltpu

F32 = jnp.float32
BF16 = jnp.bfloat16
I32 = jnp.int32

LANES = 128
SUBLANES = 8
MIB = 1024 * 1024
F32_TINY = float(np.finfo(np.float32).tiny)

RG_C = 8.0
CONV_W = 4
CONV_PAD_L = 2
ROPE_BASE = 10000.0
GRID_W = 64
EPS = 1e-6
N_EXPERTS = 16
CAPACITY_FACTOR = 2
HEAD_DIM = 128
LRU_BLOCKS = 8
LRU_MAX_STEP_BLOCKS = 4
LRU_CORE_VMEM_MIB = 44

ROW_BLOCK = 1024
QKV_ROW_BLOCK = 512


def _params(n_axes, vmem_mib):
    return pltpu.CompilerParams(
        dimension_semantics=("arbitrary",) * n_axes, vmem_limit_bytes=vmem_mib * MIB)


def _split_bf16(x):
    hi = x.astype(BF16)
    lo = (x - hi.astype(F32)).astype(BF16)
    return hi, lo


def _modulated_norm(x, gain, shift, scale):
    y = x * lax.rsqrt(jnp.mean(x * x, axis=-1, keepdims=True) + EPS)
    return (y * gain) * (1.0 + scale) + shift


def _tiles_per_row(d):
    return d // LANES


def _load_rows(ref, n_rows, d):
    if ref.shape == (n_rows, d):
        return ref[...]
    tpr = _tiles_per_row(d)
    return jnp.concatenate([ref[pl.ds(s, n_rows, stride=tpr), :] for s in range(tpr)], axis=1)


def _store_rows(ref, x, r0):
    n_rows, d = x.shape
    if ref.shape[1] == d:
        ref[r0:r0 + n_rows, :] = x.astype(ref.dtype)
        return
    tpr = _tiles_per_row(d)
    for s in range(tpr):
        ref[pl.ds(r0 * tpr + s, n_rows, stride=tpr), :] = x[:, s * LANES:(s + 1) * LANES]


def _rows_spec(arr, d, rows):
    if arr.shape[1] == d:
        return pl.BlockSpec((rows, d), lambda i: (i, 0))
    return pl.BlockSpec((rows * _tiles_per_row(d), LANES), lambda i: (i, 0))


def _residual_stream(x_ref, prev, m):
    if prev is None:
        return x_ref[...]
    acc_ref, pmod_ref, xo_ref = prev
    n_rows, d = x_ref.shape
    x = x_ref[...] + pmod_ref[pl.ds(m, 1), pl.ds(5 * d, d)] * _load_rows(acc_ref, n_rows, d)
    xo_ref[...] = x
    return x


def _mod_kernel(cond_ref, w_ref, b_ref, o_ref):
    c = cond_ref[...]
    a_hi, a_lo = _split_bf16(c * jax.nn.sigmoid(c))
    w_hi, w_lo = _split_bf16(w_ref[0])
    dot = functools.partial(jnp.dot, preferred_element_type=F32)
    o_ref[0] = dot(a_hi, w_hi) + (dot(a_lo, w_hi) + dot(a_hi, w_lo)) + b_ref[0]


def _mod_vectors(cond8, w_mod, b_mod):
    depth, d, d6 = w_mod.shape
    nb = 1536
    return pl.pallas_call(
        _mod_kernel,
        grid=(depth, d6 // nb),
        in_specs=[
            pl.BlockSpec((SUBLANES, d), lambda l, j: (0, 0)),
            pl.BlockSpec((1, d, nb), lambda l, j: (l, 0, j)),
            pl.BlockSpec((1, 1, nb), lambda l, j: (l, 0, j)),
        ],
        out_specs=pl.BlockSpec((1, SUBLANES, nb), lambda l, j: (l, 0, j)),
        out_shape=jax.ShapeDtypeStruct((depth, SUBLANES, d6), F32),
        compiler_params=_params(2, 40),
        name="adaln_mod",
    )(cond8, w_mod, b_mod.reshape(depth, 1, d6))


def _lru_in_kernel(*refs, mod_row, has_prev):
    if has_prev:
        x_ref, acc_ref, pmod_ref, mod_ref, gain_ref, w_ref, xo_ref, gate_ref, xr_ref = refs
        prev = (acc_ref, pmod_ref, xo_ref)
    else:
        x_ref, mod_ref, gain_ref, w_ref, gate_ref, xr_ref = refs
        prev = None
    d = x_ref.shape[1]
    m = mod_row(pl.program_id(0), x_ref.shape[0])
    x = _residual_stream(x_ref, prev, m)
    shift = mod_ref[pl.ds(m, 1), pl.ds(0, d)]
    scale = mod_ref[pl.ds(m, 1), pl.ds(d, d)]
    h = _modulated_norm(x, gain_ref[...], shift, scale).astype(BF16)
    u = jnp.dot(h, w_ref[...], preferred_element_type=F32)
    w = gate_ref.shape[1]
    gate_ref[...] = u[:, :w]
    xr_ref[...] = u[:, w:]


def _lru_in(x, prev, mod, gain, w_in, mod_row):
    n, d = x.shape
    w2 = w_in.shape[1]
    w = w2 // 2
    row = lambda i: (i, 0)
    fixed = lambda i: (0, 0)
    in_specs = [pl.BlockSpec((ROW_BLOCK, d), row)]
    args = [x]
    out_specs = [pl.BlockSpec((ROW_BLOCK, w), row), pl.BlockSpec((ROW_BLOCK, w), row)]
    out_shape = [jax.ShapeDtypeStruct((n, w), F32), jax.ShapeDtypeStruct((n, w), F32)]
    if prev is not None:
        in_specs += [_rows_spec(prev[0], d, ROW_BLOCK), pl.BlockSpec(prev[1].shape, fixed)]
        args += list(prev)
        out_specs = [pl.BlockSpec((ROW_BLOCK, d), row)] + out_specs
        out_shape = [jax.ShapeDtypeStruct((n, d), F32)] + out_shape
    in_specs += [pl.BlockSpec(mod.shape, fixed), pl.BlockSpec((1, d), fixed), pl.BlockSpec((d, w2), fixed)]
    args += [mod, gain, w_in]
    outs = pl.pallas_call(
        functools.partial(_lru_in_kernel, mod_row=mod_row, has_prev=prev is not None),
        grid=(n // ROW_BLOCK,),
        in_specs=in_specs,
        out_specs=out_specs,
        out_shape=out_shape,
        compiler_params=_params(1, 48),
        name="lru_in",
    )(*args)
    return outs if prev is not None else [x] + list(outs)


def _lru_core_kernel(xr_ref, gate_ref, cw_ref, cb_ref, wg_ref, bias_ref, lam_ref, h0_ref,
                     y_ref, fin_ref, pad, a_f, b_f, a_b, b_b):
    t_len, lb = xr_ref.shape
    ch = 256
    halo = SUBLANES
    seg = t_len // SUBLANES
    piece = min(ch, seg)
    seg_shift = seg.bit_length() - 1

    def seg_rows(t0):
        r = lax.shift_right_logical(t0, seg_shift)
        return pl.ds((t0 - r * seg) * SUBLANES + r, piece, stride=SUBLANES)

    zero_rows = jnp.zeros((halo, lb), F32)
    pad[0:halo, :] = zero_rows
    pad[t_len + halo:t_len + 2 * halo, :] = zero_rows

    def copy_in(c, carry):
        r = pl.multiple_of(c * ch, ch)
        pad[pl.ds(r + halo, ch), :] = xr_ref[pl.ds(r, ch), :]
        return carry

    lax.fori_loop(0, t_len // ch, copy_in, 0)

    neg_lam = -lam_ref[...]
    softplus = jnp.maximum(neg_lam, 0.0) + jnp.log1p(jnp.exp(-jnp.abs(neg_lam)))
    c_nla = (0.5 * RG_C) * softplus
    c_exp2 = (-0.5 * RG_C * math.log2(math.e)) * softplus
    cw = cw_ref[...]
    cbias = cb_ref[...]
    half_bias = bias_ref[...]
    half_wg = wg_ref[...]
    bw = half_wg.shape[1]
    n_slab = lb // LANES

    def gates(c, carry):
        r = pl.multiple_of(c * ch, ch)
        blk = pad[pl.ds(r, ch + 2 * halo), :]
        xc = cbias
        for k in range(CONV_W):
            o = halo - CONV_PAD_L + k
            xc = xc + blk[o:o + ch] * cw[k:k + 1]
        xc_bf = xc.astype(BF16)
        g = [jnp.dot(xc_bf[:, j * bw:(j + 1) * bw], half_wg[j], preferred_element_type=F32)
             for j in range(lb // bw)]
        pre = lambda k: jnp.concatenate([gj[:, k * bw:(k + 1) * bw] for gj in g], axis=1) + half_bias[k:k + 1]
        half_xc = 0.5 * xc
        for d, (a_ref, b_ref) in enumerate(((a_f, b_f), (a_b, b_b))):
            u = jnp.tanh(pre(2 * d)) + 1.0
            t_i = jnp.tanh(pre(2 * d + 1))
            a = jnp.exp2(u * c_exp2[d:d + 1])
            q = jnp.tanh(u * c_nla[d:d + 1]) * (1.0 + a * a)
            b = (q * lax.rsqrt(jnp.maximum(q, F32_TINY))) * (t_i * half_xc + half_xc)
            for p in range(ch // piece):
                rows = seg_rows(r + p * piece)
                for j in range(n_slab):
                    a_ref[j, rows, :] = a[p * piece:(p + 1) * piece, j * LANES:(j + 1) * LANES]
                    b_ref[j, rows, :] = b[p * piece:(p + 1) * piece, j * LANES:(j + 1) * LANES]
        return carry

    lax.fori_loop(0, t_len // ch, gates, 0)

    def scan(i, carry):
        rows = (pl.ds(pl.multiple_of(i * SUBLANES, SUBLANES), SUBLANES),
                pl.ds(pl.multiple_of((seg - 1 - i) * SUBLANES, SUBLANES), SUBLANES))
        out = []
        for k, (h, prod) in enumerate(carry):
            d, j = divmod(k, n_slab)
            a_ref, b_ref = ((a_f, b_f), (a_b, b_b))[d]
            a = a_ref[j, rows[d], :]
            h = a * h + b_ref[j, rows[d], :]
            prod = a * prod
            b_ref[j, rows[d], :] = h
            a_ref[j, rows[d], :] = prod
            out.append((h, prod))
        return tuple(out)

    start = (jnp.zeros((SUBLANES, LANES), F32), jnp.ones((SUBLANES, LANES), F32))
    ends = lax.fori_loop(0, seg, scan, (start,) * (2 * n_slab), unroll=8)

    enter = []
    for k, (h, prod) in enumerate(ends):
        d, j = divmod(k, n_slab)
        states = [h0_ref[0, d:d + 1, j * LANES:(j + 1) * LANES]]
        for r in (range(SUBLANES) if d == 0 else reversed(range(SUBLANES))):
            states.append(h[r:r + 1] + prod[r:r + 1] * states[-1])
        fin_ref[0, d:d + 1, j * LANES:(j + 1) * LANES] = states[SUBLANES]
        order = states[:SUBLANES] if d == 0 else states[:SUBLANES][::-1]
        enter.append(jnp.concatenate(order, axis=0))

    def fix(i, carry):
        rows = pl.ds(pl.multiple_of(i * SUBLANES, SUBLANES), SUBLANES)
        for j in range(n_slab):
            b_f[j, rows, :] = ((b_f[j, rows, :] + a_f[j, rows, :] * enter[j])
                               + (b_b[j, rows, :] + a_b[j, rows, :] * enter[n_slab + j]))
        return carry

    lax.fori_loop(0, seg, fix, 0, unroll=8)

    def emit(c, carry):
        r = pl.multiple_of(c * ch, ch)
        gt = gate_ref[pl.ds(r, ch), :]
        cdf = 0.5 * (1.0 + jnp.tanh(math.sqrt(2.0 / math.pi) * (gt + 0.044715 * (gt * gt * gt))))
        y = gt * cdf
        for p in range(ch // piece):
            rows = seg_rows(r + p * piece)
            total = jnp.concatenate([b_f[j, rows, :] for j in range(n_slab)], axis=1)
            y_ref[pl.ds(r + p * piece, piece), :] = (total * y[p * piece:(p + 1) * piece]).astype(BF16)
        return carry

    lax.fori_loop(0, t_len // ch, emit, 0)


def _lru_core(xr, gate, conv_w, conv_b, wg, bias4, lam, h0, *, seq_len):
    n, w = xr.shape
    n_seq = n // seq_len
    bw = w // LRU_BLOCKS
    block_bytes = seq_len * bw * (5 * 4 + 2 * 2 * 4 + 2 * 2)
    step_blocks = LRU_MAX_STEP_BLOCKS
    while step_blocks * block_bytes > LRU_CORE_VMEM_MIB * MIB:
        step_blocks //= 2
    lb = step_blocks * bw
    tok = lambda s, c: (s, c)
    chan = lambda s, c: (0, c)
    return pl.pallas_call(
        _lru_core_kernel,
        grid=(n_seq, LRU_BLOCKS // step_blocks),
        in_specs=[
            pl.BlockSpec((seq_len, lb), tok),
            pl.BlockSpec((seq_len, lb), tok),
            pl.BlockSpec((CONV_W, lb), chan),
            pl.BlockSpec((1, lb), chan),
            pl.BlockSpec((step_blocks, bw, 4 * bw), lambda s, c: (c, 0, 0)),
            pl.BlockSpec((4, lb), chan),
            pl.BlockSpec((2, lb), chan),
            pl.BlockSpec((1, 2, lb), lambda s, c: (s, 0, c)),
        ],
        out_specs=[pl.BlockSpec((seq_len, lb), tok),
                   pl.BlockSpec((1, 2, lb), lambda s, c: (s, 0, c))],
        out_shape=[jax.ShapeDtypeStruct((n, w), BF16), jax.ShapeDtypeStruct((n_seq, 2, w), F32)],
        scratch_shapes=[pltpu.VMEM((seq_len + 2 * SUBLANES, lb), F32)]
        + [pltpu.VMEM((lb // LANES, seq_len, LANES), F32)] * 4,
        compiler_params=_params(2, LRU_CORE_VMEM_MIB + 4),
        name=f"lru_core_{seq_len}",
    )(xr, gate, conv_w, conv_b, wg, bias4, lam, h0)


def _mix_out_kernel(a_ref, x_ref, mod_ref, gain_ref, w_ref, wrh_ref, wrl_ref,
                    x1_ref, h2_ref, lg_ref, *, mod_row):
    d = x_ref.shape[1]
    m = mod_row(pl.program_id(0), x_ref.shape[0])
    g_mix = mod_ref[pl.ds(m, 1), pl.ds(2 * d, d)]
    shift = mod_ref[pl.ds(m, 1), pl.ds(3 * d, d)]
    scale = mod_ref[pl.ds(m, 1), pl.ds(4 * d, d)]
    op = jnp.dot(a_ref[...], w_ref[...], preferred_element_type=F32)
    x1 = x_ref[...] + g_mix * op
    x1_ref[...] = x1
    h2 = _modulated_norm(x1, gain_ref[...], shift, scale)
    h_hi, h_lo = _split_bf16(h2)
    _store_rows(h2_ref, h2, 0)
    nt = functools.partial(lax.dot_general, dimension_numbers=(((1,), (1,)), ((), ())),
                           preferred_element_type=F32)
    wr_hi = wrh_ref[0]
    lg_ref[...] = nt(wr_hi, h_hi) + (nt(wrl_ref[0], h_hi) + nt(wr_hi, h_lo))


def _mix_out(a, x, mod, gain, w_out, wr_hi, wr_lo, layer, mod_row, token_tiled):
    n, d = x.shape
    e = wr_hi.shape[1]
    row = lambda i: (i, 0)
    fixed = lambda i: (0, 0)
    router = pl.BlockSpec((1,) + wr_hi.shape[1:], lambda i: (layer, 0, 0))
    tpr = _tiles_per_row(d)
    if token_tiled:
        h2_spec = pl.BlockSpec((ROW_BLOCK * tpr, LANES), row)
        h2_shape = jax.ShapeDtypeStruct((n * tpr, LANES), F32)
    else:
        h2_spec = pl.BlockSpec((ROW_BLOCK, d), row)
        h2_shape = jax.ShapeDtypeStruct((n, d), BF16)
    return pl.pallas_call(
        functools.partial(_mix_out_kernel, mod_row=mod_row),
        grid=(n // ROW_BLOCK,),
        in_specs=[
            pl.BlockSpec((ROW_BLOCK, a.shape[1]), row),
            pl.BlockSpec((ROW_BLOCK, d), row),
            pl.BlockSpec(mod.shape, fixed),
            pl.BlockSpec((1, d), fixed),
            pl.BlockSpec(w_out.shape, fixed),
            router,
            router,
        ],
        out_specs=[pl.BlockSpec((ROW_BLOCK, d), row), h2_spec,
                   pl.BlockSpec((e, ROW_BLOCK), lambda i: (0, i))],
        out_shape=[jax.ShapeDtypeStruct((n, d), F32), h2_shape, jax.ShapeDtypeStruct((e, n), F32)],
        compiler_params=_params(1, 48),
        name="mix_out",
    )(a, x, mod, gain, w_out, wr_hi, wr_lo)


def _lane_sum(tiles):
    acc = tiles[0]
    for t in tiles[1:]:
        acc = acc + t
    return jnp.sum(acc, axis=1, keepdims=True)


def _exclusive_cumsum(flags, upper):
    out = []
    off = jnp.zeros((flags[0].shape[0], 1), F32)
    for f in flags:
        out.append(jnp.dot(f.astype(BF16), upper, preferred_element_type=F32) + off)
        off = off + jnp.sum(f, axis=1, keepdims=True)
    return out


def _route_group(lg_ref, n_seq, t_len, idx_ref, g_ref):
    cap = CAPACITY_FACTOR * t_len // N_EXPERTS
    affs = []
    for s in range(n_seq):
        lg = lg_ref[:, s * t_len:(s + 1) * t_len]
        ex = jnp.exp(lg - jnp.max(lg, axis=0, keepdims=True))
        affs.append(ex / jnp.sum(ex, axis=0, keepdims=True))
    aff = jnp.concatenate(affs, axis=0)
    n_rows = aff.shape[0]
    nt = t_len // LANES
    g = [aff[:, j * LANES:(j + 1) * LANES] for j in range(nt)]

    kth_bits = jnp.zeros((n_rows, 1), I32)
    for bit in range(30, -1, -1):
        cand = kth_bits | (1 << bit)
        cand_f = pltpu.bitcast(cand, F32)
        cnt = _lane_sum([jnp.where(t >= cand_f, 1, 0) for t in g])
        kth_bits = jnp.where(cnt >= cap, cand, kth_bits)
    kth = pltpu.bitcast(kth_bits, F32)

    lane = lax.broadcasted_iota(I32, (LANES, LANES), 0)
    upper = jnp.where(lane < lax.broadcasted_iota(I32, (LANES, LANES), 1), 1.0, 0.0).astype(BF16)
    gt = [t > kth for t in g]
    eq = [t == kth for t in g]
    need = (cap - _lane_sum([jnp.where(m, 1, 0) for m in gt])).astype(F32)
    eq_rank = _exclusive_cumsum([jnp.where(m, 1.0, 0.0) for m in eq], upper)
    sel = [jnp.logical_or(gt[j], jnp.logical_and(eq[j], eq_rank[j] < need)) for j in range(nt)]
    pos = _exclusive_cumsum([jnp.where(m, 1.0, 0.0) for m in sel], upper)

    lane_r = lax.broadcasted_iota(I32, (n_rows, LANES), 1)
    d = [jnp.where(sel[j], lane_r + j * LANES - pos[j].astype(I32), -1) for j in range(nt)]
    for k in range(t_len.bit_length() - 1):
        s = 1 << k
        if s < LANES:
            d_rot = [pltpu.roll(x, LANES - s, 1) for x in d]
            g_rot = [pltpu.roll(x, LANES - s, 1) for x in g]
            same = lane_r < LANES - s
            d_in = [jnp.where(same, d_rot[j], d_rot[(j + 1) % nt]) for j in range(nt)]
            g_in = [jnp.where(same, g_rot[j], g_rot[(j + 1) % nt]) for j in range(nt)]
        else:
            q = s // LANES
            d_in = [d[(j + q) % nt] for j in range(nt)]
            g_in = [g[(j + q) % nt] for j in range(nt)]
        new_d, new_g = [], []
        for j in range(nt):
            move = jnp.logical_and(d_in[j] >= 0, ((d_in[j] >> k) & 1) == 1)
            stay = jnp.logical_and(d[j] >= 0, ((d[j] >> k) & 1) == 0)
            new_d.append(jnp.where(move, d_in[j], jnp.where(stay, d[j], -1)))
            new_g.append(jnp.where(move, g_in[j], g[j]))
        d, g = new_d, new_g

    for j in range(idx_ref.shape[1] // LANES):
        idx_ref[:, j * LANES:(j + 1) * LANES] = lane_r + j * LANES + d[j]
        g_ref[:, j * LANES:(j + 1) * LANES] = g[j]


def _route_kernel(lgp_ref, lgs_ref, idx_p_ref, g_p_ref, idx_s_ref, g_s_ref, *, p_len, s_len):
    _route_group(lgp_ref, lgp_ref.shape[1] // p_len, p_len, idx_p_ref, g_p_ref)
    _route_group(lgs_ref, lgs_ref.shape[1] // s_len, s_len, idx_s_ref, g_s_ref)


def _route(lg_p, lg_s, p_len, s_len):
    e = lg_p.shape[0]
    np_seq = lg_p.shape[1] // p_len
    ns_seq = lg_s.shape[1] // s_len
    cap_p = CAPACITY_FACTOR * p_len // N_EXPERTS
    cap_s = CAPACITY_FACTOR * s_len // N_EXPERTS
    wp = max(cap_p, LANES)
    ws = max(cap_s, LANES)
    idx_p, g_p, idx_s, g_s = pl.pallas_call(
        functools.partial(_route_kernel, p_len=p_len, s_len=s_len),
        out_shape=[jax.ShapeDtypeStruct((np_seq * e, wp), I32),
                   jax.ShapeDtypeStruct((np_seq * e, wp), F32),
                   jax.ShapeDtypeStruct((ns_seq * e, ws), I32),
                   jax.ShapeDtypeStruct((ns_seq * e, ws), F32)],
        compiler_params=pltpu.CompilerParams(vmem_limit_bytes=40 * MIB),
        name="route",
    )(lg_p, lg_s)
    return (idx_p[:, :cap_p].reshape(np_seq, e * cap_p), g_p[:, :cap_p].reshape(np_seq, e * cap_p),
            idx_s[:, :cap_s].reshape(-1), g_s[:, :cap_s].reshape(-1))


def _gather_onehot_kernel(idx_ref, h_ref, o_ref):
    t_len = h_ref.shape[0]
    n_exp, cap, _ = o_ref.shape
    idx = idx_ref[0]
    hit = jnp.where(idx == lax.broadcasted_iota(I32, (t_len, idx.shape[1]), 0), 1.0, 0.0).astype(BF16)
    xs = lax.dot_general(hit, h_ref[...], (((0,), (0,)), ((), ())), preferred_element_type=F32).astype(BF16)
    for e in range(n_exp):
        o_ref[e] = xs[e * cap:(e + 1) * cap]


def _gather_onehot(idx_row, h, *, seq_len):
    n, d = h.shape
    n_seq, _, slots = idx_row.shape
    cap = slots // N_EXPERTS
    return pl.pallas_call(
        _gather_onehot_kernel,
        grid=(n_seq,),
        in_specs=[pl.BlockSpec((1, 1, slots), lambda s: (s, 0, 0)),
                  pl.BlockSpec((seq_len, d), lambda s: (s, 0))],
        out_specs=pl.BlockSpec((N_EXPERTS, cap, d), lambda s: (0, s, 0)),
        out_shape=jax.ShapeDtypeStruct((N_EXPERTS, n_seq * cap, d), BF16),
        compiler_params=_params(1, 32),
        name="gather_onehot",
    )(idx_row, h)


def _gather_rows_kernel(idx_ref, h_ref, o_ref, tiles, *, cap, n_exp):
    tpr = _tiles_per_row(o_ref.shape[2])
    for el in range(ROW_LOOP_EXPERTS):
        base = (pl.program_id(0) * n_exp + pl.program_id(1) * ROW_LOOP_EXPERTS + el) * cap

        def body(c, carry, base=base):
            src = pl.multiple_of(idx_ref[base + c], tpr)
            tiles[pl.ds(pl.multiple_of(c * tpr, tpr), tpr), :] = h_ref[pl.ds(src, tpr), :]
            return carry

        lax.fori_loop(0, cap, body, 0, unroll=16)
        o_ref[el] = tiles[...].reshape(cap, tpr, LANES).reshape(cap, tpr * LANES).astype(BF16)


def _gather_rows(idx, h, d, *, seq_len):
    tpr = _tiles_per_row(d)
    n_seq = h.shape[0] // (seq_len * tpr)
    cap = CAPACITY_FACTOR * seq_len // N_EXPERTS
    return pl.pallas_call(
        functools.partial(_gather_rows_kernel, cap=cap, n_exp=N_EXPERTS),
        grid_spec=pltpu.PrefetchScalarGridSpec(
            num_scalar_prefetch=1,
            grid=(n_seq, N_EXPERTS // ROW_LOOP_EXPERTS),
            in_specs=[pl.BlockSpec((seq_len * tpr, LANES), lambda s, e, idx: (s, 0))],
            out_specs=pl.BlockSpec((ROW_LOOP_EXPERTS, cap, d), lambda s, e, idx: (e, s, 0)),
            scratch_shapes=[pltpu.VMEM((cap * tpr, LANES), F32)],
        ),
        out_shape=jax.ShapeDtypeStruct((N_EXPERTS, n_seq * cap, d), BF16),
        compiler_params=_params(2, 48),
        name="gather_rows",
    )(idx, h)


def _ffn_kernel(xp_ref, xs_ref, gp_ref, gs_ref, wg_ref, wu_ref, wd_ref, yp_ref, ys_ref):
    rc = 512
    f = pl.program_id(1)
    last = f == pl.num_programs(1) - 1

    def body(first):
        wg = wg_ref[0, 0].astype(BF16)
        wu = wu_ref[0, 0].astype(BF16)
        wd = wd_ref[0, 0].astype(BF16)
        for x_ref, g_ref, y_ref in ((xp_ref, gp_ref, yp_ref), (xs_ref, gs_ref, ys_ref)):
            for r in range(0, x_ref.shape[1], rc):
                x = x_ref[0, r:r + rc, :]
                hg = jnp.dot(x, wg, preferred_element_type=F32)
                hu = jnp.dot(x, wu, preferred_element_type=F32)
                hid = ((hg * jax.nn.sigmoid(hg)) * hu).astype(BF16)
                g_hi, g_lo = _split_bf16(g_ref[0, :, r:r + rc])
                ones = jnp.ones((g_hi.shape[0], LANES), BF16)
                tn = functools.partial(lax.dot_general, dimension_numbers=(((0,), (0,)), ((), ())),
                                       preferred_element_type=F32)
                weight = jnp.where(last, tn(g_hi, ones) + tn(g_lo, ones), 1.0)
                weight = jnp.concatenate([weight] * (y_ref.shape[2] // LANES), axis=1)
                y = jnp.dot(hid, wd, preferred_element_type=F32)
                if not first:
                    y = y_ref[0, r:r + rc, :] + y
                y_ref[0, r:r + rc, :] = y * weight

    pl.when(f == 0)(functools.partial(body, True))
    pl.when(f != 0)(functools.partial(body, False))


def _ffn(xs_p, xs_s, g_p, g_s, w_gate, w_up, w_down, layer):
    n_exp, rp, d = xs_p.shape
    rs = xs_s.shape[1]
    ff = w_gate.shape[3]
    fc = 1024
    return pl.pallas_call(
        _ffn_kernel,
        grid=(n_exp, ff // fc),
        in_specs=[
            pl.BlockSpec((1, rp, d), lambda e, f: (e, 0, 0)),
            pl.BlockSpec((1, rs, d), lambda e, f: (e, 0, 0)),
            pl.BlockSpec((1, SUBLANES, rp), lambda e, f: (e, 0, 0)),
            pl.BlockSpec((1, SUBLANES, rs), lambda e, f: (e, 0, 0)),
            pl.BlockSpec((1, 1, d, fc), lambda e, f: (layer, e, 0, f)),
            pl.BlockSpec((1, 1, d, fc), lambda e, f: (layer, e, 0, f)),
            pl.BlockSpec((1, 1, fc, d), lambda e, f: (layer, e, f, 0)),
        ],
        out_specs=[pl.BlockSpec((1, rp, d), lambda e, f: (e, 0, 0)),
                   pl.BlockSpec((1, rs, d), lambda e, f: (e, 0, 0))],
        out_shape=[jax.ShapeDtypeStruct((n_exp, rp, d), F32),
                   jax.ShapeDtypeStruct((n_exp, rs, d), F32)],
        compiler_params=_params(2, 58),
        name="expert_ffn",
    )(xs_p, xs_s, g_p, g_s, w_gate, w_up, w_down)


def _combine_onehot_kernel(*refs, mod_row, final):
    if final:
        idx_ref, ye_ref, x_ref, pmod_ref, gain_ref, o_ref = refs
    else:
        idx_ref, ye_ref, x_ref, pmod_ref, o_ref = refs
    t_len, d = o_ref.shape
    n_exp = ye_ref.shape[0]
    idx = idx_ref[0]
    hit = jnp.where(idx == lax.broadcasted_iota(I32, (t_len, idx.shape[1]), 0), 1.0, 0.0).astype(BF16)
    y_hi, y_lo = _split_bf16(jnp.concatenate([ye_ref[e] for e in range(n_exp)], axis=0))
    acc = jnp.dot(hit, y_hi, preferred_element_type=F32) + jnp.dot(hit, y_lo, preferred_element_type=F32)
    x = x_ref[...] + pmod_ref[pl.ds(mod_row(pl.program_id(0), t_len), 1), pl.ds(5 * d, d)] * acc
    if final:
        x = (x * lax.rsqrt(jnp.mean(x * x, axis=-1, keepdims=True) + EPS)) * gain_ref[...]
    o_ref[...] = x


def _combine_onehot(idx_row, ye, x, pmod, final_gain, *, seq_len, mod_row):
    n_exp, rows, d = ye.shape
    n_seq, _, slots = idx_row.shape
    cap = slots // n_exp
    in_specs = [pl.BlockSpec((1, 1, slots), lambda s: (s, 0, 0)),
                pl.BlockSpec((n_exp, cap, d), lambda s: (0, s, 0)),
                pl.BlockSpec((seq_len, d), lambda s: (s, 0)),
                pl.BlockSpec(pmod.shape, lambda s: (0, 0))]
    args = [idx_row, ye, x, pmod]
    if final_gain is not None:
        in_specs.append(pl.BlockSpec((1, d), lambda s: (0, 0)))
        args.append(final_gain)
    return pl.pallas_call(
        functools.partial(_combine_onehot_kernel, mod_row=mod_row, final=final_gain is not None),
        grid=(n_seq,),
        in_specs=in_specs,
        out_specs=pl.BlockSpec((seq_len, d), lambda s: (s, 0)),
        out_shape=jax.ShapeDtypeStruct((n_seq * seq_len, d), F32),
        compiler_params=_params(1, 32),
        name="combine_onehot",
    )(*args)


COMBINE_GROUP = 16
ROW_LOOP_EXPERTS = 2


def _combine_rows_kernel(idx_ref, ye_ref, o_ref, tiles, *, cap, n_exp):
    e = pl.program_id(1)
    d = ye_ref.shape[2]
    tpr = _tiles_per_row(d)
    ch = 2048

    @pl.when(e == 0)
    def _():
        def zero(c, carry):
            o_ref[pl.ds(pl.multiple_of(c * ch, ch), ch), :] = jnp.zeros((ch, LANES), F32)
            return carry
        lax.fori_loop(0, o_ref.shape[0] // ch, zero, 0)

    for el in range(ROW_LOOP_EXPERTS):
        tiles[...] = ye_ref[el].reshape(cap, tpr, LANES).reshape(cap * tpr, LANES)

        base = (pl.program_id(0) * n_exp + e * ROW_LOOP_EXPERTS + el) * cap

        def body(i, carry, base=base):
            c0 = i * COMBINE_GROUP
            dst = [pl.ds(pl.multiple_of(idx_ref[base + c0 + k], tpr), tpr) for k in range(COMBINE_GROUP)]
            vals = [o_ref[dst[k], :] + tiles[pl.ds(pl.multiple_of((c0 + k) * tpr, tpr), tpr), :]
                    for k in range(COMBINE_GROUP)]
            for k in range(COMBINE_GROUP):
                o_ref[dst[k], :] = vals[k]
            return carry

        lax.fori_loop(0, cap // COMBINE_GROUP, body, 0)


def _combine_rows(idx, ye, *, seq_len):
    n_exp, rows, d = ye.shape
    tpr = _tiles_per_row(d)
    cap = CAPACITY_FACTOR * seq_len // N_EXPERTS
    n_seq = rows // cap
    return pl.pallas_call(
        functools.partial(_combine_rows_kernel, cap=cap, n_exp=n_exp),
        grid_spec=pltpu.PrefetchScalarGridSpec(
            num_scalar_prefetch=1,
            grid=(n_seq, n_exp // ROW_LOOP_EXPERTS),
            in_specs=[pl.BlockSpec((ROW_LOOP_EXPERTS, cap, d), lambda s, e, i: (e, s, 0))],
            out_specs=pl.BlockSpec((seq_len * tpr, LANES), lambda s, e, i: (s, 0)),
            scratch_shapes=[pltpu.VMEM((cap * tpr, LANES), F32)],
        ),
        out_shape=jax.ShapeDtypeStruct((n_seq * seq_len * tpr, LANES), F32),
        compiler_params=_params(2, 48),
        name="combine_rows",
    )(idx, ye)


def _head_norm(x, gain):
    return x * lax.rsqrt(jnp.mean(x * x, axis=-1, keepdims=True) + EPS) * gain


def _qkv_kernel(*refs, rope, has_prev, mod_row, n_heads, n_kv):
    refs = list(refs)
    x_ref = refs.pop(0)
    prev = None
    if has_prev:
        acc_ref, pmod_ref = refs.pop(0), refs.pop(0)
    mod_ref, gain_ref, w_ref, qg_ref, kg_ref = refs[:5]
    refs = refs[5:]
    if rope:
        cos_ref, sin_ref = refs.pop(0), refs.pop(0)
    if has_prev:
        prev = (acc_ref, pmod_ref, refs.pop(0))
    q_ref, k_ref, v_ref = refs[:3]
    d = x_ref.shape[1]
    hd = HEAD_DIM
    m = mod_row(pl.program_id(0), x_ref.shape[0])
    x = _residual_stream(x_ref, prev, m)
    shift = mod_ref[pl.ds(m, 1), pl.ds(0, d)]
    scale = mod_ref[pl.ds(m, 1), pl.ds(d, d)]
    h = _modulated_norm(x, gain_ref[...], shift, scale).astype(BF16)
    qg = qg_ref[...]
    kg = kg_ref[...]
    q_scale = math.log2(math.e) * hd ** -0.5
    half = x_ref.shape[0] // 2
    for r0 in (0, half):
        rows = slice(r0, r0 + half)
        qkv = jnp.dot(h[rows], w_ref[...], preferred_element_type=F32)
        if rope:
            cos = cos_ref[rows, :]
            sin = sin_ref[rows, :]

            def rot(xh, cos=cos, sin=sin):
                return xh * cos + pltpu.roll(xh, hd // 2, 1) * sin
        else:
            rot = lambda xh: xh

        for i in range(n_heads):
            qh = rot(_head_norm(qkv[:, i * hd:(i + 1) * hd], qg)) * q_scale
            q_ref[rows, i * hd:(i + 1) * hd] = qh.astype(BF16)
        for i in range(n_kv):
            c0 = (n_heads + i) * hd
            kh = _head_norm(qkv[:, c0:c0 + hd], kg)
            if not rope:
                refs[3][rows, i * hd:(i + 1) * hd] = kh
            k_ref[rows, i * hd:(i + 1) * hd] = rot(kh).astype(BF16)
        v = qkv[:, (n_heads + n_kv) * hd:]
        v_ref[rows, :] = v.astype(BF16)
        if not rope:
            refs[4][rows, :] = v


def _qkv(x, prev, mod, gain, w_qkv, q_gain, k_gain, tables, *, seq_len, mod_row, n_heads, n_kv):
    n, d = x.shape
    hd = HEAD_DIM
    row = lambda i: (i, 0)
    fixed = lambda i: (0, 0)
    rope = tables is not None
    rb = QKV_ROW_BLOCK
    in_specs = [pl.BlockSpec((rb, d), row)]
    args = [x]
    if prev is not None:
        in_specs += [_rows_spec(prev[0], d, rb), pl.BlockSpec(prev[1].shape, fixed)]
        args += list(prev)
    in_specs += [
        pl.BlockSpec(mod.shape, fixed),
        pl.BlockSpec((1, d), fixed),
        pl.BlockSpec(w_qkv.shape, fixed),
        pl.BlockSpec((1, hd), fixed),
        pl.BlockSpec((1, hd), fixed),
    ]
    args += [mod, gain, w_qkv, q_gain, k_gain]
    if rope:
        per_seq = seq_len // rb
        in_specs += [pl.BlockSpec((rb, hd), lambda i: (i % per_seq, 0))] * 2
        args += list(tables)
    out_specs, out_shape = [], []
    if prev is not None:
        out_specs.append(pl.BlockSpec((rb, d), row))
        out_shape.append(jax.ShapeDtypeStruct((n, d), F32))
    out_specs += [pl.BlockSpec((rb, n_heads * hd), row),
                  pl.BlockSpec((rb, n_kv * hd), row),
                  pl.BlockSpec((rb, n_kv * hd), row)]
    out_shape += [jax.ShapeDtypeStruct((n, n_heads * hd), BF16),
                  jax.ShapeDtypeStruct((n, n_kv * hd), BF16),
                  jax.ShapeDtypeStruct((n, n_kv * hd), BF16)]
    if not rope:
        out_specs += [pl.BlockSpec((rb, n_kv * hd), row)] * 2
        out_shape += [jax.ShapeDtypeStruct((n, n_kv * hd), F32)] * 2
    outs = pl.pallas_call(
        functools.partial(_qkv_kernel, rope=rope, has_prev=prev is not None, mod_row=mod_row,
                          n_heads=n_heads, n_kv=n_kv),
        grid=(n // rb,),
        in_specs=in_specs,
        out_specs=out_specs,
        out_shape=out_shape,
        compiler_params=_params(1, 48),
        name="qkv_rope" if rope else "qkv",
    )(*args)
    return list(outs) if prev is not None else [x] + list(outs)


def _attn_kernel(*refs, group, chunk, has_cache):
    if has_cache:
        q_ref, k_ref, v_ref, kc_ref, vc_ref, o_ref = refs
    else:
        q_ref, k_ref, v_ref, o_ref = refs
    hd = HEAD_DIM
    qb = q_ref.shape[0]
    rows = group * qb
    q = jnp.concatenate([q_ref[:, g * hd:(g + 1) * hd] for g in range(group)], axis=0)
    sources = [(k_ref, v_ref, s0, chunk) for s0 in range(0, k_ref.shape[0], chunk)]
    if has_cache:
        sources.append((kc_ref.at[0], vc_ref.at[0], 0, kc_ref.shape[1]))
    m = jnp.full((rows, 1), -jnp.inf, F32)
    acc = jnp.zeros((rows, 2 * hd), F32)
    for kr, vr, s0, size in sources:
        s = lax.dot_general(q, kr[s0:s0 + size, :], (((1,), (1,)), ((), ())),
                            preferred_element_type=F32)
        m_new = jnp.maximum(m, jnp.max(s, axis=-1, keepdims=True))
        p = jnp.exp2(s - m_new).astype(BF16)
        v_ones = jnp.concatenate([vr[s0:s0 + size, :], jnp.ones((size, hd), BF16)], axis=1)
        acc = jnp.exp2(m - m_new) * acc + jnp.dot(p, v_ones, preferred_element_type=F32)
        m = m_new
    o = acc[:, :hd] / acc[:, hd:]
    o_ref[...] = jnp.concatenate([o[g * qb:(g + 1) * qb] for g in range(group)], axis=1).astype(BF16)


def _attention(q, k, v, cache, *, seq_len, qb, chunk, n_heads, n_kv):
    n = q.shape[0]
    hd = HEAD_DIM
    group = n_heads // n_kv
    nq = seq_len // qb
    in_specs = [
        pl.BlockSpec((qb, group * hd), lambda b, h, i: (b * nq + i, h)),
        pl.BlockSpec((seq_len, hd), lambda b, h, i: (b, h)),
        pl.BlockSpec((seq_len, hd), lambda b, h, i: (b, h)),
    ]
    args = [q, k, v]
    if cache is not None:
        past = cache[0].shape[1]
        in_specs += [pl.BlockSpec((1, past, hd), lambda b, h, i: (b, 0, h))] * 2
        args += list(cache)
    return pl.pallas_call(
        functools.partial(_attn_kernel, group=group, chunk=chunk, has_cache=cache is not None),
        grid=(n // seq_len, n_kv, nq),
        in_specs=in_specs,
        out_specs=pl.BlockSpec((qb, group * hd), lambda b, h, i: (b * nq + i, h)),
        out_shape=jax.ShapeDtypeStruct((n, n_heads * hd), BF16),
        compiler_params=_params(3, 48),
        name=f"attention_{seq_len}",
    )(*args)


def _final_norm_kernel(x_ref, acc_ref, pmod_ref, gain_ref, o_ref, *, mod_row):
    n_rows, d = x_ref.shape
    m = mod_row(pl.program_id(0), x_ref.shape[0])
    x = x_ref[...] + pmod_ref[pl.ds(m, 1), pl.ds(5 * d, d)] * _load_rows(acc_ref, n_rows, d)
    o_ref[...] = (x * lax.rsqrt(jnp.mean(x * x, axis=-1, keepdims=True) + EPS)) * gain_ref[...]


def _final_norm(x, acc, pmod, gain, mod_row):
    n, d = x.shape
    row = lambda i: (i, 0)
    fixed = lambda i: (0, 0)
    return pl.pallas_call(
        functools.partial(_final_norm_kernel, mod_row=mod_row),
        grid=(n // ROW_BLOCK,),
        in_specs=[pl.BlockSpec((ROW_BLOCK, d), row), _rows_spec(acc, d, ROW_BLOCK),
                  pl.BlockSpec(pmod.shape, fixed), pl.BlockSpec((1, d), fixed)],
        out_specs=pl.BlockSpec((ROW_BLOCK, d), row),
        out_shape=jax.ShapeDtypeStruct((n, d), F32),
        compiler_params=_params(1, 32),
        name="final_norm",
    )(x, acc, pmod, gain)


def _rope_head_order(x):
    q = HEAD_DIM // 4
    row1, row2, col1, col2 = (x[..., i * q:(i + 1) * q] for i in range(4))
    return jnp.concatenate([row1, col1, row2, col2], axis=-1)


def _rope_tables(seq_len):
    axis = HEAD_DIM // 2
    t = np.arange(seq_len)
    inv = ROPE_BASE ** (-np.arange(axis // 2, dtype=np.float64) * 2.0 / axis)
    ang = np.concatenate([(t // GRID_W)[:, None] * inv, (t % GRID_W)[:, None] * inv], axis=1)
    c, s = np.cos(ang), np.sin(ang)
    return (jnp.asarray(np.concatenate([c, c], axis=1), F32), jnp.asarray(np.concatenate([-s, s], axis=1), F32))


def kernel(x_prompt, x_sample, state_lru, cache_k, cache_v, c, c_ctx, w_mod, b_mod, norm_gain, final_gain,
           w_lru_in, lru_conv_w, lru_conv_b, lru_wa, lru_ba, lru_wx, lru_bx, lru_lambda, w_lru_out,
           w_qkv, q_norm, k_norm, w_attn_out, w_router, w_exp_gate, w_exp_up, w_exp_down):
    batch, seq, d = x_prompt.shape
    dec_batch, dec_seq, _ = x_sample.shape
    depth = w_mod.shape[0]
    n_kv = cache_k.shape[3]
    n_heads = w_attn_out.shape[1] // HEAD_DIM
    heads = dict(n_heads=n_heads, n_kv=n_kv)

    cond = jnp.zeros((SUBLANES, d), F32).at[0].set(c_ctx).at[1:1 + dec_batch].set(c)
    mod_all = _mod_vectors(cond, w_mod, b_mod)

    xs = [x_prompt.reshape(batch * seq, d), x_sample.reshape(dec_batch * dec_seq, d)]
    lens = [seq, dec_seq]
    mod_rows = [lambda i, rows: 0, lambda i, rows: 1 + (i * rows) // dec_seq]
    prevs = [None, None]

    fg = final_gain.reshape(1, d)
    wr_hi, wr_lo = _split_bf16(jnp.swapaxes(w_router, 1, 2))
    new_lru, new_k, new_v = [], [], []
    for l in range(depth):
        mod = mod_all[l]
        gain_a = norm_gain[l, 0].reshape(1, d)
        gain_c = norm_gain[l, 1].reshape(1, d)
        mixed = []
        if l % 2 == 0:
            li = l // 2
            w_in = w_lru_in[li].astype(BF16)
            wg = (0.5 * jnp.concatenate([lru_wa[li, 0], lru_wx[li, 0], lru_wa[li, 1], lru_wx[li, 1]],
                                        axis=2)).astype(BF16)
            bias4 = 0.5 * jnp.stack([lru_ba[li, 0], lru_bx[li, 0], lru_ba[li, 1], lru_bx[li, 1]])
            h0s = [jnp.zeros((batch, 2, w_in.shape[1] // 2), F32), state_lru[:, li]]
            for gi in range(2):
                xs[gi], gate, xr = _lru_in(xs[gi], prevs[gi], mod, gain_a, w_in, mod_rows[gi])
                y, fin = _lru_core(xr, gate, lru_conv_w[li], lru_conv_b[li].reshape(1, -1), wg, bias4,
                                   lru_lambda[li], h0s[gi], seq_len=lens[gi])
                mixed.append(y)
                if gi == 0:
                    new_lru.append(fin.astype(x_prompt.dtype))
            w_out = w_lru_out[li].astype(BF16)
        else:
            ai = l // 2
            w = w_qkv[ai].astype(BF16)
            qg = q_norm[ai].reshape(1, -1)
            kg = k_norm[ai].reshape(1, -1)
            xs[0], q_p, k_p, v_p, kf, vf = _qkv(xs[0], prevs[0], mod, gain_a, w, qg, kg, None,
                                                seq_len=seq, mod_row=mod_rows[0], **heads)
            n_qk = (n_heads + n_kv) * HEAD_DIM
            w_qk = _rope_head_order(w_qkv[ai][:, :n_qk].reshape(d, n_heads + n_kv, HEAD_DIM)).reshape(d, n_qk)
            w_r = jnp.concatenate([w_qk, w_qkv[ai][:, n_qk:]], axis=1).astype(BF16)
            xs[1], q_s, k_s, v_s = _qkv(xs[1], prevs[1], mod, gain_a, w_r, _rope_head_order(qg),
                                        _rope_head_order(kg), _rope_tables(dec_seq),
                                        seq_len=dec_seq, mod_row=mod_rows[1], **heads)
            new_k.append(kf.reshape(batch, seq, n_kv, HEAD_DIM))
            new_v.append(vf.reshape(batch, seq, n_kv, HEAD_DIM))
            past = cache_k.shape[2]
            kc = _rope_head_order(cache_k[:, ai]).reshape(dec_batch, past, n_kv * HEAD_DIM).astype(BF16)
            vc = cache_v[:, ai].reshape(dec_batch, past, n_kv * HEAD_DIM).astype(BF16)
            mixed.append(_attention(q_p, k_p, v_p, None, seq_len=seq, qb=seq, chunk=seq, **heads))
            mixed.append(_attention(q_s, k_s, v_s, (kc, vc), seq_len=dec_seq, qb=512, chunk=256, **heads))
            w_out = w_attn_out[ai].astype(BF16)

        x1_p, h2_p, lg_p = _mix_out(mixed[0], xs[0], mod, gain_c, w_out, wr_hi, wr_lo, l, mod_rows[0], False)
        x1_s, h2_s, lg_s = _mix_out(mixed[1], xs[1], mod, gain_c, w_out, wr_hi, wr_lo, l, mod_rows[1], True)
        idx_p, g_p, idx_s, g_s = _route(lg_p, lg_s, seq, dec_seq)
        xe_p = _gather_onehot(idx_p[:, None, :], h2_p, seq_len=seq)
        off_s = idx_s * _tiles_per_row(d)
        xe_s = _gather_rows(off_s, h2_s, d, seq_len=dec_seq)
        e_n = N_EXPERTS
        zero_rows = ((0, 0), (0, SUBLANES - 1), (0, 0))
        gc_p = jnp.pad(g_p.reshape(batch, e_n, -1).transpose(1, 0, 2).reshape(e_n, 1, -1), zero_rows)
        gc_s = jnp.pad(g_s.reshape(dec_batch, e_n, -1).transpose(1, 0, 2).reshape(e_n, 1, -1), zero_rows)
        ye_p, ye_s = _ffn(xe_p, xe_s, gc_p, gc_s, w_exp_gate, w_exp_up, w_exp_down, l)
        last = l == depth - 1
        x_p = _combine_onehot(idx_p[:, None, :], ye_p, x1_p, mod, fg if last else None, seq_len=seq,
                              mod_row=mod_rows[0])
        acc_s = _combine_rows(off_s, ye_s, seq_len=dec_seq)
        xs = [x_p, x1_s]
        prevs = [None, (acc_s, mod)]

    y_prompt = xs[0].reshape(batch, seq, d)
    y_sample = _final_norm(xs[1], *prevs[1], fg, mod_rows[1]).reshape(dec_batch, dec_seq, d)
    return (y_prompt, y_sample, jnp.stack(new_lru, axis=1), jnp.stack(new_k, axis=1), jnp.stack(new_v, axis=1))
```

```python
import functools
import math

import jax
import jax.numpy as jnp
import numpy as np
from jax import lax
from jax.experimental import pallas as pl
from jax.experimental.pallas import tpu as pltpu

F32 = jnp.float32
BF16 = jnp.bfloat16
I32 = jnp.int32

LANES = 128
SUBLANES = 8
MIB = 1024 * 1024
F32_TINY = float(np.finfo(np.float32).tiny)

RG_C = 8.0
CONV_W = 4
CONV_PAD_L = 2
ROPE_BASE = 10000.0
GRID_W = 64
EPS = 1e-6
N_EXPERTS = 16
CAPACITY_FACTOR = 2
HEAD_DIM = 128
LRU_BLOCKS = 8
LRU_MAX_STEP_BLOCKS = 4
LRU_CORE_VMEM_MIB = 44

ROW_BLOCK = 1024
QKV_ROW_BLOCK = 512


def _params(n_axes, vmem_mib):
    return pltpu.CompilerParams(
        dimension_semantics=("arbitrary",) * n_axes, vmem_limit_bytes=vmem_mib * MIB)


def _split_bf16(x):
    hi = x.astype(BF16)
    lo = (x - hi.astype(F32)).astype(BF16)
    return hi, lo


def _modulated_norm(x, gain, shift, scale):
    y = x * lax.rsqrt(jnp.mean(x * x, axis=-1, keepdims=True) + EPS)
    return (y * gain) * (1.0 + scale) + shift


def _tiles_per_row(d):
    return d // LANES


def _load_rows(ref, n_rows, d):
    if ref.shape == (n_rows, d):
        return ref[...]
    tpr = _tiles_per_row(d)
    return jnp.concatenate([ref[pl.ds(s, n_rows, stride=tpr), :] for s in range(tpr)], axis=1)


def _store_rows(ref, x, r0):
    n_rows, d = x.shape
    if ref.shape[1] == d:
        ref[r0:r0 + n_rows, :] = x.astype(ref.dtype)
        return
    tpr = _tiles_per_row(d)
    for s in range(tpr):
        ref[pl.ds(r0 * tpr + s, n_rows, stride=tpr), :] = x[:, s * LANES:(s + 1) * LANES]


def _rows_spec(arr, d, rows):
    if arr.shape[1] == d:
        return pl.BlockSpec((rows, d), lambda i: (i, 0))
    return pl.BlockSpec((rows * _tiles_per_row(d), LANES), lambda i: (i, 0))


def _residual_stream(x_ref, prev, m):
    if prev is None:
        return x_ref[...]
    acc_ref, pmod_ref, xo_ref = prev
    n_rows, d = x_ref.shape
    x = x_ref[...] + pmod_ref[pl.ds(m, 1), pl.ds(5 * d, d)] * _load_rows(acc_ref, n_rows, d)
    xo_ref[...] = x
    return x


def _mod_kernel(cond_ref, w_ref, b_ref, o_ref):
    c = cond_ref[...]
    a_hi, a_lo = _split_bf16(c * jax.nn.sigmoid(c))
    w_hi, w_lo = _split_bf16(w_ref[0])
    dot = functools.partial(jnp.dot, preferred_element_type=F32)
    o_ref[0] = dot(a_hi, w_hi) + (dot(a_lo, w_hi) + dot(a_hi, w_lo)) + b_ref[0]


def _mod_vectors(cond8, w_mod, b_mod):
    depth, d, d6 = w_mod.shape
    nb = 1536
    return pl.pallas_call(
        _mod_kernel,
        grid=(depth, d6 // nb),
        in_specs=[
            pl.BlockSpec((SUBLANES, d), lambda l, j: (0, 0)),
            pl.BlockSpec((1, d, nb), lambda l, j: (l, 0, j)),
            pl.BlockSpec((1, 1, nb), lambda l, j: (l, 0, j)),
        ],
        out_specs=pl.BlockSpec((1, SUBLANES, nb), lambda l, j: (l, 0, j)),
        out_shape=jax.ShapeDtypeStruct((depth, SUBLANES, d6), F32),
        compiler_params=_params(2, 40),
        name="adaln_mod",
    )(cond8, w_mod, b_mod.reshape(depth, 1, d6))


def _lru_in_kernel(*refs, mod_row, has_prev):
    if has_prev:
        x_ref, acc_ref, pmod_ref, mod_ref, gain_ref, w_ref, xo_ref, gate_ref, xr_ref = refs
        prev = (acc_ref, pmod_ref, xo_ref)
    else:
        x_ref, mod_ref, gain_ref, w_ref, gate_ref, xr_ref = refs
        prev = None
    d = x_ref.shape[1]
    m = mod_row(pl.program_id(0), x_ref.shape[0])
    x = _residual_stream(x_ref, prev, m)
    shift = mod_ref[pl.ds(m, 1), pl.ds(0, d)]
    scale = mod_ref[pl.ds(m, 1), pl.ds(d, d)]
    h = _modulated_norm(x, gain_ref[...], shift, scale).astype(BF16)
    u = jnp.dot(h, w_ref[...], preferred_element_type=F32)
    w = gate_ref.shape[1]
    gate_ref[...] = u[:, :w]
    xr_ref[...] = u[:, w:]


def _lru_in(x, prev, mod, gain, w_in, mod_row):
    n, d = x.shape
    w2 = w_in.shape[1]
    w = w2 // 2
    row = lambda i: (i, 0)
    fixed = lambda i: (0, 0)
    in_specs = [pl.BlockSpec((ROW_BLOCK, d), row)]
    args = [x]
    out_specs = [pl.BlockSpec((ROW_BLOCK, w), row), pl.BlockSpec((ROW_BLOCK, w), row)]
    out_shape = [jax.ShapeDtypeStruct((n, w), F32), jax.ShapeDtypeStruct((n, w), F32)]
    if prev is not None:
        in_specs += [_rows_spec(prev[0], d, ROW_BLOCK), pl.BlockSpec(prev[1].shape, fixed)]
        args += list(prev)
        out_specs = [pl.BlockSpec((ROW_BLOCK, d), row)] + out_specs
        out_shape = [jax.ShapeDtypeStruct((n, d), F32)] + out_shape
    in_specs += [pl.BlockSpec(mod.shape, fixed), pl.BlockSpec((1, d), fixed), pl.BlockSpec((d, w2), fixed)]
    args += [mod, gain, w_in]
    outs = pl.pallas_call(
        functools.partial(_lru_in_kernel, mod_row=mod_row, has_prev=prev is not None),
        grid=(n // ROW_BLOCK,),
        in_specs=in_specs,
        out_specs=out_specs,
        out_shape=out_shape,
        compiler_params=_params(1, 48),
        name="lru_in",
    )(*args)
    return outs if prev is not None else [x] + list(outs)


def _lru_core_kernel(xr_ref, gate_ref, cw_ref, cb_ref, wg_ref, bias_ref, lam_ref, h0_ref,
                     y_ref, fin_ref, pad, a_f, b_f, a_b, b_b):
    t_len, lb = xr_ref.shape
    ch = 256
    halo = SUBLANES
    seg = t_len // SUBLANES
    piece = min(ch, seg)
    seg_shift = seg.bit_length() - 1

    def seg_rows(t0):
        r = lax.shift_right_logical(t0, seg_shift)
        return pl.ds((t0 - r * seg) * SUBLANES + r, piece, stride=SUBLANES)

    zero_rows = jnp.zeros((halo, lb), F32)
    pad[0:halo, :] = zero_rows
    pad[t_len + halo:t_len + 2 * halo, :] = zero_rows

    def copy_in(c, carry):
        r = pl.multiple_of(c * ch, ch)
        pad[pl.ds(r + halo, ch), :] = xr_ref[pl.ds(r, ch), :]
        return carry

    lax.fori_loop(0, t_len // ch, copy_in, 0)

    neg_lam = -lam_ref[...]
    softplus = jnp.maximum(neg_lam, 0.0) + jnp.log1p(jnp.exp(-jnp.abs(neg_lam)))
    c_nla = (0.5 * RG_C) * softplus
    c_exp2 = (-0.5 * RG_C * math.log2(math.e)) * softplus
    cw = cw_ref[...]
    cbias = cb_ref[...]
    half_bias = bias_ref[...]
    half_wg = wg_ref[...]
    bw = half_wg.shape[1]
    n_slab = lb // LANES

    def gates(c, carry):
        r = pl.multiple_of(c * ch, ch)
        blk = pad[pl.ds(r, ch + 2 * halo), :]
        xc = cbias
        for k in range(CONV_W):
            o = halo - CONV_PAD_L + k
            xc = xc + blk[o:o + ch] * cw[k:k + 1]
        xc_bf = xc.astype(BF16)
        g = [jnp.dot(xc_bf[:, j * bw:(j + 1) * bw], half_wg[j], preferred_element_type=F32)
             for j in range(lb // bw)]
        pre = lambda k: jnp.concatenate([gj[:, k * bw:(k + 1) * bw] for gj in g], axis=1) + half_bias[k:k + 1]
        half_xc = 0.5 * xc
        for d, (a_ref, b_ref) in enumerate(((a_f, b_f), (a_b, b_b))):
            u = jnp.tanh(pre(2 * d)) + 1.0
            t_i = jnp.tanh(pre(2 * d + 1))
            a = jnp.exp2(u * c_exp2[d:d + 1])
            q = jnp.tanh(u * c_nla[d:d + 1]) * (1.0 + a * a)
            b = (q * lax.rsqrt(jnp.maximum(q, F32_TINY))) * (t_i * half_xc + half_xc)
            for p in range(ch // piece):
                rows = seg_rows(r + p * piece)
                for j in range(n_slab):
                    a_ref[j, rows, :] = a[p * piece:(p + 1) * piece, j * LANES:(j + 1) * LANES]
                    b_ref[j, rows, :] = b[p * piece:(p + 1) * piece, j * LANES:(j + 1) * LANES]
        return carry

    lax.fori_loop(0, t_len // ch, gates, 0)

    def scan(i, carry):
        rows = (pl.ds(pl.multiple_of(i * SUBLANES, SUBLANES), SUBLANES),
                pl.ds(pl.multiple_of((seg - 1 - i) * SUBLANES, SUBLANES), SUBLANES))
        out = []
        for k, (h, prod) in enumerate(carry):
            d, j = divmod(k, n_slab)
            a_ref, b_ref = ((a_f, b_f), (a_b, b_b))[d]
            a = a_ref[j, rows[d], :]
            h = a * h + b_ref[j, rows[d], :]
            prod = a * prod
            b_ref[j, rows[d], :] = h
            a_ref[j, rows[d], :] = prod
            out.append((h, prod))
        return tuple(out)

    start = (jnp.zeros((SUBLANES, LANES), F32), jnp.ones((SUBLANES, LANES), F32))
    ends = lax.fori_loop(0, seg, scan, (start,) * (2 * n_slab), unroll=8)

    enter = []
    for k, (h, prod) in enumerate(ends):
        d, j = divmod(k, n_slab)
        states = [h0_ref[0, d:d + 1, j * LANES:(j + 1) * LANES]]
        for r in (range(SUBLANES) if d == 0 else reversed(range(SUBLANES))):
            states.append(h[r:r + 1] + prod[r:r + 1] * states[-1])
        fin_ref[0, d:d + 1, j * LANES:(j + 1) * LANES] = states[SUBLANES]
        order = states[:SUBLANES] if d == 0 else states[:SUBLANES][::-1]
        enter.append(jnp.concatenate(order, axis=0))

    def fix(i, carry):
        rows = pl.ds(pl.multiple_of(i * SUBLANES, SUBLANES), SUBLANES)
        for j in range(n_slab):
            b_f[j, rows, :] = ((b_f[j, rows, :] + a_f[j, rows, :] * enter[j])
                               + (b_b[j, rows, :] + a_b[j, rows, :] * enter[n_slab + j]))
        return carry

    lax.fori_loop(0, seg, fix, 0, unroll=8)

    def emit(c, carry):
        r = pl.multiple_of(c * ch, ch)
        gt = gate_ref[pl.ds(r, ch), :]
        cdf = 0.5 * (1.0 + jnp.tanh(math.sqrt(2.0 / math.pi) * (gt + 0.044715 * (gt * gt * gt))))
        y = gt * cdf
        for p in range(ch // piece):
            rows = seg_rows(r + p * piece)
            total = jnp.concatenate([b_f[j, rows, :] for j in range(n_slab)], axis=1)
            y_ref[pl.ds(r + p * piece, piece), :] = (total * y[p * piece:(p + 1) * piece]).astype(BF16)
        return carry

    lax.fori_loop(0, t_len // ch, emit, 0)


def _lru_core(xr, gate, conv_w, conv_b, wg, bias4, lam, h0, *, seq_len):
    n, w = xr.shape
    n_seq = n // seq_len
    bw = w // LRU_BLOCKS
    block_bytes = seq_len * bw * (5 * 4 + 2 * 2 * 4 + 2 * 2)
    step_blocks = LRU_MAX_STEP_BLOCKS
    while step_blocks * block_bytes > LRU_CORE_VMEM_MIB * MIB:
        step_blocks //= 2
    lb = step_blocks * bw
    tok = lambda s, c: (s, c)
    chan = lambda s, c: (0, c)
    return pl.pallas_call(
        _lru_core_kernel,
        grid=(n_seq, LRU_BLOCKS // step_blocks),
        in_specs=[
            pl.BlockSpec((seq_len, lb), tok),
            pl.BlockSpec((seq_len, lb), tok),
            pl.BlockSpec((CONV_W, lb), chan),
            pl.BlockSpec((1, lb), chan),
            pl.BlockSpec((step_blocks, bw, 4 * bw), lambda s, c: (c, 0, 0)),
            pl.BlockSpec((4, lb), chan),
            pl.BlockSpec((2, lb), chan),
            pl.BlockSpec((1, 2, lb), lambda s, c: (s, 0, c)),
        ],
        out_specs=[pl.BlockSpec((seq_len, lb), tok),
                   pl.BlockSpec((1, 2, lb), lambda s, c: (s, 0, c))],
        out_shape=[jax.ShapeDtypeStruct((n, w), BF16), jax.ShapeDtypeStruct((n_seq, 2, w), F32)],
        scratch_shapes=[pltpu.VMEM((seq_len + 2 * SUBLANES, lb), F32)]
        + [pltpu.VMEM((lb // LANES, seq_len, LANES), F32)] * 4,
        compiler_params=_params(2, LRU_CORE_VMEM_MIB + 4),
        name=f"lru_core_{seq_len}",
    )(xr, gate, conv_w, conv_b, wg, bias4, lam, h0)


def _mix_out_kernel(a_ref, x_ref, mod_ref, gain_ref, w_ref, wrh_ref, wrl_ref,
                    x1_ref, h2_ref, lg_ref, *, mod_row):
    d = x_ref.shape[1]
    m = mod_row(pl.program_id(0), x_ref.shape[0])
    g_mix = mod_ref[pl.ds(m, 1), pl.ds(2 * d, d)]
    shift = mod_ref[pl.ds(m, 1), pl.ds(3 * d, d)]
    scale = mod_ref[pl.ds(m, 1), pl.ds(4 * d, d)]
    op = jnp.dot(a_ref[...], w_ref[...], preferred_element_type=F32)
    x1 = x_ref[...] + g_mix * op
    x1_ref[...] = x1
    h2 = _modulated_norm(x1, gain_ref[...], shift, scale)
    h_hi, h_lo = _split_bf16(h2)
    _store_rows(h2_ref, h2, 0)
    nt = functools.partial(lax.dot_general, dimension_numbers=(((1,), (1,)), ((), ())),
                           preferred_element_type=F32)
    wr_hi = wrh_ref[0]
    e = wr_hi.shape[0]
    both = nt(jnp.concatenate([wr_hi, wrl_ref[0]], axis=0), h_hi)
    lg_ref[...] = both[:e] + (both[e:] + nt(wr_hi, h_lo))


def _mix_out(a, x, mod, gain, w_out, wr_hi, wr_lo, layer, mod_row, token_tiled):
    n, d = x.shape
    e = wr_hi.shape[1]
    row = lambda i: (i, 0)
    fixed = lambda i: (0, 0)
    router = pl.BlockSpec((1,) + wr_hi.shape[1:], lambda i: (layer, 0, 0))
    tpr = _tiles_per_row(d)
    if token_tiled:
        h2_spec = pl.BlockSpec((ROW_BLOCK * tpr, LANES), row)
        h2_shape = jax.ShapeDtypeStruct((n * tpr, LANES), F32)
    else:
        h2_spec = pl.BlockSpec((ROW_BLOCK, d), row)
        h2_shape = jax.ShapeDtypeStruct((n, d), BF16)
    return pl.pallas_call(
        functools.partial(_mix_out_kernel, mod_row=mod_row),
        grid=(n // ROW_BLOCK,),
        in_specs=[
            pl.BlockSpec((ROW_BLOCK, a.shape[1]), row),
            pl.BlockSpec((ROW_BLOCK, d), row),
            pl.BlockSpec(mod.shape, fixed),
            pl.BlockSpec((1, d), fixed),
            pl.BlockSpec(w_out.shape, fixed),
            router,
            router,
        ],
        out_specs=[pl.BlockSpec((ROW_BLOCK, d), row), h2_spec,
                   pl.BlockSpec((e, ROW_BLOCK), lambda i: (0, i))],
        out_shape=[jax.ShapeDtypeStruct((n, d), F32), h2_shape, jax.ShapeDtypeStruct((e, n), F32)],
        compiler_params=_params(1, 48),
        name="mix_out",
    )(a, x, mod, gain, w_out, wr_hi, wr_lo)


def _lane_sum(tiles):
    acc = tiles[0]
    for t in tiles[1:]:
        acc = acc + t
    return jnp.sum(acc, axis=1, keepdims=True)


def _exclusive_cumsum(flags, upper):
    out = []
    off = jnp.zeros((flags[0].shape[0], 1), F32)
    for f in flags:
        out.append(jnp.dot(f.astype(BF16), upper, preferred_element_type=F32) + off)
        off = off + jnp.sum(f, axis=1, keepdims=True)
    return out


def _route_group(lg_ref, n_seq, t_len, idx_ref, g_ref):
    cap = CAPACITY_FACTOR * t_len // N_EXPERTS
    affs = []
    for s in range(n_seq):
        lg = lg_ref[:, s * t_len:(s + 1) * t_len]
        ex = jnp.exp(lg - jnp.max(lg, axis=0, keepdims=True))
        affs.append(ex / jnp.sum(ex, axis=0, keepdims=True))
    aff = jnp.concatenate(affs, axis=0)
    n_rows = aff.shape[0]
    nt = t_len // LANES
    g = [aff[:, j * LANES:(j + 1) * LANES] for j in range(nt)]

    kth_bits = jnp.zeros((n_rows, 1), I32)
    for bit in range(30, -1, -1):
        cand = kth_bits | (1 << bit)
        cand_f = pltpu.bitcast(cand, F32)
        cnt = _lane_sum([jnp.where(t >= cand_f, 1, 0) for t in g])
        kth_bits = jnp.where(cnt >= cap, cand, kth_bits)
    kth = pltpu.bitcast(kth_bits, F32)

    lane = lax.broadcasted_iota(I32, (LANES, LANES), 0)
    upper = jnp.where(lane < lax.broadcasted_iota(I32, (LANES, LANES), 1), 1.0, 0.0).astype(BF16)
    gt = [t > kth for t in g]
    eq = [t == kth for t in g]
    need = (cap - _lane_sum([jnp.where(m, 1, 0) for m in gt])).astype(F32)
    eq_rank = _exclusive_cumsum([jnp.where(m, 1.0, 0.0) for m in eq], upper)
    sel = [jnp.logical_or(gt[j], jnp.logical_and(eq[j], eq_rank[j] < need)) for j in range(nt)]
    pos = _exclusive_cumsum([jnp.where(m, 1.0, 0.0) for m in sel], upper)

    lane_r = lax.broadcasted_iota(I32, (n_rows, LANES), 1)
    d = [jnp.where(sel[j], lane_r + j * LANES - pos[j].astype(I32), -1) for j in range(nt)]
    for k in range(t_len.bit_length() - 1):
        s = 1 << k
        if s < LANES:
            d_rot = [pltpu.roll(x, LANES - s, 1) for x in d]
            g_rot = [pltpu.roll(x, LANES - s, 1) for x in g]
            same = lane_r < LANES - s
            d_in = [jnp.where(same, d_rot[j], d_rot[(j + 1) % nt]) for j in range(nt)]
            g_in = [jnp.where(same, g_rot[j], g_rot[(j + 1) % nt]) for j in range(nt)]
        else:
            q = s // LANES
            d_in = [d[(j + q) % nt] for j in range(nt)]
            g_in = [g[(j + q) % nt] for j in range(nt)]
        new_d, new_g = [], []
        for j in range(nt):
            move = jnp.logical_and(d_in[j] >= 0, ((d_in[j] >> k) & 1) == 1)
            stay = jnp.logical_and(d[j] >= 0, ((d[j] >> k) & 1) == 0)
            new_d.append(jnp.where(move, d_in[j], jnp.where(stay, d[j], -1)))
            new_g.append(jnp.where(move, g_in[j], g[j]))
        d, g = new_d, new_g

    for j in range(idx_ref.shape[1] // LANES):
        idx_ref[:, j * LANES:(j + 1) * LANES] = lane_r + j * LANES + d[j]
        g_ref[:, j * LANES:(j + 1) * LANES] = g[j]


def _route_kernel(lgp_ref, lgs_ref, idx_p_ref, g_p_ref, idx_s_ref, g_s_ref, *, p_len, s_len):
    _route_group(lgp_ref, lgp_ref.shape[1] // p_len, p_len, idx_p_ref, g_p_ref)
    _route_group(lgs_ref, lgs_ref.shape[1] // s_len, s_len, idx_s_ref, g_s_ref)


def _route(lg_p, lg_s, p_len, s_len):
    e = lg_p.shape[0]
    np_seq = lg_p.shape[1] // p_len
    ns_seq = lg_s.shape[1] // s_len
    cap_p = CAPACITY_FACTOR * p_len // N_EXPERTS
    cap_s = CAPACITY_FACTOR * s_len // N_EXPERTS
    wp = max(cap_p, LANES)
    ws = max(cap_s, LANES)
    idx_p, g_p, idx_s, g_s = pl.pallas_call(
        functools.partial(_route_kernel, p_len=p_len, s_len=s_len),
        out_shape=[jax.ShapeDtypeStruct((np_seq * e, wp), I32),
                   jax.ShapeDtypeStruct((np_seq * e, wp), F32),
                   jax.ShapeDtypeStruct((ns_seq * e, ws), I32),
                   jax.ShapeDtypeStruct((ns_seq * e, ws), F32)],
        compiler_params=pltpu.CompilerParams(vmem_limit_bytes=40 * MIB),
        name="route",
    )(lg_p, lg_s)
    return (idx_p[:, :cap_p].reshape(np_seq, e * cap_p), g_p[:, :cap_p].reshape(np_seq, e * cap_p),
            idx_s[:, :cap_s].reshape(-1), g_s[:, :cap_s].reshape(-1))


def _gather_onehot_kernel(idx_ref, h_ref, o_ref):
    t_len = h_ref.shape[0]
    n_exp, cap, _ = o_ref.shape
    idx = idx_ref[0]
    hit = jnp.where(idx == lax.broadcasted_iota(I32, (t_len, idx.shape[1]), 0), 1.0, 0.0).astype(BF16)
    xs = lax.dot_general(hit, h_ref[...], (((0,), (0,)), ((), ())), preferred_element_type=F32).astype(BF16)
    for e in range(n_exp):
        o_ref[e] = xs[e * cap:(e + 1) * cap]


def _gather_onehot(idx_row, h, *, seq_len):
    n, d = h.shape
    n_seq, _, slots = idx_row.shape
    cap = slots // N_EXPERTS
    return pl.pallas_call(
        _gather_onehot_kernel,
        grid=(n_seq,),
        in_specs=[pl.BlockSpec((1, 1, slots), lambda s: (s, 0, 0)),
                  pl.BlockSpec((seq_len, d), lambda s: (s, 0))],
        out_specs=pl.BlockSpec((N_EXPERTS, cap, d), lambda s: (0, s, 0)),
        out_shape=jax.ShapeDtypeStruct((N_EXPERTS, n_seq * cap, d), BF16),
        compiler_params=_params(1, 32),
        name="gather_onehot",
    )(idx_row, h)


def _gather_rows_kernel(idx_ref, h_ref, o_ref, tiles, *, cap, n_exp):
    tpr = _tiles_per_row(o_ref.shape[2])
    for el in range(ROW_LOOP_EXPERTS):
        base = (pl.program_id(0) * n_exp + pl.program_id(1) * ROW_LOOP_EXPERTS + el) * cap

        def body(c, carry, base=base):
            src = pl.multiple_of(idx_ref[base + c], tpr)
            tiles[pl.ds(pl.multiple_of(c * tpr, tpr), tpr), :] = h_ref[pl.ds(src, tpr), :]
            return carry

        lax.fori_loop(0, cap, body, 0, unroll=16)
        o_ref[el] = tiles[...].reshape(cap, tpr, LANES).reshape(cap, tpr * LANES).astype(BF16)


def _gather_rows(idx, h, d, *, seq_len):
    tpr = _tiles_per_row(d)
    n_seq = h.shape[0] // (seq_len * tpr)
    cap = CAPACITY_FACTOR * seq_len // N_EXPERTS
    return pl.pallas_call(
        functools.partial(_gather_rows_kernel, cap=cap, n_exp=N_EXPERTS),
        grid_spec=pltpu.PrefetchScalarGridSpec(
            num_scalar_prefetch=1,
            grid=(n_seq, N_EXPERTS // ROW_LOOP_EXPERTS),
            in_specs=[pl.BlockSpec((seq_len * tpr, LANES), lambda s, e, idx: (s, 0))],
            out_specs=pl.BlockSpec((ROW_LOOP_EXPERTS, cap, d), lambda s, e, idx: (e, s, 0)),
            scratch_shapes=[pltpu.VMEM((cap * tpr, LANES), F32)],
        ),
        out_shape=jax.ShapeDtypeStruct((N_EXPERTS, n_seq * cap, d), BF16),
        compiler_params=_params(2, 48),
        name="gather_rows",
    )(idx, h)


def _ffn_kernel(xp_ref, xs_ref, gp_ref, gs_ref, wg_ref, wu_ref, wd_ref, yp_ref, ys_ref):
    rc = 512
    f = pl.program_id(1)
    last = f == pl.num_programs(1) - 1

    def body(first):
        wg = wg_ref[0, 0].astype(BF16)
        wu = wu_ref[0, 0].astype(BF16)
        wd = wd_ref[0, 0].astype(BF16)
        for x_ref, g_ref, y_ref in ((xp_ref, gp_ref, yp_ref), (xs_ref, gs_ref, ys_ref)):
            for r in range(0, x_ref.shape[1], rc):
                x = x_ref[0, r:r + rc, :]
                hg = jnp.dot(x, wg, preferred_element_type=F32)
                hu = jnp.dot(x, wu, preferred_element_type=F32)
                hid = ((hg * jax.nn.sigmoid(hg)) * hu).astype(BF16)
                g_hi, g_lo = _split_bf16(g_ref[0, :, r:r + rc])
                ones = jnp.ones((g_hi.shape[0], LANES), BF16)
                tn = functools.partial(lax.dot_general, dimension_numbers=(((0,), (0,)), ((), ())),
                                       preferred_element_type=F32)
                weight = jnp.where(last, tn(g_hi, ones) + tn(g_lo, ones), 1.0)
                weight = jnp.concatenate([weight] * (y_ref.shape[2] // LANES), axis=1)
                y = jnp.dot(hid, wd, preferred_element_type=F32)
                if not first:
                    y = y_ref[0, r:r + rc, :] + y
                y_ref[0, r:r + rc, :] = y * weight

    pl.when(f == 0)(functools.partial(body, True))
    pl.when(f != 0)(functools.partial(body, False))


def _ffn(xs_p, xs_s, g_p, g_s, w_gate, w_up, w_down, layer):
    n_exp, rp, d = xs_p.shape
    rs = xs_s.shape[1]
    ff = w_gate.shape[3]
    fc = 1024
    return pl.pallas_call(
        _ffn_kernel,
        grid=(n_exp, ff // fc),
        in_specs=[
            pl.BlockSpec((1, rp, d), lambda e, f: (e, 0, 0)),
            pl.BlockSpec((1, rs, d), lambda e, f: (e, 0, 0)),
            pl.BlockSpec((1, SUBLANES, rp), lambda e, f: (e, 0, 0)),
            pl.BlockSpec((1, SUBLANES, rs), lambda e, f: (e, 0, 0)),
            pl.BlockSpec((1, 1, d, fc), lambda e, f: (layer, e, 0, f)),
            pl.BlockSpec((1, 1, d, fc), lambda e, f: (layer, e, 0, f)),
            pl.BlockSpec((1, 1, fc, d), lambda e, f: (layer, e, f, 0)),
        ],
        out_specs=[pl.BlockSpec((1, rp, d), lambda e, f: (e, 0, 0)),
                   pl.BlockSpec((1, rs, d), lambda e, f: (e, 0, 0))],
        out_shape=[jax.ShapeDtypeStruct((n_exp, rp, d), F32),
                   jax.ShapeDtypeStruct((n_exp, rs, d), F32)],
        compiler_params=_params(2, 58),
        name="expert_ffn",
    )(xs_p, xs_s, g_p, g_s, w_gate, w_up, w_down)


def _combine_onehot_kernel(*refs, mod_row, final):
    if final:
        idx_ref, ye_ref, x_ref, pmod_ref, gain_ref, o_ref = refs
    else:
        idx_ref, ye_ref, x_ref, pmod_ref, o_ref = refs
    t_len, d = o_ref.shape
    n_exp = ye_ref.shape[0]
    idx = idx_ref[0]
    hit = jnp.where(idx == lax.broadcasted_iota(I32, (t_len, idx.shape[1]), 0), 1.0, 0.0).astype(BF16)
    y_hi, y_lo = _split_bf16(jnp.concatenate([ye_ref[e] for e in range(n_exp)], axis=0))
    acc = jnp.dot(hit, y_hi, preferred_element_type=F32) + jnp.dot(hit, y_lo, preferred_element_type=F32)
    x = x_ref[...] + pmod_ref[pl.ds(mod_row(pl.program_id(0), t_len), 1), pl.ds(5 * d, d)] * acc
    if final:
        x = (x * lax.rsqrt(jnp.mean(x * x, axis=-1, keepdims=True) + EPS)) * gain_ref[...]
    o_ref[...] = x


def _combine_onehot(idx_row, ye, x, pmod, final_gain, *, seq_len, mod_row):
    n_exp, rows, d = ye.shape
    n_seq, _, slots = idx_row.shape
    cap = slots // n_exp
    in_specs = [pl.BlockSpec((1, 1, slots), lambda s: (s, 0, 0)),
                pl.BlockSpec((n_exp, cap, d), lambda s: (0, s, 0)),
                pl.BlockSpec((seq_len, d), lambda s: (s, 0)),
                pl.BlockSpec(pmod.shape, lambda s: (0, 0))]
    args = [idx_row, ye, x, pmod]
    if final_gain is not None:
        in_specs.append(pl.BlockSpec((1, d), lambda s: (0, 0)))
        args.append(final_gain)
    return pl.pallas_call(
        functools.partial(_combine_onehot_kernel, mod_row=mod_row, final=final_gain is not None),
        grid=(n_seq,),
        in_specs=in_specs,
        out_specs=pl.BlockSpec((seq_len, d), lambda s: (s, 0)),
        out_shape=jax.ShapeDtypeStruct((n_seq * seq_len, d), F32),
        compiler_params=_params(1, 32),
        name="combine_onehot",
    )(*args)


COMBINE_GROUP = 16
ROW_LOOP_EXPERTS = 2


def _combine_rows_kernel(idx_ref, ye_ref, o_ref, tiles, *, cap, n_exp):
    e = pl.program_id(1)
    d = ye_ref.shape[2]
    tpr = _tiles_per_row(d)
    ch = 2048

    @pl.when(e == 0)
    def _():
        def zero(c, carry):
            o_ref[pl.ds(pl.multiple_of(c * ch, ch), ch), :] = jnp.zeros((ch, LANES), F32)
            return carry
        lax.fori_loop(0, o_ref.shape[0] // ch, zero, 0)

    for el in range(ROW_LOOP_EXPERTS):
        tiles[...] = ye_ref[el].reshape(cap, tpr, LANES).reshape(cap * tpr, LANES)

        base = (pl.program_id(0) * n_exp + e * ROW_LOOP_EXPERTS + el) * cap

        def body(i, carry, base=base):
            c0 = i * COMBINE_GROUP
            dst = [pl.ds(pl.multiple_of(idx_ref[base + c0 + k], tpr), tpr) for k in range(COMBINE_GROUP)]
            vals = [o_ref[dst[k], :] + tiles[pl.ds(pl.multiple_of((c0 + k) * tpr, tpr), tpr), :]
                    for k in range(COMBINE_GROUP)]
            for k in range(COMBINE_GROUP):
                o_ref[dst[k], :] = vals[k]
            return carry

        lax.fori_loop(0, cap // COMBINE_GROUP, body, 0)


def _combine_rows(idx, ye, *, seq_len):
    n_exp, rows, d = ye.shape
    tpr = _tiles_per_row(d)
    cap = CAPACITY_FACTOR * seq_len // N_EXPERTS
    n_seq = rows // cap
    return pl.pallas_call(
        functools.partial(_combine_rows_kernel, cap=cap, n_exp=n_exp),
        grid_spec=pltpu.PrefetchScalarGridSpec(
            num_scalar_prefetch=1,
            grid=(n_seq, n_exp // ROW_LOOP_EXPERTS),
            in_specs=[pl.BlockSpec((ROW_LOOP_EXPERTS, cap, d), lambda s, e, i: (e, s, 0))],
            out_specs=pl.BlockSpec((seq_len * tpr, LANES), lambda s, e, i: (s, 0)),
            scratch_shapes=[pltpu.VMEM((cap * tpr, LANES), F32)],
        ),
        out_shape=jax.ShapeDtypeStruct((n_seq * seq_len * tpr, LANES), F32),
        compiler_params=_params(2, 48),
        name="combine_rows",
    )(idx, ye)


def _head_norm(x, gain):
    return x * lax.rsqrt(jnp.mean(x * x, axis=-1, keepdims=True) + EPS) * gain


def _qkv_kernel(*refs, rope, has_prev, mod_row, n_heads, n_kv):
    refs = list(refs)
    x_ref = refs.pop(0)
    prev = None
    if has_prev:
        acc_ref, pmod_ref = refs.pop(0), refs.pop(0)
    mod_ref, gain_ref, w_ref, qg_ref, kg_ref = refs[:5]
    refs = refs[5:]
    if rope:
        cos_ref, sin_ref = refs.pop(0), refs.pop(0)
    if has_prev:
        prev = (acc_ref, pmod_ref, refs.pop(0))
    q_ref, k_ref, v_ref = refs[:3]
    d = x_ref.shape[1]
    hd = HEAD_DIM
    m = mod_row(pl.program_id(0), x_ref.shape[0])
    x = _residual_stream(x_ref, prev, m)
    shift = mod_ref[pl.ds(m, 1), pl.ds(0, d)]
    scale = mod_ref[pl.ds(m, 1), pl.ds(d, d)]
    h = _modulated_norm(x, gain_ref[...], shift, scale).astype(BF16)
    qg = qg_ref[...]
    kg = kg_ref[...]
    q_scale = math.log2(math.e) * hd ** -0.5
    half = x_ref.shape[0] // 2
    for r0 in (0, half):
        rows = slice(r0, r0 + half)
        qkv = jnp.dot(h[rows], w_ref[...], preferred_element_type=F32)
        if rope:
            cos = cos_ref[rows, :]
            sin = sin_ref[rows, :]

            def rot(xh, cos=cos, sin=sin):
                return xh * cos + pltpu.roll(xh, hd // 2, 1) * sin
        else:
            rot = lambda xh: xh

        for i in range(n_heads):
            qh = rot(_head_norm(qkv[:, i * hd:(i + 1) * hd], qg)) * q_scale
            q_ref[rows, i * hd:(i + 1) * hd] = qh.astype(BF16)
        for i in range(n_kv):
            c0 = (n_heads + i) * hd
            kh = _head_norm(qkv[:, c0:c0 + hd], kg)
            if not rope:
                refs[3][rows, i * hd:(i + 1) * hd] = kh
            k_ref[rows, i * hd:(i + 1) * hd] = rot(kh).astype(BF16)
        v = qkv[:, (n_heads + n_kv) * hd:]
        v_ref[rows, :] = v.astype(BF16)
        if not rope:
            refs[4][rows, :] = v


def _qkv(x, prev, mod, gain, w_qkv, q_gain, k_gain, tables, *, seq_len, mod_row, n_heads, n_kv):
    n, d = x.shape
    hd = HEAD_DIM
    row = lambda i: (i, 0)
    fixed = lambda i: (0, 0)
    rope = tables is not None
    rb = QKV_ROW_BLOCK
    in_specs = [pl.BlockSpec((rb, d), row)]
    args = [x]
    if prev is not None:
        in_specs += [_rows_spec(prev[0], d, rb), pl.BlockSpec(prev[1].shape, fixed)]
        args += list(prev)
    in_specs += [
        pl.BlockSpec(mod.shape, fixed),
        pl.BlockSpec((1, d), fixed),
        pl.BlockSpec(w_qkv.shape, fixed),
        pl.BlockSpec((1, hd), fixed),
        pl.BlockSpec((1, hd), fixed),
    ]
    args += [mod, gain, w_qkv, q_gain, k_gain]
    if rope:
        per_seq = seq_len // rb
        in_specs += [pl.BlockSpec((rb, hd), lambda i: (i % per_seq, 0))] * 2
        args += list(tables)
    out_specs, out_shape = [], []
    if prev is not None:
        out_specs.append(pl.BlockSpec((rb, d), row))
        out_shape.append(jax.ShapeDtypeStruct((n, d), F32))
    out_specs += [pl.BlockSpec((rb, n_heads * hd), row),
                  pl.BlockSpec((rb, n_kv * hd), row),
                  pl.BlockSpec((rb, n_kv * hd), row)]
    out_shape += [jax.ShapeDtypeStruct((n, n_heads * hd), BF16),
                  jax.ShapeDtypeStruct((n, n_kv * hd), BF16),
                  jax.ShapeDtypeStruct((n, n_kv * hd), BF16)]
    if not rope:
        out_specs += [pl.BlockSpec((rb, n_kv * hd), row)] * 2
        out_shape += [jax.ShapeDtypeStruct((n, n_kv * hd), F32)] * 2
    outs = pl.pallas_call(
        functools.partial(_qkv_kernel, rope=rope, has_prev=prev is not None, mod_row=mod_row,
                          n_heads=n_heads, n_kv=n_kv),
        grid=(n // rb,),
        in_specs=in_specs,
        out_specs=out_specs,
        out_shape=out_shape,
        compiler_params=_params(1, 48),
        name="qkv_rope" if rope else "qkv",
    )(*args)
    return list(outs) if prev is not None else [x] + list(outs)


def _attn_kernel(*refs, group, chunk, has_cache):
    if has_cache:
        q_ref, k_ref, v_ref, kc_ref, vc_ref, o_ref = refs
    else:
        q_ref, k_ref, v_ref, o_ref = refs
    hd = HEAD_DIM
    qb = q_ref.shape[0]
    rows = group * qb
    q = jnp.concatenate([q_ref[:, g * hd:(g + 1) * hd] for g in range(group)], axis=0)
    sources = [(k_ref, v_ref, s0, chunk) for s0 in range(0, k_ref.shape[0], chunk)]
    if has_cache:
        sources.append((kc_ref.at[0], vc_ref.at[0], 0, kc_ref.shape[1]))
    m = jnp.full((rows, 1), -jnp.inf, F32)
    acc = jnp.zeros((rows, 2 * hd), F32)
    for kr, vr, s0, size in sources:
        s = lax.dot_general(q, kr[s0:s0 + size, :], (((1,), (1,)), ((), ())),
                            preferred_element_type=F32)
        m_new = jnp.maximum(m, jnp.max(s, axis=-1, keepdims=True))
        p = jnp.exp2(s - m_new).astype(BF16)
        v_ones = jnp.concatenate([vr[s0:s0 + size, :], jnp.ones((size, hd), BF16)], axis=1)
        acc = jnp.exp2(m - m_new) * acc + jnp.dot(p, v_ones, preferred_element_type=F32)
        m = m_new
    o = acc[:, :hd] / acc[:, hd:]
    o_ref[...] = jnp.concatenate([o[g * qb:(g + 1) * qb] for g in range(group)], axis=1).astype(BF16)


def _attention(q, k, v, cache, *, seq_len, qb, chunk, n_heads, n_kv):
    n = q.shape[0]
    hd = HEAD_DIM
    group = n_heads // n_kv
    nq = seq_len // qb
    in_specs = [
        pl.BlockSpec((qb, group * hd), lambda b, h, i: (b * nq + i, h)),
        pl.BlockSpec((seq_len, hd), lambda b, h, i: (b, h)),
        pl.BlockSpec((seq_len, hd), lambda b, h, i: (b, h)),
    ]
    args = [q, k, v]
    if cache is not None:
        past = cache[0].shape[1]
        in_specs += [pl.BlockSpec((1, past, hd), lambda b, h, i: (b, 0, h))] * 2
        args += list(cache)
    return pl.pallas_call(
        functools.partial(_attn_kernel, group=group, chunk=chunk, has_cache=cache is not None),
        grid=(n // seq_len, n_kv, nq),
        in_specs=in_specs,
        out_specs=pl.BlockSpec((qb, group * hd), lambda b, h, i: (b * nq + i, h)),
        out_shape=jax.ShapeDtypeStruct((n, n_heads * hd), BF16),
        compiler_params=_params(3, 48),
        name=f"attention_{seq_len}",
    )(*args)


def _final_norm_kernel(x_ref, acc_ref, pmod_ref, gain_ref, o_ref, *, mod_row):
    n_rows, d = x_ref.shape
    m = mod_row(pl.program_id(0), x_ref.shape[0])
    x = x_ref[...] + pmod_ref[pl.ds(m, 1), pl.ds(5 * d, d)] * _load_rows(acc_ref, n_rows, d)
    o_ref[...] = (x * lax.rsqrt(jnp.mean(x * x, axis=-1, keepdims=True) + EPS)) * gain_ref[...]


def _final_norm(x, acc, pmod, gain, mod_row):
    n, d = x.shape
    row = lambda i: (i, 0)
    fixed = lambda i: (0, 0)
    return pl.pallas_call(
        functools.partial(_final_norm_kernel, mod_row=mod_row),
        grid=(n // ROW_BLOCK,),
        in_specs=[pl.BlockSpec((ROW_BLOCK, d), row), _rows_spec(acc, d, ROW_BLOCK),
                  pl.BlockSpec(pmod.shape, fixed), pl.BlockSpec((1, d), fixed)],
        out_specs=pl.BlockSpec((ROW_BLOCK, d), row),
        out_shape=jax.ShapeDtypeStruct((n, d), F32),
        compiler_params=_params(1, 32),
        name="final_norm",
    )(x, acc, pmod, gain)


def _rope_head_order(x):
    q = HEAD_DIM // 4
    row1, row2, col1, col2 = (x[..., i * q:(i + 1) * q] for i in range(4))
    return jnp.concatenate([row1, col1, row2, col2], axis=-1)


def _rope_tables(seq_len):
    axis = HEAD_DIM // 2
    t = np.arange(seq_len)
    inv = ROPE_BASE ** (-np.arange(axis // 2, dtype=np.float64) * 2.0 / axis)
    ang = np.concatenate([(t // GRID_W)[:, None] * inv, (t % GRID_W)[:, None] * inv], axis=1)
    c, s = np.cos(ang), np.sin(ang)
    return (jnp.asarray(np.concatenate([c, c], axis=1), F32), jnp.asarray(np.concatenate([-s, s], axis=1), F32))


def kernel(x_prompt, x_sample, state_lru, cache_k, cache_v, c, c_ctx, w_mod, b_mod, norm_gain, final_gain,
           w_lru_in, lru_conv_w, lru_conv_b, lru_wa, lru_ba, lru_wx, lru_bx, lru_lambda, w_lru_out,
           w_qkv, q_norm, k_norm, w_attn_out, w_router, w_exp_gate, w_exp_up, w_exp_down):
    batch, seq, d = x_prompt.shape
    dec_batch, dec_seq, _ = x_sample.shape
    depth = w_mod.shape[0]
    n_kv = cache_k.shape[3]
    n_heads = w_attn_out.shape[1] // HEAD_DIM
    heads = dict(n_heads=n_heads, n_kv=n_kv)

    cond = jnp.zeros((SUBLANES, d), F32).at[0].set(c_ctx).at[1:1 + dec_batch].set(c)
    mod_all = _mod_vectors(cond, w_mod, b_mod)

    xs = [x_prompt.reshape(batch * seq, d), x_sample.reshape(dec_batch * dec_seq, d)]
    lens = [seq, dec_seq]
    mod_rows = [lambda i, rows: 0, lambda i, rows: 1 + (i * rows) // dec_seq]
    prevs = [None, None]

    fg = final_gain.reshape(1, d)
    wr_hi, wr_lo = _split_bf16(jnp.swapaxes(w_router, 1, 2))
    new_lru, new_k, new_v = [], [], []
    for l in range(depth):
        mod = mod_all[l]
        gain_a = norm_gain[l, 0].reshape(1, d)
        gain_c = norm_gain[l, 1].reshape(1, d)
        mixed = []
        if l % 2 == 0:
            li = l // 2
            w_in = w_lru_in[li].astype(BF16)
            wg = (0.5 * jnp.concatenate([lru_wa[li, 0], lru_wx[li, 0], lru_wa[li, 1], lru_wx[li, 1]],
                                        axis=2)).astype(BF16)
            bias4 = 0.5 * jnp.stack([lru_ba[li, 0], lru_bx[li, 0], lru_ba[li, 1], lru_bx[li, 1]])
            h0s = [jnp.zeros((batch, 2, w_in.shape[1] // 2), F32), state_lru[:, li]]
            for gi in range(2):
                xs[gi], gate, xr = _lru_in(xs[gi], prevs[gi], mod, gain_a, w_in, mod_rows[gi])
                y, fin = _lru_core(xr, gate, lru_conv_w[li], lru_conv_b[li].reshape(1, -1), wg, bias4,
                                   lru_lambda[li], h0s[gi], seq_len=lens[gi])
                mixed.append(y)
                if gi == 0:
                    new_lru.append(fin.astype(x_prompt.dtype))
            w_out = w_lru_out[li].astype(BF16)
        else:
            ai = l // 2
            w = w_qkv[ai].astype(BF16)
            qg = q_norm[ai].reshape(1, -1)
            kg = k_norm[ai].reshape(1, -1)
            xs[0], q_p, k_p, v_p, kf, vf = _qkv(xs[0], prevs[0], mod, gain_a, w, qg, kg, None,
                                                seq_len=seq, mod_row=mod_rows[0], **heads)
            n_qk = (n_heads + n_kv) * HEAD_DIM
            w_qk = _rope_head_order(w_qkv[ai][:, :n_qk].reshape(d, n_heads + n_kv, HEAD_DIM)).reshape(d, n_qk)
            w_r = jnp.concatenate([w_qk, w_qkv[ai][:, n_qk:]], axis=1).astype(BF16)
            xs[1], q_s, k_s, v_s = _qkv(xs[1], prevs[1], mod, gain_a, w_r, _rope_head_order(qg),
                                        _rope_head_order(kg), _rope_tables(dec_seq),
                                        seq_len=dec_seq, mod_row=mod_rows[1], **heads)
            new_k.append(kf.reshape(batch, seq, n_kv, HEAD_DIM))
            new_v.append(vf.reshape(batch, seq, n_kv, HEAD_DIM))
            past = cache_k.shape[2]
            kc = _rope_head_order(cache_k[:, ai]).reshape(dec_batch, past, n_kv * HEAD_DIM).astype(BF16)
            vc = cache_v[:, ai].reshape(dec_batch, past, n_kv * HEAD_DIM).astype(BF16)
            mixed.append(_attention(q_p, k_p, v_p, None, seq_len=seq, qb=seq, chunk=seq, **heads))
            mixed.append(_attention(q_s, k_s, v_s, (kc, vc), seq_len=dec_seq, qb=512, chunk=256, **heads))
            w_out = w_attn_out[ai].astype(BF16)

        x1_p, h2_p, lg_p = _mix_out(mixed[0], xs[0], mod, gain_c, w_out, wr_hi, wr_lo, l, mod_rows[0], False)
        x1_s, h2_s, lg_s = _mix_out(mixed[1], xs[1], mod, gain_c, w_out, wr_hi, wr_lo, l, mod_rows[1], True)
        idx_p, g_p, idx_s, g_s = _route(lg_p, lg_s, seq, dec_seq)
        xe_p = _gather_onehot(idx_p[:, None, :], h2_p, seq_len=seq)
        off_s = idx_s * _tiles_per_row(d)
        xe_s = _gather_rows(off_s, h2_s, d, seq_len=dec_seq)
        e_n = N_EXPERTS
        zero_rows = ((0, 0), (0, SUBLANES - 1), (0, 0))
        gc_p = jnp.pad(g_p.reshape(batch, e_n, -1).transpose(1, 0, 2).reshape(e_n, 1, -1), zero_rows)
        gc_s = jnp.pad(g_s.reshape(dec_batch, e_n, -1).transpose(1, 0, 2).reshape(e_n, 1, -1), zero_rows)
        ye_p, ye_s = _ffn(xe_p, xe_s, gc_p, gc_s, w_exp_gate, w_exp_up, w_exp_down, l)
        last = l == depth - 1
        x_p = _combine_onehot(idx_p[:, None, :], ye_p, x1_p, mod, fg if last else None, seq_len=seq,
                              mod_row=mod_rows[0])
        acc_s = _combine_rows(off_s, ye_s, seq_len=dec_seq)
        xs = [x_p, x1_s]
        prevs = [None, (acc_s, mod)]

    y_prompt = xs[0].reshape(batch, seq, d)
    y_sample = _final_norm(xs[1], *prevs[1], fg, mod_rows[1]).reshape(dec_batch, dec_seq, d)
    return (y_prompt, y_sample, jnp.stack(new_lru, axis=1), jnp.stack(new_k, axis=1), jnp.stack(new_v, axis=1))
```

```python
import functools
import math

import jax
import jax.numpy as jnp
import numpy as np
from jax import lax
from jax.experimental import pallas as pl
from jax.experimental.pallas import tpu as pltpu

F32 = jnp.float32
BF16 = jnp.bfloat16
I32 = jnp.int32

LANES = 128
SUBLANES = 8
MIB = 1024 * 1024
F32_TINY = float(np.finfo(np.float32).tiny)

RG_C = 8.0
CONV_W = 4
CONV_PAD_L = 2
ROPE_BASE = 10000.0
GRID_W = 64
EPS = 1e-6
N_EXPERTS = 16
CAPACITY_FACTOR = 2
HEAD_DIM = 128
LRU_BLOCKS = 8
LRU_MAX_STEP_BLOCKS = 4
LRU_CORE_VMEM_MIB = 44

ROW_BLOCK = 1024
QKV_ROW_BLOCK = 512


def _params(n_axes, vmem_mib):
    return pltpu.CompilerParams(
        dimension_semantics=("arbitrary",) * n_axes, vmem_limit_bytes=vmem_mib * MIB)


def _split_bf16(x):
    hi = x.astype(BF16)
    lo = (x - hi.astype(F32)).astype(BF16)
    return hi, lo


def _modulated_norm(x, gain, shift, scale):
    y = x * lax.rsqrt(jnp.mean(x * x, axis=-1, keepdims=True) + EPS)
    return (y * gain) * (1.0 + scale) + shift


def _tiles_per_row(d):
    return d // LANES


def _load_rows(ref, n_rows, d):
    if ref.shape == (n_rows, d):
        return ref[...]
    tpr = _tiles_per_row(d)
    return jnp.concatenate([ref[pl.ds(s, n_rows, stride=tpr), :] for s in range(tpr)], axis=1)


def _store_rows(ref, x, r0):
    n_rows, d = x.shape
    if ref.shape[1] == d:
        ref[r0:r0 + n_rows, :] = x.astype(ref.dtype)
        return
    tpr = _tiles_per_row(d)
    for s in range(tpr):
        ref[pl.ds(r0 * tpr + s, n_rows, stride=tpr), :] = x[:, s * LANES:(s + 1) * LANES]


def _rows_spec(arr, d, rows):
    if arr.shape[1] == d:
        return pl.BlockSpec((rows, d), lambda i: (i, 0))
    return pl.BlockSpec((rows * _tiles_per_row(d), LANES), lambda i: (i, 0))


def _residual_stream(x_ref, prev, m):
    if prev is None:
        return x_ref[...]
    acc_ref, pmod_ref, xo_ref = prev
    n_rows, d = x_ref.shape
    x = x_ref[...] + pmod_ref[pl.ds(m, 1), pl.ds(5 * d, d)] * _load_rows(acc_ref, n_rows, d)
    xo_ref[...] = x
    return x


def _mod_kernel(cond_ref, w_ref, b_ref, o_ref):
    c = cond_ref[...]
    a_hi, a_lo = _split_bf16(c * jax.nn.sigmoid(c))
    w_hi, w_lo = _split_bf16(w_ref[0])
    dot = functools.partial(jnp.dot, preferred_element_type=F32)
    o_ref[0] = dot(a_hi, w_hi) + (dot(a_lo, w_hi) + dot(a_hi, w_lo)) + b_ref[0]


def _mod_vectors(cond8, w_mod, b_mod):
    depth, d, d6 = w_mod.shape
    nb = 1536
    return pl.pallas_call(
        _mod_kernel,
        grid=(depth, d6 // nb),
        in_specs=[
            pl.BlockSpec((SUBLANES, d), lambda l, j: (0, 0)),
            pl.BlockSpec((1, d, nb), lambda l, j: (l, 0, j)),
            pl.BlockSpec((1, 1, nb), lambda l, j: (l, 0, j)),
        ],
        out_specs=pl.BlockSpec((1, SUBLANES, nb), lambda l, j: (l, 0, j)),
        out_shape=jax.ShapeDtypeStruct((depth, SUBLANES, d6), F32),
        compiler_params=_params(2, 40),
        name="adaln_mod",
    )(cond8, w_mod, b_mod.reshape(depth, 1, d6))


def _lru_in_kernel(*refs, mod_row, has_prev):
    if has_prev:
        x_ref, acc_ref, pmod_ref, mod_ref, gain_ref, w_ref, xo_ref, gate_ref, xr_ref = refs
        prev = (acc_ref, pmod_ref, xo_ref)
    else:
        x_ref, mod_ref, gain_ref, w_ref, gate_ref, xr_ref = refs
        prev = None
    d = x_ref.shape[1]
    m = mod_row(pl.program_id(0), x_ref.shape[0])
    x = _residual_stream(x_ref, prev, m)
    shift = mod_ref[pl.ds(m, 1), pl.ds(0, d)]
    scale = mod_ref[pl.ds(m, 1), pl.ds(d, d)]
    h = _modulated_norm(x, gain_ref[...], shift, scale).astype(BF16)
    u = jnp.dot(h, w_ref[...], preferred_element_type=F32)
    w = gate_ref.shape[1]
    gate_ref[...] = u[:, :w]
    xr_ref[...] = u[:, w:]


def _lru_in(x, prev, mod, gain, w_in, mod_row):
    n, d = x.shape
    w2 = w_in.shape[1]
    w = w2 // 2
    row = lambda i: (i, 0)
    fixed = lambda i: (0, 0)
    in_specs = [pl.BlockSpec((ROW_BLOCK, d), row)]
    args = [x]
    out_specs = [pl.BlockSpec((ROW_BLOCK, w), row), pl.BlockSpec((ROW_BLOCK, w), row)]
    out_shape = [jax.ShapeDtypeStruct((n, w), F32), jax.ShapeDtypeStruct((n, w), F32)]
    if prev is not None:
        in_specs += [_rows_spec(prev[0], d, ROW_BLOCK), pl.BlockSpec(prev[1].shape, fixed)]
        args += list(prev)
        out_specs = [pl.BlockSpec((ROW_BLOCK, d), row)] + out_specs
        out_shape = [jax.ShapeDtypeStruct((n, d), F32)] + out_shape
    in_specs += [pl.BlockSpec(mod.shape, fixed), pl.BlockSpec((1, d), fixed), pl.BlockSpec((d, w2), fixed)]
    args += [mod, gain, w_in]
    outs = pl.pallas_call(
        functools.partial(_lru_in_kernel, mod_row=mod_row, has_prev=prev is not None),
        grid=(n // ROW_BLOCK,),
        in_specs=in_specs,
        out_specs=out_specs,
        out_shape=out_shape,
        compiler_params=_params(1, 48),
        name="lru_in",
    )(*args)
    return outs if prev is not None else [x] + list(outs)


def _lru_core_kernel(xr_ref, gate_ref, cw_ref, cb_ref, wg_ref, bias_ref, lam_ref, h0_ref,
                     y_ref, fin_ref, pad, a_f, b_f, a_b, b_b):
    t_len, lb = xr_ref.shape
    ch = 256
    halo = SUBLANES
    seg = t_len // SUBLANES
    piece = min(ch, seg)
    seg_shift = seg.bit_length() - 1

    def seg_rows(t0):
        r = lax.shift_right_logical(t0, seg_shift)
        return pl.ds((t0 - r * seg) * SUBLANES + r, piece, stride=SUBLANES)

    zero_rows = jnp.zeros((halo, lb), F32)
    pad[0:halo, :] = zero_rows
    pad[t_len + halo:t_len + 2 * halo, :] = zero_rows

    def copy_in(c, carry):
        r = pl.multiple_of(c * ch, ch)
        pad[pl.ds(r + halo, ch), :] = xr_ref[pl.ds(r, ch), :]
        return carry

    lax.fori_loop(0, t_len // ch, copy_in, 0)

    neg_lam = -lam_ref[...]
    softplus = jnp.maximum(neg_lam, 0.0) + jnp.log1p(jnp.exp(-jnp.abs(neg_lam)))
    c_nla = (0.5 * RG_C) * softplus
    c_exp2 = (-0.5 * RG_C * math.log2(math.e)) * softplus
    cw = cw_ref[...]
    cbias = cb_ref[...]
    half_bias = bias_ref[...]
    half_wg = wg_ref[...]
    bw = half_wg.shape[1]
    n_slab = lb // LANES

    def gates(c, carry):
        r = pl.multiple_of(c * ch, ch)
        blk = pad[pl.ds(r, ch + 2 * halo), :]
        xc = cbias
        for k in range(CONV_W):
            o = halo - CONV_PAD_L + k
            xc = xc + blk[o:o + ch] * cw[k:k + 1]
        xc_bf = xc.astype(BF16)
        g = [jnp.dot(xc_bf[:, j * bw:(j + 1) * bw], half_wg[j], preferred_element_type=F32)
             for j in range(lb // bw)]
        pre = lambda k: jnp.concatenate([gj[:, k * bw:(k + 1) * bw] for gj in g], axis=1) + half_bias[k:k + 1]
        half_xc = 0.5 * xc
        for d, (a_ref, b_ref) in enumerate(((a_f, b_f), (a_b, b_b))):
            u = jnp.tanh(pre(2 * d)) + 1.0
            t_i = jnp.tanh(pre(2 * d + 1))
            a = jnp.exp2(u * c_exp2[d:d + 1])
            q = jnp.tanh(u * c_nla[d:d + 1]) * (1.0 + a * a)
            b = (q * lax.rsqrt(jnp.maximum(q, F32_TINY))) * (t_i * half_xc + half_xc)
            for p in range(ch // piece):
                rows = seg_rows(r + p * piece)
                for j in range(n_slab):
                    a_ref[j, rows, :] = a[p * piece:(p + 1) * piece, j * LANES:(j + 1) * LANES]
                    b_ref[j, rows, :] = b[p * piece:(p + 1) * piece, j * LANES:(j + 1) * LANES]
        return carry

    lax.fori_loop(0, t_len // ch, gates, 0)

    def scan(i, carry):
        rows = (pl.ds(pl.multiple_of(i * SUBLANES, SUBLANES), SUBLANES),
                pl.ds(pl.multiple_of((seg - 1 - i) * SUBLANES, SUBLANES), SUBLANES))
        out = []
        for k, (h, prod) in enumerate(carry):
            d, j = divmod(k, n_slab)
            a_ref, b_ref = ((a_f, b_f), (a_b, b_b))[d]
            a = a_ref[j, rows[d], :]
            h = a * h + b_ref[j, rows[d], :]
            prod = a * prod
            b_ref[j, rows[d], :] = h
            a_ref[j, rows[d], :] = prod
            out.append((h, prod))
        return tuple(out)

    start = (jnp.zeros((SUBLANES, LANES), F32), jnp.ones((SUBLANES, LANES), F32))
    ends = lax.fori_loop(0, seg, scan, (start,) * (2 * n_slab), unroll=8)

    enter = []
    for k, (h, prod) in enumerate(ends):
        d, j = divmod(k, n_slab)
        states = [h0_ref[0, d:d + 1, j * LANES:(j + 1) * LANES]]
        for r in (range(SUBLANES) if d == 0 else reversed(range(SUBLANES))):
            states.append(h[r:r + 1] + prod[r:r + 1] * states[-1])
        fin_ref[0, d:d + 1, j * LANES:(j + 1) * LANES] = states[SUBLANES]
        order = states[:SUBLANES] if d == 0 else states[:SUBLANES][::-1]
        enter.append(jnp.concatenate(order, axis=0))

    def fix(i, carry):
        rows = pl.ds(pl.multiple_of(i * SUBLANES, SUBLANES), SUBLANES)
        for j in range(n_slab):
            b_f[j, rows, :] = ((b_f[j, rows, :] + a_f[j, rows, :] * enter[j])
                               + (b_b[j, rows, :] + a_b[j, rows, :] * enter[n_slab + j]))
        return carry

    lax.fori_loop(0, seg, fix, 0, unroll=8)

    def emit(c, carry):
        r = pl.multiple_of(c * ch, ch)
        gt = gate_ref[pl.ds(r, ch), :]
        cdf = 0.5 * (1.0 + jnp.tanh(math.sqrt(2.0 / math.pi) * (gt + 0.044715 * (gt * gt * gt))))
        y = gt * cdf
        for p in range(ch // piece):
            rows = seg_rows(r + p * piece)
            total = jnp.concatenate([b_f[j, rows, :] for j in range(n_slab)], axis=1)
            y_ref[pl.ds(r + p * piece, piece), :] = (total * y[p * piece:(p + 1) * piece]).astype(BF16)
        return carry

    lax.fori_loop(0, t_len // ch, emit, 0)


def _lru_core(xr, gate, conv_w, conv_b, wg, bias4, lam, h0, *, seq_len):
    n, w = xr.shape
    n_seq = n // seq_len
    bw = w // LRU_BLOCKS
    block_bytes = seq_len * bw * (5 * 4 + 2 * 2 * 4 + 2 * 2)
    step_blocks = LRU_MAX_STEP_BLOCKS
    while step_blocks * block_bytes > LRU_CORE_VMEM_MIB * MIB:
        step_blocks //= 2
    lb = step_blocks * bw
    tok = lambda s, c: (s, c)
    chan = lambda s, c: (0, c)
    return pl.pallas_call(
        _lru_core_kernel,
        grid=(n_seq, LRU_BLOCKS // step_blocks),
        in_specs=[
            pl.BlockSpec((seq_len, lb), tok),
            pl.BlockSpec((seq_len, lb), tok),
            pl.BlockSpec((CONV_W, lb), chan),
            pl.BlockSpec((1, lb), chan),
            pl.BlockSpec((step_blocks, bw, 4 * bw), lambda s, c: (c, 0, 0)),
            pl.BlockSpec((4, lb), chan),
            pl.BlockSpec((2, lb), chan),
            pl.BlockSpec((1, 2, lb), lambda s, c: (s, 0, c)),
        ],
        out_specs=[pl.BlockSpec((seq_len, lb), tok),
                   pl.BlockSpec((1, 2, lb), lambda s, c: (s, 0, c))],
        out_shape=[jax.ShapeDtypeStruct((n, w), BF16), jax.ShapeDtypeStruct((n_seq, 2, w), F32)],
        scratch_shapes=[pltpu.VMEM((seq_len + 2 * SUBLANES, lb), F32)]
        + [pltpu.VMEM((lb // LANES, seq_len, LANES), F32)] * 4,
        compiler_params=_params(2, LRU_CORE_VMEM_MIB + 4),
        name=f"lru_core_{seq_len}",
    )(xr, gate, conv_w, conv_b, wg, bias4, lam, h0)


def _mix_out_kernel(a_ref, x_ref, mod_ref, gain_ref, w_ref, wrh_ref, wrl_ref,
                    x1_ref, h2_ref, lg_ref, *, mod_row):
    d = x_ref.shape[1]
    m = mod_row(pl.program_id(0), x_ref.shape[0])
    g_mix = mod_ref[pl.ds(m, 1), pl.ds(2 * d, d)]
    shift = mod_ref[pl.ds(m, 1), pl.ds(3 * d, d)]
    scale = mod_ref[pl.ds(m, 1), pl.ds(4 * d, d)]
    op = jnp.dot(a_ref[...], w_ref[...], preferred_element_type=F32)
    x1 = x_ref[...] + g_mix * op
    x1_ref[...] = x1
    h2 = _modulated_norm(x1, gain_ref[...], shift, scale)
    h_hi, h_lo = _split_bf16(h2)
    _store_rows(h2_ref, h2, 0)
    nt = functools.partial(lax.dot_general, dimension_numbers=(((1,), (1,)), ((), ())),
                           preferred_element_type=F32)
    wr_hi = wrh_ref[0]
    e = wr_hi.shape[0]
    both = nt(jnp.concatenate([wr_hi, wrl_ref[0]], axis=0), h_hi)
    lg_ref[...] = both[:e] + (both[e:] + nt(wr_hi, h_lo))


def _mix_out(a, x, mod, gain, w_out, wr_hi, wr_lo, layer, mod_row, token_tiled):
    n, d = x.shape
    e = wr_hi.shape[1]
    row = lambda i: (i, 0)
    fixed = lambda i: (0, 0)
    router = pl.BlockSpec((1,) + wr_hi.shape[1:], lambda i: (layer, 0, 0))
    tpr = _tiles_per_row(d)
    if token_tiled:
        h2_spec = pl.BlockSpec((ROW_BLOCK * tpr, LANES), row)
        h2_shape = jax.ShapeDtypeStruct((n * tpr, LANES), F32)
    else:
        h2_spec = pl.BlockSpec((ROW_BLOCK, d), row)
        h2_shape = jax.ShapeDtypeStruct((n, d), BF16)
    return pl.pallas_call(
        functools.partial(_mix_out_kernel, mod_row=mod_row),
        grid=(n // ROW_BLOCK,),
        in_specs=[
            pl.BlockSpec((ROW_BLOCK, a.shape[1]), row),
            pl.BlockSpec((ROW_BLOCK, d), row),
            pl.BlockSpec(mod.shape, fixed),
            pl.BlockSpec((1, d), fixed),
            pl.BlockSpec(w_out.shape, fixed),
            router,
            router,
        ],
        out_specs=[pl.BlockSpec((ROW_BLOCK, d), row), h2_spec,
                   pl.BlockSpec((e, ROW_BLOCK), lambda i: (0, i))],
        out_shape=[jax.ShapeDtypeStruct((n, d), F32), h2_shape, jax.ShapeDtypeStruct((e, n), F32)],
        compiler_params=_params(1, 48),
        name="mix_out",
    )(a, x, mod, gain, w_out, wr_hi, wr_lo)


def _lane_sum(tiles):
    acc = tiles[0]
    for t in tiles[1:]:
        acc = acc + t
    return jnp.sum(acc, axis=1, keepdims=True)


def _exclusive_cumsum(flags, upper):
    out = []
    off = jnp.zeros((flags[0].shape[0], 1), F32)
    for f in flags:
        out.append(jnp.dot(f.astype(BF16), upper, preferred_element_type=F32) + off)
        off = off + jnp.sum(f, axis=1, keepdims=True)
    return out


def _route_group(lg_ref, n_seq, t_len, idx_ref, g_ref):
    cap = CAPACITY_FACTOR * t_len // N_EXPERTS
    affs = []
    for s in range(n_seq):
        lg = lg_ref[:, s * t_len:(s + 1) * t_len]
        ex = jnp.exp(lg - jnp.max(lg, axis=0, keepdims=True))
        affs.append(ex / jnp.sum(ex, axis=0, keepdims=True))
    aff = jnp.concatenate(affs, axis=0)
    n_rows = aff.shape[0]
    nt = t_len // LANES
    g = [aff[:, j * LANES:(j + 1) * LANES] for j in range(nt)]

    kth_bits = jnp.zeros((n_rows, 1), I32)
    for bit in range(30, -1, -1):
        cand = kth_bits | (1 << bit)
        cand_f = pltpu.bitcast(cand, F32)
        cnt = _lane_sum([jnp.where(t >= cand_f, 1, 0) for t in g])
        kth_bits = jnp.where(cnt >= cap, cand, kth_bits)
    kth = pltpu.bitcast(kth_bits, F32)

    lane = lax.broadcasted_iota(I32, (LANES, LANES), 0)
    upper = jnp.where(lane < lax.broadcasted_iota(I32, (LANES, LANES), 1), 1.0, 0.0).astype(BF16)
    gt = [t > kth for t in g]
    eq = [t == kth for t in g]
    need = (cap - _lane_sum([jnp.where(m, 1, 0) for m in gt])).astype(F32)
    eq_rank = _exclusive_cumsum([jnp.where(m, 1.0, 0.0) for m in eq], upper)
    sel = [jnp.logical_or(gt[j], jnp.logical_and(eq[j], eq_rank[j] < need)) for j in range(nt)]
    pos = _exclusive_cumsum([jnp.where(m, 1.0, 0.0) for m in sel], upper)

    lane_r = lax.broadcasted_iota(I32, (n_rows, LANES), 1)
    d = [jnp.where(sel[j], lane_r + j * LANES - pos[j].astype(I32), -1) for j in range(nt)]
    for k in range(t_len.bit_length() - 1):
        s = 1 << k
        if s < LANES:
            d_rot = [pltpu.roll(x, LANES - s, 1) for x in d]
            g_rot = [pltpu.roll(x, LANES - s, 1) for x in g]
            same = lane_r < LANES - s
            d_in = [jnp.where(same, d_rot[j], d_rot[(j + 1) % nt]) for j in range(nt)]
            g_in = [jnp.where(same, g_rot[j], g_rot[(j + 1) % nt]) for j in range(nt)]
        else:
            q = s // LANES
            d_in = [d[(j + q) % nt] for j in range(nt)]
            g_in = [g[(j + q) % nt] for j in range(nt)]
        new_d, new_g = [], []
        for j in range(nt):
            move = jnp.logical_and(d_in[j] >= 0, ((d_in[j] >> k) & 1) == 1)
            stay = jnp.logical_and(d[j] >= 0, ((d[j] >> k) & 1) == 0)
            new_d.append(jnp.where(move, d_in[j], jnp.where(stay, d[j], -1)))
            new_g.append(jnp.where(move, g_in[j], g[j]))
        d, g = new_d, new_g

    for j in range(idx_ref.shape[1] // LANES):
        idx_ref[:, j * LANES:(j + 1) * LANES] = lane_r + j * LANES + d[j]
        g_ref[:, j * LANES:(j + 1) * LANES] = g[j]


def _route_kernel(lgp_ref, lgs_ref, idx_p_ref, g_p_ref, idx_s_ref, g_s_ref, *, p_len, s_len):
    _route_group(lgp_ref, lgp_ref.shape[1] // p_len, p_len, idx_p_ref, g_p_ref)
    _route_group(lgs_ref, lgs_ref.shape[1] // s_len, s_len, idx_s_ref, g_s_ref)


def _route(lg_p, lg_s, p_len, s_len):
    e = lg_p.shape[0]
    np_seq = lg_p.shape[1] // p_len
    ns_seq = lg_s.shape[1] // s_len
    cap_p = CAPACITY_FACTOR * p_len // N_EXPERTS
    cap_s = CAPACITY_FACTOR * s_len // N_EXPERTS
    wp = max(cap_p, LANES)
    ws = max(cap_s, LANES)
    idx_p, g_p, idx_s, g_s = pl.pallas_call(
        functools.partial(_route_kernel, p_len=p_len, s_len=s_len),
        out_shape=[jax.ShapeDtypeStruct((np_seq * e, wp), I32),
                   jax.ShapeDtypeStruct((np_seq * e, wp), F32),
                   jax.ShapeDtypeStruct((ns_seq * e, ws), I32),
                   jax.ShapeDtypeStruct((ns_seq * e, ws), F32)],
        compiler_params=pltpu.CompilerParams(vmem_limit_bytes=40 * MIB),
        name="route",
    )(lg_p, lg_s)
    return (idx_p[:, :cap_p].reshape(np_seq, e * cap_p), g_p[:, :cap_p].reshape(np_seq, e * cap_p),
            idx_s[:, :cap_s].reshape(-1), g_s[:, :cap_s].reshape(-1))


def _gather_onehot_kernel(idx_ref, h_ref, o_ref):
    t_len = h_ref.shape[0]
    n_exp, cap, _ = o_ref.shape
    idx = idx_ref[0]
    hit = jnp.where(idx == lax.broadcasted_iota(I32, (t_len, idx.shape[1]), 0), 1.0, 0.0).astype(BF16)
    xs = lax.dot_general(hit, h_ref[...], (((0,), (0,)), ((), ())), preferred_element_type=F32).astype(BF16)
    for e in range(n_exp):
        o_ref[e] = xs[e * cap:(e + 1) * cap]


def _gather_onehot(idx_row, h, *, seq_len):
    n, d = h.shape
    n_seq, _, slots = idx_row.shape
    cap = slots // N_EXPERTS
    return pl.pallas_call(
        _gather_onehot_kernel,
        grid=(n_seq,),
        in_specs=[pl.BlockSpec((1, 1, slots), lambda s: (s, 0, 0)),
                  pl.BlockSpec((seq_len, d), lambda s: (s, 0))],
        out_specs=pl.BlockSpec((N_EXPERTS, cap, d), lambda s: (0, s, 0)),
        out_shape=jax.ShapeDtypeStruct((N_EXPERTS, n_seq * cap, d), BF16),
        compiler_params=_params(1, 32),
        name="gather_onehot",
    )(idx_row, h)


def _gather_rows_kernel(idx_ref, h_ref, o_ref, tiles, *, cap, n_exp):
    tpr = _tiles_per_row(o_ref.shape[2])
    for el in range(ROW_LOOP_EXPERTS):
        base = (pl.program_id(0) * n_exp + pl.program_id(1) * ROW_LOOP_EXPERTS + el) * cap

        def body(c, carry, base=base):
            src = pl.multiple_of(idx_ref[base + c], tpr)
            tiles[pl.ds(pl.multiple_of(c * tpr, tpr), tpr), :] = h_ref[pl.ds(src, tpr), :]
            return carry

        lax.fori_loop(0, cap, body, 0, unroll=16)
        o_ref[el] = tiles[...].reshape(cap, tpr, LANES).reshape(cap, tpr * LANES).astype(BF16)


def _gather_rows(idx, h, d, *, seq_len):
    tpr = _tiles_per_row(d)
    n_seq = h.shape[0] // (seq_len * tpr)
    cap = CAPACITY_FACTOR * seq_len // N_EXPERTS
    return pl.pallas_call(
        functools.partial(_gather_rows_kernel, cap=cap, n_exp=N_EXPERTS),
        grid_spec=pltpu.PrefetchScalarGridSpec(
            num_scalar_prefetch=1,
            grid=(n_seq, N_EXPERTS // ROW_LOOP_EXPERTS),
            in_specs=[pl.BlockSpec((seq_len * tpr, LANES), lambda s, e, idx: (s, 0))],
            out_specs=pl.BlockSpec((ROW_LOOP_EXPERTS, cap, d), lambda s, e, idx: (e, s, 0)),
            scratch_shapes=[pltpu.VMEM((cap * tpr, LANES), F32)],
        ),
        out_shape=jax.ShapeDtypeStruct((N_EXPERTS, n_seq * cap, d), BF16),
        compiler_params=_params(2, 48),
        name="gather_rows",
    )(idx, h)


def _ffn_kernel(xp_ref, xs_ref, gp_ref, gs_ref, wg_ref, wu_ref, wd_ref, yp_ref, ys_ref):
    rc = 512
    f = pl.program_id(1)
    last = f == pl.num_programs(1) - 1

    def body(first):
        wg = wg_ref[0, 0].astype(BF16)
        wu = wu_ref[0, 0].astype(BF16)
        wd = wd_ref[0, 0].astype(BF16)
        for x_ref, g_ref, y_ref in ((xp_ref, gp_ref, yp_ref), (xs_ref, gs_ref, ys_ref)):
            for r in range(0, x_ref.shape[1], rc):
                x = x_ref[0, r:r + rc, :]
                hg = jnp.dot(x, wg, preferred_element_type=F32)
                hu = jnp.dot(x, wu, preferred_element_type=F32)
                hid = ((hg * jax.nn.sigmoid(hg)) * hu).astype(BF16)
                g_hi, g_lo = _split_bf16(g_ref[0, :, r:r + rc])
                ones = jnp.ones((g_hi.shape[0], LANES), BF16)
                tn = functools.partial(lax.dot_general, dimension_numbers=(((0,), (0,)), ((), ())),
                                       preferred_element_type=F32)
                weight = jnp.where(last, tn(g_hi, ones) + tn(g_lo, ones), 1.0)
                weight = jnp.concatenate([weight] * (y_ref.shape[2] // LANES), axis=1)
                y = jnp.dot(hid, wd, preferred_element_type=F32)
                if not first:
                    y = y_ref[0, r:r + rc, :] + y
                y_ref[0, r:r + rc, :] = y * weight

    pl.when(f == 0)(functools.partial(body, True))
    pl.when(f != 0)(functools.partial(body, False))


def _ffn(xs_p, xs_s, g_p, g_s, w_gate, w_up, w_down, layer):
    n_exp, rp, d = xs_p.shape
    rs = xs_s.shape[1]
    ff = w_gate.shape[3]
    fc = 1024
    return pl.pallas_call(
        _ffn_kernel,
        grid=(n_exp, ff // fc),
        in_specs=[
            pl.BlockSpec((1, rp, d), lambda e, f: (e, 0, 0)),
            pl.BlockSpec((1, rs, d), lambda e, f: (e, 0, 0)),
            pl.BlockSpec((1, SUBLANES, rp), lambda e, f: (e, 0, 0)),
            pl.BlockSpec((1, SUBLANES, rs), lambda e, f: (e, 0, 0)),
            pl.BlockSpec((1, 1, d, fc), lambda e, f: (layer, e, 0, f)),
            pl.BlockSpec((1, 1, d, fc), lambda e, f: (layer, e, 0, f)),
            pl.BlockSpec((1, 1, fc, d), lambda e, f: (layer, e, f, 0)),
        ],
        out_specs=[pl.BlockSpec((1, rp, d), lambda e, f: (e, 0, 0)),
                   pl.BlockSpec((1, rs, d), lambda e, f: (e, 0, 0))],
        out_shape=[jax.ShapeDtypeStruct((n_exp, rp, d), F32),
                   jax.ShapeDtypeStruct((n_exp, rs, d), F32)],
        compiler_params=_params(2, 58),
        name="expert_ffn",
    )(xs_p, xs_s, g_p, g_s, w_gate, w_up, w_down)


def _combine_onehot_kernel(*refs, mod_row, final):
    if final:
        idx_ref, ye_ref, x_ref, pmod_ref, gain_ref, o_ref = refs
    else:
        idx_ref, ye_ref, x_ref, pmod_ref, o_ref = refs
    t_len, d = o_ref.shape
    n_exp = ye_ref.shape[0]
    idx = idx_ref[0]
    hit = jnp.where(idx == lax.broadcasted_iota(I32, (t_len, idx.shape[1]), 0), 1.0, 0.0).astype(BF16)
    y_hi, y_lo = _split_bf16(jnp.concatenate([ye_ref[e] for e in range(n_exp)], axis=0))
    acc = jnp.dot(hit, y_hi, preferred_element_type=F32) + jnp.dot(hit, y_lo, preferred_element_type=F32)
    x = x_ref[...] + pmod_ref[pl.ds(mod_row(pl.program_id(0), t_len), 1), pl.ds(5 * d, d)] * acc
    if final:
        x = (x * lax.rsqrt(jnp.mean(x * x, axis=-1, keepdims=True) + EPS)) * gain_ref[...]
    o_ref[...] = x


def _combine_onehot(idx_row, ye, x, pmod, final_gain, *, seq_len, mod_row):
    n_exp, rows, d = ye.shape
    n_seq, _, slots = idx_row.shape
    cap = slots // n_exp
    in_specs = [pl.BlockSpec((1, 1, slots), lambda s: (s, 0, 0)),
                pl.BlockSpec((n_exp, cap, d), lambda s: (0, s, 0)),
                pl.BlockSpec((seq_len, d), lambda s: (s, 0)),
                pl.BlockSpec(pmod.shape, lambda s: (0, 0))]
    args = [idx_row, ye, x, pmod]
    if final_gain is not None:
        in_specs.append(pl.BlockSpec((1, d), lambda s: (0, 0)))
        args.append(final_gain)
    return pl.pallas_call(
        functools.partial(_combine_onehot_kernel, mod_row=mod_row, final=final_gain is not None),
        grid=(n_seq,),
        in_specs=in_specs,
        out_specs=pl.BlockSpec((seq_len, d), lambda s: (s, 0)),
        out_shape=jax.ShapeDtypeStruct((n_seq * seq_len, d), F32),
        compiler_params=_params(1, 32),
        name="combine_onehot",
    )(*args)


COMBINE_GROUP = 16
ROW_LOOP_EXPERTS = 2


def _combine_rows_kernel(idx_ref, ye_ref, o_ref, tiles, *, cap, n_exp):
    e = pl.program_id(1)
    d = ye_ref.shape[2]
    tpr = _tiles_per_row(d)
    ch = 2048

    @pl.when(e == 0)
    def _():
        def zero(c, carry):
            o_ref[pl.ds(pl.multiple_of(c * ch, ch), ch), :] = jnp.zeros((ch, LANES), F32)
            return carry
        lax.fori_loop(0, o_ref.shape[0] // ch, zero, 0)

    for el in range(ROW_LOOP_EXPERTS):
        tiles[...] = ye_ref[el].reshape(cap, tpr, LANES).reshape(cap * tpr, LANES)

        base = (pl.program_id(0) * n_exp + e * ROW_LOOP_EXPERTS + el) * cap

        def body(i, carry, base=base):
            c0 = i * COMBINE_GROUP
            dst = [pl.ds(pl.multiple_of(idx_ref[base + c0 + k], tpr), tpr) for k in range(COMBINE_GROUP)]
            vals = [o_ref[dst[k], :] + tiles[pl.ds(pl.multiple_of((c0 + k) * tpr, tpr), tpr), :]
                    for k in range(COMBINE_GROUP)]
            for k in range(COMBINE_GROUP):
                o_ref[dst[k], :] = vals[k]
            return carry

        lax.fori_loop(0, cap // COMBINE_GROUP, body, 0, unroll=2)


def _combine_rows(idx, ye, *, seq_len):
    n_exp, rows, d = ye.shape
    tpr = _tiles_per_row(d)
    cap = CAPACITY_FACTOR * seq_len // N_EXPERTS
    n_seq = rows // cap
    return pl.pallas_call(
        functools.partial(_combine_rows_kernel, cap=cap, n_exp=n_exp),
        grid_spec=pltpu.PrefetchScalarGridSpec(
            num_scalar_prefetch=1,
            grid=(n_seq, n_exp // ROW_LOOP_EXPERTS),
            in_specs=[pl.BlockSpec((ROW_LOOP_EXPERTS, cap, d), lambda s, e, i: (e, s, 0))],
            out_specs=pl.BlockSpec((seq_len * tpr, LANES), lambda s, e, i: (s, 0)),
            scratch_shapes=[pltpu.VMEM((cap * tpr, LANES), F32)],
        ),
        out_shape=jax.ShapeDtypeStruct((n_seq * seq_len * tpr, LANES), F32),
        compiler_params=_params(2, 48),
        name="combine_rows",
    )(idx, ye)


def _head_norm(x, gain):
    return x * lax.rsqrt(jnp.mean(x * x, axis=-1, keepdims=True) + EPS) * gain


def _qkv_kernel(*refs, rope, has_prev, mod_row, n_heads, n_kv):
    refs = list(refs)
    x_ref = refs.pop(0)
    prev = None
    if has_prev:
        acc_ref, pmod_ref = refs.pop(0), refs.pop(0)
    mod_ref, gain_ref, w_ref, qg_ref, kg_ref = refs[:5]
    refs = refs[5:]
    if rope:
        cos_ref, sin_ref = refs.pop(0), refs.pop(0)
    if has_prev:
        prev = (acc_ref, pmod_ref, refs.pop(0))
    q_ref, k_ref, v_ref = refs[:3]
    d = x_ref.shape[1]
    hd = HEAD_DIM
    m = mod_row(pl.program_id(0), x_ref.shape[0])
    x = _residual_stream(x_ref, prev, m)
    shift = mod_ref[pl.ds(m, 1), pl.ds(0, d)]
    scale = mod_ref[pl.ds(m, 1), pl.ds(d, d)]
    h = _modulated_norm(x, gain_ref[...], shift, scale).astype(BF16)
    qg = qg_ref[...]
    kg = kg_ref[...]
    q_scale = math.log2(math.e) * hd ** -0.5
    half = x_ref.shape[0] // 2
    for r0 in (0, half):
        rows = slice(r0, r0 + half)
        qkv = jnp.dot(h[rows], w_ref[...], preferred_element_type=F32)
        if rope:
            cos = cos_ref[rows, :]
            sin = sin_ref[rows, :]

            def rot(xh, cos=cos, sin=sin):
                return xh * cos + pltpu.roll(xh, hd // 2, 1) * sin
        else:
            rot = lambda xh: xh

        for i in range(n_heads):
            qh = rot(_head_norm(qkv[:, i * hd:(i + 1) * hd], qg)) * q_scale
            q_ref[rows, i * hd:(i + 1) * hd] = qh.astype(BF16)
        for i in range(n_kv):
            c0 = (n_heads + i) * hd
            kh = _head_norm(qkv[:, c0:c0 + hd], kg)
            if not rope:
                refs[3][rows, i * hd:(i + 1) * hd] = kh
            k_ref[rows, i * hd:(i + 1) * hd] = rot(kh).astype(BF16)
        v = qkv[:, (n_heads + n_kv) * hd:]
        v_ref[rows, :] = v.astype(BF16)
        if not rope:
            refs[4][rows, :] = v


def _qkv(x, prev, mod, gain, w_qkv, q_gain, k_gain, tables, *, seq_len, mod_row, n_heads, n_kv):
    n, d = x.shape
    hd = HEAD_DIM
    row = lambda i: (i, 0)
    fixed = lambda i: (0, 0)
    rope = tables is not None
    rb = QKV_ROW_BLOCK
    in_specs = [pl.BlockSpec((rb, d), row)]
    args = [x]
    if prev is not None:
        in_specs += [_rows_spec(prev[0], d, rb), pl.BlockSpec(prev[1].shape, fixed)]
        args += list(prev)
    in_specs += [
        pl.BlockSpec(mod.shape, fixed),
        pl.BlockSpec((1, d), fixed),
        pl.BlockSpec(w_qkv.shape, fixed),
        pl.BlockSpec((1, hd), fixed),
        pl.BlockSpec((1, hd), fixed),
    ]
    args += [mod, gain, w_qkv, q_gain, k_gain]
    if rope:
        per_seq = seq_len // rb
        in_specs += [pl.BlockSpec((rb, hd), lambda i: (i % per_seq, 0))] * 2
        args += list(tables)
    out_specs, out_shape = [], []
    if prev is not None:
        out_specs.append(pl.BlockSpec((rb, d), row))
        out_shape.append(jax.ShapeDtypeStruct((n, d), F32))
    out_specs += [pl.BlockSpec((rb, n_heads * hd), row),
                  pl.BlockSpec((rb, n_kv * hd), row),
                  pl.BlockSpec((rb, n_kv * hd), row)]
    out_shape += [jax.ShapeDtypeStruct((n, n_heads * hd), BF16),
                  jax.ShapeDtypeStruct((n, n_kv * hd), BF16),
                  jax.ShapeDtypeStruct((n, n_kv * hd), BF16)]
    if not rope:
        out_specs += [pl.BlockSpec((rb, n_kv * hd), row)] * 2
        out_shape += [jax.ShapeDtypeStruct((n, n_kv * hd), F32)] * 2
    outs = pl.pallas_call(
        functools.partial(_qkv_kernel, rope=rope, has_prev=prev is not None, mod_row=mod_row,
                          n_heads=n_heads, n_kv=n_kv),
        grid=(n // rb,),
        in_specs=in_specs,
        out_specs=out_specs,
        out_shape=out_shape,
        compiler_params=_params(1, 48),
        name="qkv_rope" if rope else "qkv",
    )(*args)
    return list(outs) if prev is not None else [x] + list(outs)


def _attn_kernel(*refs, group, chunk, has_cache):
    if has_cache:
        q_ref, k_ref, v_ref, kc_ref, vc_ref, o_ref = refs
    else:
        q_ref, k_ref, v_ref, o_ref = refs
    hd = HEAD_DIM
    qb = q_ref.shape[0]
    rows = group * qb
    q = jnp.concatenate([q_ref[:, g * hd:(g + 1) * hd] for g in range(group)], axis=0)
    sources = [(k_ref, v_ref, s0, chunk) for s0 in range(0, k_ref.shape[0], chunk)]
    if has_cache:
        sources.append((kc_ref.at[0], vc_ref.at[0], 0, kc_ref.shape[1]))
    m = jnp.full((rows, 1), -jnp.inf, F32)
    acc = jnp.zeros((rows, 2 * hd), F32)
    for kr, vr, s0, size in sources:
        s = lax.dot_general(q, kr[s0:s0 + size, :], (((1,), (1,)), ((), ())),
                            preferred_element_type=F32)
        m_new = jnp.maximum(m, jnp.max(s, axis=-1, keepdims=True))
        p = jnp.exp2(s - m_new).astype(BF16)
        v_ones = jnp.concatenate([vr[s0:s0 + size, :], jnp.ones((size, hd), BF16)], axis=1)
        acc = jnp.exp2(m - m_new) * acc + jnp.dot(p, v_ones, preferred_element_type=F32)
        m = m_new
    o = acc[:, :hd] / acc[:, hd:]
    o_ref[...] = jnp.concatenate([o[g * qb:(g + 1) * qb] for g in range(group)], axis=1).astype(BF16)


def _attention(q, k, v, cache, *, seq_len, qb, chunk, n_heads, n_kv):
    n = q.shape[0]
    hd = HEAD_DIM
    group = n_heads // n_kv
    nq = seq_len // qb
    in_specs = [
        pl.BlockSpec((qb, group * hd), lambda b, h, i: (b * nq + i, h)),
        pl.BlockSpec((seq_len, hd), lambda b, h, i: (b, h)),
        pl.BlockSpec((seq_len, hd), lambda b, h, i: (b, h)),
    ]
    args = [q, k, v]
    if cache is not None:
        past = cache[0].shape[1]
        in_specs += [pl.BlockSpec((1, past, hd), lambda b, h, i: (b, 0, h))] * 2
        args += list(cache)
    return pl.pallas_call(
        functools.partial(_attn_kernel, group=group, chunk=chunk, has_cache=cache is not None),
        grid=(n // seq_len, n_kv, nq),
        in_specs=in_specs,
        out_specs=pl.BlockSpec((qb, group * hd), lambda b, h, i: (b * nq + i, h)),
        out_shape=jax.ShapeDtypeStruct((n, n_heads * hd), BF16),
        compiler_params=_params(3, 48),
        name=f"attention_{seq_len}",
    )(*args)


def _final_norm_kernel(x_ref, acc_ref, pmod_ref, gain_ref, o_ref, *, mod_row):
    n_rows, d = x_ref.shape
    m = mod_row(pl.program_id(0), x_ref.shape[0])
    x = x_ref[...] + pmod_ref[pl.ds(m, 1), pl.ds(5 * d, d)] * _load_rows(acc_ref, n_rows, d)
    o_ref[...] = (x * lax.rsqrt(jnp.mean(x * x, axis=-1, keepdims=True) + EPS)) * gain_ref[...]


def _final_norm(x, acc, pmod, gain, mod_row):
    n, d = x.shape
    row = lambda i: (i, 0)
    fixed = lambda i: (0, 0)
    return pl.pallas_call(
        functools.partial(_final_norm_kernel, mod_row=mod_row),
        grid=(n // ROW_BLOCK,),
        in_specs=[pl.BlockSpec((ROW_BLOCK, d), row), _rows_spec(acc, d, ROW_BLOCK),
                  pl.BlockSpec(pmod.shape, fixed), pl.BlockSpec((1, d), fixed)],
        out_specs=pl.BlockSpec((ROW_BLOCK, d), row),
        out_shape=jax.ShapeDtypeStruct((n, d), F32),
        compiler_params=_params(1, 32),
        name="final_norm",
    )(x, acc, pmod, gain)


def _rope_head_order(x):
    q = HEAD_DIM // 4
    row1, row2, col1, col2 = (x[..., i * q:(i + 1) * q] for i in range(4))
    return jnp.concatenate([row1, col1, row2, col2], axis=-1)


def _rope_tables(seq_len):
    axis = HEAD_DIM // 2
    t = np.arange(seq_len)
    inv = ROPE_BASE ** (-np.arange(axis // 2, dtype=np.float64) * 2.0 / axis)
    ang = np.concatenate([(t // GRID_W)[:, None] * inv, (t % GRID_W)[:, None] * inv], axis=1)
    c, s = np.cos(ang), np.sin(ang)
    return (jnp.asarray(np.concatenate([c, c], axis=1), F32), jnp.asarray(np.concatenate([-s, s], axis=1), F32))


def kernel(x_prompt, x_sample, state_lru, cache_k, cache_v, c, c_ctx, w_mod, b_mod, norm_gain, final_gain,
           w_lru_in, lru_conv_w, lru_conv_b, lru_wa, lru_ba, lru_wx, lru_bx, lru_lambda, w_lru_out,
           w_qkv, q_norm, k_norm, w_attn_out, w_router, w_exp_gate, w_exp_up, w_exp_down):
    batch, seq, d = x_prompt.shape
    dec_batch, dec_seq, _ = x_sample.shape
    depth = w_mod.shape[0]
    n_kv = cache_k.shape[3]
    n_heads = w_attn_out.shape[1] // HEAD_DIM
    heads = dict(n_heads=n_heads, n_kv=n_kv)

    cond = jnp.zeros((SUBLANES, d), F32).at[0].set(c_ctx).at[1:1 + dec_batch].set(c)
    mod_all = _mod_vectors(cond, w_mod, b_mod)

    xs = [x_prompt.reshape(batch * seq, d), x_sample.reshape(dec_batch * dec_seq, d)]
    lens = [seq, dec_seq]
    mod_rows = [lambda i, rows: 0, lambda i, rows: 1 + (i * rows) // dec_seq]
    prevs = [None, None]

    fg = final_gain.reshape(1, d)
    wr_hi, wr_lo = _split_bf16(jnp.swapaxes(w_router, 1, 2))
    new_lru, new_k, new_v = [], [], []
    for l in range(depth):
        mod = mod_all[l]
        gain_a = norm_gain[l, 0].reshape(1, d)
        gain_c = norm_gain[l, 1].reshape(1, d)
        mixed = []
        if l % 2 == 0:
            li = l // 2
            w_in = w_lru_in[li].astype(BF16)
            wg = (0.5 * jnp.concatenate([lru_wa[li, 0], lru_wx[li, 0], lru_wa[li, 1], lru_wx[li, 1]],
                                        axis=2)).astype(BF16)
            bias4 = 0.5 * jnp.stack([lru_ba[li, 0], lru_bx[li, 0], lru_ba[li, 1], lru_bx[li, 1]])
            h0s = [jnp.zeros((batch, 2, w_in.shape[1] // 2), F32), state_lru[:, li]]
            for gi in range(2):
                xs[gi], gate, xr = _lru_in(xs[gi], prevs[gi], mod, gain_a, w_in, mod_rows[gi])
                y, fin = _lru_core(xr, gate, lru_conv_w[li], lru_conv_b[li].reshape(1, -1), wg, bias4,
                                   lru_lambda[li], h0s[gi], seq_len=lens[gi])
                mixed.append(y)
                if gi == 0:
                    new_lru.append(fin.astype(x_prompt.dtype))
            w_out = w_lru_out[li].astype(BF16)
        else:
            ai = l // 2
            w = w_qkv[ai].astype(BF16)
            qg = q_norm[ai].reshape(1, -1)
            kg = k_norm[ai].reshape(1, -1)
            xs[0], q_p, k_p, v_p, kf, vf = _qkv(xs[0], prevs[0], mod, gain_a, w, qg, kg, None,
                                                seq_len=seq, mod_row=mod_rows[0], **heads)
            n_qk = (n_heads + n_kv) * HEAD_DIM
            w_qk = _rope_head_order(w_qkv[ai][:, :n_qk].reshape(d, n_heads + n_kv, HEAD_DIM)).reshape(d, n_qk)
            w_r = jnp.concatenate([w_qk, w_qkv[ai][:, n_qk:]], axis=1).astype(BF16)
            xs[1], q_s, k_s, v_s = _qkv(xs[1], prevs[1], mod, gain_a, w_r, _rope_head_order(qg),
                                        _rope_head_order(kg), _rope_tables(dec_seq),
                                        seq_len=dec_seq, mod_row=mod_rows[1], **heads)
            new_k.append(kf.reshape(batch, seq, n_kv, HEAD_DIM))
            new_v.append(vf.reshape(batch, seq, n_kv, HEAD_DIM))
            past = cache_k.shape[2]
            kc = _rope_head_order(cache_k[:, ai]).reshape(dec_batch, past, n_kv * HEAD_DIM).astype(BF16)
            vc = cache_v[:, ai].reshape(dec_batch, past, n_kv * HEAD_DIM).astype(BF16)
            mixed.append(_attention(q_p, k_p, v_p, None, seq_len=seq, qb=seq, chunk=seq, **heads))
            mixed.append(_attention(q_s, k_s, v_s, (kc, vc), seq_len=dec_seq, qb=512, chunk=256, **heads))
            w_out = w_attn_out[ai].astype(BF16)

        x1_p, h2_p, lg_p = _mix_out(mixed[0], xs[0], mod, gain_c, w_out, wr_hi, wr_lo, l, mod_rows[0], False)
        x1_s, h2_s, lg_s = _mix_out(mixed[1], xs[1], mod, gain_c, w_out, wr_hi, wr_lo, l, mod_rows[1], True)
        idx_p, g_p, idx_s, g_s = _route(lg_p, lg_s, seq, dec_seq)
        xe_p = _gather_onehot(idx_p[:, None, :], h2_p, seq_len=seq)
        off_s = idx_s * _tiles_per_row(d)
        xe_s = _gather_rows(off_s, h2_s, d, seq_len=dec_seq)
        e_n = N_EXPERTS
        zero_rows = ((0, 0), (0, SUBLANES - 1), (0, 0))
        gc_p = jnp.pad(g_p.reshape(batch, e_n, -1).transpose(1, 0, 2).reshape(e_n, 1, -1), zero_rows)
        gc_s = jnp.pad(g_s.reshape(dec_batch, e_n, -1).transpose(1, 0, 2).reshape(e_n, 1, -1), zero_rows)
        ye_p, ye_s = _ffn(xe_p, xe_s, gc_p, gc_s, w_exp_gate, w_exp_up, w_exp_down, l)
        last = l == depth - 1
        x_p = _combine_onehot(idx_p[:, None, :], ye_p, x1_p, mod, fg if last else None, seq_len=seq,
                              mod_row=mod_rows[0])
        acc_s = _combine_rows(off_s, ye_s, seq_len=dec_seq)
        xs = [x_p, x1_s]
        prevs = [None, (acc_s, mod)]

    y_prompt = xs[0].reshape(batch, seq, d)
    y_sample = _final_norm(xs[1], *prevs[1], fg, mod_rows[1]).reshape(dec_batch, dec_seq, d)
    return (y_prompt, y_sample, jnp.stack(new_lru, axis=1), jnp.stack(new_k, axis=1), jnp.stack(new_v, axis=1))
```

```python
import functools
import math

import jax
import jax.numpy as jnp
import numpy as np
from jax import lax
from jax.experimental import pallas as pl
from jax.experimental.pallas import tpu as pltpu

F32 = jnp.float32
BF16 = jnp.bfloat16
I32 = jnp.int32

LANES = 128
SUBLANES = 8
MIB = 1024 * 1024
F32_TINY = float(np.finfo(np.float32).tiny)

RG_C = 8.0
CONV_W = 4
CONV_PAD_L = 2
ROPE_BASE = 10000.0
GRID_W = 64
EPS = 1e-6
N_EXPERTS = 16
CAPACITY_FACTOR = 2
HEAD_DIM = 128
LRU_BLOCKS = 8
LRU_MAX_STEP_BLOCKS = 4
LRU_CORE_VMEM_MIB = 44

ROW_BLOCK = 1024
QKV_ROW_BLOCK = 512


def _params(n_axes, vmem_mib):
    return pltpu.CompilerParams(
        dimension_semantics=("arbitrary",) * n_axes, vmem_limit_bytes=vmem_mib * MIB)


def _split_bf16(x):
    hi = x.astype(BF16)
    lo = (x - hi.astype(F32)).astype(BF16)
    return hi, lo


def _modulated_norm(x, gain, shift, scale):
    y = x * lax.rsqrt(jnp.mean(x * x, axis=-1, keepdims=True) + EPS)
    return (y * gain) * (1.0 + scale) + shift


def _tiles_per_row(d):
    return d // LANES


def _load_rows(ref, n_rows, d):
    if ref.shape == (n_rows, d):
        return ref[...]
    tpr = _tiles_per_row(d)
    return jnp.concatenate([ref[pl.ds(s, n_rows, stride=tpr), :] for s in range(tpr)], axis=1)


def _store_rows(ref, x, r0):
    n_rows, d = x.shape
    if ref.shape[1] == d:
        ref[r0:r0 + n_rows, :] = x.astype(ref.dtype)
        return
    tpr = _tiles_per_row(d)
    for s in range(tpr):
        ref[pl.ds(r0 * tpr + s, n_rows, stride=tpr), :] = x[:, s * LANES:(s + 1) * LANES]


def _rows_spec(arr, d, rows):
    if arr.shape[1] == d:
        return pl.BlockSpec((rows, d), lambda i: (i, 0))
    return pl.BlockSpec((rows * _tiles_per_row(d), LANES), lambda i: (i, 0))


def _residual_stream(x_ref, prev, m):
    if prev is None:
        return x_ref[...]
    acc_ref, pmod_ref, xo_ref = prev
    n_rows, d = x_ref.shape
    x = x_ref[...] + pmod_ref[pl.ds(m, 1), pl.ds(5 * d, d)] * _load_rows(acc_ref, n_rows, d)
    xo_ref[...] = x
    return x


def _mod_kernel(cond_ref, w_ref, b_ref, o_ref):
    c = cond_ref[...]
    a_hi, a_lo = _split_bf16(c * jax.nn.sigmoid(c))
    w_hi, w_lo = _split_bf16(w_ref[0])
    dot = functools.partial(jnp.dot, preferred_element_type=F32)
    o_ref[0] = dot(a_hi, w_hi) + (dot(a_lo, w_hi) + dot(a_hi, w_lo)) + b_ref[0]


def _mod_vectors(cond8, w_mod, b_mod):
    depth, d, d6 = w_mod.shape
    nb = 1536
    return pl.pallas_call(
        _mod_kernel,
        grid=(depth, d6 // nb),
        in_specs=[
            pl.BlockSpec((SUBLANES, d), lambda l, j: (0, 0)),
            pl.BlockSpec((1, d, nb), lambda l, j: (l, 0, j)),
            pl.BlockSpec((1, 1, nb), lambda l, j: (l, 0, j)),
        ],
        out_specs=pl.BlockSpec((1, SUBLANES, nb), lambda l, j: (l, 0, j)),
        out_shape=jax.ShapeDtypeStruct((depth, SUBLANES, d6), F32),
        compiler_params=_params(2, 40),
        name="adaln_mod",
    )(cond8, w_mod, b_mod.reshape(depth, 1, d6))


def _lru_in_kernel(*refs, mod_row, has_prev):
    if has_prev:
        x_ref, acc_ref, pmod_ref, mod_ref, gain_ref, w_ref, xo_ref, gate_ref, xr_ref = refs
        prev = (acc_ref, pmod_ref, xo_ref)
    else:
        x_ref, mod_ref, gain_ref, w_ref, gate_ref, xr_ref = refs
        prev = None
    d = x_ref.shape[1]
    m = mod_row(pl.program_id(0), x_ref.shape[0])
    x = _residual_stream(x_ref, prev, m)
    shift = mod_ref[pl.ds(m, 1), pl.ds(0, d)]
    scale = mod_ref[pl.ds(m, 1), pl.ds(d, d)]
    h = _modulated_norm(x, gain_ref[...], shift, scale).astype(BF16)
    u = jnp.dot(h, w_ref[...], preferred_element_type=F32)
    w = gate_ref.shape[1]
    gate_ref[...] = u[:, :w]
    xr_ref[...] = u[:, w:]


def _lru_in(x, prev, mod, gain, w_in, mod_row):
    n, d = x.shape
    w2 = w_in.shape[1]
    w = w2 // 2
    row = lambda i: (i, 0)
    fixed = lambda i: (0, 0)
    in_specs = [pl.BlockSpec((ROW_BLOCK, d), row)]
    args = [x]
    out_specs = [pl.BlockSpec((ROW_BLOCK, w), row), pl.BlockSpec((ROW_BLOCK, w), row)]
    out_shape = [jax.ShapeDtypeStruct((n, w), F32), jax.ShapeDtypeStruct((n, w), F32)]
    if prev is not None:
        in_specs += [_rows_spec(prev[0], d, ROW_BLOCK), pl.BlockSpec(prev[1].shape, fixed)]
        args += list(prev)
        out_specs = [pl.BlockSpec((ROW_BLOCK, d), row)] + out_specs
        out_shape = [jax.ShapeDtypeStruct((n, d), F32)] + out_shape
    in_specs += [pl.BlockSpec(mod.shape, fixed), pl.BlockSpec((1, d), fixed), pl.BlockSpec((d, w2), fixed)]
    args += [mod, gain, w_in]
    outs = pl.pallas_call(
        functools.partial(_lru_in_kernel, mod_row=mod_row, has_prev=prev is not None),
        grid=(n // ROW_BLOCK,),
        in_specs=in_specs,
        out_specs=out_specs,
        out_shape=out_shape,
        compiler_params=_params(1, 48),
        name="lru_in",
    )(*args)
    return outs if prev is not None else [x] + list(outs)


def _lru_core_kernel(xr_ref, gate_ref, cw_ref, cb_ref, wg_ref, bias_ref, lam_ref, h0_ref,
                     y_ref, fin_ref, pad, a_f, b_f, a_b, b_b):
    t_len, lb = xr_ref.shape
    ch = 256
    halo = SUBLANES
    seg = t_len // SUBLANES
    piece = min(ch, seg)
    seg_shift = seg.bit_length() - 1

    def seg_rows(t0):
        r = lax.shift_right_logical(t0, seg_shift)
        return pl.ds((t0 - r * seg) * SUBLANES + r, piece, stride=SUBLANES)

    zero_rows = jnp.zeros((halo, lb), F32)
    pad[0:halo, :] = zero_rows
    pad[t_len + halo:t_len + 2 * halo, :] = zero_rows

    def copy_in(c, carry):
        r = pl.multiple_of(c * ch, ch)
        pad[pl.ds(r + halo, ch), :] = xr_ref[pl.ds(r, ch), :]
        return carry

    lax.fori_loop(0, t_len // ch, copy_in, 0)

    neg_lam = -lam_ref[...]
    softplus = jnp.maximum(neg_lam, 0.0) + jnp.log1p(jnp.exp(-jnp.abs(neg_lam)))
    c_nla = (0.5 * RG_C) * softplus
    c_exp2 = (-0.5 * RG_C * math.log2(math.e)) * softplus
    cw = cw_ref[...]
    cbias = cb_ref[...]
    half_bias = bias_ref[...]
    half_wg = wg_ref[...]
    bw = half_wg.shape[1]
    n_slab = lb // LANES

    def gates(c, carry):
        r = pl.multiple_of(c * ch, ch)
        blk = pad[pl.ds(r, ch + 2 * halo), :]
        xc = cbias
        for k in range(CONV_W):
            o = halo - CONV_PAD_L + k
            xc = xc + blk[o:o + ch] * cw[k:k + 1]
        xc_bf = xc.astype(BF16)
        g = [jnp.dot(xc_bf[:, j * bw:(j + 1) * bw], half_wg[j], preferred_element_type=F32)
             for j in range(lb // bw)]
        pre = lambda k: jnp.concatenate([gj[:, k * bw:(k + 1) * bw] for gj in g], axis=1) + half_bias[k:k + 1]
        half_xc = 0.5 * xc
        for d, (a_ref, b_ref) in enumerate(((a_f, b_f), (a_b, b_b))):
            u = jnp.tanh(pre(2 * d)) + 1.0
            t_i = jnp.tanh(pre(2 * d + 1))
            a = jnp.exp2(u * c_exp2[d:d + 1])
            q = jnp.tanh(u * c_nla[d:d + 1]) * (1.0 + a * a)
            b = (q * lax.rsqrt(jnp.maximum(q, F32_TINY))) * (t_i * half_xc + half_xc)
            for p in range(ch // piece):
                rows = seg_rows(r + p * piece)
                for j in range(n_slab):
                    a_ref[j, rows, :] = a[p * piece:(p + 1) * piece, j * LANES:(j + 1) * LANES]
                    b_ref[j, rows, :] = b[p * piece:(p + 1) * piece, j * LANES:(j + 1) * LANES]
        return carry

    lax.fori_loop(0, t_len // ch, gates, 0)

    def scan(i, carry):
        rows = (pl.ds(pl.multiple_of(i * SUBLANES, SUBLANES), SUBLANES),
                pl.ds(pl.multiple_of((seg - 1 - i) * SUBLANES, SUBLANES), SUBLANES))
        out = []
        for k, (h, prod) in enumerate(carry):
            d, j = divmod(k, n_slab)
            a_ref, b_ref = ((a_f, b_f), (a_b, b_b))[d]
            a = a_ref[j, rows[d], :]
            h = a * h + b_ref[j, rows[d], :]
            prod = a * prod
            b_ref[j, rows[d], :] = h
            a_ref[j, rows[d], :] = prod
            out.append((h, prod))
        return tuple(out)

    start = (jnp.zeros((SUBLANES, LANES), F32), jnp.ones((SUBLANES, LANES), F32))
    ends = lax.fori_loop(0, seg, scan, (start,) * (2 * n_slab), unroll=8)

    enter = []
    for k, (h, prod) in enumerate(ends):
        d, j = divmod(k, n_slab)
        states = [h0_ref[0, d:d + 1, j * LANES:(j + 1) * LANES]]
        for r in (range(SUBLANES) if d == 0 else reversed(range(SUBLANES))):
            states.append(h[r:r + 1] + prod[r:r + 1] * states[-1])
        fin_ref[0, d:d + 1, j * LANES:(j + 1) * LANES] = states[SUBLANES]
        order = states[:SUBLANES] if d == 0 else states[:SUBLANES][::-1]
        enter.append(jnp.concatenate(order, axis=0))

    def fix(i, carry):
        rows = pl.ds(pl.multiple_of(i * SUBLANES, SUBLANES), SUBLANES)
        for j in range(n_slab):
            b_f[j, rows, :] = ((b_f[j, rows, :] + a_f[j, rows, :] * enter[j])
                               + (b_b[j, rows, :] + a_b[j, rows, :] * enter[n_slab + j]))
        return carry

    lax.fori_loop(0, seg, fix, 0, unroll=8)

    def emit(c, carry):
        r = pl.multiple_of(c * ch, ch)
        gt = gate_ref[pl.ds(r, ch), :]
        cdf = 0.5 * (1.0 + jnp.tanh(math.sqrt(2.0 / math.pi) * (gt + 0.044715 * (gt * gt * gt))))
        y = gt * cdf
        for p in range(ch // piece):
            rows = seg_rows(r + p * piece)
            total = jnp.concatenate([b_f[j, rows, :] for j in range(n_slab)], axis=1)
            y_ref[pl.ds(r + p * piece, piece), :] = (total * y[p * piece:(p + 1) * piece]).astype(BF16)
        return carry

    lax.fori_loop(0, t_len // ch, emit, 0)


def _lru_core(xr, gate, conv_w, conv_b, wg, bias4, lam, h0, *, seq_len):
    n, w = xr.shape
    n_seq = n // seq_len
    bw = w // LRU_BLOCKS
    block_bytes = seq_len * bw * (5 * 4 + 2 * 2 * 4 + 2 * 2)
    step_blocks = LRU_MAX_STEP_BLOCKS
    while step_blocks * block_bytes > LRU_CORE_VMEM_MIB * MIB:
        step_blocks //= 2
    lb = step_blocks * bw
    tok = lambda s, c: (s, c)
    chan = lambda s, c: (0, c)
    return pl.pallas_call(
        _lru_core_kernel,
        grid=(n_seq, LRU_BLOCKS // step_blocks),
        in_specs=[
            pl.BlockSpec((seq_len, lb), tok),
            pl.BlockSpec((seq_len, lb), tok),
            pl.BlockSpec((CONV_W, lb), chan),
            pl.BlockSpec((1, lb), chan),
            pl.BlockSpec((step_blocks, bw, 4 * bw), lambda s, c: (c, 0, 0)),
            pl.BlockSpec((4, lb), chan),
            pl.BlockSpec((2, lb), chan),
            pl.BlockSpec((1, 2, lb), lambda s, c: (s, 0, c)),
        ],
        out_specs=[pl.BlockSpec((seq_len, lb), tok),
                   pl.BlockSpec((1, 2, lb), lambda s, c: (s, 0, c))],
        out_shape=[jax.ShapeDtypeStruct((n, w), BF16), jax.ShapeDtypeStruct((n_seq, 2, w), F32)],
        scratch_shapes=[pltpu.VMEM((seq_len + 2 * SUBLANES, lb), F32)]
        + [pltpu.VMEM((lb // LANES, seq_len, LANES), F32)] * 4,
        compiler_params=_params(2, LRU_CORE_VMEM_MIB + 4),
        name=f"lru_core_{seq_len}",
    )(xr, gate, conv_w, conv_b, wg, bias4, lam, h0)


def _mix_out_kernel(a_ref, x_ref, mod_ref, gain_ref, w_ref, wrh_ref, wrl_ref,
                    x1_ref, h2_ref, lg_ref, *, mod_row):
    d = x_ref.shape[1]
    m = mod_row(pl.program_id(0), x_ref.shape[0])
    g_mix = mod_ref[pl.ds(m, 1), pl.ds(2 * d, d)]
    shift = mod_ref[pl.ds(m, 1), pl.ds(3 * d, d)]
    scale = mod_ref[pl.ds(m, 1), pl.ds(4 * d, d)]
    op = jnp.dot(a_ref[...], w_ref[...], preferred_element_type=F32)
    x1 = x_ref[...] + g_mix * op
    x1_ref[...] = x1
    h2 = _modulated_norm(x1, gain_ref[...], shift, scale)
    h_hi, h_lo = _split_bf16(h2)
    _store_rows(h2_ref, h2, 0)
    nt = functools.partial(lax.dot_general, dimension_numbers=(((1,), (1,)), ((), ())),
                           preferred_element_type=F32)
    wr_hi = wrh_ref[0]
    e = wr_hi.shape[0]
    both = nt(jnp.concatenate([wr_hi, wrl_ref[0]], axis=0), h_hi)
    lg_ref[...] = both[:e] + (both[e:] + nt(wr_hi, h_lo))


def _mix_out(a, x, mod, gain, w_out, wr_hi, wr_lo, layer, mod_row, token_tiled):
    n, d = x.shape
    e = wr_hi.shape[1]
    row = lambda i: (i, 0)
    fixed = lambda i: (0, 0)
    router = pl.BlockSpec((1,) + wr_hi.shape[1:], lambda i: (layer, 0, 0))
    tpr = _tiles_per_row(d)
    if token_tiled:
        h2_spec = pl.BlockSpec((ROW_BLOCK * tpr, LANES), row)
        h2_shape = jax.ShapeDtypeStruct((n * tpr, LANES), F32)
    else:
        h2_spec = pl.BlockSpec((ROW_BLOCK, d), row)
        h2_shape = jax.ShapeDtypeStruct((n, d), BF16)
    return pl.pallas_call(
        functools.partial(_mix_out_kernel, mod_row=mod_row),
        grid=(n // ROW_BLOCK,),
        in_specs=[
            pl.BlockSpec((ROW_BLOCK, a.shape[1]), row),
            pl.BlockSpec((ROW_BLOCK, d), row),
            pl.BlockSpec(mod.shape, fixed),
            pl.BlockSpec((1, d), fixed),
            pl.BlockSpec(w_out.shape, fixed),
            router,
            router,
        ],
        out_specs=[pl.BlockSpec((ROW_BLOCK, d), row), h2_spec,
                   pl.BlockSpec((e, ROW_BLOCK), lambda i: (0, i))],
        out_shape=[jax.ShapeDtypeStruct((n, d), F32), h2_shape, jax.ShapeDtypeStruct((e, n), F32)],
        compiler_params=_params(1, 48),
        name="mix_out",
    )(a, x, mod, gain, w_out, wr_hi, wr_lo)


def _lane_sum(tiles):
    acc = tiles[0]
    for t in tiles[1:]:
        acc = acc + t
    return jnp.sum(acc, axis=1, keepdims=True)


def _exclusive_cumsum(flags, upper):
    out = []
    off = jnp.zeros((flags[0].shape[0], 1), F32)
    for f in flags:
        out.append(jnp.dot(f.astype(BF16), upper, preferred_element_type=F32) + off)
        off = off + jnp.sum(f, axis=1, keepdims=True)
    return out


def _route_group(lg_ref, n_seq, t_len, idx_ref, g_ref):
    cap = CAPACITY_FACTOR * t_len // N_EXPERTS
    affs = []
    for s in range(n_seq):
        lg = lg_ref[:, s * t_len:(s + 1) * t_len]
        ex = jnp.exp(lg - jnp.max(lg, axis=0, keepdims=True))
        affs.append(ex / jnp.sum(ex, axis=0, keepdims=True))
    aff = jnp.concatenate(affs, axis=0)
    n_rows = aff.shape[0]
    nt = t_len // LANES
    g = [aff[:, j * LANES:(j + 1) * LANES] for j in range(nt)]

    kth_bits = jnp.zeros((n_rows, 1), I32)
    for bit in range(30, -1, -1):
        cand = kth_bits | (1 << bit)
        cand_f = pltpu.bitcast(cand, F32)
        cnt = _lane_sum([jnp.where(t >= cand_f, 1, 0) for t in g])
        kth_bits = jnp.where(cnt >= cap, cand, kth_bits)
    kth = pltpu.bitcast(kth_bits, F32)

    lane = lax.broadcasted_iota(I32, (LANES, LANES), 0)
    upper = jnp.where(lane < lax.broadcasted_iota(I32, (LANES, LANES), 1), 1.0, 0.0).astype(BF16)
    gt = [t > kth for t in g]
    eq = [t == kth for t in g]
    need = (cap - _lane_sum([jnp.where(m, 1, 0) for m in gt])).astype(F32)
    eq_rank = _exclusive_cumsum([jnp.where(m, 1.0, 0.0) for m in eq], upper)
    sel = [jnp.logical_or(gt[j], jnp.logical_and(eq[j], eq_rank[j] < need)) for j in range(nt)]
    pos = _exclusive_cumsum([jnp.where(m, 1.0, 0.0) for m in sel], upper)

    lane_r = lax.broadcasted_iota(I32, (n_rows, LANES), 1)
    d = [jnp.where(sel[j], lane_r + j * LANES - pos[j].astype(I32), -1) for j in range(nt)]
    for k in range(t_len.bit_length() - 1):
        s = 1 << k
        if s < LANES:
            d_rot = [pltpu.roll(x, LANES - s, 1) for x in d]
            g_rot = [pltpu.roll(x, LANES - s, 1) for x in g]
            same = lane_r < LANES - s
            d_in = [jnp.where(same, d_rot[j], d_rot[(j + 1) % nt]) for j in range(nt)]
            g_in = [jnp.where(same, g_rot[j], g_rot[(j + 1) % nt]) for j in range(nt)]
        else:
            q = s // LANES
            d_in = [d[(j + q) % nt] for j in range(nt)]
            g_in = [g[(j + q) % nt] for j in range(nt)]
        new_d, new_g = [], []
        for j in range(nt):
            move = jnp.logical_and(d_in[j] >= 0, ((d_in[j] >> k) & 1) == 1)
            stay = jnp.logical_and(d[j] >= 0, ((d[j] >> k) & 1) == 0)
            new_d.append(jnp.where(move, d_in[j], jnp.where(stay, d[j], -1)))
            new_g.append(jnp.where(move, g_in[j], g[j]))
        d, g = new_d, new_g

    for j in range(idx_ref.shape[1] // LANES):
        idx_ref[:, j * LANES:(j + 1) * LANES] = lane_r + j * LANES + d[j]
        g_ref[:, j * LANES:(j + 1) * LANES] = g[j]


def _route_kernel(lgp_ref, lgs_ref, idx_p_ref, g_p_ref, idx_s_ref, g_s_ref, *, p_len, s_len):
    _route_group(lgp_ref, lgp_ref.shape[1] // p_len, p_len, idx_p_ref, g_p_ref)
    _route_group(lgs_ref, lgs_ref.shape[1] // s_len, s_len, idx_s_ref, g_s_ref)


def _route(lg_p, lg_s, p_len, s_len):
    e = lg_p.shape[0]
    np_seq = lg_p.shape[1] // p_len
    ns_seq = lg_s.shape[1] // s_len
    cap_p = CAPACITY_FACTOR * p_len // N_EXPERTS
    cap_s = CAPACITY_FACTOR * s_len // N_EXPERTS
    wp = max(cap_p, LANES)
    ws = max(cap_s, LANES)
    idx_p, g_p, idx_s, g_s = pl.pallas_call(
        functools.partial(_route_kernel, p_len=p_len, s_len=s_len),
        out_shape=[jax.ShapeDtypeStruct((np_seq * e, wp), I32),
                   jax.ShapeDtypeStruct((np_seq * e, wp), F32),
                   jax.ShapeDtypeStruct((ns_seq * e, ws), I32),
                   jax.ShapeDtypeStruct((ns_seq * e, ws), F32)],
        compiler_params=pltpu.CompilerParams(vmem_limit_bytes=40 * MIB),
        name="route",
    )(lg_p, lg_s)
    return (idx_p[:, :cap_p].reshape(np_seq, e * cap_p), g_p[:, :cap_p].reshape(np_seq, e * cap_p),
            idx_s[:, :cap_s].reshape(-1), g_s[:, :cap_s].reshape(-1))


def _gather_onehot_kernel(idx_ref, h_ref, o_ref):
    t_len = h_ref.shape[0]
    n_exp, cap, _ = o_ref.shape
    idx = idx_ref[0]
    hit = jnp.where(idx == lax.broadcasted_iota(I32, (t_len, idx.shape[1]), 0), 1.0, 0.0).astype(BF16)
    xs = lax.dot_general(hit, h_ref[...], (((0,), (0,)), ((), ())), preferred_element_type=F32).astype(BF16)
    for e in range(n_exp):
        o_ref[e] = xs[e * cap:(e + 1) * cap]


def _gather_onehot(idx_row, h, *, seq_len):
    n, d = h.shape
    n_seq, _, slots = idx_row.shape
    cap = slots // N_EXPERTS
    return pl.pallas_call(
        _gather_onehot_kernel,
        grid=(n_seq,),
        in_specs=[pl.BlockSpec((1, 1, slots), lambda s: (s, 0, 0)),
                  pl.BlockSpec((seq_len, d), lambda s: (s, 0))],
        out_specs=pl.BlockSpec((N_EXPERTS, cap, d), lambda s: (0, s, 0)),
        out_shape=jax.ShapeDtypeStruct((N_EXPERTS, n_seq * cap, d), BF16),
        compiler_params=_params(1, 32),
        name="gather_onehot",
    )(idx_row, h)


def _gather_rows_kernel(idx_ref, h_ref, o_ref, tiles, *, cap, n_exp):
    tpr = _tiles_per_row(o_ref.shape[2])
    for el in range(ROW_LOOP_EXPERTS):
        base = (pl.program_id(0) * n_exp + pl.program_id(1) * ROW_LOOP_EXPERTS + el) * cap

        def body(c, carry, base=base):
            src = pl.multiple_of(idx_ref[base + c], tpr)
            tiles[pl.ds(pl.multiple_of(c * tpr, tpr), tpr), :] = h_ref[pl.ds(src, tpr), :]
            return carry

        lax.fori_loop(0, cap, body, 0, unroll=16)
        o_ref[el] = tiles[...].reshape(cap, tpr, LANES).reshape(cap, tpr * LANES).astype(BF16)


def _gather_rows(idx, h, d, *, seq_len):
    tpr = _tiles_per_row(d)
    n_seq = h.shape[0] // (seq_len * tpr)
    cap = CAPACITY_FACTOR * seq_len // N_EXPERTS
    return pl.pallas_call(
        functools.partial(_gather_rows_kernel, cap=cap, n_exp=N_EXPERTS),
        grid_spec=pltpu.PrefetchScalarGridSpec(
            num_scalar_prefetch=1,
            grid=(n_seq, N_EXPERTS // ROW_LOOP_EXPERTS),
            in_specs=[pl.BlockSpec((seq_len * tpr, LANES), lambda s, e, idx: (s, 0))],
            out_specs=pl.BlockSpec((ROW_LOOP_EXPERTS, cap, d), lambda s, e, idx: (e, s, 0)),
            scratch_shapes=[pltpu.VMEM((cap * tpr, LANES), F32)],
        ),
        out_shape=jax.ShapeDtypeStruct((N_EXPERTS, n_seq * cap, d), BF16),
        compiler_params=_params(2, 48),
        name="gather_rows",
    )(idx, h)


def _ffn_kernel(xp_ref, xs_ref, gp_ref, gs_ref, wg_ref, wu_ref, wd_ref, yp_ref, ys_ref):
    rc = 512
    f = pl.program_id(1)
    last = f == pl.num_programs(1) - 1

    def body(first):
        wg = wg_ref[0, 0].astype(BF16)
        wu = wu_ref[0, 0].astype(BF16)
        wd = wd_ref[0, 0].astype(BF16)
        for x_ref, g_ref, y_ref in ((xp_ref, gp_ref, yp_ref), (xs_ref, gs_ref, ys_ref)):
            for r in range(0, x_ref.shape[1], rc):
                x = x_ref[0, r:r + rc, :]
                hg = jnp.dot(x, wg, preferred_element_type=F32)
                hu = jnp.dot(x, wu, preferred_element_type=F32)
                hid = ((hg * jax.nn.sigmoid(hg)) * hu).astype(BF16)
                g_hi, g_lo = _split_bf16(g_ref[0, :, r:r + rc])
                ones = jnp.ones((g_hi.shape[0], LANES), BF16)
                tn = functools.partial(lax.dot_general, dimension_numbers=(((0,), (0,)), ((), ())),
                                       preferred_element_type=F32)
                weight = jnp.where(last, tn(g_hi, ones) + tn(g_lo, ones), 1.0)
                weight = jnp.concatenate([weight] * (y_ref.shape[2] // LANES), axis=1)
                y = jnp.dot(hid, wd, preferred_element_type=F32)
                if not first:
                    y = y_ref[0, r:r + rc, :] + y
                y_ref[0, r:r + rc, :] = y * weight

    pl.when(f == 0)(functools.partial(body, True))
    pl.when(f != 0)(functools.partial(body, False))


def _ffn(xs_p, xs_s, g_p, g_s, w_gate, w_up, w_down, layer):
    n_exp, rp, d = xs_p.shape
    rs = xs_s.shape[1]
    ff = w_gate.shape[3]
    fc = 1024
    return pl.pallas_call(
        _ffn_kernel,
        grid=(n_exp, ff // fc),
        in_specs=[
            pl.BlockSpec((1, rp, d), lambda e, f: (e, 0, 0)),
            pl.BlockSpec((1, rs, d), lambda e, f: (e, 0, 0)),
            pl.BlockSpec((1, SUBLANES, rp), lambda e, f: (e, 0, 0)),
            pl.BlockSpec((1, SUBLANES, rs), lambda e, f: (e, 0, 0)),
            pl.BlockSpec((1, 1, d, fc), lambda e, f: (layer, e, 0, f)),
            pl.BlockSpec((1, 1, d, fc), lambda e, f: (layer, e, 0, f)),
            pl.BlockSpec((1, 1, fc, d), lambda e, f: (layer, e, f, 0)),
        ],
        out_specs=[pl.BlockSpec((1, rp, d), lambda e, f: (e, 0, 0)),
                   pl.BlockSpec((1, rs, d), lambda e, f: (e, 0, 0))],
        out_shape=[jax.ShapeDtypeStruct((n_exp, rp, d), F32),
                   jax.ShapeDtypeStruct((n_exp, rs, d), F32)],
        compiler_params=_params(2, 58),
        name="expert_ffn",
    )(xs_p, xs_s, g_p, g_s, w_gate, w_up, w_down)


def _combine_onehot_kernel(*refs, mod_row, final):
    if final:
        idx_ref, ye_ref, x_ref, pmod_ref, gain_ref, o_ref = refs
    else:
        idx_ref, ye_ref, x_ref, pmod_ref, o_ref = refs
    t_len, d = o_ref.shape
    n_exp = ye_ref.shape[0]
    idx = idx_ref[0]
    hit = jnp.where(idx == lax.broadcasted_iota(I32, (t_len, idx.shape[1]), 0), 1.0, 0.0).astype(BF16)
    y_hi, y_lo = _split_bf16(jnp.concatenate([ye_ref[e] for e in range(n_exp)], axis=0))
    acc = jnp.dot(hit, y_hi, preferred_element_type=F32) + jnp.dot(hit, y_lo, preferred_element_type=F32)
    x = x_ref[...] + pmod_ref[pl.ds(mod_row(pl.program_id(0), t_len), 1), pl.ds(5 * d, d)] * acc
    if final:
        x = (x * lax.rsqrt(jnp.mean(x * x, axis=-1, keepdims=True) + EPS)) * gain_ref[...]
    o_ref[...] = x


def _combine_onehot(idx_row, ye, x, pmod, final_gain, *, seq_len, mod_row):
    n_exp, rows, d = ye.shape
    n_seq, _, slots = idx_row.shape
    cap = slots // n_exp
    in_specs = [pl.BlockSpec((1, 1, slots), lambda s: (s, 0, 0)),
                pl.BlockSpec((n_exp, cap, d), lambda s: (0, s, 0)),
                pl.BlockSpec((seq_len, d), lambda s: (s, 0)),
                pl.BlockSpec(pmod.shape, lambda s: (0, 0))]
    args = [idx_row, ye, x, pmod]
    if final_gain is not None:
        in_specs.append(pl.BlockSpec((1, d), lambda s: (0, 0)))
        args.append(final_gain)
    return pl.pallas_call(
        functools.partial(_combine_onehot_kernel, mod_row=mod_row, final=final_gain is not None),
        grid=(n_seq,),
        in_specs=in_specs,
        out_specs=pl.BlockSpec((seq_len, d), lambda s: (s, 0)),
        out_shape=jax.ShapeDtypeStruct((n_seq * seq_len, d), F32),
        compiler_params=_params(1, 32),
        name="combine_onehot",
    )(*args)


COMBINE_GROUP = 16
ROW_LOOP_EXPERTS = 4


def _combine_rows_kernel(idx_ref, ye_ref, o_ref, tiles, *, cap, n_exp):
    e = pl.program_id(1)
    d = ye_ref.shape[2]
    tpr = _tiles_per_row(d)
    ch = 2048

    @pl.when(e == 0)
    def _():
        def zero(c, carry):
            o_ref[pl.ds(pl.multiple_of(c * ch, ch), ch), :] = jnp.zeros((ch, LANES), F32)
            return carry
        lax.fori_loop(0, o_ref.shape[0] // ch, zero, 0)

    for el in range(ROW_LOOP_EXPERTS):
        tiles[...] = ye_ref[el].reshape(cap, tpr, LANES).reshape(cap * tpr, LANES)

        base = (pl.program_id(0) * n_exp + e * ROW_LOOP_EXPERTS + el) * cap

        def body(i, carry, base=base):
            c0 = i * COMBINE_GROUP
            dst = [pl.ds(pl.multiple_of(idx_ref[base + c0 + k], tpr), tpr) for k in range(COMBINE_GROUP)]
            vals = [o_ref[dst[k], :] + tiles[pl.ds(pl.multiple_of((c0 + k) * tpr, tpr), tpr), :]
                    for k in range(COMBINE_GROUP)]
            for k in range(COMBINE_GROUP):
                o_ref[dst[k], :] = vals[k]
            return carry

        lax.fori_loop(0, cap // COMBINE_GROUP, body, 0, unroll=2)


def _combine_rows(idx, ye, *, seq_len):
    n_exp, rows, d = ye.shape
    tpr = _tiles_per_row(d)
    cap = CAPACITY_FACTOR * seq_len // N_EXPERTS
    n_seq = rows // cap
    return pl.pallas_call(
        functools.partial(_combine_rows_kernel, cap=cap, n_exp=n_exp),
        grid_spec=pltpu.PrefetchScalarGridSpec(
            num_scalar_prefetch=1,
            grid=(n_seq, n_exp // ROW_LOOP_EXPERTS),
            in_specs=[pl.BlockSpec((ROW_LOOP_EXPERTS, cap, d), lambda s, e, i: (e, s, 0))],
            out_specs=pl.BlockSpec((seq_len * tpr, LANES), lambda s, e, i: (s, 0)),
            scratch_shapes=[pltpu.VMEM((cap * tpr, LANES), F32)],
        ),
        out_shape=jax.ShapeDtypeStruct((n_seq * seq_len * tpr, LANES), F32),
        compiler_params=_params(2, 56),
        name="combine_rows",
    )(idx, ye)


def _head_norm(x, gain):
    return x * lax.rsqrt(jnp.mean(x * x, axis=-1, keepdims=True) + EPS) * gain


def _qkv_kernel(*refs, rope, has_prev, mod_row, n_heads, n_kv):
    refs = list(refs)
    x_ref = refs.pop(0)
    prev = None
    if has_prev:
        acc_ref, pmod_ref = refs.pop(0), refs.pop(0)
    mod_ref, gain_ref, w_ref, qg_ref, kg_ref = refs[:5]
    refs = refs[5:]
    if rope:
        cos_ref, sin_ref = refs.pop(0), refs.pop(0)
    if has_prev:
        prev = (acc_ref, pmod_ref, refs.pop(0))
    q_ref, k_ref, v_ref = refs[:3]
    d = x_ref.shape[1]
    hd = HEAD_DIM
    m = mod_row(pl.program_id(0), x_ref.shape[0])
    x = _residual_stream(x_ref, prev, m)
    shift = mod_ref[pl.ds(m, 1), pl.ds(0, d)]
    scale = mod_ref[pl.ds(m, 1), pl.ds(d, d)]
    h = _modulated_norm(x, gain_ref[...], shift, scale).astype(BF16)
    qg = qg_ref[...]
    kg = kg_ref[...]
    q_scale = math.log2(math.e) * hd ** -0.5
    half = x_ref.shape[0] // 2
    for r0 in (0, half):
        rows = slice(r0, r0 + half)
        qkv = jnp.dot(h[rows], w_ref[...], preferred_element_type=F32)
        if rope:
            cos = cos_ref[rows, :]
            sin = sin_ref[rows, :]

            def rot(xh, cos=cos, sin=sin):
                return xh * cos + pltpu.roll(xh, hd // 2, 1) * sin
        else:
            rot = lambda xh: xh

        for i in range(n_heads):
            qh = rot(_head_norm(qkv[:, i * hd:(i + 1) * hd], qg)) * q_scale
            q_ref[rows, i * hd:(i + 1) * hd] = qh.astype(BF16)
        for i in range(n_kv):
            c0 = (n_heads + i) * hd
            kh = _head_norm(qkv[:, c0:c0 + hd], kg)
            if not rope:
                refs[3][rows, i * hd:(i + 1) * hd] = kh
            k_ref[rows, i * hd:(i + 1) * hd] = rot(kh).astype(BF16)
        v = qkv[:, (n_heads + n_kv) * hd:]
        v_ref[rows, :] = v.astype(BF16)
        if not rope:
            refs[4][rows, :] = v


def _qkv(x, prev, mod, gain, w_qkv, q_gain, k_gain, tables, *, seq_len, mod_row, n_heads, n_kv):
    n, d = x.shape
    hd = HEAD_DIM
    row = lambda i: (i, 0)
    fixed = lambda i: (0, 0)
    rope = tables is not None
    rb = QKV_ROW_BLOCK
    in_specs = [pl.BlockSpec((rb, d), row)]
    args = [x]
    if prev is not None:
        in_specs += [_rows_spec(prev[0], d, rb), pl.BlockSpec(prev[1].shape, fixed)]
        args += list(prev)
    in_specs += [
        pl.BlockSpec(mod.shape, fixed),
        pl.BlockSpec((1, d), fixed),
        pl.BlockSpec(w_qkv.shape, fixed),
        pl.BlockSpec((1, hd), fixed),
        pl.BlockSpec((1, hd), fixed),
    ]
    args += [mod, gain, w_qkv, q_gain, k_gain]
    if rope:
        per_seq = seq_len // rb
        in_specs += [pl.BlockSpec((rb, hd), lambda i: (i % per_seq, 0))] * 2
        args += list(tables)
    out_specs, out_shape = [], []
    if prev is not None:
        out_specs.append(pl.BlockSpec((rb, d), row))
        out_shape.append(jax.ShapeDtypeStruct((n, d), F32))
    out_specs += [pl.BlockSpec((rb, n_heads * hd), row),
                  pl.BlockSpec((rb, n_kv * hd), row),
                  pl.BlockSpec((rb, n_kv * hd), row)]
    out_shape += [jax.ShapeDtypeStruct((n, n_heads * hd), BF16),
                  jax.ShapeDtypeStruct((n, n_kv * hd), BF16),
                  jax.ShapeDtypeStruct((n, n_kv * hd), BF16)]
    if not rope:
        out_specs += [pl.BlockSpec((rb, n_kv * hd), row)] * 2
        out_shape += [jax.ShapeDtypeStruct((n, n_kv * hd), F32)] * 2
    outs = pl.pallas_call(
        functools.partial(_qkv_kernel, rope=rope, has_prev=prev is not None, mod_row=mod_row,
                          n_heads=n_heads, n_kv=n_kv),
        grid=(n // rb,),
        in_specs=in_specs,
        out_specs=out_specs,
        out_shape=out_shape,
        compiler_params=_params(1, 48),
        name="qkv_rope" if rope else "qkv",
    )(*args)
    return list(outs) if prev is not None else [x] + list(outs)


def _attn_kernel(*refs, group, chunk, has_cache):
    if has_cache:
        q_ref, k_ref, v_ref, kc_ref, vc_ref, o_ref = refs
    else:
        q_ref, k_ref, v_ref, o_ref = refs
    hd = HEAD_DIM
    qb = q_ref.shape[0]
    rows = group * qb
    q = jnp.concatenate([q_ref[:, g * hd:(g + 1) * hd] for g in range(group)], axis=0)
    sources = [(k_ref, v_ref, s0, chunk) for s0 in range(0, k_ref.shape[0], chunk)]
    if has_cache:
        sources.append((kc_ref.at[0], vc_ref.at[0], 0, kc_ref.shape[1]))
    m = jnp.full((rows, 1), -jnp.inf, F32)
    acc = jnp.zeros((rows, 2 * hd), F32)
    for kr, vr, s0, size in sources:
        s = lax.dot_general(q, kr[s0:s0 + size, :], (((1,), (1,)), ((), ())),
                            preferred_element_type=F32)
        m_new = jnp.maximum(m, jnp.max(s, axis=-1, keepdims=True))
        p = jnp.exp2(s - m_new).astype(BF16)
        v_ones = jnp.concatenate([vr[s0:s0 + size, :], jnp.ones((size, hd), BF16)], axis=1)
        acc = jnp.exp2(m - m_new) * acc + jnp.dot(p, v_ones, preferred_element_type=F32)
        m = m_new
    o = acc[:, :hd] / acc[:, hd:]
    o_ref[...] = jnp.concatenate([o[g * qb:(g + 1) * qb] for g in range(group)], axis=1).astype(BF16)


def _attention(q, k, v, cache, *, seq_len, qb, chunk, n_heads, n_kv):
    n = q.shape[0]
    hd = HEAD_DIM
    group = n_heads // n_kv
    nq = seq_len // qb
    in_specs = [
        pl.BlockSpec((qb, group * hd), lambda b, h, i: (b * nq + i, h)),
        pl.BlockSpec((seq_len, hd), lambda b, h, i: (b, h)),
        pl.BlockSpec((seq_len, hd), lambda b, h, i: (b, h)),
    ]
    args = [q, k, v]
    if cache is not None:
        past = cache[0].shape[1]
        in_specs += [pl.BlockSpec((1, past, hd), lambda b, h, i: (b, 0, h))] * 2
        args += list(cache)
    return pl.pallas_call(
        functools.partial(_attn_kernel, group=group, chunk=chunk, has_cache=cache is not None),
        grid=(n // seq_len, n_kv, nq),
        in_specs=in_specs,
        out_specs=pl.BlockSpec((qb, group * hd), lambda b, h, i: (b * nq + i, h)),
        out_shape=jax.ShapeDtypeStruct((n, n_heads * hd), BF16),
        compiler_params=_params(3, 48),
        name=f"attention_{seq_len}",
    )(*args)


def _final_norm_kernel(x_ref, acc_ref, pmod_ref, gain_ref, o_ref, *, mod_row):
    n_rows, d = x_ref.shape
    m = mod_row(pl.program_id(0), x_ref.shape[0])
    x = x_ref[...] + pmod_ref[pl.ds(m, 1), pl.ds(5 * d, d)] * _load_rows(acc_ref, n_rows, d)
    o_ref[...] = (x * lax.rsqrt(jnp.mean(x * x, axis=-1, keepdims=True) + EPS)) * gain_ref[...]


def _final_norm(x, acc, pmod, gain, mod_row):
    n, d = x.shape
    row = lambda i: (i, 0)
    fixed = lambda i: (0, 0)
    return pl.pallas_call(
        functools.partial(_final_norm_kernel, mod_row=mod_row),
        grid=(n // ROW_BLOCK,),
        in_specs=[pl.BlockSpec((ROW_BLOCK, d), row), _rows_spec(acc, d, ROW_BLOCK),
                  pl.BlockSpec(pmod.shape, fixed), pl.BlockSpec((1, d), fixed)],
        out_specs=pl.BlockSpec((ROW_BLOCK, d), row),
        out_shape=jax.ShapeDtypeStruct((n, d), F32),
        compiler_params=_params(1, 32),
        name="final_norm",
    )(x, acc, pmod, gain)


def _rope_head_order(x):
    q = HEAD_DIM // 4
    row1, row2, col1, col2 = (x[..., i * q:(i + 1) * q] for i in range(4))
    return jnp.concatenate([row1, col1, row2, col2], axis=-1)


def _rope_tables(seq_len):
    axis = HEAD_DIM // 2
    t = np.arange(seq_len)
    inv = ROPE_BASE ** (-np.arange(axis // 2, dtype=np.float64) * 2.0 / axis)
    ang = np.concatenate([(t // GRID_W)[:, None] * inv, (t % GRID_W)[:, None] * inv], axis=1)
    c, s = np.cos(ang), np.sin(ang)
    return (jnp.asarray(np.concatenate([c, c], axis=1), F32), jnp.asarray(np.concatenate([-s, s], axis=1), F32))


def kernel(x_prompt, x_sample, state_lru, cache_k, cache_v, c, c_ctx, w_mod, b_mod, norm_gain, final_gain,
           w_lru_in, lru_conv_w, lru_conv_b, lru_wa, lru_ba, lru_wx, lru_bx, lru_lambda, w_lru_out,
           w_qkv, q_norm, k_norm, w_attn_out, w_router, w_exp_gate, w_exp_up, w_exp_down):
    batch, seq, d = x_prompt.shape
    dec_batch, dec_seq, _ = x_sample.shape
    depth = w_mod.shape[0]
    n_kv = cache_k.shape[3]
    n_heads = w_attn_out.shape[1] // HEAD_DIM
    heads = dict(n_heads=n_heads, n_kv=n_kv)

    cond = jnp.zeros((SUBLANES, d), F32).at[0].set(c_ctx).at[1:1 + dec_batch].set(c)
    mod_all = _mod_vectors(cond, w_mod, b_mod)

    xs = [x_prompt.reshape(batch * seq, d), x_sample.reshape(dec_batch * dec_seq, d)]
    lens = [seq, dec_seq]
    mod_rows = [lambda i, rows: 0, lambda i, rows: 1 + (i * rows) // dec_seq]
    prevs = [None, None]

    fg = final_gain.reshape(1, d)
    wr_hi, wr_lo = _split_bf16(jnp.swapaxes(w_router, 1, 2))
    new_lru, new_k, new_v = [], [], []
    for l in range(depth):
        mod = mod_all[l]
        gain_a = norm_gain[l, 0].reshape(1, d)
        gain_c = norm_gain[l, 1].reshape(1, d)
        mixed = []
        if l % 2 == 0:
            li = l // 2
            w_in = w_lru_in[li].astype(BF16)
            wg = (0.5 * jnp.concatenate([lru_wa[li, 0], lru_wx[li, 0], lru_wa[li, 1], lru_wx[li, 1]],
                                        axis=2)).astype(BF16)
            bias4 = 0.5 * jnp.stack([lru_ba[li, 0], lru_bx[li, 0], lru_ba[li, 1], lru_bx[li, 1]])
            h0s = [jnp.zeros((batch, 2, w_in.shape[1] // 2), F32), state_lru[:, li]]
            for gi in range(2):
                xs[gi], gate, xr = _lru_in(xs[gi], prevs[gi], mod, gain_a, w_in, mod_rows[gi])
                y, fin = _lru_core(xr, gate, lru_conv_w[li], lru_conv_b[li].reshape(1, -1), wg, bias4,
                                   lru_lambda[li], h0s[gi], seq_len=lens[gi])
                mixed.append(y)
                if gi == 0:
                    new_lru.append(fin.astype(x_prompt.dtype))
            w_out = w_lru_out[li].astype(BF16)
        else:
            ai = l // 2
            w = w_qkv[ai].astype(BF16)
            qg = q_norm[ai].reshape(1, -1)
            kg = k_norm[ai].reshape(1, -1)
            xs[0], q_p, k_p, v_p, kf, vf = _qkv(xs[0], prevs[0], mod, gain_a, w, qg, kg, None,
                                                seq_len=seq, mod_row=mod_rows[0], **heads)
            n_qk = (n_heads + n_kv) * HEAD_DIM
            w_qk = _rope_head_order(w_qkv[ai][:, :n_qk].reshape(d, n_heads + n_kv, HEAD_DIM)).reshape(d, n_qk)
            w_r = jnp.concatenate([w_qk, w_qkv[ai][:, n_qk:]], axis=1).astype(BF16)
            xs[1], q_s, k_s, v_s = _qkv(xs[1], prevs[1], mod, gain_a, w_r, _rope_head_order(qg),
                                        _rope_head_order(kg), _rope_tables(dec_seq),
                                        seq_len=dec_seq, mod_row=mod_rows[1], **heads)
            new_k.append(kf.reshape(batch, seq, n_kv, HEAD_DIM))
            new_v.append(vf.reshape(batch, seq, n_kv, HEAD_DIM))
            past = cache_k.shape[2]
            kc = _rope_head_order(cache_k[:, ai]).reshape(dec_batch, past, n_kv * HEAD_DIM).astype(BF16)
            vc = cache_v[:, ai].reshape(dec_batch, past, n_kv * HEAD_DIM).astype(BF16)
            mixed.append(_attention(q_p, k_p, v_p, None, seq_len=seq, qb=seq, chunk=seq, **heads))
            mixed.append(_attention(q_s, k_s, v_s, (kc, vc), seq_len=dec_seq, qb=512, chunk=256, **heads))
            w_out = w_attn_out[ai].astype(BF16)

        x1_p, h2_p, lg_p = _mix_out(mixed[0], xs[0], mod, gain_c, w_out, wr_hi, wr_lo, l, mod_rows[0], False)
        x1_s, h2_s, lg_s = _mix_out(mixed[1], xs[1], mod, gain_c, w_out, wr_hi, wr_lo, l, mod_rows[1], True)
        idx_p, g_p, idx_s, g_s = _route(lg_p, lg_s, seq, dec_seq)
        xe_p = _gather_onehot(idx_p[:, None, :], h2_p, seq_len=seq)
        off_s = idx_s * _tiles_per_row(d)
        xe_s = _gather_rows(off_s, h2_s, d, seq_len=dec_seq)
        e_n = N_EXPERTS
        zero_rows = ((0, 0), (0, SUBLANES - 1), (0, 0))
        gc_p = jnp.pad(g_p.reshape(batch, e_n, -1).transpose(1, 0, 2).reshape(e_n, 1, -1), zero_rows)
        gc_s = jnp.pad(g_s.reshape(dec_batch, e_n, -1).transpose(1, 0, 2).reshape(e_n, 1, -1), zero_rows)
        ye_p, ye_s = _ffn(xe_p, xe_s, gc_p, gc_s, w_exp_gate, w_exp_up, w_exp_down, l)
        last = l == depth - 1
        x_p = _combine_onehot(idx_p[:, None, :], ye_p, x1_p, mod, fg if last else None, seq_len=seq,
                              mod_row=mod_rows[0])
        acc_s = _combine_rows(off_s, ye_s, seq_len=dec_seq)
        xs = [x_p, x1_s]
        prevs = [None, (acc_s, mod)]

    y_prompt = xs[0].reshape(batch, seq, d)
    y_sample = _final_norm(xs[1], *prevs[1], fg, mod_rows[1]).reshape(dec_batch, dec_seq, d)
    return (y_prompt, y_sample, jnp.stack(new_lru, axis=1), jnp.stack(new_k, axis=1), jnp.stack(new_v, axis=1))
```

```python
import functools
import math

import jax
import jax.numpy as jnp
import numpy as np
from jax import lax
from jax.experimental import pallas as pl
from jax.experimental.pallas import tpu as pltpu

F32 = jnp.float32
BF16 = jnp.bfloat16
I32 = jnp.int32

LANES = 128
SUBLANES = 8
MIB = 1024 * 1024
F32_TINY = float(np.finfo(np.float32).tiny)

RG_C = 8.0
CONV_W = 4
CONV_PAD_L = 2
ROPE_BASE = 10000.0
GRID_W = 64
EPS = 1e-6
N_EXPERTS = 16
CAPACITY_FACTOR = 2
HEAD_DIM = 128
LRU_BLOCKS = 8
LRU_MAX_STEP_BLOCKS = 8
LRU_CORE_VMEM_MIB = 44

ROW_BLOCK = 1024
QKV_ROW_BLOCK = 512


def _params(n_axes, vmem_mib):
    return pltpu.CompilerParams(
        dimension_semantics=("arbitrary",) * n_axes, vmem_limit_bytes=vmem_mib * MIB)


def _split_bf16(x):
    hi = x.astype(BF16)
    lo = (x - hi.astype(F32)).astype(BF16)
    return hi, lo


def _modulated_norm(x, gain, shift, scale):
    y = x * lax.rsqrt(jnp.mean(x * x, axis=-1, keepdims=True) + EPS)
    return (y * gain) * (1.0 + scale) + shift


def _tiles_per_row(d):
    return d // LANES


def _load_rows(ref, n_rows, d):
    if ref.shape == (n_rows, d):
        return ref[...]
    tpr = _tiles_per_row(d)
    return jnp.concatenate([ref[pl.ds(s, n_rows, stride=tpr), :] for s in range(tpr)], axis=1)


def _store_rows(ref, x, r0):
    n_rows, d = x.shape
    if ref.shape[1] == d:
        ref[r0:r0 + n_rows, :] = x.astype(ref.dtype)
        return
    tpr = _tiles_per_row(d)
    for s in range(tpr):
        ref[pl.ds(r0 * tpr + s, n_rows, stride=tpr), :] = x[:, s * LANES:(s + 1) * LANES]


def _rows_spec(arr, d, rows):
    if arr.shape[1] == d:
        return pl.BlockSpec((rows, d), lambda i: (i, 0))
    return pl.BlockSpec((rows * _tiles_per_row(d), LANES), lambda i: (i, 0))


def _residual_stream(x_ref, prev, m):
    if prev is None:
        return x_ref[...]
    acc_ref, pmod_ref, xo_ref = prev
    n_rows, d = x_ref.shape
    x = x_ref[...] + pmod_ref[pl.ds(m, 1), pl.ds(5 * d, d)] * _load_rows(acc_ref, n_rows, d)
    xo_ref[...] = x
    return x


def _mod_kernel(cond_ref, w_ref, b_ref, o_ref):
    c = cond_ref[...]
    a_hi, a_lo = _split_bf16(c * jax.nn.sigmoid(c))
    w_hi, w_lo = _split_bf16(w_ref[0])
    dot = functools.partial(jnp.dot, preferred_element_type=F32)
    o_ref[0] = dot(a_hi, w_hi) + (dot(a_lo, w_hi) + dot(a_hi, w_lo)) + b_ref[0]


def _mod_vectors(cond8, w_mod, b_mod):
    depth, d, d6 = w_mod.shape
    nb = 1536
    return pl.pallas_call(
        _mod_kernel,
        grid=(depth, d6 // nb),
        in_specs=[
            pl.BlockSpec((SUBLANES, d), lambda l, j: (0, 0)),
            pl.BlockSpec((1, d, nb), lambda l, j: (l, 0, j)),
            pl.BlockSpec((1, 1, nb), lambda l, j: (l, 0, j)),
        ],
        out_specs=pl.BlockSpec((1, SUBLANES, nb), lambda l, j: (l, 0, j)),
        out_shape=jax.ShapeDtypeStruct((depth, SUBLANES, d6), F32),
        compiler_params=_params(2, 40),
        name="adaln_mod",
    )(cond8, w_mod, b_mod.reshape(depth, 1, d6))


def _lru_in_kernel(*refs, mod_row, has_prev):
    if has_prev:
        x_ref, acc_ref, pmod_ref, mod_ref, gain_ref, w_ref, xo_ref, gate_ref, xr_ref = refs
        prev = (acc_ref, pmod_ref, xo_ref)
    else:
        x_ref, mod_ref, gain_ref, w_ref, gate_ref, xr_ref = refs
        prev = None
    d = x_ref.shape[1]
    m = mod_row(pl.program_id(0), x_ref.shape[0])
    x = _residual_stream(x_ref, prev, m)
    shift = mod_ref[pl.ds(m, 1), pl.ds(0, d)]
    scale = mod_ref[pl.ds(m, 1), pl.ds(d, d)]
    h = _modulated_norm(x, gain_ref[...], shift, scale).astype(BF16)
    u = jnp.dot(h, w_ref[...], preferred_element_type=F32)
    w = gate_ref.shape[1]
    gate_ref[...] = u[:, :w]
    xr_ref[...] = u[:, w:]


def _lru_in(x, prev, mod, gain, w_in, mod_row):
    n, d = x.shape
    w2 = w_in.shape[1]
    w = w2 // 2
    row = lambda i: (i, 0)
    fixed = lambda i: (0, 0)
    in_specs = [pl.BlockSpec((ROW_BLOCK, d), row)]
    args = [x]
    out_specs = [pl.BlockSpec((ROW_BLOCK, w), row), pl.BlockSpec((ROW_BLOCK, w), row)]
    out_shape = [jax.ShapeDtypeStruct((n, w), F32), jax.ShapeDtypeStruct((n, w), F32)]
    if prev is not None:
        in_specs += [_rows_spec(prev[0], d, ROW_BLOCK), pl.BlockSpec(prev[1].shape, fixed)]
        args += list(prev)
        out_specs = [pl.BlockSpec((ROW_BLOCK, d), row)] + out_specs
        out_shape = [jax.ShapeDtypeStruct((n, d), F32)] + out_shape
    in_specs += [pl.BlockSpec(mod.shape, fixed), pl.BlockSpec((1, d), fixed), pl.BlockSpec((d, w2), fixed)]
    args += [mod, gain, w_in]
    outs = pl.pallas_call(
        functools.partial(_lru_in_kernel, mod_row=mod_row, has_prev=prev is not None),
        grid=(n // ROW_BLOCK,),
        in_specs=in_specs,
        out_specs=out_specs,
        out_shape=out_shape,
        compiler_params=_params(1, 48),
        name="lru_in",
    )(*args)
    return outs if prev is not None else [x] + list(outs)


def _lru_core_kernel(xr_ref, gate_ref, cw_ref, cb_ref, wg_ref, bias_ref, lam_ref, h0_ref,
                     y_ref, fin_ref, pad, a_f, b_f, a_b, b_b):
    t_len, lb = xr_ref.shape
    ch = 256
    halo = SUBLANES
    seg = t_len // SUBLANES
    piece = min(ch, seg)
    seg_shift = seg.bit_length() - 1

    def seg_rows(t0):
        r = lax.shift_right_logical(t0, seg_shift)
        return pl.ds((t0 - r * seg) * SUBLANES + r, piece, stride=SUBLANES)

    zero_rows = jnp.zeros((halo, lb), F32)
    pad[0:halo, :] = zero_rows
    pad[t_len + halo:t_len + 2 * halo, :] = zero_rows

    def copy_in(c, carry):
        r = pl.multiple_of(c * ch, ch)
        pad[pl.ds(r + halo, ch), :] = xr_ref[pl.ds(r, ch), :]
        return carry

    lax.fori_loop(0, t_len // ch, copy_in, 0)

    neg_lam = -lam_ref[...]
    softplus = jnp.maximum(neg_lam, 0.0) + jnp.log1p(jnp.exp(-jnp.abs(neg_lam)))
    c_nla = (0.5 * RG_C) * softplus
    c_exp2 = (-0.5 * RG_C * math.log2(math.e)) * softplus
    cw = cw_ref[...]
    cbias = cb_ref[...]
    half_bias = bias_ref[...]
    half_wg = wg_ref[...]
    bw = half_wg.shape[1]
    n_slab = lb // LANES

    def gates(c, carry):
        r = pl.multiple_of(c * ch, ch)
        blk = pad[pl.ds(r, ch + 2 * halo), :]
        xc = cbias
        for k in range(CONV_W):
            o = halo - CONV_PAD_L + k
            xc = xc + blk[o:o + ch] * cw[k:k + 1]
        xc_bf = xc.astype(BF16)
        g = [jnp.dot(xc_bf[:, j * bw:(j + 1) * bw], half_wg[j], preferred_element_type=F32)
             for j in range(lb // bw)]
        pre = lambda k: jnp.concatenate([gj[:, k * bw:(k + 1) * bw] for gj in g], axis=1) + half_bias[k:k + 1]
        half_xc = 0.5 * xc
        for d, (a_ref, b_ref) in enumerate(((a_f, b_f), (a_b, b_b))):
            u = jnp.tanh(pre(2 * d)) + 1.0
            t_i = jnp.tanh(pre(2 * d + 1))
            a = jnp.exp2(u * c_exp2[d:d + 1])
            q = jnp.tanh(u * c_nla[d:d + 1]) * (1.0 + a * a)
            b = (q * lax.rsqrt(jnp.maximum(q, F32_TINY))) * (t_i * half_xc + half_xc)
            for p in range(ch // piece):
                rows = seg_rows(r + p * piece)
                for j in range(n_slab):
                    a_ref[j, rows, :] = a[p * piece:(p + 1) * piece, j * LANES:(j + 1) * LANES]
                    b_ref[j, rows, :] = b[p * piece:(p + 1) * piece, j * LANES:(j + 1) * LANES]
        return carry

    lax.fori_loop(0, t_len // ch, gates, 0)

    def scan(i, carry):
        rows = (pl.ds(pl.multiple_of(i * SUBLANES, SUBLANES), SUBLANES),
                pl.ds(pl.multiple_of((seg - 1 - i) * SUBLANES, SUBLANES), SUBLANES))
        out = []
        for k, (h, prod) in enumerate(carry):
            d, j = divmod(k, n_slab)
            a_ref, b_ref = ((a_f, b_f), (a_b, b_b))[d]
            a = a_ref[j, rows[d], :]
            h = a * h + b_ref[j, rows[d], :]
            prod = a * prod
            b_ref[j, rows[d], :] = h
            a_ref[j, rows[d], :] = prod
            out.append((h, prod))
        return tuple(out)

    start = (jnp.zeros((SUBLANES, LANES), F32), jnp.ones((SUBLANES, LANES), F32))
    ends = lax.fori_loop(0, seg, scan, (start,) * (2 * n_slab), unroll=8)

    enter = []
    for k, (h, prod) in enumerate(ends):
        d, j = divmod(k, n_slab)
        states = [h0_ref[0, d:d + 1, j * LANES:(j + 1) * LANES]]
        for r in (range(SUBLANES) if d == 0 else reversed(range(SUBLANES))):
            states.append(h[r:r + 1] + prod[r:r + 1] * states[-1])
        fin_ref[0, d:d + 1, j * LANES:(j + 1) * LANES] = states[SUBLANES]
        order = states[:SUBLANES] if d == 0 else states[:SUBLANES][::-1]
        enter.append(jnp.concatenate(order, axis=0))

    def fix(i, carry):
        rows = pl.ds(pl.multiple_of(i * SUBLANES, SUBLANES), SUBLANES)
        for j in range(n_slab):
            b_f[j, rows, :] = ((b_f[j, rows, :] + a_f[j, rows, :] * enter[j])
                               + (b_b[j, rows, :] + a_b[j, rows, :] * enter[n_slab + j]))
        return carry

    lax.fori_loop(0, seg, fix, 0, unroll=8)

    def emit(c, carry):
        r = pl.multiple_of(c * ch, ch)
        gt = gate_ref[pl.ds(r, ch), :]
        cdf = 0.5 * (1.0 + jnp.tanh(math.sqrt(2.0 / math.pi) * (gt + 0.044715 * (gt * gt * gt))))
        y = gt * cdf
        for p in range(ch // piece):
            rows = seg_rows(r + p * piece)
            total = jnp.concatenate([b_f[j, rows, :] for j in range(n_slab)], axis=1)
            y_ref[pl.ds(r + p * piece, piece), :] = (total * y[p * piece:(p + 1) * piece]).astype(BF16)
        return carry

    lax.fori_loop(0, t_len // ch, emit, 0)


def _lru_core(xr, gate, conv_w, conv_b, wg, bias4, lam, h0, *, seq_len):
    n, w = xr.shape
    n_seq = n // seq_len
    bw = w // LRU_BLOCKS
    block_bytes = seq_len * bw * (5 * 4 + 2 * 2 * 4 + 2 * 2)
    step_blocks = LRU_MAX_STEP_BLOCKS
    while step_blocks * block_bytes > LRU_CORE_VMEM_MIB * MIB:
        step_blocks //= 2
    lb = step_blocks * bw
    tok = lambda s, c: (s, c)
    chan = lambda s, c: (0, c)
    return pl.pallas_call(
        _lru_core_kernel,
        grid=(n_seq, LRU_BLOCKS // step_blocks),
        in_specs=[
            pl.BlockSpec((seq_len, lb), tok),
            pl.BlockSpec((seq_len, lb), tok),
            pl.BlockSpec((CONV_W, lb), chan),
            pl.BlockSpec((1, lb), chan),
            pl.BlockSpec((step_blocks, bw, 4 * bw), lambda s, c: (c, 0, 0)),
            pl.BlockSpec((4, lb), chan),
            pl.BlockSpec((2, lb), chan),
            pl.BlockSpec((1, 2, lb), lambda s, c: (s, 0, c)),
        ],
        out_specs=[pl.BlockSpec((seq_len, lb), tok),
                   pl.BlockSpec((1, 2, lb), lambda s, c: (s, 0, c))],
        out_shape=[jax.ShapeDtypeStruct((n, w), BF16), jax.ShapeDtypeStruct((n_seq, 2, w), F32)],
        scratch_shapes=[pltpu.VMEM((seq_len + 2 * SUBLANES, lb), F32)]
        + [pltpu.VMEM((lb // LANES, seq_len, LANES), F32)] * 4,
        compiler_params=_params(2, LRU_CORE_VMEM_MIB + 4),
        name=f"lru_core_{seq_len}",
    )(xr, gate, conv_w, conv_b, wg, bias4, lam, h0)


def _mix_out_kernel(a_ref, x_ref, mod_ref, gain_ref, w_ref, wrh_ref, wrl_ref,
                    x1_ref, h2_ref, lg_ref, *, mod_row):
    d = x_ref.shape[1]
    m = mod_row(pl.program_id(0), x_ref.shape[0])
    g_mix = mod_ref[pl.ds(m, 1), pl.ds(2 * d, d)]
    shift = mod_ref[pl.ds(m, 1), pl.ds(3 * d, d)]
    scale = mod_ref[pl.ds(m, 1), pl.ds(4 * d, d)]
    op = jnp.dot(a_ref[...], w_ref[...], preferred_element_type=F32)
    x1 = x_ref[...] + g_mix * op
    x1_ref[...] = x1
    h2 = _modulated_norm(x1, gain_ref[...], shift, scale)
    h_hi, h_lo = _split_bf16(h2)
    _store_rows(h2_ref, h2, 0)
    nt = functools.partial(lax.dot_general, dimension_numbers=(((1,), (1,)), ((), ())),
                           preferred_element_type=F32)
    wr_hi = wrh_ref[0]
    e = wr_hi.shape[0]
    both = nt(jnp.concatenate([wr_hi, wrl_ref[0]], axis=0), h_hi)
    lg_ref[...] = both[:e] + (both[e:] + nt(wr_hi, h_lo))


def _mix_out(a, x, mod, gain, w_out, wr_hi, wr_lo, layer, mod_row, token_tiled):
    n, d = x.shape
    e = wr_hi.shape[1]
    row = lambda i: (i, 0)
    fixed = lambda i: (0, 0)
    router = pl.BlockSpec((1,) + wr_hi.shape[1:], lambda i: (layer, 0, 0))
    tpr = _tiles_per_row(d)
    if token_tiled:
        h2_spec = pl.BlockSpec((ROW_BLOCK * tpr, LANES), row)
        h2_shape = jax.ShapeDtypeStruct((n * tpr, LANES), F32)
    else:
        h2_spec = pl.BlockSpec((ROW_BLOCK, d), row)
        h2_shape = jax.ShapeDtypeStruct((n, d), BF16)
    return pl.pallas_call(
        functools.partial(_mix_out_kernel, mod_row=mod_row),
        grid=(n // ROW_BLOCK,),
        in_specs=[
            pl.BlockSpec((ROW_BLOCK, a.shape[1]), row),
            pl.BlockSpec((ROW_BLOCK, d), row),
            pl.BlockSpec(mod.shape, fixed),
            pl.BlockSpec((1, d), fixed),
            pl.BlockSpec(w_out.shape, fixed),
            router,
            router,
        ],
        out_specs=[pl.BlockSpec((ROW_BLOCK, d), row), h2_spec,
                   pl.BlockSpec((e, ROW_BLOCK), lambda i: (0, i))],
        out_shape=[jax.ShapeDtypeStruct((n, d), F32), h2_shape, jax.ShapeDtypeStruct((e, n), F32)],
        compiler_params=_params(1, 48),
        name="mix_out",
    )(a, x, mod, gain, w_out, wr_hi, wr_lo)


def _lane_sum(tiles):
    acc = tiles[0]
    for t in tiles[1:]:
        acc = acc + t
    return jnp.sum(acc, axis=1, keepdims=True)


def _exclusive_cumsum(flags, upper):
    out = []
    off = jnp.zeros((flags[0].shape[0], 1), F32)
    for f in flags:
        out.append(jnp.dot(f.astype(BF16), upper, preferred_element_type=F32) + off)
        off = off + jnp.sum(f, axis=1, keepdims=True)
    return out


def _route_group(lg_ref, n_seq, t_len, idx_ref, g_ref):
    cap = CAPACITY_FACTOR * t_len // N_EXPERTS
    affs = []
    for s in range(n_seq):
        lg = lg_ref[:, s * t_len:(s + 1) * t_len]
        ex = jnp.exp(lg - jnp.max(lg, axis=0, keepdims=True))
        affs.append(ex / jnp.sum(ex, axis=0, keepdims=True))
    aff = jnp.concatenate(affs, axis=0)
    n_rows = aff.shape[0]
    nt = t_len // LANES
    g = [aff[:, j * LANES:(j + 1) * LANES] for j in range(nt)]

    kth_bits = jnp.zeros((n_rows, 1), I32)
    for bit in range(30, -1, -1):
        cand = kth_bits | (1 << bit)
        cand_f = pltpu.bitcast(cand, F32)
        cnt = _lane_sum([jnp.where(t >= cand_f, 1, 0) for t in g])
        kth_bits = jnp.where(cnt >= cap, cand, kth_bits)
    kth = pltpu.bitcast(kth_bits, F32)

    lane = lax.broadcasted_iota(I32, (LANES, LANES), 0)
    upper = jnp.where(lane < lax.broadcasted_iota(I32, (LANES, LANES), 1), 1.0, 0.0).astype(BF16)
    gt = [t > kth for t in g]
    eq = [t == kth for t in g]
    need = (cap - _lane_sum([jnp.where(m, 1, 0) for m in gt])).astype(F32)
    eq_rank = _exclusive_cumsum([jnp.where(m, 1.0, 0.0) for m in eq], upper)
    sel = [jnp.logical_or(gt[j], jnp.logical_and(eq[j], eq_rank[j] < need)) for j in range(nt)]
    pos = _exclusive_cumsum([jnp.where(m, 1.0, 0.0) for m in sel], upper)

    lane_r = lax.broadcasted_iota(I32, (n_rows, LANES), 1)
    d = [jnp.where(sel[j], lane_r + j * LANES - pos[j].astype(I32), -1) for j in range(nt)]
    for k in range(t_len.bit_length() - 1):
        s = 1 << k
        if s < LANES:
            d_rot = [pltpu.roll(x, LANES - s, 1) for x in d]
            g_rot = [pltpu.roll(x, LANES - s, 1) for x in g]
            same = lane_r < LANES - s
            d_in = [jnp.where(same, d_rot[j], d_rot[(j + 1) % nt]) for j in range(nt)]
            g_in = [jnp.where(same, g_rot[j], g_rot[(j + 1) % nt]) for j in range(nt)]
        else:
            q = s // LANES
            d_in = [d[(j + q) % nt] for j in range(nt)]
            g_in = [g[(j + q) % nt] for j in range(nt)]
        new_d, new_g = [], []
        for j in range(nt):
            move = jnp.logical_and(d_in[j] >= 0, ((d_in[j] >> k) & 1) == 1)
            stay = jnp.logical_and(d[j] >= 0, ((d[j] >> k) & 1) == 0)
            new_d.append(jnp.where(move, d_in[j], jnp.where(stay, d[j], -1)))
            new_g.append(jnp.where(move, g_in[j], g[j]))
        d, g = new_d, new_g

    for j in range(idx_ref.shape[1] // LANES):
        idx_ref[:, j * LANES:(j + 1) * LANES] = lane_r + j * LANES + d[j]
        g_ref[:, j * LANES:(j + 1) * LANES] = g[j]


def _route_kernel(lgp_ref, lgs_ref, idx_p_ref, g_p_ref, idx_s_ref, g_s_ref, *, p_len, s_len):
    _route_group(lgp_ref, lgp_ref.shape[1] // p_len, p_len, idx_p_ref, g_p_ref)
    _route_group(lgs_ref, lgs_ref.shape[1] // s_len, s_len, idx_s_ref, g_s_ref)


def _route(lg_p, lg_s, p_len, s_len):
    e = lg_p.shape[0]
    np_seq = lg_p.shape[1] // p_len
    ns_seq = lg_s.shape[1] // s_len
    cap_p = CAPACITY_FACTOR * p_len // N_EXPERTS
    cap_s = CAPACITY_FACTOR * s_len // N_EXPERTS
    wp = max(cap_p, LANES)
    ws = max(cap_s, LANES)
    idx_p, g_p, idx_s, g_s = pl.pallas_call(
        functools.partial(_route_kernel, p_len=p_len, s_len=s_len),
        out_shape=[jax.ShapeDtypeStruct((np_seq * e, wp), I32),
                   jax.ShapeDtypeStruct((np_seq * e, wp), F32),
                   jax.ShapeDtypeStruct((ns_seq * e, ws), I32),
                   jax.ShapeDtypeStruct((ns_seq * e, ws), F32)],
        compiler_params=pltpu.CompilerParams(vmem_limit_bytes=40 * MIB),
        name="route",
    )(lg_p, lg_s)
    return (idx_p[:, :cap_p].reshape(np_seq, e * cap_p), g_p[:, :cap_p].reshape(np_seq, e * cap_p),
            idx_s[:, :cap_s].reshape(-1), g_s[:, :cap_s].reshape(-1))


def _gather_onehot_kernel(idx_ref, h_ref, o_ref):
    t_len = h_ref.shape[0]
    n_exp, cap, _ = o_ref.shape
    idx = idx_ref[0]
    hit = jnp.where(idx == lax.broadcasted_iota(I32, (t_len, idx.shape[1]), 0), 1.0, 0.0).astype(BF16)
    xs = lax.dot_general(hit, h_ref[...], (((0,), (0,)), ((), ())), preferred_element_type=F32).astype(BF16)
    for e in range(n_exp):
        o_ref[e] = xs[e * cap:(e + 1) * cap]


def _gather_onehot(idx_row, h, *, seq_len):
    n, d = h.shape
    n_seq, _, slots = idx_row.shape
    cap = slots // N_EXPERTS
    return pl.pallas_call(
        _gather_onehot_kernel,
        grid=(n_seq,),
        in_specs=[pl.BlockSpec((1, 1, slots), lambda s: (s, 0, 0)),
                  pl.BlockSpec((seq_len, d), lambda s: (s, 0))],
        out_specs=pl.BlockSpec((N_EXPERTS, cap, d), lambda s: (0, s, 0)),
        out_shape=jax.ShapeDtypeStruct((N_EXPERTS, n_seq * cap, d), BF16),
        compiler_params=_params(1, 32),
        name="gather_onehot",
    )(idx_row, h)


def _gather_rows_kernel(idx_ref, h_ref, o_ref, tiles, *, cap, n_exp):
    tpr = _tiles_per_row(o_ref.shape[2])
    for el in range(ROW_LOOP_EXPERTS):
        base = (pl.program_id(0) * n_exp + pl.program_id(1) * ROW_LOOP_EXPERTS + el) * cap

        def body(c, carry, base=base):
            src = pl.multiple_of(idx_ref[base + c], tpr)
            tiles[pl.ds(pl.multiple_of(c * tpr, tpr), tpr), :] = h_ref[pl.ds(src, tpr), :]
            return carry

        lax.fori_loop(0, cap, body, 0, unroll=16)
        o_ref[el] = tiles[...].reshape(cap, tpr, LANES).reshape(cap, tpr * LANES).astype(BF16)


def _gather_rows(idx, h, d, *, seq_len):
    tpr = _tiles_per_row(d)
    n_seq = h.shape[0] // (seq_len * tpr)
    cap = CAPACITY_FACTOR * seq_len // N_EXPERTS
    return pl.pallas_call(
        functools.partial(_gather_rows_kernel, cap=cap, n_exp=N_EXPERTS),
        grid_spec=pltpu.PrefetchScalarGridSpec(
            num_scalar_prefetch=1,
            grid=(n_seq, N_EXPERTS // ROW_LOOP_EXPERTS),
            in_specs=[pl.BlockSpec((seq_len * tpr, LANES), lambda s, e, idx: (s, 0))],
            out_specs=pl.BlockSpec((ROW_LOOP_EXPERTS, cap, d), lambda s, e, idx: (e, s, 0)),
            scratch_shapes=[pltpu.VMEM((cap * tpr, LANES), F32)],
        ),
        out_shape=jax.ShapeDtypeStruct((N_EXPERTS, n_seq * cap, d), BF16),
        compiler_params=_params(2, 48),
        name="gather_rows",
    )(idx, h)


def _ffn_kernel(xp_ref, xs_ref, gp_ref, gs_ref, wg_ref, wu_ref, wd_ref, yp_ref, ys_ref):
    rc = 512
    f = pl.program_id(1)
    last = f == pl.num_programs(1) - 1

    def body(first):
        wg = wg_ref[0, 0].astype(BF16)
        wu = wu_ref[0, 0].astype(BF16)
        wd = wd_ref[0, 0].astype(BF16)
        for x_ref, g_ref, y_ref in ((xp_ref, gp_ref, yp_ref), (xs_ref, gs_ref, ys_ref)):
            for r in range(0, x_ref.shape[1], rc):
                x = x_ref[0, r:r + rc, :]
                hg = jnp.dot(x, wg, preferred_element_type=F32)
                hu = jnp.dot(x, wu, preferred_element_type=F32)
                hid = ((hg * jax.nn.sigmoid(hg)) * hu).astype(BF16)
                g_hi, g_lo = _split_bf16(g_ref[0, :, r:r + rc])
                ones = jnp.ones((g_hi.shape[0], LANES), BF16)
                tn = functools.partial(lax.dot_general, dimension_numbers=(((0,), (0,)), ((), ())),
                                       preferred_element_type=F32)
                weight = jnp.where(last, tn(g_hi, ones) + tn(g_lo, ones), 1.0)
                weight = jnp.concatenate([weight] * (y_ref.shape[2] // LANES), axis=1)
                y = jnp.dot(hid, wd, preferred_element_type=F32)
                if not first:
                    y = y_ref[0, r:r + rc, :] + y
                y_ref[0, r:r + rc, :] = y * weight

    pl.when(f == 0)(functools.partial(body, True))
    pl.when(f != 0)(functools.partial(body, False))


def _ffn(xs_p, xs_s, g_p, g_s, w_gate, w_up, w_down, layer):
    n_exp, rp, d = xs_p.shape
    rs = xs_s.shape[1]
    ff = w_gate.shape[3]
    fc = 1024
    return pl.pallas_call(
        _ffn_kernel,
        grid=(n_exp, ff // fc),
        in_specs=[
            pl.BlockSpec((1, rp, d), lambda e, f: (e, 0, 0)),
            pl.BlockSpec((1, rs, d), lambda e, f: (e, 0, 0)),
            pl.BlockSpec((1, SUBLANES, rp), lambda e, f: (e, 0, 0)),
            pl.BlockSpec((1, SUBLANES, rs), lambda e, f: (e, 0, 0)),
            pl.BlockSpec((1, 1, d, fc), lambda e, f: (layer, e, 0, f)),
            pl.BlockSpec((1, 1, d, fc), lambda e, f: (layer, e, 0, f)),
            pl.BlockSpec((1, 1, fc, d), lambda e, f: (layer, e, f, 0)),
        ],
        out_specs=[pl.BlockSpec((1, rp, d), lambda e, f: (e, 0, 0)),
                   pl.BlockSpec((1, rs, d), lambda e, f: (e, 0, 0))],
        out_shape=[jax.ShapeDtypeStruct((n_exp, rp, d), F32),
                   jax.ShapeDtypeStruct((n_exp, rs, d), F32)],
        compiler_params=_params(2, 58),
        name="expert_ffn",
    )(xs_p, xs_s, g_p, g_s, w_gate, w_up, w_down)


def _combine_onehot_kernel(*refs, mod_row, final):
    if final:
        idx_ref, ye_ref, x_ref, pmod_ref, gain_ref, o_ref = refs
    else:
        idx_ref, ye_ref, x_ref, pmod_ref, o_ref = refs
    t_len, d = o_ref.shape
    n_exp = ye_ref.shape[0]
    idx = idx_ref[0]
    hit = jnp.where(idx == lax.broadcasted_iota(I32, (t_len, idx.shape[1]), 0), 1.0, 0.0).astype(BF16)
    y_hi, y_lo = _split_bf16(jnp.concatenate([ye_ref[e] for e in range(n_exp)], axis=0))
    acc = jnp.dot(hit, y_hi, preferred_element_type=F32) + jnp.dot(hit, y_lo, preferred_element_type=F32)
    x = x_ref[...] + pmod_ref[pl.ds(mod_row(pl.program_id(0), t_len), 1), pl.ds(5 * d, d)] * acc
    if final:
        x = (x * lax.rsqrt(jnp.mean(x * x, axis=-1, keepdims=True) + EPS)) * gain_ref[...]
    o_ref[...] = x


def _combine_onehot(idx_row, ye, x, pmod, final_gain, *, seq_len, mod_row):
    n_exp, rows, d = ye.shape
    n_seq, _, slots = idx_row.shape
    cap = slots // n_exp
    in_specs = [pl.BlockSpec((1, 1, slots), lambda s: (s, 0, 0)),
                pl.BlockSpec((n_exp, cap, d), lambda s: (0, s, 0)),
                pl.BlockSpec((seq_len, d), lambda s: (s, 0)),
                pl.BlockSpec(pmod.shape, lambda s: (0, 0))]
    args = [idx_row, ye, x, pmod]
    if final_gain is not None:
        in_specs.append(pl.BlockSpec((1, d), lambda s: (0, 0)))
        args.append(final_gain)
    return pl.pallas_call(
        functools.partial(_combine_onehot_kernel, mod_row=mod_row, final=final_gain is not None),
        grid=(n_seq,),
        in_specs=in_specs,
        out_specs=pl.BlockSpec((seq_len, d), lambda s: (s, 0)),
        out_shape=jax.ShapeDtypeStruct((n_seq * seq_len, d), F32),
        compiler_params=_params(1, 32),
        name="combine_onehot",
    )(*args)


COMBINE_GROUP = 16
ROW_LOOP_EXPERTS = 2


def _combine_rows_kernel(idx_ref, ye_ref, o_ref, tiles, *, cap, n_exp):
    e = pl.program_id(1)
    d = ye_ref.shape[2]
    tpr = _tiles_per_row(d)
    ch = 2048

    @pl.when(e == 0)
    def _():
        def zero(c, carry):
            o_ref[pl.ds(pl.multiple_of(c * ch, ch), ch), :] = jnp.zeros((ch, LANES), F32)
            return carry
        lax.fori_loop(0, o_ref.shape[0] // ch, zero, 0)

    for el in range(ROW_LOOP_EXPERTS):
        tiles[...] = ye_ref[el].reshape(cap, tpr, LANES).reshape(cap * tpr, LANES)

        base = (pl.program_id(0) * n_exp + e * ROW_LOOP_EXPERTS + el) * cap

        def body(i, carry, base=base):
            c0 = i * COMBINE_GROUP
            dst = [pl.ds(pl.multiple_of(idx_ref[base + c0 + k], tpr), tpr) for k in range(COMBINE_GROUP)]
            vals = [o_ref[dst[k], :] + tiles[pl.ds(pl.multiple_of((c0 + k) * tpr, tpr), tpr), :]
                    for k in range(COMBINE_GROUP)]
            for k in range(COMBINE_GROUP):
                o_ref[dst[k], :] = vals[k]
            return carry

        lax.fori_loop(0, cap // COMBINE_GROUP, body, 0, unroll=2)


def _combine_rows(idx, ye, *, seq_len):
    n_exp, rows, d = ye.shape
    tpr = _tiles_per_row(d)
    cap = CAPACITY_FACTOR * seq_len // N_EXPERTS
    n_seq = rows // cap
    return pl.pallas_call(
        functools.partial(_combine_rows_kernel, cap=cap, n_exp=n_exp),
        grid_spec=pltpu.PrefetchScalarGridSpec(
            num_scalar_prefetch=1,
            grid=(n_seq, n_exp // ROW_LOOP_EXPERTS),
            in_specs=[pl.BlockSpec((ROW_LOOP_EXPERTS, cap, d), lambda s, e, i: (e, s, 0))],
            out_specs=pl.BlockSpec((seq_len * tpr, LANES), lambda s, e, i: (s, 0)),
            scratch_shapes=[pltpu.VMEM((cap * tpr, LANES), F32)],
        ),
        out_shape=jax.ShapeDtypeStruct((n_seq * seq_len * tpr, LANES), F32),
        compiler_params=_params(2, 48),
        name="combine_rows",
    )(idx, ye)


def _head_norm(x, gain):
    return x * lax.rsqrt(jnp.mean(x * x, axis=-1, keepdims=True) + EPS) * gain


def _qkv_kernel(*refs, rope, has_prev, mod_row, n_heads, n_kv):
    refs = list(refs)
    x_ref = refs.pop(0)
    prev = None
    if has_prev:
        acc_ref, pmod_ref = refs.pop(0), refs.pop(0)
    mod_ref, gain_ref, w_ref, qg_ref, kg_ref = refs[:5]
    refs = refs[5:]
    if rope:
        cos_ref, sin_ref = refs.pop(0), refs.pop(0)
    if has_prev:
        prev = (acc_ref, pmod_ref, refs.pop(0))
    q_ref, k_ref, v_ref = refs[:3]
    d = x_ref.shape[1]
    hd = HEAD_DIM
    m = mod_row(pl.program_id(0), x_ref.shape[0])
    x = _residual_stream(x_ref, prev, m)
    shift = mod_ref[pl.ds(m, 1), pl.ds(0, d)]
    scale = mod_ref[pl.ds(m, 1), pl.ds(d, d)]
    h = _modulated_norm(x, gain_ref[...], shift, scale).astype(BF16)
    qg = qg_ref[...]
    kg = kg_ref[...]
    q_scale = math.log2(math.e) * hd ** -0.5
    half = x_ref.shape[0] // 2
    for r0 in (0, half):
        rows = slice(r0, r0 + half)
        qkv = jnp.dot(h[rows], w_ref[...], preferred_element_type=F32)
        if rope:
            cos = cos_ref[rows, :]
            sin = sin_ref[rows, :]

            def rot(xh, cos=cos, sin=sin):
                return xh * cos + pltpu.roll(xh, hd // 2, 1) * sin
        else:
            rot = lambda xh: xh

        for i in range(n_heads):
            qh = rot(_head_norm(qkv[:, i * hd:(i + 1) * hd], qg)) * q_scale
            q_ref[rows, i * hd:(i + 1) * hd] = qh.astype(BF16)
        for i in range(n_kv):
            c0 = (n_heads + i) * hd
            kh = _head_norm(qkv[:, c0:c0 + hd], kg)
            if not rope:
                refs[3][rows, i * hd:(i + 1) * hd] = kh
            k_ref[rows, i * hd:(i + 1) * hd] = rot(kh).astype(BF16)
        v = qkv[:, (n_heads + n_kv) * hd:]
        v_ref[rows, :] = v.astype(BF16)
        if not rope:
            refs[4][rows, :] = v


def _qkv(x, prev, mod, gain, w_qkv, q_gain, k_gain, tables, *, seq_len, mod_row, n_heads, n_kv):
    n, d = x.shape
    hd = HEAD_DIM
    row = lambda i: (i, 0)
    fixed = lambda i: (0, 0)
    rope = tables is not None
    rb = QKV_ROW_BLOCK
    in_specs = [pl.BlockSpec((rb, d), row)]
    args = [x]
    if prev is not None:
        in_specs += [_rows_spec(prev[0], d, rb), pl.BlockSpec(prev[1].shape, fixed)]
        args += list(prev)
    in_specs += [
        pl.BlockSpec(mod.shape, fixed),
        pl.BlockSpec((1, d), fixed),
        pl.BlockSpec(w_qkv.shape, fixed),
        pl.BlockSpec((1, hd), fixed),
        pl.BlockSpec((1, hd), fixed),
    ]
    args += [mod, gain, w_qkv, q_gain, k_gain]
    if rope:
        per_seq = seq_len // rb
        in_specs += [pl.BlockSpec((rb, hd), lambda i: (i % per_seq, 0))] * 2
        args += list(tables)
    out_specs, out_shape = [], []
    if prev is not None:
        out_specs.append(pl.BlockSpec((rb, d), row))
        out_shape.append(jax.ShapeDtypeStruct((n, d), F32))
    out_specs += [pl.BlockSpec((rb, n_heads * hd), row),
                  pl.BlockSpec((rb, n_kv * hd), row),
                  pl.BlockSpec((rb, n_kv * hd), row)]
    out_shape += [jax.ShapeDtypeStruct((n, n_heads * hd), BF16),
                  jax.ShapeDtypeStruct((n, n_kv * hd), BF16),
                  jax.ShapeDtypeStruct((n, n_kv * hd), BF16)]
    if not rope:
        out_specs += [pl.BlockSpec((rb, n_kv * hd), row)] * 2
        out_shape += [jax.ShapeDtypeStruct((n, n_kv * hd), F32)] * 2
    outs = pl.pallas_call(
        functools.partial(_qkv_kernel, rope=rope, has_prev=prev is not None, mod_row=mod_row,
                          n_heads=n_heads, n_kv=n_kv),
        grid=(n // rb,),
        in_specs=in_specs,
        out_specs=out_specs,
        out_shape=out_shape,
        compiler_params=_params(1, 48),
        name="qkv_rope" if rope else "qkv",
    )(*args)
    return list(outs) if prev is not None else [x] + list(outs)


def _attn_kernel(*refs, group, chunk, has_cache):
    if has_cache:
        q_ref, k_ref, v_ref, kc_ref, vc_ref, o_ref = refs
    else:
        q_ref, k_ref, v_ref, o_ref = refs
    hd = HEAD_DIM
    qb = q_ref.shape[0]
    rows = group * qb
    q = jnp.concatenate([q_ref[:, g * hd:(g + 1) * hd] for g in range(group)], axis=0)
    sources = [(k_ref, v_ref, s0, chunk) for s0 in range(0, k_ref.shape[0], chunk)]
    if has_cache:
        sources.append((kc_ref.at[0], vc_ref.at[0], 0, kc_ref.shape[1]))
    m = jnp.full((rows, 1), -jnp.inf, F32)
    acc = jnp.zeros((rows, 2 * hd), F32)
    for kr, vr, s0, size in sources:
        s = lax.dot_general(q, kr[s0:s0 + size, :], (((1,), (1,)), ((), ())),
                            preferred_element_type=F32)
        m_new = jnp.maximum(m, jnp.max(s, axis=-1, keepdims=True))
        p = jnp.exp2(s - m_new).astype(BF16)
        v_ones = jnp.concatenate([vr[s0:s0 + size, :], jnp.ones((size, hd), BF16)], axis=1)
        acc = jnp.exp2(m - m_new) * acc + jnp.dot(p, v_ones, preferred_element_type=F32)
        m = m_new
    o = acc[:, :hd] / acc[:, hd:]
    o_ref[...] = jnp.concatenate([o[g * qb:(g + 1) * qb] for g in range(group)], axis=1).astype(BF16)


def _attention(q, k, v, cache, *, seq_len, qb, chunk, n_heads, n_kv):
    n = q.shape[0]
    hd = HEAD_DIM
    group = n_heads // n_kv
    nq = seq_len // qb
    in_specs = [
        pl.BlockSpec((qb, group * hd), lambda b, h, i: (b * nq + i, h)),
        pl.BlockSpec((seq_len, hd), lambda b, h, i: (b, h)),
        pl.BlockSpec((seq_len, hd), lambda b, h, i: (b, h)),
    ]
    args = [q, k, v]
    if cache is not None:
        past = cache[0].shape[1]
        in_specs += [pl.BlockSpec((1, past, hd), lambda b, h, i: (b, 0, h))] * 2
        args += list(cache)
    return pl.pallas_call(
        functools.partial(_attn_kernel, group=group, chunk=chunk, has_cache=cache is not None),
        grid=(n // seq_len, n_kv, nq),
        in_specs=in_specs,
        out_specs=pl.BlockSpec((qb, group * hd), lambda b, h, i: (b * nq + i, h)),
        out_shape=jax.ShapeDtypeStruct((n, n_heads * hd), BF16),
        compiler_params=_params(3, 48),
        name=f"attention_{seq_len}",
    )(*args)


def _final_norm_kernel(x_ref, acc_ref, pmod_ref, gain_ref, o_ref, *, mod_row):
    n_rows, d = x_ref.shape
    m = mod_row(pl.program_id(0), x_ref.shape[0])
    x = x_ref[...] + pmod_ref[pl.ds(m, 1), pl.ds(5 * d, d)] * _load_rows(acc_ref, n_rows, d)
    o_ref[...] = (x * lax.rsqrt(jnp.mean(x * x, axis=-1, keepdims=True) + EPS)) * gain_ref[...]


def _final_norm(x, acc, pmod, gain, mod_row):
    n, d = x.shape
    row = lambda i: (i, 0)
    fixed = lambda i: (0, 0)
    return pl.pallas_call(
        functools.partial(_final_norm_kernel, mod_row=mod_row),
        grid=(n // ROW_BLOCK,),
        in_specs=[pl.BlockSpec((ROW_BLOCK, d), row), _rows_spec(acc, d, ROW_BLOCK),
                  pl.BlockSpec(pmod.shape, fixed), pl.BlockSpec((1, d), fixed)],
        out_specs=pl.BlockSpec((ROW_BLOCK, d), row),
        out_shape=jax.ShapeDtypeStruct((n, d), F32),
        compiler_params=_params(1, 32),
        name="final_norm",
    )(x, acc, pmod, gain)


def _rope_head_order(x):
    q = HEAD_DIM // 4
    row1, row2, col1, col2 = (x[..., i * q:(i + 1) * q] for i in range(4))
    return jnp.concatenate([row1, col1, row2, col2], axis=-1)


def _rope_tables(seq_len):
    axis = HEAD_DIM // 2
    t = np.arange(seq_len)
    inv = ROPE_BASE ** (-np.arange(axis // 2, dtype=np.float64) * 2.0 / axis)
    ang = np.concatenate([(t // GRID_W)[:, None] * inv, (t % GRID_W)[:, None] * inv], axis=1)
    c, s = np.cos(ang), np.sin(ang)
    return (jnp.asarray(np.concatenate([c, c], axis=1), F32), jnp.asarray(np.concatenate([-s, s], axis=1), F32))


def kernel(x_prompt, x_sample, state_lru, cache_k, cache_v, c, c_ctx, w_mod, b_mod, norm_gain, final_gain,
           w_lru_in, lru_conv_w, lru_conv_b, lru_wa, lru_ba, lru_wx, lru_bx, lru_lambda, w_lru_out,
           w_qkv, q_norm, k_norm, w_attn_out, w_router, w_exp_gate, w_exp_up, w_exp_down):
    batch, seq, d = x_prompt.shape
    dec_batch, dec_seq, _ = x_sample.shape
    depth = w_mod.shape[0]
    n_kv = cache_k.shape[3]
    n_heads = w_attn_out.shape[1] // HEAD_DIM
    heads = dict(n_heads=n_heads, n_kv=n_kv)

    cond = jnp.zeros((SUBLANES, d), F32).at[0].set(c_ctx).at[1:1 + dec_batch].set(c)
    mod_all = _mod_vectors(cond, w_mod, b_mod)

    xs = [x_prompt.reshape(batch * seq, d), x_sample.reshape(dec_batch * dec_seq, d)]
    lens = [seq, dec_seq]
    mod_rows = [lambda i, rows: 0, lambda i, rows: 1 + (i * rows) // dec_seq]
    prevs = [None, None]

    fg = final_gain.reshape(1, d)
    wr_hi, wr_lo = _split_bf16(jnp.swapaxes(w_router, 1, 2))
    new_lru, new_k, new_v = [], [], []
    for l in range(depth):
        mod = mod_all[l]
        gain_a = norm_gain[l, 0].reshape(1, d)
        gain_c = norm_gain[l, 1].reshape(1, d)
        mixed = []
        if l % 2 == 0:
            li = l // 2
            w_in = w_lru_in[li].astype(BF16)
            wg = (0.5 * jnp.concatenate([lru_wa[li, 0], lru_wx[li, 0], lru_wa[li, 1], lru_wx[li, 1]],
                                        axis=2)).astype(BF16)
            bias4 = 0.5 * jnp.stack([lru_ba[li, 0], lru_bx[li, 0], lru_ba[li, 1], lru_bx[li, 1]])
            h0s = [jnp.zeros((batch, 2, w_in.shape[1] // 2), F32), state_lru[:, li]]
            for gi in range(2):
                xs[gi], gate, xr = _lru_in(xs[gi], prevs[gi], mod, gain_a, w_in, mod_rows[gi])
                y, fin = _lru_core(xr, gate, lru_conv_w[li], lru_conv_b[li].reshape(1, -1), wg, bias4,
                                   lru_lambda[li], h0s[gi], seq_len=lens[gi])
                mixed.append(y)
                if gi == 0:
                    new_lru.append(fin.astype(x_prompt.dtype))
            w_out = w_lru_out[li].astype(BF16)
        else:
            ai = l // 2
            w = w_qkv[ai].astype(BF16)
            qg = q_norm[ai].reshape(1, -1)
            kg = k_norm[ai].reshape(1, -1)
            xs[0], q_p, k_p, v_p, kf, vf = _qkv(xs[0], prevs[0], mod, gain_a, w, qg, kg, None,
                                                seq_len=seq, mod_row=mod_rows[0], **heads)
            n_qk = (n_heads + n_kv) * HEAD_DIM
            w_qk = _rope_head_order(w_qkv[ai][:, :n_qk].reshape(d, n_heads + n_kv, HEAD_DIM)).reshape(d, n_qk)
            w_r = jnp.concatenate([w_qk, w_qkv[ai][:, n_qk:]], axis=1).astype(BF16)
            xs[1], q_s, k_s, v_s = _qkv(xs[1], prevs[1], mod, gain_a, w_r, _rope_head_order(qg),
                                        _rope_head_order(kg), _rope_tables(dec_seq),
                                        seq_len=dec_seq, mod_row=mod_rows[1], **heads)
            new_k.append(kf.reshape(batch, seq, n_kv, HEAD_DIM))
            new_v.append(vf.reshape(batch, seq, n_kv, HEAD_DIM))
            past = cache_k.shape[2]
            kc = _rope_head_order(cache_k[:, ai]).reshape(dec_batch, past, n_kv * HEAD_DIM).astype(BF16)
            vc = cache_v[:, ai].reshape(dec_batch, past, n_kv * HEAD_DIM).astype(BF16)
            mixed.append(_attention(q_p, k_p, v_p, None, seq_len=seq, qb=seq, chunk=seq, **heads))
            mixed.append(_attention(q_s, k_s, v_s, (kc, vc), seq_len=dec_seq, qb=512, chunk=256, **heads))
            w_out = w_attn_out[ai].astype(BF16)

        x1_p, h2_p, lg_p = _mix_out(mixed[0], xs[0], mod, gain_c, w_out, wr_hi, wr_lo, l, mod_rows[0], False)
        x1_s, h2_s, lg_s = _mix_out(mixed[1], xs[1], mod, gain_c, w_out, wr_hi, wr_lo, l, mod_rows[1], True)
        idx_p, g_p, idx_s, g_s = _route(lg_p, lg_s, seq, dec_seq)
        xe_p = _gather_onehot(idx_p[:, None, :], h2_p, seq_len=seq)
        off_s = idx_s * _tiles_per_row(d)
        xe_s = _gather_rows(off_s, h2_s, d, seq_len=dec_seq)
        e_n = N_EXPERTS
        zero_rows = ((0, 0), (0, SUBLANES - 1), (0, 0))
        gc_p = jnp.pad(g_p.reshape(batch, e_n, -1).transpose(1, 0, 2).reshape(e_n, 1, -1), zero_rows)
        gc_s = jnp.pad(g_s.reshape(dec_batch, e_n, -1).transpose(1, 0, 2).reshape(e_n, 1, -1), zero_rows)
        ye_p, ye_s = _ffn(xe_p, xe_s, gc_p, gc_s, w_exp_gate, w_exp_up, w_exp_down, l)
        last = l == depth - 1
        x_p = _combine_onehot(idx_p[:, None, :], ye_p, x1_p, mod, fg if last else None, seq_len=seq,
                              mod_row=mod_rows[0])
        acc_s = _combine_rows(off_s, ye_s, seq_len=dec_seq)
        xs = [x_p, x1_s]
        prevs = [None, (acc_s, mod)]

    y_prompt = xs[0].reshape(batch, seq, d)
    y_sample = _final_norm(xs[1], *prevs[1], fg, mod_rows[1]).reshape(dec_batch, dec_seq, d)
    return (y_prompt, y_sample, jnp.stack(new_lru, axis=1), jnp.stack(new_k, axis=1), jnp.stack(new_v, axis=1))
```
